```python
import math
import jax, jax.numpy as jnp
from jax import lax
import numpy as np

D_MODEL = 1024
BATCH = 8
SEQ = 2048
DEPTH = 2

CHUNK = 64
Q_BLOCK = 128
EPS = 1e-6

MLA_HEADS = 8
MLA_NOPE = 64
MLA_ROPE = 32
MLA_V = 64
MLA_Q_RANK = 256
MLA_KV_RANK = 128
MLA_WIDTH = MLA_HEADS * MLA_V
MLA_QK_DIM = MLA_NOPE + MLA_ROPE
ROPE_THETA = 10000.0

FOX_HEADS = 8
FOX_HEAD_DIM = 64
FOX_WIDTH = FOX_HEADS * FOX_HEAD_DIM

S5_WIDTH = 512
S5_GROUP = 16
S5_GROUPS = S5_WIDTH // S5_GROUP
S5_STATE = 64
DT_MIN = 1e-3
DT_MAX = 1e-1

BRANCH_WIDTH = MLA_WIDTH + FOX_WIDTH + S5_WIDTH
IN_SPLITS = (MLA_Q_RANK, MLA_KV_RANK, MLA_ROPE,
             FOX_WIDTH, FOX_WIDTH, FOX_WIDTH, FOX_HEADS,
             S5_WIDTH,
             MLA_WIDTH, FOX_WIDTH, S5_WIDTH,
             D_MODEL, D_MODEL, D_MODEL)
IN_WIDTH = MLA_Q_RANK + MLA_KV_RANK + MLA_ROPE + 3 * FOX_WIDTH + FOX_HEADS + S5_WIDTH + BRANCH_WIDTH + 3 * D_MODEL

kernel_name = "hybrid_mla_fox_s5_gated_trunk"


def rmsnorm(x, g):
    xf = x.astype(jnp.float32)
    y = xf * lax.rsqrt(jnp.mean(xf * xf, axis=-1, keepdims=True) + EPS)
    return (y * g.astype(jnp.float32)).astype(x.dtype)


def rope_tables(positions, dtype):
    inv = 1.0 / (ROPE_THETA ** (jnp.arange(0, MLA_ROPE, 2, dtype=jnp.float32) / MLA_ROPE))
    ang = positions.astype(jnp.float32)[..., None] * inv
    return jnp.cos(ang).astype(dtype)[:, :, None, :], jnp.sin(ang).astype(dtype)[:, :, None, :]


def apply_rope(x, cos, sin):
    x1, x2 = jnp.split(x, 2, axis=-1)
    return jnp.concatenate([x1 * cos - x2 * sin, x1 * sin + x2 * cos], axis=-1)


def swept_attention(q, k, v, scale, chunk_causal, cum_logf=None):
    S = q.shape[1]
    outs = []
    for i in range(S // Q_BLOCK):
        q0, q1 = i * Q_BLOCK, (i + 1) * Q_BLOCK
        kv_end = q1
        logits = jnp.einsum('bqhd,bkhd->bhqk', q[:, q0:q1], k[:, :kv_end]).astype(jnp.float32) * scale
        qpos = jnp.arange(q0, q1)[:, None]
        kpos = jnp.arange(kv_end)[None, :]
        if chunk_causal:
            allowed = (kpos // CHUNK) <= (qpos // CHUNK)
        else:
            allowed = kpos <= qpos
        if cum_logf is not None:
            c_q = jnp.transpose(cum_logf[:, q0:q1], (0, 2, 1))[..., :, None]
            c_k = jnp.transpose(cum_logf[:, :kv_end], (0, 2, 1))[..., None, :]
            logits = logits + (c_q - c_k)
        logits = jnp.where(allowed, logits, -jnp.inf)
        p = jax.nn.softmax(logits, axis=-1).astype(v.dtype)
        outs.append(jnp.einsum('bhqk,bkhd->bqhd', p, v[:, :kv_end]))
    return jnp.concatenate(outs, axis=1)


def mla_branch(cq, ckv, kpe, cos, sin, q_a_norm, w_q_up, kv_a_norm, w_kv_up, q_norm, k_norm):
    Bsz, S, _ = cq.shape
    q = (rmsnorm(cq, q_a_norm) @ w_q_up).reshape(Bsz, S, MLA_HEADS, MLA_QK_DIM)
    q_nope, q_pe = jnp.split(q, [MLA_NOPE], axis=-1)
    kv = (rmsnorm(ckv, kv_a_norm) @ w_kv_up).reshape(Bsz, S, MLA_HEADS, MLA_NOPE + MLA_V)
    k_nope, v = jnp.split(kv, [MLA_NOPE], axis=-1)
    k_pe = apply_rope(kpe[:, :, None, :], cos, sin)
    q = jnp.concatenate([q_nope, apply_rope(q_pe, cos, sin)], axis=-1)
    k = jnp.concatenate([k_nope, jnp.broadcast_to(k_pe, (Bsz, S, MLA_HEADS, MLA_ROPE))], axis=-1)
    q = rmsnorm(q, q_norm)
    k = rmsnorm(k, k_norm)
    y = swept_attention(q, k, v, 1.0 / math.sqrt(MLA_QK_DIM), chunk_causal=True)
    return y.reshape(Bsz, S, MLA_WIDTH)


def fox_branch(fq, fk, fv, ff, b_f, q_norm, k_norm):
    Bsz, S, _ = fq.shape
    q = rmsnorm(fq.reshape(Bsz, S, FOX_HEADS, FOX_HEAD_DIM), q_norm)
    k = rmsnorm(fk.reshape(Bsz, S, FOX_HEADS, FOX_HEAD_DIM), k_norm)
    v = fv.reshape(Bsz, S, FOX_HEADS, FOX_HEAD_DIM)
    log_f = jax.nn.log_sigmoid(ff.astype(jnp.float32) + b_f.astype(jnp.float32))
    cum = jnp.cumsum(log_f, axis=1)
    y = swept_attention(q, k, v, 1.0 / math.sqrt(FOX_HEAD_DIM), chunk_causal=False, cum_logf=cum)
    return y.reshape(Bsz, S, FOX_WIDTH)


def _ssm_combine(e1, e2):
    a1r, a1i, b1r, b1i = e1
    a2r, a2i, b2r, b2i = e2
    return (a2r * a1r - a2i * a1i,
            a2r * a1i + a2i * a1r,
            a2r * b1r - a2i * b1i + b2r,
            a2r * b1i + a2i * b1r + b2i)


def s5_branch(u, lam_re, lam_im, log_dt, b_re, b_im, c_re, c_im, d_skip, w_glu, b_glu):
    Bsz, S, _ = u.shape
    uf = u.astype(jnp.float32).reshape(Bsz, S, S5_GROUPS, S5_GROUP)
    dt = jnp.exp(log_dt.astype(jnp.float32))[:, None]
    lr = lam_re.astype(jnp.float32)
    li = lam_im.astype(jnp.float32)
    mag = jnp.exp(lr * dt)
    a_re = mag * jnp.cos(li * dt)
    a_im = mag * jnp.sin(li * dt)
    den = lr * lr + li * li
    f_re = ((a_re - 1.0) * lr + a_im * li) / den
    f_im = (a_im * lr - (a_re - 1.0) * li) / den
    br = b_re.astype(jnp.float32)
    bi = b_im.astype(jnp.float32)
    bb_re = f_re[..., None] * br - f_im[..., None] * bi
    bb_im = f_re[..., None] * bi + f_im[..., None] * br
    bu_re = jnp.einsum('bsgc,gnc->bsgn', uf, bb_re)
    bu_im = jnp.einsum('bsgc,gnc->bsgn', uf, bb_im)
    a_re_full = jnp.broadcast_to(a_re, bu_re.shape)
    a_im_full = jnp.broadcast_to(a_im, bu_re.shape)
    _, _, x_re, x_im = lax.associative_scan(_ssm_combine, (a_re_full, a_im_full, bu_re, bu_im), axis=1)
    y = (jnp.einsum('bsgn,gcn->bsgc', x_re, c_re.astype(jnp.float32))
         - jnp.einsum('bsgn,gcn->bsgc', x_im, c_im.astype(jnp.float32)))
    y = (y + d_skip.astype(jnp.float32).reshape(S5_GROUPS, S5_GROUP) * uf).reshape(Bsz, S, S5_WIDTH)
    z = jax.nn.gelu(y)
    z = z * jax.nn.sigmoid(z @ w_glu.astype(jnp.float32) + b_glu.astype(jnp.float32))
    return z.astype(u.dtype)


def hybrid_layer(x, cos, sin, norm_g, w_in,
                 mla_q_a_norm, mla_w_q_up, mla_kv_a_norm, mla_w_kv_up, mla_q_norm, mla_k_norm,
                 fox_b_f, fox_q_norm, fox_k_norm,
                 s5_lambda_re, s5_lambda_im, s5_log_dt, s5_b_re, s5_b_im, s5_c_re, s5_c_im,
                 s5_d, s5_w_glu, s5_b_glu, w_branch_out, w_out):
    h = rmsnorm(x, norm_g)
    proj = h @ w_in
    offsets = np.cumsum(IN_SPLITS)[:-1].tolist()
    (cq, ckv, kpe, fq, fk, fv, ff, s5u,
     g_mla, g_fox, g_s5, m_mla, m_fox, m_s5) = jnp.split(proj, offsets, axis=-1)

    y_mla = mla_branch(cq, ckv, kpe, cos, sin, mla_q_a_norm, mla_w_q_up,
                       mla_kv_a_norm, mla_w_kv_up, mla_q_norm, mla_k_norm)
    y_fox = fox_branch(fq, fk, fv, ff, fox_b_f, fox_q_norm, fox_k_norm)
    y_s5 = s5_branch(s5u, s5_lambda_re, s5_lambda_im, s5_log_dt, s5_b_re, s5_b_im,
                     s5_c_re, s5_c_im, s5_d, s5_w_glu, s5_b_glu)

    wo_mla, wo_fox, wo_s5 = jnp.split(w_branch_out, [MLA_WIDTH, MLA_WIDTH + FOX_WIDTH], axis=0)
    o_mla = (y_mla * jax.nn.silu(g_mla)) @ wo_mla
    o_fox = (y_fox * jax.nn.silu(g_fox)) @ wo_fox
    o_s5 = (y_s5 * jax.nn.silu(g_s5)) @ wo_s5
    merged = jax.nn.sigmoid(m_mla) * o_mla + jax.nn.sigmoid(m_fox) * o_fox + jax.nn.sigmoid(m_s5) * o_s5
    return x + merged @ w_out


def _fwd_setup_inputs(seed: int = 0) -> dict:
    key = jax.random.key(seed)
    ks = jax.random.split(key, 32)
    f32 = jnp.float32

    def nrm(k, shape, scale):
        return jax.random.normal(k, shape, f32) * scale

    def gain(k, shape):
        return 1.0 + 0.02 * jax.random.normal(k, shape, f32)

    L = DEPTH
    x = jax.random.normal(ks[0], (BATCH, SEQ, D_MODEL), f32)
    start = jax.random.randint(ks[1], (BATCH, 1), 0, 4096, dtype=jnp.int32)
    positions = start + jnp.arange(SEQ, dtype=jnp.int32)[None, :]
    n_idx = jnp.arange(S5_STATE, dtype=f32)
    log_dt = jax.random.uniform(ks[17], (L, S5_GROUPS), f32, math.log(DT_MIN), math.log(DT_MAX))
    return {
        "x": x,
        "positions": positions,
        "norm_g": gain(ks[2], (L, D_MODEL)),
        "w_in": nrm(ks[3], (L, D_MODEL, IN_WIDTH), D_MODEL ** -0.5),
        "mla_q_a_norm": gain(ks[4], (L, MLA_Q_RANK)),
        "mla_w_q_up": nrm(ks[5], (L, MLA_Q_RANK, MLA_HEADS * MLA_QK_DIM), MLA_Q_RANK ** -0.5),
        "mla_kv_a_norm": gain(ks[6], (L, MLA_KV_RANK)),
        "mla_w_kv_up": nrm(ks[7], (L, MLA_KV_RANK, MLA_HEADS * (MLA_NOPE + MLA_V)), MLA_KV_RANK ** -0.5),
        "mla_q_norm": gain(ks[8], (L, MLA_QK_DIM)),
        "mla_k_norm": gain(ks[9], (L, MLA_QK_DIM)),
        "fox_b_f": 1.0 + 0.5 * jax.random.normal(ks[10], (L, FOX_HEADS), f32),
        "fox_q_norm": gain(ks[11], (L, FOX_HEAD_DIM)),
        "fox_k_norm": gain(ks[12], (L, FOX_HEAD_DIM)),
        "s5_lambda_re": -0.5 * (1.0 + 0.02 * jax.random.normal(ks[13], (L, S5_GROUPS, S5_STATE), f32)),
        "s5_lambda_im": math.pi * n_idx + 0.01 * jax.random.normal(ks[14], (L, S5_GROUPS, S5_STATE), f32),
        "s5_log_dt": log_dt,
        "s5_b_re": nrm(ks[15], (L, S5_GROUPS, S5_STATE, S5_GROUP), (2.0 * S5_GROUP) ** -0.5),
        "s5_b_im": nrm(ks[16], (L, S5_GROUPS, S5_STATE, S5_GROUP), (2.0 * S5_GROUP) ** -0.5),
        "s5_c_re": nrm(ks[18], (L, S5_GROUPS, S5_GROUP, S5_STATE), (2.0 * S5_STATE) ** -0.5),
        "s5_c_im": nrm(ks[19], (L, S5_GROUPS, S5_GROUP, S5_STATE), (2.0 * S5_STATE) ** -0.5),
        "s5_d": nrm(ks[20], (L, S5_WIDTH), 1.0),
        "s5_w_glu": nrm(ks[21], (L, S5_WIDTH, S5_WIDTH), S5_WIDTH ** -0.5),
        "s5_b_glu": nrm(ks[22], (L, S5_WIDTH), 0.02),
        "w_branch_out": nrm(ks[23], (L, BRANCH_WIDTH, D_MODEL), MLA_WIDTH ** -0.5),
        "w_out": nrm(ks[24], (L, D_MODEL, D_MODEL), D_MODEL ** -0.5),
    }


def _fwd_reference(x, positions, norm_g, w_in,
              mla_q_a_norm, mla_w_q_up, mla_kv_a_norm, mla_w_kv_up, mla_q_norm, mla_k_norm,
              fox_b_f, fox_q_norm, fox_k_norm,
              s5_lambda_re, s5_lambda_im, s5_log_dt, s5_b_re, s5_b_im, s5_c_re, s5_c_im,
              s5_d, s5_w_glu, s5_b_glu, w_branch_out, w_out):
    cos, sin = rope_tables(positions, x.dtype)
    h = x
    for l in range(DEPTH):
        h = hybrid_layer(h, cos, sin, norm_g[l], w_in[l],
                         mla_q_a_norm[l], mla_w_q_up[l], mla_kv_a_norm[l], mla_w_kv_up[l],
                         mla_q_norm[l], mla_k_norm[l],
                         fox_b_f[l], fox_q_norm[l], fox_k_norm[l],
                         s5_lambda_re[l], s5_lambda_im[l], s5_log_dt[l], s5_b_re[l], s5_b_im[l],
                         s5_c_re[l], s5_c_im[l], s5_d[l], s5_w_glu[l], s5_b_glu[l],
                         w_branch_out[l], w_out[l])
    return h


import jax as _jax
import jax.numpy as _jnp

TWIN_FORMAT = 'train_step'
FWD_PARAMS = ['x', 'positions', 'norm_g', 'w_in', 'mla_q_a_norm', 'mla_w_q_up', 'mla_kv_a_norm', 'mla_w_kv_up', 'mla_q_norm', 'mla_k_norm', 'fox_b_f', 'fox_q_norm', 'fox_k_norm', 's5_lambda_re', 's5_lambda_im', 's5_log_dt', 's5_b_re', 's5_b_im', 's5_c_re', 's5_c_im', 's5_d', 's5_w_glu', 's5_b_glu', 'w_branch_out', 'w_out']
TWIN_WEIGHTS = ['norm_g', 'w_in', 'mla_q_a_norm', 'mla_w_q_up', 'mla_kv_a_norm', 'mla_w_kv_up', 'mla_q_norm', 'mla_k_norm', 'fox_b_f', 'fox_q_norm', 'fox_k_norm', 's5_lambda_re', 's5_lambda_im', 's5_log_dt', 's5_b_re', 's5_b_im', 's5_c_re', 's5_c_im', 's5_d', 's5_w_glu', 's5_b_glu', 'w_branch_out', 'w_out']
TWIN_DIFF_INPUT = 'x'
TWIN_INPUTS = ['x', 'positions', 'norm_g', 'w_in', 'mla_q_a_norm', 'mla_w_q_up', 'mla_kv_a_norm', 'mla_w_kv_up', 'mla_q_norm', 'mla_k_norm', 'fox_b_f', 'fox_q_norm', 'fox_k_norm', 's5_lambda_re', 's5_lambda_im', 's5_log_dt', 's5_b_re', 's5_b_im', 's5_c_re', 's5_c_im', 's5_d', 's5_w_glu', 's5_b_glu', 'w_branch_out', 'w_out', 'loss_target', 'm_norm_g', 'm_w_in', 'm_mla_q_a_norm', 'm_mla_w_q_up', 'm_mla_kv_a_norm', 'm_mla_w_kv_up', 'm_mla_q_norm', 'm_mla_k_norm', 'm_fox_b_f', 'm_fox_q_norm', 'm_fox_k_norm', 'm_s5_lambda_re', 'm_s5_lambda_im', 'm_s5_log_dt', 'm_s5_b_re', 'm_s5_b_im', 'm_s5_c_re', 'm_s5_c_im', 'm_s5_d', 'm_s5_w_glu', 'm_s5_b_glu', 'm_w_branch_out', 'm_w_out', 'v_norm_g', 'v_w_in', 'v_mla_q_a_norm', 'v_mla_w_q_up', 'v_mla_kv_a_norm', 'v_mla_w_kv_up', 'v_mla_q_norm', 'v_mla_k_norm', 'v_fox_b_f', 'v_fox_q_norm', 'v_fox_k_norm', 'v_s5_lambda_re', 'v_s5_lambda_im', 'v_s5_log_dt', 'v_s5_b_re', 'v_s5_b_im', 'v_s5_c_re', 'v_s5_c_im', 'v_s5_d', 'v_s5_w_glu', 'v_s5_b_glu', 'v_w_branch_out', 'v_w_out']
TWIN_OUTPUTS = ['loss', 'grad_x', 'grad_norm_g', 'grad_w_in', 'grad_mla_q_a_norm', 'grad_mla_w_q_up', 'grad_mla_kv_a_norm', 'grad_mla_w_kv_up', 'grad_mla_q_norm', 'grad_mla_k_norm', 'grad_fox_b_f', 'grad_fox_q_norm', 'grad_fox_k_norm', 'grad_s5_lambda_re', 'grad_s5_lambda_im', 'grad_s5_log_dt', 'grad_s5_b_re', 'grad_s5_b_im', 'grad_s5_c_re', 'grad_s5_c_im', 'grad_s5_d', 'grad_s5_w_glu', 'grad_s5_b_glu', 'grad_w_branch_out', 'grad_w_out', 'delta_norm_g', 'delta_w_in', 'delta_mla_q_a_norm', 'delta_mla_w_q_up', 'delta_mla_kv_a_norm', 'delta_mla_w_kv_up', 'delta_mla_q_norm', 'delta_mla_k_norm', 'delta_fox_b_f', 'delta_fox_q_norm', 'delta_fox_k_norm', 'delta_s5_lambda_re', 'delta_s5_lambda_im', 'delta_s5_log_dt', 'delta_s5_b_re', 'delta_s5_b_im', 'delta_s5_c_re', 'delta_s5_c_im', 'delta_s5_d', 'delta_s5_w_glu', 'delta_s5_b_glu', 'delta_w_branch_out', 'delta_w_out', 'new_m_norm_g', 'new_m_w_in', 'new_m_mla_q_a_norm', 'new_m_mla_w_q_up', 'new_m_mla_kv_a_norm', 'new_m_mla_w_kv_up', 'new_m_mla_q_norm', 'new_m_mla_k_norm', 'new_m_fox_b_f', 'new_m_fox_q_norm', 'new_m_fox_k_norm', 'new_m_s5_lambda_re', 'new_m_s5_lambda_im', 'new_m_s5_log_dt', 'new_m_s5_b_re', 'new_m_s5_b_im', 'new_m_s5_c_re', 'new_m_s5_c_im', 'new_m_s5_d', 'new_m_s5_w_glu', 'new_m_s5_b_glu', 'new_m_w_branch_out', 'new_m_w_out', 'new_v_norm_g', 'new_v_w_in', 'new_v_mla_q_a_norm', 'new_v_mla_w_q_up', 'new_v_mla_kv_a_norm', 'new_v_mla_w_kv_up', 'new_v_mla_q_norm', 'new_v_mla_k_norm', 'new_v_fox_b_f', 'new_v_fox_q_norm', 'new_v_fox_k_norm', 'new_v_s5_lambda_re', 'new_v_s5_lambda_im', 'new_v_s5_log_dt', 'new_v_s5_b_re', 'new_v_s5_b_im', 'new_v_s5_c_re', 'new_v_s5_c_im', 'new_v_s5_d', 'new_v_s5_w_glu', 'new_v_s5_b_glu', 'new_v_w_branch_out', 'new_v_w_out']
TWIN_LEAF_KINDS = {'loss': 'loss', 'grad_x': 'grad_x', 'grad_norm_g': 'grad_w', 'grad_w_in': 'grad_w', 'grad_mla_q_a_norm': 'grad_w', 'grad_mla_w_q_up': 'grad_w', 'grad_mla_kv_a_norm': 'grad_w', 'grad_mla_w_kv_up': 'grad_w', 'grad_mla_q_norm': 'grad_w', 'grad_mla_k_norm': 'grad_w', 'grad_fox_b_f': 'grad_w', 'grad_fox_q_norm': 'grad_w', 'grad_fox_k_norm': 'grad_w', 'grad_s5_lambda_re': 'grad_w', 'grad_s5_lambda_im': 'grad_w', 'grad_s5_log_dt': 'grad_w', 'grad_s5_b_re': 'grad_w', 'grad_s5_b_im': 'grad_w', 'grad_s5_c_re': 'grad_w', 'grad_s5_c_im': 'grad_w', 'grad_s5_d': 'grad_w', 'grad_s5_w_glu': 'grad_w', 'grad_s5_b_glu': 'grad_w', 'grad_w_branch_out': 'grad_w', 'grad_w_out': 'grad_w', 'delta_norm_g': 'delta_w', 'delta_w_in': 'delta_w', 'delta_mla_q_a_norm': 'delta_w', 'delta_mla_w_q_up': 'delta_w', 'delta_mla_kv_a_norm': 'delta_w', 'delta_mla_w_kv_up': 'delta_w', 'delta_mla_q_norm': 'delta_w', 'delta_mla_k_norm': 'delta_w', 'delta_fox_b_f': 'delta_w', 'delta_fox_q_norm': 'delta_w', 'delta_fox_k_norm': 'delta_w', 'delta_s5_lambda_re': 'delta_w', 'delta_s5_lambda_im': 'delta_w', 'delta_s5_log_dt': 'delta_w', 'delta_s5_b_re': 'delta_w', 'delta_s5_b_im': 'delta_w', 'delta_s5_c_re': 'delta_w', 'delta_s5_c_im': 'delta_w', 'delta_s5_d': 'delta_w', 'delta_s5_w_glu': 'delta_w', 'delta_s5_b_glu': 'delta_w', 'delta_w_branch_out': 'delta_w', 'delta_w_out': 'delta_w', 'new_m_norm_g': 'new_m', 'new_m_w_in': 'new_m', 'new_m_mla_q_a_norm': 'new_m', 'new_m_mla_w_q_up': 'new_m', 'new_m_mla_kv_a_norm': 'new_m', 'new_m_mla_w_kv_up': 'new_m', 'new_m_mla_q_norm': 'new_m', 'new_m_mla_k_norm': 'new_m', 'new_m_fox_b_f': 'new_m', 'new_m_fox_q_norm': 'new_m', 'new_m_fox_k_norm': 'new_m', 'new_m_s5_lambda_re': 'new_m', 'new_m_s5_lambda_im': 'new_m', 'new_m_s5_log_dt': 'new_m', 'new_m_s5_b_re': 'new_m', 'new_m_s5_b_im': 'new_m', 'new_m_s5_c_re': 'new_m', 'new_m_s5_c_im': 'new_m', 'new_m_s5_d': 'new_m', 'new_m_s5_w_glu': 'new_m', 'new_m_s5_b_glu': 'new_m', 'new_m_w_branch_out': 'new_m', 'new_m_w_out': 'new_m', 'new_v_norm_g': 'new_v', 'new_v_w_in': 'new_v', 'new_v_mla_q_a_norm': 'new_v', 'new_v_mla_w_q_up': 'new_v', 'new_v_mla_kv_a_norm': 'new_v', 'new_v_mla_w_kv_up': 'new_v', 'new_v_mla_q_norm': 'new_v', 'new_v_mla_k_norm': 'new_v', 'new_v_fox_b_f': 'new_v', 'new_v_fox_q_norm': 'new_v', 'new_v_fox_k_norm': 'new_v', 'new_v_s5_lambda_re': 'new_v', 'new_v_s5_lambda_im': 'new_v', 'new_v_s5_log_dt': 'new_v', 'new_v_s5_b_re': 'new_v', 'new_v_s5_b_im': 'new_v', 'new_v_s5_c_re': 'new_v', 'new_v_s5_c_im': 'new_v', 'new_v_s5_d': 'new_v', 'new_v_s5_w_glu': 'new_v', 'new_v_s5_b_glu': 'new_v', 'new_v_w_branch_out': 'new_v', 'new_v_w_out': 'new_v'}


def _forward(args):
    return _fwd_reference(*[args[k] for k in FWD_PARAMS])


def _output_shape():
    out = _jax.eval_shape(lambda: _forward(_fwd_setup_inputs(0)))
    return out.shape, out.dtype

N_MICROBATCH = 1
ADAM_LR = 0.001
ADAM_B1 = 0.9
ADAM_B2 = 0.999
ADAM_EPS = 1e-08
ADAM_WD = 0.01
ADAM_STEP = 10
PER_EXAMPLE_BATCH_AXIS = {'x': 0, 'positions': 0, 'loss_target': 0}
SHARED_INPUTS = []
_WEIGHT_DTYPES = {'norm_g': _jnp.float32, 'w_in': _jnp.float32, 'mla_q_a_norm': _jnp.float32, 'mla_w_q_up': _jnp.float32, 'mla_kv_a_norm': _jnp.float32, 'mla_w_kv_up': _jnp.float32, 'mla_q_norm': _jnp.float32, 'mla_k_norm': _jnp.float32, 'fox_b_f': _jnp.float32, 'fox_q_norm': _jnp.float32, 'fox_k_norm': _jnp.float32, 's5_lambda_re': _jnp.float32, 's5_lambda_im': _jnp.float32, 's5_log_dt': _jnp.float32, 's5_b_re': _jnp.float32, 's5_b_im': _jnp.float32, 's5_c_re': _jnp.float32, 's5_c_im': _jnp.float32, 's5_d': _jnp.float32, 's5_w_glu': _jnp.float32, 's5_b_glu': _jnp.float32, 'w_branch_out': _jnp.float32, 'w_out': _jnp.float32}
MOMENT_SCALE = {'norm_g': 1.581757e+00, 'w_in': 3.569430e-02, 'mla_q_a_norm': 1.903520e-02, 'mla_w_q_up': 1.136724e-02, 'mla_kv_a_norm': 1.225678e-01, 'mla_w_kv_up': 1.365945e-02, 'mla_q_norm': 1.344282e-01, 'mla_k_norm': 1.339718e-01, 'fox_b_f': 1.486458e+01, 'fox_q_norm': 1.727559e+00, 'fox_k_norm': 1.720855e+00, 's5_lambda_re': 1.868209e-03, 's5_lambda_im': 1.867337e-03, 's5_log_dt': 1.356594e+00, 's5_b_re': 1.332560e-03, 's5_b_im': 1.337195e-03, 's5_c_re': 2.676336e-03, 's5_c_im': 2.627203e-03, 's5_d': 4.800486e-01, 's5_w_glu': 9.398918e-02, 's5_b_glu': 2.874030e-01, 'w_branch_out': 3.002227e-02, 'w_out': 4.995096e-02}


def _to_microbatches(a, axis):
    t = _jnp.moveaxis(a, axis, 0)
    t = t.reshape((N_MICROBATCH, t.shape[0] // N_MICROBATCH) + t.shape[1:])
    return _jnp.moveaxis(t, 1, axis + 1)


def setup_inputs(seed: int = 0) -> dict:
    inp = _fwd_setup_inputs(seed)
    key = _jax.random.fold_in(_jax.random.key(seed), 7919)
    shape, _ = _output_shape()
    out = dict(inp)
    out["loss_target"] = _jax.random.normal(_jax.random.fold_in(key, 0), shape, _jnp.float32)
    for i, name in enumerate(TWIN_WEIGHTS):
        w = inp[name].astype(_jnp.float32)
        if MOMENT_SCALE is None:
            s = _jnp.sqrt(_jnp.mean(_jnp.square(w)) + 1e-30)
        else:
            s = MOMENT_SCALE[name]
        km, kv = _jax.random.split(_jax.random.fold_in(key, i + 1))
        out[name] = w
        out["m_" + name] = s * _jax.random.normal(km, w.shape, _jnp.float32)
        out["v_" + name] = (s * s) * _jax.random.uniform(kv, w.shape, _jnp.float32, 0.5, 1.5)
    if N_MICROBATCH > 1:
        for name, axis in PER_EXAMPLE_BATCH_AXIS.items():
            out[name] = _to_microbatches(out[name], axis)
    return {'x': out['x'], 'positions': out['positions'], 'norm_g': out['norm_g'], 'w_in': out['w_in'], 'mla_q_a_norm': out['mla_q_a_norm'], 'mla_w_q_up': out['mla_w_q_up'], 'mla_kv_a_norm': out['mla_kv_a_norm'], 'mla_w_kv_up': out['mla_w_kv_up'], 'mla_q_norm': out['mla_q_norm'], 'mla_k_norm': out['mla_k_norm'], 'fox_b_f': out['fox_b_f'], 'fox_q_norm': out['fox_q_norm'], 'fox_k_norm': out['fox_k_norm'], 's5_lambda_re': out['s5_lambda_re'], 's5_lambda_im': out['s5_lambda_im'], 's5_log_dt': out['s5_log_dt'], 's5_b_re': out['s5_b_re'], 's5_b_im': out['s5_b_im'], 's5_c_re': out['s5_c_re'], 's5_c_im': out['s5_c_im'], 's5_d': out['s5_d'], 's5_w_glu': out['s5_w_glu'], 's5_b_glu': out['s5_b_glu'], 'w_branch_out': out['w_branch_out'], 'w_out': out['w_out'], 'loss_target': out['loss_target'], 'm_norm_g': out['m_norm_g'], 'm_w_in': out['m_w_in'], 'm_mla_q_a_norm': out['m_mla_q_a_norm'], 'm_mla_w_q_up': out['m_mla_w_q_up'], 'm_mla_kv_a_norm': out['m_mla_kv_a_norm'], 'm_mla_w_kv_up': out['m_mla_w_kv_up'], 'm_mla_q_norm': out['m_mla_q_norm'], 'm_mla_k_norm': out['m_mla_k_norm'], 'm_fox_b_f': out['m_fox_b_f'], 'm_fox_q_norm': out['m_fox_q_norm'], 'm_fox_k_norm': out['m_fox_k_norm'], 'm_s5_lambda_re': out['m_s5_lambda_re'], 'm_s5_lambda_im': out['m_s5_lambda_im'], 'm_s5_log_dt': out['m_s5_log_dt'], 'm_s5_b_re': out['m_s5_b_re'], 'm_s5_b_im': out['m_s5_b_im'], 'm_s5_c_re': out['m_s5_c_re'], 'm_s5_c_im': out['m_s5_c_im'], 'm_s5_d': out['m_s5_d'], 'm_s5_w_glu': out['m_s5_w_glu'], 'm_s5_b_glu': out['m_s5_b_glu'], 'm_w_branch_out': out['m_w_branch_out'], 'm_w_out': out['m_w_out'], 'v_norm_g': out['v_norm_g'], 'v_w_in': out['v_w_in'], 'v_mla_q_a_norm': out['v_mla_q_a_norm'], 'v_mla_w_q_up': out['v_mla_w_q_up'], 'v_mla_kv_a_norm': out['v_mla_kv_a_norm'], 'v_mla_w_kv_up': out['v_mla_w_kv_up'], 'v_mla_q_norm': out['v_mla_q_norm'], 'v_mla_k_norm': out['v_mla_k_norm'], 'v_fox_b_f': out['v_fox_b_f'], 'v_fox_q_norm': out['v_fox_q_norm'], 'v_fox_k_norm': out['v_fox_k_norm'], 'v_s5_lambda_re': out['v_s5_lambda_re'], 'v_s5_lambda_im': out['v_s5_lambda_im'], 'v_s5_log_dt': out['v_s5_log_dt'], 'v_s5_b_re': out['v_s5_b_re'], 'v_s5_b_im': out['v_s5_b_im'], 'v_s5_c_re': out['v_s5_c_re'], 'v_s5_c_im': out['v_s5_c_im'], 'v_s5_d': out['v_s5_d'], 'v_s5_w_glu': out['v_s5_w_glu'], 'v_s5_b_glu': out['v_s5_b_glu'], 'v_w_branch_out': out['v_w_branch_out'], 'v_w_out': out['v_w_out']}


def _loss(weights, diff, rest, loss_target):
    with _jax.named_scope("forward"):
        args = {**rest, TWIN_DIFF_INPUT: diff, **{k: w.astype(_WEIGHT_DTYPES[k]) for k, w in weights.items()}}
        y = _forward(args)
    with _jax.named_scope("loss_head"):
        err = _jnp.square(y.astype(_jnp.float32) - loss_target)
        return 0.5 * _jnp.sum(_jnp.mean(err, axis=-1)) if err.ndim else 0.5 * err


def _adamw(w, g, m, v):
    m = ADAM_B1 * m + (1.0 - ADAM_B1) * g
    v = ADAM_B2 * v + (1.0 - ADAM_B2) * _jnp.square(g)
    m_hat = m / (1.0 - ADAM_B1 ** ADAM_STEP)
    v_hat = v / (1.0 - ADAM_B2 ** ADAM_STEP)
    delta = -ADAM_LR * (m_hat / (_jnp.sqrt(v_hat) + ADAM_EPS) + ADAM_WD * w)
    return delta, m, v


def reference(x, positions, norm_g, w_in, mla_q_a_norm, mla_w_q_up, mla_kv_a_norm, mla_w_kv_up, mla_q_norm, mla_k_norm, fox_b_f, fox_q_norm, fox_k_norm, s5_lambda_re, s5_lambda_im, s5_log_dt, s5_b_re, s5_b_im, s5_c_re, s5_c_im, s5_d, s5_w_glu, s5_b_glu, w_branch_out, w_out, loss_target, m_norm_g, m_w_in, m_mla_q_a_norm, m_mla_w_q_up, m_mla_kv_a_norm, m_mla_w_kv_up, m_mla_q_norm, m_mla_k_norm, m_fox_b_f, m_fox_q_norm, m_fox_k_norm, m_s5_lambda_re, m_s5_lambda_im, m_s5_log_dt, m_s5_b_re, m_s5_b_im, m_s5_c_re, m_s5_c_im, m_s5_d, m_s5_w_glu, m_s5_b_glu, m_w_branch_out, m_w_out, v_norm_g, v_w_in, v_mla_q_a_norm, v_mla_w_q_up, v_mla_kv_a_norm, v_mla_w_kv_up, v_mla_q_norm, v_mla_k_norm, v_fox_b_f, v_fox_q_norm, v_fox_k_norm, v_s5_lambda_re, v_s5_lambda_im, v_s5_log_dt, v_s5_b_re, v_s5_b_im, v_s5_c_re, v_s5_c_im, v_s5_d, v_s5_w_glu, v_s5_b_glu, v_w_branch_out, v_w_out):
    given = dict(x=x, positions=positions, norm_g=norm_g, w_in=w_in, mla_q_a_norm=mla_q_a_norm, mla_w_q_up=mla_w_q_up, mla_kv_a_norm=mla_kv_a_norm, mla_w_kv_up=mla_w_kv_up, mla_q_norm=mla_q_norm, mla_k_norm=mla_k_norm, fox_b_f=fox_b_f, fox_q_norm=fox_q_norm, fox_k_norm=fox_k_norm, s5_lambda_re=s5_lambda_re, s5_lambda_im=s5_lambda_im, s5_log_dt=s5_log_dt, s5_b_re=s5_b_re, s5_b_im=s5_b_im, s5_c_re=s5_c_re, s5_c_im=s5_c_im, s5_d=s5_d, s5_w_glu=s5_w_glu, s5_b_glu=s5_b_glu, w_branch_out=w_branch_out, w_out=w_out, loss_target=loss_target, m_norm_g=m_norm_g, m_w_in=m_w_in, m_mla_q_a_norm=m_mla_q_a_norm, m_mla_w_q_up=m_mla_w_q_up, m_mla_kv_a_norm=m_mla_kv_a_norm, m_mla_w_kv_up=m_mla_w_kv_up, m_mla_q_norm=m_mla_q_norm, m_mla_k_norm=m_mla_k_norm, m_fox_b_f=m_fox_b_f, m_fox_q_norm=m_fox_q_norm, m_fox_k_norm=m_fox_k_norm, m_s5_lambda_re=m_s5_lambda_re, m_s5_lambda_im=m_s5_lambda_im, m_s5_log_dt=m_s5_log_dt, m_s5_b_re=m_s5_b_re, m_s5_b_im=m_s5_b_im, m_s5_c_re=m_s5_c_re, m_s5_c_im=m_s5_c_im, m_s5_d=m_s5_d, m_s5_w_glu=m_s5_w_glu, m_s5_b_glu=m_s5_b_glu, m_w_branch_out=m_w_branch_out, m_w_out=m_w_out, v_norm_g=v_norm_g, v_w_in=v_w_in, v_mla_q_a_norm=v_mla_q_a_norm, v_mla_w_q_up=v_mla_w_q_up, v_mla_kv_a_norm=v_mla_kv_a_norm, v_mla_w_kv_up=v_mla_w_kv_up, v_mla_q_norm=v_mla_q_norm, v_mla_k_norm=v_mla_k_norm, v_fox_b_f=v_fox_b_f, v_fox_q_norm=v_fox_q_norm, v_fox_k_norm=v_fox_k_norm, v_s5_lambda_re=v_s5_lambda_re, v_s5_lambda_im=v_s5_lambda_im, v_s5_log_dt=v_s5_log_dt, v_s5_b_re=v_s5_b_re, v_s5_b_im=v_s5_b_im, v_s5_c_re=v_s5_c_re, v_s5_c_im=v_s5_c_im, v_s5_d=v_s5_d, v_s5_w_glu=v_s5_w_glu, v_s5_b_glu=v_s5_b_glu, v_w_branch_out=v_w_branch_out, v_w_out=v_w_out)
    weights = {n: given[n] for n in TWIN_WEIGHTS}
    shared = {n: given[n] for n in SHARED_INPUTS}
    per_example = {n: given[n] for n in ['x', 'positions']}
    grad_fn = _jax.value_and_grad(_loss, argnums=(0, 1))

    def one_microbatch(ex, loss_target):
        ex = dict(ex)
        diff = ex.pop(TWIN_DIFF_INPUT)
        return grad_fn(weights, diff, {**shared, **ex}, loss_target)

    if N_MICROBATCH == 1:
        loss, (grad_w, grad_x) = one_microbatch(per_example, given["loss_target"])
    else:
        def body(carry, xs):
            loss_sum, grad_sum = carry
            l_k, (gw_k, gx_k) = one_microbatch(xs[0], xs[1])
            with _jax.named_scope("update"):
                return (loss_sum + l_k, _jax.tree.map(_jnp.add, grad_sum, gw_k)), gx_k

        init = (_jnp.zeros((), _jnp.float32), _jax.tree.map(_jnp.zeros_like, weights))
        (loss, grad_w), grad_x = _jax.lax.scan(body, init, (per_example, given["loss_target"]))
    with _jax.named_scope("update"):
        delta_w, new_m, new_v = {}, {}, {}
        for n in TWIN_WEIGHTS:
            delta_w[n], new_m[n], new_v[n] = _adamw(weights[n], grad_w[n], given["m_" + n], given["v_" + n])
    return (loss, grad_x, *[grad_w[n] for n in TWIN_WEIGHTS], *[delta_w[n] for n in TWIN_WEIGHTS],
            *[new_m[n] for n in TWIN_WEIGHTS], *[new_v[n] for n in TWIN_WEIGHTS])
```

```python
import functools
import math

import jax
import jax.numpy as jnp
import numpy as np
from jax import lax
from jax.experimental import pallas as pl
from jax.experimental.pallas import tpu as pltpu

F32 = jnp.float32
BF16 = jnp.bfloat16

D_MODEL = 1024
DEPTH = 2
CHUNK = 64
EPS = 1e-6
HEADS = 8
MLA_NOPE, MLA_ROPE, MLA_V = 64, 32, 64
MLA_Q_RANK, MLA_KV_RANK = 256, 128
MLA_QK = MLA_NOPE + MLA_ROPE
ROPE_THETA = 10000.0
FOX_DIM = 64
S5_WIDTH, S5_GROUP, S5_GROUPS, S5_STATE = 512, 16, 32, 64
S5_LANES = S5_GROUPS * S5_STATE
IN_WIDTH = 7080
N_DEV = 8
LANES = 128
SUBLANES = 8

ADAM_LR, ADAM_B1, ADAM_B2, ADAM_EPS, ADAM_WD, ADAM_STEP = 0.001, 0.9, 0.999, 1e-08, 0.01, 10

SEG_W = (512, 1536, 512, 1536, 3072)
SEG_OFF = (0, 512, 2048, 2560, 4096)
PAD_IN = 7168
NEG = -1e30

SHARDED = ("w_in", "mla_w_q_up", "mla_w_kv_up", "s5_w_glu", "w_branch_out", "w_out")
COL_SHARDED = ("w_in", "mla_w_q_up", "mla_w_kv_up")
SMALL = ("norm_g", "mla_q_a_norm", "mla_kv_a_norm", "mla_q_norm", "mla_k_norm", "fox_b_f", "fox_q_norm", "fox_k_norm",
         "s5_lambda_re", "s5_lambda_im", "s5_log_dt", "s5_b_re", "s5_b_im", "s5_c_re", "s5_c_im", "s5_d", "s5_b_glu")
WEIGHTS = ("norm_g", "w_in", "mla_q_a_norm", "mla_w_q_up", "mla_kv_a_norm", "mla_w_kv_up", "mla_q_norm", "mla_k_norm",
           "fox_b_f", "fox_q_norm", "fox_k_norm", "s5_lambda_re", "s5_lambda_im", "s5_log_dt", "s5_b_re", "s5_b_im",
           "s5_c_re", "s5_c_im", "s5_d", "s5_w_glu", "s5_b_glu", "w_branch_out", "w_out")
PACK_LANES = 512


def _pick(n, cands):
    for c in cands:
        if n % c == 0:
            return c
    return n


def _vmem(mb):
    return pltpu.CompilerParams(vmem_limit_bytes=mb * 1024 * 1024)


def _dot(a, b, dims):
    return lax.dot_general(a.astype(BF16), b.astype(BF16), (dims, ((), ())), preferred_element_type=F32)


def _nn(a, b):
    return _dot(a, b, ((1,), (0,)))


def _nt(a, b):
    return _dot(a, b, ((1,), (1,)))


def _tn(a, b):
    return _dot(a, b, ((0,), (0,)))


def _rms(x, g, n=None):
    n = x.shape[-1] if n is None else n
    return x * lax.rsqrt(jnp.sum(x * x, axis=-1, keepdims=True) / n + EPS) * g


def _rope(t, c, sa, sb):
    return t * c + pltpu.roll(t, LANES - 16, 1) * sa + pltpu.roll(t, 16, 1) * sb


def _rope_t(d, c, sa, sb):
    return d * c + pltpu.roll(d * sa, 16, 1) + pltpu.roll(d * sb, LANES - 16, 1)


def _mm(a, b, mode, name, acc=None, scale=None):
    if mode == "tn":
        kd, m = a.shape
    else:
        m, kd = a.shape
    n = b.shape[0] if mode == "nt" else b.shape[1]
    tm, tn, tk = _pick(m, (512, 256, 128)), _pick(n, (512, 256, 128)), _pick(kd, (512, 256, 128))
    nk = kd // tk
    if mode == "tn":
        a_spec = pl.BlockSpec((tk, tm), lambda i, j, k: (k, i))
    else:
        a_spec = pl.BlockSpec((tm, tk), lambda i, j, k: (i, k))
    if mode == "nt":
        b_spec = pl.BlockSpec((tn, tk), lambda i, j, k: (j, k))
    else:
        b_spec = pl.BlockSpec((tk, tn), lambda i, j, k: (k, j))
    dims = {"nn": ((1,), (0,)), "nt": ((1,), (1,)), "tn": ((0,), (0,))}[mode]
    o_spec = pl.BlockSpec((tm, tn), lambda i, j, k: (i, j))
    has_acc = acc is not None

    def body(*refs):
        if has_acc:
            a_ref, b_ref, c_ref, o_ref, acc_ref = refs
        else:
            a_ref, b_ref, o_ref, acc_ref = refs
        k = pl.program_id(2)

        @pl.when(k == 0)
        def _():
            acc_ref[...] = c_ref[...] if has_acc else jnp.zeros(acc_ref.shape, F32)

        acc_ref[...] += _dot(a_ref[...], b_ref[...], dims)

        @pl.when(k == nk - 1)
        def _():
            o_ref[...] = acc_ref[...]

    ins = [a, b] + ([acc] if has_acc else [])
    in_specs = [a_spec, b_spec] + ([o_spec] if has_acc else [])
    return pl.pallas_call(
        body, name=name, grid=(m // tm, n // tn, nk), in_specs=in_specs, out_specs=o_spec,
        out_shape=jax.ShapeDtypeStruct((m, n), F32), scratch_shapes=[pltpu.VMEM((tm, tn), F32)],
        compiler_params=pltpu.CompilerParams(dimension_semantics=("parallel", "parallel", "arbitrary"),
                                             vmem_limit_bytes=48 * 1024 * 1024),
    )(*ins)


def _stage(name, fn, n_steps, ins, outs, accs=(), scratch=(), vmem_mb=48):
    n_in, n_out, n_acc = len(ins), len(outs), len(accs)

    def body(*refs):
        in_refs = refs[:n_in]
        out_refs = refs[n_in:n_in + n_out]
        acc_refs = refs[n_in + n_out:n_in + n_out + n_acc]
        scr = refs[n_in + n_out + n_acc:]
        if n_acc:
            @pl.when(pl.program_id(0) == 0)
            def _():
                for r in acc_refs:
                    r[...] = jnp.zeros(r.shape, r.dtype)
        fn(in_refs, out_refs, acc_refs, scr)

    acc_specs = [pl.BlockSpec(a.shape, functools.partial(lambda i, nd: (0,) * nd, nd=len(a.shape))) for a in accs]
    res = pl.pallas_call(
        body, name=name, grid=(n_steps,),
        in_specs=[s for _, s in ins], out_specs=[s for _, s in outs] + acc_specs,
        out_shape=[s for s, _ in outs] + list(accs), scratch_shapes=list(scratch),
        compiler_params=pltpu.CompilerParams(dimension_semantics=("arbitrary",),
                                             vmem_limit_bytes=vmem_mb * 1024 * 1024),
    )(*[a for a, _ in ins])
    return res


def _rows(ts, w, j=0):
    return pl.BlockSpec((ts, w), lambda i: (i, j))


def _rows_rev(ts, w, n, j=0):
    return pl.BlockSpec((ts, w), lambda i: (n - 1 - i, j))


def _heads(ts, d):
    return pl.BlockSpec((HEADS, ts, d), lambda i: (0, i, 0))


def _heads_rev(ts, d, n):
    return pl.BlockSpec((HEADS, ts, d), lambda i: (0, n - 1 - i, 0))


def _full(shape):
    nd = len(shape)
    return pl.BlockSpec(tuple(shape), lambda i: (0,) * nd)


def _sds(shape, dtype=F32):
    return jax.ShapeDtypeStruct(tuple(shape), dtype)


def _norm_fwd(x, g, name):
    s = x.shape[0]
    ts = _pick(s, (256, 128))

    def fn(ins, outs, accs, scr):
        outs[0][...] = _rms(ins[0][...], ins[1][...]).astype(BF16)

    return _stage(name, fn, s // ts, [(x, _rows(ts, D_MODEL)), (g, _full(g.shape))],
                  [(_sds((s, D_MODEL), BF16), _rows(ts, D_MODEL))])[0]


def _norm_bwd(x, g, dh, dres, name):
    s = x.shape[0]
    ts = _pick(s, (256, 128))

    def fn(ins, outs, accs, scr):
        _, vjp = jax.vjp(_rms, ins[0][...], ins[1][...])
        dx, dg = vjp(ins[2][...])
        outs[0][...] = dx + ins[3][...]
        accs[0][...] += dg

    r = _stage(name, fn, s // ts,
               [(x, _rows(ts, D_MODEL)), (g, _full(g.shape)), (dh, _rows(ts, D_MODEL)), (dres, _rows(ts, D_MODEL))],
               [(_sds((s, D_MODEL)), _rows(ts, D_MODEL))], accs=[_sds((1, D_MODEL))])
    return r[0], r[1]


def _mla_q(qraw, c, sa, sb, qn):
    return _rms(_rope(qraw, c, sa, sb), qn, MLA_QK)


def _mla_prep_fwd(seg0, tabs, p, wq, wkv, name):
    s = seg0.shape[0]
    ts = _pick(s, (256, 128))

    def fn(ins, outs, accs, scr):
        blk, cos, sa, sb, qan, kvan, qn, kn, wq_r, wkv_r = ins
        b = blk[...]
        cq, ckv, kt = b[:, :256], b[:, 256:384], b[:, 384:512]
        lane = lax.broadcasted_iota(jnp.int32, kt.shape, 1)
        kpe = jnp.where(lane >= 64, kt, 0.0)
        q_raw = _nn(_rms(cq, qan[...]), wq_r[...])
        kv_raw = _nn(_rms(ckv, kvan[...]), wkv_r[...])
        c, a, bb = cos[...], sa[...], sb[...]
        for h in range(HEADS):
            outs[0][h] = _mla_q(q_raw[:, LANES * h:LANES * (h + 1)], c, a, bb, qn[...]).astype(BF16)
            outs[1][h] = _mla_q(kv_raw[:, LANES * h:LANES * (h + 1)] + kpe, c, a, bb, kn[...]).astype(BF16)
            outs[2][h] = kv_raw[:, 1024 + 64 * h:1024 + 64 * (h + 1)].astype(BF16)

    consts = [p["mla_q_a_norm"], p["mla_kv_a_norm"], p["mla_q_norm"], p["mla_k_norm"], wq, wkv]
    return _stage(name, fn, s // ts,
                  [(seg0, _rows(ts, 512))] + [(t, _rows(ts, LANES)) for t in tabs] + [(a, _full(a.shape)) for a in consts],
                  [(_sds((HEADS, s, LANES), BF16), _heads(ts, LANES)), (_sds((HEADS, s, LANES), BF16), _heads(ts, LANES)),
                   (_sds((HEADS, s, 64), BF16), _heads(ts, 64))])


def _mla_prep_bwd(seg0, tabs, p, wq, wkv, dq, dk, dv, dff, name):
    s = seg0.shape[0]
    ts = _pick(s, (256, 128))

    def fn(ins, outs, accs, scr):
        blk, cos, sa, sb, qan, kvan, qn, kn, wq_r, wkv_r, dq_r, dk_r, dv_r, dff_r = ins
        dqan, dkvan, dqn, dkn, dwq, dwkv = accs
        dqraw_s, dkvraw_s = scr
        b = blk[...]
        cq, ckv, kt = b[:, :256], b[:, 256:384], b[:, 384:512]
        lane = lax.broadcasted_iota(jnp.int32, kt.shape, 1)
        kpe = jnp.where(lane >= 64, kt, 0.0)
        cqn, vjp_cq = jax.vjp(_rms, cq, qan[...])
        ckvn, vjp_ckv = jax.vjp(_rms, ckv, kvan[...])
        q_raw = _nn(cqn, wq_r[...])
        kv_raw = _nn(ckvn, wkv_r[...])
        c, a, bb = cos[...], sa[...], sb[...]

        def head_bwd(raw, gain, d):
            t = _rope(raw, c, a, bb)
            _, vjp = jax.vjp(functools.partial(_rms, n=MLA_QK), t, gain)
            dt, dgain = vjp(d)
            return _rope_t(dt, c, a, bb), dgain

        dkpe = jnp.zeros(kt.shape, F32)
        for h in range(HEADS):
            dqh, dg = head_bwd(q_raw[:, LANES * h:LANES * (h + 1)], qn[...], dq_r[h])
            dqn[...] += dg
            dqraw_s[:, LANES * h:LANES * (h + 1)] = dqh
            dkh, dg = head_bwd(kv_raw[:, LANES * h:LANES * (h + 1)] + kpe, kn[...], dk_r[h])
            dkn[...] += dg
            dkvraw_s[:, LANES * h:LANES * (h + 1)] = dkh
            dkpe = dkpe + dkh
            dkvraw_s[:, 1024 + 64 * h:1024 + 64 * (h + 1)] = dv_r[h]
        dq_raw = dqraw_s[...]
        dkv_raw = dkvraw_s[...]
        dwq[...] += _tn(cqn, dq_raw)
        dwkv[...] += _tn(ckvn, dkv_raw)
        dcq, dg = vjp_cq(_nt(dq_raw, wq_r[...]))
        dqan[...] += dg
        dckv, dg = vjp_ckv(_nt(dkv_raw, wkv_r[...]))
        dkvan[...] += dg
        outs[0][:, 0:256] = dcq
        outs[0][:, 256:384] = dckv
        outs[0][:, 384:512] = jnp.where(lane >= 64, dkpe, 0.0) + dff_r[...]

    consts = [p["mla_q_a_norm"], p["mla_kv_a_norm"], p["mla_q_norm"], p["mla_k_norm"], wq, wkv]
    return _stage(name, fn, s // ts,
                  [(seg0, _rows(ts, 512))] + [(t, _rows(ts, LANES)) for t in tabs] + [(a, _full(a.shape)) for a in consts]
                  + [(dq, _heads(ts, LANES)), (dk, _heads(ts, LANES)), (dv, _heads(ts, 64)), (dff, _rows(ts, LANES))],
                  [(_sds((s, 512)), _rows(ts, 512))],
                  accs=[_sds((1, 256)), _sds((1, 128)), _sds((1, LANES)), _sds((1, LANES)), _sds(wq.shape), _sds(wkv.shape)],
                  scratch=[pltpu.VMEM((ts, 1024), F32), pltpu.VMEM((ts, 1536), F32)])


def _fox_prep_fwd(seg0, seg1, bf, qn, kn, name):
    s = seg0.shape[0]
    ts = _pick(s, (256, 128))
    steps = int(math.log2(ts))

    def fn(ins, outs, accs, scr):
        kt_r, x_r, bf_r, qn_r, kn_r = ins
        carry = scr[0]

        @pl.when(pl.program_id(0) == 0)
        def _():
            carry[...] = jnp.zeros(carry.shape, F32)

        x = x_r[...]
        for h in range(HEADS):
            outs[0][h] = _rms(x[:, 64 * h:64 * (h + 1)], qn_r[...]).astype(BF16)
            outs[1][h] = _rms(x[:, 512 + 64 * h:512 + 64 * (h + 1)], kn_r[...]).astype(BF16)
            outs[2][h] = x[:, 1024 + 64 * h:1024 + 64 * (h + 1)].astype(BF16)
        kt = kt_r[...]
        lane = lax.broadcasted_iota(jnp.int32, kt.shape, 1)
        row = lax.broadcasted_iota(jnp.int32, kt.shape, 0)
        cs = jnp.where(lane < HEADS, jax.nn.log_sigmoid(kt + bf_r[...]), 0.0)
        for k in range(steps):
            sh = 1 << k
            cs = cs + jnp.where(row >= sh, pltpu.roll(cs, sh, 0), 0.0)
        cs = cs + carry[0:1, :]
        outs[3][...] = cs
        outs[4][...] = cs.T[0:HEADS, :]
        carry[0:1, :] = cs[ts - 1:ts, :]

    return _stage(name, fn, s // ts,
                  [(seg0, _rows(ts, LANES, 3)), (seg1, _rows(ts, 1536)), (bf, _full(bf.shape)), (qn, _full(qn.shape)),
                   (kn, _full(kn.shape))],
                  [(_sds((HEADS, s, 64), BF16), _heads(ts, 64)), (_sds((HEADS, s, 64), BF16), _heads(ts, 64)),
                   (_sds((HEADS, s, 64), BF16), _heads(ts, 64)), (_sds((s, LANES)), _rows(ts, LANES)),
                   (_sds((HEADS, s)), pl.BlockSpec((HEADS, ts), lambda i: (0, i)))],
                  scratch=[pltpu.VMEM((SUBLANES, LANES), F32)])


def _fox_prep_bwd(seg0, seg1, bf, qn, kn, dq, dk, dv, dck, name):
    s = seg0.shape[0]
    ts = _pick(s, (256, 128))
    n = s // ts
    steps = int(math.log2(ts))

    def fn(ins, outs, accs, scr):
        kt_r, x_r, bf_r, qn_r, kn_r, dq_r, dk_r, dv_r, dck_r = ins
        dqn, dkn, dbf = accs
        carry = scr[0]

        @pl.when(pl.program_id(0) == 0)
        def _():
            carry[...] = jnp.zeros(carry.shape, F32)

        x = x_r[...]
        for h in range(HEADS):
            _, vjp = jax.vjp(_rms, x[:, 64 * h:64 * (h + 1)], qn_r[...])
            d, dg = vjp(dq_r[h])
            outs[0][:, 64 * h:64 * (h + 1)] = d
            dqn[...] += dg
            _, vjp = jax.vjp(_rms, x[:, 512 + 64 * h:512 + 64 * (h + 1)], kn_r[...])
            d, dg = vjp(dk_r[h])
            outs[0][:, 512 + 64 * h:512 + 64 * (h + 1)] = d
            dkn[...] += dg
            outs[0][:, 1024 + 64 * h:1024 + 64 * (h + 1)] = dv_r[h]
        dc = dck_r[...].reshape(HEADS, ts)
        dc = jnp.concatenate([dc, jnp.zeros((LANES - HEADS, ts), F32)], axis=0).T
        row = lax.broadcasted_iota(jnp.int32, dc.shape, 0)
        lane = lax.broadcasted_iota(jnp.int32, dc.shape, 1)
        for k in range(steps):
            sh = 1 << k
            dc = dc + jnp.where(row < ts - sh, pltpu.roll(dc, ts - sh, 0), 0.0)
        dc = dc + carry[0:1, :]
        carry[0:1, :] = dc[0:1, :]
        dff = jnp.where(lane < HEADS, dc * jax.nn.sigmoid(-(kt_r[...] + bf_r[...])), 0.0)
        outs[1][...] = dff
        dbf[...] += jnp.sum(dff, axis=0, keepdims=True)

    return _stage(name, fn, n,
                  [(seg0, _rows_rev(ts, LANES, n, 3)), (seg1, _rows_rev(ts, 1536, n)), (bf, _full(bf.shape)),
                   (qn, _full(qn.shape)), (kn, _full(kn.shape)), (dq, _heads_rev(ts, 64, n)), (dk, _heads_rev(ts, 64, n)),
                   (dv, _heads_rev(ts, 64, n)), (dck, pl.BlockSpec((HEADS, 1, ts), lambda i: (0, 0, n - 1 - i)))],
                  [(_sds((s, 1536)), _rows_rev(ts, 1536, n)), (_sds((s, LANES)), _rows_rev(ts, LANES, n))],
                  accs=[_sds((1, 64)), _sds((1, 64)), _sds((1, LANES))],
                  scratch=[pltpu.VMEM((SUBLANES, LANES), F32)])


def _allowed(i, j, t, chunk_causal):
    qpos = i * t + lax.broadcasted_iota(jnp.int32, (t, t), 0)
    kpos = j * t + lax.broadcasted_iota(jnp.int32, (t, t), 1)
    if chunk_causal:
        return (kpos // CHUNK) <= (qpos // CHUNK)
    return kpos <= qpos


def _pick_col(c_blk, h):
    lane = lax.broadcasted_iota(jnp.int32, c_blk.shape, 1)
    return jnp.sum(jnp.where(lane == h, c_blk, 0.0), axis=1, keepdims=True)


def _attn_fwd(q, k, v, scale, chunk_causal, name, c=None, ct=None):
    _, s, dk = q.shape
    dv = v.shape[2]
    t = _pick(s, (256, 128))
    bias = c is not None

    def body(*refs):
        if bias:
            q_ref, k_ref, v_ref, c_ref, ct_ref, o_ref, lse_ref = refs
        else:
            q_ref, k_ref, v_ref, o_ref, lse_ref = refs
        h, i = pl.program_id(0), pl.program_id(1)
        qb = q_ref[0]
        cq = _pick_col(c_ref[...], h) if bias else None

        def step(j, carry):
            m, l, acc = carry
            off = pl.multiple_of(j * t, t)
            sc = _nt(qb, k_ref[0, pl.ds(off, t), :]) * scale
            if bias:
                sc = sc + (cq - ct_ref[pl.ds(h, 1), pl.ds(off, t)])
            sc = jnp.where(_allowed(i, j, t, chunk_causal), sc, NEG)
            m_new = jnp.maximum(m, jnp.max(sc, axis=1, keepdims=True))
            pr = jnp.exp(sc - m_new)
            alpha = jnp.exp(m - m_new)
            return m_new, alpha * l + jnp.sum(pr, axis=1, keepdims=True), alpha * acc + _nn(pr, v_ref[0, pl.ds(off, t), :])

        m, l, acc = lax.fori_loop(0, i + 1, step, (jnp.full((t, 1), NEG, F32), jnp.zeros((t, 1), F32), jnp.zeros((t, dv), F32)))
        o_ref[0] = acc / l
        lse_ref[0] = m + jnp.log(l)

    ins = [q, k, v] + ([c, ct] if bias else [])
    in_specs = [pl.BlockSpec((1, t, dk), lambda h, i: (h, i, 0)), pl.BlockSpec((1, s, dk), lambda h, i: (h, 0, 0)),
                pl.BlockSpec((1, s, dv), lambda h, i: (h, 0, 0))]
    if bias:
        in_specs += [pl.BlockSpec((t, LANES), lambda h, i: (i, 0)), pl.BlockSpec((HEADS, s), lambda h, i: (0, 0))]
    return pl.pallas_call(
        body, name=name, grid=(HEADS, s // t), in_specs=in_specs,
        out_specs=[pl.BlockSpec((1, t, dv), lambda h, i: (h, i, 0)), pl.BlockSpec((1, t, 1), lambda h, i: (h, i, 0))],
        out_shape=[_sds((HEADS, s, dv)), _sds((HEADS, s, 1))],
        compiler_params=pltpu.CompilerParams(dimension_semantics=("parallel", "parallel"), vmem_limit_bytes=48 * 1024 * 1024),
    )(*ins)


def _attn_delta(q, k, v, do, lse, scale, chunk_causal, name, c=None, ct=None):
    _, s, dk = q.shape
    dv = v.shape[2]
    t = _pick(s, (256, 128))
    bias = c is not None

    def body(*refs):
        if bias:
            q_ref, k_ref, v_ref, do_ref, lse_ref, c_ref, ct_ref, dl_ref = refs
        else:
            q_ref, k_ref, v_ref, do_ref, lse_ref, dl_ref = refs
        h, i = pl.program_id(0), pl.program_id(1)
        qb, dob, lse_b = q_ref[0], do_ref[0], lse_ref[0]
        cq = _pick_col(c_ref[...], h) if bias else None

        def step(j, acc):
            off = pl.multiple_of(j * t, t)
            sc = _nt(qb, k_ref[0, pl.ds(off, t), :]) * scale
            if bias:
                sc = sc + (cq - ct_ref[pl.ds(h, 1), pl.ds(off, t)])
            pr = jnp.where(_allowed(i, j, t, chunk_causal), jnp.exp(sc - lse_b), 0.0)
            return acc + jnp.sum(pr * _nt(dob, v_ref[0, pl.ds(off, t), :]), axis=1, keepdims=True)

        dl_ref[0] = lax.fori_loop(0, i + 1, step, jnp.zeros((t, 1), F32))

    ins = [q, k, v, do, lse] + ([c, ct] if bias else [])
    in_specs = [pl.BlockSpec((1, t, dk), lambda h, i: (h, i, 0)), pl.BlockSpec((1, s, dk), lambda h, i: (h, 0, 0)),
                pl.BlockSpec((1, s, dv), lambda h, i: (h, 0, 0)), pl.BlockSpec((1, t, dv), lambda h, i: (h, i, 0)),
                pl.BlockSpec((1, t, 1), lambda h, i: (h, i, 0))]
    if bias:
        in_specs += [pl.BlockSpec((t, LANES), lambda h, i: (i, 0)), pl.BlockSpec((HEADS, s), lambda h, i: (0, 0))]
    return pl.pallas_call(
        body, name=name, grid=(HEADS, s // t), in_specs=in_specs,
        out_specs=pl.BlockSpec((1, t, 1), lambda h, i: (h, i, 0)), out_shape=_sds((HEADS, s, 1)),
        compiler_params=pltpu.CompilerParams(dimension_semantics=("parallel", "parallel"), vmem_limit_bytes=48 * 1024 * 1024),
    )(*ins)


def _attn_bwd(q, k, v, do, lse, delta, scale, chunk_causal, name, c=None, ct=None):
    _, s, dk = q.shape
    dv = v.shape[2]
    t = _pick(s, (256, 128))
    n = s // t
    bias = c is not None

    def body(*refs):
        if bias:
            q_ref, k_ref, v_ref, do_ref, lse_ref, dl_ref, c_ref, ct_ref, dq_ref, dk_ref, dv_ref, dck_ref = refs
        else:
            q_ref, k_ref, v_ref, do_ref, lse_ref, dl_ref, dq_ref, dk_ref, dv_ref = refs
        h, j = pl.program_id(0), pl.program_id(1)

        @pl.when(j == 0)
        def _():
            dq_ref[...] = jnp.zeros(dq_ref.shape, F32)

        kb, vb = k_ref[0], v_ref[0]
        ck = ct_ref[pl.ds(h, 1), pl.ds(pl.multiple_of(j * t, t), t)] if bias else None

        def step(i, carry):
            dk_acc, dv_acc, dck_acc = carry
            off = pl.multiple_of(i * t, t)
            qb = q_ref[0, pl.ds(off, t), :]
            dob = do_ref[0, pl.ds(off, t), :]
            sc = _nt(qb, kb) * scale
            if bias:
                sc = sc + (_pick_col(c_ref[pl.ds(off, t), :], h) - ck)
            pr = jnp.where(_allowed(i, j, t, chunk_causal), jnp.exp(sc - lse_ref[0, pl.ds(off, t), :]), 0.0)
            dv_acc = dv_acc + _tn(pr, dob)
            ds = pr * (_nt(dob, vb) - dl_ref[0, pl.ds(off, t), :])
            dq_ref[0, pl.ds(off, t), :] += _nn(ds, kb) * scale
            dk_acc = dk_acc + _tn(ds, qb) * scale
            if bias:
                dck_acc = dck_acc - jnp.sum(ds, axis=0, keepdims=True)
            return dk_acc, dv_acc, dck_acc

        dk_acc, dv_acc, dck_acc = lax.fori_loop(
            j, n, step, (jnp.zeros((t, dk), F32), jnp.zeros((t, dv), F32), jnp.zeros((1, t), F32)))
        dk_ref[0] = dk_acc
        dv_ref[0] = dv_acc
        if bias:
            dck_ref[0] = dck_acc

    ins = [q, k, v, do, lse, delta] + ([c, ct] if bias else [])
    full = lambda d: pl.BlockSpec((1, s, d), lambda h, j: (h, 0, 0))
    blk = lambda d: pl.BlockSpec((1, t, d), lambda h, j: (h, j, 0))
    in_specs = [full(dk), blk(dk), blk(dv), full(dv), full(1), full(1)]
    if bias:
        in_specs += [pl.BlockSpec((s, LANES), lambda h, j: (0, 0)), pl.BlockSpec((HEADS, s), lambda h, j: (0, 0))]
    out_specs = [full(dk), blk(dk), blk(dv)]
    out_shape = [_sds((HEADS, s, dk)), _sds((HEADS, s, dk)), _sds((HEADS, s, dv))]
    if bias:
        out_specs.append(pl.BlockSpec((1, 1, t), lambda h, j: (h, 0, j)))
        out_shape.append(_sds((HEADS, 1, s)))
    return pl.pallas_call(
        body, name=name, grid=(HEADS, n), in_specs=in_specs, out_specs=out_specs, out_shape=out_shape,
        compiler_params=pltpu.CompilerParams(dimension_semantics=("parallel", "arbitrary"), vmem_limit_bytes=56 * 1024 * 1024),
    )(*ins)


def _s5_disc(lr, li, ldt, br, bi):
    dt = jnp.exp(ldt)
    mag = jnp.exp(lr * dt)
    a_re = mag * jnp.cos(li * dt)
    a_im = mag * jnp.sin(li * dt)
    den = lr * lr + li * li
    f_re = ((a_re - 1.0) * lr + a_im * li) / den
    f_im = (a_im * lr - (a_re - 1.0) * li) / den
    return a_re, a_im, f_re * br - f_im * bi, f_re * bi + f_im * br


def _s5_params_fwd(lr16, li16, ldt16, br2, bi2, name):
    def body(a, b, c, d, e, o0, o1, o2, o3):
        r = _s5_disc(a[...], b[...], c[...], d[...], e[...])
        o0[...], o1[...], o2[...], o3[...] = r

    return pl.pallas_call(body, name=name, out_shape=[_sds((512, 64))] * 4)(lr16, li16, ldt16, br2, bi2)


def _s5_params_bwd(lr16, li16, ldt16, br2, bi2, da_re16, da_im16, dbb_re, dbb_im, name):
    def body(a, b, c, d, e, g0, g1, g2, g3, o_lr, o_li, o_dt, o_br, o_bi):
        _, vjp = jax.vjp(_s5_disc, a[...], b[...], c[...], d[...], e[...])
        dlr, dli, dldt, dbr, dbi = vjp((g0[...], g1[...], g2[...], g3[...]))
        grp = lambda z: z.reshape(S5_GROUPS, S5_GROUP, S5_STATE).sum(axis=1)
        o_lr[...] = grp(dlr)
        o_li[...] = grp(dli)
        o_dt[...] = jnp.sum(grp(dldt), axis=1, keepdims=True)
        o_br[...] = dbr
        o_bi[...] = dbi

    return pl.pallas_call(
        body, name=name, out_shape=[_sds((32, 64)), _sds((32, 64)), _sds((32, 1)), _sds((512, 64)), _sds((512, 64))],
    )(lr16, li16, ldt16, br2, bi2, da_re16, da_im16, dbb_re, dbb_im)


def _cmul(ar, ai, br, bi):
    return ar * br - ai * bi, ar * bi + ai * br


def _scan(bre, bim, a_re, a_im, reverse, name, x_re=None, x_im=None):
    s, w = bre.shape
    lw = _pick(w, (256, 128))
    nt = s // SUBLANES
    with_da = x_re is not None

    def body(*refs):
        if with_da:
            bre_r, bim_r, are_r, aim_r, xre_r, xim_r, ore_r, oim_r, dare_r, daim_r = refs
        else:
            bre_r, bim_r, are_r, aim_r, ore_r, oim_r = refs
        ar, ai = are_r[...], aim_r[...]
        shp = (SUBLANES, lw)
        row = lax.broadcasted_iota(jnp.int32, shp, 0)
        pows = [(ar, ai)]
        for _ in range(SUBLANES - 1):
            pows.append(_cmul(pows[-1][0], pows[-1][1], ar, ai))
        cm_r, cm_i = jnp.zeros(shp, F32), jnp.zeros(shp, F32)
        for r in range(SUBLANES):
            e = (SUBLANES - 1 - r) if reverse else r
            cm_r = jnp.where(row == r, jnp.broadcast_to(pows[e][0], shp), cm_r)
            cm_i = jnp.where(row == r, jnp.broadcast_to(pows[e][1], shp), cm_i)
        steps = [(1, pows[0]), (2, pows[1]), (4, pows[3])]

        def tile(it, carry):
            if with_da:
                c_r, c_i, acc_r, acc_i = carry
            else:
                c_r, c_i = carry
            r = (nt - 1 - it) if reverse else it
            off = pl.multiple_of(r * SUBLANES, SUBLANES)
            xr, xi = bre_r[pl.ds(off, SUBLANES), :], bim_r[pl.ds(off, SUBLANES), :]
            for sh, (pr, pi) in steps:
                if reverse:
                    keep = row < SUBLANES - sh
                    sr = jnp.where(keep, pltpu.roll(xr, SUBLANES - sh, 0), 0.0)
                    si = jnp.where(keep, pltpu.roll(xi, SUBLANES - sh, 0), 0.0)
                else:
                    keep = row >= sh
                    sr = jnp.where(keep, pltpu.roll(xr, sh, 0), 0.0)
                    si = jnp.where(keep, pltpu.roll(xi, sh, 0), 0.0)
                mr, mi = _cmul(pr, pi, sr, si)
                xr, xi = xr + mr, xi + mi
            mr, mi = _cmul(cm_r, cm_i, c_r, c_i)
            xr, xi = xr + mr, xi + mi
            ore_r[pl.ds(off, SUBLANES), :] = xr
            oim_r[pl.ds(off, SUBLANES), :] = xi
            edge = 0 if reverse else SUBLANES - 1
            c_r, c_i = xr[edge:edge + 1, :], xi[edge:edge + 1, :]
            if not with_da:
                return c_r, c_i
            fr, fi = xre_r[pl.ds(off, SUBLANES), :], xim_r[pl.ds(off, SUBLANES), :]
            poff = pl.multiple_of(jnp.maximum(r - 1, 0) * SUBLANES, SUBLANES)
            live = (r > 0).astype(F32)
            pr_last = xre_r[pl.ds(poff, SUBLANES), :][SUBLANES - 1:SUBLANES, :] * live
            pi_last = xim_r[pl.ds(poff, SUBLANES), :][SUBLANES - 1:SUBLANES, :] * live
            sr = jnp.where(row >= 1, pltpu.roll(fr, 1, 0), jnp.broadcast_to(pr_last, shp))
            si = jnp.where(row >= 1, pltpu.roll(fi, 1, 0), jnp.broadcast_to(pi_last, shp))
            acc_r = acc_r + xr * sr + xi * si
            acc_i = acc_i + xi * sr - xr * si
            return c_r, c_i, acc_r, acc_i

        z1 = jnp.zeros((1, lw), F32)
        if with_da:
            _, _, acc_r, acc_i = lax.fori_loop(0, nt, tile, (z1, z1, jnp.zeros(shp, F32), jnp.zeros(shp, F32)))
            dare_r[...] = jnp.sum(acc_r, axis=0, keepdims=True)
            daim_r[...] = jnp.sum(acc_i, axis=0, keepdims=True)
        else:
            lax.fori_loop(0, nt, tile, (z1, z1))

    col = pl.BlockSpec((s, lw), lambda j: (0, j))
    one = pl.BlockSpec((1, lw), lambda j: (0, j))
    ins = [bre, bim, a_re, a_im] + ([x_re, x_im] if with_da else [])
    in_specs = [col, col, one, one] + ([col, col] if with_da else [])
    out_specs = [col, col] + ([one, one] if with_da else [])
    out_shape = [_sds((s, w)), _sds((s, w))] + ([_sds((1, w)), _sds((1, w))] if with_da else [])
    return pl.pallas_call(
        body, name=name, grid=(w // lw,), in_specs=in_specs, out_specs=out_specs, out_shape=out_shape,
        compiler_params=pltpu.CompilerParams(dimension_semantics=("parallel",), vmem_limit_bytes=56 * 1024 * 1024),
    )(*ins)


def _s5_seg1(y0, u, d):
    return jax.nn.gelu(y0 + d * u)


def _s5_seg2(z, t, b):
    return z * jax.nn.sigmoid(t + b)


def _s5_post_fwd(y0, seg2, d, wglu, bglu, name):
    s = y0.shape[0]
    ts = _pick(s, (256, 128))

    def fn(ins, outs, accs, scr):
        z = _s5_seg1(ins[0][...], ins[1][...], ins[2][...])
        outs[0][...] = _s5_seg2(z, _nn(z, ins[3][...]), ins[4][...])

    return _stage(name, fn, s // ts,
                  [(y0, _rows(ts, 512)), (seg2, _rows(ts, 512)), (d, _full(d.shape)), (wglu, _full(wglu.shape)),
                   (bglu, _full(bglu.shape))], [(_sds((s, 512)), _rows(ts, 512))])[0]


def _s5_post_bwd(y0, seg2, d, wglu, bglu, dy, name):
    s = y0.shape[0]
    ts = _pick(s, (256, 128))

    def fn(ins, outs, accs, scr):
        y0_r, u_r, d_r, w_r, b_r, dy_r = ins
        z, vjp1 = jax.vjp(_s5_seg1, y0_r[...], u_r[...], d_r[...])
        t = _nn(z, w_r[...])
        _, vjp2 = jax.vjp(_s5_seg2, z, t, b_r[...])
        dz, dt, db = vjp2(dy_r[...])
        accs[0][...] += _tn(z, dt)
        accs[1][...] += db
        dy0, du, dd = vjp1(dz + _nt(dt, w_r[...]))
        accs[2][...] += dd
        outs[0][...] = dy0
        outs[1][...] = du

    return _stage(name, fn, s // ts,
                  [(y0, _rows(ts, 512)), (seg2, _rows(ts, 512)), (d, _full(d.shape)), (wglu, _full(wglu.shape)),
                   (bglu, _full(bglu.shape)), (dy, _rows(ts, 512))],
                  [(_sds((s, 512)), _rows(ts, 512)), (_sds((s, 512)), _rows(ts, 512))],
                  accs=[_sds((512, 512)), _sds((1, 512)), _sds((1, 512))])


def _gate_a(y, g):
    return y * jax.nn.silu(g)


def _gate_m(o0, o1, o2, m0, m1, m2):
    return jax.nn.sigmoid(m0) * o0 + jax.nn.sigmoid(m1) * o1 + jax.nn.sigmoid(m2) * o2


def _assemble(ybuf, o_mla, o_fox, y_s5):
    for h in range(HEADS):
        ybuf[:, 64 * h:64 * (h + 1)] = o_mla[h]
        ybuf[:, 512 + 64 * h:512 + 64 * (h + 1)] = o_fox[h]
    ybuf[:, 1024:1536] = y_s5[...]


def _gate_fwd(o_mla, o_fox, y_s5, seg3, seg4, x, wo, wout, name):
    s = x.shape[0]
    ts = _pick(s, (256, 128))

    def fn(ins, outs, accs, scr):
        om, of, ys, g_r, m_r, x_r, wo_r, wout_r = ins
        ybuf = scr[0]
        _assemble(ybuf, om, of, ys)
        a = _gate_a(ybuf[...], g_r[...])
        o = [_nn(a[:, 512 * b:512 * (b + 1)], wo_r[512 * b:512 * (b + 1), :]) for b in range(3)]
        merged = _gate_m(o[0], o[1], o[2], m_r[:, 0:1024], m_r[:, 1024:2048], m_r[:, 2048:3072])
        outs[0][...] = x_r[...] + _nn(merged, wout_r[...])

    return _stage(name, fn, s // ts,
                  [(o_mla, _heads(ts, 64)), (o_fox, _heads(ts, 64)), (y_s5, _rows(ts, 512)), (seg3, _rows(ts, 1536)),
                   (seg4, _rows(ts, 3072)), (x, _rows(ts, D_MODEL)), (wo, _full(wo.shape)), (wout, _full(wout.shape))],
                  [(_sds((s, D_MODEL)), _rows(ts, D_MODEL))], scratch=[pltpu.VMEM((ts, 1536), F32)])[0]


def _gate_bwd(o_mla, o_fox, y_s5, seg3, seg4, wo, wout, dout, name):
    s = dout.shape[0]
    ts = _pick(s, (128,))

    def fn(ins, outs, accs, scr):
        om, of, ys, g_r, m_r, wo_r, wout_r, dout_r = ins
        do_mla, do_fox, dys, dg_r, dm_r = outs
        dwo, dwout = accs
        ybuf, dabuf = scr
        _assemble(ybuf, om, of, ys)
        a, vjp_a = jax.vjp(_gate_a, ybuf[...], g_r[...])
        o = [_nn(a[:, 512 * b:512 * (b + 1)], wo_r[512 * b:512 * (b + 1), :]) for b in range(3)]
        ms = [m_r[:, 1024 * b:1024 * (b + 1)] for b in range(3)]
        merged, vjp_m = jax.vjp(_gate_m, *o, *ms)
        dout_v = dout_r[...]
        dwout[...] += _tn(merged, dout_v)
        cts = vjp_m(_nt(dout_v, wout_r[...]))
        for b in range(3):
            dm_r[:, 1024 * b:1024 * (b + 1)] = cts[3 + b]
            dwo[512 * b:512 * (b + 1), :] += _tn(a[:, 512 * b:512 * (b + 1)], cts[b])
            dabuf[:, 512 * b:512 * (b + 1)] = _nt(cts[b], wo_r[512 * b:512 * (b + 1), :])
        dy, dg = vjp_a(dabuf[...])
        dg_r[...] = dg
        dys[...] = dy[:, 1024:1536]
        for h in range(HEADS):
            do_mla[h] = dy[:, 64 * h:64 * (h + 1)]
            do_fox[h] = dy[:, 512 + 64 * h:512 + 64 * (h + 1)]

    return _stage(name, fn, s // ts,
                  [(o_mla, _heads(ts, 64)), (o_fox, _heads(ts, 64)), (y_s5, _rows(ts, 512)), (seg3, _rows(ts, 1536)),
                   (seg4, _rows(ts, 3072)), (wo, _full(wo.shape)), (wout, _full(wout.shape)), (dout, _rows(ts, D_MODEL))],
                  [(_sds((HEADS, s, 64)), _heads(ts, 64)), (_sds((HEADS, s, 64)), _heads(ts, 64)), (_sds((s, 512)), _rows(ts, 512)),
                   (_sds((s, 1536)), _rows(ts, 1536)), (_sds((s, 3072)), _rows(ts, 3072))],
                  accs=[_sds(wo.shape), _sds(wout.shape)], scratch=[pltpu.VMEM((ts, 1536), F32), pltpu.VMEM((ts, 1536), F32)],
                  vmem_mb=56)


def _loss_head(y, target, name):
    s = y.shape[0]
    ts = _pick(s, (256, 128))

    def fn(ins, outs, accs, scr):
        e = ins[0][...] - ins[1][...]
        outs[0][...] = e / D_MODEL
        accs[0][...] += 0.5 * jnp.sum(jnp.sum(e * e, axis=1, keepdims=True) / D_MODEL, axis=0, keepdims=True)

    r = _stage(name, fn, s // ts, [(y, _rows(ts, D_MODEL)), (target, _rows(ts, D_MODEL))],
               [(_sds((s, D_MODEL)), _rows(ts, D_MODEL))], accs=[_sds((1, 1))])
    return r[1], r[0]


def _pad_w_in(w):
    z = lambda n: jnp.zeros((w.shape[0], n), w.dtype)
    return jnp.concatenate([w[:, 0:384], w[:, 1952:1960], z(56), w[:, 384:416], z(32), w[:, 416:1952], w[:, 1960:IN_WIDTH]], axis=1)


def _unpad_w_in(d):
    return jnp.concatenate([d[:, 0:384], d[:, 448:480], d[:, 512:2048], d[:, 384:392], d[:, 2048:PAD_IN]], axis=1)


def _pad_wq(w):
    w = w.reshape(MLA_Q_RANK, HEADS, MLA_QK)
    return jnp.pad(w, ((0, 0), (0, 0), (0, LANES - MLA_QK))).reshape(MLA_Q_RANK, HEADS * LANES)


def _unpad_wq(d):
    return d.reshape(MLA_Q_RANK, HEADS, LANES)[:, :, :MLA_QK].reshape(MLA_Q_RANK, HEADS * MLA_QK)


def _pad_wkv(w):
    w = w.reshape(MLA_KV_RANK, HEADS, MLA_NOPE + MLA_V)
    k = jnp.pad(w[:, :, :MLA_NOPE], ((0, 0), (0, 0), (0, LANES - MLA_NOPE))).reshape(MLA_KV_RANK, HEADS * LANES)
    return jnp.concatenate([k, w[:, :, MLA_NOPE:].reshape(MLA_KV_RANK, HEADS * MLA_V)], axis=1)


def _unpad_wkv(d):
    k = d[:, :HEADS * LANES].reshape(MLA_KV_RANK, HEADS, LANES)[:, :, :MLA_NOPE]
    v = d[:, HEADS * LANES:].reshape(MLA_KV_RANK, HEADS, MLA_V)
    return jnp.concatenate([k, v], axis=2).reshape(MLA_KV_RANK, HEADS * (MLA_NOPE + MLA_V))


def _pad_lanes(v, n=LANES):
    return jnp.pad(v, (0, n - v.shape[0])).reshape(1, n)


def _block_diag(b):
    g, r, c = b.shape
    eye = jnp.eye(g, dtype=b.dtype)
    return (b[:, :, None, :] * eye[:, None, :, None]).reshape(g * r, g * c)


def _diag_blocks(m, r, c):
    g = S5_GROUPS
    m4 = m.reshape(g, r, g, c)
    idx = jnp.arange(g)
    return jnp.moveaxis(m4[idx, :, idx, :], 0, 0)


def _layer_params(l, w, small):
    p = {k: small[k][l] for k in small}
    q = {}
    q["norm_g"] = p["norm_g"].reshape(1, D_MODEL)
    q["mla_q_a_norm"] = p["mla_q_a_norm"].reshape(1, 256)
    q["mla_kv_a_norm"] = p["mla_kv_a_norm"].reshape(1, 128)
    q["mla_q_norm"] = _pad_lanes(p["mla_q_norm"])
    q["mla_k_norm"] = _pad_lanes(p["mla_k_norm"])
    q["fox_b_f"] = _pad_lanes(p["fox_b_f"])
    q["fox_q_norm"] = p["fox_q_norm"].reshape(1, 64)
    q["fox_k_norm"] = p["fox_k_norm"].reshape(1, 64)
    q["s5_d"] = p["s5_d"].reshape(1, 512)
    q["s5_b_glu"] = p["s5_b_glu"].reshape(1, 512)
    rep = lambda z: jnp.repeat(z, S5_GROUP, axis=0)
    q["lr16"], q["li16"] = rep(p["s5_lambda_re"]), rep(p["s5_lambda_im"])
    q["ldt16"] = rep(jnp.broadcast_to(p["s5_log_dt"][:, None], (S5_GROUPS, S5_STATE)))
    q["br2"] = p["s5_b_re"].transpose(0, 2, 1).reshape(512, 64)
    q["bi2"] = p["s5_b_im"].transpose(0, 2, 1).reshape(512, 64)
    q["c_re"], q["c_im"] = p["s5_c_re"], p["s5_c_im"]
    win = _pad_w_in(w["w_in"][l])
    q["w_seg"] = [win[:, SEG_OFF[k]:SEG_OFF[k] + SEG_W[k]] for k in range(5)]
    q["wq"] = _pad_wq(w["mla_w_q_up"][l])
    q["wkv"] = _pad_wkv(w["mla_w_kv_up"][l])
    q["wglu"] = w["s5_w_glu"][l]
    q["wo"] = w["w_branch_out"][l]
    q["wout"] = w["w_out"][l]
    return q


def _layer_fwd(l, x, tabs, q):
    n = lambda s: f"l{l}_{s}"
    sv = {"x": x}
    h = _norm_fwd(x, q["norm_g"], n("norm_fwd"))
    sv["h"] = h
    seg = [_mm(h, q["w_seg"][k], "nn", n(f"proj{k}")) for k in range(5)]
    sv["seg"] = seg
    mq, mk, mv = _mla_prep_fwd(seg[0], tabs, q, q["wq"], q["wkv"], n("mla_prep_fwd"))
    o_mla, lse_mla = _attn_fwd(mq, mk, mv, 1.0 / math.sqrt(MLA_QK), True, n("mla_attn_fwd"))
    sv.update(mq=mq, mk=mk, mv=mv, o_mla=o_mla, lse_mla=lse_mla)
    fq, fk, fv, c, ct = _fox_prep_fwd(seg[0], seg[1], q["fox_b_f"], q["fox_q_norm"], q["fox_k_norm"], n("fox_prep_fwd"))
    o_fox, lse_fox = _attn_fwd(fq, fk, fv, 1.0 / math.sqrt(FOX_DIM), False, n("fox_attn_fwd"), c=c, ct=ct)
    sv.update(fq=fq, fk=fk, fv=fv, c=c, ct=ct, o_fox=o_fox, lse_fox=lse_fox)
    a_re16, a_im16, bb_re, bb_im = _s5_params_fwd(q["lr16"], q["li16"], q["ldt16"], q["br2"], q["bi2"], n("s5_params_fwd"))
    a_re = a_re16.reshape(S5_GROUPS, S5_GROUP, S5_STATE)[:, 0, :].reshape(1, S5_LANES)
    a_im = a_im16.reshape(S5_GROUPS, S5_GROUP, S5_STATE)[:, 0, :].reshape(1, S5_LANES)
    wb_re = _block_diag(bb_re.reshape(S5_GROUPS, S5_GROUP, S5_STATE)).astype(BF16)
    wb_im = _block_diag(bb_im.reshape(S5_GROUPS, S5_GROUP, S5_STATE)).astype(BF16)
    wc_re = _block_diag(q["c_re"].transpose(0, 2, 1)).astype(BF16)
    wc_im = _block_diag(-q["c_im"].transpose(0, 2, 1)).astype(BF16)
    bu_re = _mm(seg[2], wb_re, "nn", n("s5_bu_re"))
    bu_im = _mm(seg[2], wb_im, "nn", n("s5_bu_im"))
    x_re, x_im = _scan(bu_re, bu_im, a_re, a_im, False, n("s5_scan_fwd"))
    y0 = _mm(x_im, wc_im, "nn", n("s5_y_im"), acc=_mm(x_re, wc_re, "nn", n("s5_y_re")))
    y_s5 = _s5_post_fwd(y0, seg[2], q["s5_d"], q["wglu"], q["s5_b_glu"], n("s5_post_fwd"))
    sv.update(a_re=a_re, a_im=a_im, wb_re=wb_re, wb_im=wb_im, wc_re=wc_re, wc_im=wc_im, x_re=x_re, x_im=x_im, y0=y0, y_s5=y_s5)
    out = _gate_fwd(o_mla, o_fox, y_s5, seg[3], seg[4], x, q["wo"], q["wout"], n("gate_fwd"))
    return out, sv


def _layer_bwd(l, dout, tabs, q, sv):
    n = lambda s: f"l{l}_{s}"
    seg = sv["seg"]
    g = {}
    (do_mla, do_fox, dy_s5, dseg3, dseg4, g["wo"], g["wout"]) = _gate_bwd(
        sv["o_mla"], sv["o_fox"], sv["y_s5"], seg[3], seg[4], q["wo"], q["wout"], dout, n("gate_bwd"))
    dy0, du_a, g["wglu"], g["s5_b_glu"], g["s5_d"] = _s5_post_bwd(sv["y0"], seg[2], q["s5_d"], q["wglu"], q["s5_b_glu"], dy_s5,
                                                                 n("s5_post_bwd"))
    dx_re = _mm(dy0, sv["wc_re"], "nt", n("s5_dx_re"))
    dx_im = _mm(dy0, sv["wc_im"], "nt", n("s5_dx_im"))
    dwc_re = _mm(sv["x_re"], dy0, "tn", n("s5_dwc_re"))
    dwc_im = _mm(sv["x_im"], dy0, "tn", n("s5_dwc_im"))
    g["s5_c_re"] = _diag_blocks(dwc_re, S5_STATE, S5_GROUP).transpose(0, 2, 1)
    g["s5_c_im"] = -_diag_blocks(dwc_im, S5_STATE, S5_GROUP).transpose(0, 2, 1)
    g_re, g_im, da_re, da_im = _scan(dx_re, dx_im, sv["a_re"], -sv["a_im"], True, n("s5_scan_bwd"), x_re=sv["x_re"], x_im=sv["x_im"])
    dseg2 = _mm(g_im, sv["wb_im"], "nt", n("s5_du_im"), acc=_mm(g_re, sv["wb_re"], "nt", n("s5_du_re"), acc=du_a))
    dbb_re = _diag_blocks(_mm(seg[2], g_re, "tn", n("s5_dwb_re")), S5_GROUP, S5_STATE).reshape(512, 64)
    dbb_im = _diag_blocks(_mm(seg[2], g_im, "tn", n("s5_dwb_im")), S5_GROUP, S5_STATE).reshape(512, 64)
    first = (jnp.arange(512) % S5_GROUP == 0).astype(F32)[:, None]
    da_re16 = jnp.repeat(da_re.reshape(S5_GROUPS, S5_STATE), S5_GROUP, axis=0) * first
    da_im16 = jnp.repeat(da_im.reshape(S5_GROUPS, S5_STATE), S5_GROUP, axis=0) * first
    dlr, dli, dldt, dbr2, dbi2 = _s5_params_bwd(q["lr16"], q["li16"], q["ldt16"], q["br2"], q["bi2"], da_re16, da_im16, dbb_re,
                                               dbb_im, n("s5_params_bwd"))
    g["s5_lambda_re"], g["s5_lambda_im"], g["s5_log_dt"] = dlr, dli, dldt.reshape(S5_GROUPS)
    g["s5_b_re"] = dbr2.reshape(S5_GROUPS, S5_GROUP, S5_STATE).transpose(0, 2, 1)
    g["s5_b_im"] = dbi2.reshape(S5_GROUPS, S5_GROUP, S5_STATE).transpose(0, 2, 1)
    dl_fox = _attn_delta(sv["fq"], sv["fk"], sv["fv"], do_fox, sv["lse_fox"], 1.0 / math.sqrt(FOX_DIM), False,
                         n("fox_attn_delta"), c=sv["c"], ct=sv["ct"])
    dfq, dfk, dfv, dck = _attn_bwd(sv["fq"], sv["fk"], sv["fv"], do_fox, sv["lse_fox"], dl_fox, 1.0 / math.sqrt(FOX_DIM), False,
                                   n("fox_attn_bwd"), c=sv["c"], ct=sv["ct"])
    dseg1, dff, g["fox_q_norm"], g["fox_k_norm"], dbf = _fox_prep_bwd(seg[0], seg[1], q["fox_b_f"], q["fox_q_norm"], q["fox_k_norm"],
                                                                      dfq, dfk, dfv, dck, n("fox_prep_bwd"))
    g["fox_b_f"] = dbf[0, :HEADS]
    dl_mla = _attn_delta(sv["mq"], sv["mk"], sv["mv"], do_mla, sv["lse_mla"], 1.0 / math.sqrt(MLA_QK), True, n("mla_attn_delta"))
    dmq, dmk, dmv = _attn_bwd(sv["mq"], sv["mk"], sv["mv"], do_mla, sv["lse_mla"], dl_mla, 1.0 / math.sqrt(MLA_QK), True,
                              n("mla_attn_bwd"))
    dseg0, dqan, dkvan, dqn, dkn, g["wq"], g["wkv"] = _mla_prep_bwd(seg[0], tabs, q, q["wq"], q["wkv"], dmq, dmk, dmv, dff,
                                                                   n("mla_prep_bwd"))
    g["mla_q_a_norm"], g["mla_kv_a_norm"] = dqan, dkvan
    g["mla_q_norm"], g["mla_k_norm"] = dqn[0, :MLA_QK], dkn[0, :MLA_QK]
    dsegs = [dseg0, dseg1, dseg2, dseg3, dseg4]
    dh = None
    for k in range(5):
        dh = _mm(dsegs[k], q["w_seg"][k], "nt", n(f"dh{k}"), acc=dh)
    g["w_in"] = jnp.concatenate([_mm(sv["h"], dsegs[k], "tn", n(f"dwin{k}")) for k in range(5)], axis=1)
    dx, g["norm_g"] = _norm_bwd(sv["x"], q["norm_g"], dh, dout, n("norm_bwd"))
    return dx, g


MESH = pl.DeviceIdType.MESH
ANY = pl.BlockSpec(memory_space=pl.ANY)


def _all_gather(block, name):
    r, cc = block.shape

    def body(x_ref, out_ref, send_sems, recv_sems, local_sem):
        x, y, c = lax.axis_index("x"), lax.axis_index("y"), lax.axis_index("c")
        me, sibling = (x, y, c), (x, y, 1 - c)
        chips = [(1 - x, y), (x, 1 - y), (1 - x, 1 - y)]

        def slot(px, py, pc):
            return out_ref.at[4 * px + 2 * py + pc]

        def copy(k, blk, to, src=None):
            return pltpu.make_async_remote_copy(src_ref=slot(*blk) if src is None else src, dst_ref=slot(*blk),
                                                send_sem=send_sems.at[k], recv_sem=recv_sems.at[k], device_id=to,
                                                device_id_type=MESH)

        mine = pltpu.make_async_copy(x_ref, slot(*me), local_sem)
        mine.start()
        first = [copy(0, me, sibling, src=x_ref)]
        first += [copy(1 + j, me, (*chip, c), src=x_ref) for j, chip in enumerate(chips)]
        for cp in first:
            cp.start()
        passed = [copy(4 + j, (*chip, c), sibling) for j, chip in enumerate(chips)]
        for j, chip in enumerate(chips):
            copy(1 + j, (*chip, c), me).wait_recv()
            passed[j].start()
        copy(0, sibling, me).wait_recv()
        for j, chip in enumerate(chips):
            copy(4 + j, (*chip, 1 - c), me).wait_recv()
        for cp in first + passed:
            cp.wait_send()
        mine.wait()

    return pl.pallas_call(
        body, name=name, out_shape=jax.ShapeDtypeStruct((N_DEV, r, cc), block.dtype), in_specs=[ANY], out_specs=ANY,
        scratch_shapes=[pltpu.SemaphoreType.DMA((7,)), pltpu.SemaphoreType.DMA((7,)), pltpu.SemaphoreType.DMA],
    )(block)


def _exchange_sibling(parts, name):
    _, r, cc = parts.shape

    def body(p_ref, got_ref, send_sems, recv_sems):
        x, y, c = lax.axis_index("x"), lax.axis_index("y"), lax.axis_index("c")
        sibling = (x, y, 1 - c)
        copies = []
        for j in range(4):
            copies.append(pltpu.make_async_remote_copy(src_ref=p_ref.at[2 * j + (1 - c)], dst_ref=got_ref.at[j],
                                                       send_sem=send_sems.at[j], recv_sem=recv_sems.at[j], device_id=sibling,
                                                       device_id_type=MESH))
        for cp in copies:
            cp.start()
        for cp in copies:
            cp.wait()

    return pl.pallas_call(
        body, name=name, out_shape=jax.ShapeDtypeStruct((4, r, cc), parts.dtype), in_specs=[ANY], out_specs=ANY,
        scratch_shapes=[pltpu.SemaphoreType.DMA((4,)), pltpu.SemaphoreType.DMA((4,))],
    )(parts)


def _exchange_chips(parts, name):
    _, r, cc = parts.shape

    def body(p_ref, got_ref, send_sems, recv_sems, local_sem):
        x, y, c = lax.axis_index("x"), lax.axis_index("y"), lax.axis_index("c")
        my_chip = 2 * x + y
        chips = [(1 - x, y), (x, 1 - y), (1 - x, 1 - y)]
        mine = pltpu.make_async_copy(p_ref.at[my_chip], got_ref.at[my_chip], local_sem)
        mine.start()
        copies = []
        for k, (px, py) in enumerate(chips):
            copies.append(pltpu.make_async_remote_copy(src_ref=p_ref.at[2 * px + py], dst_ref=got_ref.at[my_chip],
                                                       send_sem=send_sems.at[k], recv_sem=recv_sems.at[k], device_id=(px, py, c),
                                                       device_id_type=MESH))
        for cp in copies:
            cp.start()
        for k, (px, py) in enumerate(chips):
            pltpu.make_async_remote_copy(src_ref=p_ref.at[my_chip], dst_ref=got_ref.at[2 * px + py], send_sem=send_sems.at[k],
                                         recv_sem=recv_sems.at[k], device_id=(px, py, c), device_id_type=MESH).wait_recv()
        for cp in copies:
            cp.wait_send()
        mine.wait()

    return pl.pallas_call(
        body, name=name, out_shape=jax.ShapeDtypeStruct((4, r, cc), parts.dtype), in_specs=[ANY], out_specs=ANY,
        scratch_shapes=[pltpu.SemaphoreType.DMA((3,)), pltpu.SemaphoreType.DMA((3,)), pltpu.SemaphoreType.DMA],
    )(parts)


def _add_sibling(parts, got, name):
    _, r, cc = parts.shape
    tr = _pick(r, (512, 256, 128, 64, 32, 16, 8))
    c = lax.axis_index("c")

    def body(c_ref, p_ref, g_ref, o_ref):
        o_ref[...] = p_ref[...] + g_ref[...]

    return pl.pallas_call(
        body, name=name, out_shape=jax.ShapeDtypeStruct((4, r, cc), F32),
        grid_spec=pltpu.PrefetchScalarGridSpec(
            num_scalar_prefetch=1, grid=(4, r // tr),
            in_specs=[pl.BlockSpec((1, tr, cc), lambda j, i, cr: (2 * j + cr[0], i, 0)),
                      pl.BlockSpec((1, tr, cc), lambda j, i, cr: (j, i, 0))],
            out_specs=pl.BlockSpec((1, tr, cc), lambda j, i, cr: (j, i, 0))),
    )(c.reshape(1).astype(jnp.int32), parts, got)


def _sum_leading(parts, name):
    k, r, cc = parts.shape
    tr = _pick(r, (512, 256, 128, 64, 32, 16, 8))

    def body(p_ref, o_ref):
        acc = p_ref[0]
        for j in range(1, k):
            acc = acc + p_ref[j]
        o_ref[...] = acc

    return pl.pallas_call(
        body, name=name, out_shape=jax.ShapeDtypeStruct((r, cc), F32), grid=(r // tr,),
        in_specs=[pl.BlockSpec((k, tr, cc), lambda i: (0, i, 0))], out_specs=pl.BlockSpec((tr, cc), lambda i: (i, 0)),
    )(parts)


def _adamw(w, g, m, v, name):
    r, cc = w.shape
    tr = _pick(r, (256, 128, 64, 32, 16, 8))

    def body(w_ref, g_ref, m_ref, v_ref, d_ref, nm_ref, nv_ref):
        gg = g_ref[...]
        nm = ADAM_B1 * m_ref[...] + (1.0 - ADAM_B1) * gg
        nv = ADAM_B2 * v_ref[...] + (1.0 - ADAM_B2) * jnp.square(gg)
        m_hat = nm / (1.0 - ADAM_B1 ** ADAM_STEP)
        v_hat = nv / (1.0 - ADAM_B2 ** ADAM_STEP)
        d_ref[...] = -ADAM_LR * (m_hat / (jnp.sqrt(v_hat) + ADAM_EPS) + ADAM_WD * w_ref[...])
        nm_ref[...] = nm
        nv_ref[...] = nv

    spec = pl.BlockSpec((tr, cc), lambda i: (i, 0))
    return pl.pallas_call(
        body, name=name, grid=(r // tr,), in_specs=[spec] * 4, out_specs=[spec] * 3,
        out_shape=[jax.ShapeDtypeStruct((r, cc), F32)] * 3,
    )(w, g, m, v)


def _pack_rows(flat, lanes, row_mult):
    n = flat.shape[-1]
    rows = -(-n // lanes)
    rows = -(-rows // row_mult) * row_mult
    pad = rows * lanes - n
    if pad:
        flat = jnp.pad(flat, [(0, 0)] * (flat.ndim - 1) + [(0, pad)])
    return flat.reshape(flat.shape[:-1] + (rows, lanes))


def _rope_tables(positions):
    inv = 1.0 / (ROPE_THETA ** (jnp.arange(0, MLA_ROPE, 2, dtype=F32) / MLA_ROPE))
    ang = positions.astype(F32)[:, None] * inv
    cos, sin = jnp.cos(ang), jnp.sin(ang)
    s = positions.shape[0]
    z = lambda n: jnp.zeros((s, n), F32)
    c = jnp.concatenate([jnp.ones((s, 64), F32), cos, cos, z(32)], axis=1)
    sa = jnp.concatenate([z(64), -sin, z(48)], axis=1)
    sb = jnp.concatenate([z(80), sin, z(32)], axis=1)
    return c, sa, sb


def _device_grads(x, positions, target, w, small):
    tabs = _rope_tables(positions)
    qs = [_layer_params(l, w, small) for l in range(DEPTH)]
    h = x
    saves = []
    for l in range(DEPTH):
        h, sv = _layer_fwd(l, h, tabs, qs[l])
        saves.append(sv)
    loss, d = _loss_head(h, target, "loss_head")
    grads = [None] * DEPTH
    for l in reversed(range(DEPTH)):
        d, grads[l] = _layer_bwd(l, d, tabs, qs[l], saves[l])
    return loss[0, 0], d, grads


def kernel(x, positions, norm_g, w_in, mla_q_a_norm, mla_w_q_up, mla_kv_a_norm, mla_w_kv_up, mla_q_norm, mla_k_norm, fox_b_f, fox_q_norm, fox_k_norm, s5_lambda_re, s5_lambda_im, s5_log_dt, s5_b_re, s5_b_im, s5_c_re, s5_c_im, s5_d, s5_w_glu, s5_b_glu, w_branch_out, w_out, loss_target, m_norm_g, m_w_in, m_mla_q_a_norm, m_mla_w_q_up, m_mla_kv_a_norm, m_mla_w_kv_up, m_mla_q_norm, m_mla_k_norm, m_fox_b_f, m_fox_q_norm, m_fox_k_norm, m_s5_lambda_re, m_s5_lambda_im, m_s5_log_dt, m_s5_b_re, m_s5_b_im, m_s5_c_re, m_s5_c_im, m_s5_d, m_s5_w_glu, m_s5_b_glu, m_w_branch_out, m_w_out, v_norm_g, v_w_in, v_mla_q_a_norm, v_mla_w_q_up, v_mla_kv_a_norm, v_mla_w_kv_up, v_mla_q_norm, v_mla_k_norm, v_fox_b_f, v_fox_q_norm, v_fox_k_norm, v_s5_lambda_re, v_s5_lambda_im, v_s5_log_dt, v_s5_b_re, v_s5_b_im, v_s5_c_re, v_s5_c_im, v_s5_d, v_s5_w_glu, v_s5_b_glu, v_w_branch_out, v_w_out):
    env = dict(locals())
    wts = {k: env[k] for k in WEIGHTS}
    mom = {k: env["m_" + k] for k in WEIGHTS}
    var = {k: env["v_" + k] for k in WEIGHTS}

    flat = jnp.concatenate([wts[k].reshape(-1) for k in SHARDED]).astype(BF16)
    n_sh = flat.shape[0]
    gathered = _all_gather(_pack_rows(flat, PACK_LANES, 16), "gather_weights").reshape(N_DEV, -1)[:, :n_sh]
    full = {}
    off = 0
    for k in SHARDED:
        shp = wts[k].shape
        cnt = int(np.prod(shp))
        blk = gathered[:, off:off + cnt].reshape((N_DEV,) + shp)
        off += cnt
        if k in COL_SHARDED:
            full[k] = blk.transpose(1, 2, 0, 3).reshape(shp[0], shp[1], N_DEV * shp[2])
        else:
            full[k] = blk.transpose(1, 0, 2, 3).reshape(shp[0], N_DEV * shp[1], shp[2])

    small = {k: wts[k] for k in SMALL}
    loss, dx, grads = _device_grads(x[0], positions[0], loss_target[0], full, small)
    loss = lax.psum(loss, ("x", "y", "c"))

    big = {"w_in": [_unpad_w_in(g["w_in"]) for g in grads], "mla_w_q_up": [_unpad_wq(g["wq"]) for g in grads],
           "mla_w_kv_up": [_unpad_wkv(g["wkv"]) for g in grads], "s5_w_glu": [g["wglu"] for g in grads],
           "w_branch_out": [g["wo"] for g in grads], "w_out": [g["wout"] for g in grads]}
    pieces = []
    for k in SHARDED:
        gk = jnp.stack(big[k])
        lshape = wts[k].shape
        if k in COL_SHARDED:
            gk = gk.reshape(lshape[0], lshape[1], N_DEV, lshape[2]).transpose(2, 0, 1, 3)
        else:
            gk = gk.reshape(lshape[0], N_DEV, lshape[1], lshape[2]).transpose(1, 0, 2, 3)
        pieces.append(gk.reshape(N_DEV, -1))
    parts = _pack_rows(jnp.concatenate(pieces, axis=1), PACK_LANES, 8)
    got = _exchange_sibling(parts, "reduce_sibling")
    chip_sums = _add_sibling(parts, got, "reduce_sibling_add")
    mine = _sum_leading(_exchange_chips(chip_sums, "reduce_chips"), "reduce_chips_sum").reshape(-1)[:n_sh]
    grad_out, delta_out, m_out, v_out = {}, {}, {}, {}
    off = 0
    for k in SHARDED:
        shp = wts[k].shape
        cnt = int(np.prod(shp))
        gk = mine[off:off + cnt].reshape(shp)
        off += cnt
        grad_out[k] = gk
        two_d = (shp[0] * shp[1], shp[2])
        d, nm, nv = _adamw(wts[k].reshape(two_d), gk.reshape(two_d), mom[k].reshape(two_d), var[k].reshape(two_d), f"adamw_{k}")
        delta_out[k], m_out[k], v_out[k] = d.reshape(shp), nm.reshape(shp), nv.reshape(shp)

    sm = {k: jnp.stack([g[k] for g in grads]).reshape(wts[k].shape) for k in SMALL}
    flat_s = jnp.concatenate([sm[k].reshape(-1) for k in SMALL])
    n_sm = flat_s.shape[0]
    g_small = _sum_leading(_all_gather(_pack_rows(flat_s, LANES, 8), "gather_small_grads"), "sum_small_grads")
    pk = lambda d: _pack_rows(jnp.concatenate([d[k].reshape(-1) for k in SMALL]), LANES, 8)
    d_s, m_s, v_s = _adamw(pk(wts), g_small, pk(mom), pk(var), "adamw_small")
    g_small, d_s, m_s, v_s = (z.reshape(-1)[:n_sm] for z in (g_small, d_s, m_s, v_s))
    off = 0
    for k in SMALL:
        shp = wts[k].shape
        cnt = int(np.prod(shp))
        grad_out[k], delta_out[k], m_out[k], v_out[k] = (z[off:off + cnt].reshape(shp) for z in (g_small, d_s, m_s, v_s))
        off += cnt

    return (loss, dx[None], *[grad_out[k] for k in WEIGHTS], *[delta_out[k] for k in WEIGHTS],
            *[m_out[k] for k in WEIGHTS], *[v_out[k] for k in WEIGHTS])
```

```python
import functools
import math

import jax
import jax.numpy as jnp
import numpy as np
from jax import lax
from jax.experimental import pallas as pl
from jax.experimental.pallas import tpu as pltpu

F32 = jnp.float32
BF16 = jnp.bfloat16

D_MODEL = 1024
DEPTH = 2
CHUNK = 64
EPS = 1e-6
HEADS = 8
MLA_NOPE, MLA_ROPE, MLA_V = 64, 32, 64
MLA_Q_RANK, MLA_KV_RANK = 256, 128
MLA_QK = MLA_NOPE + MLA_ROPE
ROPE_THETA = 10000.0
FOX_DIM = 64
S5_WIDTH, S5_GROUP, S5_GROUPS, S5_STATE = 512, 16, 32, 64
S5_LANES = S5_GROUPS * S5_STATE
IN_WIDTH = 7080
N_DEV = 8
LANES = 128
SUBLANES = 8

ADAM_LR, ADAM_B1, ADAM_B2, ADAM_EPS, ADAM_WD, ADAM_STEP = 0.001, 0.9, 0.999, 1e-08, 0.01, 10

SEG_W = (512, 1536, 512, 1536, 3072)
SEG_OFF = (0, 512, 2048, 2560, 4096)
PAD_IN = 7168
NEG = -1e30

SHARDED = ("w_in", "mla_w_q_up", "mla_w_kv_up", "s5_w_glu", "w_branch_out", "w_out")
COL_SHARDED = ("w_in", "mla_w_q_up", "mla_w_kv_up")
SMALL = ("norm_g", "mla_q_a_norm", "mla_kv_a_norm", "mla_q_norm", "mla_k_norm", "fox_b_f", "fox_q_norm", "fox_k_norm",
         "s5_lambda_re", "s5_lambda_im", "s5_log_dt", "s5_b_re", "s5_b_im", "s5_c_re", "s5_c_im", "s5_d", "s5_b_glu")
WEIGHTS = ("norm_g", "w_in", "mla_q_a_norm", "mla_w_q_up", "mla_kv_a_norm", "mla_w_kv_up", "mla_q_norm", "mla_k_norm",
           "fox_b_f", "fox_q_norm", "fox_k_norm", "s5_lambda_re", "s5_lambda_im", "s5_log_dt", "s5_b_re", "s5_b_im",
           "s5_c_re", "s5_c_im", "s5_d", "s5_w_glu", "s5_b_glu", "w_branch_out", "w_out")
PACK_LANES = 512


def _pick(n, cands):
    for c in cands:
        if n % c == 0:
            return c
    return n


def _vmem(mb):
    return pltpu.CompilerParams(vmem_limit_bytes=mb * 1024 * 1024)


def _dot(a, b, dims):
    return lax.dot_general(a.astype(BF16), b.astype(BF16), (dims, ((), ())), preferred_element_type=F32)


def _nn(a, b):
    return _dot(a, b, ((1,), (0,)))


def _nt(a, b):
    return _dot(a, b, ((1,), (1,)))


def _tn(a, b):
    return _dot(a, b, ((0,), (0,)))


def _rms(x, g, n=None):
    n = x.shape[-1] if n is None else n
    return x * lax.rsqrt(jnp.sum(x * x, axis=-1, keepdims=True) / n + EPS) * g


def _rope(t, c, sa, sb):
    return t * c + pltpu.roll(t, LANES - 16, 1) * sa + pltpu.roll(t, 16, 1) * sb


def _rope_t(d, c, sa, sb):
    return d * c + pltpu.roll(d * sa, 16, 1) + pltpu.roll(d * sb, LANES - 16, 1)


def _mm(a, b, mode, name, acc=None, scale=None):
    if mode == "tn":
        kd, m = a.shape
    else:
        m, kd = a.shape
    n = b.shape[0] if mode == "nt" else b.shape[1]
    tm, tn, tk = _pick(m, (512, 256, 128)), _pick(n, (512, 256, 128)), _pick(kd, (512, 256, 128))
    nk = kd // tk
    if mode == "tn":
        a_spec = pl.BlockSpec((tk, tm), lambda i, j, k: (k, i))
    else:
        a_spec = pl.BlockSpec((tm, tk), lambda i, j, k: (i, k))
    if mode == "nt":
        b_spec = pl.BlockSpec((tn, tk), lambda i, j, k: (j, k))
    else:
        b_spec = pl.BlockSpec((tk, tn), lambda i, j, k: (k, j))
    dims = {"nn": ((1,), (0,)), "nt": ((1,), (1,)), "tn": ((0,), (0,))}[mode]
    o_spec = pl.BlockSpec((tm, tn), lambda i, j, k: (i, j))
    has_acc = acc is not None

    def body(*refs):
        if has_acc:
            a_ref, b_ref, c_ref, o_ref, acc_ref = refs
        else:
            a_ref, b_ref, o_ref, acc_ref = refs
        k = pl.program_id(2)

        @pl.when(k == 0)
        def _():
            acc_ref[...] = c_ref[...] if has_acc else jnp.zeros(acc_ref.shape, F32)

        acc_ref[...] += _dot(a_ref[...], b_ref[...], dims)

        @pl.when(k == nk - 1)
        def _():
            o_ref[...] = acc_ref[...]

    ins = [a, b] + ([acc] if has_acc else [])
    in_specs = [a_spec, b_spec] + ([o_spec] if has_acc else [])
    return pl.pallas_call(
        body, name=name, grid=(m // tm, n // tn, nk), in_specs=in_specs, out_specs=o_spec,
        out_shape=jax.ShapeDtypeStruct((m, n), F32), scratch_shapes=[pltpu.VMEM((tm, tn), F32)],
        compiler_params=pltpu.CompilerParams(dimension_semantics=("parallel", "parallel", "arbitrary"),
                                             vmem_limit_bytes=48 * 1024 * 1024),
    )(*ins)


def _stage(name, fn, n_steps, ins, outs, accs=(), scratch=(), vmem_mb=48):
    n_in, n_out, n_acc = len(ins), len(outs), len(accs)

    def body(*refs):
        in_refs = refs[:n_in]
        out_refs = refs[n_in:n_in + n_out]
        acc_refs = refs[n_in + n_out:n_in + n_out + n_acc]
        scr = refs[n_in + n_out + n_acc:]
        if n_acc:
            @pl.when(pl.program_id(0) == 0)
            def _():
                for r in acc_refs:
                    r[...] = jnp.zeros(r.shape, r.dtype)
        fn(in_refs, out_refs, acc_refs, scr)

    acc_specs = [pl.BlockSpec(a.shape, functools.partial(lambda i, nd: (0,) * nd, nd=len(a.shape))) for a in accs]
    res = pl.pallas_call(
        body, name=name, grid=(n_steps,),
        in_specs=[s for _, s in ins], out_specs=[s for _, s in outs] + acc_specs,
        out_shape=[s for s, _ in outs] + list(accs), scratch_shapes=list(scratch),
        compiler_params=pltpu.CompilerParams(dimension_semantics=("arbitrary",),
                                             vmem_limit_bytes=vmem_mb * 1024 * 1024),
    )(*[a for a, _ in ins])
    return res


def _rows(ts, w, j=0):
    return pl.BlockSpec((ts, w), lambda i: (i, j))


def _rows_rev(ts, w, n, j=0):
    return pl.BlockSpec((ts, w), lambda i: (n - 1 - i, j))


def _heads(ts, d):
    return pl.BlockSpec((HEADS, ts, d), lambda i: (0, i, 0))


def _heads_rev(ts, d, n):
    return pl.BlockSpec((HEADS, ts, d), lambda i: (0, n - 1 - i, 0))


def _full(shape):
    nd = len(shape)
    return pl.BlockSpec(tuple(shape), lambda i: (0,) * nd)


def _sds(shape, dtype=F32):
    return jax.ShapeDtypeStruct(tuple(shape), dtype)


def _norm_fwd(x, g, name):
    s = x.shape[0]
    ts = _pick(s, (256, 128))

    def fn(ins, outs, accs, scr):
        outs[0][...] = _rms(ins[0][...], ins[1][...]).astype(BF16)

    return _stage(name, fn, s // ts, [(x, _rows(ts, D_MODEL)), (g, _full(g.shape))],
                  [(_sds((s, D_MODEL), BF16), _rows(ts, D_MODEL))])[0]


def _norm_bwd(x, g, dh, dres, name):
    s = x.shape[0]
    ts = _pick(s, (256, 128))

    def fn(ins, outs, accs, scr):
        _, vjp = jax.vjp(_rms, ins[0][...], ins[1][...])
        dx, dg = vjp(ins[2][...])
        outs[0][...] = dx + ins[3][...]
        accs[0][...] += dg

    r = _stage(name, fn, s // ts,
               [(x, _rows(ts, D_MODEL)), (g, _full(g.shape)), (dh, _rows(ts, D_MODEL)), (dres, _rows(ts, D_MODEL))],
               [(_sds((s, D_MODEL)), _rows(ts, D_MODEL))], accs=[_sds((1, D_MODEL))])
    return r[0], r[1]


def _mla_q(qraw, c, sa, sb, qn):
    return _rms(_rope(qraw, c, sa, sb), qn, MLA_QK)


def _mla_prep_fwd(seg0, tabs, p, wq, wkv, name):
    s = seg0.shape[0]
    ts = _pick(s, (256, 128))

    def fn(ins, outs, accs, scr):
        blk, cos, sa, sb, qan, kvan, qn, kn, wq_r, wkv_r = ins
        b = blk[...]
        cq, ckv, kt = b[:, :256], b[:, 256:384], b[:, 384:512]
        lane = lax.broadcasted_iota(jnp.int32, kt.shape, 1)
        kpe = jnp.where(lane >= 64, kt, 0.0)
        q_raw = _nn(_rms(cq, qan[...]), wq_r[...])
        kv_raw = _nn(_rms(ckv, kvan[...]), wkv_r[...])
        c, a, bb = cos[...], sa[...], sb[...]
        for h in range(HEADS):
            outs[0][h] = _mla_q(q_raw[:, LANES * h:LANES * (h + 1)], c, a, bb, qn[...]).astype(BF16)
            outs[1][h] = _mla_q(kv_raw[:, LANES * h:LANES * (h + 1)] + kpe, c, a, bb, kn[...]).astype(BF16)
            outs[2][h] = kv_raw[:, 1024 + 64 * h:1024 + 64 * (h + 1)].astype(BF16)

    consts = [p["mla_q_a_norm"], p["mla_kv_a_norm"], p["mla_q_norm"], p["mla_k_norm"], wq, wkv]
    return _stage(name, fn, s // ts,
                  [(seg0, _rows(ts, 512))] + [(t, _rows(ts, LANES)) for t in tabs] + [(a, _full(a.shape)) for a in consts],
                  [(_sds((HEADS, s, LANES), BF16), _heads(ts, LANES)), (_sds((HEADS, s, LANES), BF16), _heads(ts, LANES)),
                   (_sds((HEADS, s, 64), BF16), _heads(ts, 64))])


def _mla_prep_bwd(seg0, tabs, p, wq, wkv, dq, dk, dv, dff, name):
    s = seg0.shape[0]
    ts = _pick(s, (256, 128))

    def fn(ins, outs, accs, scr):
        blk, cos, sa, sb, qan, kvan, qn, kn, wq_r, wkv_r, dq_r, dk_r, dv_r, dff_r = ins
        dqan, dkvan, dqn, dkn, dwq, dwkv = accs
        dqraw_s, dkvraw_s = scr
        b = blk[...]
        cq, ckv, kt = b[:, :256], b[:, 256:384], b[:, 384:512]
        lane = lax.broadcasted_iota(jnp.int32, kt.shape, 1)
        kpe = jnp.where(lane >= 64, kt, 0.0)
        cqn, vjp_cq = jax.vjp(_rms, cq, qan[...])
        ckvn, vjp_ckv = jax.vjp(_rms, ckv, kvan[...])
        q_raw = _nn(cqn, wq_r[...])
        kv_raw = _nn(ckvn, wkv_r[...])
        c, a, bb = cos[...], sa[...], sb[...]

        def head_bwd(raw, gain, d):
            t = _rope(raw, c, a, bb)
            _, vjp = jax.vjp(functools.partial(_rms, n=MLA_QK), t, gain)
            dt, dgain = vjp(d)
            return _rope_t(dt, c, a, bb), dgain

        dkpe = jnp.zeros(kt.shape, F32)
        for h in range(HEADS):
            dqh, dg = head_bwd(q_raw[:, LANES * h:LANES * (h + 1)], qn[...], dq_r[h])
            dqn[...] += dg
            dqraw_s[:, LANES * h:LANES * (h + 1)] = dqh
            dkh, dg = head_bwd(kv_raw[:, LANES * h:LANES * (h + 1)] + kpe, kn[...], dk_r[h])
            dkn[...] += dg
            dkvraw_s[:, LANES * h:LANES * (h + 1)] = dkh
            dkpe = dkpe + dkh
            dkvraw_s[:, 1024 + 64 * h:1024 + 64 * (h + 1)] = dv_r[h]
        dq_raw = dqraw_s[...]
        dkv_raw = dkvraw_s[...]
        dwq[...] += _tn(cqn, dq_raw)
        dwkv[...] += _tn(ckvn, dkv_raw)
        dcq, dg = vjp_cq(_nt(dq_raw, wq_r[...]))
        dqan[...] += dg
        dckv, dg = vjp_ckv(_nt(dkv_raw, wkv_r[...]))
        dkvan[...] += dg
        outs[0][:, 0:256] = dcq
        outs[0][:, 256:384] = dckv
        outs[0][:, 384:512] = jnp.where(lane >= 64, dkpe, 0.0) + dff_r[...]

    consts = [p["mla_q_a_norm"], p["mla_kv_a_norm"], p["mla_q_norm"], p["mla_k_norm"], wq, wkv]
    return _stage(name, fn, s // ts,
                  [(seg0, _rows(ts, 512))] + [(t, _rows(ts, LANES)) for t in tabs] + [(a, _full(a.shape)) for a in consts]
                  + [(dq, _heads(ts, LANES)), (dk, _heads(ts, LANES)), (dv, _heads(ts, 64)), (dff, _rows(ts, LANES))],
                  [(_sds((s, 512)), _rows(ts, 512))],
                  accs=[_sds((1, 256)), _sds((1, 128)), _sds((1, LANES)), _sds((1, LANES)), _sds(wq.shape), _sds(wkv.shape)],
                  scratch=[pltpu.VMEM((ts, 1024), F32), pltpu.VMEM((ts, 1536), F32)])


def _fox_prep_fwd(seg0, seg1, bf, qn, kn, name):
    s = seg0.shape[0]
    ts = _pick(s, (256, 128))
    steps = int(math.log2(ts))

    def fn(ins, outs, accs, scr):
        kt_r, x_r, bf_r, qn_r, kn_r = ins
        carry = scr[0]

        @pl.when(pl.program_id(0) == 0)
        def _():
            carry[...] = jnp.zeros(carry.shape, F32)

        x = x_r[...]
        for h in range(HEADS):
            outs[0][h] = _rms(x[:, 64 * h:64 * (h + 1)], qn_r[...]).astype(BF16)
            outs[1][h] = _rms(x[:, 512 + 64 * h:512 + 64 * (h + 1)], kn_r[...]).astype(BF16)
            outs[2][h] = x[:, 1024 + 64 * h:1024 + 64 * (h + 1)].astype(BF16)
        kt = kt_r[...]
        lane = lax.broadcasted_iota(jnp.int32, kt.shape, 1)
        row = lax.broadcasted_iota(jnp.int32, kt.shape, 0)
        cs = jnp.where(lane < HEADS, jax.nn.log_sigmoid(kt + bf_r[...]), 0.0)
        for k in range(steps):
            sh = 1 << k
            cs = cs + jnp.where(row >= sh, pltpu.roll(cs, sh, 0), 0.0)
        cs = cs + carry[0:1, :]
        outs[3][...] = cs
        outs[4][...] = cs.T[0:HEADS, :]
        carry[0:1, :] = cs[ts - 1:ts, :]

    return _stage(name, fn, s // ts,
                  [(seg0, _rows(ts, LANES, 3)), (seg1, _rows(ts, 1536)), (bf, _full(bf.shape)), (qn, _full(qn.shape)),
                   (kn, _full(kn.shape))],
                  [(_sds((HEADS, s, 64), BF16), _heads(ts, 64)), (_sds((HEADS, s, 64), BF16), _heads(ts, 64)),
                   (_sds((HEADS, s, 64), BF16), _heads(ts, 64)), (_sds((s, LANES)), _rows(ts, LANES)),
                   (_sds((HEADS, s)), pl.BlockSpec((HEADS, ts), lambda i: (0, i)))],
                  scratch=[pltpu.VMEM((SUBLANES, LANES), F32)])


def _fox_prep_bwd(seg0, seg1, bf, qn, kn, dq, dk, dv, dck, name):
    s = seg0.shape[0]
    ts = _pick(s, (256, 128))
    n = s // ts
    steps = int(math.log2(ts))

    def fn(ins, outs, accs, scr):
        kt_r, x_r, bf_r, qn_r, kn_r, dq_r, dk_r, dv_r, dck_r = ins
        dqn, dkn, dbf = accs
        carry = scr[0]

        @pl.when(pl.program_id(0) == 0)
        def _():
            carry[...] = jnp.zeros(carry.shape, F32)

        x = x_r[...]
        for h in range(HEADS):
            _, vjp = jax.vjp(_rms, x[:, 64 * h:64 * (h + 1)], qn_r[...])
            d, dg = vjp(dq_r[h])
            outs[0][:, 64 * h:64 * (h + 1)] = d
            dqn[...] += dg
            _, vjp = jax.vjp(_rms, x[:, 512 + 64 * h:512 + 64 * (h + 1)], kn_r[...])
            d, dg = vjp(dk_r[h])
            outs[0][:, 512 + 64 * h:512 + 64 * (h + 1)] = d
            dkn[...] += dg
            outs[0][:, 1024 + 64 * h:1024 + 64 * (h + 1)] = dv_r[h]
        dc = dck_r[...].reshape(HEADS, ts)
        dc = jnp.concatenate([dc, jnp.zeros((LANES - HEADS, ts), F32)], axis=0).T
        row = lax.broadcasted_iota(jnp.int32, dc.shape, 0)
        lane = lax.broadcasted_iota(jnp.int32, dc.shape, 1)
        for k in range(steps):
            sh = 1 << k
            dc = dc + jnp.where(row < ts - sh, pltpu.roll(dc, ts - sh, 0), 0.0)
        dc = dc + carry[0:1, :]
        carry[0:1, :] = dc[0:1, :]
        dff = jnp.where(lane < HEADS, dc * jax.nn.sigmoid(-(kt_r[...] + bf_r[...])), 0.0)
        outs[1][...] = dff
        dbf[...] += jnp.sum(dff, axis=0, keepdims=True)

    return _stage(name, fn, n,
                  [(seg0, _rows_rev(ts, LANES, n, 3)), (seg1, _rows_rev(ts, 1536, n)), (bf, _full(bf.shape)),
                   (qn, _full(qn.shape)), (kn, _full(kn.shape)), (dq, _heads_rev(ts, 64, n)), (dk, _heads_rev(ts, 64, n)),
                   (dv, _heads_rev(ts, 64, n)), (dck, pl.BlockSpec((HEADS, 1, ts), lambda i: (0, 0, n - 1 - i)))],
                  [(_sds((s, 1536)), _rows_rev(ts, 1536, n)), (_sds((s, LANES)), _rows_rev(ts, LANES, n))],
                  accs=[_sds((1, 64)), _sds((1, 64)), _sds((1, LANES))],
                  scratch=[pltpu.VMEM((SUBLANES, LANES), F32)])


def _allowed(i, j, t, chunk_causal):
    qpos = i * t + lax.broadcasted_iota(jnp.int32, (t, t), 0)
    kpos = j * t + lax.broadcasted_iota(jnp.int32, (t, t), 1)
    if chunk_causal:
        return (kpos // CHUNK) <= (qpos // CHUNK)
    return kpos <= qpos


def _pick_col(c_blk, h):
    lane = lax.broadcasted_iota(jnp.int32, c_blk.shape, 1)
    return jnp.sum(jnp.where(lane == h, c_blk, 0.0), axis=1, keepdims=True)


def _attn_fwd(q, k, v, scale, chunk_causal, name, c=None, ct=None):
    _, s, dk = q.shape
    dv = v.shape[2]
    t = _pick(s, (256, 128))
    bias = c is not None

    def body(*refs):
        if bias:
            q_ref, k_ref, v_ref, c_ref, ct_ref, o_ref, lse_ref = refs
        else:
            q_ref, k_ref, v_ref, o_ref, lse_ref = refs
        h, i = pl.program_id(0), pl.program_id(1)
        qb = q_ref[0]
        cq = _pick_col(c_ref[...], h) if bias else None

        def step(j, carry, diagonal):
            m, l, acc = carry
            off = pl.multiple_of(j * t, t)
            sc = _nt(qb, k_ref[0, pl.ds(off, t), :]) * scale
            if bias:
                sc = sc + (cq - ct_ref[pl.ds(h, 1), pl.ds(off, t)])
            if diagonal:
                sc = jnp.where(_allowed(i, j, t, chunk_causal), sc, NEG)
            m_new = jnp.maximum(m, jnp.max(sc, axis=1, keepdims=True))
            pr = jnp.exp(sc - m_new)
            alpha = jnp.exp(m - m_new)
            return m_new, alpha * l + jnp.sum(pr, axis=1, keepdims=True), alpha * acc + _nn(pr, v_ref[0, pl.ds(off, t), :])

        init = (jnp.full((t, 1), NEG, F32), jnp.zeros((t, 1), F32), jnp.zeros((t, dv), F32))
        m, l, acc = step(i, lax.fori_loop(0, i, functools.partial(step, diagonal=False), init), True)
        o_ref[0] = acc / l
        lse_ref[0] = m + jnp.log(l)

    ins = [q, k, v] + ([c, ct] if bias else [])
    in_specs = [pl.BlockSpec((1, t, dk), lambda h, i: (h, i, 0)), pl.BlockSpec((1, s, dk), lambda h, i: (h, 0, 0)),
                pl.BlockSpec((1, s, dv), lambda h, i: (h, 0, 0))]
    if bias:
        in_specs += [pl.BlockSpec((t, LANES), lambda h, i: (i, 0)), pl.BlockSpec((HEADS, s), lambda h, i: (0, 0))]
    return pl.pallas_call(
        body, name=name, grid=(HEADS, s // t), in_specs=in_specs,
        out_specs=[pl.BlockSpec((1, t, dv), lambda h, i: (h, i, 0)), pl.BlockSpec((1, t, 1), lambda h, i: (h, i, 0))],
        out_shape=[_sds((HEADS, s, dv)), _sds((HEADS, s, 1))],
        compiler_params=pltpu.CompilerParams(dimension_semantics=("parallel", "parallel"), vmem_limit_bytes=48 * 1024 * 1024),
    )(*ins)


def _attn_delta(q, k, v, do, lse, scale, chunk_causal, name, c=None, ct=None):
    _, s, dk = q.shape
    dv = v.shape[2]
    t = _pick(s, (256, 128))
    bias = c is not None

    def body(*refs):
        if bias:
            q_ref, k_ref, v_ref, do_ref, lse_ref, c_ref, ct_ref, dl_ref = refs
        else:
            q_ref, k_ref, v_ref, do_ref, lse_ref, dl_ref = refs
        h, i = pl.program_id(0), pl.program_id(1)
        qb, dob, lse_b = q_ref[0], do_ref[0], lse_ref[0]
        cq = _pick_col(c_ref[...], h) if bias else None

        def step(j, acc, diagonal):
            off = pl.multiple_of(j * t, t)
            sc = _nt(qb, k_ref[0, pl.ds(off, t), :]) * scale
            if bias:
                sc = sc + (cq - ct_ref[pl.ds(h, 1), pl.ds(off, t)])
            pr = jnp.exp(sc - lse_b)
            if diagonal:
                pr = jnp.where(_allowed(i, j, t, chunk_causal), pr, 0.0)
            return acc + jnp.sum(pr * _nt(dob, v_ref[0, pl.ds(off, t), :]), axis=1, keepdims=True)

        dl_ref[0] = step(i, lax.fori_loop(0, i, functools.partial(step, diagonal=False), jnp.zeros((t, 1), F32)), True)

    ins = [q, k, v, do, lse] + ([c, ct] if bias else [])
    in_specs = [pl.BlockSpec((1, t, dk), lambda h, i: (h, i, 0)), pl.BlockSpec((1, s, dk), lambda h, i: (h, 0, 0)),
                pl.BlockSpec((1, s, dv), lambda h, i: (h, 0, 0)), pl.BlockSpec((1, t, dv), lambda h, i: (h, i, 0)),
                pl.BlockSpec((1, t, 1), lambda h, i: (h, i, 0))]
    if bias:
        in_specs += [pl.BlockSpec((t, LANES), lambda h, i: (i, 0)), pl.BlockSpec((HEADS, s), lambda h, i: (0, 0))]
    return pl.pallas_call(
        body, name=name, grid=(HEADS, s // t), in_specs=in_specs,
        out_specs=pl.BlockSpec((1, t, 1), lambda h, i: (h, i, 0)), out_shape=_sds((HEADS, s, 1)),
        compiler_params=pltpu.CompilerParams(dimension_semantics=("parallel", "parallel"), vmem_limit_bytes=48 * 1024 * 1024),
    )(*ins)


def _attn_bwd(q, k, v, do, lse, delta, scale, chunk_causal, name, c=None, ct=None):
    _, s, dk = q.shape
    dv = v.shape[2]
    t = _pick(s, (256, 128))
    n = s // t
    bias = c is not None

    def body(*refs):
        if bias:
            q_ref, k_ref, v_ref, do_ref, lse_ref, dl_ref, c_ref, ct_ref, dq_ref, dk_ref, dv_ref, dck_ref = refs
        else:
            q_ref, k_ref, v_ref, do_ref, lse_ref, dl_ref, dq_ref, dk_ref, dv_ref = refs
        h, j = pl.program_id(0), pl.program_id(1)

        @pl.when(j == 0)
        def _():
            dq_ref[...] = jnp.zeros(dq_ref.shape, F32)

        kb, vb = k_ref[0], v_ref[0]
        ck = ct_ref[pl.ds(h, 1), pl.ds(pl.multiple_of(j * t, t), t)] if bias else None

        def step(i, carry, diagonal):
            dk_acc, dv_acc, dck_acc = carry
            off = pl.multiple_of(i * t, t)
            qb = q_ref[0, pl.ds(off, t), :]
            dob = do_ref[0, pl.ds(off, t), :]
            sc = _nt(qb, kb) * scale
            if bias:
                sc = sc + (_pick_col(c_ref[pl.ds(off, t), :], h) - ck)
            pr = jnp.exp(sc - lse_ref[0, pl.ds(off, t), :])
            if diagonal:
                pr = jnp.where(_allowed(i, j, t, chunk_causal), pr, 0.0)
            dv_acc = dv_acc + _tn(pr, dob)
            ds = pr * (_nt(dob, vb) - dl_ref[0, pl.ds(off, t), :])
            dq_ref[0, pl.ds(off, t), :] += _nn(ds, kb) * scale
            dk_acc = dk_acc + _tn(ds, qb) * scale
            if bias:
                dck_acc = dck_acc - jnp.sum(ds, axis=0, keepdims=True)
            return dk_acc, dv_acc, dck_acc

        first = step(j, (jnp.zeros((t, dk), F32), jnp.zeros((t, dv), F32), jnp.zeros((1, t), F32)), True)
        dk_acc, dv_acc, dck_acc = lax.fori_loop(j + 1, n, functools.partial(step, diagonal=False), first)
        dk_ref[0] = dk_acc
        dv_ref[0] = dv_acc
        if bias:
            dck_ref[0] = dck_acc

    ins = [q, k, v, do, lse, delta] + ([c, ct] if bias else [])
    full = lambda d: pl.BlockSpec((1, s, d), lambda h, j: (h, 0, 0))
    blk = lambda d: pl.BlockSpec((1, t, d), lambda h, j: (h, j, 0))
    in_specs = [full(dk), blk(dk), blk(dv), full(dv), full(1), full(1)]
    if bias:
        in_specs += [pl.BlockSpec((s, LANES), lambda h, j: (0, 0)), pl.BlockSpec((HEADS, s), lambda h, j: (0, 0))]
    out_specs = [full(dk), blk(dk), blk(dv)]
    out_shape = [_sds((HEADS, s, dk)), _sds((HEADS, s, dk)), _sds((HEADS, s, dv))]
    if bias:
        out_specs.append(pl.BlockSpec((1, 1, t), lambda h, j: (h, 0, j)))
        out_shape.append(_sds((HEADS, 1, s)))
    return pl.pallas_call(
        body, name=name, grid=(HEADS, n), in_specs=in_specs, out_specs=out_specs, out_shape=out_shape,
        compiler_params=pltpu.CompilerParams(dimension_semantics=("parallel", "arbitrary"), vmem_limit_bytes=56 * 1024 * 1024),
    )(*ins)


def _s5_disc(lr, li, ldt, br, bi):
    dt = jnp.exp(ldt)
    mag = jnp.exp(lr * dt)
    a_re = mag * jnp.cos(li * dt)
    a_im = mag * jnp.sin(li * dt)
    den = lr * lr + li * li
    f_re = ((a_re - 1.0) * lr + a_im * li) / den
    f_im = (a_im * lr - (a_re - 1.0) * li) / den
    return a_re, a_im, f_re * br - f_im * bi, f_re * bi + f_im * br


def _s5_params_fwd(lr16, li16, ldt16, br2, bi2, name):
    def body(a, b, c, d, e, o0, o1, o2, o3):
        r = _s5_disc(a[...], b[...], c[...], d[...], e[...])
        o0[...], o1[...], o2[...], o3[...] = r

    return pl.pallas_call(body, name=name, out_shape=[_sds((512, 64))] * 4)(lr16, li16, ldt16, br2, bi2)


def _s5_params_bwd(lr16, li16, ldt16, br2, bi2, da_re16, da_im16, dbb_re, dbb_im, name):
    def body(a, b, c, d, e, g0, g1, g2, g3, o_lr, o_li, o_dt, o_br, o_bi):
        _, vjp = jax.vjp(_s5_disc, a[...], b[...], c[...], d[...], e[...])
        dlr, dli, dldt, dbr, dbi = vjp((g0[...], g1[...], g2[...], g3[...]))
        grp = lambda z: z.reshape(S5_GROUPS, S5_GROUP, S5_STATE).sum(axis=1)
        o_lr[...] = grp(dlr)
        o_li[...] = grp(dli)
        o_dt[...] = jnp.sum(grp(dldt), axis=1, keepdims=True)
        o_br[...] = dbr
        o_bi[...] = dbi

    return pl.pallas_call(
        body, name=name, out_shape=[_sds((32, 64)), _sds((32, 64)), _sds((32, 1)), _sds((512, 64)), _sds((512, 64))],
    )(lr16, li16, ldt16, br2, bi2, da_re16, da_im16, dbb_re, dbb_im)


def _cmul(ar, ai, br, bi):
    return ar * br - ai * bi, ar * bi + ai * br


def _scan(bre, bim, a_re, a_im, reverse, name, x_re=None, x_im=None):
    s, w = bre.shape
    lw = _pick(w, (256, 128))
    nt = s // SUBLANES
    with_da = x_re is not None

    def body(*refs):
        if with_da:
            bre_r, bim_r, are_r, aim_r, xre_r, xim_r, ore_r, oim_r, dare_r, daim_r = refs
        else:
            bre_r, bim_r, are_r, aim_r, ore_r, oim_r = refs
        ar, ai = are_r[...], aim_r[...]
        shp = (SUBLANES, lw)
        row = lax.broadcasted_iota(jnp.int32, shp, 0)
        pows = [(ar, ai)]
        for _ in range(SUBLANES - 1):
            pows.append(_cmul(pows[-1][0], pows[-1][1], ar, ai))
        cm_r, cm_i = jnp.zeros(shp, F32), jnp.zeros(shp, F32)
        for r in range(SUBLANES):
            e = (SUBLANES - 1 - r) if reverse else r
            cm_r = jnp.where(row == r, jnp.broadcast_to(pows[e][0], shp), cm_r)
            cm_i = jnp.where(row == r, jnp.broadcast_to(pows[e][1], shp), cm_i)
        steps = [(1, pows[0]), (2, pows[1]), (4, pows[3])]

        def tile(it, carry):
            if with_da:
                c_r, c_i, acc_r, acc_i = carry
            else:
                c_r, c_i = carry
            r = (nt - 1 - it) if reverse else it
            off = pl.multiple_of(r * SUBLANES, SUBLANES)
            xr, xi = bre_r[pl.ds(off, SUBLANES), :], bim_r[pl.ds(off, SUBLANES), :]
            for sh, (pr, pi) in steps:
                if reverse:
                    keep = row < SUBLANES - sh
                    sr = jnp.where(keep, pltpu.roll(xr, SUBLANES - sh, 0), 0.0)
                    si = jnp.where(keep, pltpu.roll(xi, SUBLANES - sh, 0), 0.0)
                else:
                    keep = row >= sh
                    sr = jnp.where(keep, pltpu.roll(xr, sh, 0), 0.0)
                    si = jnp.where(keep, pltpu.roll(xi, sh, 0), 0.0)
                mr, mi = _cmul(pr, pi, sr, si)
                xr, xi = xr + mr, xi + mi
            mr, mi = _cmul(cm_r, cm_i, c_r, c_i)
            xr, xi = xr + mr, xi + mi
            ore_r[pl.ds(off, SUBLANES), :] = xr
            oim_r[pl.ds(off, SUBLANES), :] = xi
            edge = 0 if reverse else SUBLANES - 1
            c_r, c_i = xr[edge:edge + 1, :], xi[edge:edge + 1, :]
            if not with_da:
                return c_r, c_i
            fr, fi = xre_r[pl.ds(off, SUBLANES), :], xim_r[pl.ds(off, SUBLANES), :]
            poff = pl.multiple_of(jnp.maximum(r - 1, 0) * SUBLANES, SUBLANES)
            live = (r > 0).astype(F32)
            pr_last = xre_r[pl.ds(poff, SUBLANES), :][SUBLANES - 1:SUBLANES, :] * live
            pi_last = xim_r[pl.ds(poff, SUBLANES), :][SUBLANES - 1:SUBLANES, :] * live
            sr = jnp.where(row >= 1, pltpu.roll(fr, 1, 0), jnp.broadcast_to(pr_last, shp))
            si = jnp.where(row >= 1, pltpu.roll(fi, 1, 0), jnp.broadcast_to(pi_last, shp))
            acc_r = acc_r + xr * sr + xi * si
            acc_i = acc_i + xi * sr - xr * si
            return c_r, c_i, acc_r, acc_i

        z1 = jnp.zeros((1, lw), F32)
        if with_da:
            _, _, acc_r, acc_i = lax.fori_loop(0, nt, tile, (z1, z1, jnp.zeros(shp, F32), jnp.zeros(shp, F32)))
            dare_r[...] = jnp.sum(acc_r, axis=0, keepdims=True)
            daim_r[...] = jnp.sum(acc_i, axis=0, keepdims=True)
        else:
            lax.fori_loop(0, nt, tile, (z1, z1))

    col = pl.BlockSpec((s, lw), lambda j: (0, j))
    one = pl.BlockSpec((1, lw), lambda j: (0, j))
    ins = [bre, bim, a_re, a_im] + ([x_re, x_im] if with_da else [])
    in_specs = [col, col, one, one] + ([col, col] if with_da else [])
    out_specs = [col, col] + ([one, one] if with_da else [])
    out_shape = [_sds((s, w)), _sds((s, w))] + ([_sds((1, w)), _sds((1, w))] if with_da else [])
    return pl.pallas_call(
        body, name=name, grid=(w // lw,), in_specs=in_specs, out_specs=out_specs, out_shape=out_shape,
        compiler_params=pltpu.CompilerParams(dimension_semantics=("parallel",), vmem_limit_bytes=56 * 1024 * 1024),
    )(*ins)


S5_SUPER = 4


def _s5_mm(a, w4, mode, name, acc=None):
    s = a.shape[0]
    _, r, c = w4.shape
    wa, wo = (r, c) if mode == "nn" else (c, r)
    tm = _pick(s, (512, 256, 128))
    has_acc = acc is not None

    def body(*refs):
        if has_acc:
            a_ref, w_ref, c_ref, o_ref = refs
        else:
            a_ref, w_ref, o_ref = refs
        res = _nn(a_ref[...], w_ref[0]) if mode == "nn" else _nt(a_ref[...], w_ref[0])
        o_ref[...] = res + c_ref[...] if has_acc else res

    o_spec = pl.BlockSpec((tm, wo), lambda k, i: (i, k))
    ins = [a, w4] + ([acc] if has_acc else [])
    in_specs = [pl.BlockSpec((tm, wa), lambda k, i: (i, k)), pl.BlockSpec((1, r, c), lambda k, i: (k, 0, 0))]
    return pl.pallas_call(
        body, name=name, grid=(S5_SUPER, s // tm), in_specs=in_specs + ([o_spec] if has_acc else []), out_specs=o_spec,
        out_shape=_sds((s, S5_SUPER * wo)),
        compiler_params=pltpu.CompilerParams(dimension_semantics=("parallel", "parallel"), vmem_limit_bytes=48 * 1024 * 1024),
    )(*ins)


def _s5_group_tn(a, b, name):
    s = a.shape[0]
    tk = _pick(s, (512, 256, 128))
    nk = s // tk

    def body(a_ref, b_ref, o_ref, acc_ref):
        k = pl.program_id(1)

        @pl.when(k == 0)
        def _():
            acc_ref[...] = jnp.zeros(acc_ref.shape, F32)

        acc_ref[...] += _tn(a_ref[...], b_ref[...])

        @pl.when(k == nk - 1)
        def _():
            p = acc_ref[...]
            grp = lax.broadcasted_iota(jnp.int32, (LANES, S5_STATE), 0) // S5_GROUP
            out = jnp.zeros((LANES, S5_STATE), F32)
            for j in range(LANES // S5_GROUP):
                out = jnp.where(grp == j, p[:, S5_STATE * j:S5_STATE * (j + 1)], out)
            o_ref[...] = out

    return pl.pallas_call(
        body, name=name, grid=(S5_SUPER, nk),
        in_specs=[pl.BlockSpec((tk, LANES), lambda g, k: (k, g)), pl.BlockSpec((tk, 512), lambda g, k: (k, g))],
        out_specs=pl.BlockSpec((LANES, S5_STATE), lambda g, k: (g, 0)), out_shape=_sds((512, S5_STATE)),
        scratch_shapes=[pltpu.VMEM((LANES, 512), F32)],
        compiler_params=pltpu.CompilerParams(dimension_semantics=("parallel", "arbitrary")),
    )(a, b)


def _s5_seg1(y0, u, d):
    return jax.nn.gelu(y0 + d * u)


def _s5_seg2(z, t, b):
    return z * jax.nn.sigmoid(t + b)


def _s5_post_fwd(y0, seg2, d, wglu, bglu, name):
    s = y0.shape[0]
    ts = _pick(s, (256, 128))

    def fn(ins, outs, accs, scr):
        z = _s5_seg1(ins[0][...], ins[1][...], ins[2][...])
        outs[0][...] = _s5_seg2(z, _nn(z, ins[3][...]), ins[4][...])

    return _stage(name, fn, s // ts,
                  [(y0, _rows(ts, 512)), (seg2, _rows(ts, 512)), (d, _full(d.shape)), (wglu, _full(wglu.shape)),
                   (bglu, _full(bglu.shape))], [(_sds((s, 512)), _rows(ts, 512))])[0]


def _s5_post_bwd(y0, seg2, d, wglu, bglu, dy, name):
    s = y0.shape[0]
    ts = _pick(s, (256, 128))

    def fn(ins, outs, accs, scr):
        y0_r, u_r, d_r, w_r, b_r, dy_r = ins
        z, vjp1 = jax.vjp(_s5_seg1, y0_r[...], u_r[...], d_r[...])
        t = _nn(z, w_r[...])
        _, vjp2 = jax.vjp(_s5_seg2, z, t, b_r[...])
        dz, dt, db = vjp2(dy_r[...])
        accs[0][...] += _tn(z, dt)
        accs[1][...] += db
        dy0, du, dd = vjp1(dz + _nt(dt, w_r[...]))
        accs[2][...] += dd
        outs[0][...] = dy0
        outs[1][...] = du

    return _stage(name, fn, s // ts,
                  [(y0, _rows(ts, 512)), (seg2, _rows(ts, 512)), (d, _full(d.shape)), (wglu, _full(wglu.shape)),
                   (bglu, _full(bglu.shape)), (dy, _rows(ts, 512))],
                  [(_sds((s, 512)), _rows(ts, 512)), (_sds((s, 512)), _rows(ts, 512))],
                  accs=[_sds((512, 512)), _sds((1, 512)), _sds((1, 512))])


def _gate_a(y, g):
    return y * jax.nn.silu(g)


def _gate_m(o0, o1, o2, m0, m1, m2):
    return jax.nn.sigmoid(m0) * o0 + jax.nn.sigmoid(m1) * o1 + jax.nn.sigmoid(m2) * o2


def _assemble(ybuf, o_mla, o_fox, y_s5):
    for h in range(HEADS):
        ybuf[:, 64 * h:64 * (h + 1)] = o_mla[h]
        ybuf[:, 512 + 64 * h:512 + 64 * (h + 1)] = o_fox[h]
    ybuf[:, 1024:1536] = y_s5[...]


def _gate_fwd(o_mla, o_fox, y_s5, seg3, seg4, x, wo, wout, name):
    s = x.shape[0]
    ts = _pick(s, (256, 128))

    def fn(ins, outs, accs, scr):
        om, of, ys, g_r, m_r, x_r, wo_r, wout_r = ins
        ybuf = scr[0]
        _assemble(ybuf, om, of, ys)
        a = _gate_a(ybuf[...], g_r[...])
        o = [_nn(a[:, 512 * b:512 * (b + 1)], wo_r[512 * b:512 * (b + 1), :]) for b in range(3)]
        merged = _gate_m(o[0], o[1], o[2], m_r[:, 0:1024], m_r[:, 1024:2048], m_r[:, 2048:3072])
        outs[0][...] = x_r[...] + _nn(merged, wout_r[...])

    return _stage(name, fn, s // ts,
                  [(o_mla, _heads(ts, 64)), (o_fox, _heads(ts, 64)), (y_s5, _rows(ts, 512)), (seg3, _rows(ts, 1536)),
                   (seg4, _rows(ts, 3072)), (x, _rows(ts, D_MODEL)), (wo, _full(wo.shape)), (wout, _full(wout.shape))],
                  [(_sds((s, D_MODEL)), _rows(ts, D_MODEL))], scratch=[pltpu.VMEM((ts, 1536), F32)])[0]


def _gate_bwd(o_mla, o_fox, y_s5, seg3, seg4, wo, wout, dout, name):
    s = dout.shape[0]
    ts = _pick(s, (128,))

    def fn(ins, outs, accs, scr):
        om, of, ys, g_r, m_r, wo_r, wout_r, dout_r = ins
        do_mla, do_fox, dys, dg_r, dm_r = outs
        dwo, dwout = accs
        ybuf, dabuf = scr
        _assemble(ybuf, om, of, ys)
        a, vjp_a = jax.vjp(_gate_a, ybuf[...], g_r[...])
        o = [_nn(a[:, 512 * b:512 * (b + 1)], wo_r[512 * b:512 * (b + 1), :]) for b in range(3)]
        ms = [m_r[:, 1024 * b:1024 * (b + 1)] for b in range(3)]
        merged, vjp_m = jax.vjp(_gate_m, *o, *ms)
        dout_v = dout_r[...]
        dwout[...] += _tn(merged, dout_v)
        cts = vjp_m(_nt(dout_v, wout_r[...]))
        for b in range(3):
            dm_r[:, 1024 * b:1024 * (b + 1)] = cts[3 + b]
            dwo[512 * b:512 * (b + 1), :] += _tn(a[:, 512 * b:512 * (b + 1)], cts[b])
            dabuf[:, 512 * b:512 * (b + 1)] = _nt(cts[b], wo_r[512 * b:512 * (b + 1), :])
        dy, dg = vjp_a(dabuf[...])
        dg_r[...] = dg
        dys[...] = dy[:, 1024:1536]
        for h in range(HEADS):
            do_mla[h] = dy[:, 64 * h:64 * (h + 1)]
            do_fox[h] = dy[:, 512 + 64 * h:512 + 64 * (h + 1)]

    return _stage(name, fn, s // ts,
                  [(o_mla, _heads(ts, 64)), (o_fox, _heads(ts, 64)), (y_s5, _rows(ts, 512)), (seg3, _rows(ts, 1536)),
                   (seg4, _rows(ts, 3072)), (wo, _full(wo.shape)), (wout, _full(wout.shape)), (dout, _rows(ts, D_MODEL))],
                  [(_sds((HEADS, s, 64)), _heads(ts, 64)), (_sds((HEADS, s, 64)), _heads(ts, 64)), (_sds((s, 512)), _rows(ts, 512)),
                   (_sds((s, 1536)), _rows(ts, 1536)), (_sds((s, 3072)), _rows(ts, 3072))],
                  accs=[_sds(wo.shape), _sds(wout.shape)], scratch=[pltpu.VMEM((ts, 1536), F32), pltpu.VMEM((ts, 1536), F32)],
                  vmem_mb=56)


def _loss_head(y, target, name):
    s = y.shape[0]
    ts = _pick(s, (256, 128))

    def fn(ins, outs, accs, scr):
        e = ins[0][...] - ins[1][...]
        outs[0][...] = e / D_MODEL
        accs[0][...] += 0.5 * jnp.sum(jnp.sum(e * e, axis=1, keepdims=True) / D_MODEL, axis=0, keepdims=True)

    r = _stage(name, fn, s // ts, [(y, _rows(ts, D_MODEL)), (target, _rows(ts, D_MODEL))],
               [(_sds((s, D_MODEL)), _rows(ts, D_MODEL))], accs=[_sds((1, 1))])
    return r[1], r[0]


def _pad_w_in(w):
    z = lambda n: jnp.zeros((w.shape[0], n), w.dtype)
    return jnp.concatenate([w[:, 0:384], w[:, 1952:1960], z(56), w[:, 384:416], z(32), w[:, 416:1952], w[:, 1960:IN_WIDTH]], axis=1)


def _unpad_w_in(d):
    return jnp.concatenate([d[:, 0:384], d[:, 448:480], d[:, 512:2048], d[:, 384:392], d[:, 2048:PAD_IN]], axis=1)


def _pad_wq(w):
    w = w.reshape(MLA_Q_RANK, HEADS, MLA_QK)
    return jnp.pad(w, ((0, 0), (0, 0), (0, LANES - MLA_QK))).reshape(MLA_Q_RANK, HEADS * LANES)


def _unpad_wq(d):
    return d.reshape(MLA_Q_RANK, HEADS, LANES)[:, :, :MLA_QK].reshape(MLA_Q_RANK, HEADS * MLA_QK)


def _pad_wkv(w):
    w = w.reshape(MLA_KV_RANK, HEADS, MLA_NOPE + MLA_V)
    k = jnp.pad(w[:, :, :MLA_NOPE], ((0, 0), (0, 0), (0, LANES - MLA_NOPE))).reshape(MLA_KV_RANK, HEADS * LANES)
    return jnp.concatenate([k, w[:, :, MLA_NOPE:].reshape(MLA_KV_RANK, HEADS * MLA_V)], axis=1)


def _unpad_wkv(d):
    k = d[:, :HEADS * LANES].reshape(MLA_KV_RANK, HEADS, LANES)[:, :, :MLA_NOPE]
    v = d[:, HEADS * LANES:].reshape(MLA_KV_RANK, HEADS, MLA_V)
    return jnp.concatenate([k, v], axis=2).reshape(MLA_KV_RANK, HEADS * (MLA_NOPE + MLA_V))


def _pad_lanes(v, n=LANES):
    return jnp.pad(v, (0, n - v.shape[0])).reshape(1, n)


def _super_blocks(b):
    _, r, c = b.shape
    per = S5_GROUPS // S5_SUPER
    b = b.reshape(S5_SUPER, per, r, c)
    eye = jnp.eye(per, dtype=b.dtype)
    return (b[:, :, :, None, :] * eye[None, :, None, :, None]).reshape(S5_SUPER, per * r, per * c)


def _layer_params(l, w, small):
    p = {k: small[k][l] for k in small}
    q = {}
    q["norm_g"] = p["norm_g"].reshape(1, D_MODEL)
    q["mla_q_a_norm"] = p["mla_q_a_norm"].reshape(1, 256)
    q["mla_kv_a_norm"] = p["mla_kv_a_norm"].reshape(1, 128)
    q["mla_q_norm"] = _pad_lanes(p["mla_q_norm"])
    q["mla_k_norm"] = _pad_lanes(p["mla_k_norm"])
    q["fox_b_f"] = _pad_lanes(p["fox_b_f"])
    q["fox_q_norm"] = p["fox_q_norm"].reshape(1, 64)
    q["fox_k_norm"] = p["fox_k_norm"].reshape(1, 64)
    q["s5_d"] = p["s5_d"].reshape(1, 512)
    q["s5_b_glu"] = p["s5_b_glu"].reshape(1, 512)
    rep = lambda z: jnp.repeat(z, S5_GROUP, axis=0)
    q["lr16"], q["li16"] = rep(p["s5_lambda_re"]), rep(p["s5_lambda_im"])
    q["ldt16"] = rep(jnp.broadcast_to(p["s5_log_dt"][:, None], (S5_GROUPS, S5_STATE)))
    q["br2"] = p["s5_b_re"].transpose(0, 2, 1).reshape(512, 64)
    q["bi2"] = p["s5_b_im"].transpose(0, 2, 1).reshape(512, 64)
    q["c_re"], q["c_im"] = p["s5_c_re"], p["s5_c_im"]
    win = _pad_w_in(w["w_in"][l])
    q["w_seg"] = [win[:, SEG_OFF[k]:SEG_OFF[k] + SEG_W[k]] for k in range(5)]
    q["wq"] = _pad_wq(w["mla_w_q_up"][l])
    q["wkv"] = _pad_wkv(w["mla_w_kv_up"][l])
    q["wglu"] = w["s5_w_glu"][l]
    q["wo"] = w["w_branch_out"][l]
    q["wout"] = w["w_out"][l]
    return q


def _layer_fwd(l, x, tabs, q):
    n = lambda s: f"l{l}_{s}"
    sv = {"x": x}
    h = _norm_fwd(x, q["norm_g"], n("norm_fwd"))
    sv["h"] = h
    seg = [_mm(h, q["w_seg"][k], "nn", n(f"proj{k}")) for k in range(5)]
    sv["seg"] = seg
    mq, mk, mv = _mla_prep_fwd(seg[0], tabs, q, q["wq"], q["wkv"], n("mla_prep_fwd"))
    o_mla, lse_mla = _attn_fwd(mq, mk, mv, 1.0 / math.sqrt(MLA_QK), True, n("mla_attn_fwd"))
    sv.update(mq=mq, mk=mk, mv=mv, o_mla=o_mla, lse_mla=lse_mla)
    fq, fk, fv, c, ct = _fox_prep_fwd(seg[0], seg[1], q["fox_b_f"], q["fox_q_norm"], q["fox_k_norm"], n("fox_prep_fwd"))
    o_fox, lse_fox = _attn_fwd(fq, fk, fv, 1.0 / math.sqrt(FOX_DIM), False, n("fox_attn_fwd"), c=c, ct=ct)
    sv.update(fq=fq, fk=fk, fv=fv, c=c, ct=ct, o_fox=o_fox, lse_fox=lse_fox)
    a_re16, a_im16, bb_re, bb_im = _s5_params_fwd(q["lr16"], q["li16"], q["ldt16"], q["br2"], q["bi2"], n("s5_params_fwd"))
    a_re = a_re16.reshape(S5_GROUPS, S5_GROUP, S5_STATE)[:, 0, :].reshape(1, S5_LANES)
    a_im = a_im16.reshape(S5_GROUPS, S5_GROUP, S5_STATE)[:, 0, :].reshape(1, S5_LANES)
    wb_re = _super_blocks(bb_re.reshape(S5_GROUPS, S5_GROUP, S5_STATE)).astype(BF16)
    wb_im = _super_blocks(bb_im.reshape(S5_GROUPS, S5_GROUP, S5_STATE)).astype(BF16)
    wc_re = _super_blocks(q["c_re"].transpose(0, 2, 1)).astype(BF16)
    wc_im = _super_blocks(-q["c_im"].transpose(0, 2, 1)).astype(BF16)
    bu_re = _s5_mm(seg[2], wb_re, "nn", n("s5_bu_re"))
    bu_im = _s5_mm(seg[2], wb_im, "nn", n("s5_bu_im"))
    x_re, x_im = _scan(bu_re, bu_im, a_re, a_im, False, n("s5_scan_fwd"))
    y0 = _s5_mm(x_im, wc_im, "nn", n("s5_y_im"), acc=_s5_mm(x_re, wc_re, "nn", n("s5_y_re")))
    y_s5 = _s5_post_fwd(y0, seg[2], q["s5_d"], q["wglu"], q["s5_b_glu"], n("s5_post_fwd"))
    sv.update(a_re=a_re, a_im=a_im, wb_re=wb_re, wb_im=wb_im, wc_re=wc_re, wc_im=wc_im, x_re=x_re, x_im=x_im, y0=y0, y_s5=y_s5)
    out = _gate_fwd(o_mla, o_fox, y_s5, seg[3], seg[4], x, q["wo"], q["wout"], n("gate_fwd"))
    return out, sv


def _layer_bwd(l, dout, tabs, q, sv):
    n = lambda s: f"l{l}_{s}"
    seg = sv["seg"]
    g = {}
    (do_mla, do_fox, dy_s5, dseg3, dseg4, g["wo"], g["wout"]) = _gate_bwd(
        sv["o_mla"], sv["o_fox"], sv["y_s5"], seg[3], seg[4], q["wo"], q["wout"], dout, n("gate_bwd"))
    dy0, du_a, g["wglu"], g["s5_b_glu"], g["s5_d"] = _s5_post_bwd(sv["y0"], seg[2], q["s5_d"], q["wglu"], q["s5_b_glu"], dy_s5,
                                                                 n("s5_post_bwd"))
    dx_re = _s5_mm(dy0, sv["wc_re"], "nt", n("s5_dx_re"))
    dx_im = _s5_mm(dy0, sv["wc_im"], "nt", n("s5_dx_im"))
    g["s5_c_re"] = _s5_group_tn(dy0, sv["x_re"], n("s5_dc_re")).reshape(S5_GROUPS, S5_GROUP, S5_STATE)
    g["s5_c_im"] = -_s5_group_tn(dy0, sv["x_im"], n("s5_dc_im")).reshape(S5_GROUPS, S5_GROUP, S5_STATE)
    g_re, g_im, da_re, da_im = _scan(dx_re, dx_im, sv["a_re"], -sv["a_im"], True, n("s5_scan_bwd"), x_re=sv["x_re"], x_im=sv["x_im"])
    dseg2 = _s5_mm(g_im, sv["wb_im"], "nt", n("s5_du_im"), acc=_s5_mm(g_re, sv["wb_re"], "nt", n("s5_du_re"), acc=du_a))
    dbb_re = _s5_group_tn(seg[2], g_re, n("s5_dbb_re"))
    dbb_im = _s5_group_tn(seg[2], g_im, n("s5_dbb_im"))
    first = (jnp.arange(512) % S5_GROUP == 0).astype(F32)[:, None]
    da_re16 = jnp.repeat(da_re.reshape(S5_GROUPS, S5_STATE), S5_GROUP, axis=0) * first
    da_im16 = jnp.repeat(da_im.reshape(S5_GROUPS, S5_STATE), S5_GROUP, axis=0) * first
    dlr, dli, dldt, dbr2, dbi2 = _s5_params_bwd(q["lr16"], q["li16"], q["ldt16"], q["br2"], q["bi2"], da_re16, da_im16, dbb_re,
                                               dbb_im, n("s5_params_bwd"))
    g["s5_lambda_re"], g["s5_lambda_im"], g["s5_log_dt"] = dlr, dli, dldt.reshape(S5_GROUPS)
    g["s5_b_re"] = dbr2.reshape(S5_GROUPS, S5_GROUP, S5_STATE).transpose(0, 2, 1)
    g["s5_b_im"] = dbi2.reshape(S5_GROUPS, S5_GROUP, S5_STATE).transpose(0, 2, 1)
    dl_fox = _attn_delta(sv["fq"], sv["fk"], sv["fv"], do_fox, sv["lse_fox"], 1.0 / math.sqrt(FOX_DIM), False,
                         n("fox_attn_delta"), c=sv["c"], ct=sv["ct"])
    dfq, dfk, dfv, dck = _attn_bwd(sv["fq"], sv["fk"], sv["fv"], do_fox, sv["lse_fox"], dl_fox, 1.0 / math.sqrt(FOX_DIM), False,
                                   n("fox_attn_bwd"), c=sv["c"], ct=sv["ct"])
    dseg1, dff, g["fox_q_norm"], g["fox_k_norm"], dbf = _fox_prep_bwd(seg[0], seg[1], q["fox_b_f"], q["fox_q_norm"], q["fox_k_norm"],
                                                                      dfq, dfk, dfv, dck, n("fox_prep_bwd"))
    g["fox_b_f"] = dbf[0, :HEADS]
    dl_mla = _attn_delta(sv["mq"], sv["mk"], sv["mv"], do_mla, sv["lse_mla"], 1.0 / math.sqrt(MLA_QK), True, n("mla_attn_delta"))
    dmq, dmk, dmv = _attn_bwd(sv["mq"], sv["mk"], sv["mv"], do_mla, sv["lse_mla"], dl_mla, 1.0 / math.sqrt(MLA_QK), True,
                              n("mla_attn_bwd"))
    dseg0, dqan, dkvan, dqn, dkn, g["wq"], g["wkv"] = _mla_prep_bwd(seg[0], tabs, q, q["wq"], q["wkv"], dmq, dmk, dmv, dff,
                                                                   n("mla_prep_bwd"))
    g["mla_q_a_norm"], g["mla_kv_a_norm"] = dqan, dkvan
    g["mla_q_norm"], g["mla_k_norm"] = dqn[0, :MLA_QK], dkn[0, :MLA_QK]
    dsegs = [dseg0, dseg1, dseg2, dseg3, dseg4]
    dh = None
    for k in range(5):
        dh = _mm(dsegs[k], q["w_seg"][k], "nt", n(f"dh{k}"), acc=dh)
    g["w_in"] = jnp.concatenate([_mm(sv["h"], dsegs[k], "tn", n(f"dwin{k}")) for k in range(5)], axis=1)
    dx, g["norm_g"] = _norm_bwd(sv["x"], q["norm_g"], dh, dout, n("norm_bwd"))
    return dx, g


MESH = pl.DeviceIdType.MESH
ANY = pl.BlockSpec(memory_space=pl.ANY)


def _all_gather(block, name):
    r, cc = block.shape

    def body(x_ref, out_ref, send_sems, recv_sems, local_sem):
        x, y, c = lax.axis_index("x"), lax.axis_index("y"), lax.axis_index("c")
        me, sibling = (x, y, c), (x, y, 1 - c)
        chips = [(1 - x, y), (x, 1 - y), (1 - x, 1 - y)]

        def slot(px, py, pc):
            return out_ref.at[4 * px + 2 * py + pc]

        def copy(k, blk, to, src=None):
            return pltpu.make_async_remote_copy(src_ref=slot(*blk) if src is None else src, dst_ref=slot(*blk),
                                                send_sem=send_sems.at[k], recv_sem=recv_sems.at[k], device_id=to,
                                                device_id_type=MESH)

        mine = pltpu.make_async_copy(x_ref, slot(*me), local_sem)
        mine.start()
        first = [copy(0, me, sibling, src=x_ref)]
        first += [copy(1 + j, me, (*chip, c), src=x_ref) for j, chip in enumerate(chips)]
        for cp in first:
            cp.start()
        passed = [copy(4 + j, (*chip, c), sibling) for j, chip in enumerate(chips)]
        for j, chip in enumerate(chips):
            copy(1 + j, (*chip, c), me).wait_recv()
            passed[j].start()
        copy(0, sibling, me).wait_recv()
        for j, chip in enumerate(chips):
            copy(4 + j, (*chip, 1 - c), me).wait_recv()
        for cp in first + passed:
            cp.wait_send()
        mine.wait()

    return pl.pallas_call(
        body, name=name, out_shape=jax.ShapeDtypeStruct((N_DEV, r, cc), block.dtype), in_specs=[ANY], out_specs=ANY,
        scratch_shapes=[pltpu.SemaphoreType.DMA((7,)), pltpu.SemaphoreType.DMA((7,)), pltpu.SemaphoreType.DMA],
    )(block)


def _exchange_sibling(parts, name):
    _, r, cc = parts.shape

    def body(p_ref, got_ref, send_sems, recv_sems):
        x, y, c = lax.axis_index("x"), lax.axis_index("y"), lax.axis_index("c")
        sibling = (x, y, 1 - c)
        copies = []
        for j in range(4):
            copies.append(pltpu.make_async_remote_copy(src_ref=p_ref.at[2 * j + (1 - c)], dst_ref=got_ref.at[j],
                                                       send_sem=send_sems.at[j], recv_sem=recv_sems.at[j], device_id=sibling,
                                                       device_id_type=MESH))
        for cp in copies:
            cp.start()
        for cp in copies:
            cp.wait()

    return pl.pallas_call(
        body, name=name, out_shape=jax.ShapeDtypeStruct((4, r, cc), parts.dtype), in_specs=[ANY], out_specs=ANY,
        scratch_shapes=[pltpu.SemaphoreType.DMA((4,)), pltpu.SemaphoreType.DMA((4,))],
    )(parts)


def _exchange_chips(parts, name):
    _, r, cc = parts.shape

    def body(p_ref, got_ref, send_sems, recv_sems, local_sem):
        x, y, c = lax.axis_index("x"), lax.axis_index("y"), lax.axis_index("c")
        my_chip = 2 * x + y
        chips = [(1 - x, y), (x, 1 - y), (1 - x, 1 - y)]
        mine = pltpu.make_async_copy(p_ref.at[my_chip], got_ref.at[my_chip], local_sem)
        mine.start()
        copies = []
        for k, (px, py) in enumerate(chips):
            copies.append(pltpu.make_async_remote_copy(src_ref=p_ref.at[2 * px + py], dst_ref=got_ref.at[my_chip],
                                                       send_sem=send_sems.at[k], recv_sem=recv_sems.at[k], device_id=(px, py, c),
                                                       device_id_type=MESH))
        for cp in copies:
            cp.start()
        for k, (px, py) in enumerate(chips):
            pltpu.make_async_remote_copy(src_ref=p_ref.at[my_chip], dst_ref=got_ref.at[2 * px + py], send_sem=send_sems.at[k],
                                         recv_sem=recv_sems.at[k], device_id=(px, py, c), device_id_type=MESH).wait_recv()
        for cp in copies:
            cp.wait_send()
        mine.wait()

    return pl.pallas_call(
        body, name=name, out_shape=jax.ShapeDtypeStruct((4, r, cc), parts.dtype), in_specs=[ANY], out_specs=ANY,
        scratch_shapes=[pltpu.SemaphoreType.DMA((3,)), pltpu.SemaphoreType.DMA((3,)), pltpu.SemaphoreType.DMA],
    )(parts)


def _add_sibling(parts, got, name):
    _, r, cc = parts.shape
    tr = _pick(r, (512, 256, 128, 64, 32, 16, 8))
    c = lax.axis_index("c")

    def body(c_ref, p_ref, g_ref, o_ref):
        o_ref[...] = p_ref[...] + g_ref[...]

    return pl.pallas_call(
        body, name=name, out_shape=jax.ShapeDtypeStruct((4, r, cc), F32),
        grid_spec=pltpu.PrefetchScalarGridSpec(
            num_scalar_prefetch=1, grid=(4, r // tr),
            in_specs=[pl.BlockSpec((1, tr, cc), lambda j, i, cr: (2 * j + cr[0], i, 0)),
                      pl.BlockSpec((1, tr, cc), lambda j, i, cr: (j, i, 0))],
            out_specs=pl.BlockSpec((1, tr, cc), lambda j, i, cr: (j, i, 0))),
    )(c.reshape(1).astype(jnp.int32), parts, got)


def _sum_leading(parts, name):
    k, r, cc = parts.shape
    tr = _pick(r, (512, 256, 128, 64, 32, 16, 8))

    def body(p_ref, o_ref):
        acc = p_ref[0]
        for j in range(1, k):
            acc = acc + p_ref[j]
        o_ref[...] = acc

    return pl.pallas_call(
        body, name=name, out_shape=jax.ShapeDtypeStruct((r, cc), F32), grid=(r // tr,),
        in_specs=[pl.BlockSpec((k, tr, cc), lambda i: (0, i, 0))], out_specs=pl.BlockSpec((tr, cc), lambda i: (i, 0)),
    )(parts)


def _adamw(w, g, m, v, name):
    r, cc = w.shape
    tr = _pick(r, (256, 128, 64, 32, 16, 8))

    def body(w_ref, g_ref, m_ref, v_ref, d_ref, nm_ref, nv_ref):
        gg = g_ref[...]
        nm = ADAM_B1 * m_ref[...] + (1.0 - ADAM_B1) * gg
        nv = ADAM_B2 * v_ref[...] + (1.0 - ADAM_B2) * jnp.square(gg)
        m_hat = nm / (1.0 - ADAM_B1 ** ADAM_STEP)
        v_hat = nv / (1.0 - ADAM_B2 ** ADAM_STEP)
        d_ref[...] = -ADAM_LR * (m_hat / (jnp.sqrt(v_hat) + ADAM_EPS) + ADAM_WD * w_ref[...])
        nm_ref[...] = nm
        nv_ref[...] = nv

    spec = pl.BlockSpec((tr, cc), lambda i: (i, 0))
    return pl.pallas_call(
        body, name=name, grid=(r // tr,), in_specs=[spec] * 4, out_specs=[spec] * 3,
        out_shape=[jax.ShapeDtypeStruct((r, cc), F32)] * 3,
    )(w, g, m, v)


def _pack_rows(flat, lanes, row_mult):
    n = flat.shape[-1]
    rows = -(-n // lanes)
    rows = -(-rows // row_mult) * row_mult
    pad = rows * lanes - n
    if pad:
        flat = jnp.pad(flat, [(0, 0)] * (flat.ndim - 1) + [(0, pad)])
    return flat.reshape(flat.shape[:-1] + (rows, lanes))


def _rope_tables(positions):
    inv = 1.0 / (ROPE_THETA ** (jnp.arange(0, MLA_ROPE, 2, dtype=F32) / MLA_ROPE))
    ang = positions.astype(F32)[:, None] * inv
    cos, sin = jnp.cos(ang), jnp.sin(ang)
    s = positions.shape[0]
    z = lambda n: jnp.zeros((s, n), F32)
    c = jnp.concatenate([jnp.ones((s, 64), F32), cos, cos, z(32)], axis=1)
    sa = jnp.concatenate([z(64), -sin, z(48)], axis=1)
    sb = jnp.concatenate([z(80), sin, z(32)], axis=1)
    return c, sa, sb


def _device_grads(x, positions, target, w, small):
    tabs = _rope_tables(positions)
    qs = [_layer_params(l, w, small) for l in range(DEPTH)]
    h = x
    saves = []
    for l in range(DEPTH):
        h, sv = _layer_fwd(l, h, tabs, qs[l])
        saves.append(sv)
    loss, d = _loss_head(h, target, "loss_head")
    grads = [None] * DEPTH
    for l in reversed(range(DEPTH)):
        d, grads[l] = _layer_bwd(l, d, tabs, qs[l], saves[l])
    return loss[0, 0], d, grads


def kernel(x, positions, norm_g, w_in, mla_q_a_norm, mla_w_q_up, mla_kv_a_norm, mla_w_kv_up, mla_q_norm, mla_k_norm, fox_b_f, fox_q_norm, fox_k_norm, s5_lambda_re, s5_lambda_im, s5_log_dt, s5_b_re, s5_b_im, s5_c_re, s5_c_im, s5_d, s5_w_glu, s5_b_glu, w_branch_out, w_out, loss_target, m_norm_g, m_w_in, m_mla_q_a_norm, m_mla_w_q_up, m_mla_kv_a_norm, m_mla_w_kv_up, m_mla_q_norm, m_mla_k_norm, m_fox_b_f, m_fox_q_norm, m_fox_k_norm, m_s5_lambda_re, m_s5_lambda_im, m_s5_log_dt, m_s5_b_re, m_s5_b_im, m_s5_c_re, m_s5_c_im, m_s5_d, m_s5_w_glu, m_s5_b_glu, m_w_branch_out, m_w_out, v_norm_g, v_w_in, v_mla_q_a_norm, v_mla_w_q_up, v_mla_kv_a_norm, v_mla_w_kv_up, v_mla_q_norm, v_mla_k_norm, v_fox_b_f, v_fox_q_norm, v_fox_k_norm, v_s5_lambda_re, v_s5_lambda_im, v_s5_log_dt, v_s5_b_re, v_s5_b_im, v_s5_c_re, v_s5_c_im, v_s5_d, v_s5_w_glu, v_s5_b_glu, v_w_branch_out, v_w_out):
    env = dict(locals())
    wts = {k: env[k] for k in WEIGHTS}
    mom = {k: env["m_" + k] for k in WEIGHTS}
    var = {k: env["v_" + k] for k in WEIGHTS}

    flat = jnp.concatenate([wts[k].reshape(-1) for k in SHARDED]).astype(BF16)
    n_sh = flat.shape[0]
    gathered = _all_gather(_pack_rows(flat, PACK_LANES, 16), "gather_weights").reshape(N_DEV, -1)[:, :n_sh]
    full = {}
    off = 0
    for k in SHARDED:
        shp = wts[k].shape
        cnt = int(np.prod(shp))
        blk = gathered[:, off:off + cnt].reshape((N_DEV,) + shp)
        off += cnt
        if k in COL_SHARDED:
            full[k] = blk.transpose(1, 2, 0, 3).reshape(shp[0], shp[1], N_DEV * shp[2])
        else:
            full[k] = blk.transpose(1, 0, 2, 3).reshape(shp[0], N_DEV * shp[1], shp[2])

    small = {k: wts[k] for k in SMALL}
    loss, dx, grads = _device_grads(x[0], positions[0], loss_target[0], full, small)
    loss = lax.psum(loss, ("x", "y", "c"))

    big = {"w_in": [_unpad_w_in(g["w_in"]) for g in grads], "mla_w_q_up": [_unpad_wq(g["wq"]) for g in grads],
           "mla_w_kv_up": [_unpad_wkv(g["wkv"]) for g in grads], "s5_w_glu": [g["wglu"] for g in grads],
           "w_branch_out": [g["wo"] for g in grads], "w_out": [g["wout"] for g in grads]}
    pieces = []
    for k in SHARDED:
        gk = jnp.stack(big[k])
        lshape = wts[k].shape
        if k in COL_SHARDED:
            gk = gk.reshape(lshape[0], lshape[1], N_DEV, lshape[2]).transpose(2, 0, 1, 3)
        else:
            gk = gk.reshape(lshape[0], N_DEV, lshape[1], lshape[2]).transpose(1, 0, 2, 3)
        pieces.append(gk.reshape(N_DEV, -1))
    parts = _pack_rows(jnp.concatenate(pieces, axis=1), PACK_LANES, 512)
    got = _exchange_sibling(parts, "reduce_sibling")
    chip_sums = _add_sibling(parts, got, "reduce_sibling_add")
    mine = _sum_leading(_exchange_chips(chip_sums, "reduce_chips"), "reduce_chips_sum").reshape(-1)[:n_sh]
    grad_out, delta_out, m_out, v_out = {}, {}, {}, {}
    off = 0
    for k in SHARDED:
        shp = wts[k].shape
        cnt = int(np.prod(shp))
        gk = mine[off:off + cnt].reshape(shp)
        off += cnt
        grad_out[k] = gk
        two_d = (shp[0] * shp[1], shp[2])
        d, nm, nv = _adamw(wts[k].reshape(two_d), gk.reshape(two_d), mom[k].reshape(two_d), var[k].reshape(two_d), f"adamw_{k}")
        delta_out[k], m_out[k], v_out[k] = d.reshape(shp), nm.reshape(shp), nv.reshape(shp)

    sm = {k: jnp.stack([g[k] for g in grads]).reshape(wts[k].shape) for k in SMALL}
    flat_s = jnp.concatenate([sm[k].reshape(-1) for k in SMALL])
    n_sm = flat_s.shape[0]
    g_small = _sum_leading(_all_gather(_pack_rows(flat_s, LANES, 256), "gather_small_grads"), "sum_small_grads")
    pk = lambda d: _pack_rows(jnp.concatenate([d[k].reshape(-1) for k in SMALL]), LANES, 256)
    d_s, m_s, v_s = _adamw(pk(wts), g_small, pk(mom), pk(var), "adamw_small")
    g_small, d_s, m_s, v_s = (z.reshape(-1)[:n_sm] for z in (g_small, d_s, m_s, v_s))
    off = 0
    for k in SMALL:
        shp = wts[k].shape
        cnt = int(np.prod(shp))
        grad_out[k], delta_out[k], m_out[k], v_out[k] = (z[off:off + cnt].reshape(shp) for z in (g_small, d_s, m_s, v_s))
        off += cnt

    return (loss, dx[None], *[grad_out[k] for k in WEIGHTS], *[delta_out[k] for k in WEIGHTS],
            *[m_out[k] for k in WEIGHTS], *[v_out[k] for k in WEIGHTS])
```

```python
import functools
import math

import jax
import jax.numpy as jnp
import numpy as np
from jax import lax
from jax.experimental import pallas as pl
from jax.experimental.pallas import tpu as pltpu

F32 = jnp.float32
BF16 = jnp.bfloat16

D_MODEL = 1024
DEPTH = 2
CHUNK = 64
EPS = 1e-6
HEADS = 8
MLA_NOPE, MLA_ROPE, MLA_V = 64, 32, 64
MLA_Q_RANK, MLA_KV_RANK = 256, 128
MLA_QK = MLA_NOPE + MLA_ROPE
ROPE_THETA = 10000.0
FOX_DIM = 64
S5_WIDTH, S5_GROUP, S5_GROUPS, S5_STATE = 512, 16, 32, 64
S5_LANES = S5_GROUPS * S5_STATE
IN_WIDTH = 7080
N_DEV = 8
LANES = 128
SUBLANES = 8

ADAM_LR, ADAM_B1, ADAM_B2, ADAM_EPS, ADAM_WD, ADAM_STEP = 0.001, 0.9, 0.999, 1e-08, 0.01, 10

SEG_W = (512, 1536, 512, 1536, 3072)
SEG_OFF = (0, 512, 2048, 2560, 4096)
PAD_IN = 7168
NEG = -1e30

SHARDED = ("w_in", "mla_w_q_up", "mla_w_kv_up", "s5_w_glu", "w_branch_out", "w_out")
COL_SHARDED = ("w_in", "mla_w_q_up", "mla_w_kv_up")
SMALL = ("norm_g", "mla_q_a_norm", "mla_kv_a_norm", "mla_q_norm", "mla_k_norm", "fox_b_f", "fox_q_norm", "fox_k_norm",
         "s5_lambda_re", "s5_lambda_im", "s5_log_dt", "s5_b_re", "s5_b_im", "s5_c_re", "s5_c_im", "s5_d", "s5_b_glu")
WEIGHTS = ("norm_g", "w_in", "mla_q_a_norm", "mla_w_q_up", "mla_kv_a_norm", "mla_w_kv_up", "mla_q_norm", "mla_k_norm",
           "fox_b_f", "fox_q_norm", "fox_k_norm", "s5_lambda_re", "s5_lambda_im", "s5_log_dt", "s5_b_re", "s5_b_im",
           "s5_c_re", "s5_c_im", "s5_d", "s5_w_glu", "s5_b_glu", "w_branch_out", "w_out")
PACK_LANES = 512


def _pick(n, cands):
    for c in cands:
        if n % c == 0:
            return c
    return n


def _vmem(mb):
    return pltpu.CompilerParams(vmem_limit_bytes=mb * 1024 * 1024)


def _dot(a, b, dims):
    return lax.dot_general(a.astype(BF16), b.astype(BF16), (dims, ((), ())), preferred_element_type=F32)


def _nn(a, b):
    return _dot(a, b, ((1,), (0,)))


def _nt(a, b):
    return _dot(a, b, ((1,), (1,)))


def _tn(a, b):
    return _dot(a, b, ((0,), (0,)))


def _rms(x, g, n=None):
    n = x.shape[-1] if n is None else n
    return x * lax.rsqrt(jnp.sum(x * x, axis=-1, keepdims=True) / n + EPS) * g


def _rope(t, c, sa, sb):
    return t * c + pltpu.roll(t, LANES - 16, 1) * sa + pltpu.roll(t, 16, 1) * sb


def _rope_t(d, c, sa, sb):
    return d * c + pltpu.roll(d * sa, 16, 1) + pltpu.roll(d * sb, LANES - 16, 1)


def _mm(a, b, mode, name, acc=None, scale=None):
    if mode == "tn":
        kd, m = a.shape
    else:
        m, kd = a.shape
    n = b.shape[0] if mode == "nt" else b.shape[1]
    tm, tn, tk = _pick(m, (1024, 512, 256, 128)), _pick(n, (1024, 512, 256, 128)), _pick(kd, (512, 256, 128))
    nk = kd // tk
    if mode == "tn":
        a_spec = pl.BlockSpec((tk, tm), lambda i, j, k: (k, i))
    else:
        a_spec = pl.BlockSpec((tm, tk), lambda i, j, k: (i, k))
    if mode == "nt":
        b_spec = pl.BlockSpec((tn, tk), lambda i, j, k: (j, k))
    else:
        b_spec = pl.BlockSpec((tk, tn), lambda i, j, k: (k, j))
    dims = {"nn": ((1,), (0,)), "nt": ((1,), (1,)), "tn": ((0,), (0,))}[mode]
    o_spec = pl.BlockSpec((tm, tn), lambda i, j, k: (i, j))
    has_acc = acc is not None

    def body(*refs):
        if has_acc:
            a_ref, b_ref, c_ref, o_ref, acc_ref = refs
        else:
            a_ref, b_ref, o_ref, acc_ref = refs
        k = pl.program_id(2)

        @pl.when(k == 0)
        def _():
            acc_ref[...] = c_ref[...] if has_acc else jnp.zeros(acc_ref.shape, F32)

        acc_ref[...] += _dot(a_ref[...], b_ref[...], dims)

        @pl.when(k == nk - 1)
        def _():
            o_ref[...] = acc_ref[...]

    ins = [a, b] + ([acc] if has_acc else [])
    in_specs = [a_spec, b_spec] + ([o_spec] if has_acc else [])
    return pl.pallas_call(
        body, name=name, grid=(m // tm, n // tn, nk), in_specs=in_specs, out_specs=o_spec,
        out_shape=jax.ShapeDtypeStruct((m, n), F32), scratch_shapes=[pltpu.VMEM((tm, tn), F32)],
        compiler_params=pltpu.CompilerParams(dimension_semantics=("parallel", "parallel", "arbitrary"),
                                             vmem_limit_bytes=48 * 1024 * 1024),
    )(*ins)


def _stage(name, fn, n_steps, ins, outs, accs=(), scratch=(), vmem_mb=48):
    n_in, n_out, n_acc = len(ins), len(outs), len(accs)

    def body(*refs):
        in_refs = refs[:n_in]
        out_refs = refs[n_in:n_in + n_out]
        acc_refs = refs[n_in + n_out:n_in + n_out + n_acc]
        scr = refs[n_in + n_out + n_acc:]
        if n_acc:
            @pl.when(pl.program_id(0) == 0)
            def _():
                for r in acc_refs:
                    r[...] = jnp.zeros(r.shape, r.dtype)
        fn(in_refs, out_refs, acc_refs, scr)

    acc_specs = [pl.BlockSpec(a.shape, functools.partial(lambda i, nd: (0,) * nd, nd=len(a.shape))) for a in accs]
    res = pl.pallas_call(
        body, name=name, grid=(n_steps,),
        in_specs=[s for _, s in ins], out_specs=[s for _, s in outs] + acc_specs,
        out_shape=[s for s, _ in outs] + list(accs), scratch_shapes=list(scratch),
        compiler_params=pltpu.CompilerParams(dimension_semantics=("arbitrary",),
                                             vmem_limit_bytes=vmem_mb * 1024 * 1024),
    )(*[a for a, _ in ins])
    return res


def _rows(ts, w, j=0):
    return pl.BlockSpec((ts, w), lambda i: (i, j))


def _rows_rev(ts, w, n, j=0):
    return pl.BlockSpec((ts, w), lambda i: (n - 1 - i, j))


def _heads(ts, d):
    return pl.BlockSpec((HEADS, ts, d), lambda i: (0, i, 0))


def _heads_rev(ts, d, n):
    return pl.BlockSpec((HEADS, ts, d), lambda i: (0, n - 1 - i, 0))


def _full(shape):
    nd = len(shape)
    return pl.BlockSpec(tuple(shape), lambda i: (0,) * nd)


def _sds(shape, dtype=F32):
    return jax.ShapeDtypeStruct(tuple(shape), dtype)


def _norm_fwd(x, g, name):
    s = x.shape[0]
    ts = _pick(s, (256, 128))

    def fn(ins, outs, accs, scr):
        outs[0][...] = _rms(ins[0][...], ins[1][...]).astype(BF16)

    return _stage(name, fn, s // ts, [(x, _rows(ts, D_MODEL)), (g, _full(g.shape))],
                  [(_sds((s, D_MODEL), BF16), _rows(ts, D_MODEL))])[0]


def _norm_bwd(x, g, dh, dres, name):
    s = x.shape[0]
    ts = _pick(s, (256, 128))

    def fn(ins, outs, accs, scr):
        _, vjp = jax.vjp(_rms, ins[0][...], ins[1][...])
        dx, dg = vjp(ins[2][...])
        outs[0][...] = dx + ins[3][...]
        accs[0][...] += dg

    r = _stage(name, fn, s // ts,
               [(x, _rows(ts, D_MODEL)), (g, _full(g.shape)), (dh, _rows(ts, D_MODEL)), (dres, _rows(ts, D_MODEL))],
               [(_sds((s, D_MODEL)), _rows(ts, D_MODEL))], accs=[_sds((1, D_MODEL))])
    return r[0], r[1]


def _mla_q(qraw, c, sa, sb, qn):
    return _rms(_rope(qraw, c, sa, sb), qn, MLA_QK)


def _mla_prep_fwd(seg0, tabs, p, wq, wkv, name):
    s = seg0.shape[0]
    ts = _pick(s, (256, 128))

    def fn(ins, outs, accs, scr):
        blk, cos, sa, sb, qan, kvan, qn, kn, wq_r, wkv_r = ins
        b = blk[...]
        cq, ckv, kt = b[:, :256], b[:, 256:384], b[:, 384:512]
        lane = lax.broadcasted_iota(jnp.int32, kt.shape, 1)
        kpe = jnp.where(lane >= 64, kt, 0.0)
        q_raw = _nn(_rms(cq, qan[...]), wq_r[...])
        kv_raw = _nn(_rms(ckv, kvan[...]), wkv_r[...])
        c, a, bb = cos[...], sa[...], sb[...]
        for h in range(HEADS):
            outs[0][h] = _mla_q(q_raw[:, LANES * h:LANES * (h + 1)], c, a, bb, qn[...]).astype(BF16)
            outs[1][h] = _mla_q(kv_raw[:, LANES * h:LANES * (h + 1)] + kpe, c, a, bb, kn[...]).astype(BF16)
            outs[2][h] = kv_raw[:, 1024 + 64 * h:1024 + 64 * (h + 1)].astype(BF16)

    consts = [p["mla_q_a_norm"], p["mla_kv_a_norm"], p["mla_q_norm"], p["mla_k_norm"], wq, wkv]
    return _stage(name, fn, s // ts,
                  [(seg0, _rows(ts, 512))] + [(t, _rows(ts, LANES)) for t in tabs] + [(a, _full(a.shape)) for a in consts],
                  [(_sds((HEADS, s, LANES), BF16), _heads(ts, LANES)), (_sds((HEADS, s, LANES), BF16), _heads(ts, LANES)),
                   (_sds((HEADS, s, 64), BF16), _heads(ts, 64))])


def _mla_prep_bwd(seg0, tabs, p, wq, wkv, dq, dk, dv, dff, name):
    s = seg0.shape[0]
    ts = _pick(s, (256, 128))

    def fn(ins, outs, accs, scr):
        blk, cos, sa, sb, qan, kvan, qn, kn, wq_r, wkv_r, dq_r, dk_r, dv_r, dff_r = ins
        dqan, dkvan, dqn, dkn, dwq, dwkv = accs
        dqraw_s, dkvraw_s = scr
        b = blk[...]
        cq, ckv, kt = b[:, :256], b[:, 256:384], b[:, 384:512]
        lane = lax.broadcasted_iota(jnp.int32, kt.shape, 1)
        kpe = jnp.where(lane >= 64, kt, 0.0)
        cqn, vjp_cq = jax.vjp(_rms, cq, qan[...])
        ckvn, vjp_ckv = jax.vjp(_rms, ckv, kvan[...])
        q_raw = _nn(cqn, wq_r[...])
        kv_raw = _nn(ckvn, wkv_r[...])
        c, a, bb = cos[...], sa[...], sb[...]

        def head_bwd(raw, gain, d):
            t = _rope(raw, c, a, bb)
            _, vjp = jax.vjp(functools.partial(_rms, n=MLA_QK), t, gain)
            dt, dgain = vjp(d)
            return _rope_t(dt, c, a, bb), dgain

        dkpe = jnp.zeros(kt.shape, F32)
        for h in range(HEADS):
            dqh, dg = head_bwd(q_raw[:, LANES * h:LANES * (h + 1)], qn[...], dq_r[h])
            dqn[...] += dg
            dqraw_s[:, LANES * h:LANES * (h + 1)] = dqh
            dkh, dg = head_bwd(kv_raw[:, LANES * h:LANES * (h + 1)] + kpe, kn[...], dk_r[h])
            dkn[...] += dg
            dkvraw_s[:, LANES * h:LANES * (h + 1)] = dkh
            dkpe = dkpe + dkh
            dkvraw_s[:, 1024 + 64 * h:1024 + 64 * (h + 1)] = dv_r[h]
        dq_raw = dqraw_s[...]
        dkv_raw = dkvraw_s[...]
        dwq[...] += _tn(cqn, dq_raw)
        dwkv[...] += _tn(ckvn, dkv_raw)
        dcq, dg = vjp_cq(_nt(dq_raw, wq_r[...]))
        dqan[...] += dg
        dckv, dg = vjp_ckv(_nt(dkv_raw, wkv_r[...]))
        dkvan[...] += dg
        outs[0][:, 0:256] = dcq.astype(BF16)
        outs[0][:, 256:384] = dckv.astype(BF16)
        outs[0][:, 384:512] = (jnp.where(lane >= 64, dkpe, 0.0) + dff_r[...]).astype(BF16)

    consts = [p["mla_q_a_norm"], p["mla_kv_a_norm"], p["mla_q_norm"], p["mla_k_norm"], wq, wkv]
    return _stage(name, fn, s // ts,
                  [(seg0, _rows(ts, 512))] + [(t, _rows(ts, LANES)) for t in tabs] + [(a, _full(a.shape)) for a in consts]
                  + [(dq, _heads(ts, LANES)), (dk, _heads(ts, LANES)), (dv, _heads(ts, 64)), (dff, _rows(ts, LANES))],
                  [(_sds((s, 512), BF16), _rows(ts, 512))],
                  accs=[_sds((1, 256)), _sds((1, 128)), _sds((1, LANES)), _sds((1, LANES)), _sds(wq.shape), _sds(wkv.shape)],
                  scratch=[pltpu.VMEM((ts, 1024), F32), pltpu.VMEM((ts, 1536), F32)])


def _fox_prep_fwd(seg0, seg1, bf, qn, kn, name):
    s = seg0.shape[0]
    ts = _pick(s, (256, 128))
    steps = int(math.log2(ts))

    def fn(ins, outs, accs, scr):
        kt_r, x_r, bf_r, qn_r, kn_r = ins
        carry = scr[0]

        @pl.when(pl.program_id(0) == 0)
        def _():
            carry[...] = jnp.zeros(carry.shape, F32)

        x = x_r[...]
        for h in range(HEADS):
            outs[0][h] = _rms(x[:, 64 * h:64 * (h + 1)], qn_r[...]).astype(BF16)
            outs[1][h] = _rms(x[:, 512 + 64 * h:512 + 64 * (h + 1)], kn_r[...]).astype(BF16)
            outs[2][h] = x[:, 1024 + 64 * h:1024 + 64 * (h + 1)].astype(BF16)
        kt = kt_r[...]
        lane = lax.broadcasted_iota(jnp.int32, kt.shape, 1)
        row = lax.broadcasted_iota(jnp.int32, kt.shape, 0)
        cs = jnp.where(lane < HEADS, jax.nn.log_sigmoid(kt + bf_r[...]), 0.0)
        for k in range(steps):
            sh = 1 << k
            cs = cs + jnp.where(row >= sh, pltpu.roll(cs, sh, 0), 0.0)
        cs = cs + carry[0:1, :]
        outs[3][...] = cs
        outs[4][...] = cs.T[0:HEADS, :]
        carry[0:1, :] = cs[ts - 1:ts, :]

    return _stage(name, fn, s // ts,
                  [(seg0, _rows(ts, LANES, 3)), (seg1, _rows(ts, 1536)), (bf, _full(bf.shape)), (qn, _full(qn.shape)),
                   (kn, _full(kn.shape))],
                  [(_sds((HEADS, s, 64), BF16), _heads(ts, 64)), (_sds((HEADS, s, 64), BF16), _heads(ts, 64)),
                   (_sds((HEADS, s, 64), BF16), _heads(ts, 64)), (_sds((s, LANES)), _rows(ts, LANES)),
                   (_sds((HEADS, s)), pl.BlockSpec((HEADS, ts), lambda i: (0, i)))],
                  scratch=[pltpu.VMEM((SUBLANES, LANES), F32)])


def _fox_prep_bwd(seg0, seg1, bf, qn, kn, dq, dk, dv, dck, name):
    s = seg0.shape[0]
    ts = _pick(s, (256, 128))
    n = s // ts
    steps = int(math.log2(ts))

    def fn(ins, outs, accs, scr):
        kt_r, x_r, bf_r, qn_r, kn_r, dq_r, dk_r, dv_r, dck_r = ins
        dqn, dkn, dbf = accs
        carry, dbuf = scr

        @pl.when(pl.program_id(0) == 0)
        def _():
            carry[...] = jnp.zeros(carry.shape, F32)

        x = x_r[...]
        for h in range(HEADS):
            _, vjp = jax.vjp(_rms, x[:, 64 * h:64 * (h + 1)], qn_r[...])
            d, dg = vjp(dq_r[h])
            dbuf[:, 64 * h:64 * (h + 1)] = d
            dqn[...] += dg
            _, vjp = jax.vjp(_rms, x[:, 512 + 64 * h:512 + 64 * (h + 1)], kn_r[...])
            d, dg = vjp(dk_r[h])
            dbuf[:, 512 + 64 * h:512 + 64 * (h + 1)] = d
            dkn[...] += dg
            dbuf[:, 1024 + 64 * h:1024 + 64 * (h + 1)] = dv_r[h]
        outs[0][...] = dbuf[...].astype(BF16)
        dc = dck_r[...].reshape(HEADS, ts)
        dc = jnp.concatenate([dc, jnp.zeros((LANES - HEADS, ts), F32)], axis=0).T
        row = lax.broadcasted_iota(jnp.int32, dc.shape, 0)
        lane = lax.broadcasted_iota(jnp.int32, dc.shape, 1)
        for k in range(steps):
            sh = 1 << k
            dc = dc + jnp.where(row < ts - sh, pltpu.roll(dc, ts - sh, 0), 0.0)
        dc = dc + carry[0:1, :]
        carry[0:1, :] = dc[0:1, :]
        dff = jnp.where(lane < HEADS, dc * jax.nn.sigmoid(-(kt_r[...] + bf_r[...])), 0.0)
        outs[1][...] = dff
        dbf[...] += jnp.sum(dff, axis=0, keepdims=True)

    return _stage(name, fn, n,
                  [(seg0, _rows_rev(ts, LANES, n, 3)), (seg1, _rows_rev(ts, 1536, n)), (bf, _full(bf.shape)),
                   (qn, _full(qn.shape)), (kn, _full(kn.shape)), (dq, _heads_rev(ts, 64, n)), (dk, _heads_rev(ts, 64, n)),
                   (dv, _heads_rev(ts, 64, n)), (dck, pl.BlockSpec((HEADS, 1, ts), lambda i: (0, 0, n - 1 - i)))],
                  [(_sds((s, 1536), BF16), _rows_rev(ts, 1536, n)), (_sds((s, LANES)), _rows_rev(ts, LANES, n))],
                  accs=[_sds((1, 64)), _sds((1, 64)), _sds((1, LANES))],
                  scratch=[pltpu.VMEM((SUBLANES, LANES), F32), pltpu.VMEM((ts, 1536), F32)])


def _allowed(i, j, t, chunk_causal):
    qpos = i * t + lax.broadcasted_iota(jnp.int32, (t, t), 0)
    kpos = j * t + lax.broadcasted_iota(jnp.int32, (t, t), 1)
    if chunk_causal:
        return (kpos // CHUNK) <= (qpos // CHUNK)
    return kpos <= qpos


HPS = 2


def _pick_col(c_blk, h):
    lane = lax.broadcasted_iota(jnp.int32, c_blk.shape, 1)
    return jnp.sum(jnp.where(lane == h, c_blk, 0.0), axis=1, keepdims=True)


def _attn_fwd(q, k, v, scale, chunk_causal, name, c=None, ct=None):
    _, s, dk = q.shape
    dv = v.shape[2]
    t = _pick(s, (256, 128))
    bias = c is not None

    def body(*refs):
        if bias:
            q_ref, k_ref, v_ref, c_ref, ct_ref, o_ref, lse_ref = refs
        else:
            q_ref, k_ref, v_ref, o_ref, lse_ref = refs
        hp, i = pl.program_id(0), pl.program_id(1)
        qb = [q_ref[e] for e in range(HPS)]
        cq = [_pick_col(c_ref[...], hp * HPS + e) if bias else None for e in range(HPS)]

        def step(j, carry, diagonal):
            off = pl.multiple_of(j * t, t)
            out = []
            for e in range(HPS):
                m, l, acc = carry[e]
                sc = _nt(qb[e], k_ref[e, pl.ds(off, t), :]) * scale
                if bias:
                    sc = sc + (cq[e] - ct_ref[pl.ds(hp * HPS + e, 1), pl.ds(off, t)])
                if diagonal:
                    sc = jnp.where(_allowed(i, j, t, chunk_causal), sc, NEG)
                m_new = jnp.maximum(m, jnp.max(sc, axis=1, keepdims=True))
                pr = jnp.exp(sc - m_new)
                alpha = jnp.exp(m - m_new)
                out.append((m_new, alpha * l + jnp.sum(pr, axis=1, keepdims=True),
                            alpha * acc + _nn(pr, v_ref[e, pl.ds(off, t), :])))
            return tuple(out)

        init = tuple((jnp.full((t, 1), NEG, F32), jnp.zeros((t, 1), F32), jnp.zeros((t, dv), F32)) for _ in range(HPS))
        res = step(i, lax.fori_loop(0, i, functools.partial(step, diagonal=False), init), True)
        for e in range(HPS):
            m, l, acc = res[e]
            o_ref[e] = acc / l
            lse_ref[e] = m + jnp.log(l)

    ins = [q, k, v] + ([c, ct] if bias else [])
    in_specs = [pl.BlockSpec((HPS, t, dk), lambda h, i: (h, i, 0)), pl.BlockSpec((HPS, s, dk), lambda h, i: (h, 0, 0)),
                pl.BlockSpec((HPS, s, dv), lambda h, i: (h, 0, 0))]
    if bias:
        in_specs += [pl.BlockSpec((t, LANES), lambda h, i: (i, 0)), pl.BlockSpec((HEADS, s), lambda h, i: (0, 0))]
    return pl.pallas_call(
        body, name=name, grid=(HEADS // HPS, s // t), in_specs=in_specs,
        out_specs=[pl.BlockSpec((HPS, t, dv), lambda h, i: (h, i, 0)), pl.BlockSpec((HPS, t, 1), lambda h, i: (h, i, 0))],
        out_shape=[_sds((HEADS, s, dv)), _sds((HEADS, s, 1))],
        compiler_params=pltpu.CompilerParams(dimension_semantics=("parallel", "parallel"), vmem_limit_bytes=48 * 1024 * 1024),
    )(*ins)


def _attn_delta(q, k, v, do, lse, scale, chunk_causal, name, c=None, ct=None):
    _, s, dk = q.shape
    dv = v.shape[2]
    t = _pick(s, (256, 128))
    bias = c is not None

    def body(*refs):
        if bias:
            q_ref, k_ref, v_ref, do_ref, lse_ref, c_ref, ct_ref, dl_ref = refs
        else:
            q_ref, k_ref, v_ref, do_ref, lse_ref, dl_ref = refs
        hp, i = pl.program_id(0), pl.program_id(1)
        qb, dob, lse_b = ([r[e] for e in range(HPS)] for r in (q_ref, do_ref, lse_ref))
        cq = [_pick_col(c_ref[...], hp * HPS + e) if bias else None for e in range(HPS)]

        def step(j, acc, diagonal):
            off = pl.multiple_of(j * t, t)
            out = []
            for e in range(HPS):
                sc = _nt(qb[e], k_ref[e, pl.ds(off, t), :]) * scale
                if bias:
                    sc = sc + (cq[e] - ct_ref[pl.ds(hp * HPS + e, 1), pl.ds(off, t)])
                pr = jnp.exp(sc - lse_b[e])
                if diagonal:
                    pr = jnp.where(_allowed(i, j, t, chunk_causal), pr, 0.0)
                out.append(acc[e] + jnp.sum(pr * _nt(dob[e], v_ref[e, pl.ds(off, t), :]), axis=1, keepdims=True))
            return tuple(out)

        init = tuple(jnp.zeros((t, 1), F32) for _ in range(HPS))
        res = step(i, lax.fori_loop(0, i, functools.partial(step, diagonal=False), init), True)
        for e in range(HPS):
            dl_ref[e] = res[e]

    ins = [q, k, v, do, lse] + ([c, ct] if bias else [])
    in_specs = [pl.BlockSpec((HPS, t, dk), lambda h, i: (h, i, 0)), pl.BlockSpec((HPS, s, dk), lambda h, i: (h, 0, 0)),
                pl.BlockSpec((HPS, s, dv), lambda h, i: (h, 0, 0)), pl.BlockSpec((HPS, t, dv), lambda h, i: (h, i, 0)),
                pl.BlockSpec((HPS, t, 1), lambda h, i: (h, i, 0))]
    if bias:
        in_specs += [pl.BlockSpec((t, LANES), lambda h, i: (i, 0)), pl.BlockSpec((HEADS, s), lambda h, i: (0, 0))]
    return pl.pallas_call(
        body, name=name, grid=(HEADS // HPS, s // t), in_specs=in_specs,
        out_specs=pl.BlockSpec((HPS, t, 1), lambda h, i: (h, i, 0)), out_shape=_sds((HEADS, s, 1)),
        compiler_params=pltpu.CompilerParams(dimension_semantics=("parallel", "parallel"), vmem_limit_bytes=48 * 1024 * 1024),
    )(*ins)


def _attn_bwd(q, k, v, do, lse, delta, scale, chunk_causal, name, c=None, ct=None):
    _, s, dk = q.shape
    dv = v.shape[2]
    t = _pick(s, (256, 128))
    n = s // t
    bias = c is not None

    def body(*refs):
        if bias:
            q_ref, k_ref, v_ref, do_ref, lse_ref, dl_ref, c_ref, ct_ref, dq_ref, dk_ref, dv_ref, dck_ref = refs
        else:
            q_ref, k_ref, v_ref, do_ref, lse_ref, dl_ref, dq_ref, dk_ref, dv_ref = refs
        hp, j = pl.program_id(0), pl.program_id(1)

        @pl.when(j == 0)
        def _():
            dq_ref[...] = jnp.zeros(dq_ref.shape, F32)

        kb, vb = [k_ref[e] for e in range(HPS)], [v_ref[e] for e in range(HPS)]
        joff = pl.multiple_of(j * t, t)
        ck = [ct_ref[pl.ds(hp * HPS + e, 1), pl.ds(joff, t)] if bias else None for e in range(HPS)]

        def step(i, carry, diagonal):
            off = pl.multiple_of(i * t, t)
            out = []
            for e in range(HPS):
                dk_acc, dv_acc, dck_acc = carry[e]
                qb = q_ref[e, pl.ds(off, t), :]
                dob = do_ref[e, pl.ds(off, t), :]
                sc = _nt(qb, kb[e]) * scale
                if bias:
                    sc = sc + (_pick_col(c_ref[pl.ds(off, t), :], hp * HPS + e) - ck[e])
                pr = jnp.exp(sc - lse_ref[e, pl.ds(off, t), :])
                if diagonal:
                    pr = jnp.where(_allowed(i, j, t, chunk_causal), pr, 0.0)
                dv_acc = dv_acc + _tn(pr, dob)
                ds = pr * (_nt(dob, vb[e]) - dl_ref[e, pl.ds(off, t), :])
                dq_ref[e, pl.ds(off, t), :] += _nn(ds, kb[e]) * scale
                dk_acc = dk_acc + _tn(ds, qb) * scale
                if bias:
                    dck_acc = dck_acc - jnp.sum(ds, axis=0, keepdims=True)
                out.append((dk_acc, dv_acc, dck_acc))
            return tuple(out)

        zero = tuple((jnp.zeros((t, dk), F32), jnp.zeros((t, dv), F32), jnp.zeros((1, t), F32)) for _ in range(HPS))
        res = lax.fori_loop(j + 1, n, functools.partial(step, diagonal=False), step(j, zero, True))
        for e in range(HPS):
            dk_ref[e], dv_ref[e] = res[e][0], res[e][1]
            if bias:
                dck_ref[e] = res[e][2]

    ins = [q, k, v, do, lse, delta] + ([c, ct] if bias else [])
    full = lambda d: pl.BlockSpec((HPS, s, d), lambda h, j: (h, 0, 0))
    blk = lambda d: pl.BlockSpec((HPS, t, d), lambda h, j: (h, j, 0))
    in_specs = [full(dk), blk(dk), blk(dv), full(dv), full(1), full(1)]
    if bias:
        in_specs += [pl.BlockSpec((s, LANES), lambda h, j: (0, 0)), pl.BlockSpec((HEADS, s), lambda h, j: (0, 0))]
    out_specs = [full(dk), blk(dk), blk(dv)]
    out_shape = [_sds((HEADS, s, dk)), _sds((HEADS, s, dk)), _sds((HEADS, s, dv))]
    if bias:
        out_specs.append(pl.BlockSpec((HPS, 1, t), lambda h, j: (h, 0, j)))
        out_shape.append(_sds((HEADS, 1, s)))
    return pl.pallas_call(
        body, name=name, grid=(HEADS // HPS, n), in_specs=in_specs, out_specs=out_specs, out_shape=out_shape,
        compiler_params=pltpu.CompilerParams(dimension_semantics=("parallel", "arbitrary"), vmem_limit_bytes=56 * 1024 * 1024),
    )(*ins)


def _s5_disc(lr, li, ldt, br, bi):
    dt = jnp.exp(ldt)
    mag = jnp.exp(lr * dt)
    a_re = mag * jnp.cos(li * dt)
    a_im = mag * jnp.sin(li * dt)
    den = lr * lr + li * li
    f_re = ((a_re - 1.0) * lr + a_im * li) / den
    f_im = (a_im * lr - (a_re - 1.0) * li) / den
    return a_re, a_im, f_re * br - f_im * bi, f_re * bi + f_im * br


def _s5_params_fwd(lr16, li16, ldt16, br2, bi2, name):
    def body(a, b, c, d, e, o0, o1, o2, o3):
        r = _s5_disc(a[...], b[...], c[...], d[...], e[...])
        o0[...], o1[...], o2[...], o3[...] = r

    return pl.pallas_call(body, name=name, out_shape=[_sds((512, 64))] * 4)(lr16, li16, ldt16, br2, bi2)


def _s5_params_bwd(lr16, li16, ldt16, br2, bi2, da_re16, da_im16, dbb_re, dbb_im, name):
    def body(a, b, c, d, e, g0, g1, g2, g3, o_lr, o_li, o_dt, o_br, o_bi):
        _, vjp = jax.vjp(_s5_disc, a[...], b[...], c[...], d[...], e[...])
        dlr, dli, dldt, dbr, dbi = vjp((g0[...], g1[...], g2[...], g3[...]))
        grp = lambda z: z.reshape(S5_GROUPS, S5_GROUP, S5_STATE).sum(axis=1)
        o_lr[...] = grp(dlr)
        o_li[...] = grp(dli)
        o_dt[...] = jnp.sum(grp(dldt), axis=1, keepdims=True)
        o_br[...] = dbr
        o_bi[...] = dbi

    return pl.pallas_call(
        body, name=name, out_shape=[_sds((32, 64)), _sds((32, 64)), _sds((32, 1)), _sds((512, 64)), _sds((512, 64))],
    )(lr16, li16, ldt16, br2, bi2, da_re16, da_im16, dbb_re, dbb_im)


def _cmul(ar, ai, br, bi):
    return ar * br - ai * bi, ar * bi + ai * br


def _scan(bre, bim, a_re, a_im, reverse, name, x_re=None, x_im=None):
    s, w = bre.shape
    lw = _pick(w, (256, 128))
    nt = s // SUBLANES
    with_da = x_re is not None

    def body(*refs):
        if with_da:
            bre_r, bim_r, are_r, aim_r, xre_r, xim_r, ore_r, oim_r, dare_r, daim_r = refs
        else:
            bre_r, bim_r, are_r, aim_r, ore_r, oim_r = refs
        ar, ai = are_r[...], aim_r[...]
        shp = (SUBLANES, lw)
        row = lax.broadcasted_iota(jnp.int32, shp, 0)
        pows = [(ar, ai)]
        for _ in range(SUBLANES - 1):
            pows.append(_cmul(pows[-1][0], pows[-1][1], ar, ai))
        cm_r, cm_i = jnp.zeros(shp, F32), jnp.zeros(shp, F32)
        for r in range(SUBLANES):
            e = (SUBLANES - 1 - r) if reverse else r
            cm_r = jnp.where(row == r, jnp.broadcast_to(pows[e][0], shp), cm_r)
            cm_i = jnp.where(row == r, jnp.broadcast_to(pows[e][1], shp), cm_i)
        steps = [(1, pows[0]), (2, pows[1]), (4, pows[3])]

        def tile(it, carry):
            if with_da:
                c_r, c_i, acc_r, acc_i = carry
            else:
                c_r, c_i = carry
            r = (nt - 1 - it) if reverse else it
            off = pl.multiple_of(r * SUBLANES, SUBLANES)
            xr, xi = bre_r[pl.ds(off, SUBLANES), :], bim_r[pl.ds(off, SUBLANES), :]
            for sh, (pr, pi) in steps:
                if reverse:
                    keep = row < SUBLANES - sh
                    sr = jnp.where(keep, pltpu.roll(xr, SUBLANES - sh, 0), 0.0)
                    si = jnp.where(keep, pltpu.roll(xi, SUBLANES - sh, 0), 0.0)
                else:
                    keep = row >= sh
                    sr = jnp.where(keep, pltpu.roll(xr, sh, 0), 0.0)
                    si = jnp.where(keep, pltpu.roll(xi, sh, 0), 0.0)
                mr, mi = _cmul(pr, pi, sr, si)
                xr, xi = xr + mr, xi + mi
            mr, mi = _cmul(cm_r, cm_i, c_r, c_i)
            xr, xi = xr + mr, xi + mi
            ore_r[pl.ds(off, SUBLANES), :] = xr
            oim_r[pl.ds(off, SUBLANES), :] = xi
            edge = 0 if reverse else SUBLANES - 1
            c_r, c_i = xr[edge:edge + 1, :], xi[edge:edge + 1, :]
            if not with_da:
                return c_r, c_i
            fr, fi = xre_r[pl.ds(off, SUBLANES), :], xim_r[pl.ds(off, SUBLANES), :]
            poff = pl.multiple_of(jnp.maximum(r - 1, 0) * SUBLANES, SUBLANES)
            live = (r > 0).astype(F32)
            pr_last = xre_r[pl.ds(poff, SUBLANES), :][SUBLANES - 1:SUBLANES, :] * live
            pi_last = xim_r[pl.ds(poff, SUBLANES), :][SUBLANES - 1:SUBLANES, :] * live
            sr = jnp.where(row >= 1, pltpu.roll(fr, 1, 0), jnp.broadcast_to(pr_last, shp))
            si = jnp.where(row >= 1, pltpu.roll(fi, 1, 0), jnp.broadcast_to(pi_last, shp))
            acc_r = acc_r + xr * sr + xi * si
            acc_i = acc_i + xi * sr - xr * si
            return c_r, c_i, acc_r, acc_i

        z1 = jnp.zeros((1, lw), F32)
        if with_da:
            _, _, acc_r, acc_i = lax.fori_loop(0, nt, tile, (z1, z1, jnp.zeros(shp, F32), jnp.zeros(shp, F32)))
            dare_r[...] = jnp.sum(acc_r, axis=0, keepdims=True)
            daim_r[...] = jnp.sum(acc_i, axis=0, keepdims=True)
        else:
            lax.fori_loop(0, nt, tile, (z1, z1))

    col = pl.BlockSpec((s, lw), lambda j: (0, j))
    one = pl.BlockSpec((1, lw), lambda j: (0, j))
    ins = [bre, bim, a_re, a_im] + ([x_re, x_im] if with_da else [])
    in_specs = [col, col, one, one] + ([col, col] if with_da else [])
    out_specs = [col, col] + ([one, one] if with_da else [])
    out_shape = [_sds((s, w)), _sds((s, w))] + ([_sds((1, w)), _sds((1, w))] if with_da else [])
    return pl.pallas_call(
        body, name=name, grid=(w // lw,), in_specs=in_specs, out_specs=out_specs, out_shape=out_shape,
        compiler_params=pltpu.CompilerParams(dimension_semantics=("parallel",), vmem_limit_bytes=56 * 1024 * 1024),
    )(*ins)


S5_SUPER = 4


def _s5_mm(a, w4, mode, name, acc=None):
    s = a.shape[0]
    _, r, c = w4.shape
    wa, wo = (r, c) if mode == "nn" else (c, r)
    tm = _pick(s, (512, 256, 128))
    has_acc = acc is not None

    def body(*refs):
        if has_acc:
            a_ref, w_ref, c_ref, o_ref = refs
        else:
            a_ref, w_ref, o_ref = refs
        res = _nn(a_ref[...], w_ref[0]) if mode == "nn" else _nt(a_ref[...], w_ref[0])
        o_ref[...] = res + c_ref[...] if has_acc else res

    o_spec = pl.BlockSpec((tm, wo), lambda k, i: (i, k))
    ins = [a, w4] + ([acc] if has_acc else [])
    in_specs = [pl.BlockSpec((tm, wa), lambda k, i: (i, k)), pl.BlockSpec((1, r, c), lambda k, i: (k, 0, 0))]
    return pl.pallas_call(
        body, name=name, grid=(S5_SUPER, s // tm), in_specs=in_specs + ([o_spec] if has_acc else []), out_specs=o_spec,
        out_shape=_sds((s, S5_SUPER * wo)),
        compiler_params=pltpu.CompilerParams(dimension_semantics=("parallel", "parallel"), vmem_limit_bytes=48 * 1024 * 1024),
    )(*ins)


def _s5_group_tn(a, b, name):
    s = a.shape[0]
    tk = _pick(s, (512, 256, 128))
    nk = s // tk

    def body(a_ref, b_ref, o_ref, acc_ref):
        k = pl.program_id(1)

        @pl.when(k == 0)
        def _():
            acc_ref[...] = jnp.zeros(acc_ref.shape, F32)

        acc_ref[...] += _tn(a_ref[...], b_ref[...])

        @pl.when(k == nk - 1)
        def _():
            p = acc_ref[...]
            grp = lax.broadcasted_iota(jnp.int32, (LANES, S5_STATE), 0) // S5_GROUP
            out = jnp.zeros((LANES, S5_STATE), F32)
            for j in range(LANES // S5_GROUP):
                out = jnp.where(grp == j, p[:, S5_STATE * j:S5_STATE * (j + 1)], out)
            o_ref[...] = out

    return pl.pallas_call(
        body, name=name, grid=(S5_SUPER, nk),
        in_specs=[pl.BlockSpec((tk, LANES), lambda g, k: (k, g)), pl.BlockSpec((tk, 512), lambda g, k: (k, g))],
        out_specs=pl.BlockSpec((LANES, S5_STATE), lambda g, k: (g, 0)), out_shape=_sds((512, S5_STATE)),
        scratch_shapes=[pltpu.VMEM((LANES, 512), F32)],
        compiler_params=pltpu.CompilerParams(dimension_semantics=("parallel", "arbitrary")),
    )(a, b)


def _s5_seg1(y0, u, d):
    return jax.nn.gelu(y0 + d * u)


def _s5_seg2(z, t, b):
    return z * jax.nn.sigmoid(t + b)


def _s5_post_fwd(y0, seg2, d, wglu, bglu, name):
    s = y0.shape[0]
    ts = _pick(s, (256, 128))

    def fn(ins, outs, accs, scr):
        z = _s5_seg1(ins[0][...], ins[1][...], ins[2][...])
        outs[0][...] = _s5_seg2(z, _nn(z, ins[3][...]), ins[4][...])

    return _stage(name, fn, s // ts,
                  [(y0, _rows(ts, 512)), (seg2, _rows(ts, 512)), (d, _full(d.shape)), (wglu, _full(wglu.shape)),
                   (bglu, _full(bglu.shape))], [(_sds((s, 512)), _rows(ts, 512))])[0]


def _s5_post_bwd(y0, seg2, d, wglu, bglu, dy, name):
    s = y0.shape[0]
    ts = _pick(s, (256, 128))

    def fn(ins, outs, accs, scr):
        y0_r, u_r, d_r, w_r, b_r, dy_r = ins
        z, vjp1 = jax.vjp(_s5_seg1, y0_r[...], u_r[...], d_r[...])
        t = _nn(z, w_r[...])
        _, vjp2 = jax.vjp(_s5_seg2, z, t, b_r[...])
        dz, dt, db = vjp2(dy_r[...])
        accs[0][...] += _tn(z, dt)
        accs[1][...] += db
        dy0, du, dd = vjp1(dz + _nt(dt, w_r[...]))
        accs[2][...] += dd
        outs[0][...] = dy0
        outs[1][...] = du

    return _stage(name, fn, s // ts,
                  [(y0, _rows(ts, 512)), (seg2, _rows(ts, 512)), (d, _full(d.shape)), (wglu, _full(wglu.shape)),
                   (bglu, _full(bglu.shape)), (dy, _rows(ts, 512))],
                  [(_sds((s, 512)), _rows(ts, 512)), (_sds((s, 512)), _rows(ts, 512))],
                  accs=[_sds((512, 512)), _sds((1, 512)), _sds((1, 512))])


def _gate_a(y, g):
    return y * jax.nn.silu(g)


def _gate_m(o0, o1, o2, m0, m1, m2):
    return jax.nn.sigmoid(m0) * o0 + jax.nn.sigmoid(m1) * o1 + jax.nn.sigmoid(m2) * o2


def _assemble(ybuf, o_mla, o_fox, y_s5):
    for h in range(HEADS):
        ybuf[:, 64 * h:64 * (h + 1)] = o_mla[h]
        ybuf[:, 512 + 64 * h:512 + 64 * (h + 1)] = o_fox[h]
    ybuf[:, 1024:1536] = y_s5[...]


def _gate_fwd(o_mla, o_fox, y_s5, seg3, seg4, x, wo, wout, name):
    s = x.shape[0]
    ts = _pick(s, (256, 128))

    def fn(ins, outs, accs, scr):
        om, of, ys, g_r, m_r, x_r, wo_r, wout_r = ins
        ybuf = scr[0]
        _assemble(ybuf, om, of, ys)
        a = _gate_a(ybuf[...], g_r[...])
        o = [_nn(a[:, 512 * b:512 * (b + 1)], wo_r[512 * b:512 * (b + 1), :]) for b in range(3)]
        merged = _gate_m(o[0], o[1], o[2], m_r[:, 0:1024], m_r[:, 1024:2048], m_r[:, 2048:3072])
        outs[0][...] = x_r[...] + _nn(merged, wout_r[...])

    return _stage(name, fn, s // ts,
                  [(o_mla, _heads(ts, 64)), (o_fox, _heads(ts, 64)), (y_s5, _rows(ts, 512)), (seg3, _rows(ts, 1536)),
                   (seg4, _rows(ts, 3072)), (x, _rows(ts, D_MODEL)), (wo, _full(wo.shape)), (wout, _full(wout.shape))],
                  [(_sds((s, D_MODEL)), _rows(ts, D_MODEL))], scratch=[pltpu.VMEM((ts, 1536), F32)])[0]


def _gate_bwd(o_mla, o_fox, y_s5, seg3, seg4, wo, wout, dout, name):
    s = dout.shape[0]
    ts = _pick(s, (128,))

    def fn(ins, outs, accs, scr):
        om, of, ys, g_r, m_r, wo_r, wout_r, dout_r = ins
        do_mla, do_fox, dys, dg_r, dm_r = outs
        dwo, dwout = accs
        ybuf, dabuf = scr
        _assemble(ybuf, om, of, ys)
        a, vjp_a = jax.vjp(_gate_a, ybuf[...], g_r[...])
        o = [_nn(a[:, 512 * b:512 * (b + 1)], wo_r[512 * b:512 * (b + 1), :]) for b in range(3)]
        ms = [m_r[:, 1024 * b:1024 * (b + 1)] for b in range(3)]
        merged, vjp_m = jax.vjp(_gate_m, *o, *ms)
        dout_v = dout_r[...]
        dwout[...] += _tn(merged, dout_v)
        cts = vjp_m(_nt(dout_v, wout_r[...]))
        for b in range(3):
            dm_r[:, 1024 * b:1024 * (b + 1)] = cts[3 + b].astype(BF16)
            dwo[512 * b:512 * (b + 1), :] += _tn(a[:, 512 * b:512 * (b + 1)], cts[b])
            dabuf[:, 512 * b:512 * (b + 1)] = _nt(cts[b], wo_r[512 * b:512 * (b + 1), :])
        dy, dg = vjp_a(dabuf[...])
        dg_r[...] = dg.astype(BF16)
        dys[...] = dy[:, 1024:1536]
        for h in range(HEADS):
            do_mla[h] = dy[:, 64 * h:64 * (h + 1)]
            do_fox[h] = dy[:, 512 + 64 * h:512 + 64 * (h + 1)]

    return _stage(name, fn, s // ts,
                  [(o_mla, _heads(ts, 64)), (o_fox, _heads(ts, 64)), (y_s5, _rows(ts, 512)), (seg3, _rows(ts, 1536)),
                   (seg4, _rows(ts, 3072)), (wo, _full(wo.shape)), (wout, _full(wout.shape)), (dout, _rows(ts, D_MODEL))],
                  [(_sds((HEADS, s, 64)), _heads(ts, 64)), (_sds((HEADS, s, 64)), _heads(ts, 64)), (_sds((s, 512)), _rows(ts, 512)),
                   (_sds((s, 1536), BF16), _rows(ts, 1536)), (_sds((s, 3072), BF16), _rows(ts, 3072))],
                  accs=[_sds(wo.shape), _sds(wout.shape)], scratch=[pltpu.VMEM((ts, 1536), F32), pltpu.VMEM((ts, 1536), F32)],
                  vmem_mb=56)


def _loss_head(y, target, name):
    s = y.shape[0]
    ts = _pick(s, (256, 128))

    def fn(ins, outs, accs, scr):
        e = ins[0][...] - ins[1][...]
        outs[0][...] = e / D_MODEL
        accs[0][...] += 0.5 * jnp.sum(jnp.sum(e * e, axis=1, keepdims=True) / D_MODEL, axis=0, keepdims=True)

    r = _stage(name, fn, s // ts, [(y, _rows(ts, D_MODEL)), (target, _rows(ts, D_MODEL))],
               [(_sds((s, D_MODEL)), _rows(ts, D_MODEL))], accs=[_sds((1, 1))])
    return r[1], r[0]


def _pad_w_in(w):
    z = lambda n: jnp.zeros((w.shape[0], n), w.dtype)
    return jnp.concatenate([w[:, 0:384], w[:, 1952:1960], z(56), w[:, 384:416], z(32), w[:, 416:1952], w[:, 1960:IN_WIDTH]], axis=1)


def _unpad_w_in(d):
    return jnp.concatenate([d[:, 0:384], d[:, 448:480], d[:, 512:2048], d[:, 384:392], d[:, 2048:PAD_IN]], axis=1)


def _pad_wq(w):
    w = w.reshape(MLA_Q_RANK, HEADS, MLA_QK)
    return jnp.pad(w, ((0, 0), (0, 0), (0, LANES - MLA_QK))).reshape(MLA_Q_RANK, HEADS * LANES)


def _unpad_wq(d):
    return d.reshape(MLA_Q_RANK, HEADS, LANES)[:, :, :MLA_QK].reshape(MLA_Q_RANK, HEADS * MLA_QK)


def _pad_wkv(w):
    w = w.reshape(MLA_KV_RANK, HEADS, MLA_NOPE + MLA_V)
    k = jnp.pad(w[:, :, :MLA_NOPE], ((0, 0), (0, 0), (0, LANES - MLA_NOPE))).reshape(MLA_KV_RANK, HEADS * LANES)
    return jnp.concatenate([k, w[:, :, MLA_NOPE:].reshape(MLA_KV_RANK, HEADS * MLA_V)], axis=1)


def _unpad_wkv(d):
    k = d[:, :HEADS * LANES].reshape(MLA_KV_RANK, HEADS, LANES)[:, :, :MLA_NOPE]
    v = d[:, HEADS * LANES:].reshape(MLA_KV_RANK, HEADS, MLA_V)
    return jnp.concatenate([k, v], axis=2).reshape(MLA_KV_RANK, HEADS * (MLA_NOPE + MLA_V))


def _pad_lanes(v, n=LANES):
    return jnp.pad(v, (0, n - v.shape[0])).reshape(1, n)


def _super_blocks(b):
    _, r, c = b.shape
    per = S5_GROUPS // S5_SUPER
    b = b.reshape(S5_SUPER, per, r, c)
    eye = jnp.eye(per, dtype=b.dtype)
    return (b[:, :, :, None, :] * eye[None, :, None, :, None]).reshape(S5_SUPER, per * r, per * c)


def _layer_params(l, w, small):
    p = {k: small[k][l] for k in small}
    q = {}
    q["norm_g"] = p["norm_g"].reshape(1, D_MODEL)
    q["mla_q_a_norm"] = p["mla_q_a_norm"].reshape(1, 256)
    q["mla_kv_a_norm"] = p["mla_kv_a_norm"].reshape(1, 128)
    q["mla_q_norm"] = _pad_lanes(p["mla_q_norm"])
    q["mla_k_norm"] = _pad_lanes(p["mla_k_norm"])
    q["fox_b_f"] = _pad_lanes(p["fox_b_f"])
    q["fox_q_norm"] = p["fox_q_norm"].reshape(1, 64)
    q["fox_k_norm"] = p["fox_k_norm"].reshape(1, 64)
    q["s5_d"] = p["s5_d"].reshape(1, 512)
    q["s5_b_glu"] = p["s5_b_glu"].reshape(1, 512)
    rep = lambda z: jnp.repeat(z, S5_GROUP, axis=0)
    q["lr16"], q["li16"] = rep(p["s5_lambda_re"]), rep(p["s5_lambda_im"])
    q["ldt16"] = rep(jnp.broadcast_to(p["s5_log_dt"][:, None], (S5_GROUPS, S5_STATE)))
    q["br2"] = p["s5_b_re"].transpose(0, 2, 1).reshape(512, 64)
    q["bi2"] = p["s5_b_im"].transpose(0, 2, 1).reshape(512, 64)
    q["c_re"], q["c_im"] = p["s5_c_re"], p["s5_c_im"]
    win = _pad_w_in(w["w_in"][l])
    q["w_seg"] = [win[:, SEG_OFF[k]:SEG_OFF[k] + SEG_W[k]] for k in range(5)]
    q["wq"] = _pad_wq(w["mla_w_q_up"][l])
    q["wkv"] = _pad_wkv(w["mla_w_kv_up"][l])
    q["wglu"] = w["s5_w_glu"][l]
    q["wo"] = w["w_branch_out"][l]
    q["wout"] = w["w_out"][l]
    return q


def _layer_fwd(l, x, tabs, q):
    n = lambda s: f"l{l}_{s}"
    sv = {"x": x}
    h = _norm_fwd(x, q["norm_g"], n("norm_fwd"))
    sv["h"] = h
    seg = [_mm(h, q["w_seg"][k], "nn", n(f"proj{k}")) for k in range(5)]
    sv["seg"] = seg
    mq, mk, mv = _mla_prep_fwd(seg[0], tabs, q, q["wq"], q["wkv"], n("mla_prep_fwd"))
    o_mla, lse_mla = _attn_fwd(mq, mk, mv, 1.0 / math.sqrt(MLA_QK), True, n("mla_attn_fwd"))
    sv.update(mq=mq, mk=mk, mv=mv, o_mla=o_mla, lse_mla=lse_mla)
    fq, fk, fv, c, ct = _fox_prep_fwd(seg[0], seg[1], q["fox_b_f"], q["fox_q_norm"], q["fox_k_norm"], n("fox_prep_fwd"))
    o_fox, lse_fox = _attn_fwd(fq, fk, fv, 1.0 / math.sqrt(FOX_DIM), False, n("fox_attn_fwd"), c=c, ct=ct)
    sv.update(fq=fq, fk=fk, fv=fv, c=c, ct=ct, o_fox=o_fox, lse_fox=lse_fox)
    a_re16, a_im16, bb_re, bb_im = _s5_params_fwd(q["lr16"], q["li16"], q["ldt16"], q["br2"], q["bi2"], n("s5_params_fwd"))
    a_re = a_re16.reshape(S5_GROUPS, S5_GROUP, S5_STATE)[:, 0, :].reshape(1, S5_LANES)
    a_im = a_im16.reshape(S5_GROUPS, S5_GROUP, S5_STATE)[:, 0, :].reshape(1, S5_LANES)
    wb_re = _super_blocks(bb_re.reshape(S5_GROUPS, S5_GROUP, S5_STATE)).astype(BF16)
    wb_im = _super_blocks(bb_im.reshape(S5_GROUPS, S5_GROUP, S5_STATE)).astype(BF16)
    wc_re = _super_blocks(q["c_re"].transpose(0, 2, 1)).astype(BF16)
    wc_im = _super_blocks(-q["c_im"].transpose(0, 2, 1)).astype(BF16)
    bu_re = _s5_mm(seg[2], wb_re, "nn", n("s5_bu_re"))
    bu_im = _s5_mm(seg[2], wb_im, "nn", n("s5_bu_im"))
    x_re, x_im = _scan(bu_re, bu_im, a_re, a_im, False, n("s5_scan_fwd"))
    y0 = _s5_mm(x_im, wc_im, "nn", n("s5_y_im"), acc=_s5_mm(x_re, wc_re, "nn", n("s5_y_re")))
    y_s5 = _s5_post_fwd(y0, seg[2], q["s5_d"], q["wglu"], q["s5_b_glu"], n("s5_post_fwd"))
    sv.update(a_re=a_re, a_im=a_im, wb_re=wb_re, wb_im=wb_im, wc_re=wc_re, wc_im=wc_im, x_re=x_re, x_im=x_im, y0=y0, y_s5=y_s5)
    out = _gate_fwd(o_mla, o_fox, y_s5, seg[3], seg[4], x, q["wo"], q["wout"], n("gate_fwd"))
    return out, sv


def _layer_bwd(l, dout, tabs, q, sv):
    n = lambda s: f"l{l}_{s}"
    seg = sv["seg"]
    g = {}
    (do_mla, do_fox, dy_s5, dseg3, dseg4, g["wo"], g["wout"]) = _gate_bwd(
        sv["o_mla"], sv["o_fox"], sv["y_s5"], seg[3], seg[4], q["wo"], q["wout"], dout, n("gate_bwd"))
    dy0, du_a, g["wglu"], g["s5_b_glu"], g["s5_d"] = _s5_post_bwd(sv["y0"], seg[2], q["s5_d"], q["wglu"], q["s5_b_glu"], dy_s5,
                                                                 n("s5_post_bwd"))
    dx_re = _s5_mm(dy0, sv["wc_re"], "nt", n("s5_dx_re"))
    dx_im = _s5_mm(dy0, sv["wc_im"], "nt", n("s5_dx_im"))
    g["s5_c_re"] = _s5_group_tn(dy0, sv["x_re"], n("s5_dc_re")).reshape(S5_GROUPS, S5_GROUP, S5_STATE)
    g["s5_c_im"] = -_s5_group_tn(dy0, sv["x_im"], n("s5_dc_im")).reshape(S5_GROUPS, S5_GROUP, S5_STATE)
    g_re, g_im, da_re, da_im = _scan(dx_re, dx_im, sv["a_re"], -sv["a_im"], True, n("s5_scan_bwd"), x_re=sv["x_re"], x_im=sv["x_im"])
    dseg2 = _s5_mm(g_im, sv["wb_im"], "nt", n("s5_du_im"), acc=_s5_mm(g_re, sv["wb_re"], "nt", n("s5_du_re"), acc=du_a))
    dbb_re = _s5_group_tn(seg[2], g_re, n("s5_dbb_re"))
    dbb_im = _s5_group_tn(seg[2], g_im, n("s5_dbb_im"))
    first = (jnp.arange(512) % S5_GROUP == 0).astype(F32)[:, None]
    da_re16 = jnp.repeat(da_re.reshape(S5_GROUPS, S5_STATE), S5_GROUP, axis=0) * first
    da_im16 = jnp.repeat(da_im.reshape(S5_GROUPS, S5_STATE), S5_GROUP, axis=0) * first
    dlr, dli, dldt, dbr2, dbi2 = _s5_params_bwd(q["lr16"], q["li16"], q["ldt16"], q["br2"], q["bi2"], da_re16, da_im16, dbb_re,
                                               dbb_im, n("s5_params_bwd"))
    g["s5_lambda_re"], g["s5_lambda_im"], g["s5_log_dt"] = dlr, dli, dldt.reshape(S5_GROUPS)
    g["s5_b_re"] = dbr2.reshape(S5_GROUPS, S5_GROUP, S5_STATE).transpose(0, 2, 1)
    g["s5_b_im"] = dbi2.reshape(S5_GROUPS, S5_GROUP, S5_STATE).transpose(0, 2, 1)
    dl_fox = _attn_delta(sv["fq"], sv["fk"], sv["fv"], do_fox, sv["lse_fox"], 1.0 / math.sqrt(FOX_DIM), False,
                         n("fox_attn_delta"), c=sv["c"], ct=sv["ct"])
    dfq, dfk, dfv, dck = _attn_bwd(sv["fq"], sv["fk"], sv["fv"], do_fox, sv["lse_fox"], dl_fox, 1.0 / math.sqrt(FOX_DIM), False,
                                   n("fox_attn_bwd"), c=sv["c"], ct=sv["ct"])
    dseg1, dff, g["fox_q_norm"], g["fox_k_norm"], dbf = _fox_prep_bwd(seg[0], seg[1], q["fox_b_f"], q["fox_q_norm"], q["fox_k_norm"],
                                                                      dfq, dfk, dfv, dck, n("fox_prep_bwd"))
    g["fox_b_f"] = dbf[0, :HEADS]
    dl_mla = _attn_delta(sv["mq"], sv["mk"], sv["mv"], do_mla, sv["lse_mla"], 1.0 / math.sqrt(MLA_QK), True, n("mla_attn_delta"))
    dmq, dmk, dmv = _attn_bwd(sv["mq"], sv["mk"], sv["mv"], do_mla, sv["lse_mla"], dl_mla, 1.0 / math.sqrt(MLA_QK), True,
                              n("mla_attn_bwd"))
    dseg0, dqan, dkvan, dqn, dkn, g["wq"], g["wkv"] = _mla_prep_bwd(seg[0], tabs, q, q["wq"], q["wkv"], dmq, dmk, dmv, dff,
                                                                   n("mla_prep_bwd"))
    g["mla_q_a_norm"], g["mla_kv_a_norm"] = dqan, dkvan
    g["mla_q_norm"], g["mla_k_norm"] = dqn[0, :MLA_QK], dkn[0, :MLA_QK]
    dsegs = [dseg0, dseg1, dseg2, dseg3, dseg4]
    dh = None
    for k in range(5):
        dh = _mm(dsegs[k], q["w_seg"][k], "nt", n(f"dh{k}"), acc=dh)
    g["w_in"] = jnp.concatenate([_mm(sv["h"], dsegs[k], "tn", n(f"dwin{k}")) for k in range(5)], axis=1)
    dx, g["norm_g"] = _norm_bwd(sv["x"], q["norm_g"], dh, dout, n("norm_bwd"))
    return dx, g


MESH = pl.DeviceIdType.MESH
ANY = pl.BlockSpec(memory_space=pl.ANY)


def _all_gather(blocks, name):
    n = len(blocks)

    def body(*refs):
        x_refs, out_refs = refs[:n], refs[n:2 * n]
        send_sems, recv_sems, local_sems = refs[2 * n:]
        x, y, c = lax.axis_index("x"), lax.axis_index("y"), lax.axis_index("c")
        me, sibling = (x, y, c), (x, y, 1 - c)
        chips = [(1 - x, y), (x, 1 - y), (1 - x, 1 - y)]

        def slot(a, px, py, pc):
            return out_refs[a].at[4 * px + 2 * py + pc]

        def copy(a, k, blk, to, src=None):
            return pltpu.make_async_remote_copy(src_ref=slot(a, *blk) if src is None else src, dst_ref=slot(a, *blk),
                                                send_sem=send_sems.at[7 * a + k], recv_sem=recv_sems.at[7 * a + k],
                                                device_id=to, device_id_type=MESH)

        mine = [pltpu.make_async_copy(x_refs[a], slot(a, *me), local_sems.at[a]) for a in range(n)]
        for cp in mine:
            cp.start()
        first = []
        for j, chip in enumerate(chips):
            first += [copy(a, 1 + j, me, (*chip, c), src=x_refs[a]) for a in range(n)]
        first += [copy(a, 0, me, sibling, src=x_refs[a]) for a in range(n)]
        for cp in first:
            cp.start()
        passed = []
        for j, chip in enumerate(chips):
            for a in range(n):
                copy(a, 1 + j, (*chip, c), me).wait_recv()
                passed.append(copy(a, 4 + j, (*chip, c), sibling))
                passed[-1].start()
        for a in range(n):
            copy(a, 0, sibling, me).wait_recv()
        for j, chip in enumerate(chips):
            for a in range(n):
                copy(a, 4 + j, (*chip, 1 - c), me).wait_recv()
        for cp in first + passed:
            cp.wait_send()
        for cp in mine:
            cp.wait()

    return pl.pallas_call(
        body, name=name, out_shape=[jax.ShapeDtypeStruct((N_DEV,) + b.shape, b.dtype) for b in blocks],
        in_specs=[ANY] * n, out_specs=[ANY] * n,
        scratch_shapes=[pltpu.SemaphoreType.DMA((7 * n,)), pltpu.SemaphoreType.DMA((7 * n,)), pltpu.SemaphoreType.DMA((n,))],
    )(*blocks)


def _exchange_sibling(parts, name):
    n = len(parts)

    def body(*refs):
        p_refs, got_refs = refs[:n], refs[n:2 * n]
        send_sems, recv_sems = refs[2 * n:]
        x, y, c = lax.axis_index("x"), lax.axis_index("y"), lax.axis_index("c")
        sibling = (x, y, 1 - c)
        copies = []
        for a in range(n):
            for j in range(4):
                copies.append(pltpu.make_async_remote_copy(
                    src_ref=p_refs[a].at[2 * j + (1 - c)], dst_ref=got_refs[a].at[j], send_sem=send_sems.at[4 * a + j],
                    recv_sem=recv_sems.at[4 * a + j], device_id=sibling, device_id_type=MESH))
        for cp in copies:
            cp.start()
        for cp in copies:
            cp.wait()

    return pl.pallas_call(
        body, name=name, out_shape=[jax.ShapeDtypeStruct((4,) + p.shape[1:], p.dtype) for p in parts],
        in_specs=[ANY] * n, out_specs=[ANY] * n,
        scratch_shapes=[pltpu.SemaphoreType.DMA((4 * n,)), pltpu.SemaphoreType.DMA((4 * n,))],
    )(*parts)


def _exchange_chips(parts, name):
    n = len(parts)

    def body(*refs):
        p_refs, got_refs = refs[:n], refs[n:2 * n]
        send_sems, recv_sems, local_sems = refs[2 * n:]
        x, y, c = lax.axis_index("x"), lax.axis_index("y"), lax.axis_index("c")
        my_chip = 2 * x + y
        chips = [(1 - x, y), (x, 1 - y), (1 - x, 1 - y)]
        mine = [pltpu.make_async_copy(p_refs[a].at[my_chip], got_refs[a].at[my_chip], local_sems.at[a]) for a in range(n)]
        for cp in mine:
            cp.start()

        def copy(a, k, px, py):
            return pltpu.make_async_remote_copy(src_ref=p_refs[a].at[2 * px + py], dst_ref=got_refs[a].at[my_chip],
                                                send_sem=send_sems.at[3 * a + k], recv_sem=recv_sems.at[3 * a + k],
                                                device_id=(px, py, c), device_id_type=MESH)

        copies = [copy(a, k, px, py) for k, (px, py) in enumerate(chips) for a in range(n)]
        for cp in copies:
            cp.start()
        for k, (px, py) in enumerate(chips):
            for a in range(n):
                pltpu.make_async_remote_copy(src_ref=p_refs[a].at[my_chip], dst_ref=got_refs[a].at[2 * px + py],
                                             send_sem=send_sems.at[3 * a + k], recv_sem=recv_sems.at[3 * a + k],
                                             device_id=(px, py, c), device_id_type=MESH).wait_recv()
        for cp in copies:
            cp.wait_send()
        for cp in mine:
            cp.wait()

    return pl.pallas_call(
        body, name=name, out_shape=[jax.ShapeDtypeStruct(p.shape, p.dtype) for p in parts],
        in_specs=[ANY] * n, out_specs=[ANY] * n,
        scratch_shapes=[pltpu.SemaphoreType.DMA((3 * n,)), pltpu.SemaphoreType.DMA((3 * n,)), pltpu.SemaphoreType.DMA((n,))],
    )(*parts)


def _add_sibling(parts, got, name):
    _, r, cc = parts.shape
    tr = _pick(r, (512, 256, 128, 64, 32, 16, 8))
    c = lax.axis_index("c")

    def body(c_ref, p_ref, g_ref, o_ref):
        o_ref[...] = p_ref[...] + g_ref[...]

    return pl.pallas_call(
        body, name=name, out_shape=jax.ShapeDtypeStruct((4, r, cc), F32),
        grid_spec=pltpu.PrefetchScalarGridSpec(
            num_scalar_prefetch=1, grid=(4, r // tr),
            in_specs=[pl.BlockSpec((1, tr, cc), lambda j, i, cr: (2 * j + cr[0], i, 0)),
                      pl.BlockSpec((1, tr, cc), lambda j, i, cr: (j, i, 0))],
            out_specs=pl.BlockSpec((1, tr, cc), lambda j, i, cr: (j, i, 0))),
        compiler_params=_vmem(48),
    )(c.reshape(1).astype(jnp.int32), parts, got)


def _sum_leading(parts, name):
    k, r, cc = parts.shape
    tr = _pick(r, (512, 256, 128, 64, 32, 16, 8))

    def body(p_ref, o_ref):
        acc = p_ref[0]
        for j in range(1, k):
            acc = acc + p_ref[j]
        o_ref[...] = acc

    return pl.pallas_call(
        body, name=name, out_shape=jax.ShapeDtypeStruct((r, cc), F32), grid=(r // tr,),
        in_specs=[pl.BlockSpec((k, tr, cc), lambda i: (0, i, 0))], out_specs=pl.BlockSpec((tr, cc), lambda i: (i, 0)),
    )(parts)


def _adamw_math(w, g, m, v):
    nm = ADAM_B1 * m + (1.0 - ADAM_B1) * g
    nv = ADAM_B2 * v + (1.0 - ADAM_B2) * jnp.square(g)
    m_hat = nm / (1.0 - ADAM_B1 ** ADAM_STEP)
    v_hat = nv / (1.0 - ADAM_B2 ** ADAM_STEP)
    return -ADAM_LR * (m_hat / (jnp.sqrt(v_hat) + ADAM_EPS) + ADAM_WD * w), nm, nv


def _adamw_sum(w, contribs, m, v, name):
    r, cc = w.shape
    k = contribs.shape[0]
    tr = _pick(r, (256, 128, 64, 32, 16, 8))

    def body(w_ref, c_ref, m_ref, v_ref, g_ref, d_ref, nm_ref, nv_ref):
        g = c_ref[0]
        for j in range(1, k):
            g = g + c_ref[j]
        g_ref[...] = g
        d_ref[...], nm_ref[...], nv_ref[...] = _adamw_math(w_ref[...], g, m_ref[...], v_ref[...])

    spec = pl.BlockSpec((tr, cc), lambda i: (i, 0))
    return pl.pallas_call(
        body, name=name, grid=(r // tr,), in_specs=[spec, pl.BlockSpec((k, tr, cc), lambda i: (0, i, 0)), spec, spec],
        out_specs=[spec] * 4, out_shape=[jax.ShapeDtypeStruct((r, cc), F32)] * 4,
        compiler_params=pltpu.CompilerParams(dimension_semantics=("parallel",), vmem_limit_bytes=48 * 1024 * 1024),
    )(w, contribs, m, v)


def _adamw_many(ws, gs, ms, vs, name):
    n = len(ws)

    def body(*refs):
        w_r, g_r, m_r, v_r = refs[:n], refs[n:2 * n], refs[2 * n:3 * n], refs[3 * n:4 * n]
        d_r, nm_r, nv_r = refs[4 * n:5 * n], refs[5 * n:6 * n], refs[6 * n:7 * n]
        for a in range(n):
            d_r[a][...], nm_r[a][...], nv_r[a][...] = _adamw_math(w_r[a][...], g_r[a][...], m_r[a][...], v_r[a][...])

    shapes = [jax.ShapeDtypeStruct(w.shape, F32) for w in ws]
    res = pl.pallas_call(body, name=name, out_shape=shapes * 3,
                         compiler_params=pltpu.CompilerParams(vmem_limit_bytes=56 * 1024 * 1024))(*ws, *gs, *ms, *vs)
    return res[:n], res[n:2 * n], res[2 * n:]


def _pack_rows(flat, lanes, row_mult):
    n = flat.shape[-1]
    rows = -(-n // lanes)
    rows = -(-rows // row_mult) * row_mult
    pad = rows * lanes - n
    if pad:
        flat = jnp.pad(flat, [(0, 0)] * (flat.ndim - 1) + [(0, pad)])
    return flat.reshape(flat.shape[:-1] + (rows, lanes))


def _rope_tables(positions):
    inv = 1.0 / (ROPE_THETA ** (jnp.arange(0, MLA_ROPE, 2, dtype=F32) / MLA_ROPE))
    ang = positions.astype(F32)[:, None] * inv
    cos, sin = jnp.cos(ang), jnp.sin(ang)
    s = positions.shape[0]
    z = lambda n: jnp.zeros((s, n), F32)
    c = jnp.concatenate([jnp.ones((s, 64), F32), cos, cos, z(32)], axis=1)
    sa = jnp.concatenate([z(64), -sin, z(48)], axis=1)
    sb = jnp.concatenate([z(80), sin, z(32)], axis=1)
    return c, sa, sb


def _device_grads(x, positions, target, w, small):
    tabs = _rope_tables(positions)
    qs = [_layer_params(l, w, small) for l in range(DEPTH)]
    h = x
    saves = []
    for l in range(DEPTH):
        h, sv = _layer_fwd(l, h, tabs, qs[l])
        saves.append(sv)
    loss, d = _loss_head(h, target, "loss_head")
    grads = [None] * DEPTH
    for l in reversed(range(DEPTH)):
        d, grads[l] = _layer_bwd(l, d, tabs, qs[l], saves[l])
    return loss[0, 0], d, grads


def kernel(x, positions, norm_g, w_in, mla_q_a_norm, mla_w_q_up, mla_kv_a_norm, mla_w_kv_up, mla_q_norm, mla_k_norm, fox_b_f, fox_q_norm, fox_k_norm, s5_lambda_re, s5_lambda_im, s5_log_dt, s5_b_re, s5_b_im, s5_c_re, s5_c_im, s5_d, s5_w_glu, s5_b_glu, w_branch_out, w_out, loss_target, m_norm_g, m_w_in, m_mla_q_a_norm, m_mla_w_q_up, m_mla_kv_a_norm, m_mla_w_kv_up, m_mla_q_norm, m_mla_k_norm, m_fox_b_f, m_fox_q_norm, m_fox_k_norm, m_s5_lambda_re, m_s5_lambda_im, m_s5_log_dt, m_s5_b_re, m_s5_b_im, m_s5_c_re, m_s5_c_im, m_s5_d, m_s5_w_glu, m_s5_b_glu, m_w_branch_out, m_w_out, v_norm_g, v_w_in, v_mla_q_a_norm, v_mla_w_q_up, v_mla_kv_a_norm, v_mla_w_kv_up, v_mla_q_norm, v_mla_k_norm, v_fox_b_f, v_fox_q_norm, v_fox_k_norm, v_s5_lambda_re, v_s5_lambda_im, v_s5_log_dt, v_s5_b_re, v_s5_b_im, v_s5_c_re, v_s5_c_im, v_s5_d, v_s5_w_glu, v_s5_b_glu, v_w_branch_out, v_w_out):
    env = dict(locals())
    wts = {k: env[k] for k in WEIGHTS}
    mom = {k: env["m_" + k] for k in WEIGHTS}
    var = {k: env["v_" + k] for k in WEIGHTS}

    two_d = {k: (wts[k].shape[0] * wts[k].shape[1], wts[k].shape[2]) for k in SHARDED}
    gathered = _all_gather([wts[k].reshape(two_d[k]).astype(BF16) for k in SHARDED], "gather_weights")
    full = {}
    for k, gk in zip(SHARDED, gathered):
        shp = wts[k].shape
        blk = gk.reshape((N_DEV,) + shp)
        if k in COL_SHARDED:
            full[k] = blk.transpose(1, 2, 0, 3).reshape(shp[0], shp[1], N_DEV * shp[2])
        else:
            full[k] = blk.transpose(1, 0, 2, 3).reshape(shp[0], N_DEV * shp[1], shp[2])

    small = {k: wts[k] for k in SMALL}
    loss, dx, grads = _device_grads(x[0], positions[0], loss_target[0], full, small)
    loss = lax.psum(loss, ("x", "y", "c"))

    big = {"w_in": [_unpad_w_in(g["w_in"]) for g in grads], "mla_w_q_up": [_unpad_wq(g["wq"]) for g in grads],
           "mla_w_kv_up": [_unpad_wkv(g["wkv"]) for g in grads], "s5_w_glu": [g["wglu"] for g in grads],
           "w_branch_out": [g["wo"] for g in grads], "w_out": [g["wout"] for g in grads]}
    parts = []
    for k in SHARDED:
        gk = jnp.stack(big[k])
        lshape = wts[k].shape
        if k in COL_SHARDED:
            gk = gk.reshape(lshape[0], lshape[1], N_DEV, lshape[2]).transpose(2, 0, 1, 3)
        else:
            gk = gk.reshape(lshape[0], N_DEV, lshape[1], lshape[2]).transpose(1, 0, 2, 3)
        parts.append(gk.reshape((N_DEV,) + two_d[k]))
    got = _exchange_sibling(parts, "reduce_sibling")
    chip_sums = [_add_sibling(p, g, f"reduce_sibling_add_{k}") for k, p, g in zip(SHARDED, parts, got)]
    contribs = _exchange_chips(chip_sums, "reduce_chips")
    grad_out, delta_out, m_out, v_out = {}, {}, {}, {}
    for k, ck in zip(SHARDED, contribs):
        shp = wts[k].shape
        res = _adamw_sum(wts[k].reshape(two_d[k]), ck, mom[k].reshape(two_d[k]), var[k].reshape(two_d[k]), f"adamw_{k}")
        grad_out[k], delta_out[k], m_out[k], v_out[k] = (z.reshape(shp) for z in res)

    sm = {k: jnp.stack([g[k] for g in grads]).reshape(wts[k].shape) for k in SMALL}
    flat_s = jnp.concatenate([sm[k].reshape(-1) for k in SMALL])
    g_small = _sum_leading(_all_gather([_pack_rows(flat_s, LANES, 256)], "gather_small_grads")[0], "sum_small_grads").reshape(-1)
    off = 0
    for k in SMALL:
        cnt = int(np.prod(wts[k].shape))
        grad_out[k] = g_small[off:off + cnt].reshape(wts[k].shape)
        off += cnt
    flat2 = lambda a: a.reshape(-1, a.shape[-1])
    d_s, m_s, v_s = _adamw_many([flat2(wts[k]) for k in SMALL], [flat2(grad_out[k]) for k in SMALL],
                                [flat2(mom[k]) for k in SMALL], [flat2(var[k]) for k in SMALL], "adamw_small")
    for i, k in enumerate(SMALL):
        delta_out[k], m_out[k], v_out[k] = (z[i].reshape(wts[k].shape) for z in (d_s, m_s, v_s))

    return (loss, dx[None], *[grad_out[k] for k in WEIGHTS], *[delta_out[k] for k in WEIGHTS],
            *[m_out[k] for k in WEIGHTS], *[v_out[k] for k in WEIGHTS])
```

```python
import functools
import math

import jax
import jax.numpy as jnp
import numpy as np
from jax import lax
from jax.experimental import pallas as pl
from jax.experimental.pallas import tpu as pltpu

F32 = jnp.float32
BF16 = jnp.bfloat16

D_MODEL = 1024
DEPTH = 2
CHUNK = 64
EPS = 1e-6
HEADS = 8
MLA_NOPE, MLA_ROPE, MLA_V = 64, 32, 64
MLA_Q_RANK, MLA_KV_RANK = 256, 128
MLA_QK = MLA_NOPE + MLA_ROPE
ROPE_THETA = 10000.0
FOX_DIM = 64
S5_WIDTH, S5_GROUP, S5_GROUPS, S5_STATE = 512, 16, 32, 64
S5_LANES = S5_GROUPS * S5_STATE
IN_WIDTH = 7080
N_DEV = 8
LANES = 128
SUBLANES = 8

ADAM_LR, ADAM_B1, ADAM_B2, ADAM_EPS, ADAM_WD, ADAM_STEP = 0.001, 0.9, 0.999, 1e-08, 0.01, 10

SEG_W = (512, 1536, 512, 1536, 3072)
SEG_OFF = (0, 512, 2048, 2560, 4096)
PAD_IN = 7168
NEG = -1e30

SHARDED = ("w_in", "mla_w_q_up", "mla_w_kv_up", "s5_w_glu", "w_branch_out", "w_out")
COL_SHARDED = ("w_in", "mla_w_q_up", "mla_w_kv_up")
SMALL = ("norm_g", "mla_q_a_norm", "mla_kv_a_norm", "mla_q_norm", "mla_k_norm", "fox_b_f", "fox_q_norm", "fox_k_norm",
         "s5_lambda_re", "s5_lambda_im", "s5_log_dt", "s5_b_re", "s5_b_im", "s5_c_re", "s5_c_im", "s5_d", "s5_b_glu")
WEIGHTS = ("norm_g", "w_in", "mla_q_a_norm", "mla_w_q_up", "mla_kv_a_norm", "mla_w_kv_up", "mla_q_norm", "mla_k_norm",
           "fox_b_f", "fox_q_norm", "fox_k_norm", "s5_lambda_re", "s5_lambda_im", "s5_log_dt", "s5_b_re", "s5_b_im",
           "s5_c_re", "s5_c_im", "s5_d", "s5_w_glu", "s5_b_glu", "w_branch_out", "w_out")
PACK_LANES = 512


def _pick(n, cands):
    for c in cands:
        if n % c == 0:
            return c
    return n


def _vmem(mb):
    return pltpu.CompilerParams(vmem_limit_bytes=mb * 1024 * 1024)


def _dot(a, b, dims):
    return lax.dot_general(a.astype(BF16), b.astype(BF16), (dims, ((), ())), preferred_element_type=F32)


def _nn(a, b):
    return _dot(a, b, ((1,), (0,)))


def _nt(a, b):
    return _dot(a, b, ((1,), (1,)))


def _tn(a, b):
    return _dot(a, b, ((0,), (0,)))


def _rms(x, g, n=None):
    n = x.shape[-1] if n is None else n
    return x * lax.rsqrt(jnp.sum(x * x, axis=-1, keepdims=True) / n + EPS) * g


def _rope(t, c, sa, sb):
    return t * c + pltpu.roll(t, LANES - 16, 1) * sa + pltpu.roll(t, 16, 1) * sb


def _rope_t(d, c, sa, sb):
    return d * c + pltpu.roll(d * sa, 16, 1) + pltpu.roll(d * sb, LANES - 16, 1)


def _mm(a, b, mode, name, acc=None, scale=None):
    if mode == "tn":
        kd, m = a.shape
    else:
        m, kd = a.shape
    n = b.shape[0] if mode == "nt" else b.shape[1]
    tm, tn, tk = _pick(m, (1024, 512, 256, 128)), _pick(n, (1024, 512, 256, 128)), _pick(kd, (512, 256, 128))
    nk = kd // tk
    if mode == "tn":
        a_spec = pl.BlockSpec((tk, tm), lambda i, j, k: (k, i))
    else:
        a_spec = pl.BlockSpec((tm, tk), lambda i, j, k: (i, k))
    if mode == "nt":
        b_spec = pl.BlockSpec((tn, tk), lambda i, j, k: (j, k))
    else:
        b_spec = pl.BlockSpec((tk, tn), lambda i, j, k: (k, j))
    dims = {"nn": ((1,), (0,)), "nt": ((1,), (1,)), "tn": ((0,), (0,))}[mode]
    o_spec = pl.BlockSpec((tm, tn), lambda i, j, k: (i, j))
    has_acc = acc is not None

    def body(*refs):
        if has_acc:
            a_ref, b_ref, c_ref, o_ref, acc_ref = refs
        else:
            a_ref, b_ref, o_ref, acc_ref = refs
        k = pl.program_id(2)

        @pl.when(k == 0)
        def _():
            acc_ref[...] = c_ref[...] if has_acc else jnp.zeros(acc_ref.shape, F32)

        acc_ref[...] += _dot(a_ref[...], b_ref[...], dims)

        @pl.when(k == nk - 1)
        def _():
            o_ref[...] = acc_ref[...]

    ins = [a, b] + ([acc] if has_acc else [])
    in_specs = [a_spec, b_spec] + ([o_spec] if has_acc else [])
    return pl.pallas_call(
        body, name=name, grid=(m // tm, n // tn, nk), in_specs=in_specs, out_specs=o_spec,
        out_shape=jax.ShapeDtypeStruct((m, n), F32), scratch_shapes=[pltpu.VMEM((tm, tn), F32)],
        compiler_params=pltpu.CompilerParams(dimension_semantics=("parallel", "parallel", "arbitrary"),
                                             vmem_limit_bytes=48 * 1024 * 1024),
    )(*ins)


def _stage(name, fn, n_steps, ins, outs, accs=(), scratch=(), vmem_mb=48):
    n_in, n_out, n_acc = len(ins), len(outs), len(accs)

    def body(*refs):
        in_refs = refs[:n_in]
        out_refs = refs[n_in:n_in + n_out]
        acc_refs = refs[n_in + n_out:n_in + n_out + n_acc]
        scr = refs[n_in + n_out + n_acc:]
        if n_acc:
            @pl.when(pl.program_id(0) == 0)
            def _():
                for r in acc_refs:
                    r[...] = jnp.zeros(r.shape, r.dtype)
        fn(in_refs, out_refs, acc_refs, scr)

    acc_specs = [pl.BlockSpec(a.shape, functools.partial(lambda i, nd: (0,) * nd, nd=len(a.shape))) for a in accs]
    res = pl.pallas_call(
        body, name=name, grid=(n_steps,),
        in_specs=[s for _, s in ins], out_specs=[s for _, s in outs] + acc_specs,
        out_shape=[s for s, _ in outs] + list(accs), scratch_shapes=list(scratch),
        compiler_params=pltpu.CompilerParams(dimension_semantics=("arbitrary",),
                                             vmem_limit_bytes=vmem_mb * 1024 * 1024),
    )(*[a for a, _ in ins])
    return res


def _rows(ts, w, j=0):
    return pl.BlockSpec((ts, w), lambda i: (i, j))


def _rows_rev(ts, w, n, j=0):
    return pl.BlockSpec((ts, w), lambda i: (n - 1 - i, j))


def _heads(ts, d):
    return pl.BlockSpec((HEADS, ts, d), lambda i: (0, i, 0))


def _heads_rev(ts, d, n):
    return pl.BlockSpec((HEADS, ts, d), lambda i: (0, n - 1 - i, 0))


def _full(shape):
    nd = len(shape)
    return pl.BlockSpec(tuple(shape), lambda i: (0,) * nd)


def _sds(shape, dtype=F32):
    return jax.ShapeDtypeStruct(tuple(shape), dtype)


def _norm_fwd(x, g, name):
    s = x.shape[0]
    ts = _pick(s, (256, 128))

    def fn(ins, outs, accs, scr):
        outs[0][...] = _rms(ins[0][...], ins[1][...]).astype(BF16)

    return _stage(name, fn, s // ts, [(x, _rows(ts, D_MODEL)), (g, _full(g.shape))],
                  [(_sds((s, D_MODEL), BF16), _rows(ts, D_MODEL))])[0]


def _norm_bwd(x, g, dh, dres, name):
    s = x.shape[0]
    ts = _pick(s, (256, 128))

    def fn(ins, outs, accs, scr):
        _, vjp = jax.vjp(_rms, ins[0][...], ins[1][...])
        dx, dg = vjp(ins[2][...])
        outs[0][...] = dx + ins[3][...]
        accs[0][...] += dg

    r = _stage(name, fn, s // ts,
               [(x, _rows(ts, D_MODEL)), (g, _full(g.shape)), (dh, _rows(ts, D_MODEL)), (dres, _rows(ts, D_MODEL))],
               [(_sds((s, D_MODEL)), _rows(ts, D_MODEL))], accs=[_sds((1, D_MODEL))])
    return r[0], r[1]


def _mla_q(qraw, c, sa, sb, qn):
    return _rms(_rope(qraw, c, sa, sb), qn, MLA_QK)


def _mla_prep_fwd(seg0, tabs, p, wq, wkv, name):
    s = seg0.shape[0]
    ts = _pick(s, (256, 128))

    def fn(ins, outs, accs, scr):
        blk, cos, sa, sb, qan, kvan, qn, kn, wq_r, wkv_r = ins
        b = blk[...]
        cq, ckv, kt = b[:, :256], b[:, 256:384], b[:, 384:512]
        lane = lax.broadcasted_iota(jnp.int32, kt.shape, 1)
        kpe = jnp.where(lane >= 64, kt, 0.0)
        q_raw = _nn(_rms(cq, qan[...]), wq_r[...])
        kv_raw = _nn(_rms(ckv, kvan[...]), wkv_r[...])
        c, a, bb = cos[...], sa[...], sb[...]
        for h in range(HEADS):
            outs[0][h] = _mla_q(q_raw[:, LANES * h:LANES * (h + 1)], c, a, bb, qn[...]).astype(BF16)
            outs[1][h] = _mla_q(kv_raw[:, LANES * h:LANES * (h + 1)] + kpe, c, a, bb, kn[...]).astype(BF16)
            outs[2][h] = kv_raw[:, 1024 + 64 * h:1024 + 64 * (h + 1)].astype(BF16)

    consts = [p["mla_q_a_norm"], p["mla_kv_a_norm"], p["mla_q_norm"], p["mla_k_norm"], wq, wkv]
    return _stage(name, fn, s // ts,
                  [(seg0, _rows(ts, 512))] + [(t, _rows(ts, LANES)) for t in tabs] + [(a, _full(a.shape)) for a in consts],
                  [(_sds((HEADS, s, LANES), BF16), _heads(ts, LANES)), (_sds((HEADS, s, LANES), BF16), _heads(ts, LANES)),
                   (_sds((HEADS, s, 64), BF16), _heads(ts, 64))])


def _mla_prep_bwd(seg0, tabs, p, wq, wkv, dq, dk, dv, dff, name):
    s = seg0.shape[0]
    ts = _pick(s, (256, 128))

    def fn(ins, outs, accs, scr):
        blk, cos, sa, sb, qan, kvan, qn, kn, wq_r, wkv_r, dq_r, dk_r, dv_r, dff_r = ins
        dqan, dkvan, dqn, dkn, dwq, dwkv = accs
        dqraw_s, dkvraw_s = scr
        b = blk[...]
        cq, ckv, kt = b[:, :256], b[:, 256:384], b[:, 384:512]
        lane = lax.broadcasted_iota(jnp.int32, kt.shape, 1)
        kpe = jnp.where(lane >= 64, kt, 0.0)
        cqn, vjp_cq = jax.vjp(_rms, cq, qan[...])
        ckvn, vjp_ckv = jax.vjp(_rms, ckv, kvan[...])
        q_raw = _nn(cqn, wq_r[...])
        kv_raw = _nn(ckvn, wkv_r[...])
        c, a, bb = cos[...], sa[...], sb[...]

        def head_bwd(raw, gain, d):
            t = _rope(raw, c, a, bb)
            _, vjp = jax.vjp(functools.partial(_rms, n=MLA_QK), t, gain)
            dt, dgain = vjp(d)
            return _rope_t(dt, c, a, bb), dgain

        dkpe = jnp.zeros(kt.shape, F32)
        for h in range(HEADS):
            dqh, dg = head_bwd(q_raw[:, LANES * h:LANES * (h + 1)], qn[...], dq_r[h])
            dqn[...] += dg
            dqraw_s[:, LANES * h:LANES * (h + 1)] = dqh
            dkh, dg = head_bwd(kv_raw[:, LANES * h:LANES * (h + 1)] + kpe, kn[...], dk_r[h])
            dkn[...] += dg
            dkvraw_s[:, LANES * h:LANES * (h + 1)] = dkh
            dkpe = dkpe + dkh
            dkvraw_s[:, 1024 + 64 * h:1024 + 64 * (h + 1)] = dv_r[h]
        dq_raw = dqraw_s[...]
        dkv_raw = dkvraw_s[...]
        dwq[...] += _tn(cqn, dq_raw)
        dwkv[...] += _tn(ckvn, dkv_raw)
        dcq, dg = vjp_cq(_nt(dq_raw, wq_r[...]))
        dqan[...] += dg
        dckv, dg = vjp_ckv(_nt(dkv_raw, wkv_r[...]))
        dkvan[...] += dg
        outs[0][:, 0:256] = dcq.astype(BF16)
        outs[0][:, 256:384] = dckv.astype(BF16)
        outs[0][:, 384:512] = (jnp.where(lane >= 64, dkpe, 0.0) + dff_r[...]).astype(BF16)

    consts = [p["mla_q_a_norm"], p["mla_kv_a_norm"], p["mla_q_norm"], p["mla_k_norm"], wq, wkv]
    return _stage(name, fn, s // ts,
                  [(seg0, _rows(ts, 512))] + [(t, _rows(ts, LANES)) for t in tabs] + [(a, _full(a.shape)) for a in consts]
                  + [(dq, _heads(ts, LANES)), (dk, _heads(ts, LANES)), (dv, _heads(ts, 64)), (dff, _rows(ts, LANES))],
                  [(_sds((s, 512), BF16), _rows(ts, 512))],
                  accs=[_sds((1, 256)), _sds((1, 128)), _sds((1, LANES)), _sds((1, LANES)), _sds(wq.shape), _sds(wkv.shape)],
                  scratch=[pltpu.VMEM((ts, 1024), F32), pltpu.VMEM((ts, 1536), F32)])


def _fox_prep_fwd(seg0, seg1, bf, qn, kn, name):
    s = seg0.shape[0]
    ts = _pick(s, (256, 128))
    steps = int(math.log2(ts))

    def fn(ins, outs, accs, scr):
        kt_r, x_r, bf_r, qn_r, kn_r = ins
        carry = scr[0]

        @pl.when(pl.program_id(0) == 0)
        def _():
            carry[...] = jnp.zeros(carry.shape, F32)

        x = x_r[...]
        for h in range(HEADS):
            outs[0][h] = _rms(x[:, 64 * h:64 * (h + 1)], qn_r[...]).astype(BF16)
            outs[1][h] = _rms(x[:, 512 + 64 * h:512 + 64 * (h + 1)], kn_r[...]).astype(BF16)
            outs[2][h] = x[:, 1024 + 64 * h:1024 + 64 * (h + 1)].astype(BF16)
        kt = kt_r[...]
        lane = lax.broadcasted_iota(jnp.int32, kt.shape, 1)
        row = lax.broadcasted_iota(jnp.int32, kt.shape, 0)
        cs = jnp.where(lane < HEADS, jax.nn.log_sigmoid(kt + bf_r[...]), 0.0)
        for k in range(steps):
            sh = 1 << k
            cs = cs + jnp.where(row >= sh, pltpu.roll(cs, sh, 0), 0.0)
        cs = cs + carry[0:1, :]
        outs[3][...] = cs
        outs[4][...] = cs.T[0:HEADS, :]
        carry[0:1, :] = cs[ts - 1:ts, :]

    return _stage(name, fn, s // ts,
                  [(seg0, _rows(ts, LANES, 3)), (seg1, _rows(ts, 1536)), (bf, _full(bf.shape)), (qn, _full(qn.shape)),
                   (kn, _full(kn.shape))],
                  [(_sds((HEADS, s, 64), BF16), _heads(ts, 64)), (_sds((HEADS, s, 64), BF16), _heads(ts, 64)),
                   (_sds((HEADS, s, 64), BF16), _heads(ts, 64)), (_sds((s, LANES)), _rows(ts, LANES)),
                   (_sds((HEADS, s)), pl.BlockSpec((HEADS, ts), lambda i: (0, i)))],
                  scratch=[pltpu.VMEM((SUBLANES, LANES), F32)])


def _fox_prep_bwd(seg0, seg1, bf, qn, kn, dq, dk, dv, dck, name):
    s = seg0.shape[0]
    ts = _pick(s, (256, 128))
    n = s // ts
    steps = int(math.log2(ts))

    def fn(ins, outs, accs, scr):
        kt_r, x_r, bf_r, qn_r, kn_r, dq_r, dk_r, dv_r, dck_r = ins
        dqn, dkn, dbf = accs
        carry, dbuf = scr

        @pl.when(pl.program_id(0) == 0)
        def _():
            carry[...] = jnp.zeros(carry.shape, F32)

        x = x_r[...]
        for h in range(HEADS):
            _, vjp = jax.vjp(_rms, x[:, 64 * h:64 * (h + 1)], qn_r[...])
            d, dg = vjp(dq_r[h])
            dbuf[:, 64 * h:64 * (h + 1)] = d
            dqn[...] += dg
            _, vjp = jax.vjp(_rms, x[:, 512 + 64 * h:512 + 64 * (h + 1)], kn_r[...])
            d, dg = vjp(dk_r[h])
            dbuf[:, 512 + 64 * h:512 + 64 * (h + 1)] = d
            dkn[...] += dg
            dbuf[:, 1024 + 64 * h:1024 + 64 * (h + 1)] = dv_r[h]
        outs[0][...] = dbuf[...].astype(BF16)
        dc = dck_r[...].reshape(HEADS, ts)
        dc = jnp.concatenate([dc, jnp.zeros((LANES - HEADS, ts), F32)], axis=0).T
        row = lax.broadcasted_iota(jnp.int32, dc.shape, 0)
        lane = lax.broadcasted_iota(jnp.int32, dc.shape, 1)
        for k in range(steps):
            sh = 1 << k
            dc = dc + jnp.where(row < ts - sh, pltpu.roll(dc, ts - sh, 0), 0.0)
        dc = dc + carry[0:1, :]
        carry[0:1, :] = dc[0:1, :]
        dff = jnp.where(lane < HEADS, dc * jax.nn.sigmoid(-(kt_r[...] + bf_r[...])), 0.0)
        outs[1][...] = dff
        dbf[...] += jnp.sum(dff, axis=0, keepdims=True)

    return _stage(name, fn, n,
                  [(seg0, _rows_rev(ts, LANES, n, 3)), (seg1, _rows_rev(ts, 1536, n)), (bf, _full(bf.shape)),
                   (qn, _full(qn.shape)), (kn, _full(kn.shape)), (dq, _heads_rev(ts, 64, n)), (dk, _heads_rev(ts, 64, n)),
                   (dv, _heads_rev(ts, 64, n)), (dck, pl.BlockSpec((HEADS, 1, ts), lambda i: (0, 0, n - 1 - i)))],
                  [(_sds((s, 1536), BF16), _rows_rev(ts, 1536, n)), (_sds((s, LANES)), _rows_rev(ts, LANES, n))],
                  accs=[_sds((1, 64)), _sds((1, 64)), _sds((1, LANES))],
                  scratch=[pltpu.VMEM((SUBLANES, LANES), F32), pltpu.VMEM((ts, 1536), F32)])


def _allowed(i, j, t, chunk_causal):
    qpos = i * t + lax.broadcasted_iota(jnp.int32, (t, t), 0)
    kpos = j * t + lax.broadcasted_iota(jnp.int32, (t, t), 1)
    if chunk_causal:
        return (kpos // CHUNK) <= (qpos // CHUNK)
    return kpos <= qpos


def _pick_col(c_blk, h):
    lane = lax.broadcasted_iota(jnp.int32, c_blk.shape, 1)
    return jnp.sum(jnp.where(lane == h, c_blk, 0.0), axis=1, keepdims=True)


def _attn_fwd(q, k, v, scale, chunk_causal, name, c=None, ct=None, hps=8):
    _, s, dk = q.shape
    dv = v.shape[2]
    t = _pick(s, (256, 128))
    bias = c is not None

    def body(*refs):
        if bias:
            q_ref, k_ref, v_ref, c_ref, ct_ref, o_ref, lse_ref = refs
        else:
            q_ref, k_ref, v_ref, o_ref, lse_ref = refs
        hp, i = pl.program_id(0), pl.program_id(1)
        qb = [q_ref[e] for e in range(hps)]
        cq = [_pick_col(c_ref[...], hp * hps + e) if bias else None for e in range(hps)]

        def step(j, carry, diagonal):
            off = pl.multiple_of(j * t, t)
            out = []
            for e in range(hps):
                m, l, acc = carry[e]
                sc = _nt(qb[e], k_ref[e, pl.ds(off, t), :]) * scale
                if bias:
                    sc = sc + (cq[e] - ct_ref[pl.ds(hp * hps + e, 1), pl.ds(off, t)])
                if diagonal:
                    sc = jnp.where(_allowed(i, j, t, chunk_causal), sc, NEG)
                m_new = jnp.maximum(m, jnp.max(sc, axis=1, keepdims=True))
                pr = jnp.exp(sc - m_new)
                alpha = jnp.exp(m - m_new)
                out.append((m_new, alpha * l + jnp.sum(pr, axis=1, keepdims=True),
                            alpha * acc + _nn(pr, v_ref[e, pl.ds(off, t), :])))
            return tuple(out)

        init = tuple((jnp.full((t, 1), NEG, F32), jnp.zeros((t, 1), F32), jnp.zeros((t, dv), F32)) for _ in range(hps))
        res = step(i, lax.fori_loop(0, i, functools.partial(step, diagonal=False), init), True)
        for e in range(hps):
            m, l, acc = res[e]
            o_ref[e] = acc / l
            lse_ref[e] = m + jnp.log(l)

    ins = [q, k, v] + ([c, ct] if bias else [])
    in_specs = [pl.BlockSpec((hps, t, dk), lambda h, i: (h, i, 0)), pl.BlockSpec((hps, s, dk), lambda h, i: (h, 0, 0)),
                pl.BlockSpec((hps, s, dv), lambda h, i: (h, 0, 0))]
    if bias:
        in_specs += [pl.BlockSpec((t, LANES), lambda h, i: (i, 0)), pl.BlockSpec((HEADS, s), lambda h, i: (0, 0))]
    return pl.pallas_call(
        body, name=name, grid=(HEADS // hps, s // t), in_specs=in_specs,
        out_specs=[pl.BlockSpec((hps, t, dv), lambda h, i: (h, i, 0)), pl.BlockSpec((hps, t, 1), lambda h, i: (h, i, 0))],
        out_shape=[_sds((HEADS, s, dv)), _sds((HEADS, s, 1))],
        compiler_params=pltpu.CompilerParams(dimension_semantics=("parallel", "parallel"), vmem_limit_bytes=48 * 1024 * 1024),
    )(*ins)


def _attn_delta(q, k, v, do, lse, scale, chunk_causal, name, c=None, ct=None, hps=4):
    _, s, dk = q.shape
    dv = v.shape[2]
    t = _pick(s, (256, 128))
    bias = c is not None

    def body(*refs):
        if bias:
            q_ref, k_ref, v_ref, do_ref, lse_ref, c_ref, ct_ref, dl_ref = refs
        else:
            q_ref, k_ref, v_ref, do_ref, lse_ref, dl_ref = refs
        hp, i = pl.program_id(0), pl.program_id(1)
        qb, dob, lse_b = ([r[e] for e in range(hps)] for r in (q_ref, do_ref, lse_ref))
        cq = [_pick_col(c_ref[...], hp * hps + e) if bias else None for e in range(hps)]

        def step(j, acc, diagonal):
            off = pl.multiple_of(j * t, t)
            out = []
            for e in range(hps):
                sc = _nt(qb[e], k_ref[e, pl.ds(off, t), :]) * scale
                if bias:
                    sc = sc + (cq[e] - ct_ref[pl.ds(hp * hps + e, 1), pl.ds(off, t)])
                pr = jnp.exp(sc - lse_b[e])
                if diagonal:
                    pr = jnp.where(_allowed(i, j, t, chunk_causal), pr, 0.0)
                out.append(acc[e] + jnp.sum(pr * _nt(dob[e], v_ref[e, pl.ds(off, t), :]), axis=1, keepdims=True))
            return tuple(out)

        init = tuple(jnp.zeros((t, 1), F32) for _ in range(hps))
        res = step(i, lax.fori_loop(0, i, functools.partial(step, diagonal=False), init), True)
        for e in range(hps):
            dl_ref[e] = res[e]

    ins = [q, k, v, do, lse] + ([c, ct] if bias else [])
    in_specs = [pl.BlockSpec((hps, t, dk), lambda h, i: (h, i, 0)), pl.BlockSpec((hps, s, dk), lambda h, i: (h, 0, 0)),
                pl.BlockSpec((hps, s, dv), lambda h, i: (h, 0, 0)), pl.BlockSpec((hps, t, dv), lambda h, i: (h, i, 0)),
                pl.BlockSpec((hps, t, 1), lambda h, i: (h, i, 0))]
    if bias:
        in_specs += [pl.BlockSpec((t, LANES), lambda h, i: (i, 0)), pl.BlockSpec((HEADS, s), lambda h, i: (0, 0))]
    return pl.pallas_call(
        body, name=name, grid=(HEADS // hps, s // t), in_specs=in_specs,
        out_specs=pl.BlockSpec((hps, t, 1), lambda h, i: (h, i, 0)), out_shape=_sds((HEADS, s, 1)),
        compiler_params=pltpu.CompilerParams(dimension_semantics=("parallel", "parallel"), vmem_limit_bytes=48 * 1024 * 1024),
    )(*ins)


def _attn_bwd(q, k, v, do, lse, delta, scale, chunk_causal, name, c=None, ct=None, hps=8):
    _, s, dk = q.shape
    dv = v.shape[2]
    t = _pick(s, (256, 128))
    n = s // t
    bias = c is not None

    def body(*refs):
        if bias:
            q_ref, k_ref, v_ref, do_ref, lse_ref, dl_ref, c_ref, ct_ref, dq_ref, dk_ref, dv_ref, dck_ref = refs
        else:
            q_ref, k_ref, v_ref, do_ref, lse_ref, dl_ref, dq_ref, dk_ref, dv_ref = refs
        hp, j = pl.program_id(0), pl.program_id(1)

        @pl.when(j == 0)
        def _():
            dq_ref[...] = jnp.zeros(dq_ref.shape, F32)

        kb, vb = [k_ref[e] for e in range(hps)], [v_ref[e] for e in range(hps)]
        joff = pl.multiple_of(j * t, t)
        ck = [ct_ref[pl.ds(hp * hps + e, 1), pl.ds(joff, t)] if bias else None for e in range(hps)]

        def step(i, carry, diagonal):
            off = pl.multiple_of(i * t, t)
            out = []
            for e in range(hps):
                dk_acc, dv_acc, dck_acc = carry[e]
                qb = q_ref[e, pl.ds(off, t), :]
                dob = do_ref[e, pl.ds(off, t), :]
                sc = _nt(qb, kb[e]) * scale
                if bias:
                    sc = sc + (_pick_col(c_ref[pl.ds(off, t), :], hp * hps + e) - ck[e])
                pr = jnp.exp(sc - lse_ref[e, pl.ds(off, t), :])
                if diagonal:
                    pr = jnp.where(_allowed(i, j, t, chunk_causal), pr, 0.0)
                dv_acc = dv_acc + _tn(pr, dob)
                ds = pr * (_nt(dob, vb[e]) - dl_ref[e, pl.ds(off, t), :])
                dq_ref[e, pl.ds(off, t), :] += _nn(ds, kb[e]) * scale
                dk_acc = dk_acc + _tn(ds, qb) * scale
                if bias:
                    dck_acc = dck_acc - jnp.sum(ds, axis=0, keepdims=True)
                out.append((dk_acc, dv_acc, dck_acc))
            return tuple(out)

        zero = tuple((jnp.zeros((t, dk), F32), jnp.zeros((t, dv), F32), jnp.zeros((1, t), F32)) for _ in range(hps))
        res = lax.fori_loop(j + 1, n, functools.partial(step, diagonal=False), step(j, zero, True))
        for e in range(hps):
            dk_ref[e], dv_ref[e] = res[e][0], res[e][1]
            if bias:
                dck_ref[e] = res[e][2]

    ins = [q, k, v, do, lse, delta] + ([c, ct] if bias else [])
    full = lambda d: pl.BlockSpec((hps, s, d), lambda h, j: (h, 0, 0))
    blk = lambda d: pl.BlockSpec((hps, t, d), lambda h, j: (h, j, 0))
    in_specs = [full(dk), blk(dk), blk(dv), full(dv), full(1), full(1)]
    if bias:
        in_specs += [pl.BlockSpec((s, LANES), lambda h, j: (0, 0)), pl.BlockSpec((HEADS, s), lambda h, j: (0, 0))]
    out_specs = [full(dk), blk(dk), blk(dv)]
    out_shape = [_sds((HEADS, s, dk)), _sds((HEADS, s, dk)), _sds((HEADS, s, dv))]
    if bias:
        out_specs.append(pl.BlockSpec((hps, 1, t), lambda h, j: (h, 0, j)))
        out_shape.append(_sds((HEADS, 1, s)))
    return pl.pallas_call(
        body, name=name, grid=(HEADS // hps, n), in_specs=in_specs, out_specs=out_specs, out_shape=out_shape,
        compiler_params=pltpu.CompilerParams(dimension_semantics=("parallel", "arbitrary"), vmem_limit_bytes=56 * 1024 * 1024),
    )(*ins)


def _s5_disc(lr, li, ldt, br, bi):
    dt = jnp.exp(ldt)
    mag = jnp.exp(lr * dt)
    a_re = mag * jnp.cos(li * dt)
    a_im = mag * jnp.sin(li * dt)
    den = lr * lr + li * li
    f_re = ((a_re - 1.0) * lr + a_im * li) / den
    f_im = (a_im * lr - (a_re - 1.0) * li) / den
    return a_re, a_im, f_re * br - f_im * bi, f_re * bi + f_im * br


def _s5_params_fwd(lr16, li16, ldt16, br2, bi2, name):
    def body(a, b, c, d, e, o0, o1, o2, o3):
        r = _s5_disc(a[...], b[...], c[...], d[...], e[...])
        o0[...], o1[...], o2[...], o3[...] = r

    return pl.pallas_call(body, name=name, out_shape=[_sds((512, 64))] * 4)(lr16, li16, ldt16, br2, bi2)


def _s5_params_bwd(lr16, li16, ldt16, br2, bi2, da_re16, da_im16, dbb_re, dbb_im, name):
    def body(a, b, c, d, e, g0, g1, g2, g3, o_lr, o_li, o_dt, o_br, o_bi):
        _, vjp = jax.vjp(_s5_disc, a[...], b[...], c[...], d[...], e[...])
        dlr, dli, dldt, dbr, dbi = vjp((g0[...], g1[...], g2[...], g3[...]))
        grp = lambda z: z.reshape(S5_GROUPS, S5_GROUP, S5_STATE).sum(axis=1)
        o_lr[...] = grp(dlr)
        o_li[...] = grp(dli)
        o_dt[...] = jnp.sum(grp(dldt), axis=1, keepdims=True)
        o_br[...] = dbr
        o_bi[...] = dbi

    return pl.pallas_call(
        body, name=name, out_shape=[_sds((32, 64)), _sds((32, 64)), _sds((32, 1)), _sds((512, 64)), _sds((512, 64))],
    )(lr16, li16, ldt16, br2, bi2, da_re16, da_im16, dbb_re, dbb_im)


def _cmul(ar, ai, br, bi):
    return ar * br - ai * bi, ar * bi + ai * br


def _scan(bre, bim, a_re, a_im, reverse, name, x_re=None, x_im=None):
    s, w = bre.shape
    lw = _pick(w, (256, 128))
    nt = s // SUBLANES
    with_da = x_re is not None

    def body(*refs):
        if with_da:
            bre_r, bim_r, are_r, aim_r, xre_r, xim_r, ore_r, oim_r, dare_r, daim_r = refs
        else:
            bre_r, bim_r, are_r, aim_r, ore_r, oim_r = refs
        ar, ai = are_r[...], aim_r[...]
        shp = (SUBLANES, lw)
        row = lax.broadcasted_iota(jnp.int32, shp, 0)
        pows = [(ar, ai)]
        for _ in range(SUBLANES - 1):
            pows.append(_cmul(pows[-1][0], pows[-1][1], ar, ai))
        cm_r, cm_i = jnp.zeros(shp, F32), jnp.zeros(shp, F32)
        for r in range(SUBLANES):
            e = (SUBLANES - 1 - r) if reverse else r
            cm_r = jnp.where(row == r, jnp.broadcast_to(pows[e][0], shp), cm_r)
            cm_i = jnp.where(row == r, jnp.broadcast_to(pows[e][1], shp), cm_i)
        steps = [(1, pows[0]), (2, pows[1]), (4, pows[3])]

        def tile(it, carry):
            if with_da:
                c_r, c_i, acc_r, acc_i = carry
            else:
                c_r, c_i = carry
            r = (nt - 1 - it) if reverse else it
            off = pl.multiple_of(r * SUBLANES, SUBLANES)
            xr, xi = bre_r[pl.ds(off, SUBLANES), :], bim_r[pl.ds(off, SUBLANES), :]
            for sh, (pr, pi) in steps:
                if reverse:
                    keep = row < SUBLANES - sh
                    sr = jnp.where(keep, pltpu.roll(xr, SUBLANES - sh, 0), 0.0)
                    si = jnp.where(keep, pltpu.roll(xi, SUBLANES - sh, 0), 0.0)
                else:
                    keep = row >= sh
                    sr = jnp.where(keep, pltpu.roll(xr, sh, 0), 0.0)
                    si = jnp.where(keep, pltpu.roll(xi, sh, 0), 0.0)
                mr, mi = _cmul(pr, pi, sr, si)
                xr, xi = xr + mr, xi + mi
            mr, mi = _cmul(cm_r, cm_i, c_r, c_i)
            xr, xi = xr + mr, xi + mi
            ore_r[pl.ds(off, SUBLANES), :] = xr
            oim_r[pl.ds(off, SUBLANES), :] = xi
            edge = 0 if reverse else SUBLANES - 1
            c_r, c_i = xr[edge:edge + 1, :], xi[edge:edge + 1, :]
            if not with_da:
                return c_r, c_i
            fr, fi = xre_r[pl.ds(off, SUBLANES), :], xim_r[pl.ds(off, SUBLANES), :]
            poff = pl.multiple_of(jnp.maximum(r - 1, 0) * SUBLANES, SUBLANES)
            live = (r > 0).astype(F32)
            pr_last = xre_r[pl.ds(poff, SUBLANES), :][SUBLANES - 1:SUBLANES, :] * live
            pi_last = xim_r[pl.ds(poff, SUBLANES), :][SUBLANES - 1:SUBLANES, :] * live
            sr = jnp.where(row >= 1, pltpu.roll(fr, 1, 0), jnp.broadcast_to(pr_last, shp))
            si = jnp.where(row >= 1, pltpu.roll(fi, 1, 0), jnp.broadcast_to(pi_last, shp))
            acc_r = acc_r + xr * sr + xi * si
            acc_i = acc_i + xi * sr - xr * si
            return c_r, c_i, acc_r, acc_i

        z1 = jnp.zeros((1, lw), F32)
        if with_da:
            _, _, acc_r, acc_i = lax.fori_loop(0, nt, tile, (z1, z1, jnp.zeros(shp, F32), jnp.zeros(shp, F32)))
            dare_r[...] = jnp.sum(acc_r, axis=0, keepdims=True)
            daim_r[...] = jnp.sum(acc_i, axis=0, keepdims=True)
        else:
            lax.fori_loop(0, nt, tile, (z1, z1))

    col = pl.BlockSpec((s, lw), lambda j: (0, j))
    one = pl.BlockSpec((1, lw), lambda j: (0, j))
    ins = [bre, bim, a_re, a_im] + ([x_re, x_im] if with_da else [])
    in_specs = [col, col, one, one] + ([col, col] if with_da else [])
    out_specs = [col, col] + ([one, one] if with_da else [])
    out_shape = [_sds((s, w)), _sds((s, w))] + ([_sds((1, w)), _sds((1, w))] if with_da else [])
    return pl.pallas_call(
        body, name=name, grid=(w // lw,), in_specs=in_specs, out_specs=out_specs, out_shape=out_shape,
        compiler_params=pltpu.CompilerParams(dimension_semantics=("parallel",), vmem_limit_bytes=56 * 1024 * 1024),
    )(*ins)


S5_SUPER = 4


def _s5_mm(a, w4, mode, name, acc=None):
    s = a.shape[0]
    _, r, c = w4.shape
    wa, wo = (r, c) if mode == "nn" else (c, r)
    tm = _pick(s, (512, 256, 128))
    has_acc = acc is not None

    def body(*refs):
        if has_acc:
            a_ref, w_ref, c_ref, o_ref = refs
        else:
            a_ref, w_ref, o_ref = refs
        res = _nn(a_ref[...], w_ref[0]) if mode == "nn" else _nt(a_ref[...], w_ref[0])
        o_ref[...] = res + c_ref[...] if has_acc else res

    o_spec = pl.BlockSpec((tm, wo), lambda k, i: (i, k))
    ins = [a, w4] + ([acc] if has_acc else [])
    in_specs = [pl.BlockSpec((tm, wa), lambda k, i: (i, k)), pl.BlockSpec((1, r, c), lambda k, i: (k, 0, 0))]
    return pl.pallas_call(
        body, name=name, grid=(S5_SUPER, s // tm), in_specs=in_specs + ([o_spec] if has_acc else []), out_specs=o_spec,
        out_shape=_sds((s, S5_SUPER * wo)),
        compiler_params=pltpu.CompilerParams(dimension_semantics=("parallel", "parallel"), vmem_limit_bytes=48 * 1024 * 1024),
    )(*ins)


def _s5_group_tn(a, b, name):
    s = a.shape[0]
    tk = _pick(s, (512, 256, 128))
    nk = s // tk

    def body(a_ref, b_ref, o_ref, acc_ref):
        k = pl.program_id(1)

        @pl.when(k == 0)
        def _():
            acc_ref[...] = jnp.zeros(acc_ref.shape, F32)

        acc_ref[...] += _tn(a_ref[...], b_ref[...])

        @pl.when(k == nk - 1)
        def _():
            p = acc_ref[...]
            grp = lax.broadcasted_iota(jnp.int32, (LANES, S5_STATE), 0) // S5_GROUP
            out = jnp.zeros((LANES, S5_STATE), F32)
            for j in range(LANES // S5_GROUP):
                out = jnp.where(grp == j, p[:, S5_STATE * j:S5_STATE * (j + 1)], out)
            o_ref[...] = out

    return pl.pallas_call(
        body, name=name, grid=(S5_SUPER, nk),
        in_specs=[pl.BlockSpec((tk, LANES), lambda g, k: (k, g)), pl.BlockSpec((tk, 512), lambda g, k: (k, g))],
        out_specs=pl.BlockSpec((LANES, S5_STATE), lambda g, k: (g, 0)), out_shape=_sds((512, S5_STATE)),
        scratch_shapes=[pltpu.VMEM((LANES, 512), F32)],
        compiler_params=pltpu.CompilerParams(dimension_semantics=("parallel", "arbitrary")),
    )(a, b)


def _s5_seg1(y0, u, d):
    return jax.nn.gelu(y0 + d * u)


def _s5_seg2(z, t, b):
    return z * jax.nn.sigmoid(t + b)


def _s5_post_fwd(y0, seg2, d, wglu, bglu, name):
    s = y0.shape[0]
    ts = _pick(s, (256, 128))

    def fn(ins, outs, accs, scr):
        z = _s5_seg1(ins[0][...], ins[1][...], ins[2][...])
        outs[0][...] = _s5_seg2(z, _nn(z, ins[3][...]), ins[4][...])

    return _stage(name, fn, s // ts,
                  [(y0, _rows(ts, 512)), (seg2, _rows(ts, 512)), (d, _full(d.shape)), (wglu, _full(wglu.shape)),
                   (bglu, _full(bglu.shape))], [(_sds((s, 512)), _rows(ts, 512))])[0]


def _s5_post_bwd(y0, seg2, d, wglu, bglu, dy, name):
    s = y0.shape[0]
    ts = _pick(s, (256, 128))

    def fn(ins, outs, accs, scr):
        y0_r, u_r, d_r, w_r, b_r, dy_r = ins
        z, vjp1 = jax.vjp(_s5_seg1, y0_r[...], u_r[...], d_r[...])
        t = _nn(z, w_r[...])
        _, vjp2 = jax.vjp(_s5_seg2, z, t, b_r[...])
        dz, dt, db = vjp2(dy_r[...])
        accs[0][...] += _tn(z, dt)
        accs[1][...] += db
        dy0, du, dd = vjp1(dz + _nt(dt, w_r[...]))
        accs[2][...] += dd
        outs[0][...] = dy0
        outs[1][...] = du

    return _stage(name, fn, s // ts,
                  [(y0, _rows(ts, 512)), (seg2, _rows(ts, 512)), (d, _full(d.shape)), (wglu, _full(wglu.shape)),
                   (bglu, _full(bglu.shape)), (dy, _rows(ts, 512))],
                  [(_sds((s, 512)), _rows(ts, 512)), (_sds((s, 512)), _rows(ts, 512))],
                  accs=[_sds((512, 512)), _sds((1, 512)), _sds((1, 512))])


def _gate_a(y, g):
    return y * jax.nn.silu(g)


def _gate_m(o0, o1, o2, m0, m1, m2):
    return jax.nn.sigmoid(m0) * o0 + jax.nn.sigmoid(m1) * o1 + jax.nn.sigmoid(m2) * o2


def _assemble(ybuf, o_mla, o_fox, y_s5):
    for h in range(HEADS):
        ybuf[:, 64 * h:64 * (h + 1)] = o_mla[h]
        ybuf[:, 512 + 64 * h:512 + 64 * (h + 1)] = o_fox[h]
    ybuf[:, 1024:1536] = y_s5[...]


def _gate_fwd(o_mla, o_fox, y_s5, seg3, seg4, x, wo, wout, name):
    s = x.shape[0]
    ts = _pick(s, (256, 128))

    def fn(ins, outs, accs, scr):
        om, of, ys, g_r, m_r, x_r, wo_r, wout_r = ins
        ybuf = scr[0]
        _assemble(ybuf, om, of, ys)
        a = _gate_a(ybuf[...], g_r[...])
        o = [_nn(a[:, 512 * b:512 * (b + 1)], wo_r[512 * b:512 * (b + 1), :]) for b in range(3)]
        merged = _gate_m(o[0], o[1], o[2], m_r[:, 0:1024], m_r[:, 1024:2048], m_r[:, 2048:3072])
        outs[0][...] = x_r[...] + _nn(merged, wout_r[...])

    return _stage(name, fn, s // ts,
                  [(o_mla, _heads(ts, 64)), (o_fox, _heads(ts, 64)), (y_s5, _rows(ts, 512)), (seg3, _rows(ts, 1536)),
                   (seg4, _rows(ts, 3072)), (x, _rows(ts, D_MODEL)), (wo, _full(wo.shape)), (wout, _full(wout.shape))],
                  [(_sds((s, D_MODEL)), _rows(ts, D_MODEL))], scratch=[pltpu.VMEM((ts, 1536), F32)])[0]


def _gate_bwd(o_mla, o_fox, y_s5, seg3, seg4, wo, wout, dout, name):
    s = dout.shape[0]
    ts = _pick(s, (128,))

    def fn(ins, outs, accs, scr):
        om, of, ys, g_r, m_r, wo_r, wout_r, dout_r = ins
        do_mla, do_fox, dys, dg_r, dm_r = outs
        dwo, dwout = accs
        ybuf, dabuf = scr
        _assemble(ybuf, om, of, ys)
        a, vjp_a = jax.vjp(_gate_a, ybuf[...], g_r[...])
        o = [_nn(a[:, 512 * b:512 * (b + 1)], wo_r[512 * b:512 * (b + 1), :]) for b in range(3)]
        ms = [m_r[:, 1024 * b:1024 * (b + 1)] for b in range(3)]
        merged, vjp_m = jax.vjp(_gate_m, *o, *ms)
        dout_v = dout_r[...]
        dwout[...] += _tn(merged, dout_v)
        cts = vjp_m(_nt(dout_v, wout_r[...]))
        for b in range(3):
            dm_r[:, 1024 * b:1024 * (b + 1)] = cts[3 + b].astype(BF16)
            dwo[512 * b:512 * (b + 1), :] += _tn(a[:, 512 * b:512 * (b + 1)], cts[b])
            dabuf[:, 512 * b:512 * (b + 1)] = _nt(cts[b], wo_r[512 * b:512 * (b + 1), :])
        dy, dg = vjp_a(dabuf[...])
        dg_r[...] = dg.astype(BF16)
        dys[...] = dy[:, 1024:1536]
        for h in range(HEADS):
            do_mla[h] = dy[:, 64 * h:64 * (h + 1)]
            do_fox[h] = dy[:, 512 + 64 * h:512 + 64 * (h + 1)]

    return _stage(name, fn, s // ts,
                  [(o_mla, _heads(ts, 64)), (o_fox, _heads(ts, 64)), (y_s5, _rows(ts, 512)), (seg3, _rows(ts, 1536)),
                   (seg4, _rows(ts, 3072)), (wo, _full(wo.shape)), (wout, _full(wout.shape)), (dout, _rows(ts, D_MODEL))],
                  [(_sds((HEADS, s, 64)), _heads(ts, 64)), (_sds((HEADS, s, 64)), _heads(ts, 64)), (_sds((s, 512)), _rows(ts, 512)),
                   (_sds((s, 1536), BF16), _rows(ts, 1536)), (_sds((s, 3072), BF16), _rows(ts, 3072))],
                  accs=[_sds(wo.shape), _sds(wout.shape)], scratch=[pltpu.VMEM((ts, 1536), F32), pltpu.VMEM((ts, 1536), F32)],
                  vmem_mb=56)


def _loss_head(y, target, name):
    s = y.shape[0]
    ts = _pick(s, (256, 128))

    def fn(ins, outs, accs, scr):
        e = ins[0][...] - ins[1][...]
        outs[0][...] = e / D_MODEL
        accs[0][...] += 0.5 * jnp.sum(jnp.sum(e * e, axis=1, keepdims=True) / D_MODEL, axis=0, keepdims=True)

    r = _stage(name, fn, s // ts, [(y, _rows(ts, D_MODEL)), (target, _rows(ts, D_MODEL))],
               [(_sds((s, D_MODEL)), _rows(ts, D_MODEL))], accs=[_sds((1, 1))])
    return r[1], r[0]


def _pad_w_in(w):
    z = lambda n: jnp.zeros((w.shape[0], n), w.dtype)
    return jnp.concatenate([w[:, 0:384], w[:, 1952:1960], z(56), w[:, 384:416], z(32), w[:, 416:1952], w[:, 1960:IN_WIDTH]], axis=1)


def _unpad_w_in(d):
    return jnp.concatenate([d[:, 0:384], d[:, 448:480], d[:, 512:2048], d[:, 384:392], d[:, 2048:PAD_IN]], axis=1)


def _pad_wq(w):
    w = w.reshape(MLA_Q_RANK, HEADS, MLA_QK)
    return jnp.pad(w, ((0, 0), (0, 0), (0, LANES - MLA_QK))).reshape(MLA_Q_RANK, HEADS * LANES)


def _unpad_wq(d):
    return d.reshape(MLA_Q_RANK, HEADS, LANES)[:, :, :MLA_QK].reshape(MLA_Q_RANK, HEADS * MLA_QK)


def _pad_wkv(w):
    w = w.reshape(MLA_KV_RANK, HEADS, MLA_NOPE + MLA_V)
    k = jnp.pad(w[:, :, :MLA_NOPE], ((0, 0), (0, 0), (0, LANES - MLA_NOPE))).reshape(MLA_KV_RANK, HEADS * LANES)
    return jnp.concatenate([k, w[:, :, MLA_NOPE:].reshape(MLA_KV_RANK, HEADS * MLA_V)], axis=1)


def _unpad_wkv(d):
    k = d[:, :HEADS * LANES].reshape(MLA_KV_RANK, HEADS, LANES)[:, :, :MLA_NOPE]
    v = d[:, HEADS * LANES:].reshape(MLA_KV_RANK, HEADS, MLA_V)
    return jnp.concatenate([k, v], axis=2).reshape(MLA_KV_RANK, HEADS * (MLA_NOPE + MLA_V))


def _pad_lanes(v, n=LANES):
    return jnp.pad(v, (0, n - v.shape[0])).reshape(1, n)


def _super_blocks(b):
    _, r, c = b.shape
    per = S5_GROUPS // S5_SUPER
    b = b.reshape(S5_SUPER, per, r, c)
    eye = jnp.eye(per, dtype=b.dtype)
    return (b[:, :, :, None, :] * eye[None, :, None, :, None]).reshape(S5_SUPER, per * r, per * c)


def _layer_params(l, w, small):
    p = {k: small[k][l] for k in small}
    q = {}
    q["norm_g"] = p["norm_g"].reshape(1, D_MODEL)
    q["mla_q_a_norm"] = p["mla_q_a_norm"].reshape(1, 256)
    q["mla_kv_a_norm"] = p["mla_kv_a_norm"].reshape(1, 128)
    q["mla_q_norm"] = _pad_lanes(p["mla_q_norm"])
    q["mla_k_norm"] = _pad_lanes(p["mla_k_norm"])
    q["fox_b_f"] = _pad_lanes(p["fox_b_f"])
    q["fox_q_norm"] = p["fox_q_norm"].reshape(1, 64)
    q["fox_k_norm"] = p["fox_k_norm"].reshape(1, 64)
    q["s5_d"] = p["s5_d"].reshape(1, 512)
    q["s5_b_glu"] = p["s5_b_glu"].reshape(1, 512)
    rep = lambda z: jnp.repeat(z, S5_GROUP, axis=0)
    q["lr16"], q["li16"] = rep(p["s5_lambda_re"]), rep(p["s5_lambda_im"])
    q["ldt16"] = rep(jnp.broadcast_to(p["s5_log_dt"][:, None], (S5_GROUPS, S5_STATE)))
    q["br2"] = p["s5_b_re"].transpose(0, 2, 1).reshape(512, 64)
    q["bi2"] = p["s5_b_im"].transpose(0, 2, 1).reshape(512, 64)
    q["c_re"], q["c_im"] = p["s5_c_re"], p["s5_c_im"]
    win = _pad_w_in(w["w_in"][l])
    q["w_seg"] = [win[:, SEG_OFF[k]:SEG_OFF[k] + SEG_W[k]] for k in range(5)]
    q["wq"] = _pad_wq(w["mla_w_q_up"][l])
    q["wkv"] = _pad_wkv(w["mla_w_kv_up"][l])
    q["wglu"] = w["s5_w_glu"][l]
    q["wo"] = w["w_branch_out"][l]
    q["wout"] = w["w_out"][l]
    return q


def _layer_fwd(l, x, tabs, q):
    n = lambda s: f"l{l}_{s}"
    sv = {"x": x}
    h = _norm_fwd(x, q["norm_g"], n("norm_fwd"))
    sv["h"] = h
    seg = [_mm(h, q["w_seg"][k], "nn", n(f"proj{k}")) for k in range(5)]
    sv["seg"] = seg
    mq, mk, mv = _mla_prep_fwd(seg[0], tabs, q, q["wq"], q["wkv"], n("mla_prep_fwd"))
    o_mla, lse_mla = _attn_fwd(mq, mk, mv, 1.0 / math.sqrt(MLA_QK), True, n("mla_attn_fwd"))
    sv.update(mq=mq, mk=mk, mv=mv, o_mla=o_mla, lse_mla=lse_mla)
    fq, fk, fv, c, ct = _fox_prep_fwd(seg[0], seg[1], q["fox_b_f"], q["fox_q_norm"], q["fox_k_norm"], n("fox_prep_fwd"))
    o_fox, lse_fox = _attn_fwd(fq, fk, fv, 1.0 / math.sqrt(FOX_DIM), False, n("fox_attn_fwd"), c=c, ct=ct)
    sv.update(fq=fq, fk=fk, fv=fv, c=c, ct=ct, o_fox=o_fox, lse_fox=lse_fox)
    a_re16, a_im16, bb_re, bb_im = _s5_params_fwd(q["lr16"], q["li16"], q["ldt16"], q["br2"], q["bi2"], n("s5_params_fwd"))
    a_re = a_re16.reshape(S5_GROUPS, S5_GROUP, S5_STATE)[:, 0, :].reshape(1, S5_LANES)
    a_im = a_im16.reshape(S5_GROUPS, S5_GROUP, S5_STATE)[:, 0, :].reshape(1, S5_LANES)
    wb_re = _super_blocks(bb_re.reshape(S5_GROUPS, S5_GROUP, S5_STATE)).astype(BF16)
    wb_im = _super_blocks(bb_im.reshape(S5_GROUPS, S5_GROUP, S5_STATE)).astype(BF16)
    wc_re = _super_blocks(q["c_re"].transpose(0, 2, 1)).astype(BF16)
    wc_im = _super_blocks(-q["c_im"].transpose(0, 2, 1)).astype(BF16)
    bu_re = _s5_mm(seg[2], wb_re, "nn", n("s5_bu_re"))
    bu_im = _s5_mm(seg[2], wb_im, "nn", n("s5_bu_im"))
    x_re, x_im = _scan(bu_re, bu_im, a_re, a_im, False, n("s5_scan_fwd"))
    y0 = _s5_mm(x_im, wc_im, "nn", n("s5_y_im"), acc=_s5_mm(x_re, wc_re, "nn", n("s5_y_re")))
    y_s5 = _s5_post_fwd(y0, seg[2], q["s5_d"], q["wglu"], q["s5_b_glu"], n("s5_post_fwd"))
    sv.update(a_re=a_re, a_im=a_im, wb_re=wb_re, wb_im=wb_im, wc_re=wc_re, wc_im=wc_im, x_re=x_re, x_im=x_im, y0=y0, y_s5=y_s5)
    out = _gate_fwd(o_mla, o_fox, y_s5, seg[3], seg[4], x, q["wo"], q["wout"], n("gate_fwd"))
    return out, sv


def _layer_bwd(l, dout, tabs, q, sv):
    n = lambda s: f"l{l}_{s}"
    seg = sv["seg"]
    g = {}
    (do_mla, do_fox, dy_s5, dseg3, dseg4, g["wo"], g["wout"]) = _gate_bwd(
        sv["o_mla"], sv["o_fox"], sv["y_s5"], seg[3], seg[4], q["wo"], q["wout"], dout, n("gate_bwd"))
    dy0, du_a, g["wglu"], g["s5_b_glu"], g["s5_d"] = _s5_post_bwd(sv["y0"], seg[2], q["s5_d"], q["wglu"], q["s5_b_glu"], dy_s5,
                                                                 n("s5_post_bwd"))
    dx_re = _s5_mm(dy0, sv["wc_re"], "nt", n("s5_dx_re"))
    dx_im = _s5_mm(dy0, sv["wc_im"], "nt", n("s5_dx_im"))
    g["s5_c_re"] = _s5_group_tn(dy0, sv["x_re"], n("s5_dc_re")).reshape(S5_GROUPS, S5_GROUP, S5_STATE)
    g["s5_c_im"] = -_s5_group_tn(dy0, sv["x_im"], n("s5_dc_im")).reshape(S5_GROUPS, S5_GROUP, S5_STATE)
    g_re, g_im, da_re, da_im = _scan(dx_re, dx_im, sv["a_re"], -sv["a_im"], True, n("s5_scan_bwd"), x_re=sv["x_re"], x_im=sv["x_im"])
    dseg2 = _s5_mm(g_im, sv["wb_im"], "nt", n("s5_du_im"), acc=_s5_mm(g_re, sv["wb_re"], "nt", n("s5_du_re"), acc=du_a))
    dbb_re = _s5_group_tn(seg[2], g_re, n("s5_dbb_re"))
    dbb_im = _s5_group_tn(seg[2], g_im, n("s5_dbb_im"))
    first = (jnp.arange(512) % S5_GROUP == 0).astype(F32)[:, None]
    da_re16 = jnp.repeat(da_re.reshape(S5_GROUPS, S5_STATE), S5_GROUP, axis=0) * first
    da_im16 = jnp.repeat(da_im.reshape(S5_GROUPS, S5_STATE), S5_GROUP, axis=0) * first
    dlr, dli, dldt, dbr2, dbi2 = _s5_params_bwd(q["lr16"], q["li16"], q["ldt16"], q["br2"], q["bi2"], da_re16, da_im16, dbb_re,
                                               dbb_im, n("s5_params_bwd"))
    g["s5_lambda_re"], g["s5_lambda_im"], g["s5_log_dt"] = dlr, dli, dldt.reshape(S5_GROUPS)
    g["s5_b_re"] = dbr2.reshape(S5_GROUPS, S5_GROUP, S5_STATE).transpose(0, 2, 1)
    g["s5_b_im"] = dbi2.reshape(S5_GROUPS, S5_GROUP, S5_STATE).transpose(0, 2, 1)
    dl_fox = _attn_delta(sv["fq"], sv["fk"], sv["fv"], do_fox, sv["lse_fox"], 1.0 / math.sqrt(FOX_DIM), False,
                         n("fox_attn_delta"), c=sv["c"], ct=sv["ct"])
    dfq, dfk, dfv, dck = _attn_bwd(sv["fq"], sv["fk"], sv["fv"], do_fox, sv["lse_fox"], dl_fox, 1.0 / math.sqrt(FOX_DIM), False,
                                   n("fox_attn_bwd"), c=sv["c"], ct=sv["ct"])
    dseg1, dff, g["fox_q_norm"], g["fox_k_norm"], dbf = _fox_prep_bwd(seg[0], seg[1], q["fox_b_f"], q["fox_q_norm"], q["fox_k_norm"],
                                                                      dfq, dfk, dfv, dck, n("fox_prep_bwd"))
    g["fox_b_f"] = dbf[0, :HEADS]
    dl_mla = _attn_delta(sv["mq"], sv["mk"], sv["mv"], do_mla, sv["lse_mla"], 1.0 / math.sqrt(MLA_QK), True, n("mla_attn_delta"))
    dmq, dmk, dmv = _attn_bwd(sv["mq"], sv["mk"], sv["mv"], do_mla, sv["lse_mla"], dl_mla, 1.0 / math.sqrt(MLA_QK), True,
                              n("mla_attn_bwd"))
    dseg0, dqan, dkvan, dqn, dkn, g["wq"], g["wkv"] = _mla_prep_bwd(seg[0], tabs, q, q["wq"], q["wkv"], dmq, dmk, dmv, dff,
                                                                   n("mla_prep_bwd"))
    g["mla_q_a_norm"], g["mla_kv_a_norm"] = dqan, dkvan
    g["mla_q_norm"], g["mla_k_norm"] = dqn[0, :MLA_QK], dkn[0, :MLA_QK]
    dsegs = [dseg0, dseg1, dseg2, dseg3, dseg4]
    dh = None
    for k in range(5):
        dh = _mm(dsegs[k], q["w_seg"][k], "nt", n(f"dh{k}"), acc=dh)
    g["w_in"] = jnp.concatenate([_mm(sv["h"], dsegs[k], "tn", n(f"dwin{k}")) for k in range(5)], axis=1)
    dx, g["norm_g"] = _norm_bwd(sv["x"], q["norm_g"], dh, dout, n("norm_bwd"))
    return dx, g


MESH = pl.DeviceIdType.MESH
ANY = pl.BlockSpec(memory_space=pl.ANY)


def _all_gather(blocks, name):
    n = len(blocks)

    def body(*refs):
        x_refs, out_refs = refs[:n], refs[n:2 * n]
        send_sems, recv_sems, local_sems = refs[2 * n:]
        x, y, c = lax.axis_index("x"), lax.axis_index("y"), lax.axis_index("c")
        me, sibling = (x, y, c), (x, y, 1 - c)
        chips = [(1 - x, y), (x, 1 - y), (1 - x, 1 - y)]

        def slot(a, px, py, pc):
            return out_refs[a].at[4 * px + 2 * py + pc]

        def copy(a, k, blk, to, src=None):
            return pltpu.make_async_remote_copy(src_ref=slot(a, *blk) if src is None else src, dst_ref=slot(a, *blk),
                                                send_sem=send_sems.at[7 * a + k], recv_sem=recv_sems.at[7 * a + k],
                                                device_id=to, device_id_type=MESH)

        mine = [pltpu.make_async_copy(x_refs[a], slot(a, *me), local_sems.at[a]) for a in range(n)]
        for cp in mine:
            cp.start()
        first = []
        for j, chip in enumerate(chips):
            first += [copy(a, 1 + j, me, (*chip, c), src=x_refs[a]) for a in range(n)]
        first += [copy(a, 0, me, sibling, src=x_refs[a]) for a in range(n)]
        for cp in first:
            cp.start()
        passed = []
        for j, chip in enumerate(chips):
            for a in range(n):
                copy(a, 1 + j, (*chip, c), me).wait_recv()
                passed.append(copy(a, 4 + j, (*chip, c), sibling))
                passed[-1].start()
        for a in range(n):
            copy(a, 0, sibling, me).wait_recv()
        for j, chip in enumerate(chips):
            for a in range(n):
                copy(a, 4 + j, (*chip, 1 - c), me).wait_recv()
        for cp in first + passed:
            cp.wait_send()
        for cp in mine:
            cp.wait()

    return pl.pallas_call(
        body, name=name, out_shape=[jax.ShapeDtypeStruct((N_DEV,) + b.shape, b.dtype) for b in blocks],
        in_specs=[ANY] * n, out_specs=[ANY] * n,
        scratch_shapes=[pltpu.SemaphoreType.DMA((7 * n,)), pltpu.SemaphoreType.DMA((7 * n,)), pltpu.SemaphoreType.DMA((n,))],
    )(*blocks)


def _exchange_sibling(parts, name):
    n = len(parts)

    def body(*refs):
        p_refs, got_refs = refs[:n], refs[n:2 * n]
        send_sems, recv_sems = refs[2 * n:]
        x, y, c = lax.axis_index("x"), lax.axis_index("y"), lax.axis_index("c")
        sibling = (x, y, 1 - c)
        copies = []
        for a in range(n):
            for j in range(4):
                copies.append(pltpu.make_async_remote_copy(
                    src_ref=p_refs[a].at[2 * j + (1 - c)], dst_ref=got_refs[a].at[j], send_sem=send_sems.at[4 * a + j],
                    recv_sem=recv_sems.at[4 * a + j], device_id=sibling, device_id_type=MESH))
        for cp in copies:
            cp.start()
        for cp in copies:
            cp.wait()

    return pl.pallas_call(
        body, name=name, out_shape=[jax.ShapeDtypeStruct((4,) + p.shape[1:], p.dtype) for p in parts],
        in_specs=[ANY] * n, out_specs=[ANY] * n,
        scratch_shapes=[pltpu.SemaphoreType.DMA((4 * n,)), pltpu.SemaphoreType.DMA((4 * n,))],
    )(*parts)


def _exchange_chips(parts, name):
    n = len(parts)

    def body(*refs):
        p_refs, got_refs = refs[:n], refs[n:2 * n]
        send_sems, recv_sems, local_sems = refs[2 * n:]
        x, y, c = lax.axis_index("x"), lax.axis_index("y"), lax.axis_index("c")
        my_chip = 2 * x + y
        chips = [(1 - x, y), (x, 1 - y), (1 - x, 1 - y)]
        mine = [pltpu.make_async_copy(p_refs[a].at[my_chip], got_refs[a].at[my_chip], local_sems.at[a]) for a in range(n)]
        for cp in mine:
            cp.start()

        def copy(a, k, px, py):
            return pltpu.make_async_remote_copy(src_ref=p_refs[a].at[2 * px + py], dst_ref=got_refs[a].at[my_chip],
                                                send_sem=send_sems.at[3 * a + k], recv_sem=recv_sems.at[3 * a + k],
                                                device_id=(px, py, c), device_id_type=MESH)

        copies = [copy(a, k, px, py) for k, (px, py) in enumerate(chips) for a in range(n)]
        for cp in copies:
            cp.start()
        for k, (px, py) in enumerate(chips):
            for a in range(n):
                pltpu.make_async_remote_copy(src_ref=p_refs[a].at[my_chip], dst_ref=got_refs[a].at[2 * px + py],
                                             send_sem=send_sems.at[3 * a + k], recv_sem=recv_sems.at[3 * a + k],
                                             device_id=(px, py, c), device_id_type=MESH).wait_recv()
        for cp in copies:
            cp.wait_send()
        for cp in mine:
            cp.wait()

    return pl.pallas_call(
        body, name=name, out_shape=[jax.ShapeDtypeStruct(p.shape, p.dtype) for p in parts],
        in_specs=[ANY] * n, out_specs=[ANY] * n,
        scratch_shapes=[pltpu.SemaphoreType.DMA((3 * n,)), pltpu.SemaphoreType.DMA((3 * n,)), pltpu.SemaphoreType.DMA((n,))],
    )(*parts)


def _add_sibling(parts, got, name):
    _, r, cc = parts.shape
    tr = _pick(r, (512, 256, 128, 64, 32, 16))
    c = lax.axis_index("c")

    def body(c_ref, p_ref, g_ref, o_ref):
        o_ref[...] = (p_ref[...] + g_ref[...]).astype(BF16)

    return pl.pallas_call(
        body, name=name, out_shape=jax.ShapeDtypeStruct((4, r, cc), BF16),
        grid_spec=pltpu.PrefetchScalarGridSpec(
            num_scalar_prefetch=1, grid=(4, r // tr),
            in_specs=[pl.BlockSpec((1, tr, cc), lambda j, i, cr: (2 * j + cr[0], i, 0)),
                      pl.BlockSpec((1, tr, cc), lambda j, i, cr: (j, i, 0))],
            out_specs=pl.BlockSpec((1, tr, cc), lambda j, i, cr: (j, i, 0))),
        compiler_params=_vmem(48),
    )(c.reshape(1).astype(jnp.int32), parts, got)


def _sum_leading(parts, name):
    k, r, cc = parts.shape
    tr = _pick(r, (512, 256, 128, 64, 32, 16, 8))

    def body(p_ref, o_ref):
        acc = p_ref[0]
        for j in range(1, k):
            acc = acc + p_ref[j]
        o_ref[...] = acc

    return pl.pallas_call(
        body, name=name, out_shape=jax.ShapeDtypeStruct((r, cc), F32), grid=(r // tr,),
        in_specs=[pl.BlockSpec((k, tr, cc), lambda i: (0, i, 0))], out_specs=pl.BlockSpec((tr, cc), lambda i: (i, 0)),
    )(parts)


def _adamw_math(w, g, m, v):
    nm = ADAM_B1 * m + (1.0 - ADAM_B1) * g
    nv = ADAM_B2 * v + (1.0 - ADAM_B2) * jnp.square(g)
    m_hat = nm / (1.0 - ADAM_B1 ** ADAM_STEP)
    v_hat = nv / (1.0 - ADAM_B2 ** ADAM_STEP)
    return -ADAM_LR * (m_hat / (jnp.sqrt(v_hat) + ADAM_EPS) + ADAM_WD * w), nm, nv


def _adamw_sum(w, contribs, m, v, name):
    r, cc = w.shape
    k = contribs.shape[0]
    tr = _pick(r, (256, 128, 64, 32, 16, 8))

    def body(w_ref, c_ref, m_ref, v_ref, g_ref, d_ref, nm_ref, nv_ref):
        g = c_ref[0].astype(F32)
        for j in range(1, k):
            g = g + c_ref[j].astype(F32)
        g_ref[...] = g
        d_ref[...], nm_ref[...], nv_ref[...] = _adamw_math(w_ref[...], g, m_ref[...], v_ref[...])

    spec = pl.BlockSpec((tr, cc), lambda i: (i, 0))
    return pl.pallas_call(
        body, name=name, grid=(r // tr,), in_specs=[spec, pl.BlockSpec((k, tr, cc), lambda i: (0, i, 0)), spec, spec],
        out_specs=[spec] * 4, out_shape=[jax.ShapeDtypeStruct((r, cc), F32)] * 4,
        compiler_params=pltpu.CompilerParams(dimension_semantics=("parallel",), vmem_limit_bytes=48 * 1024 * 1024),
    )(w, contribs, m, v)


def _adamw_many(ws, gs, ms, vs, name):
    n = len(ws)

    def body(*refs):
        w_r, g_r, m_r, v_r = refs[:n], refs[n:2 * n], refs[2 * n:3 * n], refs[3 * n:4 * n]
        d_r, nm_r, nv_r = refs[4 * n:5 * n], refs[5 * n:6 * n], refs[6 * n:7 * n]
        for a in range(n):
            d_r[a][...], nm_r[a][...], nv_r[a][...] = _adamw_math(w_r[a][...], g_r[a][...], m_r[a][...], v_r[a][...])

    shapes = [jax.ShapeDtypeStruct(w.shape, F32) for w in ws]
    res = pl.pallas_call(body, name=name, out_shape=shapes * 3,
                         compiler_params=pltpu.CompilerParams(vmem_limit_bytes=56 * 1024 * 1024))(*ws, *gs, *ms, *vs)
    return res[:n], res[n:2 * n], res[2 * n:]


def _pack_rows(flat, lanes, row_mult):
    n = flat.shape[-1]
    rows = -(-n // lanes)
    rows = -(-rows // row_mult) * row_mult
    pad = rows * lanes - n
    if pad:
        flat = jnp.pad(flat, [(0, 0)] * (flat.ndim - 1) + [(0, pad)])
    return flat.reshape(flat.shape[:-1] + (rows, lanes))


def _rope_tables(positions):
    inv = 1.0 / (ROPE_THETA ** (jnp.arange(0, MLA_ROPE, 2, dtype=F32) / MLA_ROPE))
    ang = positions.astype(F32)[:, None] * inv
    cos, sin = jnp.cos(ang), jnp.sin(ang)
    s = positions.shape[0]
    z = lambda n: jnp.zeros((s, n), F32)
    c = jnp.concatenate([jnp.ones((s, 64), F32), cos, cos, z(32)], axis=1)
    sa = jnp.concatenate([z(64), -sin, z(48)], axis=1)
    sb = jnp.concatenate([z(80), sin, z(32)], axis=1)
    return c, sa, sb


def _device_grads(x, positions, target, w, small):
    tabs = _rope_tables(positions)
    qs = [_layer_params(l, w, small) for l in range(DEPTH)]
    h = x
    saves = []
    for l in range(DEPTH):
        h, sv = _layer_fwd(l, h, tabs, qs[l])
        saves.append(sv)
    loss, d = _loss_head(h, target, "loss_head")
    grads = [None] * DEPTH
    for l in reversed(range(DEPTH)):
        d, grads[l] = _layer_bwd(l, d, tabs, qs[l], saves[l])
    return loss[0, 0], d, grads


def kernel(x, positions, norm_g, w_in, mla_q_a_norm, mla_w_q_up, mla_kv_a_norm, mla_w_kv_up, mla_q_norm, mla_k_norm, fox_b_f, fox_q_norm, fox_k_norm, s5_lambda_re, s5_lambda_im, s5_log_dt, s5_b_re, s5_b_im, s5_c_re, s5_c_im, s5_d, s5_w_glu, s5_b_glu, w_branch_out, w_out, loss_target, m_norm_g, m_w_in, m_mla_q_a_norm, m_mla_w_q_up, m_mla_kv_a_norm, m_mla_w_kv_up, m_mla_q_norm, m_mla_k_norm, m_fox_b_f, m_fox_q_norm, m_fox_k_norm, m_s5_lambda_re, m_s5_lambda_im, m_s5_log_dt, m_s5_b_re, m_s5_b_im, m_s5_c_re, m_s5_c_im, m_s5_d, m_s5_w_glu, m_s5_b_glu, m_w_branch_out, m_w_out, v_norm_g, v_w_in, v_mla_q_a_norm, v_mla_w_q_up, v_mla_kv_a_norm, v_mla_w_kv_up, v_mla_q_norm, v_mla_k_norm, v_fox_b_f, v_fox_q_norm, v_fox_k_norm, v_s5_lambda_re, v_s5_lambda_im, v_s5_log_dt, v_s5_b_re, v_s5_b_im, v_s5_c_re, v_s5_c_im, v_s5_d, v_s5_w_glu, v_s5_b_glu, v_w_branch_out, v_w_out):
    env = dict(locals())
    wts = {k: env[k] for k in WEIGHTS}
    mom = {k: env["m_" + k] for k in WEIGHTS}
    var = {k: env["v_" + k] for k in WEIGHTS}

    two_d = {k: (wts[k].shape[0] * wts[k].shape[1], wts[k].shape[2]) for k in SHARDED}
    gathered = _all_gather([wts[k].reshape(two_d[k]).astype(BF16) for k in SHARDED], "gather_weights")
    full = {}
    for k, gk in zip(SHARDED, gathered):
        shp = wts[k].shape
        blk = gk.reshape((N_DEV,) + shp)
        if k in COL_SHARDED:
            full[k] = blk.transpose(1, 2, 0, 3).reshape(shp[0], shp[1], N_DEV * shp[2])
        else:
            full[k] = blk.transpose(1, 0, 2, 3).reshape(shp[0], N_DEV * shp[1], shp[2])

    small = {k: wts[k] for k in SMALL}
    loss, dx, grads = _device_grads(x[0], positions[0], loss_target[0], full, small)
    loss = lax.psum(loss, ("x", "y", "c"))

    big = {"w_in": [_unpad_w_in(g["w_in"]) for g in grads], "mla_w_q_up": [_unpad_wq(g["wq"]) for g in grads],
           "mla_w_kv_up": [_unpad_wkv(g["wkv"]) for g in grads], "s5_w_glu": [g["wglu"] for g in grads],
           "w_branch_out": [g["wo"] for g in grads], "w_out": [g["wout"] for g in grads]}
    parts = []
    for k in SHARDED:
        gk = jnp.stack(big[k])
        lshape = wts[k].shape
        if k in COL_SHARDED:
            gk = gk.reshape(lshape[0], lshape[1], N_DEV, lshape[2]).transpose(2, 0, 1, 3)
        else:
            gk = gk.reshape(lshape[0], N_DEV, lshape[1], lshape[2]).transpose(1, 0, 2, 3)
        parts.append(gk.reshape((N_DEV,) + two_d[k]))
    got = _exchange_sibling(parts, "reduce_sibling")
    chip_sums = [_add_sibling(p, g, f"reduce_sibling_add_{k}") for k, p, g in zip(SHARDED, parts, got)]
    contribs = _exchange_chips(chip_sums, "reduce_chips")
    grad_out, delta_out, m_out, v_out = {}, {}, {}, {}
    for k, ck in zip(SHARDED, contribs):
        shp = wts[k].shape
        res = _adamw_sum(wts[k].reshape(two_d[k]), ck, mom[k].reshape(two_d[k]), var[k].reshape(two_d[k]), f"adamw_{k}")
        grad_out[k], delta_out[k], m_out[k], v_out[k] = (z.reshape(shp) for z in res)

    sm = {k: jnp.stack([g[k] for g in grads]).reshape(wts[k].shape) for k in SMALL}
    flat_s = jnp.concatenate([sm[k].reshape(-1) for k in SMALL])
    g_small = _sum_leading(_all_gather([_pack_rows(flat_s, LANES, 256)], "gather_small_grads")[0], "sum_small_grads").reshape(-1)
    off = 0
    for k in SMALL:
        cnt = int(np.prod(wts[k].shape))
        grad_out[k] = g_small[off:off + cnt].reshape(wts[k].shape)
        off += cnt
    flat2 = lambda a: a.reshape(-1, a.shape[-1])
    d_s, m_s, v_s = _adamw_many([flat2(wts[k]) for k in SMALL], [flat2(grad_out[k]) for k in SMALL],
                                [flat2(mom[k]) for k in SMALL], [flat2(var[k]) for k in SMALL], "adamw_small")
    for i, k in enumerate(SMALL):
        delta_out[k], m_out[k], v_out[k] = (z[i].reshape(wts[k].shape) for z in (d_s, m_s, v_s))

    return (loss, dx[None], *[grad_out[k] for k in WEIGHTS], *[delta_out[k] for k in WEIGHTS],
            *[m_out[k] for k in WEIGHTS], *[v_out[k] for k in WEIGHTS])
```

```python
import functools
import math

import jax
import jax.numpy as jnp
import numpy as np
from jax import lax
from jax.experimental import pallas as pl
from jax.experimental.pallas import tpu as pltpu

F32 = jnp.float32
BF16 = jnp.bfloat16

D_MODEL = 1024
DEPTH = 2
CHUNK = 64
EPS = 1e-6
HEADS = 8
MLA_NOPE, MLA_ROPE, MLA_V = 64, 32, 64
MLA_Q_RANK, MLA_KV_RANK = 256, 128
MLA_QK = MLA_NOPE + MLA_ROPE
ROPE_THETA = 10000.0
FOX_DIM = 64
S5_WIDTH, S5_GROUP, S5_GROUPS, S5_STATE = 512, 16, 32, 64
S5_LANES = S5_GROUPS * S5_STATE
IN_WIDTH = 7080
N_DEV = 8
LANES = 128
SUBLANES = 8

ADAM_LR, ADAM_B1, ADAM_B2, ADAM_EPS, ADAM_WD, ADAM_STEP = 0.001, 0.9, 0.999, 1e-08, 0.01, 10

SEG_W = (512, 1536, 512, 1536, 3072)
SEG_OFF = (0, 512, 2048, 2560, 4096)
PAD_IN = 7168
NEG = -1e30

SHARDED = ("w_in", "mla_w_q_up", "mla_w_kv_up", "s5_w_glu", "w_branch_out", "w_out")
COL_SHARDED = ("w_in", "mla_w_q_up", "mla_w_kv_up")
SMALL = ("norm_g", "mla_q_a_norm", "mla_kv_a_norm", "mla_q_norm", "mla_k_norm", "fox_b_f", "fox_q_norm", "fox_k_norm",
         "s5_lambda_re", "s5_lambda_im", "s5_log_dt", "s5_b_re", "s5_b_im", "s5_c_re", "s5_c_im", "s5_d", "s5_b_glu")
WEIGHTS = ("norm_g", "w_in", "mla_q_a_norm", "mla_w_q_up", "mla_kv_a_norm", "mla_w_kv_up", "mla_q_norm", "mla_k_norm",
           "fox_b_f", "fox_q_norm", "fox_k_norm", "s5_lambda_re", "s5_lambda_im", "s5_log_dt", "s5_b_re", "s5_b_im",
           "s5_c_re", "s5_c_im", "s5_d", "s5_w_glu", "s5_b_glu", "w_branch_out", "w_out")
PACK_LANES = 512


def _pick(n, cands):
    for c in cands:
        if n % c == 0:
            return c
    return n


def _vmem(mb):
    return pltpu.CompilerParams(vmem_limit_bytes=mb * 1024 * 1024)


def _dot(a, b, dims):
    return lax.dot_general(a.astype(BF16), b.astype(BF16), (dims, ((), ())), preferred_element_type=F32)


def _nn(a, b):
    return _dot(a, b, ((1,), (0,)))


def _nt(a, b):
    return _dot(a, b, ((1,), (1,)))


def _tn(a, b):
    return _dot(a, b, ((0,), (0,)))


def _rms(x, g, n=None):
    n = x.shape[-1] if n is None else n
    return x * lax.rsqrt(jnp.sum(x * x, axis=-1, keepdims=True) / n + EPS) * g


def _rope(t, c, sa, sb):
    return t * c + pltpu.roll(t, LANES - 16, 1) * sa + pltpu.roll(t, 16, 1) * sb


def _rope_t(d, c, sa, sb):
    return d * c + pltpu.roll(d * sa, 16, 1) + pltpu.roll(d * sb, LANES - 16, 1)


def _mm(a, b, mode, name, acc=None, scale=None):
    if mode == "tn":
        kd, m = a.shape
    else:
        m, kd = a.shape
    n = b.shape[0] if mode == "nt" else b.shape[1]
    tm, tn, tk = _pick(m, (1024, 512, 256, 128)), _pick(n, (1024, 512, 256, 128)), _pick(kd, (512, 256, 128))
    nk = kd // tk
    if mode == "tn":
        a_spec = pl.BlockSpec((tk, tm), lambda i, j, k: (k, i))
    else:
        a_spec = pl.BlockSpec((tm, tk), lambda i, j, k: (i, k))
    if mode == "nt":
        b_spec = pl.BlockSpec((tn, tk), lambda i, j, k: (j, k))
    else:
        b_spec = pl.BlockSpec((tk, tn), lambda i, j, k: (k, j))
    dims = {"nn": ((1,), (0,)), "nt": ((1,), (1,)), "tn": ((0,), (0,))}[mode]
    o_spec = pl.BlockSpec((tm, tn), lambda i, j, k: (i, j))
    has_acc = acc is not None

    def body(*refs):
        if has_acc:
            a_ref, b_ref, c_ref, o_ref, acc_ref = refs
        else:
            a_ref, b_ref, o_ref, acc_ref = refs
        k = pl.program_id(2)

        @pl.when(k == 0)
        def _():
            acc_ref[...] = c_ref[...] if has_acc else jnp.zeros(acc_ref.shape, F32)

        acc_ref[...] += _dot(a_ref[...], b_ref[...], dims)

        @pl.when(k == nk - 1)
        def _():
            o_ref[...] = acc_ref[...]

    ins = [a, b] + ([acc] if has_acc else [])
    in_specs = [a_spec, b_spec] + ([o_spec] if has_acc else [])
    return pl.pallas_call(
        body, name=name, grid=(m // tm, n // tn, nk), in_specs=in_specs, out_specs=o_spec,
        out_shape=jax.ShapeDtypeStruct((m, n), F32), scratch_shapes=[pltpu.VMEM((tm, tn), F32)],
        compiler_params=pltpu.CompilerParams(dimension_semantics=("parallel", "parallel", "arbitrary"),
                                             vmem_limit_bytes=48 * 1024 * 1024),
    )(*ins)


def _stage(name, fn, n_steps, ins, outs, accs=(), scratch=(), vmem_mb=48):
    n_in, n_out, n_acc = len(ins), len(outs), len(accs)

    def body(*refs):
        in_refs = refs[:n_in]
        out_refs = refs[n_in:n_in + n_out]
        acc_refs = refs[n_in + n_out:n_in + n_out + n_acc]
        scr = refs[n_in + n_out + n_acc:]
        if n_acc:
            @pl.when(pl.program_id(0) == 0)
            def _():
                for r in acc_refs:
                    r[...] = jnp.zeros(r.shape, r.dtype)
        fn(in_refs, out_refs, acc_refs, scr)

    acc_specs = [pl.BlockSpec(a.shape, functools.partial(lambda i, nd: (0,) * nd, nd=len(a.shape))) for a in accs]
    res = pl.pallas_call(
        body, name=name, grid=(n_steps,),
        in_specs=[s for _, s in ins], out_specs=[s for _, s in outs] + acc_specs,
        out_shape=[s for s, _ in outs] + list(accs), scratch_shapes=list(scratch),
        compiler_params=pltpu.CompilerParams(dimension_semantics=("arbitrary",),
                                             vmem_limit_bytes=vmem_mb * 1024 * 1024),
    )(*[a for a, _ in ins])
    return res


def _rows(ts, w, j=0):
    return pl.BlockSpec((ts, w), lambda i: (i, j))


def _rows_rev(ts, w, n, j=0):
    return pl.BlockSpec((ts, w), lambda i: (n - 1 - i, j))


def _heads(ts, d):
    return pl.BlockSpec((HEADS, ts, d), lambda i: (0, i, 0))


def _heads_rev(ts, d, n):
    return pl.BlockSpec((HEADS, ts, d), lambda i: (0, n - 1 - i, 0))


def _full(shape):
    nd = len(shape)
    return pl.BlockSpec(tuple(shape), lambda i: (0,) * nd)


def _sds(shape, dtype=F32):
    return jax.ShapeDtypeStruct(tuple(shape), dtype)


def _norm_fwd(x, g, name):
    s = x.shape[0]
    ts = _pick(s, (256, 128))

    def fn(ins, outs, accs, scr):
        outs[0][...] = _rms(ins[0][...], ins[1][...]).astype(BF16)

    return _stage(name, fn, s // ts, [(x, _rows(ts, D_MODEL)), (g, _full(g.shape))],
                  [(_sds((s, D_MODEL), BF16), _rows(ts, D_MODEL))])[0]


def _norm_bwd(x, g, dh, dres, name):
    s = x.shape[0]
    ts = _pick(s, (256, 128))

    def fn(ins, outs, accs, scr):
        _, vjp = jax.vjp(_rms, ins[0][...], ins[1][...])
        dx, dg = vjp(ins[2][...])
        outs[0][...] = dx + ins[3][...]
        accs[0][...] += dg

    r = _stage(name, fn, s // ts,
               [(x, _rows(ts, D_MODEL)), (g, _full(g.shape)), (dh, _rows(ts, D_MODEL)), (dres, _rows(ts, D_MODEL))],
               [(_sds((s, D_MODEL)), _rows(ts, D_MODEL))], accs=[_sds((1, D_MODEL))])
    return r[0], r[1]


def _mla_q(qraw, c, sa, sb, qn):
    return _rms(_rope(qraw, c, sa, sb), qn, MLA_QK)


def _mla_prep_fwd(seg0, tabs, p, wq, wkv, name):
    s = seg0.shape[0]
    ts = _pick(s, (256, 128))

    def fn(ins, outs, accs, scr):
        blk, cos, sa, sb, qan, kvan, qn, kn, wq_r, wkv_r = ins
        b = blk[...]
        cq, ckv, kt = b[:, :256], b[:, 256:384], b[:, 384:512]
        lane = lax.broadcasted_iota(jnp.int32, kt.shape, 1)
        kpe = jnp.where(lane >= 64, kt, 0.0)
        q_raw = _nn(_rms(cq, qan[...]), wq_r[...])
        kv_raw = _nn(_rms(ckv, kvan[...]), wkv_r[...])
        c, a, bb = cos[...], sa[...], sb[...]
        for h in range(HEADS):
            outs[0][h] = _mla_q(q_raw[:, LANES * h:LANES * (h + 1)], c, a, bb, qn[...]).astype(BF16)
            outs[1][h] = _mla_q(kv_raw[:, LANES * h:LANES * (h + 1)] + kpe, c, a, bb, kn[...]).astype(BF16)
            outs[2][h] = kv_raw[:, 1024 + 64 * h:1024 + 64 * (h + 1)].astype(BF16)

    consts = [p["mla_q_a_norm"], p["mla_kv_a_norm"], p["mla_q_norm"], p["mla_k_norm"], wq, wkv]
    return _stage(name, fn, s // ts,
                  [(seg0, _rows(ts, 512))] + [(t, _rows(ts, LANES)) for t in tabs] + [(a, _full(a.shape)) for a in consts],
                  [(_sds((HEADS, s, LANES), BF16), _heads(ts, LANES)), (_sds((HEADS, s, LANES), BF16), _heads(ts, LANES)),
                   (_sds((HEADS, s, 64), BF16), _heads(ts, 64))])


def _mla_prep_bwd(seg0, tabs, p, wq, wkv, dq, dk, dv, dff, name):
    s = seg0.shape[0]
    ts = _pick(s, (256, 128))

    def fn(ins, outs, accs, scr):
        blk, cos, sa, sb, qan, kvan, qn, kn, wq_r, wkv_r, dq_r, dk_r, dv_r, dff_r = ins
        dqan, dkvan, dqn, dkn, dwq, dwkv = accs
        dqraw_s, dkvraw_s = scr
        b = blk[...]
        cq, ckv, kt = b[:, :256], b[:, 256:384], b[:, 384:512]
        lane = lax.broadcasted_iota(jnp.int32, kt.shape, 1)
        kpe = jnp.where(lane >= 64, kt, 0.0)
        cqn, vjp_cq = jax.vjp(_rms, cq, qan[...])
        ckvn, vjp_ckv = jax.vjp(_rms, ckv, kvan[...])
        q_raw = _nn(cqn, wq_r[...])
        kv_raw = _nn(ckvn, wkv_r[...])
        c, a, bb = cos[...], sa[...], sb[...]

        def head_bwd(raw, gain, d):
            t = _rope(raw, c, a, bb)
            _, vjp = jax.vjp(functools.partial(_rms, n=MLA_QK), t, gain)
            dt, dgain = vjp(d)
            return _rope_t(dt, c, a, bb), dgain

        dkpe = jnp.zeros(kt.shape, F32)
        for h in range(HEADS):
            dqh, dg = head_bwd(q_raw[:, LANES * h:LANES * (h + 1)], qn[...], dq_r[h])
            dqn[...] += dg
            dqraw_s[:, LANES * h:LANES * (h + 1)] = dqh
            dkh, dg = head_bwd(kv_raw[:, LANES * h:LANES * (h + 1)] + kpe, kn[...], dk_r[h])
            dkn[...] += dg
            dkvraw_s[:, LANES * h:LANES * (h + 1)] = dkh
            dkpe = dkpe + dkh
            dkvraw_s[:, 1024 + 64 * h:1024 + 64 * (h + 1)] = dv_r[h]
        dq_raw = dqraw_s[...]
        dkv_raw = dkvraw_s[...]
        dwq[...] += _tn(cqn, dq_raw)
        dwkv[...] += _tn(ckvn, dkv_raw)
        dcq, dg = vjp_cq(_nt(dq_raw, wq_r[...]))
        dqan[...] += dg
        dckv, dg = vjp_ckv(_nt(dkv_raw, wkv_r[...]))
        dkvan[...] += dg
        outs[0][:, 0:256] = dcq.astype(BF16)
        outs[0][:, 256:384] = dckv.astype(BF16)
        outs[0][:, 384:512] = (jnp.where(lane >= 64, dkpe, 0.0) + dff_r[...]).astype(BF16)

    consts = [p["mla_q_a_norm"], p["mla_kv_a_norm"], p["mla_q_norm"], p["mla_k_norm"], wq, wkv]
    return _stage(name, fn, s // ts,
                  [(seg0, _rows(ts, 512))] + [(t, _rows(ts, LANES)) for t in tabs] + [(a, _full(a.shape)) for a in consts]
                  + [(dq, _heads(ts, LANES)), (dk, _heads(ts, LANES)), (dv, _heads(ts, 64)), (dff, _rows(ts, LANES))],
                  [(_sds((s, 512), BF16), _rows(ts, 512))],
                  accs=[_sds((1, 256)), _sds((1, 128)), _sds((1, LANES)), _sds((1, LANES)), _sds(wq.shape), _sds(wkv.shape)],
                  scratch=[pltpu.VMEM((ts, 1024), F32), pltpu.VMEM((ts, 1536), F32)])


def _fox_prep_fwd(seg0, seg1, bf, qn, kn, name):
    s = seg0.shape[0]
    ts = _pick(s, (256, 128))
    steps = int(math.log2(ts))

    def fn(ins, outs, accs, scr):
        kt_r, x_r, bf_r, qn_r, kn_r = ins
        carry = scr[0]

        @pl.when(pl.program_id(0) == 0)
        def _():
            carry[...] = jnp.zeros(carry.shape, F32)

        x = x_r[...]
        for h in range(HEADS):
            outs[0][h] = _rms(x[:, 64 * h:64 * (h + 1)], qn_r[...]).astype(BF16)
            outs[1][h] = _rms(x[:, 512 + 64 * h:512 + 64 * (h + 1)], kn_r[...]).astype(BF16)
            outs[2][h] = x[:, 1024 + 64 * h:1024 + 64 * (h + 1)].astype(BF16)
        kt = kt_r[...]
        lane = lax.broadcasted_iota(jnp.int32, kt.shape, 1)
        row = lax.broadcasted_iota(jnp.int32, kt.shape, 0)
        cs = jnp.where(lane < HEADS, jax.nn.log_sigmoid(kt + bf_r[...]), 0.0)
        for k in range(steps):
            sh = 1 << k
            cs = cs + jnp.where(row >= sh, pltpu.roll(cs, sh, 0), 0.0)
        cs = cs + carry[0:1, :]
        outs[3][...] = cs
        outs[4][...] = cs.T[0:HEADS, :]
        carry[0:1, :] = cs[ts - 1:ts, :]

    return _stage(name, fn, s // ts,
                  [(seg0, _rows(ts, LANES, 3)), (seg1, _rows(ts, 1536)), (bf, _full(bf.shape)), (qn, _full(qn.shape)),
                   (kn, _full(kn.shape))],
                  [(_sds((HEADS, s, 64), BF16), _heads(ts, 64)), (_sds((HEADS, s, 64), BF16), _heads(ts, 64)),
                   (_sds((HEADS, s, 64), BF16), _heads(ts, 64)), (_sds((s, LANES)), _rows(ts, LANES)),
                   (_sds((HEADS, s)), pl.BlockSpec((HEADS, ts), lambda i: (0, i)))],
                  scratch=[pltpu.VMEM((SUBLANES, LANES), F32)])


def _fox_prep_bwd(seg0, seg1, bf, qn, kn, dq, dk, dv, dck, name):
    s = seg0.shape[0]
    ts = _pick(s, (256, 128))
    n = s // ts
    steps = int(math.log2(ts))

    def fn(ins, outs, accs, scr):
        kt_r, x_r, bf_r, qn_r, kn_r, dq_r, dk_r, dv_r, dck_r = ins
        dqn, dkn, dbf = accs
        carry, dbuf = scr

        @pl.when(pl.program_id(0) == 0)
        def _():
            carry[...] = jnp.zeros(carry.shape, F32)

        x = x_r[...]
        for h in range(HEADS):
            _, vjp = jax.vjp(_rms, x[:, 64 * h:64 * (h + 1)], qn_r[...])
            d, dg = vjp(dq_r[h])
            dbuf[:, 64 * h:64 * (h + 1)] = d
            dqn[...] += dg
            _, vjp = jax.vjp(_rms, x[:, 512 + 64 * h:512 + 64 * (h + 1)], kn_r[...])
            d, dg = vjp(dk_r[h])
            dbuf[:, 512 + 64 * h:512 + 64 * (h + 1)] = d
            dkn[...] += dg
            dbuf[:, 1024 + 64 * h:1024 + 64 * (h + 1)] = dv_r[h]
        outs[0][...] = dbuf[...].astype(BF16)
        dc = dck_r[...].reshape(HEADS, ts)
        dc = jnp.concatenate([dc, jnp.zeros((LANES - HEADS, ts), F32)], axis=0).T
        row = lax.broadcasted_iota(jnp.int32, dc.shape, 0)
        lane = lax.broadcasted_iota(jnp.int32, dc.shape, 1)
        for k in range(steps):
            sh = 1 << k
            dc = dc + jnp.where(row < ts - sh, pltpu.roll(dc, ts - sh, 0), 0.0)
        dc = dc + carry[0:1, :]
        carry[0:1, :] = dc[0:1, :]
        dff = jnp.where(lane < HEADS, dc * jax.nn.sigmoid(-(kt_r[...] + bf_r[...])), 0.0)
        outs[1][...] = dff
        dbf[...] += jnp.sum(dff, axis=0, keepdims=True)

    return _stage(name, fn, n,
                  [(seg0, _rows_rev(ts, LANES, n, 3)), (seg1, _rows_rev(ts, 1536, n)), (bf, _full(bf.shape)),
                   (qn, _full(qn.shape)), (kn, _full(kn.shape)), (dq, _heads_rev(ts, 64, n)), (dk, _heads_rev(ts, 64, n)),
                   (dv, _heads_rev(ts, 64, n)), (dck, pl.BlockSpec((HEADS, 1, ts), lambda i: (0, 0, n - 1 - i)))],
                  [(_sds((s, 1536), BF16), _rows_rev(ts, 1536, n)), (_sds((s, LANES)), _rows_rev(ts, LANES, n))],
                  accs=[_sds((1, 64)), _sds((1, 64)), _sds((1, LANES))],
                  scratch=[pltpu.VMEM((SUBLANES, LANES), F32), pltpu.VMEM((ts, 1536), F32)])


def _allowed(i, j, t, chunk_causal):
    qpos = i * t + lax.broadcasted_iota(jnp.int32, (t, t), 0)
    kpos = j * t + lax.broadcasted_iota(jnp.int32, (t, t), 1)
    if chunk_causal:
        return (kpos // CHUNK) <= (qpos // CHUNK)
    return kpos <= qpos


def _pick_col(c_blk, h):
    lane = lax.broadcasted_iota(jnp.int32, c_blk.shape, 1)
    return jnp.sum(jnp.where(lane == h, c_blk, 0.0), axis=1, keepdims=True)


class _Ride:
    def __init__(self, make, take):
        self.make, self.take = make, take


class _Plan:
    def __init__(self, ins, out_shapes, n_remote, n_local, copies, aliases=None):
        self.ins, self.out_shapes, self.n_remote, self.n_local = list(ins), list(out_shapes), n_remote, n_local
        self.copies, self.aliases = copies, dict(aliases or {})

    def scratch(self):
        return [pltpu.SemaphoreType.DMA((self.n_remote,)), pltpu.SemaphoreType.DMA((self.n_remote,)),
                pltpu.SemaphoreType.DMA((max(self.n_local, 1),))]

    def start(self, in_refs, out_refs, sems):
        remote, local = self.copies(in_refs, out_refs, *sems)
        for cp in local + remote:
            cp.start()

    def wait(self, in_refs, out_refs, sems):
        remote, local = self.copies(in_refs, out_refs, *sems)
        for cp in remote:
            cp.wait()
        for cp in local:
            cp.wait()


def _call_with_ride(core, name, grid, ins, in_specs, out_shape, out_specs, semantics, vmem_mb, ride):
    n_in, n_out = len(ins), len(out_shape)
    if ride is None:
        return pl.pallas_call(
            core, name=name, grid=grid, in_specs=in_specs, out_specs=out_specs, out_shape=out_shape,
            compiler_params=pltpu.CompilerParams(dimension_semantics=semantics, vmem_limit_bytes=vmem_mb * 1024 * 1024),
        )(*ins)
    plan = ride.make()
    ci, co = len(plan.ins), len(plan.out_shapes)

    def body(*refs):
        c_in = refs[n_in:n_in + ci]
        a_out = refs[n_in + ci:n_in + ci + n_out]
        c_out = refs[n_in + ci + n_out:n_in + ci + n_out + co]
        sems = refs[n_in + ci + n_out + co:]
        ids = [pl.program_id(d) for d in range(len(grid))]
        first = functools.reduce(jnp.logical_and, [i == 0 for i in ids])
        last = functools.reduce(jnp.logical_and, [i == g - 1 for i, g in zip(ids, grid)])

        @pl.when(first)
        def _():
            plan.start(c_in, c_out, sems)

        core(*refs[:n_in], *a_out)

        @pl.when(last)
        def _():
            plan.wait(c_in, c_out, sems)

    res = pl.pallas_call(
        body, name=name, grid=grid, in_specs=list(in_specs) + [ANY] * ci, out_specs=list(out_specs) + [ANY] * co,
        out_shape=list(out_shape) + plan.out_shapes, scratch_shapes=plan.scratch(),
        input_output_aliases={n_in + i: n_out + o for i, o in plan.aliases.items()},
        compiler_params=pltpu.CompilerParams(dimension_semantics=("arbitrary",) * len(grid),
                                             vmem_limit_bytes=vmem_mb * 1024 * 1024),
    )(*ins, *plan.ins)
    ride.take(res[n_out:])
    return res[:n_out]


def _run_plan(plan, name):
    ci = len(plan.ins)

    def body(*refs):
        c_in, c_out, sems = refs[:ci], refs[ci:ci + len(plan.out_shapes)], refs[ci + len(plan.out_shapes):]
        plan.start(c_in, c_out, sems)
        plan.wait(c_in, c_out, sems)

    return pl.pallas_call(body, name=name, in_specs=[ANY] * ci, out_specs=[ANY] * len(plan.out_shapes),
                          out_shape=plan.out_shapes, scratch_shapes=plan.scratch(),
                          input_output_aliases=plan.aliases)(*plan.ins)


def _attn_fwd(q, k, v, scale, chunk_causal, name, c=None, ct=None, hps=8, ride=None):
    _, s, dk = q.shape
    dv = v.shape[2]
    t = _pick(s, (256, 128))
    bias = c is not None

    def body(*refs):
        if bias:
            q_ref, k_ref, v_ref, c_ref, ct_ref, o_ref, lse_ref = refs
        else:
            q_ref, k_ref, v_ref, o_ref, lse_ref = refs
        hp, i = pl.program_id(0), pl.program_id(1)
        qb = [q_ref[e] for e in range(hps)]
        cq = [_pick_col(c_ref[...], hp * hps + e) if bias else None for e in range(hps)]

        def step(j, carry, diagonal):
            off = pl.multiple_of(j * t, t)
            out = []
            for e in range(hps):
                m, l, acc = carry[e]
                sc = _nt(qb[e], k_ref[e, pl.ds(off, t), :]) * scale
                if bias:
                    sc = sc + (cq[e] - ct_ref[pl.ds(hp * hps + e, 1), pl.ds(off, t)])
                if diagonal:
                    sc = jnp.where(_allowed(i, j, t, chunk_causal), sc, NEG)
                m_new = jnp.maximum(m, jnp.max(sc, axis=1, keepdims=True))
                pr = jnp.exp(sc - m_new)
                alpha = jnp.exp(m - m_new)
                out.append((m_new, alpha * l + jnp.sum(pr, axis=1, keepdims=True),
                            alpha * acc + _nn(pr, v_ref[e, pl.ds(off, t), :])))
            return tuple(out)

        init = tuple((jnp.full((t, 1), NEG, F32), jnp.zeros((t, 1), F32), jnp.zeros((t, dv), F32)) for _ in range(hps))
        res = step(i, lax.fori_loop(0, i, functools.partial(step, diagonal=False), init), True)
        for e in range(hps):
            m, l, acc = res[e]
            o_ref[e] = acc / l
            lse_ref[e] = m + jnp.log(l)

    ins = [q, k, v] + ([c, ct] if bias else [])
    in_specs = [pl.BlockSpec((hps, t, dk), lambda h, i: (h, i, 0)), pl.BlockSpec((hps, s, dk), lambda h, i: (h, 0, 0)),
                pl.BlockSpec((hps, s, dv), lambda h, i: (h, 0, 0))]
    if bias:
        in_specs += [pl.BlockSpec((t, LANES), lambda h, i: (i, 0)), pl.BlockSpec((HEADS, s), lambda h, i: (0, 0))]
    return _call_with_ride(
        body, name, (HEADS // hps, s // t), ins, in_specs, [_sds((HEADS, s, dv)), _sds((HEADS, s, 1))],
        [pl.BlockSpec((hps, t, dv), lambda h, i: (h, i, 0)), pl.BlockSpec((hps, t, 1), lambda h, i: (h, i, 0))],
        ("parallel", "parallel"), 48, ride)


def _attn_delta(q, k, v, do, lse, scale, chunk_causal, name, c=None, ct=None, hps=4, ride=None):
    _, s, dk = q.shape
    dv = v.shape[2]
    t = _pick(s, (256, 128))
    bias = c is not None

    def body(*refs):
        if bias:
            q_ref, k_ref, v_ref, do_ref, lse_ref, c_ref, ct_ref, dl_ref = refs
        else:
            q_ref, k_ref, v_ref, do_ref, lse_ref, dl_ref = refs
        hp, i = pl.program_id(0), pl.program_id(1)
        qb, dob, lse_b = ([r[e] for e in range(hps)] for r in (q_ref, do_ref, lse_ref))
        cq = [_pick_col(c_ref[...], hp * hps + e) if bias else None for e in range(hps)]

        def step(j, acc, diagonal):
            off = pl.multiple_of(j * t, t)
            out = []
            for e in range(hps):
                sc = _nt(qb[e], k_ref[e, pl.ds(off, t), :]) * scale
                if bias:
                    sc = sc + (cq[e] - ct_ref[pl.ds(hp * hps + e, 1), pl.ds(off, t)])
                pr = jnp.exp(sc - lse_b[e])
                if diagonal:
                    pr = jnp.where(_allowed(i, j, t, chunk_causal), pr, 0.0)
                out.append(acc[e] + jnp.sum(pr * _nt(dob[e], v_ref[e, pl.ds(off, t), :]), axis=1, keepdims=True))
            return tuple(out)

        init = tuple(jnp.zeros((t, 1), F32) for _ in range(hps))
        res = step(i, lax.fori_loop(0, i, functools.partial(step, diagonal=False), init), True)
        for e in range(hps):
            dl_ref[e] = res[e]

    ins = [q, k, v, do, lse] + ([c, ct] if bias else [])
    in_specs = [pl.BlockSpec((hps, t, dk), lambda h, i: (h, i, 0)), pl.BlockSpec((hps, s, dk), lambda h, i: (h, 0, 0)),
                pl.BlockSpec((hps, s, dv), lambda h, i: (h, 0, 0)), pl.BlockSpec((hps, t, dv), lambda h, i: (h, i, 0)),
                pl.BlockSpec((hps, t, 1), lambda h, i: (h, i, 0))]
    if bias:
        in_specs += [pl.BlockSpec((t, LANES), lambda h, i: (i, 0)), pl.BlockSpec((HEADS, s), lambda h, i: (0, 0))]
    return _call_with_ride(body, name, (HEADS // hps, s // t), ins, in_specs, [_sds((HEADS, s, 1))],
                           [pl.BlockSpec((hps, t, 1), lambda h, i: (h, i, 0))], ("parallel", "parallel"), 48, ride)[0]


def _attn_bwd(q, k, v, do, lse, delta, scale, chunk_causal, name, c=None, ct=None, hps=8, ride=None):
    _, s, dk = q.shape
    dv = v.shape[2]
    t = _pick(s, (256, 128))
    n = s // t
    bias = c is not None

    def body(*refs):
        if bias:
            q_ref, k_ref, v_ref, do_ref, lse_ref, dl_ref, c_ref, ct_ref, dq_ref, dk_ref, dv_ref, dck_ref = refs
        else:
            q_ref, k_ref, v_ref, do_ref, lse_ref, dl_ref, dq_ref, dk_ref, dv_ref = refs
        hp, j = pl.program_id(0), pl.program_id(1)

        @pl.when(j == 0)
        def _():
            dq_ref[...] = jnp.zeros(dq_ref.shape, F32)

        kb, vb = [k_ref[e] for e in range(hps)], [v_ref[e] for e in range(hps)]
        joff = pl.multiple_of(j * t, t)
        ck = [ct_ref[pl.ds(hp * hps + e, 1), pl.ds(joff, t)] if bias else None for e in range(hps)]

        def step(i, carry, diagonal):
            off = pl.multiple_of(i * t, t)
            out = []
            for e in range(hps):
                dk_acc, dv_acc, dck_acc = carry[e]
                qb = q_ref[e, pl.ds(off, t), :]
                dob = do_ref[e, pl.ds(off, t), :]
                sc = _nt(qb, kb[e]) * scale
                if bias:
                    sc = sc + (_pick_col(c_ref[pl.ds(off, t), :], hp * hps + e) - ck[e])
                pr = jnp.exp(sc - lse_ref[e, pl.ds(off, t), :])
                if diagonal:
                    pr = jnp.where(_allowed(i, j, t, chunk_causal), pr, 0.0)
                dv_acc = dv_acc + _tn(pr, dob)
                ds = pr * (_nt(dob, vb[e]) - dl_ref[e, pl.ds(off, t), :])
                dq_ref[e, pl.ds(off, t), :] += _nn(ds, kb[e]) * scale
                dk_acc = dk_acc + _tn(ds, qb) * scale
                if bias:
                    dck_acc = dck_acc - jnp.sum(ds, axis=0, keepdims=True)
                out.append((dk_acc, dv_acc, dck_acc))
            return tuple(out)

        zero = tuple((jnp.zeros((t, dk), F32), jnp.zeros((t, dv), F32), jnp.zeros((1, t), F32)) for _ in range(hps))
        res = lax.fori_loop(j + 1, n, functools.partial(step, diagonal=False), step(j, zero, True))
        for e in range(hps):
            dk_ref[e], dv_ref[e] = res[e][0], res[e][1]
            if bias:
                dck_ref[e] = res[e][2]

    ins = [q, k, v, do, lse, delta] + ([c, ct] if bias else [])
    full = lambda d: pl.BlockSpec((hps, s, d), lambda h, j: (h, 0, 0))
    blk = lambda d: pl.BlockSpec((hps, t, d), lambda h, j: (h, j, 0))
    in_specs = [full(dk), blk(dk), blk(dv), full(dv), full(1), full(1)]
    if bias:
        in_specs += [pl.BlockSpec((s, LANES), lambda h, j: (0, 0)), pl.BlockSpec((HEADS, s), lambda h, j: (0, 0))]
    out_specs = [full(dk), blk(dk), blk(dv)]
    out_shape = [_sds((HEADS, s, dk)), _sds((HEADS, s, dk)), _sds((HEADS, s, dv))]
    if bias:
        out_specs.append(pl.BlockSpec((hps, 1, t), lambda h, j: (h, 0, j)))
        out_shape.append(_sds((HEADS, 1, s)))
    return _call_with_ride(body, name, (HEADS // hps, n), ins, in_specs, out_shape, out_specs, ("parallel", "arbitrary"), 56,
                           ride)


def _s5_disc(lr, li, ldt, br, bi):
    dt = jnp.exp(ldt)
    mag = jnp.exp(lr * dt)
    a_re = mag * jnp.cos(li * dt)
    a_im = mag * jnp.sin(li * dt)
    den = lr * lr + li * li
    f_re = ((a_re - 1.0) * lr + a_im * li) / den
    f_im = (a_im * lr - (a_re - 1.0) * li) / den
    return a_re, a_im, f_re * br - f_im * bi, f_re * bi + f_im * br


def _s5_params_fwd(lr16, li16, ldt16, br2, bi2, name):
    def body(a, b, c, d, e, o0, o1, o2, o3):
        r = _s5_disc(a[...], b[...], c[...], d[...], e[...])
        o0[...], o1[...], o2[...], o3[...] = r

    return pl.pallas_call(body, name=name, out_shape=[_sds((512, 64))] * 4)(lr16, li16, ldt16, br2, bi2)


def _s5_params_bwd(lr16, li16, ldt16, br2, bi2, da_re16, da_im16, dbb_re, dbb_im, name):
    def body(a, b, c, d, e, g0, g1, g2, g3, o_lr, o_li, o_dt, o_br, o_bi):
        _, vjp = jax.vjp(_s5_disc, a[...], b[...], c[...], d[...], e[...])
        dlr, dli, dldt, dbr, dbi = vjp((g0[...], g1[...], g2[...], g3[...]))
        grp = lambda z: z.reshape(S5_GROUPS, S5_GROUP, S5_STATE).sum(axis=1)
        o_lr[...] = grp(dlr)
        o_li[...] = grp(dli)
        o_dt[...] = jnp.sum(grp(dldt), axis=1, keepdims=True)
        o_br[...] = dbr
        o_bi[...] = dbi

    return pl.pallas_call(
        body, name=name, out_shape=[_sds((32, 64)), _sds((32, 64)), _sds((32, 1)), _sds((512, 64)), _sds((512, 64))],
    )(lr16, li16, ldt16, br2, bi2, da_re16, da_im16, dbb_re, dbb_im)


def _cmul(ar, ai, br, bi):
    return ar * br - ai * bi, ar * bi + ai * br


def _scan(bre, bim, a_re, a_im, reverse, name, x_re=None, x_im=None):
    s, w = bre.shape
    lw = _pick(w, (256, 128))
    nt = s // SUBLANES
    with_da = x_re is not None

    def body(*refs):
        if with_da:
            bre_r, bim_r, are_r, aim_r, xre_r, xim_r, ore_r, oim_r, dare_r, daim_r = refs
        else:
            bre_r, bim_r, are_r, aim_r, ore_r, oim_r = refs
        ar, ai = are_r[...], aim_r[...]
        shp = (SUBLANES, lw)
        row = lax.broadcasted_iota(jnp.int32, shp, 0)
        pows = [(ar, ai)]
        for _ in range(SUBLANES - 1):
            pows.append(_cmul(pows[-1][0], pows[-1][1], ar, ai))
        cm_r, cm_i = jnp.zeros(shp, F32), jnp.zeros(shp, F32)
        for r in range(SUBLANES):
            e = (SUBLANES - 1 - r) if reverse else r
            cm_r = jnp.where(row == r, jnp.broadcast_to(pows[e][0], shp), cm_r)
            cm_i = jnp.where(row == r, jnp.broadcast_to(pows[e][1], shp), cm_i)
        steps = [(1, pows[0]), (2, pows[1]), (4, pows[3])]

        def tile(it, carry):
            if with_da:
                c_r, c_i, acc_r, acc_i = carry
            else:
                c_r, c_i = carry
            r = (nt - 1 - it) if reverse else it
            off = pl.multiple_of(r * SUBLANES, SUBLANES)
            xr, xi = bre_r[pl.ds(off, SUBLANES), :], bim_r[pl.ds(off, SUBLANES), :]
            for sh, (pr, pi) in steps:
                if reverse:
                    keep = row < SUBLANES - sh
                    sr = jnp.where(keep, pltpu.roll(xr, SUBLANES - sh, 0), 0.0)
                    si = jnp.where(keep, pltpu.roll(xi, SUBLANES - sh, 0), 0.0)
                else:
                    keep = row >= sh
                    sr = jnp.where(keep, pltpu.roll(xr, sh, 0), 0.0)
                    si = jnp.where(keep, pltpu.roll(xi, sh, 0), 0.0)
                mr, mi = _cmul(pr, pi, sr, si)
                xr, xi = xr + mr, xi + mi
            mr, mi = _cmul(cm_r, cm_i, c_r, c_i)
            xr, xi = xr + mr, xi + mi
            ore_r[pl.ds(off, SUBLANES), :] = xr
            oim_r[pl.ds(off, SUBLANES), :] = xi
            edge = 0 if reverse else SUBLANES - 1
            c_r, c_i = xr[edge:edge + 1, :], xi[edge:edge + 1, :]
            if not with_da:
                return c_r, c_i
            fr, fi = xre_r[pl.ds(off, SUBLANES), :], xim_r[pl.ds(off, SUBLANES), :]
            poff = pl.multiple_of(jnp.maximum(r - 1, 0) * SUBLANES, SUBLANES)
            live = (r > 0).astype(F32)
            pr_last = xre_r[pl.ds(poff, SUBLANES), :][SUBLANES - 1:SUBLANES, :] * live
            pi_last = xim_r[pl.ds(poff, SUBLANES), :][SUBLANES - 1:SUBLANES, :] * live
            sr = jnp.where(row >= 1, pltpu.roll(fr, 1, 0), jnp.broadcast_to(pr_last, shp))
            si = jnp.where(row >= 1, pltpu.roll(fi, 1, 0), jnp.broadcast_to(pi_last, shp))
            acc_r = acc_r + xr * sr + xi * si
            acc_i = acc_i + xi * sr - xr * si
            return c_r, c_i, acc_r, acc_i

        z1 = jnp.zeros((1, lw), F32)
        if with_da:
            _, _, acc_r, acc_i = lax.fori_loop(0, nt, tile, (z1, z1, jnp.zeros(shp, F32), jnp.zeros(shp, F32)))
            dare_r[...] = jnp.sum(acc_r, axis=0, keepdims=True)
            daim_r[...] = jnp.sum(acc_i, axis=0, keepdims=True)
        else:
            lax.fori_loop(0, nt, tile, (z1, z1))

    col = pl.BlockSpec((s, lw), lambda j: (0, j))
    one = pl.BlockSpec((1, lw), lambda j: (0, j))
    ins = [bre, bim, a_re, a_im] + ([x_re, x_im] if with_da else [])
    in_specs = [col, col, one, one] + ([col, col] if with_da else [])
    out_specs = [col, col] + ([one, one] if with_da else [])
    out_shape = [_sds((s, w)), _sds((s, w))] + ([_sds((1, w)), _sds((1, w))] if with_da else [])
    return pl.pallas_call(
        body, name=name, grid=(w // lw,), in_specs=in_specs, out_specs=out_specs, out_shape=out_shape,
        compiler_params=pltpu.CompilerParams(dimension_semantics=("parallel",), vmem_limit_bytes=56 * 1024 * 1024),
    )(*ins)


S5_SUPER = 4


def _s5_mm(a, w4, mode, name, acc=None):
    s = a.shape[0]
    _, r, c = w4.shape
    wa, wo = (r, c) if mode == "nn" else (c, r)
    tm = _pick(s, (512, 256, 128))
    has_acc = acc is not None

    def body(*refs):
        if has_acc:
            a_ref, w_ref, c_ref, o_ref = refs
        else:
            a_ref, w_ref, o_ref = refs
        res = _nn(a_ref[...], w_ref[0]) if mode == "nn" else _nt(a_ref[...], w_ref[0])
        o_ref[...] = res + c_ref[...] if has_acc else res

    o_spec = pl.BlockSpec((tm, wo), lambda k, i: (i, k))
    ins = [a, w4] + ([acc] if has_acc else [])
    in_specs = [pl.BlockSpec((tm, wa), lambda k, i: (i, k)), pl.BlockSpec((1, r, c), lambda k, i: (k, 0, 0))]
    return pl.pallas_call(
        body, name=name, grid=(S5_SUPER, s // tm), in_specs=in_specs + ([o_spec] if has_acc else []), out_specs=o_spec,
        out_shape=_sds((s, S5_SUPER * wo)),
        compiler_params=pltpu.CompilerParams(dimension_semantics=("parallel", "parallel"), vmem_limit_bytes=48 * 1024 * 1024),
    )(*ins)


def _s5_group_tn(a, b, name):
    s = a.shape[0]
    tk = _pick(s, (512, 256, 128))
    nk = s // tk

    def body(a_ref, b_ref, o_ref, acc_ref):
        k = pl.program_id(1)

        @pl.when(k == 0)
        def _():
            acc_ref[...] = jnp.zeros(acc_ref.shape, F32)

        acc_ref[...] += _tn(a_ref[...], b_ref[...])

        @pl.when(k == nk - 1)
        def _():
            p = acc_ref[...]
            grp = lax.broadcasted_iota(jnp.int32, (LANES, S5_STATE), 0) // S5_GROUP
            out = jnp.zeros((LANES, S5_STATE), F32)
            for j in range(LANES // S5_GROUP):
                out = jnp.where(grp == j, p[:, S5_STATE * j:S5_STATE * (j + 1)], out)
            o_ref[...] = out

    return pl.pallas_call(
        body, name=name, grid=(S5_SUPER, nk),
        in_specs=[pl.BlockSpec((tk, LANES), lambda g, k: (k, g)), pl.BlockSpec((tk, 512), lambda g, k: (k, g))],
        out_specs=pl.BlockSpec((LANES, S5_STATE), lambda g, k: (g, 0)), out_shape=_sds((512, S5_STATE)),
        scratch_shapes=[pltpu.VMEM((LANES, 512), F32)],
        compiler_params=pltpu.CompilerParams(dimension_semantics=("parallel", "arbitrary")),
    )(a, b)


def _s5_seg1(y0, u, d):
    return jax.nn.gelu(y0 + d * u)


def _s5_seg2(z, t, b):
    return z * jax.nn.sigmoid(t + b)


def _s5_post_fwd(y0, seg2, d, wglu, bglu, name):
    s = y0.shape[0]
    ts = _pick(s, (256, 128))

    def fn(ins, outs, accs, scr):
        z = _s5_seg1(ins[0][...], ins[1][...], ins[2][...])
        outs[0][...] = _s5_seg2(z, _nn(z, ins[3][...]), ins[4][...])

    return _stage(name, fn, s // ts,
                  [(y0, _rows(ts, 512)), (seg2, _rows(ts, 512)), (d, _full(d.shape)), (wglu, _full(wglu.shape)),
                   (bglu, _full(bglu.shape))], [(_sds((s, 512)), _rows(ts, 512))])[0]


def _s5_post_bwd(y0, seg2, d, wglu, bglu, dy, name):
    s = y0.shape[0]
    ts = _pick(s, (256, 128))

    def fn(ins, outs, accs, scr):
        y0_r, u_r, d_r, w_r, b_r, dy_r = ins
        z, vjp1 = jax.vjp(_s5_seg1, y0_r[...], u_r[...], d_r[...])
        t = _nn(z, w_r[...])
        _, vjp2 = jax.vjp(_s5_seg2, z, t, b_r[...])
        dz, dt, db = vjp2(dy_r[...])
        accs[0][...] += _tn(z, dt)
        accs[1][...] += db
        dy0, du, dd = vjp1(dz + _nt(dt, w_r[...]))
        accs[2][...] += dd
        outs[0][...] = dy0
        outs[1][...] = du

    return _stage(name, fn, s // ts,
                  [(y0, _rows(ts, 512)), (seg2, _rows(ts, 512)), (d, _full(d.shape)), (wglu, _full(wglu.shape)),
                   (bglu, _full(bglu.shape)), (dy, _rows(ts, 512))],
                  [(_sds((s, 512)), _rows(ts, 512)), (_sds((s, 512)), _rows(ts, 512))],
                  accs=[_sds((512, 512)), _sds((1, 512)), _sds((1, 512))])


def _gate_a(y, g):
    return y * jax.nn.silu(g)


def _gate_m(o0, o1, o2, m0, m1, m2):
    return jax.nn.sigmoid(m0) * o0 + jax.nn.sigmoid(m1) * o1 + jax.nn.sigmoid(m2) * o2


def _assemble(ybuf, o_mla, o_fox, y_s5):
    for h in range(HEADS):
        ybuf[:, 64 * h:64 * (h + 1)] = o_mla[h]
        ybuf[:, 512 + 64 * h:512 + 64 * (h + 1)] = o_fox[h]
    ybuf[:, 1024:1536] = y_s5[...]


def _gate_fwd(o_mla, o_fox, y_s5, seg3, seg4, x, wo, wout, name):
    s = x.shape[0]
    ts = _pick(s, (256, 128))

    def fn(ins, outs, accs, scr):
        om, of, ys, g_r, m_r, x_r, wo_r, wout_r = ins
        ybuf = scr[0]
        _assemble(ybuf, om, of, ys)
        a = _gate_a(ybuf[...], g_r[...])
        o = [_nn(a[:, 512 * b:512 * (b + 1)], wo_r[512 * b:512 * (b + 1), :]) for b in range(3)]
        merged = _gate_m(o[0], o[1], o[2], m_r[:, 0:1024], m_r[:, 1024:2048], m_r[:, 2048:3072])
        outs[0][...] = x_r[...] + _nn(merged, wout_r[...])

    return _stage(name, fn, s // ts,
                  [(o_mla, _heads(ts, 64)), (o_fox, _heads(ts, 64)), (y_s5, _rows(ts, 512)), (seg3, _rows(ts, 1536)),
                   (seg4, _rows(ts, 3072)), (x, _rows(ts, D_MODEL)), (wo, _full(wo.shape)), (wout, _full(wout.shape))],
                  [(_sds((s, D_MODEL)), _rows(ts, D_MODEL))], scratch=[pltpu.VMEM((ts, 1536), F32)])[0]


def _gate_bwd(o_mla, o_fox, y_s5, seg3, seg4, wo, wout, dout, name):
    s = dout.shape[0]
    ts = _pick(s, (128,))

    def fn(ins, outs, accs, scr):
        om, of, ys, g_r, m_r, wo_r, wout_r, dout_r = ins
        do_mla, do_fox, dys, dg_r, dm_r = outs
        dwo, dwout = accs
        ybuf, dabuf = scr
        _assemble(ybuf, om, of, ys)
        a, vjp_a = jax.vjp(_gate_a, ybuf[...], g_r[...])
        o = [_nn(a[:, 512 * b:512 * (b + 1)], wo_r[512 * b:512 * (b + 1), :]) for b in range(3)]
        ms = [m_r[:, 1024 * b:1024 * (b + 1)] for b in range(3)]
        merged, vjp_m = jax.vjp(_gate_m, *o, *ms)
        dout_v = dout_r[...]
        dwout[...] += _tn(merged, dout_v)
        cts = vjp_m(_nt(dout_v, wout_r[...]))
        for b in range(3):
            dm_r[:, 1024 * b:1024 * (b + 1)] = cts[3 + b].astype(BF16)
            dwo[512 * b:512 * (b + 1), :] += _tn(a[:, 512 * b:512 * (b + 1)], cts[b])
            dabuf[:, 512 * b:512 * (b + 1)] = _nt(cts[b], wo_r[512 * b:512 * (b + 1), :])
        dy, dg = vjp_a(dabuf[...])
        dg_r[...] = dg.astype(BF16)
        dys[...] = dy[:, 1024:1536]
        for h in range(HEADS):
            do_mla[h] = dy[:, 64 * h:64 * (h + 1)]
            do_fox[h] = dy[:, 512 + 64 * h:512 + 64 * (h + 1)]

    return _stage(name, fn, s // ts,
                  [(o_mla, _heads(ts, 64)), (o_fox, _heads(ts, 64)), (y_s5, _rows(ts, 512)), (seg3, _rows(ts, 1536)),
                   (seg4, _rows(ts, 3072)), (wo, _full(wo.shape)), (wout, _full(wout.shape)), (dout, _rows(ts, D_MODEL))],
                  [(_sds((HEADS, s, 64)), _heads(ts, 64)), (_sds((HEADS, s, 64)), _heads(ts, 64)), (_sds((s, 512)), _rows(ts, 512)),
                   (_sds((s, 1536), BF16), _rows(ts, 1536)), (_sds((s, 3072), BF16), _rows(ts, 3072))],
                  accs=[_sds(wo.shape), _sds(wout.shape)], scratch=[pltpu.VMEM((ts, 1536), F32), pltpu.VMEM((ts, 1536), F32)],
                  vmem_mb=56)


def _loss_head(y, target, name):
    s = y.shape[0]
    ts = _pick(s, (256, 128))

    def fn(ins, outs, accs, scr):
        e = ins[0][...] - ins[1][...]
        outs[0][...] = e / D_MODEL
        accs[0][...] += 0.5 * jnp.sum(jnp.sum(e * e, axis=1, keepdims=True) / D_MODEL, axis=0, keepdims=True)

    r = _stage(name, fn, s // ts, [(y, _rows(ts, D_MODEL)), (target, _rows(ts, D_MODEL))],
               [(_sds((s, D_MODEL)), _rows(ts, D_MODEL))], accs=[_sds((1, 1))])
    return r[1], r[0]


def _pad_w_in(w):
    z = lambda n: jnp.zeros((w.shape[0], n), w.dtype)
    return jnp.concatenate([w[:, 0:384], w[:, 1952:1960], z(56), w[:, 384:416], z(32), w[:, 416:1952], w[:, 1960:IN_WIDTH]], axis=1)


def _unpad_w_in(d):
    return jnp.concatenate([d[:, 0:384], d[:, 448:480], d[:, 512:2048], d[:, 384:392], d[:, 2048:PAD_IN]], axis=1)


def _pad_wq(w):
    w = w.reshape(MLA_Q_RANK, HEADS, MLA_QK)
    return jnp.pad(w, ((0, 0), (0, 0), (0, LANES - MLA_QK))).reshape(MLA_Q_RANK, HEADS * LANES)


def _unpad_wq(d):
    return d.reshape(MLA_Q_RANK, HEADS, LANES)[:, :, :MLA_QK].reshape(MLA_Q_RANK, HEADS * MLA_QK)


def _pad_wkv(w):
    w = w.reshape(MLA_KV_RANK, HEADS, MLA_NOPE + MLA_V)
    k = jnp.pad(w[:, :, :MLA_NOPE], ((0, 0), (0, 0), (0, LANES - MLA_NOPE))).reshape(MLA_KV_RANK, HEADS * LANES)
    return jnp.concatenate([k, w[:, :, MLA_NOPE:].reshape(MLA_KV_RANK, HEADS * MLA_V)], axis=1)


def _unpad_wkv(d):
    k = d[:, :HEADS * LANES].reshape(MLA_KV_RANK, HEADS, LANES)[:, :, :MLA_NOPE]
    v = d[:, HEADS * LANES:].reshape(MLA_KV_RANK, HEADS, MLA_V)
    return jnp.concatenate([k, v], axis=2).reshape(MLA_KV_RANK, HEADS * (MLA_NOPE + MLA_V))


def _pad_lanes(v, n=LANES):
    return jnp.pad(v, (0, n - v.shape[0])).reshape(1, n)


def _super_blocks(b):
    _, r, c = b.shape
    per = S5_GROUPS // S5_SUPER
    b = b.reshape(S5_SUPER, per, r, c)
    eye = jnp.eye(per, dtype=b.dtype)
    return (b[:, :, :, None, :] * eye[None, :, None, :, None]).reshape(S5_SUPER, per * r, per * c)


def _layer_params(l, w, small):
    p = {k: small[k][l] for k in small}
    q = {}
    q["norm_g"] = p["norm_g"].reshape(1, D_MODEL)
    q["mla_q_a_norm"] = p["mla_q_a_norm"].reshape(1, 256)
    q["mla_kv_a_norm"] = p["mla_kv_a_norm"].reshape(1, 128)
    q["mla_q_norm"] = _pad_lanes(p["mla_q_norm"])
    q["mla_k_norm"] = _pad_lanes(p["mla_k_norm"])
    q["fox_b_f"] = _pad_lanes(p["fox_b_f"])
    q["fox_q_norm"] = p["fox_q_norm"].reshape(1, 64)
    q["fox_k_norm"] = p["fox_k_norm"].reshape(1, 64)
    q["s5_d"] = p["s5_d"].reshape(1, 512)
    q["s5_b_glu"] = p["s5_b_glu"].reshape(1, 512)
    rep = lambda z: jnp.repeat(z, S5_GROUP, axis=0)
    q["lr16"], q["li16"] = rep(p["s5_lambda_re"]), rep(p["s5_lambda_im"])
    q["ldt16"] = rep(jnp.broadcast_to(p["s5_log_dt"][:, None], (S5_GROUPS, S5_STATE)))
    q["br2"] = p["s5_b_re"].transpose(0, 2, 1).reshape(512, 64)
    q["bi2"] = p["s5_b_im"].transpose(0, 2, 1).reshape(512, 64)
    q["c_re"], q["c_im"] = p["s5_c_re"], p["s5_c_im"]
    win = _pad_w_in(w["w_in"])
    q["w_seg"] = [win[:, SEG_OFF[k]:SEG_OFF[k] + SEG_W[k]] for k in range(5)]
    q["wq"] = _pad_wq(w["mla_w_q_up"])
    q["wkv"] = _pad_wkv(w["mla_w_kv_up"])
    q["wglu"] = w["s5_w_glu"]
    q["wo"] = w["w_branch_out"]
    q["wout"] = w["w_out"]
    return q


def _layer_fwd(l, x, tabs, q, rides=None):
    rides = rides or {}
    n = lambda s: f"l{l}_{s}"
    sv = {"x": x}
    h = _norm_fwd(x, q["norm_g"], n("norm_fwd"))
    sv["h"] = h
    seg = [_mm(h, q["w_seg"][k], "nn", n(f"proj{k}")) for k in range(5)]
    sv["seg"] = seg
    mq, mk, mv = _mla_prep_fwd(seg[0], tabs, q, q["wq"], q["wkv"], n("mla_prep_fwd"))
    o_mla, lse_mla = _attn_fwd(mq, mk, mv, 1.0 / math.sqrt(MLA_QK), True, n("mla_attn_fwd"), ride=rides.get("mla_attn_fwd"))
    sv.update(mq=mq, mk=mk, mv=mv, o_mla=o_mla, lse_mla=lse_mla)
    fq, fk, fv, c, ct = _fox_prep_fwd(seg[0], seg[1], q["fox_b_f"], q["fox_q_norm"], q["fox_k_norm"], n("fox_prep_fwd"))
    o_fox, lse_fox = _attn_fwd(fq, fk, fv, 1.0 / math.sqrt(FOX_DIM), False, n("fox_attn_fwd"), c=c, ct=ct,
                               ride=rides.get("fox_attn_fwd"))
    sv.update(fq=fq, fk=fk, fv=fv, c=c, ct=ct, o_fox=o_fox, lse_fox=lse_fox)
    a_re16, a_im16, bb_re, bb_im = _s5_params_fwd(q["lr16"], q["li16"], q["ldt16"], q["br2"], q["bi2"], n("s5_params_fwd"))
    a_re = a_re16.reshape(S5_GROUPS, S5_GROUP, S5_STATE)[:, 0, :].reshape(1, S5_LANES)
    a_im = a_im16.reshape(S5_GROUPS, S5_GROUP, S5_STATE)[:, 0, :].reshape(1, S5_LANES)
    wb_re = _super_blocks(bb_re.reshape(S5_GROUPS, S5_GROUP, S5_STATE)).astype(BF16)
    wb_im = _super_blocks(bb_im.reshape(S5_GROUPS, S5_GROUP, S5_STATE)).astype(BF16)
    wc_re = _super_blocks(q["c_re"].transpose(0, 2, 1)).astype(BF16)
    wc_im = _super_blocks(-q["c_im"].transpose(0, 2, 1)).astype(BF16)
    bu_re = _s5_mm(seg[2], wb_re, "nn", n("s5_bu_re"))
    bu_im = _s5_mm(seg[2], wb_im, "nn", n("s5_bu_im"))
    x_re, x_im = _scan(bu_re, bu_im, a_re, a_im, False, n("s5_scan_fwd"))
    y0 = _s5_mm(x_im, wc_im, "nn", n("s5_y_im"), acc=_s5_mm(x_re, wc_re, "nn", n("s5_y_re")))
    y_s5 = _s5_post_fwd(y0, seg[2], q["s5_d"], q["wglu"], q["s5_b_glu"], n("s5_post_fwd"))
    sv.update(a_re=a_re, a_im=a_im, wb_re=wb_re, wb_im=wb_im, wc_re=wc_re, wc_im=wc_im, x_re=x_re, x_im=x_im, y0=y0, y_s5=y_s5)
    out = _gate_fwd(o_mla, o_fox, y_s5, seg[3], seg[4], x, q["wo"], q["wout"], n("gate_fwd"))
    return out, sv


def _layer_bwd(l, dout, tabs, q, sv, rides=None):
    rides = rides or {}
    n = lambda s: f"l{l}_{s}"
    seg = sv["seg"]
    g = {}
    (do_mla, do_fox, dy_s5, dseg3, dseg4, g["wo"], g["wout"]) = _gate_bwd(
        sv["o_mla"], sv["o_fox"], sv["y_s5"], seg[3], seg[4], q["wo"], q["wout"], dout, n("gate_bwd"))
    dy0, du_a, g["wglu"], g["s5_b_glu"], g["s5_d"] = _s5_post_bwd(sv["y0"], seg[2], q["s5_d"], q["wglu"], q["s5_b_glu"], dy_s5,
                                                                 n("s5_post_bwd"))
    dx_re = _s5_mm(dy0, sv["wc_re"], "nt", n("s5_dx_re"))
    dx_im = _s5_mm(dy0, sv["wc_im"], "nt", n("s5_dx_im"))
    g["s5_c_re"] = _s5_group_tn(dy0, sv["x_re"], n("s5_dc_re")).reshape(S5_GROUPS, S5_GROUP, S5_STATE)
    g["s5_c_im"] = -_s5_group_tn(dy0, sv["x_im"], n("s5_dc_im")).reshape(S5_GROUPS, S5_GROUP, S5_STATE)
    g_re, g_im, da_re, da_im = _scan(dx_re, dx_im, sv["a_re"], -sv["a_im"], True, n("s5_scan_bwd"), x_re=sv["x_re"], x_im=sv["x_im"])
    dseg2 = _s5_mm(g_im, sv["wb_im"], "nt", n("s5_du_im"), acc=_s5_mm(g_re, sv["wb_re"], "nt", n("s5_du_re"), acc=du_a))
    dbb_re = _s5_group_tn(seg[2], g_re, n("s5_dbb_re"))
    dbb_im = _s5_group_tn(seg[2], g_im, n("s5_dbb_im"))
    first = (jnp.arange(512) % S5_GROUP == 0).astype(F32)[:, None]
    da_re16 = jnp.repeat(da_re.reshape(S5_GROUPS, S5_STATE), S5_GROUP, axis=0) * first
    da_im16 = jnp.repeat(da_im.reshape(S5_GROUPS, S5_STATE), S5_GROUP, axis=0) * first
    dlr, dli, dldt, dbr2, dbi2 = _s5_params_bwd(q["lr16"], q["li16"], q["ldt16"], q["br2"], q["bi2"], da_re16, da_im16, dbb_re,
                                               dbb_im, n("s5_params_bwd"))
    g["s5_lambda_re"], g["s5_lambda_im"], g["s5_log_dt"] = dlr, dli, dldt.reshape(S5_GROUPS)
    g["s5_b_re"] = dbr2.reshape(S5_GROUPS, S5_GROUP, S5_STATE).transpose(0, 2, 1)
    g["s5_b_im"] = dbi2.reshape(S5_GROUPS, S5_GROUP, S5_STATE).transpose(0, 2, 1)
    dl_fox = _attn_delta(sv["fq"], sv["fk"], sv["fv"], do_fox, sv["lse_fox"], 1.0 / math.sqrt(FOX_DIM), False,
                         n("fox_attn_delta"), c=sv["c"], ct=sv["ct"], ride=rides.get("fox_attn_delta"))
    dfq, dfk, dfv, dck = _attn_bwd(sv["fq"], sv["fk"], sv["fv"], do_fox, sv["lse_fox"], dl_fox, 1.0 / math.sqrt(FOX_DIM), False,
                                   n("fox_attn_bwd"), c=sv["c"], ct=sv["ct"], ride=rides.get("fox_attn_bwd"))
    dseg1, dff, g["fox_q_norm"], g["fox_k_norm"], dbf = _fox_prep_bwd(seg[0], seg[1], q["fox_b_f"], q["fox_q_norm"], q["fox_k_norm"],
                                                                      dfq, dfk, dfv, dck, n("fox_prep_bwd"))
    g["fox_b_f"] = dbf[0, :HEADS]
    dl_mla = _attn_delta(sv["mq"], sv["mk"], sv["mv"], do_mla, sv["lse_mla"], 1.0 / math.sqrt(MLA_QK), True, n("mla_attn_delta"))
    dmq, dmk, dmv = _attn_bwd(sv["mq"], sv["mk"], sv["mv"], do_mla, sv["lse_mla"], dl_mla, 1.0 / math.sqrt(MLA_QK), True,
                              n("mla_attn_bwd"))
    dseg0, dqan, dkvan, dqn, dkn, g["wq"], g["wkv"] = _mla_prep_bwd(seg[0], tabs, q, q["wq"], q["wkv"], dmq, dmk, dmv, dff,
                                                                   n("mla_prep_bwd"))
    g["mla_q_a_norm"], g["mla_kv_a_norm"] = dqan, dkvan
    g["mla_q_norm"], g["mla_k_norm"] = dqn[0, :MLA_QK], dkn[0, :MLA_QK]
    dsegs = [dseg0, dseg1, dseg2, dseg3, dseg4]
    dh = None
    for k in range(5):
        dh = _mm(dsegs[k], q["w_seg"][k], "nt", n(f"dh{k}"), acc=dh)
    g["w_in"] = jnp.concatenate([_mm(sv["h"], dsegs[k], "tn", n(f"dwin{k}")) for k in range(5)], axis=1)
    dx, g["norm_g"] = _norm_bwd(sv["x"], q["norm_g"], dh, dout, n("norm_bwd"))
    return dx, g


MESH = pl.DeviceIdType.MESH
ANY = pl.BlockSpec(memory_space=pl.ANY)


def _all_gather(blocks, name):
    n = len(blocks)

    def body(*refs):
        x_refs, out_refs = refs[:n], refs[n:2 * n]
        send_sems, recv_sems, local_sems = refs[2 * n:]
        x, y, c = lax.axis_index("x"), lax.axis_index("y"), lax.axis_index("c")
        me, sibling = (x, y, c), (x, y, 1 - c)
        chips = [(1 - x, y), (x, 1 - y), (1 - x, 1 - y)]

        def slot(a, px, py, pc):
            return out_refs[a].at[4 * px + 2 * py + pc]

        def copy(a, k, blk, to, src=None):
            return pltpu.make_async_remote_copy(src_ref=slot(a, *blk) if src is None else src, dst_ref=slot(a, *blk),
                                                send_sem=send_sems.at[7 * a + k], recv_sem=recv_sems.at[7 * a + k],
                                                device_id=to, device_id_type=MESH)

        mine = [pltpu.make_async_copy(x_refs[a], slot(a, *me), local_sems.at[a]) for a in range(n)]
        for cp in mine:
            cp.start()
        first = []
        for j, chip in enumerate(chips):
            first += [copy(a, 1 + j, me, (*chip, c), src=x_refs[a]) for a in range(n)]
        first += [copy(a, 0, me, sibling, src=x_refs[a]) for a in range(n)]
        for cp in first:
            cp.start()
        passed = []
        for j, chip in enumerate(chips):
            for a in range(n):
                copy(a, 1 + j, (*chip, c), me).wait_recv()
                passed.append(copy(a, 4 + j, (*chip, c), sibling))
                passed[-1].start()
        for a in range(n):
            copy(a, 0, sibling, me).wait_recv()
        for j, chip in enumerate(chips):
            for a in range(n):
                copy(a, 4 + j, (*chip, 1 - c), me).wait_recv()
        for cp in first + passed:
            cp.wait_send()
        for cp in mine:
            cp.wait()

    return pl.pallas_call(
        body, name=name, out_shape=[jax.ShapeDtypeStruct((N_DEV,) + b.shape, b.dtype) for b in blocks],
        in_specs=[ANY] * n, out_specs=[ANY] * n,
        scratch_shapes=[pltpu.SemaphoreType.DMA((7 * n,)), pltpu.SemaphoreType.DMA((7 * n,)), pltpu.SemaphoreType.DMA((n,))],
    )(*blocks)


def _place():
    x, y, c = lax.axis_index("x"), lax.axis_index("y"), lax.axis_index("c")
    return x, y, c, [(1 - x, y), (x, 1 - y), (1 - x, 1 - y)]


def _remote(src, dst, send, recv, k, to):
    return pltpu.make_async_remote_copy(src_ref=src, dst_ref=dst, send_sem=send.at[k], recv_sem=recv.at[k], device_id=to,
                                        device_id_type=MESH)


def _plan_gather_ici(blocks):
    n = len(blocks)

    def copies(in_refs, out_refs, send, recv, local):
        x, y, c, chips = _place()
        mine = 4 * x + 2 * y + c
        loc = [pltpu.make_async_copy(in_refs[a], out_refs[a].at[mine], local.at[a]) for a in range(n)]
        rem = [_remote(in_refs[a], out_refs[a].at[mine], send, recv, 3 * a + j, (px, py, c))
               for j, (px, py) in enumerate(chips) for a in range(n)]
        return rem, loc

    return _Plan(blocks, [jax.ShapeDtypeStruct((N_DEV,) + b.shape, b.dtype) for b in blocks], 3 * n, n, copies)


def _plan_gather_d2d(gathered):
    n = len(gathered)

    def copies(in_refs, out_refs, send, recv, local):
        x, y, c, _ = _place()
        rem = [_remote(in_refs[a].at[2 * j + c], out_refs[a].at[2 * j + c], send, recv, 4 * a + j, (x, y, 1 - c))
               for a in range(n) for j in range(4)]
        return rem, []

    return _Plan(gathered, [jax.ShapeDtypeStruct(g.shape, g.dtype) for g in gathered], 4 * n, 0, copies,
                 aliases={a: a for a in range(n)})


def _plan_reduce_sibling(parts):
    n = len(parts)

    def copies(in_refs, out_refs, send, recv, local):
        x, y, c, _ = _place()
        rem = [_remote(in_refs[a].at[2 * j + (1 - c)], out_refs[a].at[j], send, recv, 4 * a + j, (x, y, 1 - c))
               for a in range(n) for j in range(4)]
        return rem, []

    return _Plan(parts, [jax.ShapeDtypeStruct((4,) + p.shape[1:], p.dtype) for p in parts], 4 * n, 0, copies)


def _plan_reduce_chips(sums):
    n = len(sums)

    def copies(in_refs, out_refs, send, recv, local):
        x, y, c, chips = _place()
        mine = 2 * x + y
        loc = [pltpu.make_async_copy(in_refs[a].at[mine], out_refs[a].at[mine], local.at[a]) for a in range(n)]
        rem = [_remote(in_refs[a].at[2 * px + py], out_refs[a].at[mine], send, recv, 3 * a + k, (px, py, c))
               for k, (px, py) in enumerate(chips) for a in range(n)]
        return rem, loc

    return _Plan(sums, [jax.ShapeDtypeStruct(p.shape, p.dtype) for p in sums], 3 * n, n, copies)


def _add_sibling(parts, got, name):
    _, r, cc = parts.shape
    tr = _pick(r, (512, 256, 128, 64, 32, 16))
    c = lax.axis_index("c")

    def body(c_ref, p_ref, g_ref, o_ref):
        o_ref[...] = (p_ref[...] + g_ref[...]).astype(BF16)

    return pl.pallas_call(
        body, name=name, out_shape=jax.ShapeDtypeStruct((4, r, cc), BF16),
        grid_spec=pltpu.PrefetchScalarGridSpec(
            num_scalar_prefetch=1, grid=(4, r // tr),
            in_specs=[pl.BlockSpec((1, tr, cc), lambda j, i, cr: (2 * j + cr[0], i, 0)),
                      pl.BlockSpec((1, tr, cc), lambda j, i, cr: (j, i, 0))],
            out_specs=pl.BlockSpec((1, tr, cc), lambda j, i, cr: (j, i, 0))),
        compiler_params=_vmem(48),
    )(c.reshape(1).astype(jnp.int32), parts, got)


def _sum_leading(parts, name):
    k, r, cc = parts.shape
    tr = _pick(r, (512, 256, 128, 64, 32, 16, 8))

    def body(p_ref, o_ref):
        acc = p_ref[0]
        for j in range(1, k):
            acc = acc + p_ref[j]
        o_ref[...] = acc

    return pl.pallas_call(
        body, name=name, out_shape=jax.ShapeDtypeStruct((r, cc), F32), grid=(r // tr,),
        in_specs=[pl.BlockSpec((k, tr, cc), lambda i: (0, i, 0))], out_specs=pl.BlockSpec((tr, cc), lambda i: (i, 0)),
    )(parts)


def _adamw_math(w, g, m, v):
    nm = ADAM_B1 * m + (1.0 - ADAM_B1) * g
    nv = ADAM_B2 * v + (1.0 - ADAM_B2) * jnp.square(g)
    m_hat = nm / (1.0 - ADAM_B1 ** ADAM_STEP)
    v_hat = nv / (1.0 - ADAM_B2 ** ADAM_STEP)
    return -ADAM_LR * (m_hat / (jnp.sqrt(v_hat) + ADAM_EPS) + ADAM_WD * w), nm, nv


def _adamw_sum(w, contribs, m, v, name):
    nl = len(contribs)
    k, r, cc = contribs[0].shape
    tr = _pick(r, (256, 128, 64, 32, 16))
    nb = r // tr

    def body(w_ref, *rest):
        c_refs = rest[:nl]
        m_ref, v_ref, g_ref, d_ref, nm_ref, nv_ref = rest[nl:]
        for li in range(nl):
            @pl.when(pl.program_id(0) == li)
            def _(c_ref=c_refs[li]):
                g = c_ref[0].astype(F32)
                for j in range(1, k):
                    g = g + c_ref[j].astype(F32)
                g_ref[...] = g
                d_ref[...], nm_ref[...], nv_ref[...] = _adamw_math(w_ref[...], g, m_ref[...], v_ref[...])

    spec = pl.BlockSpec((tr, cc), lambda l, i: (l * nb + i, 0))
    cspec = pl.BlockSpec((k, tr, cc), lambda l, i: (0, i, 0))
    return pl.pallas_call(
        body, name=name, grid=(nl, nb), in_specs=[spec] + [cspec] * nl + [spec, spec],
        out_specs=[spec] * 4, out_shape=[jax.ShapeDtypeStruct(w.shape, F32)] * 4,
        compiler_params=pltpu.CompilerParams(dimension_semantics=("parallel", "parallel"), vmem_limit_bytes=48 * 1024 * 1024),
    )(w, *contribs, m, v)


def _adamw_many(ws, gs, ms, vs, name):
    n = len(ws)

    def body(*refs):
        w_r, g_r, m_r, v_r = refs[:n], refs[n:2 * n], refs[2 * n:3 * n], refs[3 * n:4 * n]
        d_r, nm_r, nv_r = refs[4 * n:5 * n], refs[5 * n:6 * n], refs[6 * n:7 * n]
        for a in range(n):
            d_r[a][...], nm_r[a][...], nv_r[a][...] = _adamw_math(w_r[a][...], g_r[a][...], m_r[a][...], v_r[a][...])

    shapes = [jax.ShapeDtypeStruct(w.shape, F32) for w in ws]
    res = pl.pallas_call(body, name=name, out_shape=shapes * 3,
                         compiler_params=pltpu.CompilerParams(vmem_limit_bytes=56 * 1024 * 1024))(*ws, *gs, *ms, *vs)
    return res[:n], res[n:2 * n], res[2 * n:]


def _pack_rows(flat, lanes, row_mult):
    n = flat.shape[-1]
    rows = -(-n // lanes)
    rows = -(-rows // row_mult) * row_mult
    pad = rows * lanes - n
    if pad:
        flat = jnp.pad(flat, [(0, 0)] * (flat.ndim - 1) + [(0, pad)])
    return flat.reshape(flat.shape[:-1] + (rows, lanes))


def _rope_tables(positions):
    inv = 1.0 / (ROPE_THETA ** (jnp.arange(0, MLA_ROPE, 2, dtype=F32) / MLA_ROPE))
    ang = positions.astype(F32)[:, None] * inv
    cos, sin = jnp.cos(ang), jnp.sin(ang)
    s = positions.shape[0]
    z = lambda n: jnp.zeros((s, n), F32)
    c = jnp.concatenate([jnp.ones((s, 64), F32), cos, cos, z(32)], axis=1)
    sa = jnp.concatenate([z(64), -sin, z(48)], axis=1)
    sb = jnp.concatenate([z(80), sin, z(32)], axis=1)
    return c, sa, sb


def _full_weights(gathered):
    full = {}
    for k, g in zip(SHARDED, gathered):
        _, r, c = g.shape
        full[k] = g.transpose(1, 0, 2).reshape(r, N_DEV * c) if k in COL_SHARDED else g.reshape(N_DEV * r, c)
    return full


def _owner_major(g):
    big = {"w_in": _unpad_w_in(g["w_in"]), "mla_w_q_up": _unpad_wq(g["wq"]), "mla_w_kv_up": _unpad_wkv(g["wkv"]),
           "s5_w_glu": g["wglu"], "w_branch_out": g["wo"], "w_out": g["wout"]}
    parts = []
    for k in SHARDED:
        r, c = big[k].shape
        if k in COL_SHARDED:
            parts.append(big[k].reshape(r, N_DEV, c // N_DEV).transpose(1, 0, 2))
        else:
            parts.append(big[k].reshape(N_DEV, r // N_DEV, c))
    return parts


def _device_step(x, positions, target, shards, small):
    tabs = _rope_tables(positions)
    box = {}
    full0 = _full_weights(_all_gather(shards[0], "gather_weights_l0"))
    q0 = _layer_params(0, full0, small)
    rides = {"mla_attn_fwd": _Ride(lambda: _plan_gather_ici(shards[1]), lambda outs: box.update(ici=outs)),
             "fox_attn_fwd": _Ride(lambda: _plan_gather_d2d(box["ici"]), lambda outs: box.update(w1=outs))}
    h, sv0 = _layer_fwd(0, x, tabs, q0, rides)
    q1 = _layer_params(1, _full_weights(box["w1"]), small)
    h, sv1 = _layer_fwd(1, h, tabs, q1)
    loss, d = _loss_head(h, target, "loss_head")
    d, g1 = _layer_bwd(1, d, tabs, q1, sv1)
    parts1 = _owner_major(g1)

    def chips_plan(parts, got, tag):
        return _plan_reduce_chips([_add_sibling(p, g, f"reduce_add_{tag}_{k}") for k, p, g in zip(SHARDED, parts, got)])

    rides = {"fox_attn_delta": _Ride(lambda: _plan_reduce_sibling(parts1), lambda outs: box.update(got1=outs)),
             "fox_attn_bwd": _Ride(lambda: chips_plan(parts1, box["got1"], "l1"), lambda outs: box.update(contribs1=outs))}
    d, g0 = _layer_bwd(0, d, tabs, q0, sv0, rides)
    parts0 = _owner_major(g0)
    got0 = _run_plan(_plan_reduce_sibling(parts0), "reduce_sibling_l0")
    contribs0 = _run_plan(chips_plan(parts0, got0, "l0"), "reduce_chips_l0")
    return loss[0, 0], d, [g0, g1], [contribs0, box["contribs1"]]


def kernel(x, positions, norm_g, w_in, mla_q_a_norm, mla_w_q_up, mla_kv_a_norm, mla_w_kv_up, mla_q_norm, mla_k_norm, fox_b_f, fox_q_norm, fox_k_norm, s5_lambda_re, s5_lambda_im, s5_log_dt, s5_b_re, s5_b_im, s5_c_re, s5_c_im, s5_d, s5_w_glu, s5_b_glu, w_branch_out, w_out, loss_target, m_norm_g, m_w_in, m_mla_q_a_norm, m_mla_w_q_up, m_mla_kv_a_norm, m_mla_w_kv_up, m_mla_q_norm, m_mla_k_norm, m_fox_b_f, m_fox_q_norm, m_fox_k_norm, m_s5_lambda_re, m_s5_lambda_im, m_s5_log_dt, m_s5_b_re, m_s5_b_im, m_s5_c_re, m_s5_c_im, m_s5_d, m_s5_w_glu, m_s5_b_glu, m_w_branch_out, m_w_out, v_norm_g, v_w_in, v_mla_q_a_norm, v_mla_w_q_up, v_mla_kv_a_norm, v_mla_w_kv_up, v_mla_q_norm, v_mla_k_norm, v_fox_b_f, v_fox_q_norm, v_fox_k_norm, v_s5_lambda_re, v_s5_lambda_im, v_s5_log_dt, v_s5_b_re, v_s5_b_im, v_s5_c_re, v_s5_c_im, v_s5_d, v_s5_w_glu, v_s5_b_glu, v_w_branch_out, v_w_out):
    env = dict(locals())
    wts = {k: env[k] for k in WEIGHTS}
    mom = {k: env["m_" + k] for k in WEIGHTS}
    var = {k: env["v_" + k] for k in WEIGHTS}

    shards = [[wts[k][l].astype(BF16) for k in SHARDED] for l in range(DEPTH)]
    small = {k: wts[k] for k in SMALL}
    loss, dx, grads, contribs = _device_step(x[0], positions[0], loss_target[0], shards, small)
    loss = lax.psum(loss, ("x", "y", "c"))

    two_d = {k: (wts[k].shape[0] * wts[k].shape[1], wts[k].shape[2]) for k in SHARDED}
    grad_out, delta_out, m_out, v_out = {}, {}, {}, {}
    for i, k in enumerate(SHARDED):
        shp = wts[k].shape
        res = _adamw_sum(wts[k].reshape(two_d[k]), [contribs[l][i] for l in range(DEPTH)], mom[k].reshape(two_d[k]),
                         var[k].reshape(two_d[k]), f"adamw_{k}")
        grad_out[k], delta_out[k], m_out[k], v_out[k] = (z.reshape(shp) for z in res)

    sm = {k: jnp.stack([g[k] for g in grads]).reshape(wts[k].shape) for k in SMALL}
    flat_s = jnp.concatenate([sm[k].reshape(-1) for k in SMALL])
    g_small = _sum_leading(_all_gather([_pack_rows(flat_s, LANES, 256)], "gather_small_grads")[0], "sum_small_grads").reshape(-1)
    off = 0
    for k in SMALL:
        cnt = int(np.prod(wts[k].shape))
        grad_out[k] = g_small[off:off + cnt].reshape(wts[k].shape)
        off += cnt
    flat2 = lambda a: a.reshape(-1, a.shape[-1])
    d_s, m_s, v_s = _adamw_many([flat2(wts[k]) for k in SMALL], [flat2(grad_out[k]) for k in SMALL],
                                [flat2(mom[k]) for k in SMALL], [flat2(var[k]) for k in SMALL], "adamw_small")
    for i, k in enumerate(SMALL):
        delta_out[k], m_out[k], v_out[k] = (z[i].reshape(wts[k].shape) for z in (d_s, m_s, v_s))

    return (loss, dx[None], *[grad_out[k] for k in WEIGHTS], *[delta_out[k] for k in WEIGHTS],
            *[m_out[k] for k in WEIGHTS], *[v_out[k] for k in WEIGHTS])
```

```python
import functools
import math

import jax
import jax.numpy as jnp
import numpy as np
from jax import lax
from jax.experimental import pallas as pl
from jax.experimental.pallas import tpu as pltpu

F32 = jnp.float32
BF16 = jnp.bfloat16

D_MODEL = 1024
DEPTH = 2
CHUNK = 64
EPS = 1e-6
HEADS = 8
MLA_NOPE, MLA_ROPE, MLA_V = 64, 32, 64
MLA_Q_RANK, MLA_KV_RANK = 256, 128
MLA_QK = MLA_NOPE + MLA_ROPE
ROPE_THETA = 10000.0
FOX_DIM = 64
S5_WIDTH, S5_GROUP, S5_GROUPS, S5_STATE = 512, 16, 32, 64
S5_LANES = S5_GROUPS * S5_STATE
IN_WIDTH = 7080
N_DEV = 8
LANES = 128
SUBLANES = 8

ADAM_LR, ADAM_B1, ADAM_B2, ADAM_EPS, ADAM_WD, ADAM_STEP = 0.001, 0.9, 0.999, 1e-08, 0.01, 10

SEG_W = (512, 1536, 512, 1536, 3072)
SEG_OFF = (0, 512, 2048, 2560, 4096)
PAD_IN = 7168
NEG = -1e30

SHARDED = ("w_in", "mla_w_q_up", "mla_w_kv_up", "s5_w_glu", "w_branch_out", "w_out")
COL_SHARDED = ("w_in", "mla_w_q_up", "mla_w_kv_up")
SMALL = ("norm_g", "mla_q_a_norm", "mla_kv_a_norm", "mla_q_norm", "mla_k_norm", "fox_b_f", "fox_q_norm", "fox_k_norm",
         "s5_lambda_re", "s5_lambda_im", "s5_log_dt", "s5_b_re", "s5_b_im", "s5_c_re", "s5_c_im", "s5_d", "s5_b_glu")
WEIGHTS = ("norm_g", "w_in", "mla_q_a_norm", "mla_w_q_up", "mla_kv_a_norm", "mla_w_kv_up", "mla_q_norm", "mla_k_norm",
           "fox_b_f", "fox_q_norm", "fox_k_norm", "s5_lambda_re", "s5_lambda_im", "s5_log_dt", "s5_b_re", "s5_b_im",
           "s5_c_re", "s5_c_im", "s5_d", "s5_w_glu", "s5_b_glu", "w_branch_out", "w_out")
PACK_LANES = 512


def _pick(n, cands):
    for c in cands:
        if n % c == 0:
            return c
    return n


def _vmem(mb):
    return pltpu.CompilerParams(vmem_limit_bytes=mb * 1024 * 1024)


def _dot(a, b, dims):
    return lax.dot_general(a.astype(BF16), b.astype(BF16), (dims, ((), ())), preferred_element_type=F32)


def _nn(a, b):
    return _dot(a, b, ((1,), (0,)))


def _nt(a, b):
    return _dot(a, b, ((1,), (1,)))


def _tn(a, b):
    return _dot(a, b, ((0,), (0,)))


def _rms(x, g, n=None):
    n = x.shape[-1] if n is None else n
    return x * lax.rsqrt(jnp.sum(x * x, axis=-1, keepdims=True) / n + EPS) * g


def _rope(t, c, sa, sb):
    return t * c + pltpu.roll(t, LANES - 16, 1) * sa + pltpu.roll(t, 16, 1) * sb


def _rope_t(d, c, sa, sb):
    return d * c + pltpu.roll(d * sa, 16, 1) + pltpu.roll(d * sb, LANES - 16, 1)


def _mm(a, b, mode, name, acc=None, b_cols=None):
    if mode == "tn":
        kd, m = a.shape
    else:
        m, kd = a.shape
    b_off, b_w = b_cols if b_cols is not None else (0, b.shape[1])
    n = b.shape[0] if mode == "nt" else b_w
    tm, tn, tk = _pick(m, (1024, 512, 256, 128)), _pick(n, (1024, 512, 256, 128)), _pick(kd, (512, 256, 128))
    nk = kd // tk
    if mode == "tn":
        a_spec = pl.BlockSpec((tk, tm), lambda i, j, k: (k, i))
    else:
        a_spec = pl.BlockSpec((tm, tk), lambda i, j, k: (i, k))
    if mode == "nt":
        assert b_off % tk == 0
        b_spec = pl.BlockSpec((tn, tk), lambda i, j, k: (j, k + b_off // tk))
    else:
        assert b_off % tn == 0
        b_spec = pl.BlockSpec((tk, tn), lambda i, j, k: (k, j + b_off // tn))
    dims = {"nn": ((1,), (0,)), "nt": ((1,), (1,)), "tn": ((0,), (0,))}[mode]
    o_spec = pl.BlockSpec((tm, tn), lambda i, j, k: (i, j))
    has_acc = acc is not None

    def body(*refs):
        if has_acc:
            a_ref, b_ref, c_ref, o_ref, acc_ref = refs
        else:
            a_ref, b_ref, o_ref, acc_ref = refs
        k = pl.program_id(2)

        @pl.when(k == 0)
        def _():
            acc_ref[...] = c_ref[...] if has_acc else jnp.zeros(acc_ref.shape, F32)

        acc_ref[...] += _dot(a_ref[...], b_ref[...], dims)

        @pl.when(k == nk - 1)
        def _():
            o_ref[...] = acc_ref[...]

    ins = [a, b] + ([acc] if has_acc else [])
    in_specs = [a_spec, b_spec] + ([o_spec] if has_acc else [])
    return pl.pallas_call(
        body, name=name, grid=(m // tm, n // tn, nk), in_specs=in_specs, out_specs=o_spec,
        out_shape=jax.ShapeDtypeStruct((m, n), F32), scratch_shapes=[pltpu.VMEM((tm, tn), F32)],
        compiler_params=pltpu.CompilerParams(dimension_semantics=("parallel", "parallel", "arbitrary"),
                                             vmem_limit_bytes=48 * 1024 * 1024),
    )(*ins)


def _stage(name, fn, n_steps, ins, outs, accs=(), scratch=(), vmem_mb=48):
    n_in, n_out, n_acc = len(ins), len(outs), len(accs)

    def body(*refs):
        in_refs = refs[:n_in]
        out_refs = refs[n_in:n_in + n_out]
        acc_refs = refs[n_in + n_out:n_in + n_out + n_acc]
        scr = refs[n_in + n_out + n_acc:]
        if n_acc:
            @pl.when(pl.program_id(0) == 0)
            def _():
                for r in acc_refs:
                    r[...] = jnp.zeros(r.shape, r.dtype)
        fn(in_refs, out_refs, acc_refs, scr)

    acc_specs = [pl.BlockSpec(a.shape, functools.partial(lambda i, nd: (0,) * nd, nd=len(a.shape))) for a in accs]
    res = pl.pallas_call(
        body, name=name, grid=(n_steps,),
        in_specs=[s for _, s in ins], out_specs=[s for _, s in outs] + acc_specs,
        out_shape=[s for s, _ in outs] + list(accs), scratch_shapes=list(scratch),
        compiler_params=pltpu.CompilerParams(dimension_semantics=("arbitrary",),
                                             vmem_limit_bytes=vmem_mb * 1024 * 1024),
    )(*[a for a, _ in ins])
    return res


def _rows(ts, w, j=0):
    return pl.BlockSpec((ts, w), lambda i: (i, j))


def _rows_rev(ts, w, n, j=0):
    return pl.BlockSpec((ts, w), lambda i: (n - 1 - i, j))


def _heads(ts, d):
    return pl.BlockSpec((HEADS, ts, d), lambda i: (0, i, 0))


def _heads_rev(ts, d, n):
    return pl.BlockSpec((HEADS, ts, d), lambda i: (0, n - 1 - i, 0))


def _full(shape):
    nd = len(shape)
    return pl.BlockSpec(tuple(shape), lambda i: (0,) * nd)


def _sds(shape, dtype=F32):
    return jax.ShapeDtypeStruct(tuple(shape), dtype)


def _norm_fwd(x, g, name):
    s = x.shape[0]
    ts = _pick(s, (256, 128))

    def fn(ins, outs, accs, scr):
        outs[0][...] = _rms(ins[0][...], ins[1][...]).astype(BF16)

    return _stage(name, fn, s // ts, [(x, _rows(ts, D_MODEL)), (g, _full(g.shape))],
                  [(_sds((s, D_MODEL), BF16), _rows(ts, D_MODEL))])[0]


def _norm_bwd(x, g, dh, dres, name):
    s = x.shape[0]
    ts = _pick(s, (256, 128))

    def fn(ins, outs, accs, scr):
        _, vjp = jax.vjp(_rms, ins[0][...], ins[1][...])
        dx, dg = vjp(ins[2][...])
        outs[0][...] = dx + ins[3][...]
        accs[0][...] += dg

    r = _stage(name, fn, s // ts,
               [(x, _rows(ts, D_MODEL)), (g, _full(g.shape)), (dh, _rows(ts, D_MODEL)), (dres, _rows(ts, D_MODEL))],
               [(_sds((s, D_MODEL)), _rows(ts, D_MODEL))], accs=[_sds((1, D_MODEL))])
    return r[0], r[1]


def _mla_q(qraw, c, sa, sb, qn):
    return _rms(_rope(qraw, c, sa, sb), qn, MLA_QK)


def _mla_prep_fwd(seg0, tabs, p, wq, wkv, name):
    s = seg0.shape[0]
    ts = _pick(s, (256, 128))

    def fn(ins, outs, accs, scr):
        blk, cos, sa, sb, qan, kvan, qn, kn, wq_r, wkv_r = ins
        b = blk[...]
        cq, ckv, kt = b[:, :256], b[:, 256:384], b[:, 384:512]
        lane = lax.broadcasted_iota(jnp.int32, kt.shape, 1)
        kpe = jnp.where(lane >= 64, kt, 0.0)
        q_raw = _nn(_rms(cq, qan[...]), wq_r[...])
        kv_raw = _nn(_rms(ckv, kvan[...]), wkv_r[...])
        c, a, bb = cos[...], sa[...], sb[...]
        for h in range(HEADS):
            outs[0][h] = _mla_q(q_raw[:, LANES * h:LANES * (h + 1)], c, a, bb, qn[...]).astype(BF16)
            outs[1][h] = _mla_q(kv_raw[:, LANES * h:LANES * (h + 1)] + kpe, c, a, bb, kn[...]).astype(BF16)
            outs[2][h] = kv_raw[:, 1024 + 64 * h:1024 + 64 * (h + 1)].astype(BF16)

    consts = [p["mla_q_a_norm"], p["mla_kv_a_norm"], p["mla_q_norm"], p["mla_k_norm"], wq, wkv]
    return _stage(name, fn, s // ts,
                  [(seg0, _rows(ts, 512))] + [(t, _rows(ts, LANES)) for t in tabs] + [(a, _full(a.shape)) for a in consts],
                  [(_sds((HEADS, s, LANES), BF16), _heads(ts, LANES)), (_sds((HEADS, s, LANES), BF16), _heads(ts, LANES)),
                   (_sds((HEADS, s, 64), BF16), _heads(ts, 64))])


def _mla_prep_bwd(seg0, tabs, p, wq, wkv, dq, dk, dv, dff, name):
    s = seg0.shape[0]
    ts = _pick(s, (256, 128))

    def fn(ins, outs, accs, scr):
        blk, cos, sa, sb, qan, kvan, qn, kn, wq_r, wkv_r, dq_r, dk_r, dv_r, dff_r = ins
        dqan, dkvan, dqn, dkn, dwq, dwkv = accs
        dqraw_s, dkvraw_s = scr
        b = blk[...]
        cq, ckv, kt = b[:, :256], b[:, 256:384], b[:, 384:512]
        lane = lax.broadcasted_iota(jnp.int32, kt.shape, 1)
        kpe = jnp.where(lane >= 64, kt, 0.0)
        cqn, vjp_cq = jax.vjp(_rms, cq, qan[...])
        ckvn, vjp_ckv = jax.vjp(_rms, ckv, kvan[...])
        q_raw = _nn(cqn, wq_r[...])
        kv_raw = _nn(ckvn, wkv_r[...])
        c, a, bb = cos[...], sa[...], sb[...]

        def head_bwd(raw, gain, d):
            t = _rope(raw, c, a, bb)
            _, vjp = jax.vjp(functools.partial(_rms, n=MLA_QK), t, gain)
            dt, dgain = vjp(d)
            return _rope_t(dt, c, a, bb), dgain

        dkpe = jnp.zeros(kt.shape, F32)
        for h in range(HEADS):
            dqh, dg = head_bwd(q_raw[:, LANES * h:LANES * (h + 1)], qn[...], dq_r[h])
            dqn[...] += dg
            dqraw_s[:, LANES * h:LANES * (h + 1)] = dqh
            dkh, dg = head_bwd(kv_raw[:, LANES * h:LANES * (h + 1)] + kpe, kn[...], dk_r[h])
            dkn[...] += dg
            dkvraw_s[:, LANES * h:LANES * (h + 1)] = dkh
            dkpe = dkpe + dkh
            dkvraw_s[:, 1024 + 64 * h:1024 + 64 * (h + 1)] = dv_r[h]
        dq_raw = dqraw_s[...]
        dkv_raw = dkvraw_s[...]
        dwq[...] += _tn(cqn, dq_raw)
        dwkv[...] += _tn(ckvn, dkv_raw)
        dcq, dg = vjp_cq(_nt(dq_raw, wq_r[...]))
        dqan[...] += dg
        dckv, dg = vjp_ckv(_nt(dkv_raw, wkv_r[...]))
        dkvan[...] += dg
        outs[0][:, 0:256] = dcq.astype(BF16)
        outs[0][:, 256:384] = dckv.astype(BF16)
        outs[0][:, 384:512] = (jnp.where(lane >= 64, dkpe, 0.0) + dff_r[...]).astype(BF16)

    consts = [p["mla_q_a_norm"], p["mla_kv_a_norm"], p["mla_q_norm"], p["mla_k_norm"], wq, wkv]
    return _stage(name, fn, s // ts,
                  [(seg0, _rows(ts, 512))] + [(t, _rows(ts, LANES)) for t in tabs] + [(a, _full(a.shape)) for a in consts]
                  + [(dq, _heads(ts, LANES)), (dk, _heads(ts, LANES)), (dv, _heads(ts, 64)), (dff, _rows(ts, LANES))],
                  [(_sds((s, 512), BF16), _rows(ts, 512))],
                  accs=[_sds((1, 256)), _sds((1, 128)), _sds((1, LANES)), _sds((1, LANES)), _sds(wq.shape), _sds(wkv.shape)],
                  scratch=[pltpu.VMEM((ts, 1024), F32), pltpu.VMEM((ts, 1536), F32)])


def _fox_prep_fwd(seg0, seg1, bf, qn, kn, name):
    s = seg0.shape[0]
    ts = _pick(s, (256, 128))
    steps = int(math.log2(ts))

    def fn(ins, outs, accs, scr):
        kt_r, x_r, bf_r, qn_r, kn_r = ins
        carry = scr[0]

        @pl.when(pl.program_id(0) == 0)
        def _():
            carry[...] = jnp.zeros(carry.shape, F32)

        x = x_r[...]
        for h in range(HEADS):
            outs[0][h] = _rms(x[:, 64 * h:64 * (h + 1)], qn_r[...]).astype(BF16)
            outs[1][h] = _rms(x[:, 512 + 64 * h:512 + 64 * (h + 1)], kn_r[...]).astype(BF16)
            outs[2][h] = x[:, 1024 + 64 * h:1024 + 64 * (h + 1)].astype(BF16)
        kt = kt_r[...]
        lane = lax.broadcasted_iota(jnp.int32, kt.shape, 1)
        row = lax.broadcasted_iota(jnp.int32, kt.shape, 0)
        cs = jnp.where(lane < HEADS, jax.nn.log_sigmoid(kt + bf_r[...]), 0.0)
        for k in range(steps):
            sh = 1 << k
            cs = cs + jnp.where(row >= sh, pltpu.roll(cs, sh, 0), 0.0)
        cs = cs + carry[0:1, :]
        outs[3][...] = cs
        outs[4][...] = cs.T[0:HEADS, :]
        carry[0:1, :] = cs[ts - 1:ts, :]

    return _stage(name, fn, s // ts,
                  [(seg0, _rows(ts, LANES, 3)), (seg1, _rows(ts, 1536)), (bf, _full(bf.shape)), (qn, _full(qn.shape)),
                   (kn, _full(kn.shape))],
                  [(_sds((HEADS, s, 64), BF16), _heads(ts, 64)), (_sds((HEADS, s, 64), BF16), _heads(ts, 64)),
                   (_sds((HEADS, s, 64), BF16), _heads(ts, 64)), (_sds((s, LANES)), _rows(ts, LANES)),
                   (_sds((HEADS, s)), pl.BlockSpec((HEADS, ts), lambda i: (0, i)))],
                  scratch=[pltpu.VMEM((SUBLANES, LANES), F32)])


def _fox_prep_bwd(seg0, seg1, bf, qn, kn, dq, dk, dv, dck, name):
    s = seg0.shape[0]
    ts = _pick(s, (256, 128))
    n = s // ts
    steps = int(math.log2(ts))

    def fn(ins, outs, accs, scr):
        kt_r, x_r, bf_r, qn_r, kn_r, dq_r, dk_r, dv_r, dck_r = ins
        dqn, dkn, dbf = accs
        carry, dbuf = scr

        @pl.when(pl.program_id(0) == 0)
        def _():
            carry[...] = jnp.zeros(carry.shape, F32)

        x = x_r[...]
        for h in range(HEADS):
            _, vjp = jax.vjp(_rms, x[:, 64 * h:64 * (h + 1)], qn_r[...])
            d, dg = vjp(dq_r[h])
            dbuf[:, 64 * h:64 * (h + 1)] = d
            dqn[...] += dg
            _, vjp = jax.vjp(_rms, x[:, 512 + 64 * h:512 + 64 * (h + 1)], kn_r[...])
            d, dg = vjp(dk_r[h])
            dbuf[:, 512 + 64 * h:512 + 64 * (h + 1)] = d
            dkn[...] += dg
            dbuf[:, 1024 + 64 * h:1024 + 64 * (h + 1)] = dv_r[h]
        outs[0][...] = dbuf[...].astype(BF16)
        dc = dck_r[...].reshape(HEADS, ts)
        dc = jnp.concatenate([dc, jnp.zeros((LANES - HEADS, ts), F32)], axis=0).T
        row = lax.broadcasted_iota(jnp.int32, dc.shape, 0)
        lane = lax.broadcasted_iota(jnp.int32, dc.shape, 1)
        for k in range(steps):
            sh = 1 << k
            dc = dc + jnp.where(row < ts - sh, pltpu.roll(dc, ts - sh, 0), 0.0)
        dc = dc + carry[0:1, :]
        carry[0:1, :] = dc[0:1, :]
        dff = jnp.where(lane < HEADS, dc * jax.nn.sigmoid(-(kt_r[...] + bf_r[...])), 0.0)
        outs[1][...] = dff
        dbf[...] += jnp.sum(dff, axis=0, keepdims=True)

    return _stage(name, fn, n,
                  [(seg0, _rows_rev(ts, LANES, n, 3)), (seg1, _rows_rev(ts, 1536, n)), (bf, _full(bf.shape)),
                   (qn, _full(qn.shape)), (kn, _full(kn.shape)), (dq, _heads_rev(ts, 64, n)), (dk, _heads_rev(ts, 64, n)),
                   (dv, _heads_rev(ts, 64, n)), (dck, pl.BlockSpec((HEADS, 1, ts), lambda i: (0, 0, n - 1 - i)))],
                  [(_sds((s, 1536), BF16), _rows_rev(ts, 1536, n)), (_sds((s, LANES)), _rows_rev(ts, LANES, n))],
                  accs=[_sds((1, 64)), _sds((1, 64)), _sds((1, LANES))],
                  scratch=[pltpu.VMEM((SUBLANES, LANES), F32), pltpu.VMEM((ts, 1536), F32)])


def _allowed(i, j, t, chunk_causal):
    qpos = i * t + lax.broadcasted_iota(jnp.int32, (t, t), 0)
    kpos = j * t + lax.broadcasted_iota(jnp.int32, (t, t), 1)
    if chunk_causal:
        return (kpos // CHUNK) <= (qpos // CHUNK)
    return kpos <= qpos


def _pick_col(c_blk, h):
    lane = lax.broadcasted_iota(jnp.int32, c_blk.shape, 1)
    return jnp.sum(jnp.where(lane == h, c_blk, 0.0), axis=1, keepdims=True)


class _Ride:
    def __init__(self, make, take):
        self.make, self.take = make, take


class _Plan:
    def __init__(self, ins, out_shapes, n_remote, n_local, copies, aliases=None):
        self.ins, self.out_shapes, self.n_remote, self.n_local = list(ins), list(out_shapes), n_remote, n_local
        self.copies, self.aliases = copies, dict(aliases or {})

    def scratch(self):
        return [pltpu.SemaphoreType.DMA((self.n_remote,)), pltpu.SemaphoreType.DMA((self.n_remote,)),
                pltpu.SemaphoreType.DMA((max(self.n_local, 1),))]

    def start(self, in_refs, out_refs, sems):
        remote, local = self.copies(in_refs, out_refs, *sems)
        for cp in local + remote:
            cp.start()

    def wait(self, in_refs, out_refs, sems):
        remote, local = self.copies(in_refs, out_refs, *sems)
        for cp in remote:
            cp.wait()
        for cp in local:
            cp.wait()


class _Off:
    def __init__(self, ref, off):
        self.ref, self.off, self.at = ref, off, self

    def __getitem__(self, k):
        return self.ref.at[k + self.off]


def _join(plans):
    ins = [a for p in plans for a in p.ins]
    outs = [o for p in plans for o in p.out_shapes]
    aliases, i0, o0 = {}, 0, 0
    for p in plans:
        aliases.update({i0 + i: o0 + o for i, o in p.aliases.items()})
        i0, o0 = i0 + len(p.ins), o0 + len(p.out_shapes)

    def copies(in_refs, out_refs, send, recv, local):
        rem, loc, i0, o0, r0, l0 = [], [], 0, 0, 0, 0
        for p in plans:
            r, l = p.copies(in_refs[i0:i0 + len(p.ins)], out_refs[o0:o0 + len(p.out_shapes)], _Off(send, r0), _Off(recv, r0),
                            _Off(local, l0))
            rem, loc = rem + r, loc + l
            i0, o0, r0, l0 = i0 + len(p.ins), o0 + len(p.out_shapes), r0 + p.n_remote, l0 + p.n_local
        return rem, loc

    return _Plan(ins, outs, sum(p.n_remote for p in plans), sum(p.n_local for p in plans), copies, aliases)


def _call_with_ride(core, name, grid, ins, in_specs, out_shape, out_specs, semantics, vmem_mb, ride):
    n_in, n_out = len(ins), len(out_shape)
    if ride is None:
        return pl.pallas_call(
            core, name=name, grid=grid, in_specs=in_specs, out_specs=out_specs, out_shape=out_shape,
            compiler_params=pltpu.CompilerParams(dimension_semantics=semantics, vmem_limit_bytes=vmem_mb * 1024 * 1024),
        )(*ins)
    plan = ride.make()
    ci, co = len(plan.ins), len(plan.out_shapes)

    def body(*refs):
        c_in = refs[n_in:n_in + ci]
        a_out = refs[n_in + ci:n_in + ci + n_out]
        c_out = refs[n_in + ci + n_out:n_in + ci + n_out + co]
        sems = refs[n_in + ci + n_out + co:]
        ids = [pl.program_id(d) for d in range(len(grid))]
        first = functools.reduce(jnp.logical_and, [i == 0 for i in ids])
        last = functools.reduce(jnp.logical_and, [i == g - 1 for i, g in zip(ids, grid)])

        @pl.when(first)
        def _():
            plan.start(c_in, c_out, sems)

        core(*refs[:n_in], *a_out)

        @pl.when(last)
        def _():
            plan.wait(c_in, c_out, sems)

    res = pl.pallas_call(
        body, name=name, grid=grid, in_specs=list(in_specs) + [ANY] * ci, out_specs=list(out_specs) + [ANY] * co,
        out_shape=list(out_shape) + plan.out_shapes, scratch_shapes=plan.scratch(),
        input_output_aliases={n_in + i: n_out + o for i, o in plan.aliases.items()},
        compiler_params=pltpu.CompilerParams(dimension_semantics=("arbitrary",) * len(grid),
                                             vmem_limit_bytes=vmem_mb * 1024 * 1024),
    )(*ins, *plan.ins)
    ride.take(res[n_out:])
    return res[:n_out]


def _run_plan(plan, name):
    ci = len(plan.ins)

    def body(*refs):
        c_in, c_out, sems = refs[:ci], refs[ci:ci + len(plan.out_shapes)], refs[ci + len(plan.out_shapes):]
        plan.start(c_in, c_out, sems)
        plan.wait(c_in, c_out, sems)

    return pl.pallas_call(body, name=name, in_specs=[ANY] * ci, out_specs=[ANY] * len(plan.out_shapes),
                          out_shape=plan.out_shapes, scratch_shapes=plan.scratch(),
                          input_output_aliases=plan.aliases)(*plan.ins)


def _attn_fwd(q, k, v, scale, chunk_causal, name, c=None, ct=None, hps=8, ride=None):
    _, s, dk = q.shape
    dv = v.shape[2]
    t = _pick(s, (256, 128))
    bias = c is not None

    def body(*refs):
        if bias:
            q_ref, k_ref, v_ref, c_ref, ct_ref, o_ref, lse_ref = refs
        else:
            q_ref, k_ref, v_ref, o_ref, lse_ref = refs
        hp, i = pl.program_id(0), pl.program_id(1)
        qb = [q_ref[e] for e in range(hps)]
        cq = [_pick_col(c_ref[...], hp * hps + e) if bias else None for e in range(hps)]

        def step(j, carry, diagonal):
            off = pl.multiple_of(j * t, t)
            out = []
            for e in range(hps):
                m, l, acc = carry[e]
                sc = _nt(qb[e], k_ref[e, pl.ds(off, t), :]) * scale
                if bias:
                    sc = sc + (cq[e] - ct_ref[pl.ds(hp * hps + e, 1), pl.ds(off, t)])
                if diagonal:
                    sc = jnp.where(_allowed(i, j, t, chunk_causal), sc, NEG)
                m_new = jnp.maximum(m, jnp.max(sc, axis=1, keepdims=True))
                pr = jnp.exp(sc - m_new)
                alpha = jnp.exp(m - m_new)
                out.append((m_new, alpha * l + jnp.sum(pr, axis=1, keepdims=True),
                            alpha * acc + _nn(pr, v_ref[e, pl.ds(off, t), :])))
            return tuple(out)

        init = tuple((jnp.full((t, 1), NEG, F32), jnp.zeros((t, 1), F32), jnp.zeros((t, dv), F32)) for _ in range(hps))
        res = step(i, lax.fori_loop(0, i, functools.partial(step, diagonal=False), init), True)
        for e in range(hps):
            m, l, acc = res[e]
            o_ref[e] = acc / l
            lse_ref[e] = m + jnp.log(l)

    ins = [q, k, v] + ([c, ct] if bias else [])
    in_specs = [pl.BlockSpec((hps, t, dk), lambda h, i: (h, i, 0)), pl.BlockSpec((hps, s, dk), lambda h, i: (h, 0, 0)),
                pl.BlockSpec((hps, s, dv), lambda h, i: (h, 0, 0))]
    if bias:
        in_specs += [pl.BlockSpec((t, LANES), lambda h, i: (i, 0)), pl.BlockSpec((HEADS, s), lambda h, i: (0, 0))]
    return _call_with_ride(
        body, name, (HEADS // hps, s // t), ins, in_specs, [_sds((HEADS, s, dv)), _sds((HEADS, s, 1))],
        [pl.BlockSpec((hps, t, dv), lambda h, i: (h, i, 0)), pl.BlockSpec((hps, t, 1), lambda h, i: (h, i, 0))],
        ("parallel", "parallel"), 48, ride)


def _attn_delta(q, k, v, do, lse, scale, chunk_causal, name, c=None, ct=None, hps=4, ride=None):
    _, s, dk = q.shape
    dv = v.shape[2]
    t = _pick(s, (256, 128))
    bias = c is not None

    def body(*refs):
        if bias:
            q_ref, k_ref, v_ref, do_ref, lse_ref, c_ref, ct_ref, dl_ref = refs
        else:
            q_ref, k_ref, v_ref, do_ref, lse_ref, dl_ref = refs
        hp, i = pl.program_id(0), pl.program_id(1)
        qb, dob, lse_b = ([r[e] for e in range(hps)] for r in (q_ref, do_ref, lse_ref))
        cq = [_pick_col(c_ref[...], hp * hps + e) if bias else None for e in range(hps)]

        def step(j, acc, diagonal):
            off = pl.multiple_of(j * t, t)
            out = []
            for e in range(hps):
                sc = _nt(qb[e], k_ref[e, pl.ds(off, t), :]) * scale
                if bias:
                    sc = sc + (cq[e] - ct_ref[pl.ds(hp * hps + e, 1), pl.ds(off, t)])
                pr = jnp.exp(sc - lse_b[e])
                if diagonal:
                    pr = jnp.where(_allowed(i, j, t, chunk_causal), pr, 0.0)
                out.append(acc[e] + jnp.sum(pr * _nt(dob[e], v_ref[e, pl.ds(off, t), :]), axis=1, keepdims=True))
            return tuple(out)

        init = tuple(jnp.zeros((t, 1), F32) for _ in range(hps))
        res = step(i, lax.fori_loop(0, i, functools.partial(step, diagonal=False), init), True)
        for e in range(hps):
            dl_ref[e] = res[e]

    ins = [q, k, v, do, lse] + ([c, ct] if bias else [])
    in_specs = [pl.BlockSpec((hps, t, dk), lambda h, i: (h, i, 0)), pl.BlockSpec((hps, s, dk), lambda h, i: (h, 0, 0)),
                pl.BlockSpec((hps, s, dv), lambda h, i: (h, 0, 0)), pl.BlockSpec((hps, t, dv), lambda h, i: (h, i, 0)),
                pl.BlockSpec((hps, t, 1), lambda h, i: (h, i, 0))]
    if bias:
        in_specs += [pl.BlockSpec((t, LANES), lambda h, i: (i, 0)), pl.BlockSpec((HEADS, s), lambda h, i: (0, 0))]
    return _call_with_ride(body, name, (HEADS // hps, s // t), ins, in_specs, [_sds((HEADS, s, 1))],
                           [pl.BlockSpec((hps, t, 1), lambda h, i: (h, i, 0))], ("parallel", "parallel"), 48, ride)[0]


def _attn_bwd(q, k, v, do, lse, delta, scale, chunk_causal, name, c=None, ct=None, hps=8, ride=None):
    _, s, dk = q.shape
    dv = v.shape[2]
    t = _pick(s, (256, 128))
    n = s // t
    bias = c is not None

    def body(*refs):
        if bias:
            q_ref, k_ref, v_ref, do_ref, lse_ref, dl_ref, c_ref, ct_ref, dq_ref, dk_ref, dv_ref, dck_ref = refs
        else:
            q_ref, k_ref, v_ref, do_ref, lse_ref, dl_ref, dq_ref, dk_ref, dv_ref = refs
        hp, j = pl.program_id(0), pl.program_id(1)

        @pl.when(j == 0)
        def _():
            dq_ref[...] = jnp.zeros(dq_ref.shape, F32)

        kb, vb = [k_ref[e] for e in range(hps)], [v_ref[e] for e in range(hps)]
        joff = pl.multiple_of(j * t, t)
        ck = [ct_ref[pl.ds(hp * hps + e, 1), pl.ds(joff, t)] if bias else None for e in range(hps)]

        def step(i, carry, diagonal):
            off = pl.multiple_of(i * t, t)
            out = []
            for e in range(hps):
                dk_acc, dv_acc, dck_acc = carry[e]
                qb = q_ref[e, pl.ds(off, t), :]
                dob = do_ref[e, pl.ds(off, t), :]
                sc = _nt(qb, kb[e]) * scale
                if bias:
                    sc = sc + (_pick_col(c_ref[pl.ds(off, t), :], hp * hps + e) - ck[e])
                pr = jnp.exp(sc - lse_ref[e, pl.ds(off, t), :])
                if diagonal:
                    pr = jnp.where(_allowed(i, j, t, chunk_causal), pr, 0.0)
                dv_acc = dv_acc + _tn(pr, dob)
                ds = pr * (_nt(dob, vb[e]) - dl_ref[e, pl.ds(off, t), :])
                dq_ref[e, pl.ds(off, t), :] += _nn(ds, kb[e]) * scale
                dk_acc = dk_acc + _tn(ds, qb) * scale
                if bias:
                    dck_acc = dck_acc - jnp.sum(ds, axis=0, keepdims=True)
                out.append((dk_acc, dv_acc, dck_acc))
            return tuple(out)

        zero = tuple((jnp.zeros((t, dk), F32), jnp.zeros((t, dv), F32), jnp.zeros((1, t), F32)) for _ in range(hps))
        res = lax.fori_loop(j + 1, n, functools.partial(step, diagonal=False), step(j, zero, True))
        for e in range(hps):
            dk_ref[e], dv_ref[e] = res[e][0], res[e][1]
            if bias:
                dck_ref[e] = res[e][2]

    ins = [q, k, v, do, lse, delta] + ([c, ct] if bias else [])
    full = lambda d: pl.BlockSpec((hps, s, d), lambda h, j: (h, 0, 0))
    blk = lambda d: pl.BlockSpec((hps, t, d), lambda h, j: (h, j, 0))
    in_specs = [full(dk), blk(dk), blk(dv), full(dv), full(1), full(1)]
    if bias:
        in_specs += [pl.BlockSpec((s, LANES), lambda h, j: (0, 0)), pl.BlockSpec((HEADS, s), lambda h, j: (0, 0))]
    out_specs = [full(dk), blk(dk), blk(dv)]
    out_shape = [_sds((HEADS, s, dk)), _sds((HEADS, s, dk)), _sds((HEADS, s, dv))]
    if bias:
        out_specs.append(pl.BlockSpec((hps, 1, t), lambda h, j: (h, 0, j)))
        out_shape.append(_sds((HEADS, 1, s)))
    return _call_with_ride(body, name, (HEADS // hps, n), ins, in_specs, out_shape, out_specs, ("parallel", "arbitrary"), 56,
                           ride)


def _s5_disc(lr, li, ldt, br, bi):
    dt = jnp.exp(ldt)
    mag = jnp.exp(lr * dt)
    a_re = mag * jnp.cos(li * dt)
    a_im = mag * jnp.sin(li * dt)
    den = lr * lr + li * li
    f_re = ((a_re - 1.0) * lr + a_im * li) / den
    f_im = (a_im * lr - (a_re - 1.0) * li) / den
    return a_re, a_im, f_re * br - f_im * bi, f_re * bi + f_im * br


def _s5_params_fwd(lr16, li16, ldt16, br2, bi2, name):
    def body(a, b, c, d, e, o0, o1, o2, o3):
        r = _s5_disc(a[...], b[...], c[...], d[...], e[...])
        o0[...], o1[...], o2[...], o3[...] = r

    return pl.pallas_call(body, name=name, out_shape=[_sds((512, 64))] * 4)(lr16, li16, ldt16, br2, bi2)


def _s5_params_bwd(lr16, li16, ldt16, br2, bi2, da_re16, da_im16, dbb_re, dbb_im, name):
    def body(a, b, c, d, e, g0, g1, g2, g3, o_lr, o_li, o_dt, o_br, o_bi):
        _, vjp = jax.vjp(_s5_disc, a[...], b[...], c[...], d[...], e[...])
        dlr, dli, dldt, dbr, dbi = vjp((g0[...], g1[...], g2[...], g3[...]))
        grp = lambda z: z.reshape(S5_GROUPS, S5_GROUP, S5_STATE).sum(axis=1)
        o_lr[...] = grp(dlr)
        o_li[...] = grp(dli)
        o_dt[...] = jnp.sum(grp(dldt), axis=1, keepdims=True)
        o_br[...] = dbr
        o_bi[...] = dbi

    return pl.pallas_call(
        body, name=name, out_shape=[_sds((32, 64)), _sds((32, 64)), _sds((32, 1)), _sds((512, 64)), _sds((512, 64))],
    )(lr16, li16, ldt16, br2, bi2, da_re16, da_im16, dbb_re, dbb_im)


def _cmul(ar, ai, br, bi):
    return ar * br - ai * bi, ar * bi + ai * br


def _scan(bre, bim, a_re, a_im, reverse, name, x_re=None, x_im=None, ride=None):
    s, w = bre.shape
    lw = _pick(w, (256, 128))
    nt = s // SUBLANES
    with_da = x_re is not None

    def body(*refs):
        if with_da:
            bre_r, bim_r, are_r, aim_r, xre_r, xim_r, ore_r, oim_r, dare_r, daim_r = refs
        else:
            bre_r, bim_r, are_r, aim_r, ore_r, oim_r = refs
        ar, ai = are_r[...], aim_r[...]
        shp = (SUBLANES, lw)
        row = lax.broadcasted_iota(jnp.int32, shp, 0)
        pows = [(ar, ai)]
        for _ in range(SUBLANES - 1):
            pows.append(_cmul(pows[-1][0], pows[-1][1], ar, ai))
        cm_r, cm_i = jnp.zeros(shp, F32), jnp.zeros(shp, F32)
        for r in range(SUBLANES):
            e = (SUBLANES - 1 - r) if reverse else r
            cm_r = jnp.where(row == r, jnp.broadcast_to(pows[e][0], shp), cm_r)
            cm_i = jnp.where(row == r, jnp.broadcast_to(pows[e][1], shp), cm_i)
        steps = [(1, pows[0]), (2, pows[1]), (4, pows[3])]

        def tile(it, carry):
            if with_da:
                c_r, c_i, acc_r, acc_i = carry
            else:
                c_r, c_i = carry
            r = (nt - 1 - it) if reverse else it
            off = pl.multiple_of(r * SUBLANES, SUBLANES)
            xr, xi = bre_r[pl.ds(off, SUBLANES), :], bim_r[pl.ds(off, SUBLANES), :]
            for sh, (pr, pi) in steps:
                if reverse:
                    keep = row < SUBLANES - sh
                    sr = jnp.where(keep, pltpu.roll(xr, SUBLANES - sh, 0), 0.0)
                    si = jnp.where(keep, pltpu.roll(xi, SUBLANES - sh, 0), 0.0)
                else:
                    keep = row >= sh
                    sr = jnp.where(keep, pltpu.roll(xr, sh, 0), 0.0)
                    si = jnp.where(keep, pltpu.roll(xi, sh, 0), 0.0)
                mr, mi = _cmul(pr, pi, sr, si)
                xr, xi = xr + mr, xi + mi
            mr, mi = _cmul(cm_r, cm_i, c_r, c_i)
            xr, xi = xr + mr, xi + mi
            ore_r[pl.ds(off, SUBLANES), :] = xr
            oim_r[pl.ds(off, SUBLANES), :] = xi
            edge = 0 if reverse else SUBLANES - 1
            c_r, c_i = xr[edge:edge + 1, :], xi[edge:edge + 1, :]
            if not with_da:
                return c_r, c_i
            fr, fi = xre_r[pl.ds(off, SUBLANES), :], xim_r[pl.ds(off, SUBLANES), :]
            poff = pl.multiple_of(jnp.maximum(r - 1, 0) * SUBLANES, SUBLANES)
            live = (r > 0).astype(F32)
            pr_last = xre_r[pl.ds(poff, SUBLANES), :][SUBLANES - 1:SUBLANES, :] * live
            pi_last = xim_r[pl.ds(poff, SUBLANES), :][SUBLANES - 1:SUBLANES, :] * live
            sr = jnp.where(row >= 1, pltpu.roll(fr, 1, 0), jnp.broadcast_to(pr_last, shp))
            si = jnp.where(row >= 1, pltpu.roll(fi, 1, 0), jnp.broadcast_to(pi_last, shp))
            acc_r = acc_r + xr * sr + xi * si
            acc_i = acc_i + xi * sr - xr * si
            return c_r, c_i, acc_r, acc_i

        z1 = jnp.zeros((1, lw), F32)
        if with_da:
            _, _, acc_r, acc_i = lax.fori_loop(0, nt, tile, (z1, z1, jnp.zeros(shp, F32), jnp.zeros(shp, F32)))
            dare_r[...] = jnp.sum(acc_r, axis=0, keepdims=True)
            daim_r[...] = jnp.sum(acc_i, axis=0, keepdims=True)
        else:
            lax.fori_loop(0, nt, tile, (z1, z1))

    col = pl.BlockSpec((s, lw), lambda j: (0, j))
    one = pl.BlockSpec((1, lw), lambda j: (0, j))
    ins = [bre, bim, a_re, a_im] + ([x_re, x_im] if with_da else [])
    in_specs = [col, col, one, one] + ([col, col] if with_da else [])
    out_specs = [col, col] + ([one, one] if with_da else [])
    out_shape = [_sds((s, w)), _sds((s, w))] + ([_sds((1, w)), _sds((1, w))] if with_da else [])
    return _call_with_ride(body, name, (w // lw,), ins, in_specs, out_shape, out_specs, ("parallel",), 56, ride)


S5_SUPER = 4


def _s5_mm(a, w4, mode, name, acc=None):
    s = a.shape[0]
    _, r, c = w4.shape
    wa, wo = (r, c) if mode == "nn" else (c, r)
    tm = _pick(s, (512, 256, 128))
    has_acc = acc is not None

    def body(*refs):
        if has_acc:
            a_ref, w_ref, c_ref, o_ref = refs
        else:
            a_ref, w_ref, o_ref = refs
        res = _nn(a_ref[...], w_ref[0]) if mode == "nn" else _nt(a_ref[...], w_ref[0])
        o_ref[...] = res + c_ref[...] if has_acc else res

    o_spec = pl.BlockSpec((tm, wo), lambda k, i: (i, k))
    ins = [a, w4] + ([acc] if has_acc else [])
    in_specs = [pl.BlockSpec((tm, wa), lambda k, i: (i, k)), pl.BlockSpec((1, r, c), lambda k, i: (k, 0, 0))]
    return pl.pallas_call(
        body, name=name, grid=(S5_SUPER, s // tm), in_specs=in_specs + ([o_spec] if has_acc else []), out_specs=o_spec,
        out_shape=_sds((s, S5_SUPER * wo)),
        compiler_params=pltpu.CompilerParams(dimension_semantics=("parallel", "parallel"), vmem_limit_bytes=48 * 1024 * 1024),
    )(*ins)


def _s5_group_tn(a, b, name):
    s = a.shape[0]
    tk = _pick(s, (512, 256, 128))
    nk = s // tk

    def body(a_ref, b_ref, o_ref, acc_ref):
        k = pl.program_id(1)

        @pl.when(k == 0)
        def _():
            acc_ref[...] = jnp.zeros(acc_ref.shape, F32)

        acc_ref[...] += _tn(a_ref[...], b_ref[...])

        @pl.when(k == nk - 1)
        def _():
            p = acc_ref[...]
            grp = lax.broadcasted_iota(jnp.int32, (LANES, S5_STATE), 0) // S5_GROUP
            out = jnp.zeros((LANES, S5_STATE), F32)
            for j in range(LANES // S5_GROUP):
                out = jnp.where(grp == j, p[:, S5_STATE * j:S5_STATE * (j + 1)], out)
            o_ref[...] = out

    return pl.pallas_call(
        body, name=name, grid=(S5_SUPER, nk),
        in_specs=[pl.BlockSpec((tk, LANES), lambda g, k: (k, g)), pl.BlockSpec((tk, 512), lambda g, k: (k, g))],
        out_specs=pl.BlockSpec((LANES, S5_STATE), lambda g, k: (g, 0)), out_shape=_sds((512, S5_STATE)),
        scratch_shapes=[pltpu.VMEM((LANES, 512), F32)],
        compiler_params=pltpu.CompilerParams(dimension_semantics=("parallel", "arbitrary")),
    )(a, b)


def _s5_seg1(y0, u, d):
    return jax.nn.gelu(y0 + d * u)


def _s5_seg2(z, t, b):
    return z * jax.nn.sigmoid(t + b)


def _s5_post_fwd(y0, seg2, d, wglu, bglu, name):
    s = y0.shape[0]
    ts = _pick(s, (256, 128))

    def fn(ins, outs, accs, scr):
        z = _s5_seg1(ins[0][...], ins[1][...], ins[2][...])
        outs[0][...] = _s5_seg2(z, _nn(z, ins[3][...]), ins[4][...])

    return _stage(name, fn, s // ts,
                  [(y0, _rows(ts, 512)), (seg2, _rows(ts, 512)), (d, _full(d.shape)), (wglu, _full(wglu.shape)),
                   (bglu, _full(bglu.shape))], [(_sds((s, 512)), _rows(ts, 512))])[0]


def _s5_post_bwd(y0, seg2, d, wglu, bglu, dy, name):
    s = y0.shape[0]
    ts = _pick(s, (256, 128))

    def fn(ins, outs, accs, scr):
        y0_r, u_r, d_r, w_r, b_r, dy_r = ins
        z, vjp1 = jax.vjp(_s5_seg1, y0_r[...], u_r[...], d_r[...])
        t = _nn(z, w_r[...])
        _, vjp2 = jax.vjp(_s5_seg2, z, t, b_r[...])
        dz, dt, db = vjp2(dy_r[...])
        accs[0][...] += _tn(z, dt)
        accs[1][...] += db
        dy0, du, dd = vjp1(dz + _nt(dt, w_r[...]))
        accs[2][...] += dd
        outs[0][...] = dy0
        outs[1][...] = du

    return _stage(name, fn, s // ts,
                  [(y0, _rows(ts, 512)), (seg2, _rows(ts, 512)), (d, _full(d.shape)), (wglu, _full(wglu.shape)),
                   (bglu, _full(bglu.shape)), (dy, _rows(ts, 512))],
                  [(_sds((s, 512)), _rows(ts, 512)), (_sds((s, 512)), _rows(ts, 512))],
                  accs=[_sds((512, 512)), _sds((1, 512)), _sds((1, 512))])


def _gate_a(y, g):
    return y * jax.nn.silu(g)


def _gate_m(o0, o1, o2, m0, m1, m2):
    return jax.nn.sigmoid(m0) * o0 + jax.nn.sigmoid(m1) * o1 + jax.nn.sigmoid(m2) * o2


def _assemble(ybuf, o_mla, o_fox, y_s5):
    for h in range(HEADS):
        ybuf[:, 64 * h:64 * (h + 1)] = o_mla[h]
        ybuf[:, 512 + 64 * h:512 + 64 * (h + 1)] = o_fox[h]
    ybuf[:, 1024:1536] = y_s5[...]


def _gate_fwd(o_mla, o_fox, y_s5, seg3, seg4, x, wo, wout, name):
    s = x.shape[0]
    ts = _pick(s, (256, 128))

    def fn(ins, outs, accs, scr):
        om, of, ys, g_r, m_r, x_r, wo_r, wout_r = ins
        ybuf = scr[0]
        _assemble(ybuf, om, of, ys)
        a = _gate_a(ybuf[...], g_r[...])
        o = [_nn(a[:, 512 * b:512 * (b + 1)], wo_r[512 * b:512 * (b + 1), :]) for b in range(3)]
        merged = _gate_m(o[0], o[1], o[2], m_r[:, 0:1024], m_r[:, 1024:2048], m_r[:, 2048:3072])
        outs[0][...] = x_r[...] + _nn(merged, wout_r[...])

    return _stage(name, fn, s // ts,
                  [(o_mla, _heads(ts, 64)), (o_fox, _heads(ts, 64)), (y_s5, _rows(ts, 512)), (seg3, _rows(ts, 1536)),
                   (seg4, _rows(ts, 3072)), (x, _rows(ts, D_MODEL)), (wo, _full(wo.shape)), (wout, _full(wout.shape))],
                  [(_sds((s, D_MODEL)), _rows(ts, D_MODEL))], scratch=[pltpu.VMEM((ts, 1536), F32)])[0]


def _gate_bwd(o_mla, o_fox, y_s5, seg3, seg4, wo, wout, dout, name):
    s = dout.shape[0]
    ts = _pick(s, (128,))

    def fn(ins, outs, accs, scr):
        om, of, ys, g_r, m_r, wo_r, wout_r, dout_r = ins
        do_mla, do_fox, dys, dg_r, dm_r = outs
        dwo, dwout = accs
        ybuf, dabuf = scr
        _assemble(ybuf, om, of, ys)
        a, vjp_a = jax.vjp(_gate_a, ybuf[...], g_r[...])
        o = [_nn(a[:, 512 * b:512 * (b + 1)], wo_r[512 * b:512 * (b + 1), :]) for b in range(3)]
        ms = [m_r[:, 1024 * b:1024 * (b + 1)] for b in range(3)]
        merged, vjp_m = jax.vjp(_gate_m, *o, *ms)
        dout_v = dout_r[...]
        dwout[...] += _tn(merged, dout_v)
        cts = vjp_m(_nt(dout_v, wout_r[...]))
        for b in range(3):
            dm_r[:, 1024 * b:1024 * (b + 1)] = cts[3 + b].astype(BF16)
            dwo[512 * b:512 * (b + 1), :] += _tn(a[:, 512 * b:512 * (b + 1)], cts[b])
            dabuf[:, 512 * b:512 * (b + 1)] = _nt(cts[b], wo_r[512 * b:512 * (b + 1), :])
        dy, dg = vjp_a(dabuf[...])
        dg_r[...] = dg.astype(BF16)
        dys[...] = dy[:, 1024:1536]
        for h in range(HEADS):
            do_mla[h] = dy[:, 64 * h:64 * (h + 1)]
            do_fox[h] = dy[:, 512 + 64 * h:512 + 64 * (h + 1)]

    return _stage(name, fn, s // ts,
                  [(o_mla, _heads(ts, 64)), (o_fox, _heads(ts, 64)), (y_s5, _rows(ts, 512)), (seg3, _rows(ts, 1536)),
                   (seg4, _rows(ts, 3072)), (wo, _full(wo.shape)), (wout, _full(wout.shape)), (dout, _rows(ts, D_MODEL))],
                  [(_sds((HEADS, s, 64)), _heads(ts, 64)), (_sds((HEADS, s, 64)), _heads(ts, 64)), (_sds((s, 512)), _rows(ts, 512)),
                   (_sds((s, 1536), BF16), _rows(ts, 1536)), (_sds((s, 3072), BF16), _rows(ts, 3072))],
                  accs=[_sds(wo.shape), _sds(wout.shape)], scratch=[pltpu.VMEM((ts, 1536), F32), pltpu.VMEM((ts, 1536), F32)],
                  vmem_mb=56)


def _loss_head(y, target, name):
    s = y.shape[0]
    ts = _pick(s, (256, 128))

    def fn(ins, outs, accs, scr):
        e = ins[0][...] - ins[1][...]
        outs[0][...] = e / D_MODEL
        accs[0][...] += 0.5 * jnp.sum(jnp.sum(e * e, axis=1, keepdims=True) / D_MODEL, axis=0, keepdims=True)

    r = _stage(name, fn, s // ts, [(y, _rows(ts, D_MODEL)), (target, _rows(ts, D_MODEL))],
               [(_sds((s, D_MODEL)), _rows(ts, D_MODEL))], accs=[_sds((1, 1))])
    return r[1], r[0]


IN_RANGES = ((0, 384, 0), (384, 416, 448), (416, 1952, 512), (1952, 1960, 384), (1960, IN_WIDTH, 2048))
SHARD_W = IN_WIDTH // N_DEV


def _win_pieces(d):
    lo, hi = SHARD_W * d, SHARD_W * (d + 1)
    out = []
    for a, b, p in IN_RANGES:
        s, e = max(a, lo), min(b, hi)
        while s < e:
            pad = p + (s - a)
            k = max(i for i in range(5) if SEG_OFF[i] <= pad)
            w = min(e - s, SEG_OFF[k] + SEG_W[k] - pad)
            out.append((s - lo, w, k, pad - SEG_OFF[k]))
            s += w
    return out


def _win_pad(g, name):
    _, r, _ = g.shape
    tr = 64

    def body(g_ref, o_ref):
        o_ref[...] = jnp.zeros(o_ref.shape, o_ref.dtype)
        for d in range(N_DEV):
            for dst, w, k, src in _win_pieces(d):
                o_ref[:, SEG_OFF[k] + src:SEG_OFF[k] + src + w] = g_ref[d, :, dst:dst + w]

    return pl.pallas_call(
        body, name=name, grid=(r // tr,), in_specs=[pl.BlockSpec((N_DEV, tr, SHARD_W), lambda i: (0, i, 0))],
        out_specs=pl.BlockSpec((tr, PAD_IN), lambda i: (i, 0)), out_shape=jax.ShapeDtypeStruct((r, PAD_IN), g.dtype),
        compiler_params=pltpu.CompilerParams(dimension_semantics=("parallel",)),
    )(g)


def _win_unpad(dsegs, name):
    r = dsegs[0].shape[0]
    tr = 64

    def body(*refs):
        o_ref = refs[5]
        for d in range(N_DEV):
            for dst, w, k, src in _win_pieces(d):
                o_ref[d, :, dst:dst + w] = refs[k][:, src:src + w]

    return pl.pallas_call(
        body, name=name, grid=(r // tr,), in_specs=[pl.BlockSpec((tr, SEG_W[k]), lambda i: (i, 0)) for k in range(5)],
        out_specs=pl.BlockSpec((N_DEV, tr, SHARD_W), lambda i: (0, i, 0)),
        out_shape=jax.ShapeDtypeStruct((N_DEV, r, SHARD_W), dsegs[0].dtype),
        compiler_params=pltpu.CompilerParams(dimension_semantics=("parallel",)),
    )(*dsegs)


def _pad_wq(w):
    w = w.reshape(MLA_Q_RANK, HEADS, MLA_QK)
    return jnp.pad(w, ((0, 0), (0, 0), (0, LANES - MLA_QK))).reshape(MLA_Q_RANK, HEADS * LANES)


def _unpad_wq(d):
    return d.reshape(MLA_Q_RANK, HEADS, LANES)[:, :, :MLA_QK].reshape(MLA_Q_RANK, HEADS * MLA_QK)


def _pad_wkv(w):
    w = w.reshape(MLA_KV_RANK, HEADS, MLA_NOPE + MLA_V)
    k = jnp.pad(w[:, :, :MLA_NOPE], ((0, 0), (0, 0), (0, LANES - MLA_NOPE))).reshape(MLA_KV_RANK, HEADS * LANES)
    return jnp.concatenate([k, w[:, :, MLA_NOPE:].reshape(MLA_KV_RANK, HEADS * MLA_V)], axis=1)


def _unpad_wkv(d):
    k = d[:, :HEADS * LANES].reshape(MLA_KV_RANK, HEADS, LANES)[:, :, :MLA_NOPE]
    v = d[:, HEADS * LANES:].reshape(MLA_KV_RANK, HEADS, MLA_V)
    return jnp.concatenate([k, v], axis=2).reshape(MLA_KV_RANK, HEADS * (MLA_NOPE + MLA_V))


def _pad_lanes(v, n=LANES):
    return jnp.pad(v, (0, n - v.shape[0])).reshape(1, n)


def _super_blocks(b):
    _, r, c = b.shape
    per = S5_GROUPS // S5_SUPER
    b = b.reshape(S5_SUPER, per, r, c)
    eye = jnp.eye(per, dtype=b.dtype)
    return (b[:, :, :, None, :] * eye[None, :, None, :, None]).reshape(S5_SUPER, per * r, per * c)


def _layer_params(l, w, small):
    p = {k: small[k][l] for k in small}
    q = {}
    q["norm_g"] = p["norm_g"].reshape(1, D_MODEL)
    q["mla_q_a_norm"] = p["mla_q_a_norm"].reshape(1, 256)
    q["mla_kv_a_norm"] = p["mla_kv_a_norm"].reshape(1, 128)
    q["mla_q_norm"] = _pad_lanes(p["mla_q_norm"])
    q["mla_k_norm"] = _pad_lanes(p["mla_k_norm"])
    q["fox_b_f"] = _pad_lanes(p["fox_b_f"])
    q["fox_q_norm"] = p["fox_q_norm"].reshape(1, 64)
    q["fox_k_norm"] = p["fox_k_norm"].reshape(1, 64)
    q["s5_d"] = p["s5_d"].reshape(1, 512)
    q["s5_b_glu"] = p["s5_b_glu"].reshape(1, 512)
    rep = lambda z: jnp.repeat(z, S5_GROUP, axis=0)
    q["lr16"], q["li16"] = rep(p["s5_lambda_re"]), rep(p["s5_lambda_im"])
    q["ldt16"] = rep(jnp.broadcast_to(p["s5_log_dt"][:, None], (S5_GROUPS, S5_STATE)))
    q["br2"] = p["s5_b_re"].transpose(0, 2, 1).reshape(512, 64)
    q["bi2"] = p["s5_b_im"].transpose(0, 2, 1).reshape(512, 64)
    q["c_re"], q["c_im"] = p["s5_c_re"], p["s5_c_im"]
    q["w_in"] = _win_pad(w["w_in"], f"l{l}_w_in_pad")
    q["wq"] = _pad_wq(w["mla_w_q_up"])
    q["wkv"] = _pad_wkv(w["mla_w_kv_up"])
    q["wglu"] = w["s5_w_glu"]
    q["wo"] = w["w_branch_out"]
    q["wout"] = w["w_out"]
    return q


def _layer_fwd(l, x, tabs, q, rides=None):
    rides = rides or {}
    n = lambda s: f"l{l}_{s}"
    sv = {"x": x}
    h = _norm_fwd(x, q["norm_g"], n("norm_fwd"))
    sv["h"] = h
    seg = [_mm(h, q["w_in"], "nn", n(f"proj{k}"), b_cols=(SEG_OFF[k], SEG_W[k])) for k in range(5)]
    sv["seg"] = seg
    mq, mk, mv = _mla_prep_fwd(seg[0], tabs, q, q["wq"], q["wkv"], n("mla_prep_fwd"))
    o_mla, lse_mla = _attn_fwd(mq, mk, mv, 1.0 / math.sqrt(MLA_QK), True, n("mla_attn_fwd"), ride=rides.get("mla_attn_fwd"))
    sv.update(mq=mq, mk=mk, mv=mv, o_mla=o_mla, lse_mla=lse_mla)
    fq, fk, fv, c, ct = _fox_prep_fwd(seg[0], seg[1], q["fox_b_f"], q["fox_q_norm"], q["fox_k_norm"], n("fox_prep_fwd"))
    o_fox, lse_fox = _attn_fwd(fq, fk, fv, 1.0 / math.sqrt(FOX_DIM), False, n("fox_attn_fwd"), c=c, ct=ct,
                               ride=rides.get("fox_attn_fwd"))
    sv.update(fq=fq, fk=fk, fv=fv, c=c, ct=ct, o_fox=o_fox, lse_fox=lse_fox)
    a_re16, a_im16, bb_re, bb_im = _s5_params_fwd(q["lr16"], q["li16"], q["ldt16"], q["br2"], q["bi2"], n("s5_params_fwd"))
    a_re = a_re16.reshape(S5_GROUPS, S5_GROUP, S5_STATE)[:, 0, :].reshape(1, S5_LANES)
    a_im = a_im16.reshape(S5_GROUPS, S5_GROUP, S5_STATE)[:, 0, :].reshape(1, S5_LANES)
    wb_re = _super_blocks(bb_re.reshape(S5_GROUPS, S5_GROUP, S5_STATE)).astype(BF16)
    wb_im = _super_blocks(bb_im.reshape(S5_GROUPS, S5_GROUP, S5_STATE)).astype(BF16)
    wc_re = _super_blocks(q["c_re"].transpose(0, 2, 1)).astype(BF16)
    wc_im = _super_blocks(-q["c_im"].transpose(0, 2, 1)).astype(BF16)
    bu_re = _s5_mm(seg[2], wb_re, "nn", n("s5_bu_re"))
    bu_im = _s5_mm(seg[2], wb_im, "nn", n("s5_bu_im"))
    x_re, x_im = _scan(bu_re, bu_im, a_re, a_im, False, n("s5_scan_fwd"), ride=rides.get("s5_scan_fwd"))
    y0 = _s5_mm(x_im, wc_im, "nn", n("s5_y_im"), acc=_s5_mm(x_re, wc_re, "nn", n("s5_y_re")))
    y_s5 = _s5_post_fwd(y0, seg[2], q["s5_d"], q["wglu"], q["s5_b_glu"], n("s5_post_fwd"))
    sv.update(a_re=a_re, a_im=a_im, wb_re=wb_re, wb_im=wb_im, wc_re=wc_re, wc_im=wc_im, x_re=x_re, x_im=x_im, y0=y0, y_s5=y_s5)
    out = _gate_fwd(o_mla, o_fox, y_s5, seg[3], seg[4], x, q["wo"], q["wout"], n("gate_fwd"))
    return out, sv


def _layer_bwd(l, dout, tabs, q, sv, rides=None, g=None):
    rides = rides or {}
    n = lambda s: f"l{l}_{s}"
    seg = sv["seg"]
    g = {} if g is None else g
    (do_mla, do_fox, dy_s5, dseg3, dseg4, g["wo"], g["wout"]) = _gate_bwd(
        sv["o_mla"], sv["o_fox"], sv["y_s5"], seg[3], seg[4], q["wo"], q["wout"], dout, n("gate_bwd"))
    dy0, du_a, g["wglu"], g["s5_b_glu"], g["s5_d"] = _s5_post_bwd(sv["y0"], seg[2], q["s5_d"], q["wglu"], q["s5_b_glu"], dy_s5,
                                                                 n("s5_post_bwd"))
    dx_re = _s5_mm(dy0, sv["wc_re"], "nt", n("s5_dx_re"))
    dx_im = _s5_mm(dy0, sv["wc_im"], "nt", n("s5_dx_im"))
    g["s5_c_re"] = _s5_group_tn(dy0, sv["x_re"], n("s5_dc_re")).reshape(S5_GROUPS, S5_GROUP, S5_STATE)
    g["s5_c_im"] = -_s5_group_tn(dy0, sv["x_im"], n("s5_dc_im")).reshape(S5_GROUPS, S5_GROUP, S5_STATE)
    g_re, g_im, da_re, da_im = _scan(dx_re, dx_im, sv["a_re"], -sv["a_im"], True, n("s5_scan_bwd"), x_re=sv["x_re"], x_im=sv["x_im"])
    dseg2 = _s5_mm(g_im, sv["wb_im"], "nt", n("s5_du_im"), acc=_s5_mm(g_re, sv["wb_re"], "nt", n("s5_du_re"), acc=du_a))
    dbb_re = _s5_group_tn(seg[2], g_re, n("s5_dbb_re"))
    dbb_im = _s5_group_tn(seg[2], g_im, n("s5_dbb_im"))
    first = (jnp.arange(512) % S5_GROUP == 0).astype(F32)[:, None]
    da_re16 = jnp.repeat(da_re.reshape(S5_GROUPS, S5_STATE), S5_GROUP, axis=0) * first
    da_im16 = jnp.repeat(da_im.reshape(S5_GROUPS, S5_STATE), S5_GROUP, axis=0) * first
    dlr, dli, dldt, dbr2, dbi2 = _s5_params_bwd(q["lr16"], q["li16"], q["ldt16"], q["br2"], q["bi2"], da_re16, da_im16, dbb_re,
                                               dbb_im, n("s5_params_bwd"))
    g["s5_lambda_re"], g["s5_lambda_im"], g["s5_log_dt"] = dlr, dli, dldt.reshape(S5_GROUPS)
    g["s5_b_re"] = dbr2.reshape(S5_GROUPS, S5_GROUP, S5_STATE).transpose(0, 2, 1)
    g["s5_b_im"] = dbi2.reshape(S5_GROUPS, S5_GROUP, S5_STATE).transpose(0, 2, 1)
    dl_fox = _attn_delta(sv["fq"], sv["fk"], sv["fv"], do_fox, sv["lse_fox"], 1.0 / math.sqrt(FOX_DIM), False,
                         n("fox_attn_delta"), c=sv["c"], ct=sv["ct"], ride=rides.get("fox_attn_delta"))
    dfq, dfk, dfv, dck = _attn_bwd(sv["fq"], sv["fk"], sv["fv"], do_fox, sv["lse_fox"], dl_fox, 1.0 / math.sqrt(FOX_DIM), False,
                                   n("fox_attn_bwd"), c=sv["c"], ct=sv["ct"], ride=rides.get("fox_attn_bwd"))
    dseg1, dff, g["fox_q_norm"], g["fox_k_norm"], dbf = _fox_prep_bwd(seg[0], seg[1], q["fox_b_f"], q["fox_q_norm"], q["fox_k_norm"],
                                                                      dfq, dfk, dfv, dck, n("fox_prep_bwd"))
    g["fox_b_f"] = dbf[0, :HEADS]
    dl_mla = _attn_delta(sv["mq"], sv["mk"], sv["mv"], do_mla, sv["lse_mla"], 1.0 / math.sqrt(MLA_QK), True, n("mla_attn_delta"),
                         ride=rides.get("mla_attn_delta"))
    dmq, dmk, dmv = _attn_bwd(sv["mq"], sv["mk"], sv["mv"], do_mla, sv["lse_mla"], dl_mla, 1.0 / math.sqrt(MLA_QK), True,
                              n("mla_attn_bwd"), ride=rides.get("mla_attn_bwd"))
    dseg0, dqan, dkvan, dqn, dkn, g["wq"], g["wkv"] = _mla_prep_bwd(seg[0], tabs, q, q["wq"], q["wkv"], dmq, dmk, dmv, dff,
                                                                   n("mla_prep_bwd"))
    g["mla_q_a_norm"], g["mla_kv_a_norm"] = dqan, dkvan
    g["mla_q_norm"], g["mla_k_norm"] = dqn[0, :MLA_QK], dkn[0, :MLA_QK]
    dsegs = [dseg0, dseg1, dseg2, dseg3, dseg4]
    dh = None
    for k in range(5):
        dh = _mm(dsegs[k], q["w_in"], "nt", n(f"dh{k}"), acc=dh, b_cols=(SEG_OFF[k], SEG_W[k]))
    g["w_in"] = [_mm(sv["h"], dsegs[k], "tn", n(f"dwin{k}")) for k in range(5)]
    dx, g["norm_g"] = _norm_bwd(sv["x"], q["norm_g"], dh, dout, n("norm_bwd"))
    return dx, g


MESH = pl.DeviceIdType.MESH
ANY = pl.BlockSpec(memory_space=pl.ANY)


def _all_gather(blocks, name):
    n = len(blocks)

    def body(*refs):
        x_refs, out_refs = refs[:n], refs[n:2 * n]
        send_sems, recv_sems, local_sems = refs[2 * n:]
        x, y, c = lax.axis_index("x"), lax.axis_index("y"), lax.axis_index("c")
        me, sibling = (x, y, c), (x, y, 1 - c)
        chips = [(1 - x, y), (x, 1 - y), (1 - x, 1 - y)]

        def slot(a, px, py, pc):
            return out_refs[a].at[4 * px + 2 * py + pc]

        def copy(a, k, blk, to, src=None):
            return pltpu.make_async_remote_copy(src_ref=slot(a, *blk) if src is None else src, dst_ref=slot(a, *blk),
                                                send_sem=send_sems.at[7 * a + k], recv_sem=recv_sems.at[7 * a + k],
                                                device_id=to, device_id_type=MESH)

        mine = [pltpu.make_async_copy(x_refs[a], slot(a, *me), local_sems.at[a]) for a in range(n)]
        for cp in mine:
            cp.start()
        first = []
        for j, chip in enumerate(chips):
            first += [copy(a, 1 + j, me, (*chip, c), src=x_refs[a]) for a in range(n)]
        first += [copy(a, 0, me, sibling, src=x_refs[a]) for a in range(n)]
        for cp in first:
            cp.start()
        passed = []
        for j, chip in enumerate(chips):
            for a in range(n):
                copy(a, 1 + j, (*chip, c), me).wait_recv()
                passed.append(copy(a, 4 + j, (*chip, c), sibling))
                passed[-1].start()
        for a in range(n):
            copy(a, 0, sibling, me).wait_recv()
        for j, chip in enumerate(chips):
            for a in range(n):
                copy(a, 4 + j, (*chip, 1 - c), me).wait_recv()
        for cp in first + passed:
            cp.wait_send()
        for cp in mine:
            cp.wait()

    return pl.pallas_call(
        body, name=name, out_shape=[jax.ShapeDtypeStruct((N_DEV,) + b.shape, b.dtype) for b in blocks],
        in_specs=[ANY] * n, out_specs=[ANY] * n,
        scratch_shapes=[pltpu.SemaphoreType.DMA((7 * n,)), pltpu.SemaphoreType.DMA((7 * n,)), pltpu.SemaphoreType.DMA((n,))],
    )(*blocks)


def _place():
    x, y, c = lax.axis_index("x"), lax.axis_index("y"), lax.axis_index("c")
    return x, y, c, [(1 - x, y), (x, 1 - y), (1 - x, 1 - y)]


def _remote(src, dst, send, recv, k, to):
    return pltpu.make_async_remote_copy(src_ref=src, dst_ref=dst, send_sem=send.at[k], recv_sem=recv.at[k], device_id=to,
                                        device_id_type=MESH)


def _plan_gather_ici(blocks):
    n = len(blocks)

    def copies(in_refs, out_refs, send, recv, local):
        x, y, c, chips = _place()
        mine = 4 * x + 2 * y + c
        loc = [pltpu.make_async_copy(in_refs[a], out_refs[a].at[mine], local.at[a]) for a in range(n)]
        rem = [_remote(in_refs[a], out_refs[a].at[mine], send, recv, 3 * a + j, (px, py, c))
               for j, (px, py) in enumerate(chips) for a in range(n)]
        return rem, loc

    return _Plan(blocks, [jax.ShapeDtypeStruct((N_DEV,) + b.shape, b.dtype) for b in blocks], 3 * n, n, copies)


def _plan_gather_d2d(gathered):
    n = len(gathered)

    def copies(in_refs, out_refs, send, recv, local):
        x, y, c, _ = _place()
        rem = [_remote(in_refs[a].at[2 * j + c], out_refs[a].at[2 * j + c], send, recv, 4 * a + j, (x, y, 1 - c))
               for a in range(n) for j in range(4)]
        return rem, []

    return _Plan(gathered, [jax.ShapeDtypeStruct(g.shape, g.dtype) for g in gathered], 4 * n, 0, copies,
                 aliases={a: a for a in range(n)})


def _plan_reduce_sibling(parts):
    n = len(parts)

    def copies(in_refs, out_refs, send, recv, local):
        x, y, c, _ = _place()
        rem = [_remote(in_refs[a].at[2 * j + (1 - c)], out_refs[a].at[j], send, recv, 4 * a + j, (x, y, 1 - c))
               for a in range(n) for j in range(4)]
        return rem, []

    return _Plan(parts, [jax.ShapeDtypeStruct((4,) + p.shape[1:], p.dtype) for p in parts], 4 * n, 0, copies)


def _plan_reduce_chips(sums):
    n = len(sums)

    def copies(in_refs, out_refs, send, recv, local):
        x, y, c, chips = _place()
        mine = 2 * x + y
        loc = [pltpu.make_async_copy(in_refs[a].at[mine], out_refs[a].at[mine], local.at[a]) for a in range(n)]
        rem = [_remote(in_refs[a].at[2 * px + py], out_refs[a].at[mine], send, recv, 3 * a + k, (px, py, c))
               for k, (px, py) in enumerate(chips) for a in range(n)]
        return rem, loc

    return _Plan(sums, [jax.ShapeDtypeStruct(p.shape, p.dtype) for p in sums], 3 * n, n, copies)


def _add_sibling(parts, got, name):
    _, r, cc = parts.shape
    tr = _pick(r, (512, 256, 128, 64, 32, 16))
    c = lax.axis_index("c")

    def body(c_ref, p_ref, g_ref, o_ref):
        o_ref[...] = (p_ref[...] + g_ref[...]).astype(BF16)

    return pl.pallas_call(
        body, name=name, out_shape=jax.ShapeDtypeStruct((4, r, cc), BF16),
        grid_spec=pltpu.PrefetchScalarGridSpec(
            num_scalar_prefetch=1, grid=(4, r // tr),
            in_specs=[pl.BlockSpec((1, tr, cc), lambda j, i, cr: (2 * j + cr[0], i, 0)),
                      pl.BlockSpec((1, tr, cc), lambda j, i, cr: (j, i, 0))],
            out_specs=pl.BlockSpec((1, tr, cc), lambda j, i, cr: (j, i, 0))),
        compiler_params=_vmem(48),
    )(c.reshape(1).astype(jnp.int32), parts, got)


def _sum_leading(parts, name):
    k, r, cc = parts.shape
    tr = _pick(r, (512, 256, 128, 64, 32, 16, 8))

    def body(p_ref, o_ref):
        acc = p_ref[0]
        for j in range(1, k):
            acc = acc + p_ref[j]
        o_ref[...] = acc

    return pl.pallas_call(
        body, name=name, out_shape=jax.ShapeDtypeStruct((r, cc), F32), grid=(r // tr,),
        in_specs=[pl.BlockSpec((k, tr, cc), lambda i: (0, i, 0))], out_specs=pl.BlockSpec((tr, cc), lambda i: (i, 0)),
    )(parts)


def _adamw_math(w, g, m, v):
    nm = ADAM_B1 * m + (1.0 - ADAM_B1) * g
    nv = ADAM_B2 * v + (1.0 - ADAM_B2) * jnp.square(g)
    m_hat = nm / (1.0 - ADAM_B1 ** ADAM_STEP)
    v_hat = nv / (1.0 - ADAM_B2 ** ADAM_STEP)
    return -ADAM_LR * (m_hat / (jnp.sqrt(v_hat) + ADAM_EPS) + ADAM_WD * w), nm, nv


def _adamw_sum(w, contribs, m, v, name):
    nl = len(contribs)
    k, r, cc = contribs[0].shape
    tr = _pick(r, (256, 128, 64, 32, 16))
    nb = r // tr

    def body(w_ref, *rest):
        c_refs = rest[:nl]
        m_ref, v_ref, g_ref, d_ref, nm_ref, nv_ref = rest[nl:]
        for li in range(nl):
            @pl.when(pl.program_id(0) == li)
            def _(c_ref=c_refs[li]):
                g = c_ref[0].astype(F32)
                for j in range(1, k):
                    g = g + c_ref[j].astype(F32)
                g_ref[...] = g
                d_ref[...], nm_ref[...], nv_ref[...] = _adamw_math(w_ref[...], g, m_ref[...], v_ref[...])

    spec = pl.BlockSpec((tr, cc), lambda l, i: (l * nb + i, 0))
    cspec = pl.BlockSpec((k, tr, cc), lambda l, i: (0, i, 0))
    return pl.pallas_call(
        body, name=name, grid=(nl, nb), in_specs=[spec] + [cspec] * nl + [spec, spec],
        out_specs=[spec] * 4, out_shape=[jax.ShapeDtypeStruct(w.shape, F32)] * 4,
        compiler_params=pltpu.CompilerParams(dimension_semantics=("parallel", "parallel"), vmem_limit_bytes=48 * 1024 * 1024),
    )(w, *contribs, m, v)


def _adamw_many(ws, gs, ms, vs, name):
    n = len(ws)

    def body(*refs):
        w_r, g_r, m_r, v_r = refs[:n], refs[n:2 * n], refs[2 * n:3 * n], refs[3 * n:4 * n]
        d_r, nm_r, nv_r = refs[4 * n:5 * n], refs[5 * n:6 * n], refs[6 * n:7 * n]
        for a in range(n):
            d_r[a][...], nm_r[a][...], nv_r[a][...] = _adamw_math(w_r[a][...], g_r[a][...], m_r[a][...], v_r[a][...])

    shapes = [jax.ShapeDtypeStruct(w.shape, F32) for w in ws]
    res = pl.pallas_call(body, name=name, out_shape=shapes * 3,
                         compiler_params=pltpu.CompilerParams(vmem_limit_bytes=56 * 1024 * 1024))(*ws, *gs, *ms, *vs)
    return res[:n], res[n:2 * n], res[2 * n:]


def _pack_rows(flat, lanes, row_mult):
    n = flat.shape[-1]
    rows = -(-n // lanes)
    rows = -(-rows // row_mult) * row_mult
    pad = rows * lanes - n
    if pad:
        flat = jnp.pad(flat, [(0, 0)] * (flat.ndim - 1) + [(0, pad)])
    return flat.reshape(flat.shape[:-1] + (rows, lanes))


def _rope_tables(positions):
    inv = 1.0 / (ROPE_THETA ** (jnp.arange(0, MLA_ROPE, 2, dtype=F32) / MLA_ROPE))
    ang = positions.astype(F32)[:, None] * inv
    cos, sin = jnp.cos(ang), jnp.sin(ang)
    s = positions.shape[0]
    z = lambda n: jnp.zeros((s, n), F32)
    c = jnp.concatenate([jnp.ones((s, 64), F32), cos, cos, z(32)], axis=1)
    sa = jnp.concatenate([z(64), -sin, z(48)], axis=1)
    sb = jnp.concatenate([z(80), sin, z(32)], axis=1)
    return c, sa, sb


def _full_weights(gathered):
    full = {}
    for k, g in zip(SHARDED, gathered):
        _, r, c = g.shape
        if k == "w_in":
            full[k] = g
        else:
            full[k] = g.transpose(1, 0, 2).reshape(r, N_DEV * c) if k in COL_SHARDED else g.reshape(N_DEV * r, c)
    return full


EARLY = ("s5_w_glu", "w_branch_out", "w_out")
LATE = ("w_in", "mla_w_q_up", "mla_w_kv_up")


def _owner_major(g, names, tag):
    parts = []
    for k in names:
        if k == "w_in":
            parts.append(_win_unpad(g["w_in"], f"{tag}_w_in_unpad"))
            continue
        big = {"mla_w_q_up": lambda: _unpad_wq(g["wq"]), "mla_w_kv_up": lambda: _unpad_wkv(g["wkv"]), "s5_w_glu": lambda: g["wglu"],
               "w_branch_out": lambda: g["wo"], "w_out": lambda: g["wout"]}[k]()
        r, c = big.shape
        if k in COL_SHARDED:
            parts.append(big.reshape(r, N_DEV, c // N_DEV).transpose(1, 0, 2))
        else:
            parts.append(big.reshape(N_DEV, r // N_DEV, c))
    return parts


def _device_step(x, positions, target, shards, small):
    tabs = _rope_tables(positions)
    box = {}
    full0 = _full_weights(_all_gather(shards[0], "gather_weights_l0"))
    q0 = _layer_params(0, full0, small)
    rides = {
        "mla_attn_fwd": _Ride(lambda: _plan_gather_ici(shards[1][:1]), lambda o: box.update(ici_a=o)),
        "fox_attn_fwd": _Ride(lambda: _join([_plan_gather_ici(shards[1][1:]), _plan_gather_d2d(box["ici_a"])]),
                              lambda o: box.update(ici_b=o[:5], w1_a=o[5:])),
        "s5_scan_fwd": _Ride(lambda: _plan_gather_d2d(box["ici_b"]), lambda o: box.update(w1_b=o)),
    }
    h, sv0 = _layer_fwd(0, x, tabs, q0, rides)
    q1 = _layer_params(1, _full_weights(list(box["w1_a"]) + list(box["w1_b"])), small)
    h, sv1 = _layer_fwd(1, h, tabs, q1)
    loss, d = _loss_head(h, target, "loss_head")
    d, g1 = _layer_bwd(1, d, tabs, q1, sv1)
    parts1 = _owner_major(g1, SHARDED, "l1")

    def chips_plan(names, parts, got, tag):
        return _plan_reduce_chips([_add_sibling(p, g, f"reduce_add_{tag}_{k}") for k, p, g in zip(names, parts, got)])

    g0 = {}
    rides = {
        "fox_attn_delta": _Ride(lambda: _plan_reduce_sibling(parts1), lambda o: box.update(got1=o)),
        "fox_attn_bwd": _Ride(lambda: chips_plan(SHARDED, parts1, box["got1"], "l1"), lambda o: box.update(contribs1=o)),
        "mla_attn_delta": _Ride(lambda: _plan_reduce_sibling(box.setdefault("early0", _owner_major(g0, EARLY, "l0"))),
                                lambda o: box.update(got0e=o)),
        "mla_attn_bwd": _Ride(lambda: chips_plan(EARLY, box["early0"], box["got0e"], "l0"), lambda o: box.update(contribs0e=o)),
    }
    d, _ = _layer_bwd(0, d, tabs, q0, sv0, rides, g0)
    late0 = _owner_major(g0, LATE, "l0")
    got0 = _run_plan(_plan_reduce_sibling(late0), "reduce_sibling_l0")
    contribs0 = list(_run_plan(chips_plan(LATE, late0, got0, "l0"), "reduce_chips_l0")) + list(box["contribs0e"])
    return loss[0, 0], d, [g0, g1], [contribs0, box["contribs1"]]


def kernel(x, positions, norm_g, w_in, mla_q_a_norm, mla_w_q_up, mla_kv_a_norm, mla_w_kv_up, mla_q_norm, mla_k_norm, fox_b_f, fox_q_norm, fox_k_norm, s5_lambda_re, s5_lambda_im, s5_log_dt, s5_b_re, s5_b_im, s5_c_re, s5_c_im, s5_d, s5_w_glu, s5_b_glu, w_branch_out, w_out, loss_target, m_norm_g, m_w_in, m_mla_q_a_norm, m_mla_w_q_up, m_mla_kv_a_norm, m_mla_w_kv_up, m_mla_q_norm, m_mla_k_norm, m_fox_b_f, m_fox_q_norm, m_fox_k_norm, m_s5_lambda_re, m_s5_lambda_im, m_s5_log_dt, m_s5_b_re, m_s5_b_im, m_s5_c_re, m_s5_c_im, m_s5_d, m_s5_w_glu, m_s5_b_glu, m_w_branch_out, m_w_out, v_norm_g, v_w_in, v_mla_q_a_norm, v_mla_w_q_up, v_mla_kv_a_norm, v_mla_w_kv_up, v_mla_q_norm, v_mla_k_norm, v_fox_b_f, v_fox_q_norm, v_fox_k_norm, v_s5_lambda_re, v_s5_lambda_im, v_s5_log_dt, v_s5_b_re, v_s5_b_im, v_s5_c_re, v_s5_c_im, v_s5_d, v_s5_w_glu, v_s5_b_glu, v_w_branch_out, v_w_out):
    env = dict(locals())
    wts = {k: env[k] for k in WEIGHTS}
    mom = {k: env["m_" + k] for k in WEIGHTS}
    var = {k: env["v_" + k] for k in WEIGHTS}

    shards = [[wts[k][l].astype(BF16) for k in SHARDED] for l in range(DEPTH)]
    small = {k: wts[k] for k in SMALL}
    loss, dx, grads, contribs = _device_step(x[0], positions[0], loss_target[0], shards, small)
    loss = lax.psum(loss, ("x", "y", "c"))

    two_d = {k: (wts[k].shape[0] * wts[k].shape[1], wts[k].shape[2]) for k in SHARDED}
    grad_out, delta_out, m_out, v_out = {}, {}, {}, {}
    for i, k in enumerate(SHARDED):
        shp = wts[k].shape
        res = _adamw_sum(wts[k].reshape(two_d[k]), [contribs[l][i] for l in range(DEPTH)], mom[k].reshape(two_d[k]),
                         var[k].reshape(two_d[k]), f"adamw_{k}")
        grad_out[k], delta_out[k], m_out[k], v_out[k] = (z.reshape(shp) for z in res)

    sm = {k: jnp.stack([g[k] for g in grads]).reshape(wts[k].shape) for k in SMALL}
    flat_s = jnp.concatenate([sm[k].reshape(-1) for k in SMALL])
    g_small = _sum_leading(_all_gather([_pack_rows(flat_s, LANES, 256)], "gather_small_grads")[0], "sum_small_grads").reshape(-1)
    off = 0
    for k in SMALL:
        cnt = int(np.prod(wts[k].shape))
        grad_out[k] = g_small[off:off + cnt].reshape(wts[k].shape)
        off += cnt
    flat2 = lambda a: a.reshape(-1, a.shape[-1])
    d_s, m_s, v_s = _adamw_many([flat2(wts[k]) for k in SMALL], [flat2(grad_out[k]) for k in SMALL],
                                [flat2(mom[k]) for k in SMALL], [flat2(var[k]) for k in SMALL], "adamw_small")
    for i, k in enumerate(SMALL):
        delta_out[k], m_out[k], v_out[k] = (z[i].reshape(wts[k].shape) for z in (d_s, m_s, v_s))

    return (loss, dx[None], *[grad_out[k] for k in WEIGHTS], *[delta_out[k] for k in WEIGHTS],
            *[m_out[k] for k in WEIGHTS], *[v_out[k] for k in WEIGHTS])
```

```python
import functools
import math

import jax
import jax.numpy as jnp
import numpy as np
from jax import lax
from jax.experimental import pallas as pl
from jax.experimental.pallas import tpu as pltpu

F32 = jnp.float32
BF16 = jnp.bfloat16

D_MODEL = 1024
DEPTH = 2
CHUNK = 64
EPS = 1e-6
HEADS = 8
MLA_NOPE, MLA_ROPE, MLA_V = 64, 32, 64
MLA_Q_RANK, MLA_KV_RANK = 256, 128
MLA_QK = MLA_NOPE + MLA_ROPE
ROPE_THETA = 10000.0
FOX_DIM = 64
S5_WIDTH, S5_GROUP, S5_GROUPS, S5_STATE = 512, 16, 32, 64
S5_LANES = S5_GROUPS * S5_STATE
IN_WIDTH = 7080
N_DEV = 8
LANES = 128
SUBLANES = 8

ADAM_LR, ADAM_B1, ADAM_B2, ADAM_EPS, ADAM_WD, ADAM_STEP = 0.001, 0.9, 0.999, 1e-08, 0.01, 10

SEG_W = (512, 1536, 512, 1536, 3072)
SEG_OFF = (0, 512, 2048, 2560, 4096)
PAD_IN = 7168
NEG = -1e30

SHARDED = ("w_in", "mla_w_q_up", "mla_w_kv_up", "s5_w_glu", "w_branch_out", "w_out")
COL_SHARDED = ("w_in", "mla_w_q_up", "mla_w_kv_up")
SMALL = ("norm_g", "mla_q_a_norm", "mla_kv_a_norm", "mla_q_norm", "mla_k_norm", "fox_b_f", "fox_q_norm", "fox_k_norm",
         "s5_lambda_re", "s5_lambda_im", "s5_log_dt", "s5_b_re", "s5_b_im", "s5_c_re", "s5_c_im", "s5_d", "s5_b_glu")
WEIGHTS = ("norm_g", "w_in", "mla_q_a_norm", "mla_w_q_up", "mla_kv_a_norm", "mla_w_kv_up", "mla_q_norm", "mla_k_norm",
           "fox_b_f", "fox_q_norm", "fox_k_norm", "s5_lambda_re", "s5_lambda_im", "s5_log_dt", "s5_b_re", "s5_b_im",
           "s5_c_re", "s5_c_im", "s5_d", "s5_w_glu", "s5_b_glu", "w_branch_out", "w_out")
PACK_LANES = 512


def _pick(n, cands):
    for c in cands:
        if n % c == 0:
            return c
    return n


def _vmem(mb):
    return pltpu.CompilerParams(vmem_limit_bytes=mb * 1024 * 1024)


def _dot(a, b, dims):
    return lax.dot_general(a.astype(BF16), b.astype(BF16), (dims, ((), ())), preferred_element_type=F32)


def _nn(a, b):
    return _dot(a, b, ((1,), (0,)))


def _nt(a, b):
    return _dot(a, b, ((1,), (1,)))


def _tn(a, b):
    return _dot(a, b, ((0,), (0,)))


def _rms(x, g, n=None):
    n = x.shape[-1] if n is None else n
    return x * lax.rsqrt(jnp.sum(x * x, axis=-1, keepdims=True) / n + EPS) * g


def _rope(t, c, sa, sb):
    return t * c + pltpu.roll(t, LANES - 16, 1) * sa + pltpu.roll(t, 16, 1) * sb


def _rope_t(d, c, sa, sb):
    return d * c + pltpu.roll(d * sa, 16, 1) + pltpu.roll(d * sb, LANES - 16, 1)


def _mm(a, b, mode, name, acc=None, b_cols=None):
    if mode == "tn":
        kd, m = a.shape
    else:
        m, kd = a.shape
    b_off, b_w = b_cols if b_cols is not None else (0, b.shape[1])
    n = b.shape[0] if mode == "nt" else b_w
    tm, tn, tk = _pick(m, (1024, 512, 256, 128)), _pick(n, (1024, 512, 256, 128)), _pick(kd, (1024, 512, 256, 128))
    nk = kd // tk
    if mode == "tn":
        a_spec = pl.BlockSpec((tk, tm), lambda i, j, k: (k, i))
    else:
        a_spec = pl.BlockSpec((tm, tk), lambda i, j, k: (i, k))
    if mode == "nt":
        assert b_off % tk == 0
        b_spec = pl.BlockSpec((tn, tk), lambda i, j, k: (j, k + b_off // tk))
    else:
        assert b_off % tn == 0
        b_spec = pl.BlockSpec((tk, tn), lambda i, j, k: (k, j + b_off // tn))
    dims = {"nn": ((1,), (0,)), "nt": ((1,), (1,)), "tn": ((0,), (0,))}[mode]
    o_spec = pl.BlockSpec((tm, tn), lambda i, j, k: (i, j))
    has_acc = acc is not None

    def body(*refs):
        if has_acc:
            a_ref, b_ref, c_ref, o_ref = refs
        else:
            a_ref, b_ref, o_ref = refs
        k = pl.program_id(2)
        prod = _dot(a_ref[...], b_ref[...], dims)

        @pl.when(k == 0)
        def _():
            o_ref[...] = prod + c_ref[...] if has_acc else prod

        @pl.when(k > 0)
        def _():
            o_ref[...] += prod

    ins = [a, b] + ([acc] if has_acc else [])
    in_specs = [a_spec, b_spec] + ([o_spec] if has_acc else [])
    return pl.pallas_call(
        body, name=name, grid=(m // tm, n // tn, nk), in_specs=in_specs, out_specs=o_spec,
        out_shape=jax.ShapeDtypeStruct((m, n), F32),
        compiler_params=pltpu.CompilerParams(dimension_semantics=("parallel", "parallel", "arbitrary"),
                                             vmem_limit_bytes=48 * 1024 * 1024),
    )(*ins)


def _stage(name, fn, n_steps, ins, outs, accs=(), scratch=(), vmem_mb=48):
    n_in, n_out, n_acc = len(ins), len(outs), len(accs)

    def body(*refs):
        in_refs = refs[:n_in]
        out_refs = refs[n_in:n_in + n_out]
        acc_refs = refs[n_in + n_out:n_in + n_out + n_acc]
        scr = refs[n_in + n_out + n_acc:]
        if n_acc:
            @pl.when(pl.program_id(0) == 0)
            def _():
                for r in acc_refs:
                    r[...] = jnp.zeros(r.shape, r.dtype)
        fn(in_refs, out_refs, acc_refs, scr)

    acc_specs = [pl.BlockSpec(a.shape, functools.partial(lambda i, nd: (0,) * nd, nd=len(a.shape))) for a in accs]
    res = pl.pallas_call(
        body, name=name, grid=(n_steps,),
        in_specs=[s for _, s in ins], out_specs=[s for _, s in outs] + acc_specs,
        out_shape=[s for s, _ in outs] + list(accs), scratch_shapes=list(scratch),
        compiler_params=pltpu.CompilerParams(dimension_semantics=("arbitrary",),
                                             vmem_limit_bytes=vmem_mb * 1024 * 1024),
    )(*[a for a, _ in ins])
    return res


def _rows(ts, w, j=0):
    return pl.BlockSpec((ts, w), lambda i: (i, j))


def _rows_rev(ts, w, n, j=0):
    return pl.BlockSpec((ts, w), lambda i: (n - 1 - i, j))


def _heads(ts, d):
    return pl.BlockSpec((HEADS, ts, d), lambda i: (0, i, 0))


def _heads_rev(ts, d, n):
    return pl.BlockSpec((HEADS, ts, d), lambda i: (0, n - 1 - i, 0))


def _full(shape):
    nd = len(shape)
    return pl.BlockSpec(tuple(shape), lambda i: (0,) * nd)


def _sds(shape, dtype=F32):
    return jax.ShapeDtypeStruct(tuple(shape), dtype)


def _norm_fwd(x, g, name):
    s = x.shape[0]
    ts = _pick(s, (256, 128))

    def fn(ins, outs, accs, scr):
        outs[0][...] = _rms(ins[0][...], ins[1][...]).astype(BF16)

    return _stage(name, fn, s // ts, [(x, _rows(ts, D_MODEL)), (g, _full(g.shape))],
                  [(_sds((s, D_MODEL), BF16), _rows(ts, D_MODEL))])[0]


def _norm_bwd(x, g, dh, dres, name):
    s = x.shape[0]
    ts = _pick(s, (256, 128))

    def fn(ins, outs, accs, scr):
        _, vjp = jax.vjp(_rms, ins[0][...], ins[1][...])
        dx, dg = vjp(ins[2][...])
        outs[0][...] = dx + ins[3][...]
        accs[0][...] += dg

    r = _stage(name, fn, s // ts,
               [(x, _rows(ts, D_MODEL)), (g, _full(g.shape)), (dh, _rows(ts, D_MODEL)), (dres, _rows(ts, D_MODEL))],
               [(_sds((s, D_MODEL)), _rows(ts, D_MODEL))], accs=[_sds((1, D_MODEL))])
    return r[0], r[1]


def _mla_q(qraw, c, sa, sb, qn):
    return _rms(_rope(qraw, c, sa, sb), qn, MLA_QK)


def _mla_prep_fwd(seg0, tabs, p, wq, wkv, name):
    s = seg0.shape[0]
    ts = _pick(s, (256, 128))

    def fn(ins, outs, accs, scr):
        blk, cos, sa, sb, qan, kvan, qn, kn, wq_r, wkv_r = ins
        b = blk[...]
        cq, ckv, kt = b[:, :256], b[:, 256:384], b[:, 384:512]
        lane = lax.broadcasted_iota(jnp.int32, kt.shape, 1)
        kpe = jnp.where(lane >= 64, kt, 0.0)
        q_raw = _nn(_rms(cq, qan[...]), wq_r[...])
        kv_raw = _nn(_rms(ckv, kvan[...]), wkv_r[...])
        c, a, bb = cos[...], sa[...], sb[...]
        for h in range(HEADS):
            outs[0][h] = _mla_q(q_raw[:, LANES * h:LANES * (h + 1)], c, a, bb, qn[...]).astype(BF16)
            outs[1][h] = _mla_q(kv_raw[:, LANES * h:LANES * (h + 1)] + kpe, c, a, bb, kn[...]).astype(BF16)
            outs[2][h] = kv_raw[:, 1024 + 64 * h:1024 + 64 * (h + 1)].astype(BF16)

    consts = [p["mla_q_a_norm"], p["mla_kv_a_norm"], p["mla_q_norm"], p["mla_k_norm"], wq, wkv]
    return _stage(name, fn, s // ts,
                  [(seg0, _rows(ts, 512))] + [(t, _rows(ts, LANES)) for t in tabs] + [(a, _full(a.shape)) for a in consts],
                  [(_sds((HEADS, s, LANES), BF16), _heads(ts, LANES)), (_sds((HEADS, s, LANES), BF16), _heads(ts, LANES)),
                   (_sds((HEADS, s, 64), BF16), _heads(ts, 64))])


def _mla_prep_bwd(seg0, tabs, p, wq, wkv, dq, dk, dv, dff, name):
    s = seg0.shape[0]
    ts = _pick(s, (256, 128))

    def fn(ins, outs, accs, scr):
        blk, cos, sa, sb, qan, kvan, qn, kn, wq_r, wkv_r, dq_r, dk_r, dv_r, dff_r = ins
        dqan, dkvan, dqn, dkn, dwq, dwkv = accs
        dqraw_s, dkvraw_s = scr
        b = blk[...]
        cq, ckv, kt = b[:, :256], b[:, 256:384], b[:, 384:512]
        lane = lax.broadcasted_iota(jnp.int32, kt.shape, 1)
        kpe = jnp.where(lane >= 64, kt, 0.0)
        cqn, vjp_cq = jax.vjp(_rms, cq, qan[...])
        ckvn, vjp_ckv = jax.vjp(_rms, ckv, kvan[...])
        q_raw = _nn(cqn, wq_r[...])
        kv_raw = _nn(ckvn, wkv_r[...])
        c, a, bb = cos[...], sa[...], sb[...]

        def head_bwd(raw, gain, d):
            t = _rope(raw, c, a, bb)
            _, vjp = jax.vjp(functools.partial(_rms, n=MLA_QK), t, gain)
            dt, dgain = vjp(d)
            return _rope_t(dt, c, a, bb), dgain

        dkpe = jnp.zeros(kt.shape, F32)
        for h in range(HEADS):
            dqh, dg = head_bwd(q_raw[:, LANES * h:LANES * (h + 1)], qn[...], dq_r[h])
            dqn[...] += dg
            dqraw_s[:, LANES * h:LANES * (h + 1)] = dqh
            dkh, dg = head_bwd(kv_raw[:, LANES * h:LANES * (h + 1)] + kpe, kn[...], dk_r[h])
            dkn[...] += dg
            dkvraw_s[:, LANES * h:LANES * (h + 1)] = dkh
            dkpe = dkpe + dkh
            dkvraw_s[:, 1024 + 64 * h:1024 + 64 * (h + 1)] = dv_r[h]
        dq_raw = dqraw_s[...]
        dkv_raw = dkvraw_s[...]
        dwq[...] += _tn(cqn, dq_raw)
        dwkv[...] += _tn(ckvn, dkv_raw)
        dcq, dg = vjp_cq(_nt(dq_raw, wq_r[...]))
        dqan[...] += dg
        dckv, dg = vjp_ckv(_nt(dkv_raw, wkv_r[...]))
        dkvan[...] += dg
        outs[0][:, 0:256] = dcq.astype(BF16)
        outs[0][:, 256:384] = dckv.astype(BF16)
        outs[0][:, 384:512] = (jnp.where(lane >= 64, dkpe, 0.0) + dff_r[...]).astype(BF16)

    consts = [p["mla_q_a_norm"], p["mla_kv_a_norm"], p["mla_q_norm"], p["mla_k_norm"], wq, wkv]
    return _stage(name, fn, s // ts,
                  [(seg0, _rows(ts, 512))] + [(t, _rows(ts, LANES)) for t in tabs] + [(a, _full(a.shape)) for a in consts]
                  + [(dq, _heads(ts, LANES)), (dk, _heads(ts, LANES)), (dv, _heads(ts, 64)), (dff, _rows(ts, LANES))],
                  [(_sds((s, 512), BF16), _rows(ts, 512))],
                  accs=[_sds((1, 256)), _sds((1, 128)), _sds((1, LANES)), _sds((1, LANES)), _sds(wq.shape), _sds(wkv.shape)],
                  scratch=[pltpu.VMEM((ts, 1024), F32), pltpu.VMEM((ts, 1536), F32)])


def _fox_prep_fwd(seg0, seg1, bf, qn, kn, name):
    s = seg0.shape[0]
    ts = _pick(s, (256, 128))
    steps = int(math.log2(ts))

    def fn(ins, outs, accs, scr):
        kt_r, x_r, bf_r, qn_r, kn_r = ins
        carry = scr[0]

        @pl.when(pl.program_id(0) == 0)
        def _():
            carry[...] = jnp.zeros(carry.shape, F32)

        x = x_r[...]
        for h in range(HEADS):
            outs[0][h] = _rms(x[:, 64 * h:64 * (h + 1)], qn_r[...]).astype(BF16)
            outs[1][h] = _rms(x[:, 512 + 64 * h:512 + 64 * (h + 1)], kn_r[...]).astype(BF16)
            outs[2][h] = x[:, 1024 + 64 * h:1024 + 64 * (h + 1)].astype(BF16)
        kt = kt_r[...]
        lane = lax.broadcasted_iota(jnp.int32, kt.shape, 1)
        row = lax.broadcasted_iota(jnp.int32, kt.shape, 0)
        cs = jnp.where(lane < HEADS, jax.nn.log_sigmoid(kt + bf_r[...]), 0.0)
        for k in range(steps):
            sh = 1 << k
            cs = cs + jnp.where(row >= sh, pltpu.roll(cs, sh, 0), 0.0)
        cs = cs + carry[0:1, :]
        outs[3][...] = cs
        outs[4][...] = cs.T[0:HEADS, :]
        carry[0:1, :] = cs[ts - 1:ts, :]

    return _stage(name, fn, s // ts,
                  [(seg0, _rows(ts, LANES, 3)), (seg1, _rows(ts, 1536)), (bf, _full(bf.shape)), (qn, _full(qn.shape)),
                   (kn, _full(kn.shape))],
                  [(_sds((HEADS, s, 64), BF16), _heads(ts, 64)), (_sds((HEADS, s, 64), BF16), _heads(ts, 64)),
                   (_sds((HEADS, s, 64), BF16), _heads(ts, 64)), (_sds((s, LANES)), _rows(ts, LANES)),
                   (_sds((HEADS, s)), pl.BlockSpec((HEADS, ts), lambda i: (0, i)))],
                  scratch=[pltpu.VMEM((SUBLANES, LANES), F32)])


def _fox_prep_bwd(seg0, seg1, bf, qn, kn, dq, dk, dv, dck, name):
    s = seg0.shape[0]
    ts = _pick(s, (256, 128))
    n = s // ts
    steps = int(math.log2(ts))

    def fn(ins, outs, accs, scr):
        kt_r, x_r, bf_r, qn_r, kn_r, dq_r, dk_r, dv_r, dck_r = ins
        dqn, dkn, dbf = accs
        carry, dbuf = scr

        @pl.when(pl.program_id(0) == 0)
        def _():
            carry[...] = jnp.zeros(carry.shape, F32)

        x = x_r[...]
        for h in range(HEADS):
            _, vjp = jax.vjp(_rms, x[:, 64 * h:64 * (h + 1)], qn_r[...])
            d, dg = vjp(dq_r[h])
            dbuf[:, 64 * h:64 * (h + 1)] = d
            dqn[...] += dg
            _, vjp = jax.vjp(_rms, x[:, 512 + 64 * h:512 + 64 * (h + 1)], kn_r[...])
            d, dg = vjp(dk_r[h])
            dbuf[:, 512 + 64 * h:512 + 64 * (h + 1)] = d
            dkn[...] += dg
            dbuf[:, 1024 + 64 * h:1024 + 64 * (h + 1)] = dv_r[h]
        outs[0][...] = dbuf[...].astype(BF16)
        dc = dck_r[...].reshape(HEADS, ts)
        dc = jnp.concatenate([dc, jnp.zeros((LANES - HEADS, ts), F32)], axis=0).T
        row = lax.broadcasted_iota(jnp.int32, dc.shape, 0)
        lane = lax.broadcasted_iota(jnp.int32, dc.shape, 1)
        for k in range(steps):
            sh = 1 << k
            dc = dc + jnp.where(row < ts - sh, pltpu.roll(dc, ts - sh, 0), 0.0)
        dc = dc + carry[0:1, :]
        carry[0:1, :] = dc[0:1, :]
        dff = jnp.where(lane < HEADS, dc * jax.nn.sigmoid(-(kt_r[...] + bf_r[...])), 0.0)
        outs[1][...] = dff
        dbf[...] += jnp.sum(dff, axis=0, keepdims=True)

    return _stage(name, fn, n,
                  [(seg0, _rows_rev(ts, LANES, n, 3)), (seg1, _rows_rev(ts, 1536, n)), (bf, _full(bf.shape)),
                   (qn, _full(qn.shape)), (kn, _full(kn.shape)), (dq, _heads_rev(ts, 64, n)), (dk, _heads_rev(ts, 64, n)),
                   (dv, _heads_rev(ts, 64, n)), (dck, pl.BlockSpec((HEADS, 1, ts), lambda i: (0, 0, n - 1 - i)))],
                  [(_sds((s, 1536), BF16), _rows_rev(ts, 1536, n)), (_sds((s, LANES)), _rows_rev(ts, LANES, n))],
                  accs=[_sds((1, 64)), _sds((1, 64)), _sds((1, LANES))],
                  scratch=[pltpu.VMEM((SUBLANES, LANES), F32), pltpu.VMEM((ts, 1536), F32)])


def _allowed(i, j, t, chunk_causal):
    qpos = i * t + lax.broadcasted_iota(jnp.int32, (t, t), 0)
    kpos = j * t + lax.broadcasted_iota(jnp.int32, (t, t), 1)
    if chunk_causal:
        return (kpos // CHUNK) <= (qpos // CHUNK)
    return kpos <= qpos


def _pick_col(c_blk, h):
    lane = lax.broadcasted_iota(jnp.int32, c_blk.shape, 1)
    return jnp.sum(jnp.where(lane == h, c_blk, 0.0), axis=1, keepdims=True)


class _Ride:
    def __init__(self, make, take):
        self.make, self.take = make, take


class _Plan:
    def __init__(self, ins, out_shapes, n_remote, n_local, copies, aliases=None):
        self.ins, self.out_shapes, self.n_remote, self.n_local = list(ins), list(out_shapes), n_remote, n_local
        self.copies, self.aliases = copies, dict(aliases or {})

    def scratch(self):
        return [pltpu.SemaphoreType.DMA((self.n_remote,)), pltpu.SemaphoreType.DMA((self.n_remote,)),
                pltpu.SemaphoreType.DMA((max(self.n_local, 1),))]

    def start(self, in_refs, out_refs, sems):
        remote, local = self.copies(in_refs, out_refs, *sems)
        for cp in local + remote:
            cp.start()

    def wait(self, in_refs, out_refs, sems):
        remote, local = self.copies(in_refs, out_refs, *sems)
        for cp in remote:
            cp.wait()
        for cp in local:
            cp.wait()


class _Off:
    def __init__(self, ref, off):
        self.ref, self.off, self.at = ref, off, self

    def __getitem__(self, k):
        return self.ref.at[k + self.off]


def _join(plans):
    ins = [a for p in plans for a in p.ins]
    outs = [o for p in plans for o in p.out_shapes]
    aliases, i0, o0 = {}, 0, 0
    for p in plans:
        aliases.update({i0 + i: o0 + o for i, o in p.aliases.items()})
        i0, o0 = i0 + len(p.ins), o0 + len(p.out_shapes)

    def copies(in_refs, out_refs, send, recv, local):
        rem, loc, i0, o0, r0, l0 = [], [], 0, 0, 0, 0
        for p in plans:
            r, l = p.copies(in_refs[i0:i0 + len(p.ins)], out_refs[o0:o0 + len(p.out_shapes)], _Off(send, r0), _Off(recv, r0),
                            _Off(local, l0))
            rem, loc = rem + r, loc + l
            i0, o0, r0, l0 = i0 + len(p.ins), o0 + len(p.out_shapes), r0 + p.n_remote, l0 + p.n_local
        return rem, loc

    return _Plan(ins, outs, sum(p.n_remote for p in plans), sum(p.n_local for p in plans), copies, aliases)


def _call_with_ride(core, name, grid, ins, in_specs, out_shape, out_specs, semantics, vmem_mb, ride):
    n_in, n_out = len(ins), len(out_shape)
    if ride is None:
        return pl.pallas_call(
            core, name=name, grid=grid, in_specs=in_specs, out_specs=out_specs, out_shape=out_shape,
            compiler_params=pltpu.CompilerParams(dimension_semantics=semantics, vmem_limit_bytes=vmem_mb * 1024 * 1024),
        )(*ins)
    plan = ride.make()
    ci, co = len(plan.ins), len(plan.out_shapes)

    def body(*refs):
        c_in = refs[n_in:n_in + ci]
        a_out = refs[n_in + ci:n_in + ci + n_out]
        c_out = refs[n_in + ci + n_out:n_in + ci + n_out + co]
        sems = refs[n_in + ci + n_out + co:]
        ids = [pl.program_id(d) for d in range(len(grid))]
        first = functools.reduce(jnp.logical_and, [i == 0 for i in ids])
        last = functools.reduce(jnp.logical_and, [i == g - 1 for i, g in zip(ids, grid)])

        @pl.when(first)
        def _():
            plan.start(c_in, c_out, sems)

        core(*refs[:n_in], *a_out)

        @pl.when(last)
        def _():
            plan.wait(c_in, c_out, sems)

    res = pl.pallas_call(
        body, name=name, grid=grid, in_specs=list(in_specs) + [ANY] * ci, out_specs=list(out_specs) + [ANY] * co,
        out_shape=list(out_shape) + plan.out_shapes, scratch_shapes=plan.scratch(),
        input_output_aliases={n_in + i: n_out + o for i, o in plan.aliases.items()},
        compiler_params=pltpu.CompilerParams(dimension_semantics=("arbitrary",) * len(grid),
                                             vmem_limit_bytes=vmem_mb * 1024 * 1024),
    )(*ins, *plan.ins)
    ride.take(res[n_out:])
    return res[:n_out]


def _run_plan(plan, name):
    ci = len(plan.ins)

    def body(*refs):
        c_in, c_out, sems = refs[:ci], refs[ci:ci + len(plan.out_shapes)], refs[ci + len(plan.out_shapes):]
        plan.start(c_in, c_out, sems)
        plan.wait(c_in, c_out, sems)

    return pl.pallas_call(body, name=name, in_specs=[ANY] * ci, out_specs=[ANY] * len(plan.out_shapes),
                          out_shape=plan.out_shapes, scratch_shapes=plan.scratch(),
                          input_output_aliases=plan.aliases)(*plan.ins)


def _attn_fwd(q, k, v, scale, chunk_causal, name, c=None, ct=None, hps=8, ride=None):
    _, s, dk = q.shape
    dv = v.shape[2]
    t = _pick(s, (256, 128))
    bias = c is not None

    def body(*refs):
        if bias:
            q_ref, k_ref, v_ref, c_ref, ct_ref, o_ref, lse_ref = refs
        else:
            q_ref, k_ref, v_ref, o_ref, lse_ref = refs
        hp, i = pl.program_id(0), pl.program_id(1)
        qb = [q_ref[e] for e in range(hps)]
        cq = [_pick_col(c_ref[...], hp * hps + e) if bias else None for e in range(hps)]

        def step(j, carry, diagonal):
            off = pl.multiple_of(j * t, t)
            out = []
            for e in range(hps):
                m, l, acc = carry[e]
                sc = _nt(qb[e], k_ref[e, pl.ds(off, t), :]) * scale
                if bias:
                    sc = sc + (cq[e] - ct_ref[pl.ds(hp * hps + e, 1), pl.ds(off, t)])
                if diagonal:
                    sc = jnp.where(_allowed(i, j, t, chunk_causal), sc, NEG)
                m_new = jnp.maximum(m, jnp.max(sc, axis=1, keepdims=True))
                pr = jnp.exp(sc - m_new)
                alpha = jnp.exp(m - m_new)
                out.append((m_new, alpha * l + jnp.sum(pr, axis=1, keepdims=True),
                            alpha * acc + _nn(pr, v_ref[e, pl.ds(off, t), :])))
            return tuple(out)

        init = tuple((jnp.full((t, 1), NEG, F32), jnp.zeros((t, 1), F32), jnp.zeros((t, dv), F32)) for _ in range(hps))
        res = step(i, lax.fori_loop(0, i, functools.partial(step, diagonal=False), init), True)
        for e in range(hps):
            m, l, acc = res[e]
            o_ref[e] = acc / l
            lse_ref[e] = m + jnp.log(l)

    ins = [q, k, v] + ([c, ct] if bias else [])
    in_specs = [pl.BlockSpec((hps, t, dk), lambda h, i: (h, i, 0)), pl.BlockSpec((hps, s, dk), lambda h, i: (h, 0, 0)),
                pl.BlockSpec((hps, s, dv), lambda h, i: (h, 0, 0))]
    if bias:
        in_specs += [pl.BlockSpec((t, LANES), lambda h, i: (i, 0)), pl.BlockSpec((HEADS, s), lambda h, i: (0, 0))]
    return _call_with_ride(
        body, name, (HEADS // hps, s // t), ins, in_specs, [_sds((HEADS, s, dv)), _sds((HEADS, s, 1))],
        [pl.BlockSpec((hps, t, dv), lambda h, i: (h, i, 0)), pl.BlockSpec((hps, t, 1), lambda h, i: (h, i, 0))],
        ("parallel", "parallel"), 48, ride)


def _attn_delta(q, k, v, do, lse, scale, chunk_causal, name, c=None, ct=None, hps=4, ride=None):
    _, s, dk = q.shape
    dv = v.shape[2]
    t = _pick(s, (256, 128))
    bias = c is not None

    def body(*refs):
        if bias:
            q_ref, k_ref, v_ref, do_ref, lse_ref, c_ref, ct_ref, dl_ref = refs
        else:
            q_ref, k_ref, v_ref, do_ref, lse_ref, dl_ref = refs
        hp, i = pl.program_id(0), pl.program_id(1)
        qb, dob, lse_b = ([r[e] for e in range(hps)] for r in (q_ref, do_ref, lse_ref))
        cq = [_pick_col(c_ref[...], hp * hps + e) if bias else None for e in range(hps)]

        def step(j, acc, diagonal):
            off = pl.multiple_of(j * t, t)
            out = []
            for e in range(hps):
                sc = _nt(qb[e], k_ref[e, pl.ds(off, t), :]) * scale
                if bias:
                    sc = sc + (cq[e] - ct_ref[pl.ds(hp * hps + e, 1), pl.ds(off, t)])
                pr = jnp.exp(sc - lse_b[e])
                if diagonal:
                    pr = jnp.where(_allowed(i, j, t, chunk_causal), pr, 0.0)
                out.append(acc[e] + jnp.sum(pr * _nt(dob[e], v_ref[e, pl.ds(off, t), :]), axis=1, keepdims=True))
            return tuple(out)

        init = tuple(jnp.zeros((t, 1), F32) for _ in range(hps))
        res = step(i, lax.fori_loop(0, i, functools.partial(step, diagonal=False), init), True)
        for e in range(hps):
            dl_ref[e] = res[e]

    ins = [q, k, v, do, lse] + ([c, ct] if bias else [])
    in_specs = [pl.BlockSpec((hps, t, dk), lambda h, i: (h, i, 0)), pl.BlockSpec((hps, s, dk), lambda h, i: (h, 0, 0)),
                pl.BlockSpec((hps, s, dv), lambda h, i: (h, 0, 0)), pl.BlockSpec((hps, t, dv), lambda h, i: (h, i, 0)),
                pl.BlockSpec((hps, t, 1), lambda h, i: (h, i, 0))]
    if bias:
        in_specs += [pl.BlockSpec((t, LANES), lambda h, i: (i, 0)), pl.BlockSpec((HEADS, s), lambda h, i: (0, 0))]
    return _call_with_ride(body, name, (HEADS // hps, s // t), ins, in_specs, [_sds((HEADS, s, 1))],
                           [pl.BlockSpec((hps, t, 1), lambda h, i: (h, i, 0))], ("parallel", "parallel"), 48, ride)[0]


def _attn_bwd(q, k, v, do, lse, delta, scale, chunk_causal, name, c=None, ct=None, hps=8, ride=None):
    _, s, dk = q.shape
    dv = v.shape[2]
    t = _pick(s, (256, 128))
    n = s // t
    bias = c is not None

    def body(*refs):
        if bias:
            q_ref, k_ref, v_ref, do_ref, lse_ref, dl_ref, c_ref, ct_ref, dq_ref, dk_ref, dv_ref, dck_ref = refs
        else:
            q_ref, k_ref, v_ref, do_ref, lse_ref, dl_ref, dq_ref, dk_ref, dv_ref = refs
        hp, j = pl.program_id(0), pl.program_id(1)

        @pl.when(j == 0)
        def _():
            dq_ref[...] = jnp.zeros(dq_ref.shape, F32)

        kb, vb = [k_ref[e] for e in range(hps)], [v_ref[e] for e in range(hps)]
        joff = pl.multiple_of(j * t, t)
        ck = [ct_ref[pl.ds(hp * hps + e, 1), pl.ds(joff, t)] if bias else None for e in range(hps)]

        def step(i, carry, diagonal):
            off = pl.multiple_of(i * t, t)
            out = []
            for e in range(hps):
                dk_acc, dv_acc, dck_acc = carry[e]
                qb = q_ref[e, pl.ds(off, t), :]
                dob = do_ref[e, pl.ds(off, t), :]
                sc = _nt(qb, kb[e]) * scale
                if bias:
                    sc = sc + (_pick_col(c_ref[pl.ds(off, t), :], hp * hps + e) - ck[e])
                pr = jnp.exp(sc - lse_ref[e, pl.ds(off, t), :])
                if diagonal:
                    pr = jnp.where(_allowed(i, j, t, chunk_causal), pr, 0.0)
                dv_acc = dv_acc + _tn(pr, dob)
                ds = pr * (_nt(dob, vb[e]) - dl_ref[e, pl.ds(off, t), :])
                dq_ref[e, pl.ds(off, t), :] += _nn(ds, kb[e]) * scale
                dk_acc = dk_acc + _tn(ds, qb) * scale
                if bias:
                    dck_acc = dck_acc - jnp.sum(ds, axis=0, keepdims=True)
                out.append((dk_acc, dv_acc, dck_acc))
            return tuple(out)

        zero = tuple((jnp.zeros((t, dk), F32), jnp.zeros((t, dv), F32), jnp.zeros((1, t), F32)) for _ in range(hps))
        res = lax.fori_loop(j + 1, n, functools.partial(step, diagonal=False), step(j, zero, True))
        for e in range(hps):
            dk_ref[e], dv_ref[e] = res[e][0], res[e][1]
            if bias:
                dck_ref[e] = res[e][2]

    ins = [q, k, v, do, lse, delta] + ([c, ct] if bias else [])
    full = lambda d: pl.BlockSpec((hps, s, d), lambda h, j: (h, 0, 0))
    blk = lambda d: pl.BlockSpec((hps, t, d), lambda h, j: (h, j, 0))
    in_specs = [full(dk), blk(dk), blk(dv), full(dv), full(1), full(1)]
    if bias:
        in_specs += [pl.BlockSpec((s, LANES), lambda h, j: (0, 0)), pl.BlockSpec((HEADS, s), lambda h, j: (0, 0))]
    out_specs = [full(dk), blk(dk), blk(dv)]
    out_shape = [_sds((HEADS, s, dk)), _sds((HEADS, s, dk)), _sds((HEADS, s, dv))]
    if bias:
        out_specs.append(pl.BlockSpec((hps, 1, t), lambda h, j: (h, 0, j)))
        out_shape.append(_sds((HEADS, 1, s)))
    return _call_with_ride(body, name, (HEADS // hps, n), ins, in_specs, out_shape, out_specs, ("parallel", "arbitrary"), 56,
                           ride)


def _s5_disc(lr, li, ldt, br, bi):
    dt = jnp.exp(ldt)
    mag = jnp.exp(lr * dt)
    a_re = mag * jnp.cos(li * dt)
    a_im = mag * jnp.sin(li * dt)
    den = lr * lr + li * li
    f_re = ((a_re - 1.0) * lr + a_im * li) / den
    f_im = (a_im * lr - (a_re - 1.0) * li) / den
    return a_re, a_im, f_re * br - f_im * bi, f_re * bi + f_im * br


def _s5_params_fwd(lr16, li16, ldt16, br2, bi2, name):
    def body(a, b, c, d, e, o0, o1, o2, o3):
        r = _s5_disc(a[...], b[...], c[...], d[...], e[...])
        o0[...], o1[...], o2[...], o3[...] = r

    return pl.pallas_call(body, name=name, out_shape=[_sds((512, 64))] * 4)(lr16, li16, ldt16, br2, bi2)


def _s5_params_bwd(lr16, li16, ldt16, br2, bi2, da_re16, da_im16, dbb_re, dbb_im, name):
    def body(a, b, c, d, e, g0, g1, g2, g3, o_lr, o_li, o_dt, o_br, o_bi):
        _, vjp = jax.vjp(_s5_disc, a[...], b[...], c[...], d[...], e[...])
        dlr, dli, dldt, dbr, dbi = vjp((g0[...], g1[...], g2[...], g3[...]))
        grp = lambda z: z.reshape(S5_GROUPS, S5_GROUP, S5_STATE).sum(axis=1)
        o_lr[...] = grp(dlr)
        o_li[...] = grp(dli)
        o_dt[...] = jnp.sum(grp(dldt), axis=1, keepdims=True)
        o_br[...] = dbr
        o_bi[...] = dbi

    return pl.pallas_call(
        body, name=name, out_shape=[_sds((32, 64)), _sds((32, 64)), _sds((32, 1)), _sds((512, 64)), _sds((512, 64))],
    )(lr16, li16, ldt16, br2, bi2, da_re16, da_im16, dbb_re, dbb_im)


def _cmul(ar, ai, br, bi):
    return ar * br - ai * bi, ar * bi + ai * br


def _scan(bre, bim, a_re, a_im, reverse, name, x_re=None, x_im=None, ride=None):
    s, w = bre.shape
    lw = _pick(w, (256, 128))
    nt = s // SUBLANES
    with_da = x_re is not None

    def body(*refs):
        if with_da:
            bre_r, bim_r, are_r, aim_r, xre_r, xim_r, ore_r, oim_r, dare_r, daim_r = refs
        else:
            bre_r, bim_r, are_r, aim_r, ore_r, oim_r = refs
        ar, ai = are_r[...], aim_r[...]
        shp = (SUBLANES, lw)
        row = lax.broadcasted_iota(jnp.int32, shp, 0)
        pows = [(ar, ai)]
        for _ in range(SUBLANES - 1):
            pows.append(_cmul(pows[-1][0], pows[-1][1], ar, ai))
        cm_r, cm_i = jnp.zeros(shp, F32), jnp.zeros(shp, F32)
        for r in range(SUBLANES):
            e = (SUBLANES - 1 - r) if reverse else r
            cm_r = jnp.where(row == r, jnp.broadcast_to(pows[e][0], shp), cm_r)
            cm_i = jnp.where(row == r, jnp.broadcast_to(pows[e][1], shp), cm_i)
        steps = [(1, pows[0]), (2, pows[1]), (4, pows[3])]

        def tile(it, carry):
            if with_da:
                c_r, c_i, acc_r, acc_i = carry
            else:
                c_r, c_i = carry
            r = (nt - 1 - it) if reverse else it
            off = pl.multiple_of(r * SUBLANES, SUBLANES)
            xr, xi = bre_r[pl.ds(off, SUBLANES), :], bim_r[pl.ds(off, SUBLANES), :]
            for sh, (pr, pi) in steps:
                if reverse:
                    keep = row < SUBLANES - sh
                    sr = jnp.where(keep, pltpu.roll(xr, SUBLANES - sh, 0), 0.0)
                    si = jnp.where(keep, pltpu.roll(xi, SUBLANES - sh, 0), 0.0)
                else:
                    keep = row >= sh
                    sr = jnp.where(keep, pltpu.roll(xr, sh, 0), 0.0)
                    si = jnp.where(keep, pltpu.roll(xi, sh, 0), 0.0)
                mr, mi = _cmul(pr, pi, sr, si)
                xr, xi = xr + mr, xi + mi
            mr, mi = _cmul(cm_r, cm_i, c_r, c_i)
            xr, xi = xr + mr, xi + mi
            ore_r[pl.ds(off, SUBLANES), :] = xr
            oim_r[pl.ds(off, SUBLANES), :] = xi
            edge = 0 if reverse else SUBLANES - 1
            c_r, c_i = xr[edge:edge + 1, :], xi[edge:edge + 1, :]
            if not with_da:
                return c_r, c_i
            fr, fi = xre_r[pl.ds(off, SUBLANES), :], xim_r[pl.ds(off, SUBLANES), :]
            poff = pl.multiple_of(jnp.maximum(r - 1, 0) * SUBLANES, SUBLANES)
            live = (r > 0).astype(F32)
            pr_last = xre_r[pl.ds(poff, SUBLANES), :][SUBLANES - 1:SUBLANES, :] * live
            pi_last = xim_r[pl.ds(poff, SUBLANES), :][SUBLANES - 1:SUBLANES, :] * live
            sr = jnp.where(row >= 1, pltpu.roll(fr, 1, 0), jnp.broadcast_to(pr_last, shp))
            si = jnp.where(row >= 1, pltpu.roll(fi, 1, 0), jnp.broadcast_to(pi_last, shp))
            acc_r = acc_r + xr * sr + xi * si
            acc_i = acc_i + xi * sr - xr * si
            return c_r, c_i, acc_r, acc_i

        z1 = jnp.zeros((1, lw), F32)
        if with_da:
            _, _, acc_r, acc_i = lax.fori_loop(0, nt, tile, (z1, z1, jnp.zeros(shp, F32), jnp.zeros(shp, F32)))
            dare_r[...] = jnp.sum(acc_r, axis=0, keepdims=True)
            daim_r[...] = jnp.sum(acc_i, axis=0, keepdims=True)
        else:
            lax.fori_loop(0, nt, tile, (z1, z1))

    col = pl.BlockSpec((s, lw), lambda j: (0, j))
    one = pl.BlockSpec((1, lw), lambda j: (0, j))
    ins = [bre, bim, a_re, a_im] + ([x_re, x_im] if with_da else [])
    in_specs = [col, col, one, one] + ([col, col] if with_da else [])
    out_specs = [col, col] + ([one, one] if with_da else [])
    out_shape = [_sds((s, w)), _sds((s, w))] + ([_sds((1, w)), _sds((1, w))] if with_da else [])
    return _call_with_ride(body, name, (w // lw,), ins, in_specs, out_shape, out_specs, ("parallel",), 56, ride)


S5_SUPER = 4


def _s5_mm(a, w4, mode, name, acc=None):
    s = a.shape[0]
    _, r, c = w4.shape
    wa, wo = (r, c) if mode == "nn" else (c, r)
    tm = _pick(s, (512, 256, 128))
    has_acc = acc is not None

    def body(*refs):
        if has_acc:
            a_ref, w_ref, c_ref, o_ref = refs
        else:
            a_ref, w_ref, o_ref = refs
        res = _nn(a_ref[...], w_ref[0]) if mode == "nn" else _nt(a_ref[...], w_ref[0])
        o_ref[...] = res + c_ref[...] if has_acc else res

    o_spec = pl.BlockSpec((tm, wo), lambda k, i: (i, k))
    ins = [a, w4] + ([acc] if has_acc else [])
    in_specs = [pl.BlockSpec((tm, wa), lambda k, i: (i, k)), pl.BlockSpec((1, r, c), lambda k, i: (k, 0, 0))]
    return pl.pallas_call(
        body, name=name, grid=(S5_SUPER, s // tm), in_specs=in_specs + ([o_spec] if has_acc else []), out_specs=o_spec,
        out_shape=_sds((s, S5_SUPER * wo)),
        compiler_params=pltpu.CompilerParams(dimension_semantics=("parallel", "parallel"), vmem_limit_bytes=48 * 1024 * 1024),
    )(*ins)


def _s5_group_tn(a, b, name):
    s = a.shape[0]
    tk = _pick(s, (512, 256, 128))
    nk = s // tk

    def body(a_ref, b_ref, o_ref, acc_ref):
        k = pl.program_id(1)

        @pl.when(k == 0)
        def _():
            acc_ref[...] = jnp.zeros(acc_ref.shape, F32)

        acc_ref[...] += _tn(a_ref[...], b_ref[...])

        @pl.when(k == nk - 1)
        def _():
            p = acc_ref[...]
            grp = lax.broadcasted_iota(jnp.int32, (LANES, S5_STATE), 0) // S5_GROUP
            out = jnp.zeros((LANES, S5_STATE), F32)
            for j in range(LANES // S5_GROUP):
                out = jnp.where(grp == j, p[:, S5_STATE * j:S5_STATE * (j + 1)], out)
            o_ref[...] = out

    return pl.pallas_call(
        body, name=name, grid=(S5_SUPER, nk),
        in_specs=[pl.BlockSpec((tk, LANES), lambda g, k: (k, g)), pl.BlockSpec((tk, 512), lambda g, k: (k, g))],
        out_specs=pl.BlockSpec((LANES, S5_STATE), lambda g, k: (g, 0)), out_shape=_sds((512, S5_STATE)),
        scratch_shapes=[pltpu.VMEM((LANES, 512), F32)],
        compiler_params=pltpu.CompilerParams(dimension_semantics=("parallel", "arbitrary")),
    )(a, b)


def _s5_seg1(y0, u, d):
    return jax.nn.gelu(y0 + d * u)


def _s5_seg2(z, t, b):
    return z * jax.nn.sigmoid(t + b)


def _s5_post_fwd(y0, seg2, d, wglu, bglu, name):
    s = y0.shape[0]
    ts = _pick(s, (256, 128))

    def fn(ins, outs, accs, scr):
        z = _s5_seg1(ins[0][...], ins[1][...], ins[2][...])
        outs[0][...] = _s5_seg2(z, _nn(z, ins[3][...]), ins[4][...])

    return _stage(name, fn, s // ts,
                  [(y0, _rows(ts, 512)), (seg2, _rows(ts, 512)), (d, _full(d.shape)), (wglu, _full(wglu.shape)),
                   (bglu, _full(bglu.shape))], [(_sds((s, 512)), _rows(ts, 512))])[0]


def _s5_post_bwd(y0, seg2, d, wglu, bglu, dy, name):
    s = y0.shape[0]
    ts = _pick(s, (256, 128))

    def fn(ins, outs, accs, scr):
        y0_r, u_r, d_r, w_r, b_r, dy_r = ins
        z, vjp1 = jax.vjp(_s5_seg1, y0_r[...], u_r[...], d_r[...])
        t = _nn(z, w_r[...])
        _, vjp2 = jax.vjp(_s5_seg2, z, t, b_r[...])
        dz, dt, db = vjp2(dy_r[...])
        accs[0][...] += _tn(z, dt)
        accs[1][...] += db
        dy0, du, dd = vjp1(dz + _nt(dt, w_r[...]))
        accs[2][...] += dd
        outs[0][...] = dy0
        outs[1][...] = du

    return _stage(name, fn, s // ts,
                  [(y0, _rows(ts, 512)), (seg2, _rows(ts, 512)), (d, _full(d.shape)), (wglu, _full(wglu.shape)),
                   (bglu, _full(bglu.shape)), (dy, _rows(ts, 512))],
                  [(_sds((s, 512)), _rows(ts, 512)), (_sds((s, 512)), _rows(ts, 512))],
                  accs=[_sds((512, 512)), _sds((1, 512)), _sds((1, 512))])


def _gate_a(y, g):
    return y * jax.nn.silu(g)


def _gate_m(o0, o1, o2, m0, m1, m2):
    return jax.nn.sigmoid(m0) * o0 + jax.nn.sigmoid(m1) * o1 + jax.nn.sigmoid(m2) * o2


def _assemble(ybuf, o_mla, o_fox, y_s5):
    for h in range(HEADS):
        ybuf[:, 64 * h:64 * (h + 1)] = o_mla[h]
        ybuf[:, 512 + 64 * h:512 + 64 * (h + 1)] = o_fox[h]
    ybuf[:, 1024:1536] = y_s5[...]


def _gate_fwd(o_mla, o_fox, y_s5, seg3, seg4, x, wo, wout, name):
    s = x.shape[0]
    ts = _pick(s, (256, 128))

    def fn(ins, outs, accs, scr):
        om, of, ys, g_r, m_r, x_r, wo_r, wout_r = ins
        ybuf = scr[0]
        _assemble(ybuf, om, of, ys)
        a = _gate_a(ybuf[...], g_r[...])
        o = [_nn(a[:, 512 * b:512 * (b + 1)], wo_r[512 * b:512 * (b + 1), :]) for b in range(3)]
        merged = _gate_m(o[0], o[1], o[2], m_r[:, 0:1024], m_r[:, 1024:2048], m_r[:, 2048:3072])
        outs[0][...] = x_r[...] + _nn(merged, wout_r[...])

    return _stage(name, fn, s // ts,
                  [(o_mla, _heads(ts, 64)), (o_fox, _heads(ts, 64)), (y_s5, _rows(ts, 512)), (seg3, _rows(ts, 1536)),
                   (seg4, _rows(ts, 3072)), (x, _rows(ts, D_MODEL)), (wo, _full(wo.shape)), (wout, _full(wout.shape))],
                  [(_sds((s, D_MODEL)), _rows(ts, D_MODEL))], scratch=[pltpu.VMEM((ts, 1536), F32)])[0]


def _gate_bwd(o_mla, o_fox, y_s5, seg3, seg4, wo, wout, dout, name):
    s = dout.shape[0]
    ts = _pick(s, (128,))

    def fn(ins, outs, accs, scr):
        om, of, ys, g_r, m_r, wo_r, wout_r, dout_r = ins
        do_mla, do_fox, dys, dg_r, dm_r = outs
        dwo, dwout = accs
        ybuf, dabuf = scr
        _assemble(ybuf, om, of, ys)
        a, vjp_a = jax.vjp(_gate_a, ybuf[...], g_r[...])
        o = [_nn(a[:, 512 * b:512 * (b + 1)], wo_r[512 * b:512 * (b + 1), :]) for b in range(3)]
        ms = [m_r[:, 1024 * b:1024 * (b + 1)] for b in range(3)]
        merged, vjp_m = jax.vjp(_gate_m, *o, *ms)
        dout_v = dout_r[...]
        dwout[...] += _tn(merged, dout_v)
        cts = vjp_m(_nt(dout_v, wout_r[...]))
        for b in range(3):
            dm_r[:, 1024 * b:1024 * (b + 1)] = cts[3 + b].astype(BF16)
            dwo[512 * b:512 * (b + 1), :] += _tn(a[:, 512 * b:512 * (b + 1)], cts[b])
            dabuf[:, 512 * b:512 * (b + 1)] = _nt(cts[b], wo_r[512 * b:512 * (b + 1), :])
        dy, dg = vjp_a(dabuf[...])
        dg_r[...] = dg.astype(BF16)
        dys[...] = dy[:, 1024:1536]
        for h in range(HEADS):
            do_mla[h] = dy[:, 64 * h:64 * (h + 1)]
            do_fox[h] = dy[:, 512 + 64 * h:512 + 64 * (h + 1)]

    return _stage(name, fn, s // ts,
                  [(o_mla, _heads(ts, 64)), (o_fox, _heads(ts, 64)), (y_s5, _rows(ts, 512)), (seg3, _rows(ts, 1536)),
                   (seg4, _rows(ts, 3072)), (wo, _full(wo.shape)), (wout, _full(wout.shape)), (dout, _rows(ts, D_MODEL))],
                  [(_sds((HEADS, s, 64)), _heads(ts, 64)), (_sds((HEADS, s, 64)), _heads(ts, 64)), (_sds((s, 512)), _rows(ts, 512)),
                   (_sds((s, 1536), BF16), _rows(ts, 1536)), (_sds((s, 3072), BF16), _rows(ts, 3072))],
                  accs=[_sds(wo.shape), _sds(wout.shape)], scratch=[pltpu.VMEM((ts, 1536), F32), pltpu.VMEM((ts, 1536), F32)],
                  vmem_mb=56)


def _loss_head(y, target, name):
    s = y.shape[0]
    ts = _pick(s, (256, 128))

    def fn(ins, outs, accs, scr):
        e = ins[0][...] - ins[1][...]
        outs[0][...] = e / D_MODEL
        accs[0][...] += 0.5 * jnp.sum(jnp.sum(e * e, axis=1, keepdims=True) / D_MODEL, axis=0, keepdims=True)

    r = _stage(name, fn, s // ts, [(y, _rows(ts, D_MODEL)), (target, _rows(ts, D_MODEL))],
               [(_sds((s, D_MODEL)), _rows(ts, D_MODEL))], accs=[_sds((1, 1))])
    return r[1], r[0]


IN_RANGES = ((0, 384, 0), (384, 416, 448), (416, 1952, 512), (1952, 1960, 384), (1960, IN_WIDTH, 2048))
SHARD_W = IN_WIDTH // N_DEV


def _win_pieces(d):
    lo, hi = SHARD_W * d, SHARD_W * (d + 1)
    out = []
    for a, b, p in IN_RANGES:
        s, e = max(a, lo), min(b, hi)
        while s < e:
            pad = p + (s - a)
            k = max(i for i in range(5) if SEG_OFF[i] <= pad)
            w = min(e - s, SEG_OFF[k] + SEG_W[k] - pad)
            out.append((s - lo, w, k, pad - SEG_OFF[k]))
            s += w
    return out


def _win_pad(g, name):
    _, r, _ = g.shape
    tr = 64

    def body(g_ref, o_ref):
        o_ref[...] = jnp.zeros(o_ref.shape, o_ref.dtype)
        for d in range(N_DEV):
            for dst, w, k, src in _win_pieces(d):
                o_ref[:, SEG_OFF[k] + src:SEG_OFF[k] + src + w] = g_ref[d, :, dst:dst + w]

    return pl.pallas_call(
        body, name=name, grid=(r // tr,), in_specs=[pl.BlockSpec((N_DEV, tr, SHARD_W), lambda i: (0, i, 0))],
        out_specs=pl.BlockSpec((tr, PAD_IN), lambda i: (i, 0)), out_shape=jax.ShapeDtypeStruct((r, PAD_IN), g.dtype),
        compiler_params=pltpu.CompilerParams(dimension_semantics=("parallel",)),
    )(g)


def _win_unpad(dsegs, name):
    r = dsegs[0].shape[0]
    tr = 64

    def body(*refs):
        o_ref = refs[5]
        for d in range(N_DEV):
            for dst, w, k, src in _win_pieces(d):
                o_ref[d, :, dst:dst + w] = refs[k][:, src:src + w]

    return pl.pallas_call(
        body, name=name, grid=(r // tr,), in_specs=[pl.BlockSpec((tr, SEG_W[k]), lambda i: (i, 0)) for k in range(5)],
        out_specs=pl.BlockSpec((N_DEV, tr, SHARD_W), lambda i: (0, i, 0)),
        out_shape=jax.ShapeDtypeStruct((N_DEV, r, SHARD_W), dsegs[0].dtype),
        compiler_params=pltpu.CompilerParams(dimension_semantics=("parallel",)),
    )(*dsegs)


def _pad_wq(w):
    w = w.reshape(MLA_Q_RANK, HEADS, MLA_QK)
    return jnp.pad(w, ((0, 0), (0, 0), (0, LANES - MLA_QK))).reshape(MLA_Q_RANK, HEADS * LANES)


def _unpad_wq(d):
    return d.reshape(MLA_Q_RANK, HEADS, LANES)[:, :, :MLA_QK].reshape(MLA_Q_RANK, HEADS * MLA_QK)


def _pad_wkv(w):
    w = w.reshape(MLA_KV_RANK, HEADS, MLA_NOPE + MLA_V)
    k = jnp.pad(w[:, :, :MLA_NOPE], ((0, 0), (0, 0), (0, LANES - MLA_NOPE))).reshape(MLA_KV_RANK, HEADS * LANES)
    return jnp.concatenate([k, w[:, :, MLA_NOPE:].reshape(MLA_KV_RANK, HEADS * MLA_V)], axis=1)


def _unpad_wkv(d):
    k = d[:, :HEADS * LANES].reshape(MLA_KV_RANK, HEADS, LANES)[:, :, :MLA_NOPE]
    v = d[:, HEADS * LANES:].reshape(MLA_KV_RANK, HEADS, MLA_V)
    return jnp.concatenate([k, v], axis=2).reshape(MLA_KV_RANK, HEADS * (MLA_NOPE + MLA_V))


def _pad_lanes(v, n=LANES):
    return jnp.pad(v, (0, n - v.shape[0])).reshape(1, n)


def _super_blocks(b):
    _, r, c = b.shape
    per = S5_GROUPS // S5_SUPER
    b = b.reshape(S5_SUPER, per, r, c)
    eye = jnp.eye(per, dtype=b.dtype)
    return (b[:, :, :, None, :] * eye[None, :, None, :, None]).reshape(S5_SUPER, per * r, per * c)


def _layer_params(l, w, small):
    p = {k: small[k][l] for k in small}
    q = {}
    q["norm_g"] = p["norm_g"].reshape(1, D_MODEL)
    q["mla_q_a_norm"] = p["mla_q_a_norm"].reshape(1, 256)
    q["mla_kv_a_norm"] = p["mla_kv_a_norm"].reshape(1, 128)
    q["mla_q_norm"] = _pad_lanes(p["mla_q_norm"])
    q["mla_k_norm"] = _pad_lanes(p["mla_k_norm"])
    q["fox_b_f"] = _pad_lanes(p["fox_b_f"])
    q["fox_q_norm"] = p["fox_q_norm"].reshape(1, 64)
    q["fox_k_norm"] = p["fox_k_norm"].reshape(1, 64)
    q["s5_d"] = p["s5_d"].reshape(1, 512)
    q["s5_b_glu"] = p["s5_b_glu"].reshape(1, 512)
    rep = lambda z: jnp.repeat(z, S5_GROUP, axis=0)
    q["lr16"], q["li16"] = rep(p["s5_lambda_re"]), rep(p["s5_lambda_im"])
    q["ldt16"] = rep(jnp.broadcast_to(p["s5_log_dt"][:, None], (S5_GROUPS, S5_STATE)))
    q["br2"] = p["s5_b_re"].transpose(0, 2, 1).reshape(512, 64)
    q["bi2"] = p["s5_b_im"].transpose(0, 2, 1).reshape(512, 64)
    q["c_re"], q["c_im"] = p["s5_c_re"], p["s5_c_im"]
    q["w_in"] = _win_pad(w["w_in"], f"l{l}_w_in_pad")
    q["wq"] = _pad_wq(w["mla_w_q_up"])
    q["wkv"] = _pad_wkv(w["mla_w_kv_up"])
    q["wglu"] = w["s5_w_glu"]
    q["wo"] = w["w_branch_out"]
    q["wout"] = w["w_out"]
    return q


def _layer_fwd(l, x, tabs, q, rides=None):
    rides = rides or {}
    n = lambda s: f"l{l}_{s}"
    sv = {"x": x}
    h = _norm_fwd(x, q["norm_g"], n("norm_fwd"))
    sv["h"] = h
    seg = [_mm(h, q["w_in"], "nn", n(f"proj{k}"), b_cols=(SEG_OFF[k], SEG_W[k])) for k in range(5)]
    sv["seg"] = seg
    mq, mk, mv = _mla_prep_fwd(seg[0], tabs, q, q["wq"], q["wkv"], n("mla_prep_fwd"))
    o_mla, lse_mla = _attn_fwd(mq, mk, mv, 1.0 / math.sqrt(MLA_QK), True, n("mla_attn_fwd"), ride=rides.get("mla_attn_fwd"))
    sv.update(mq=mq, mk=mk, mv=mv, o_mla=o_mla, lse_mla=lse_mla)
    fq, fk, fv, c, ct = _fox_prep_fwd(seg[0], seg[1], q["fox_b_f"], q["fox_q_norm"], q["fox_k_norm"], n("fox_prep_fwd"))
    o_fox, lse_fox = _attn_fwd(fq, fk, fv, 1.0 / math.sqrt(FOX_DIM), False, n("fox_attn_fwd"), c=c, ct=ct,
                               ride=rides.get("fox_attn_fwd"))
    sv.update(fq=fq, fk=fk, fv=fv, c=c, ct=ct, o_fox=o_fox, lse_fox=lse_fox)
    a_re16, a_im16, bb_re, bb_im = _s5_params_fwd(q["lr16"], q["li16"], q["ldt16"], q["br2"], q["bi2"], n("s5_params_fwd"))
    a_re = a_re16.reshape(S5_GROUPS, S5_GROUP, S5_STATE)[:, 0, :].reshape(1, S5_LANES)
    a_im = a_im16.reshape(S5_GROUPS, S5_GROUP, S5_STATE)[:, 0, :].reshape(1, S5_LANES)
    wb_re = _super_blocks(bb_re.reshape(S5_GROUPS, S5_GROUP, S5_STATE)).astype(BF16)
    wb_im = _super_blocks(bb_im.reshape(S5_GROUPS, S5_GROUP, S5_STATE)).astype(BF16)
    wc_re = _super_blocks(q["c_re"].transpose(0, 2, 1)).astype(BF16)
    wc_im = _super_blocks(-q["c_im"].transpose(0, 2, 1)).astype(BF16)
    bu_re = _s5_mm(seg[2], wb_re, "nn", n("s5_bu_re"))
    bu_im = _s5_mm(seg[2], wb_im, "nn", n("s5_bu_im"))
    x_re, x_im = _scan(bu_re, bu_im, a_re, a_im, False, n("s5_scan_fwd"), ride=rides.get("s5_scan_fwd"))
    y0 = _s5_mm(x_im, wc_im, "nn", n("s5_y_im"), acc=_s5_mm(x_re, wc_re, "nn", n("s5_y_re")))
    y_s5 = _s5_post_fwd(y0, seg[2], q["s5_d"], q["wglu"], q["s5_b_glu"], n("s5_post_fwd"))
    sv.update(a_re=a_re, a_im=a_im, wb_re=wb_re, wb_im=wb_im, wc_re=wc_re, wc_im=wc_im, x_re=x_re, x_im=x_im, y0=y0, y_s5=y_s5)
    out = _gate_fwd(o_mla, o_fox, y_s5, seg[3], seg[4], x, q["wo"], q["wout"], n("gate_fwd"))
    return out, sv


def _layer_bwd(l, dout, tabs, q, sv, rides=None, g=None):
    rides = rides or {}
    n = lambda s: f"l{l}_{s}"
    seg = sv["seg"]
    g = {} if g is None else g
    (do_mla, do_fox, dy_s5, dseg3, dseg4, g["wo"], g["wout"]) = _gate_bwd(
        sv["o_mla"], sv["o_fox"], sv["y_s5"], seg[3], seg[4], q["wo"], q["wout"], dout, n("gate_bwd"))
    dy0, du_a, g["wglu"], g["s5_b_glu"], g["s5_d"] = _s5_post_bwd(sv["y0"], seg[2], q["s5_d"], q["wglu"], q["s5_b_glu"], dy_s5,
                                                                 n("s5_post_bwd"))
    dx_re = _s5_mm(dy0, sv["wc_re"], "nt", n("s5_dx_re"))
    dx_im = _s5_mm(dy0, sv["wc_im"], "nt", n("s5_dx_im"))
    g["s5_c_re"] = _s5_group_tn(dy0, sv["x_re"], n("s5_dc_re")).reshape(S5_GROUPS, S5_GROUP, S5_STATE)
    g["s5_c_im"] = -_s5_group_tn(dy0, sv["x_im"], n("s5_dc_im")).reshape(S5_GROUPS, S5_GROUP, S5_STATE)
    g_re, g_im, da_re, da_im = _scan(dx_re, dx_im, sv["a_re"], -sv["a_im"], True, n("s5_scan_bwd"), x_re=sv["x_re"], x_im=sv["x_im"])
    dseg2 = _s5_mm(g_im, sv["wb_im"], "nt", n("s5_du_im"), acc=_s5_mm(g_re, sv["wb_re"], "nt", n("s5_du_re"), acc=du_a))
    dbb_re = _s5_group_tn(seg[2], g_re, n("s5_dbb_re"))
    dbb_im = _s5_group_tn(seg[2], g_im, n("s5_dbb_im"))
    first = (jnp.arange(512) % S5_GROUP == 0).astype(F32)[:, None]
    da_re16 = jnp.repeat(da_re.reshape(S5_GROUPS, S5_STATE), S5_GROUP, axis=0) * first
    da_im16 = jnp.repeat(da_im.reshape(S5_GROUPS, S5_STATE), S5_GROUP, axis=0) * first
    dlr, dli, dldt, dbr2, dbi2 = _s5_params_bwd(q["lr16"], q["li16"], q["ldt16"], q["br2"], q["bi2"], da_re16, da_im16, dbb_re,
                                               dbb_im, n("s5_params_bwd"))
    g["s5_lambda_re"], g["s5_lambda_im"], g["s5_log_dt"] = dlr, dli, dldt.reshape(S5_GROUPS)
    g["s5_b_re"] = dbr2.reshape(S5_GROUPS, S5_GROUP, S5_STATE).transpose(0, 2, 1)
    g["s5_b_im"] = dbi2.reshape(S5_GROUPS, S5_GROUP, S5_STATE).transpose(0, 2, 1)
    dl_fox = _attn_delta(sv["fq"], sv["fk"], sv["fv"], do_fox, sv["lse_fox"], 1.0 / math.sqrt(FOX_DIM), False,
                         n("fox_attn_delta"), c=sv["c"], ct=sv["ct"], ride=rides.get("fox_attn_delta"))
    dfq, dfk, dfv, dck = _attn_bwd(sv["fq"], sv["fk"], sv["fv"], do_fox, sv["lse_fox"], dl_fox, 1.0 / math.sqrt(FOX_DIM), False,
                                   n("fox_attn_bwd"), c=sv["c"], ct=sv["ct"], ride=rides.get("fox_attn_bwd"))
    dseg1, dff, g["fox_q_norm"], g["fox_k_norm"], dbf = _fox_prep_bwd(seg[0], seg[1], q["fox_b_f"], q["fox_q_norm"], q["fox_k_norm"],
                                                                      dfq, dfk, dfv, dck, n("fox_prep_bwd"))
    g["fox_b_f"] = dbf[0, :HEADS]
    dl_mla = _attn_delta(sv["mq"], sv["mk"], sv["mv"], do_mla, sv["lse_mla"], 1.0 / math.sqrt(MLA_QK), True, n("mla_attn_delta"),
                         ride=rides.get("mla_attn_delta"))
    dmq, dmk, dmv = _attn_bwd(sv["mq"], sv["mk"], sv["mv"], do_mla, sv["lse_mla"], dl_mla, 1.0 / math.sqrt(MLA_QK), True,
                              n("mla_attn_bwd"), ride=rides.get("mla_attn_bwd"))
    dseg0, dqan, dkvan, dqn, dkn, g["wq"], g["wkv"] = _mla_prep_bwd(seg[0], tabs, q, q["wq"], q["wkv"], dmq, dmk, dmv, dff,
                                                                   n("mla_prep_bwd"))
    g["mla_q_a_norm"], g["mla_kv_a_norm"] = dqan, dkvan
    g["mla_q_norm"], g["mla_k_norm"] = dqn[0, :MLA_QK], dkn[0, :MLA_QK]
    dsegs = [dseg0, dseg1, dseg2, dseg3, dseg4]
    dh = None
    for k in range(5):
        dh = _mm(dsegs[k], q["w_in"], "nt", n(f"dh{k}"), acc=dh, b_cols=(SEG_OFF[k], SEG_W[k]))
    g["w_in"] = [_mm(sv["h"], dsegs[k], "tn", n(f"dwin{k}")) for k in range(5)]
    dx, g["norm_g"] = _norm_bwd(sv["x"], q["norm_g"], dh, dout, n("norm_bwd"))
    return dx, g


MESH = pl.DeviceIdType.MESH
ANY = pl.BlockSpec(memory_space=pl.ANY)


def _all_gather(blocks, name):
    n = len(blocks)

    def body(*refs):
        x_refs, out_refs = refs[:n], refs[n:2 * n]
        send_sems, recv_sems, local_sems = refs[2 * n:]
        x, y, c = lax.axis_index("x"), lax.axis_index("y"), lax.axis_index("c")
        me, sibling = (x, y, c), (x, y, 1 - c)
        chips = [(1 - x, y), (x, 1 - y), (1 - x, 1 - y)]

        def slot(a, px, py, pc):
            return out_refs[a].at[4 * px + 2 * py + pc]

        def copy(a, k, blk, to, src=None):
            return pltpu.make_async_remote_copy(src_ref=slot(a, *blk) if src is None else src, dst_ref=slot(a, *blk),
                                                send_sem=send_sems.at[7 * a + k], recv_sem=recv_sems.at[7 * a + k],
                                                device_id=to, device_id_type=MESH)

        mine = [pltpu.make_async_copy(x_refs[a], slot(a, *me), local_sems.at[a]) for a in range(n)]
        for cp in mine:
            cp.start()
        first = []
        for j, chip in enumerate(chips):
            first += [copy(a, 1 + j, me, (*chip, c), src=x_refs[a]) for a in range(n)]
        first += [copy(a, 0, me, sibling, src=x_refs[a]) for a in range(n)]
        for cp in first:
            cp.start()
        passed = []
        for j, chip in enumerate(chips):
            for a in range(n):
                copy(a, 1 + j, (*chip, c), me).wait_recv()
                passed.append(copy(a, 4 + j, (*chip, c), sibling))
                passed[-1].start()
        for a in range(n):
            copy(a, 0, sibling, me).wait_recv()
        for j, chip in enumerate(chips):
            for a in range(n):
                copy(a, 4 + j, (*chip, 1 - c), me).wait_recv()
        for cp in first + passed:
            cp.wait_send()
        for cp in mine:
            cp.wait()

    return pl.pallas_call(
        body, name=name, out_shape=[jax.ShapeDtypeStruct((N_DEV,) + b.shape, b.dtype) for b in blocks],
        in_specs=[ANY] * n, out_specs=[ANY] * n,
        scratch_shapes=[pltpu.SemaphoreType.DMA((7 * n,)), pltpu.SemaphoreType.DMA((7 * n,)), pltpu.SemaphoreType.DMA((n,))],
    )(*blocks)


def _place():
    x, y, c = lax.axis_index("x"), lax.axis_index("y"), lax.axis_index("c")
    return x, y, c, [(1 - x, y), (x, 1 - y), (1 - x, 1 - y)]


def _remote(src, dst, send, recv, k, to):
    return pltpu.make_async_remote_copy(src_ref=src, dst_ref=dst, send_sem=send.at[k], recv_sem=recv.at[k], device_id=to,
                                        device_id_type=MESH)


def _plan_gather_ici(blocks):
    n = len(blocks)

    def copies(in_refs, out_refs, send, recv, local):
        x, y, c, chips = _place()
        mine = 4 * x + 2 * y + c
        loc = [pltpu.make_async_copy(in_refs[a], out_refs[a].at[mine], local.at[a]) for a in range(n)]
        rem = [_remote(in_refs[a], out_refs[a].at[mine], send, recv, 3 * a + j, (px, py, c))
               for j, (px, py) in enumerate(chips) for a in range(n)]
        return rem, loc

    return _Plan(blocks, [jax.ShapeDtypeStruct((N_DEV,) + b.shape, b.dtype) for b in blocks], 3 * n, n, copies)


def _plan_gather_d2d(gathered):
    n = len(gathered)

    def copies(in_refs, out_refs, send, recv, local):
        x, y, c, _ = _place()
        rem = [_remote(in_refs[a].at[2 * j + c], out_refs[a].at[2 * j + c], send, recv, 4 * a + j, (x, y, 1 - c))
               for a in range(n) for j in range(4)]
        return rem, []

    return _Plan(gathered, [jax.ShapeDtypeStruct(g.shape, g.dtype) for g in gathered], 4 * n, 0, copies,
                 aliases={a: a for a in range(n)})


def _plan_reduce_sibling(parts):
    n = len(parts)

    def copies(in_refs, out_refs, send, recv, local):
        x, y, c, _ = _place()
        rem = [_remote(in_refs[a].at[2 * j + (1 - c)], out_refs[a].at[j], send, recv, 4 * a + j, (x, y, 1 - c))
               for a in range(n) for j in range(4)]
        return rem, []

    return _Plan(parts, [jax.ShapeDtypeStruct((4,) + p.shape[1:], p.dtype) for p in parts], 4 * n, 0, copies)


def _plan_reduce_chips(sums):
    n = len(sums)

    def copies(in_refs, out_refs, send, recv, local):
        x, y, c, chips = _place()
        mine = 2 * x + y
        loc = [pltpu.make_async_copy(in_refs[a].at[mine], out_refs[a].at[mine], local.at[a]) for a in range(n)]
        rem = [_remote(in_refs[a].at[2 * px + py], out_refs[a].at[mine], send, recv, 3 * a + k, (px, py, c))
               for k, (px, py) in enumerate(chips) for a in range(n)]
        return rem, loc

    return _Plan(sums, [jax.ShapeDtypeStruct(p.shape, p.dtype) for p in sums], 3 * n, n, copies)


def _add_sibling(parts, got, name):
    _, r, cc = parts.shape
    tr = _pick(r, (512, 256, 128, 64, 32, 16))
    c = lax.axis_index("c")

    def body(c_ref, p_ref, g_ref, o_ref):
        o_ref[...] = (p_ref[...] + g_ref[...]).astype(BF16)

    return pl.pallas_call(
        body, name=name, out_shape=jax.ShapeDtypeStruct((4, r, cc), BF16),
        grid_spec=pltpu.PrefetchScalarGridSpec(
            num_scalar_prefetch=1, grid=(4, r // tr),
            in_specs=[pl.BlockSpec((1, tr, cc), lambda j, i, cr: (2 * j + cr[0], i, 0)),
                      pl.BlockSpec((1, tr, cc), lambda j, i, cr: (j, i, 0))],
            out_specs=pl.BlockSpec((1, tr, cc), lambda j, i, cr: (j, i, 0))),
        compiler_params=_vmem(48),
    )(c.reshape(1).astype(jnp.int32), parts, got)


def _sum_leading(parts, name):
    k, r, cc = parts.shape
    tr = _pick(r, (512, 256, 128, 64, 32, 16, 8))

    def body(p_ref, o_ref):
        acc = p_ref[0]
        for j in range(1, k):
            acc = acc + p_ref[j]
        o_ref[...] = acc

    return pl.pallas_call(
        body, name=name, out_shape=jax.ShapeDtypeStruct((r, cc), F32), grid=(r // tr,),
        in_specs=[pl.BlockSpec((k, tr, cc), lambda i: (0, i, 0))], out_specs=pl.BlockSpec((tr, cc), lambda i: (i, 0)),
    )(parts)


def _adamw_math(w, g, m, v):
    nm = ADAM_B1 * m + (1.0 - ADAM_B1) * g
    nv = ADAM_B2 * v + (1.0 - ADAM_B2) * jnp.square(g)
    m_hat = nm / (1.0 - ADAM_B1 ** ADAM_STEP)
    v_hat = nv / (1.0 - ADAM_B2 ** ADAM_STEP)
    return -ADAM_LR * (m_hat / (jnp.sqrt(v_hat) + ADAM_EPS) + ADAM_WD * w), nm, nv


def _adamw_sum(w, contribs, m, v, name, ride=None):
    nl = len(contribs)
    k, r, cc = contribs[0].shape
    tr = _pick(r, (256, 128, 64, 32, 16))
    nb = r // tr

    def body(w_ref, *rest):
        c_refs = rest[:nl]
        m_ref, v_ref, g_ref, d_ref, nm_ref, nv_ref = rest[nl:]
        for li in range(nl):
            @pl.when(pl.program_id(0) == li)
            def _(c_ref=c_refs[li]):
                g = c_ref[0].astype(F32)
                for j in range(1, k):
                    g = g + c_ref[j].astype(F32)
                g_ref[...] = g
                d_ref[...], nm_ref[...], nv_ref[...] = _adamw_math(w_ref[...], g, m_ref[...], v_ref[...])

    spec = pl.BlockSpec((tr, cc), lambda l, i: (l * nb + i, 0))
    cspec = pl.BlockSpec((k, tr, cc), lambda l, i: (0, i, 0))
    return _call_with_ride(body, name, (nl, nb), [w, *contribs, m, v], [spec] + [cspec] * nl + [spec, spec],
                           [jax.ShapeDtypeStruct(w.shape, F32)] * 4, [spec] * 4, ("parallel", "parallel"), 48, ride)


def _adamw_many(ws, gs, ms, vs, name):
    n = len(ws)

    def body(*refs):
        w_r, g_r, m_r, v_r = refs[:n], refs[n:2 * n], refs[2 * n:3 * n], refs[3 * n:4 * n]
        d_r, nm_r, nv_r = refs[4 * n:5 * n], refs[5 * n:6 * n], refs[6 * n:7 * n]
        for a in range(n):
            d_r[a][...], nm_r[a][...], nv_r[a][...] = _adamw_math(w_r[a][...], g_r[a][...], m_r[a][...], v_r[a][...])

    shapes = [jax.ShapeDtypeStruct(w.shape, F32) for w in ws]
    res = pl.pallas_call(body, name=name, out_shape=shapes * 3,
                         compiler_params=pltpu.CompilerParams(vmem_limit_bytes=56 * 1024 * 1024))(*ws, *gs, *ms, *vs)
    return res[:n], res[n:2 * n], res[2 * n:]


def _pack_rows(flat, lanes, row_mult):
    n = flat.shape[-1]
    rows = -(-n // lanes)
    rows = -(-rows // row_mult) * row_mult
    pad = rows * lanes - n
    if pad:
        flat = jnp.pad(flat, [(0, 0)] * (flat.ndim - 1) + [(0, pad)])
    return flat.reshape(flat.shape[:-1] + (rows, lanes))


def _rope_tables(positions):
    inv = 1.0 / (ROPE_THETA ** (jnp.arange(0, MLA_ROPE, 2, dtype=F32) / MLA_ROPE))
    ang = positions.astype(F32)[:, None] * inv
    cos, sin = jnp.cos(ang), jnp.sin(ang)
    s = positions.shape[0]
    z = lambda n: jnp.zeros((s, n), F32)
    c = jnp.concatenate([jnp.ones((s, 64), F32), cos, cos, z(32)], axis=1)
    sa = jnp.concatenate([z(64), -sin, z(48)], axis=1)
    sb = jnp.concatenate([z(80), sin, z(32)], axis=1)
    return c, sa, sb


def _full_weights(gathered):
    full = {}
    for k, g in zip(SHARDED, gathered):
        _, r, c = g.shape
        if k == "w_in":
            full[k] = g
        else:
            full[k] = g.transpose(1, 0, 2).reshape(r, N_DEV * c) if k in COL_SHARDED else g.reshape(N_DEV * r, c)
    return full


EARLY = ("s5_w_glu", "w_branch_out", "w_out")
LATE = ("w_in", "mla_w_q_up", "mla_w_kv_up")


def _owner_major(g, names, tag):
    parts = []
    for k in names:
        if k == "w_in":
            parts.append(_win_unpad(g["w_in"], f"{tag}_w_in_unpad"))
            continue
        big = {"mla_w_q_up": lambda: _unpad_wq(g["wq"]), "mla_w_kv_up": lambda: _unpad_wkv(g["wkv"]), "s5_w_glu": lambda: g["wglu"],
               "w_branch_out": lambda: g["wo"], "w_out": lambda: g["wout"]}[k]()
        r, c = big.shape
        if k in COL_SHARDED:
            parts.append(big.reshape(r, N_DEV, c // N_DEV).transpose(1, 0, 2))
        else:
            parts.append(big.reshape(N_DEV, r // N_DEV, c))
    return parts


def _device_step(x, positions, target, shards, small):
    tabs = _rope_tables(positions)
    box = {}
    full0 = _full_weights(_all_gather(shards[0], "gather_weights_l0"))
    q0 = _layer_params(0, full0, small)
    rides = {
        "mla_attn_fwd": _Ride(lambda: _plan_gather_ici(shards[1][:1]), lambda o: box.update(ici_a=o)),
        "fox_attn_fwd": _Ride(lambda: _join([_plan_gather_ici(shards[1][1:]), _plan_gather_d2d(box["ici_a"])]),
                              lambda o: box.update(ici_b=o[:5], w1_a=o[5:])),
        "s5_scan_fwd": _Ride(lambda: _plan_gather_d2d(box["ici_b"]), lambda o: box.update(w1_b=o)),
    }
    h, sv0 = _layer_fwd(0, x, tabs, q0, rides)
    q1 = _layer_params(1, _full_weights(list(box["w1_a"]) + list(box["w1_b"])), small)
    h, sv1 = _layer_fwd(1, h, tabs, q1)
    loss, d = _loss_head(h, target, "loss_head")
    d, g1 = _layer_bwd(1, d, tabs, q1, sv1)
    parts1 = _owner_major(g1, SHARDED, "l1")

    def chips_plan(names, parts, got, tag):
        return _plan_reduce_chips([_add_sibling(p, g, f"reduce_add_{tag}_{k}") for k, p, g in zip(names, parts, got)])

    g0 = {}
    rides = {
        "fox_attn_delta": _Ride(lambda: _plan_reduce_sibling(parts1), lambda o: box.update(got1=o)),
        "fox_attn_bwd": _Ride(lambda: chips_plan(SHARDED, parts1, box["got1"], "l1"), lambda o: box.update(contribs1=o)),
        "mla_attn_delta": _Ride(lambda: _plan_reduce_sibling(box.setdefault("early0", _owner_major(g0, EARLY, "l0"))),
                                lambda o: box.update(got0e=o)),
        "mla_attn_bwd": _Ride(lambda: chips_plan(EARLY, box["early0"], box["got0e"], "l0"), lambda o: box.update(contribs0e=o)),
    }
    d, _ = _layer_bwd(0, d, tabs, q0, sv0, rides, g0)
    late0 = _owner_major(g0, LATE, "l0")
    got0 = _run_plan(_plan_reduce_sibling(late0), "reduce_sibling_l0")
    contribs0 = list(_run_plan(chips_plan(LATE, late0, got0, "l0"), "reduce_chips_l0")) + list(box["contribs0e"])
    return loss[0, 0], d, [g0, g1], [contribs0, box["contribs1"]]


def kernel(x, positions, norm_g, w_in, mla_q_a_norm, mla_w_q_up, mla_kv_a_norm, mla_w_kv_up, mla_q_norm, mla_k_norm, fox_b_f, fox_q_norm, fox_k_norm, s5_lambda_re, s5_lambda_im, s5_log_dt, s5_b_re, s5_b_im, s5_c_re, s5_c_im, s5_d, s5_w_glu, s5_b_glu, w_branch_out, w_out, loss_target, m_norm_g, m_w_in, m_mla_q_a_norm, m_mla_w_q_up, m_mla_kv_a_norm, m_mla_w_kv_up, m_mla_q_norm, m_mla_k_norm, m_fox_b_f, m_fox_q_norm, m_fox_k_norm, m_s5_lambda_re, m_s5_lambda_im, m_s5_log_dt, m_s5_b_re, m_s5_b_im, m_s5_c_re, m_s5_c_im, m_s5_d, m_s5_w_glu, m_s5_b_glu, m_w_branch_out, m_w_out, v_norm_g, v_w_in, v_mla_q_a_norm, v_mla_w_q_up, v_mla_kv_a_norm, v_mla_w_kv_up, v_mla_q_norm, v_mla_k_norm, v_fox_b_f, v_fox_q_norm, v_fox_k_norm, v_s5_lambda_re, v_s5_lambda_im, v_s5_log_dt, v_s5_b_re, v_s5_b_im, v_s5_c_re, v_s5_c_im, v_s5_d, v_s5_w_glu, v_s5_b_glu, v_w_branch_out, v_w_out):
    env = dict(locals())
    wts = {k: env[k] for k in WEIGHTS}
    mom = {k: env["m_" + k] for k in WEIGHTS}
    var = {k: env["v_" + k] for k in WEIGHTS}

    shards = [[wts[k][l].astype(BF16) for k in SHARDED] for l in range(DEPTH)]
    small = {k: wts[k] for k in SMALL}
    loss, dx, grads, contribs = _device_step(x[0], positions[0], loss_target[0], shards, small)
    loss = lax.psum(loss, ("x", "y", "c"))

    two_d = {k: (wts[k].shape[0] * wts[k].shape[1], wts[k].shape[2]) for k in SHARDED}
    sm = {k: jnp.stack([g[k] for g in grads]).reshape(wts[k].shape) for k in SMALL}
    small_block = _pack_rows(jnp.concatenate([sm[k].reshape(-1) for k in SMALL]), LANES, 256)
    box = {}
    rides = {SHARDED[0]: _Ride(lambda: _plan_gather_ici([small_block]), lambda o: box.update(ici=o)),
             SHARDED[1]: _Ride(lambda: _plan_gather_d2d(box["ici"]), lambda o: box.update(all=o))}
    grad_out, delta_out, m_out, v_out = {}, {}, {}, {}
    for i, k in enumerate(SHARDED):
        shp = wts[k].shape
        res = _adamw_sum(wts[k].reshape(two_d[k]), [contribs[l][i] for l in range(DEPTH)], mom[k].reshape(two_d[k]),
                         var[k].reshape(two_d[k]), f"adamw_{k}", ride=rides.get(k))
        grad_out[k], delta_out[k], m_out[k], v_out[k] = (z.reshape(shp) for z in res)
    g_small = _sum_leading(box["all"][0], "sum_small_grads").reshape(-1)
    off = 0
    for k in SMALL:
        cnt = int(np.prod(wts[k].shape))
        grad_out[k] = g_small[off:off + cnt].reshape(wts[k].shape)
        off += cnt
    flat2 = lambda a: a.reshape(-1, a.shape[-1])
    d_s, m_s, v_s = _adamw_many([flat2(wts[k]) for k in SMALL], [flat2(grad_out[k]) for k in SMALL],
                                [flat2(mom[k]) for k in SMALL], [flat2(var[k]) for k in SMALL], "adamw_small")
    for i, k in enumerate(SMALL):
        delta_out[k], m_out[k], v_out[k] = (z[i].reshape(wts[k].shape) for z in (d_s, m_s, v_s))

    return (loss, dx[None], *[grad_out[k] for k in WEIGHTS], *[delta_out[k] for k in WEIGHTS],
            *[m_out[k] for k in WEIGHTS], *[v_out[k] for k in WEIGHTS])
```

```python
import functools
import math

import jax
import jax.numpy as jnp
import numpy as np
from jax import lax
from jax.experimental import pallas as pl
from jax.experimental.pallas import tpu as pltpu

F32 = jnp.float32
BF16 = jnp.bfloat16

D_MODEL = 1024
DEPTH = 2
CHUNK = 64
EPS = 1e-6
HEADS = 8
MLA_NOPE, MLA_ROPE, MLA_V = 64, 32, 64
MLA_Q_RANK, MLA_KV_RANK = 256, 128
MLA_QK = MLA_NOPE + MLA_ROPE
ROPE_THETA = 10000.0
FOX_DIM = 64
S5_WIDTH, S5_GROUP, S5_GROUPS, S5_STATE = 512, 16, 32, 64
S5_LANES = S5_GROUPS * S5_STATE
IN_WIDTH = 7080
N_DEV = 8
LANES = 128
SUBLANES = 8

ADAM_LR, ADAM_B1, ADAM_B2, ADAM_EPS, ADAM_WD, ADAM_STEP = 0.001, 0.9, 0.999, 1e-08, 0.01, 10

SEG_W = (512, 1536, 512, 1536, 3072)
SEG_OFF = (0, 512, 2048, 2560, 4096)
PAD_IN = 7168
NEG = -1e30

SHARDED = ("w_in", "mla_w_q_up", "mla_w_kv_up", "s5_w_glu", "w_branch_out", "w_out")
COL_SHARDED = ("w_in", "mla_w_q_up", "mla_w_kv_up")
SMALL = ("norm_g", "mla_q_a_norm", "mla_kv_a_norm", "mla_q_norm", "mla_k_norm", "fox_b_f", "fox_q_norm", "fox_k_norm",
         "s5_lambda_re", "s5_lambda_im", "s5_log_dt", "s5_b_re", "s5_b_im", "s5_c_re", "s5_c_im", "s5_d", "s5_b_glu")
WEIGHTS = ("norm_g", "w_in", "mla_q_a_norm", "mla_w_q_up", "mla_kv_a_norm", "mla_w_kv_up", "mla_q_norm", "mla_k_norm",
           "fox_b_f", "fox_q_norm", "fox_k_norm", "s5_lambda_re", "s5_lambda_im", "s5_log_dt", "s5_b_re", "s5_b_im",
           "s5_c_re", "s5_c_im", "s5_d", "s5_w_glu", "s5_b_glu", "w_branch_out", "w_out")
PACK_LANES = 512


def _pick(n, cands):
    for c in cands:
        if n % c == 0:
            return c
    return n


def _vmem(mb):
    return pltpu.CompilerParams(vmem_limit_bytes=mb * 1024 * 1024)


def _dot(a, b, dims):
    return lax.dot_general(a.astype(BF16), b.astype(BF16), (dims, ((), ())), preferred_element_type=F32)


def _nn(a, b):
    return _dot(a, b, ((1,), (0,)))


def _nt(a, b):
    return _dot(a, b, ((1,), (1,)))


def _tn(a, b):
    return _dot(a, b, ((0,), (0,)))


def _rms(x, g, n=None):
    n = x.shape[-1] if n is None else n
    return x * lax.rsqrt(jnp.sum(x * x, axis=-1, keepdims=True) / n + EPS) * g


def _rope(t, c, sa, sb):
    return t * c + pltpu.roll(t, LANES - 16, 1) * sa + pltpu.roll(t, 16, 1) * sb


def _rope_t(d, c, sa, sb):
    return d * c + pltpu.roll(d * sa, 16, 1) + pltpu.roll(d * sb, LANES - 16, 1)


def _mm(a, b, mode, name, acc=None, b_cols=None):
    if mode == "tn":
        kd, m = a.shape
    else:
        m, kd = a.shape
    b_off, b_w = b_cols if b_cols is not None else (0, b.shape[1])
    n = b.shape[0] if mode == "nt" else b_w
    tm, tn, tk = _pick(m, (1024, 512, 256, 128)), _pick(n, (1024, 512, 256, 128)), _pick(kd, (1024, 512, 256, 128))
    nk = kd // tk
    if mode == "tn":
        a_spec = pl.BlockSpec((tk, tm), lambda i, j, k: (k, i))
    else:
        a_spec = pl.BlockSpec((tm, tk), lambda i, j, k: (i, k))
    if mode == "nt":
        assert b_off % tk == 0
        b_spec = pl.BlockSpec((tn, tk), lambda i, j, k: (j, k + b_off // tk))
    else:
        assert b_off % tn == 0
        b_spec = pl.BlockSpec((tk, tn), lambda i, j, k: (k, j + b_off // tn))
    dims = {"nn": ((1,), (0,)), "nt": ((1,), (1,)), "tn": ((0,), (0,))}[mode]
    o_spec = pl.BlockSpec((tm, tn), lambda i, j, k: (i, j))
    has_acc = acc is not None

    def body(*refs):
        if has_acc:
            a_ref, b_ref, c_ref, o_ref = refs
        else:
            a_ref, b_ref, o_ref = refs
        k = pl.program_id(2)
        prod = _dot(a_ref[...], b_ref[...], dims)

        @pl.when(k == 0)
        def _():
            o_ref[...] = prod + c_ref[...] if has_acc else prod

        @pl.when(k > 0)
        def _():
            o_ref[...] += prod

    ins = [a, b] + ([acc] if has_acc else [])
    in_specs = [a_spec, b_spec] + ([o_spec] if has_acc else [])
    return pl.pallas_call(
        body, name=name, grid=(m // tm, n // tn, nk), in_specs=in_specs, out_specs=o_spec,
        out_shape=jax.ShapeDtypeStruct((m, n), F32),
        compiler_params=pltpu.CompilerParams(dimension_semantics=("parallel", "parallel", "arbitrary"),
                                             vmem_limit_bytes=48 * 1024 * 1024),
    )(*ins)


def _stage(name, fn, n_steps, ins, outs, accs=(), scratch=(), vmem_mb=48):
    n_in, n_out, n_acc = len(ins), len(outs), len(accs)

    def body(*refs):
        in_refs = refs[:n_in]
        out_refs = refs[n_in:n_in + n_out]
        acc_refs = refs[n_in + n_out:n_in + n_out + n_acc]
        scr = refs[n_in + n_out + n_acc:]
        if n_acc:
            @pl.when(pl.program_id(0) == 0)
            def _():
                for r in acc_refs:
                    r[...] = jnp.zeros(r.shape, r.dtype)
        fn(in_refs, out_refs, acc_refs, scr)

    acc_specs = [pl.BlockSpec(a.shape, functools.partial(lambda i, nd: (0,) * nd, nd=len(a.shape))) for a in accs]
    res = pl.pallas_call(
        body, name=name, grid=(n_steps,),
        in_specs=[s for _, s in ins], out_specs=[s for _, s in outs] + acc_specs,
        out_shape=[s for s, _ in outs] + list(accs), scratch_shapes=list(scratch),
        compiler_params=pltpu.CompilerParams(dimension_semantics=("arbitrary",),
                                             vmem_limit_bytes=vmem_mb * 1024 * 1024),
    )(*[a for a, _ in ins])
    return res


def _rows(ts, w, j=0):
    return pl.BlockSpec((ts, w), lambda i: (i, j))


def _rows_rev(ts, w, n, j=0):
    return pl.BlockSpec((ts, w), lambda i: (n - 1 - i, j))


def _heads(ts, d):
    return pl.BlockSpec((HEADS, ts, d), lambda i: (0, i, 0))


def _heads_rev(ts, d, n):
    return pl.BlockSpec((HEADS, ts, d), lambda i: (0, n - 1 - i, 0))


def _full(shape):
    nd = len(shape)
    return pl.BlockSpec(tuple(shape), lambda i: (0,) * nd)


def _sds(shape, dtype=F32):
    return jax.ShapeDtypeStruct(tuple(shape), dtype)


def _norm_fwd(x, g, name):
    s = x.shape[0]
    ts = _pick(s, (256, 128))

    def fn(ins, outs, accs, scr):
        outs[0][...] = _rms(ins[0][...], ins[1][...]).astype(BF16)

    return _stage(name, fn, s // ts, [(x, _rows(ts, D_MODEL)), (g, _full(g.shape))],
                  [(_sds((s, D_MODEL), BF16), _rows(ts, D_MODEL))])[0]


def _norm_bwd(x, g, dh, dres, name):
    s = x.shape[0]
    ts = _pick(s, (256, 128))

    def fn(ins, outs, accs, scr):
        _, vjp = jax.vjp(_rms, ins[0][...], ins[1][...])
        dx, dg = vjp(ins[2][...])
        outs[0][...] = dx + ins[3][...]
        accs[0][...] += dg

    r = _stage(name, fn, s // ts,
               [(x, _rows(ts, D_MODEL)), (g, _full(g.shape)), (dh, _rows(ts, D_MODEL)), (dres, _rows(ts, D_MODEL))],
               [(_sds((s, D_MODEL)), _rows(ts, D_MODEL))], accs=[_sds((1, D_MODEL))])
    return r[0], r[1]


def _mla_q(qraw, c, sa, sb, qn):
    return _rms(_rope(qraw, c, sa, sb), qn, MLA_QK)


def _mla_prep_fwd(seg0, tabs, p, wq, wkv, name):
    s = seg0.shape[0]
    ts = _pick(s, (256, 128))

    def fn(ins, outs, accs, scr):
        blk, cos, sa, sb, qan, kvan, qn, kn, wq_r, wkv_r = ins
        b = blk[...]
        cq, ckv, kt = b[:, :256], b[:, 256:384], b[:, 384:512]
        lane = lax.broadcasted_iota(jnp.int32, kt.shape, 1)
        kpe = jnp.where(lane >= 64, kt, 0.0)
        q_raw = _nn(_rms(cq, qan[...]), wq_r[...])
        kv_raw = _nn(_rms(ckv, kvan[...]), wkv_r[...])
        c, a, bb = cos[...], sa[...], sb[...]
        for h in range(HEADS):
            outs[0][h] = _mla_q(q_raw[:, LANES * h:LANES * (h + 1)], c, a, bb, qn[...]).astype(BF16)
            outs[1][h] = _mla_q(kv_raw[:, LANES * h:LANES * (h + 1)] + kpe, c, a, bb, kn[...]).astype(BF16)
            outs[2][h] = kv_raw[:, 1024 + 64 * h:1024 + 64 * (h + 1)].astype(BF16)

    consts = [p["mla_q_a_norm"], p["mla_kv_a_norm"], p["mla_q_norm"], p["mla_k_norm"], wq, wkv]
    return _stage(name, fn, s // ts,
                  [(seg0, _rows(ts, 512))] + [(t, _rows(ts, LANES)) for t in tabs] + [(a, _full(a.shape)) for a in consts],
                  [(_sds((HEADS, s, LANES), BF16), _heads(ts, LANES)), (_sds((HEADS, s, LANES), BF16), _heads(ts, LANES)),
                   (_sds((HEADS, s, 64), BF16), _heads(ts, 64))])


def _mla_prep_bwd(seg0, tabs, p, wq, wkv, dq, dk, dv, dff, name):
    s = seg0.shape[0]
    ts = _pick(s, (256, 128))

    def fn(ins, outs, accs, scr):
        blk, cos, sa, sb, qan, kvan, qn, kn, wq_r, wkv_r, dq_r, dk_r, dv_r, dff_r = ins
        dqan, dkvan, dqn, dkn, dwq, dwkv = accs
        dqraw_s, dkvraw_s = scr
        b = blk[...]
        cq, ckv, kt = b[:, :256], b[:, 256:384], b[:, 384:512]
        lane = lax.broadcasted_iota(jnp.int32, kt.shape, 1)
        kpe = jnp.where(lane >= 64, kt, 0.0)
        cqn, vjp_cq = jax.vjp(_rms, cq, qan[...])
        ckvn, vjp_ckv = jax.vjp(_rms, ckv, kvan[...])
        q_raw = _nn(cqn, wq_r[...])
        kv_raw = _nn(ckvn, wkv_r[...])
        c, a, bb = cos[...], sa[...], sb[...]

        def head_bwd(raw, gain, d):
            t = _rope(raw, c, a, bb)
            _, vjp = jax.vjp(functools.partial(_rms, n=MLA_QK), t, gain)
            dt, dgain = vjp(d)
            return _rope_t(dt, c, a, bb), dgain

        dkpe = jnp.zeros(kt.shape, F32)
        for h in range(HEADS):
            dqh, dg = head_bwd(q_raw[:, LANES * h:LANES * (h + 1)], qn[...], dq_r[h])
            dqn[...] += dg
            dqraw_s[:, LANES * h:LANES * (h + 1)] = dqh
            dkh, dg = head_bwd(kv_raw[:, LANES * h:LANES * (h + 1)] + kpe, kn[...], dk_r[h])
            dkn[...] += dg
            dkvraw_s[:, LANES * h:LANES * (h + 1)] = dkh
            dkpe = dkpe + dkh
            dkvraw_s[:, 1024 + 64 * h:1024 + 64 * (h + 1)] = dv_r[h]
        dq_raw = dqraw_s[...]
        dkv_raw = dkvraw_s[...]
        dwq[...] += _tn(cqn, dq_raw)
        dwkv[...] += _tn(ckvn, dkv_raw)
        dcq, dg = vjp_cq(_nt(dq_raw, wq_r[...]))
        dqan[...] += dg
        dckv, dg = vjp_ckv(_nt(dkv_raw, wkv_r[...]))
        dkvan[...] += dg
        outs[0][:, 0:256] = dcq.astype(BF16)
        outs[0][:, 256:384] = dckv.astype(BF16)
        outs[0][:, 384:512] = (jnp.where(lane >= 64, dkpe, 0.0) + dff_r[...]).astype(BF16)

    consts = [p["mla_q_a_norm"], p["mla_kv_a_norm"], p["mla_q_norm"], p["mla_k_norm"], wq, wkv]
    return _stage(name, fn, s // ts,
                  [(seg0, _rows(ts, 512))] + [(t, _rows(ts, LANES)) for t in tabs] + [(a, _full(a.shape)) for a in consts]
                  + [(dq, _heads(ts, LANES)), (dk, _heads(ts, LANES)), (dv, _heads(ts, 64)), (dff, _rows(ts, LANES))],
                  [(_sds((s, 512), BF16), _rows(ts, 512))],
                  accs=[_sds((1, 256)), _sds((1, 128)), _sds((1, LANES)), _sds((1, LANES)), _sds(wq.shape), _sds(wkv.shape)],
                  scratch=[pltpu.VMEM((ts, 1024), F32), pltpu.VMEM((ts, 1536), F32)])


def _fox_prep_fwd(seg0, seg1, bf, qn, kn, name):
    s = seg0.shape[0]
    ts = _pick(s, (256, 128))
    steps = int(math.log2(ts))

    def fn(ins, outs, accs, scr):
        kt_r, x_r, bf_r, qn_r, kn_r = ins
        carry = scr[0]

        @pl.when(pl.program_id(0) == 0)
        def _():
            carry[...] = jnp.zeros(carry.shape, F32)

        x = x_r[...]
        for h in range(HEADS):
            outs[0][h] = _rms(x[:, 64 * h:64 * (h + 1)], qn_r[...]).astype(BF16)
            outs[1][h] = _rms(x[:, 512 + 64 * h:512 + 64 * (h + 1)], kn_r[...]).astype(BF16)
            outs[2][h] = x[:, 1024 + 64 * h:1024 + 64 * (h + 1)].astype(BF16)
        kt = kt_r[...]
        lane = lax.broadcasted_iota(jnp.int32, kt.shape, 1)
        row = lax.broadcasted_iota(jnp.int32, kt.shape, 0)
        cs = jnp.where(lane < HEADS, jax.nn.log_sigmoid(kt + bf_r[...]), 0.0)
        for k in range(steps):
            sh = 1 << k
            cs = cs + jnp.where(row >= sh, pltpu.roll(cs, sh, 0), 0.0)
        cs = cs + carry[0:1, :]
        outs[3][...] = cs
        outs[4][...] = cs.T[0:HEADS, :]
        carry[0:1, :] = cs[ts - 1:ts, :]

    return _stage(name, fn, s // ts,
                  [(seg0, _rows(ts, LANES, 3)), (seg1, _rows(ts, 1536)), (bf, _full(bf.shape)), (qn, _full(qn.shape)),
                   (kn, _full(kn.shape))],
                  [(_sds((HEADS, s, 64), BF16), _heads(ts, 64)), (_sds((HEADS, s, 64), BF16), _heads(ts, 64)),
                   (_sds((HEADS, s, 64), BF16), _heads(ts, 64)), (_sds((s, LANES)), _rows(ts, LANES)),
                   (_sds((HEADS, s)), pl.BlockSpec((HEADS, ts), lambda i: (0, i)))],
                  scratch=[pltpu.VMEM((SUBLANES, LANES), F32)])


def _fox_prep_bwd(seg0, seg1, bf, qn, kn, dq, dk, dv, dck, name):
    s = seg0.shape[0]
    ts = _pick(s, (256, 128))
    n = s // ts
    steps = int(math.log2(ts))

    def fn(ins, outs, accs, scr):
        kt_r, x_r, bf_r, qn_r, kn_r, dq_r, dk_r, dv_r, dck_r = ins
        dqn, dkn, dbf = accs
        carry, dbuf = scr

        @pl.when(pl.program_id(0) == 0)
        def _():
            carry[...] = jnp.zeros(carry.shape, F32)

        x = x_r[...]
        for h in range(HEADS):
            _, vjp = jax.vjp(_rms, x[:, 64 * h:64 * (h + 1)], qn_r[...])
            d, dg = vjp(dq_r[h])
            dbuf[:, 64 * h:64 * (h + 1)] = d
            dqn[...] += dg
            _, vjp = jax.vjp(_rms, x[:, 512 + 64 * h:512 + 64 * (h + 1)], kn_r[...])
            d, dg = vjp(dk_r[h])
            dbuf[:, 512 + 64 * h:512 + 64 * (h + 1)] = d
            dkn[...] += dg
            dbuf[:, 1024 + 64 * h:1024 + 64 * (h + 1)] = dv_r[h]
        outs[0][...] = dbuf[...].astype(BF16)
        dc = dck_r[...].reshape(HEADS, ts)
        dc = jnp.concatenate([dc, jnp.zeros((LANES - HEADS, ts), F32)], axis=0).T
        row = lax.broadcasted_iota(jnp.int32, dc.shape, 0)
        lane = lax.broadcasted_iota(jnp.int32, dc.shape, 1)
        for k in range(steps):
            sh = 1 << k
            dc = dc + jnp.where(row < ts - sh, pltpu.roll(dc, ts - sh, 0), 0.0)
        dc = dc + carry[0:1, :]
        carry[0:1, :] = dc[0:1, :]
        dff = jnp.where(lane < HEADS, dc * jax.nn.sigmoid(-(kt_r[...] + bf_r[...])), 0.0)
        outs[1][...] = dff
        dbf[...] += jnp.sum(dff, axis=0, keepdims=True)

    return _stage(name, fn, n,
                  [(seg0, _rows_rev(ts, LANES, n, 3)), (seg1, _rows_rev(ts, 1536, n)), (bf, _full(bf.shape)),
                   (qn, _full(qn.shape)), (kn, _full(kn.shape)), (dq, _heads_rev(ts, 64, n)), (dk, _heads_rev(ts, 64, n)),
                   (dv, _heads_rev(ts, 64, n)), (dck, pl.BlockSpec((HEADS, 1, ts), lambda i: (0, 0, n - 1 - i)))],
                  [(_sds((s, 1536), BF16), _rows_rev(ts, 1536, n)), (_sds((s, LANES)), _rows_rev(ts, LANES, n))],
                  accs=[_sds((1, 64)), _sds((1, 64)), _sds((1, LANES))],
                  scratch=[pltpu.VMEM((SUBLANES, LANES), F32), pltpu.VMEM((ts, 1536), F32)])


def _allowed(i, j, t, chunk_causal):
    qpos = i * t + lax.broadcasted_iota(jnp.int32, (t, t), 0)
    kpos = j * t + lax.broadcasted_iota(jnp.int32, (t, t), 1)
    if chunk_causal:
        return (kpos // CHUNK) <= (qpos // CHUNK)
    return kpos <= qpos


def _pick_col(c_blk, h):
    lane = lax.broadcasted_iota(jnp.int32, c_blk.shape, 1)
    return jnp.sum(jnp.where(lane == h, c_blk, 0.0), axis=1, keepdims=True)


class _Ride:
    def __init__(self, make, take):
        self.make, self.take = make, take


class _Plan:
    def __init__(self, ins, out_shapes, n_remote, n_local, copies, aliases=None):
        self.ins, self.out_shapes, self.n_remote, self.n_local = list(ins), list(out_shapes), n_remote, n_local
        self.copies, self.aliases = copies, dict(aliases or {})

    def scratch(self):
        return [pltpu.SemaphoreType.DMA((self.n_remote,)), pltpu.SemaphoreType.DMA((self.n_remote,)),
                pltpu.SemaphoreType.DMA((max(self.n_local, 1),))]

    def start(self, in_refs, out_refs, sems):
        remote, local = self.copies(in_refs, out_refs, *sems)
        for cp in local + remote:
            cp.start()

    def wait(self, in_refs, out_refs, sems):
        remote, local = self.copies(in_refs, out_refs, *sems)
        for cp in remote:
            cp.wait()
        for cp in local:
            cp.wait()


class _Off:
    def __init__(self, ref, off):
        self.ref, self.off, self.at = ref, off, self

    def __getitem__(self, k):
        return self.ref.at[k + self.off]


def _join(plans):
    ins = [a for p in plans for a in p.ins]
    outs = [o for p in plans for o in p.out_shapes]
    aliases, i0, o0 = {}, 0, 0
    for p in plans:
        aliases.update({i0 + i: o0 + o for i, o in p.aliases.items()})
        i0, o0 = i0 + len(p.ins), o0 + len(p.out_shapes)

    def copies(in_refs, out_refs, send, recv, local):
        rem, loc, i0, o0, r0, l0 = [], [], 0, 0, 0, 0
        for p in plans:
            r, l = p.copies(in_refs[i0:i0 + len(p.ins)], out_refs[o0:o0 + len(p.out_shapes)], _Off(send, r0), _Off(recv, r0),
                            _Off(local, l0))
            rem, loc = rem + r, loc + l
            i0, o0, r0, l0 = i0 + len(p.ins), o0 + len(p.out_shapes), r0 + p.n_remote, l0 + p.n_local
        return rem, loc

    return _Plan(ins, outs, sum(p.n_remote for p in plans), sum(p.n_local for p in plans), copies, aliases)


def _call_with_ride(core, name, grid, ins, in_specs, out_shape, out_specs, semantics, vmem_mb, ride, scratch=()):
    n_in, n_out, n_scr = len(ins), len(out_shape), len(scratch)
    if ride is None:
        return pl.pallas_call(
            core, name=name, grid=grid, in_specs=in_specs, out_specs=out_specs, out_shape=out_shape,
            scratch_shapes=list(scratch),
            compiler_params=pltpu.CompilerParams(dimension_semantics=semantics, vmem_limit_bytes=vmem_mb * 1024 * 1024),
        )(*ins)
    plan = ride.make()
    ci, co = len(plan.ins), len(plan.out_shapes)

    def body(*refs):
        c_in = refs[n_in:n_in + ci]
        a_out = refs[n_in + ci:n_in + ci + n_out]
        c_out = refs[n_in + ci + n_out:n_in + ci + n_out + co]
        own = refs[n_in + ci + n_out + co:n_in + ci + n_out + co + n_scr]
        sems = refs[n_in + ci + n_out + co + n_scr:]
        ids = [pl.program_id(d) for d in range(len(grid))]
        first = functools.reduce(jnp.logical_and, [i == 0 for i in ids])
        last = functools.reduce(jnp.logical_and, [i == g - 1 for i, g in zip(ids, grid)])

        @pl.when(first)
        def _():
            plan.start(c_in, c_out, sems)

        core(*refs[:n_in], *a_out, *own)

        @pl.when(last)
        def _():
            plan.wait(c_in, c_out, sems)

    res = pl.pallas_call(
        body, name=name, grid=grid, in_specs=list(in_specs) + [ANY] * ci, out_specs=list(out_specs) + [ANY] * co,
        out_shape=list(out_shape) + plan.out_shapes, scratch_shapes=list(scratch) + plan.scratch(),
        input_output_aliases={n_in + i: n_out + o for i, o in plan.aliases.items()},
        compiler_params=pltpu.CompilerParams(dimension_semantics=("arbitrary",) * len(grid),
                                             vmem_limit_bytes=vmem_mb * 1024 * 1024),
    )(*ins, *plan.ins)
    ride.take(res[n_out:])
    return res[:n_out]


def _run_plan(plan, name):
    ci = len(plan.ins)

    def body(*refs):
        c_in, c_out, sems = refs[:ci], refs[ci:ci + len(plan.out_shapes)], refs[ci + len(plan.out_shapes):]
        plan.start(c_in, c_out, sems)
        plan.wait(c_in, c_out, sems)

    return pl.pallas_call(body, name=name, in_specs=[ANY] * ci, out_specs=[ANY] * len(plan.out_shapes),
                          out_shape=plan.out_shapes, scratch_shapes=plan.scratch(),
                          input_output_aliases=plan.aliases)(*plan.ins)


def _attn_fwd(q, k, v, scale, chunk_causal, name, c=None, ct=None, hps=8, ride=None):
    _, s, dk = q.shape
    dv = v.shape[2]
    t = _pick(s, (256, 128))
    bias = c is not None

    def body(*refs):
        if bias:
            q_ref, k_ref, v_ref, c_ref, ct_ref, o_ref, lse_ref = refs
        else:
            q_ref, k_ref, v_ref, o_ref, lse_ref = refs
        hp, i = pl.program_id(0), pl.program_id(1)
        qb = [q_ref[e] for e in range(hps)]
        cq = [_pick_col(c_ref[...], hp * hps + e) if bias else None for e in range(hps)]

        def step(j, carry, diagonal):
            off = pl.multiple_of(j * t, t)
            out = []
            for e in range(hps):
                m, l, acc = carry[e]
                sc = _nt(qb[e], k_ref[e, pl.ds(off, t), :]) * scale
                if bias:
                    sc = sc + (cq[e] - ct_ref[pl.ds(hp * hps + e, 1), pl.ds(off, t)])
                if diagonal:
                    sc = jnp.where(_allowed(i, j, t, chunk_causal), sc, NEG)
                m_new = jnp.maximum(m, jnp.max(sc, axis=1, keepdims=True))
                pr = jnp.exp(sc - m_new)
                alpha = jnp.exp(m - m_new)
                out.append((m_new, alpha * l + jnp.sum(pr, axis=1, keepdims=True),
                            alpha * acc + _nn(pr, v_ref[e, pl.ds(off, t), :])))
            return tuple(out)

        init = tuple((jnp.full((t, 1), NEG, F32), jnp.zeros((t, 1), F32), jnp.zeros((t, dv), F32)) for _ in range(hps))
        res = step(i, lax.fori_loop(0, i, functools.partial(step, diagonal=False), init), True)
        for e in range(hps):
            m, l, acc = res[e]
            o_ref[e] = acc / l
            lse_ref[e] = m + jnp.log(l)

    ins = [q, k, v] + ([c, ct] if bias else [])
    in_specs = [pl.BlockSpec((hps, t, dk), lambda h, i: (h, i, 0)), pl.BlockSpec((hps, s, dk), lambda h, i: (h, 0, 0)),
                pl.BlockSpec((hps, s, dv), lambda h, i: (h, 0, 0))]
    if bias:
        in_specs += [pl.BlockSpec((t, LANES), lambda h, i: (i, 0)), pl.BlockSpec((HEADS, s), lambda h, i: (0, 0))]
    return _call_with_ride(
        body, name, (HEADS // hps, s // t), ins, in_specs, [_sds((HEADS, s, dv)), _sds((HEADS, s, 1))],
        [pl.BlockSpec((hps, t, dv), lambda h, i: (h, i, 0)), pl.BlockSpec((hps, t, 1), lambda h, i: (h, i, 0))],
        ("parallel", "parallel"), 48, ride)


def _attn_delta(q, k, v, do, lse, scale, chunk_causal, name, c=None, ct=None, hps=4, ride=None):
    _, s, dk = q.shape
    dv = v.shape[2]
    t = _pick(s, (256, 128))
    bias = c is not None

    def body(*refs):
        if bias:
            q_ref, k_ref, v_ref, do_ref, lse_ref, c_ref, ct_ref, dl_ref = refs
        else:
            q_ref, k_ref, v_ref, do_ref, lse_ref, dl_ref = refs
        hp, i = pl.program_id(0), pl.program_id(1)
        qb, dob, lse_b = ([r[e] for e in range(hps)] for r in (q_ref, do_ref, lse_ref))
        cq = [_pick_col(c_ref[...], hp * hps + e) if bias else None for e in range(hps)]

        def step(j, acc, diagonal):
            off = pl.multiple_of(j * t, t)
            out = []
            for e in range(hps):
                sc = _nt(qb[e], k_ref[e, pl.ds(off, t), :]) * scale
                if bias:
                    sc = sc + (cq[e] - ct_ref[pl.ds(hp * hps + e, 1), pl.ds(off, t)])
                pr = jnp.exp(sc - lse_b[e])
                if diagonal:
                    pr = jnp.where(_allowed(i, j, t, chunk_causal), pr, 0.0)
                out.append(acc[e] + jnp.sum(pr * _nt(dob[e], v_ref[e, pl.ds(off, t), :]), axis=1, keepdims=True))
            return tuple(out)

        init = tuple(jnp.zeros((t, 1), F32) for _ in range(hps))
        res = step(i, lax.fori_loop(0, i, functools.partial(step, diagonal=False), init), True)
        for e in range(hps):
            dl_ref[e] = res[e]

    ins = [q, k, v, do, lse] + ([c, ct] if bias else [])
    in_specs = [pl.BlockSpec((hps, t, dk), lambda h, i: (h, i, 0)), pl.BlockSpec((hps, s, dk), lambda h, i: (h, 0, 0)),
                pl.BlockSpec((hps, s, dv), lambda h, i: (h, 0, 0)), pl.BlockSpec((hps, t, dv), lambda h, i: (h, i, 0)),
                pl.BlockSpec((hps, t, 1), lambda h, i: (h, i, 0))]
    if bias:
        in_specs += [pl.BlockSpec((t, LANES), lambda h, i: (i, 0)), pl.BlockSpec((HEADS, s), lambda h, i: (0, 0))]
    return _call_with_ride(body, name, (HEADS // hps, s // t), ins, in_specs, [_sds((HEADS, s, 1))],
                           [pl.BlockSpec((hps, t, 1), lambda h, i: (h, i, 0))], ("parallel", "parallel"), 48, ride)[0]


def _attn_bwd(q, k, v, do, lse, delta, scale, chunk_causal, name, c=None, ct=None, hps=8, ride=None):
    _, s, dk = q.shape
    dv = v.shape[2]
    t = _pick(s, (256, 128))
    n = s // t
    bias = c is not None

    def body(*refs):
        if bias:
            q_ref, k_ref, v_ref, do_ref, lse_ref, dl_ref, c_ref, ct_ref, dq_ref, dk_ref, dv_ref, dck_ref = refs
        else:
            q_ref, k_ref, v_ref, do_ref, lse_ref, dl_ref, dq_ref, dk_ref, dv_ref = refs
        hp, j = pl.program_id(0), pl.program_id(1)

        @pl.when(j == 0)
        def _():
            dq_ref[...] = jnp.zeros(dq_ref.shape, F32)

        kb, vb = [k_ref[e] for e in range(hps)], [v_ref[e] for e in range(hps)]
        joff = pl.multiple_of(j * t, t)
        ck = [ct_ref[pl.ds(hp * hps + e, 1), pl.ds(joff, t)] if bias else None for e in range(hps)]

        def step(i, carry, diagonal):
            off = pl.multiple_of(i * t, t)
            out = []
            for e in range(hps):
                dk_acc, dv_acc, dck_acc = carry[e]
                qb = q_ref[e, pl.ds(off, t), :]
                dob = do_ref[e, pl.ds(off, t), :]
                sc = _nt(qb, kb[e]) * scale
                if bias:
                    sc = sc + (_pick_col(c_ref[pl.ds(off, t), :], hp * hps + e) - ck[e])
                pr = jnp.exp(sc - lse_ref[e, pl.ds(off, t), :])
                if diagonal:
                    pr = jnp.where(_allowed(i, j, t, chunk_causal), pr, 0.0)
                dv_acc = dv_acc + _tn(pr, dob)
                ds = pr * (_nt(dob, vb[e]) - dl_ref[e, pl.ds(off, t), :])
                dq_ref[e, pl.ds(off, t), :] += _nn(ds, kb[e]) * scale
                dk_acc = dk_acc + _tn(ds, qb) * scale
                if bias:
                    dck_acc = dck_acc - jnp.sum(ds, axis=0, keepdims=True)
                out.append((dk_acc, dv_acc, dck_acc))
            return tuple(out)

        zero = tuple((jnp.zeros((t, dk), F32), jnp.zeros((t, dv), F32), jnp.zeros((1, t), F32)) for _ in range(hps))
        res = lax.fori_loop(j + 1, n, functools.partial(step, diagonal=False), step(j, zero, True))
        for e in range(hps):
            dk_ref[e], dv_ref[e] = res[e][0], res[e][1]
            if bias:
                dck_ref[e] = res[e][2]

    ins = [q, k, v, do, lse, delta] + ([c, ct] if bias else [])
    full = lambda d: pl.BlockSpec((hps, s, d), lambda h, j: (h, 0, 0))
    blk = lambda d: pl.BlockSpec((hps, t, d), lambda h, j: (h, j, 0))
    in_specs = [full(dk), blk(dk), blk(dv), full(dv), full(1), full(1)]
    if bias:
        in_specs += [pl.BlockSpec((s, LANES), lambda h, j: (0, 0)), pl.BlockSpec((HEADS, s), lambda h, j: (0, 0))]
    out_specs = [full(dk), blk(dk), blk(dv)]
    out_shape = [_sds((HEADS, s, dk)), _sds((HEADS, s, dk)), _sds((HEADS, s, dv))]
    if bias:
        out_specs.append(pl.BlockSpec((hps, 1, t), lambda h, j: (h, 0, j)))
        out_shape.append(_sds((HEADS, 1, s)))
    return _call_with_ride(body, name, (HEADS // hps, n), ins, in_specs, out_shape, out_specs, ("parallel", "arbitrary"), 56,
                           ride)


def _s5_disc(lr, li, ldt, br, bi):
    dt = jnp.exp(ldt)
    mag = jnp.exp(lr * dt)
    a_re = mag * jnp.cos(li * dt)
    a_im = mag * jnp.sin(li * dt)
    den = lr * lr + li * li
    f_re = ((a_re - 1.0) * lr + a_im * li) / den
    f_im = (a_im * lr - (a_re - 1.0) * li) / den
    return a_re, a_im, f_re * br - f_im * bi, f_re * bi + f_im * br


def _s5_params_fwd(lr16, li16, ldt16, br2, bi2, name):
    def body(a, b, c, d, e, o0, o1, o2, o3):
        r = _s5_disc(a[...], b[...], c[...], d[...], e[...])
        o0[...], o1[...], o2[...], o3[...] = r

    return pl.pallas_call(body, name=name, out_shape=[_sds((512, 64))] * 4)(lr16, li16, ldt16, br2, bi2)


def _s5_params_bwd(lr16, li16, ldt16, br2, bi2, da_re16, da_im16, dbb_re, dbb_im, name):
    def body(a, b, c, d, e, g0, g1, g2, g3, o_lr, o_li, o_dt, o_br, o_bi):
        _, vjp = jax.vjp(_s5_disc, a[...], b[...], c[...], d[...], e[...])
        dlr, dli, dldt, dbr, dbi = vjp((g0[...], g1[...], g2[...], g3[...]))
        grp = lambda z: z.reshape(S5_GROUPS, S5_GROUP, S5_STATE).sum(axis=1)
        o_lr[...] = grp(dlr)
        o_li[...] = grp(dli)
        o_dt[...] = jnp.sum(grp(dldt), axis=1, keepdims=True)
        o_br[...] = dbr
        o_bi[...] = dbi

    return pl.pallas_call(
        body, name=name, out_shape=[_sds((32, 64)), _sds((32, 64)), _sds((32, 1)), _sds((512, 64)), _sds((512, 64))],
    )(lr16, li16, ldt16, br2, bi2, da_re16, da_im16, dbb_re, dbb_im)


def _cmul(ar, ai, br, bi):
    return ar * br - ai * bi, ar * bi + ai * br


def _scan(bre, bim, a_re, a_im, reverse, name, x_re=None, x_im=None, ride=None):
    s, w = bre.shape
    lw = _pick(w, (256, 128))
    nt = s // SUBLANES
    with_da = x_re is not None

    def body(*refs):
        if with_da:
            bre_r, bim_r, are_r, aim_r, xre_r, xim_r, ore_r, oim_r, dare_r, daim_r = refs
        else:
            bre_r, bim_r, are_r, aim_r, ore_r, oim_r = refs
        ar, ai = are_r[...], aim_r[...]
        shp = (SUBLANES, lw)
        row = lax.broadcasted_iota(jnp.int32, shp, 0)
        pows = [(ar, ai)]
        for _ in range(SUBLANES - 1):
            pows.append(_cmul(pows[-1][0], pows[-1][1], ar, ai))
        cm_r, cm_i = jnp.zeros(shp, F32), jnp.zeros(shp, F32)
        for r in range(SUBLANES):
            e = (SUBLANES - 1 - r) if reverse else r
            cm_r = jnp.where(row == r, jnp.broadcast_to(pows[e][0], shp), cm_r)
            cm_i = jnp.where(row == r, jnp.broadcast_to(pows[e][1], shp), cm_i)
        steps = [(1, pows[0]), (2, pows[1]), (4, pows[3])]

        def tile(it, carry):
            if with_da:
                c_r, c_i, acc_r, acc_i = carry
            else:
                c_r, c_i = carry
            r = (nt - 1 - it) if reverse else it
            off = pl.multiple_of(r * SUBLANES, SUBLANES)
            xr, xi = bre_r[pl.ds(off, SUBLANES), :], bim_r[pl.ds(off, SUBLANES), :]
            for sh, (pr, pi) in steps:
                if reverse:
                    keep = row < SUBLANES - sh
                    sr = jnp.where(keep, pltpu.roll(xr, SUBLANES - sh, 0), 0.0)
                    si = jnp.where(keep, pltpu.roll(xi, SUBLANES - sh, 0), 0.0)
                else:
                    keep = row >= sh
                    sr = jnp.where(keep, pltpu.roll(xr, sh, 0), 0.0)
                    si = jnp.where(keep, pltpu.roll(xi, sh, 0), 0.0)
                mr, mi = _cmul(pr, pi, sr, si)
                xr, xi = xr + mr, xi + mi
            mr, mi = _cmul(cm_r, cm_i, c_r, c_i)
            xr, xi = xr + mr, xi + mi
            ore_r[pl.ds(off, SUBLANES), :] = xr
            oim_r[pl.ds(off, SUBLANES), :] = xi
            edge = 0 if reverse else SUBLANES - 1
            c_r, c_i = xr[edge:edge + 1, :], xi[edge:edge + 1, :]
            if not with_da:
                return c_r, c_i
            fr, fi = xre_r[pl.ds(off, SUBLANES), :], xim_r[pl.ds(off, SUBLANES), :]
            poff = pl.multiple_of(jnp.maximum(r - 1, 0) * SUBLANES, SUBLANES)
            live = (r > 0).astype(F32)
            pr_last = xre_r[pl.ds(poff, SUBLANES), :][SUBLANES - 1:SUBLANES, :] * live
            pi_last = xim_r[pl.ds(poff, SUBLANES), :][SUBLANES - 1:SUBLANES, :] * live
            sr = jnp.where(row >= 1, pltpu.roll(fr, 1, 0), jnp.broadcast_to(pr_last, shp))
            si = jnp.where(row >= 1, pltpu.roll(fi, 1, 0), jnp.broadcast_to(pi_last, shp))
            acc_r = acc_r + xr * sr + xi * si
            acc_i = acc_i + xi * sr - xr * si
            return c_r, c_i, acc_r, acc_i

        z1 = jnp.zeros((1, lw), F32)
        if with_da:
            _, _, acc_r, acc_i = lax.fori_loop(0, nt, tile, (z1, z1, jnp.zeros(shp, F32), jnp.zeros(shp, F32)))
            dare_r[...] = jnp.sum(acc_r, axis=0, keepdims=True)
            daim_r[...] = jnp.sum(acc_i, axis=0, keepdims=True)
        else:
            lax.fori_loop(0, nt, tile, (z1, z1))

    col = pl.BlockSpec((s, lw), lambda j: (0, j))
    one = pl.BlockSpec((1, lw), lambda j: (0, j))
    ins = [bre, bim, a_re, a_im] + ([x_re, x_im] if with_da else [])
    in_specs = [col, col, one, one] + ([col, col] if with_da else [])
    out_specs = [col, col] + ([one, one] if with_da else [])
    out_shape = [_sds((s, w)), _sds((s, w))] + ([_sds((1, w)), _sds((1, w))] if with_da else [])
    return _call_with_ride(body, name, (w // lw,), ins, in_specs, out_shape, out_specs, ("parallel",), 56, ride)


S5_SUPER = 4


def _s5_mm(a, w4, mode, name, acc=None):
    s = a.shape[0]
    _, r, c = w4.shape
    wa, wo = (r, c) if mode == "nn" else (c, r)
    tm = _pick(s, (512, 256, 128))
    has_acc = acc is not None

    def body(*refs):
        if has_acc:
            a_ref, w_ref, c_ref, o_ref = refs
        else:
            a_ref, w_ref, o_ref = refs
        res = _nn(a_ref[...], w_ref[0]) if mode == "nn" else _nt(a_ref[...], w_ref[0])
        o_ref[...] = res + c_ref[...] if has_acc else res

    o_spec = pl.BlockSpec((tm, wo), lambda k, i: (i, k))
    ins = [a, w4] + ([acc] if has_acc else [])
    in_specs = [pl.BlockSpec((tm, wa), lambda k, i: (i, k)), pl.BlockSpec((1, r, c), lambda k, i: (k, 0, 0))]
    return pl.pallas_call(
        body, name=name, grid=(S5_SUPER, s // tm), in_specs=in_specs + ([o_spec] if has_acc else []), out_specs=o_spec,
        out_shape=_sds((s, S5_SUPER * wo)),
        compiler_params=pltpu.CompilerParams(dimension_semantics=("parallel", "parallel"), vmem_limit_bytes=48 * 1024 * 1024),
    )(*ins)


def _s5_group_tn(a, b, name):
    s = a.shape[0]
    tk = _pick(s, (512, 256, 128))
    nk = s // tk

    def body(a_ref, b_ref, o_ref, acc_ref):
        k = pl.program_id(1)

        @pl.when(k == 0)
        def _():
            acc_ref[...] = jnp.zeros(acc_ref.shape, F32)

        acc_ref[...] += _tn(a_ref[...], b_ref[...])

        @pl.when(k == nk - 1)
        def _():
            p = acc_ref[...]
            grp = lax.broadcasted_iota(jnp.int32, (LANES, S5_STATE), 0) // S5_GROUP
            out = jnp.zeros((LANES, S5_STATE), F32)
            for j in range(LANES // S5_GROUP):
                out = jnp.where(grp == j, p[:, S5_STATE * j:S5_STATE * (j + 1)], out)
            o_ref[...] = out

    return pl.pallas_call(
        body, name=name, grid=(S5_SUPER, nk),
        in_specs=[pl.BlockSpec((tk, LANES), lambda g, k: (k, g)), pl.BlockSpec((tk, 512), lambda g, k: (k, g))],
        out_specs=pl.BlockSpec((LANES, S5_STATE), lambda g, k: (g, 0)), out_shape=_sds((512, S5_STATE)),
        scratch_shapes=[pltpu.VMEM((LANES, 512), F32)],
        compiler_params=pltpu.CompilerParams(dimension_semantics=("parallel", "arbitrary")),
    )(a, b)


def _scan_loop(bre_r, bim_r, ar, ai, ore_r, oim_r, reverse, xre_r=None, xim_r=None):
    s, lw = bre_r.shape
    nt = s // SUBLANES
    with_da = xre_r is not None
    shp = (SUBLANES, lw)
    row = lax.broadcasted_iota(jnp.int32, shp, 0)
    pows = [(ar, ai)]
    for _ in range(SUBLANES - 1):
        pows.append(_cmul(pows[-1][0], pows[-1][1], ar, ai))
    cm_r, cm_i = jnp.zeros(shp, F32), jnp.zeros(shp, F32)
    for r in range(SUBLANES):
        e = (SUBLANES - 1 - r) if reverse else r
        cm_r = jnp.where(row == r, jnp.broadcast_to(pows[e][0], shp), cm_r)
        cm_i = jnp.where(row == r, jnp.broadcast_to(pows[e][1], shp), cm_i)
    steps = [(1, pows[0]), (2, pows[1]), (4, pows[3])]

    def tile(it, carry):
        if with_da:
            c_r, c_i, acc_r, acc_i = carry
        else:
            c_r, c_i = carry
        r = (nt - 1 - it) if reverse else it
        off = pl.multiple_of(r * SUBLANES, SUBLANES)
        xr, xi = bre_r[pl.ds(off, SUBLANES), :], bim_r[pl.ds(off, SUBLANES), :]
        for sh, (pr, pi) in steps:
            if reverse:
                keep = row < SUBLANES - sh
                sr = jnp.where(keep, pltpu.roll(xr, SUBLANES - sh, 0), 0.0)
                si = jnp.where(keep, pltpu.roll(xi, SUBLANES - sh, 0), 0.0)
            else:
                keep = row >= sh
                sr = jnp.where(keep, pltpu.roll(xr, sh, 0), 0.0)
                si = jnp.where(keep, pltpu.roll(xi, sh, 0), 0.0)
            mr, mi = _cmul(pr, pi, sr, si)
            xr, xi = xr + mr, xi + mi
        mr, mi = _cmul(cm_r, cm_i, c_r, c_i)
        xr, xi = xr + mr, xi + mi
        ore_r[pl.ds(off, SUBLANES), :] = xr
        oim_r[pl.ds(off, SUBLANES), :] = xi
        edge = 0 if reverse else SUBLANES - 1
        c_r, c_i = xr[edge:edge + 1, :], xi[edge:edge + 1, :]
        if not with_da:
            return c_r, c_i
        fr, fi = xre_r[pl.ds(off, SUBLANES), :], xim_r[pl.ds(off, SUBLANES), :]
        poff = pl.multiple_of(jnp.maximum(r - 1, 0) * SUBLANES, SUBLANES)
        live = (r > 0).astype(F32)
        pr_last = xre_r[pl.ds(poff, SUBLANES), :][SUBLANES - 1:SUBLANES, :] * live
        pi_last = xim_r[pl.ds(poff, SUBLANES), :][SUBLANES - 1:SUBLANES, :] * live
        sr = jnp.where(row >= 1, pltpu.roll(fr, 1, 0), jnp.broadcast_to(pr_last, shp))
        si = jnp.where(row >= 1, pltpu.roll(fi, 1, 0), jnp.broadcast_to(pi_last, shp))
        return c_r, c_i, acc_r + xr * sr + xi * si, acc_i + xi * sr - xr * si

    z1 = jnp.zeros((1, lw), F32)
    if not with_da:
        lax.fori_loop(0, nt, tile, (z1, z1))
        return None
    _, _, acc_r, acc_i = lax.fori_loop(0, nt, tile, (z1, z1, jnp.zeros(shp, F32), jnp.zeros(shp, F32)))
    return jnp.sum(acc_r, axis=0, keepdims=True), jnp.sum(acc_i, axis=0, keepdims=True)


S5_ROWS = 512


def _group_compact(p):
    grp = lax.broadcasted_iota(jnp.int32, (LANES, S5_STATE), 0) // S5_GROUP
    out = jnp.zeros((LANES, S5_STATE), F32)
    for j in range(LANES // S5_GROUP):
        out = jnp.where(grp == j, p[:, S5_STATE * j:S5_STATE * (j + 1)], out)
    return out


def _s5_core_fwd(u, wb_re, wb_im, wc_re, wc_im, a_re, a_im, name, ride=None):
    s = u.shape[0]
    lw = S5_LANES // S5_SUPER
    rows = _pick(s, (S5_ROWS, 256, 128))

    def body(u_r, wbr, wbi, wcr, wci, are_r, aim_r, xre_r, xim_r, y_r, bre_s, bim_s):
        for r0 in range(0, s, rows):
            ub = u_r[r0:r0 + rows, :]
            bre_s[r0:r0 + rows, :] = _nn(ub, wbr[0])
            bim_s[r0:r0 + rows, :] = _nn(ub, wbi[0])
        _scan_loop(bre_s, bim_s, are_r[...], aim_r[...], xre_r, xim_r, False)
        for r0 in range(0, s, rows):
            y_r[r0:r0 + rows, :] = _nn(xre_r[r0:r0 + rows, :], wcr[0]) + _nn(xim_r[r0:r0 + rows, :], wci[0])

    nar = pl.BlockSpec((s, LANES), lambda k: (0, k))
    wide = pl.BlockSpec((s, lw), lambda k: (0, k))
    one = pl.BlockSpec((1, lw), lambda k: (0, k))
    wb = pl.BlockSpec((1, LANES, lw), lambda k: (k, 0, 0))
    wc = pl.BlockSpec((1, lw, LANES), lambda k: (k, 0, 0))
    return _call_with_ride(body, name, (S5_SUPER,), [u, wb_re, wb_im, wc_re, wc_im, a_re, a_im], [nar, wb, wb, wc, wc, one, one],
                           [_sds((s, S5_LANES)), _sds((s, S5_LANES)), _sds((s, S5_WIDTH))], [wide, wide, nar], ("parallel",), 56,
                           ride, scratch=[pltpu.VMEM((s, lw), F32), pltpu.VMEM((s, lw), F32)])


def _s5_core_bwd(dy0, u, du_a, x_re, x_im, wb_re, wb_im, wc_re, wc_im, a_re, a_im_neg, name):
    s = u.shape[0]
    lw = S5_LANES // S5_SUPER
    rows = _pick(s, (S5_ROWS, 256, 128))

    def body(dy_r, u_r, dua_r, xre_r, xim_r, wbr, wbi, wcr, wci, are_r, aim_r, du_r, dare_r, daim_r, dbr_r, dbi_r, dcr_r, dci_r,
             dre_s, dim_s, gre_s, gim_s):
        for r0 in range(0, s, rows):
            dyb = dy_r[r0:r0 + rows, :]
            dre_s[r0:r0 + rows, :] = _nt(dyb, wcr[0])
            dim_s[r0:r0 + rows, :] = _nt(dyb, wci[0])
        dare_r[...], daim_r[...] = _scan_loop(dre_s, dim_s, are_r[...], aim_r[...], gre_s, gim_s, True, xre_r, xim_r)
        acc = [jnp.zeros((LANES, lw), F32) for _ in range(4)]
        for r0 in range(0, s, rows):
            sl = slice(r0, r0 + rows)
            gr, gi, ub, dyb = gre_s[sl, :], gim_s[sl, :], u_r[sl, :], dy_r[sl, :]
            du_r[sl, :] = dua_r[sl, :] + _nt(gr, wbr[0]) + _nt(gi, wbi[0])
            acc = [acc[0] + _tn(ub, gr), acc[1] + _tn(ub, gi), acc[2] + _tn(dyb, xre_r[sl, :]), acc[3] + _tn(dyb, xim_r[sl, :])]
        dbr_r[...], dbi_r[...], dcr_r[...], dci_r[...] = (_group_compact(a) for a in acc)

    nar = pl.BlockSpec((s, LANES), lambda k: (0, k))
    wide = pl.BlockSpec((s, lw), lambda k: (0, k))
    one = pl.BlockSpec((1, lw), lambda k: (0, k))
    wb = pl.BlockSpec((1, LANES, lw), lambda k: (k, 0, 0))
    wc = pl.BlockSpec((1, lw, LANES), lambda k: (k, 0, 0))
    blk = pl.BlockSpec((LANES, S5_STATE), lambda k: (k, 0))
    return _call_with_ride(
        body, name, (S5_SUPER,), [dy0, u, du_a, x_re, x_im, wb_re, wb_im, wc_re, wc_im, a_re, a_im_neg],
        [nar, nar, nar, wide, wide, wb, wb, wc, wc, one, one],
        [_sds((s, S5_WIDTH)), _sds((1, S5_LANES)), _sds((1, S5_LANES))] + [_sds((S5_WIDTH, S5_STATE))] * 4,
        [nar, one, one, blk, blk, blk, blk], ("parallel",), 60, None, scratch=[pltpu.VMEM((s, lw), F32)] * 4)


def _s5_seg1(y0, u, d):
    return jax.nn.gelu(y0 + d * u)


def _s5_seg2(z, t, b):
    return z * jax.nn.sigmoid(t + b)


def _s5_post_fwd(y0, seg2, d, wglu, bglu, name):
    s = y0.shape[0]
    ts = _pick(s, (256, 128))

    def fn(ins, outs, accs, scr):
        z = _s5_seg1(ins[0][...], ins[1][...], ins[2][...])
        outs[0][...] = _s5_seg2(z, _nn(z, ins[3][...]), ins[4][...])

    return _stage(name, fn, s // ts,
                  [(y0, _rows(ts, 512)), (seg2, _rows(ts, 512)), (d, _full(d.shape)), (wglu, _full(wglu.shape)),
                   (bglu, _full(bglu.shape))], [(_sds((s, 512)), _rows(ts, 512))])[0]


def _s5_post_bwd(y0, seg2, d, wglu, bglu, dy, name):
    s = y0.shape[0]
    ts = _pick(s, (256, 128))

    def fn(ins, outs, accs, scr):
        y0_r, u_r, d_r, w_r, b_r, dy_r = ins
        z, vjp1 = jax.vjp(_s5_seg1, y0_r[...], u_r[...], d_r[...])
        t = _nn(z, w_r[...])
        _, vjp2 = jax.vjp(_s5_seg2, z, t, b_r[...])
        dz, dt, db = vjp2(dy_r[...])
        accs[0][...] += _tn(z, dt)
        accs[1][...] += db
        dy0, du, dd = vjp1(dz + _nt(dt, w_r[...]))
        accs[2][...] += dd
        outs[0][...] = dy0
        outs[1][...] = du

    return _stage(name, fn, s // ts,
                  [(y0, _rows(ts, 512)), (seg2, _rows(ts, 512)), (d, _full(d.shape)), (wglu, _full(wglu.shape)),
                   (bglu, _full(bglu.shape)), (dy, _rows(ts, 512))],
                  [(_sds((s, 512)), _rows(ts, 512)), (_sds((s, 512)), _rows(ts, 512))],
                  accs=[_sds((512, 512)), _sds((1, 512)), _sds((1, 512))])


def _gate_a(y, g):
    return y * jax.nn.silu(g)


def _gate_m(o0, o1, o2, m0, m1, m2):
    return jax.nn.sigmoid(m0) * o0 + jax.nn.sigmoid(m1) * o1 + jax.nn.sigmoid(m2) * o2


def _assemble(ybuf, o_mla, o_fox, y_s5):
    for h in range(HEADS):
        ybuf[:, 64 * h:64 * (h + 1)] = o_mla[h]
        ybuf[:, 512 + 64 * h:512 + 64 * (h + 1)] = o_fox[h]
    ybuf[:, 1024:1536] = y_s5[...]


def _gate_fwd(o_mla, o_fox, y_s5, seg3, seg4, x, wo, wout, name):
    s = x.shape[0]
    ts = _pick(s, (256, 128))

    def fn(ins, outs, accs, scr):
        om, of, ys, g_r, m_r, x_r, wo_r, wout_r = ins
        ybuf = scr[0]
        _assemble(ybuf, om, of, ys)
        a = _gate_a(ybuf[...], g_r[...])
        o = [_nn(a[:, 512 * b:512 * (b + 1)], wo_r[512 * b:512 * (b + 1), :]) for b in range(3)]
        merged = _gate_m(o[0], o[1], o[2], m_r[:, 0:1024], m_r[:, 1024:2048], m_r[:, 2048:3072])
        outs[0][...] = x_r[...] + _nn(merged, wout_r[...])

    return _stage(name, fn, s // ts,
                  [(o_mla, _heads(ts, 64)), (o_fox, _heads(ts, 64)), (y_s5, _rows(ts, 512)), (seg3, _rows(ts, 1536)),
                   (seg4, _rows(ts, 3072)), (x, _rows(ts, D_MODEL)), (wo, _full(wo.shape)), (wout, _full(wout.shape))],
                  [(_sds((s, D_MODEL)), _rows(ts, D_MODEL))], scratch=[pltpu.VMEM((ts, 1536), F32)])[0]


def _gate_bwd(o_mla, o_fox, y_s5, seg3, seg4, wo, wout, dout, name):
    s = dout.shape[0]
    ts = _pick(s, (128,))

    def fn(ins, outs, accs, scr):
        om, of, ys, g_r, m_r, wo_r, wout_r, dout_r = ins
        do_mla, do_fox, dys, dg_r, dm_r = outs
        dwo, dwout = accs
        ybuf, dabuf = scr
        _assemble(ybuf, om, of, ys)
        a, vjp_a = jax.vjp(_gate_a, ybuf[...], g_r[...])
        o = [_nn(a[:, 512 * b:512 * (b + 1)], wo_r[512 * b:512 * (b + 1), :]) for b in range(3)]
        ms = [m_r[:, 1024 * b:1024 * (b + 1)] for b in range(3)]
        merged, vjp_m = jax.vjp(_gate_m, *o, *ms)
        dout_v = dout_r[...]
        dwout[...] += _tn(merged, dout_v)
        cts = vjp_m(_nt(dout_v, wout_r[...]))
        for b in range(3):
            dm_r[:, 1024 * b:1024 * (b + 1)] = cts[3 + b].astype(BF16)
            dwo[512 * b:512 * (b + 1), :] += _tn(a[:, 512 * b:512 * (b + 1)], cts[b])
            dabuf[:, 512 * b:512 * (b + 1)] = _nt(cts[b], wo_r[512 * b:512 * (b + 1), :])
        dy, dg = vjp_a(dabuf[...])
        dg_r[...] = dg.astype(BF16)
        dys[...] = dy[:, 1024:1536]
        for h in range(HEADS):
            do_mla[h] = dy[:, 64 * h:64 * (h + 1)]
            do_fox[h] = dy[:, 512 + 64 * h:512 + 64 * (h + 1)]

    return _stage(name, fn, s // ts,
                  [(o_mla, _heads(ts, 64)), (o_fox, _heads(ts, 64)), (y_s5, _rows(ts, 512)), (seg3, _rows(ts, 1536)),
                   (seg4, _rows(ts, 3072)), (wo, _full(wo.shape)), (wout, _full(wout.shape)), (dout, _rows(ts, D_MODEL))],
                  [(_sds((HEADS, s, 64)), _heads(ts, 64)), (_sds((HEADS, s, 64)), _heads(ts, 64)), (_sds((s, 512)), _rows(ts, 512)),
                   (_sds((s, 1536), BF16), _rows(ts, 1536)), (_sds((s, 3072), BF16), _rows(ts, 3072))],
                  accs=[_sds(wo.shape), _sds(wout.shape)], scratch=[pltpu.VMEM((ts, 1536), F32), pltpu.VMEM((ts, 1536), F32)],
                  vmem_mb=56)


def _loss_head(y, target, name):
    s = y.shape[0]
    ts = _pick(s, (256, 128))

    def fn(ins, outs, accs, scr):
        e = ins[0][...] - ins[1][...]
        outs[0][...] = e / D_MODEL
        accs[0][...] += 0.5 * jnp.sum(jnp.sum(e * e, axis=1, keepdims=True) / D_MODEL, axis=0, keepdims=True)

    r = _stage(name, fn, s // ts, [(y, _rows(ts, D_MODEL)), (target, _rows(ts, D_MODEL))],
               [(_sds((s, D_MODEL)), _rows(ts, D_MODEL))], accs=[_sds((1, 1))])
    return r[1], r[0]


IN_RANGES = ((0, 384, 0), (384, 416, 448), (416, 1952, 512), (1952, 1960, 384), (1960, IN_WIDTH, 2048))
SHARD_W = IN_WIDTH // N_DEV


def _win_pieces(d):
    lo, hi = SHARD_W * d, SHARD_W * (d + 1)
    out = []
    for a, b, p in IN_RANGES:
        s, e = max(a, lo), min(b, hi)
        while s < e:
            pad = p + (s - a)
            k = max(i for i in range(5) if SEG_OFF[i] <= pad)
            w = min(e - s, SEG_OFF[k] + SEG_W[k] - pad)
            out.append((s - lo, w, k, pad - SEG_OFF[k]))
            s += w
    return out


def _win_pad(g, name):
    _, r, _ = g.shape
    tr = 64

    def body(g_ref, o_ref):
        o_ref[...] = jnp.zeros(o_ref.shape, o_ref.dtype)
        for d in range(N_DEV):
            for dst, w, k, src in _win_pieces(d):
                o_ref[:, SEG_OFF[k] + src:SEG_OFF[k] + src + w] = g_ref[d, :, dst:dst + w]

    return pl.pallas_call(
        body, name=name, grid=(r // tr,), in_specs=[pl.BlockSpec((N_DEV, tr, SHARD_W), lambda i: (0, i, 0))],
        out_specs=pl.BlockSpec((tr, PAD_IN), lambda i: (i, 0)), out_shape=jax.ShapeDtypeStruct((r, PAD_IN), g.dtype),
        compiler_params=pltpu.CompilerParams(dimension_semantics=("parallel",)),
    )(g)


def _win_unpad(dsegs, name):
    r = dsegs[0].shape[0]
    tr = 64

    def body(*refs):
        o_ref = refs[5]
        for d in range(N_DEV):
            for dst, w, k, src in _win_pieces(d):
                o_ref[d, :, dst:dst + w] = refs[k][:, src:src + w]

    return pl.pallas_call(
        body, name=name, grid=(r // tr,), in_specs=[pl.BlockSpec((tr, SEG_W[k]), lambda i: (i, 0)) for k in range(5)],
        out_specs=pl.BlockSpec((N_DEV, tr, SHARD_W), lambda i: (0, i, 0)),
        out_shape=jax.ShapeDtypeStruct((N_DEV, r, SHARD_W), dsegs[0].dtype),
        compiler_params=pltpu.CompilerParams(dimension_semantics=("parallel",)),
    )(*dsegs)


def _pad_wq(w):
    w = w.reshape(MLA_Q_RANK, HEADS, MLA_QK)
    return jnp.pad(w, ((0, 0), (0, 0), (0, LANES - MLA_QK))).reshape(MLA_Q_RANK, HEADS * LANES)


def _unpad_wq(d):
    return d.reshape(MLA_Q_RANK, HEADS, LANES)[:, :, :MLA_QK].reshape(MLA_Q_RANK, HEADS * MLA_QK)


def _pad_wkv(w):
    w = w.reshape(MLA_KV_RANK, HEADS, MLA_NOPE + MLA_V)
    k = jnp.pad(w[:, :, :MLA_NOPE], ((0, 0), (0, 0), (0, LANES - MLA_NOPE))).reshape(MLA_KV_RANK, HEADS * LANES)
    return jnp.concatenate([k, w[:, :, MLA_NOPE:].reshape(MLA_KV_RANK, HEADS * MLA_V)], axis=1)


def _unpad_wkv(d):
    k = d[:, :HEADS * LANES].reshape(MLA_KV_RANK, HEADS, LANES)[:, :, :MLA_NOPE]
    v = d[:, HEADS * LANES:].reshape(MLA_KV_RANK, HEADS, MLA_V)
    return jnp.concatenate([k, v], axis=2).reshape(MLA_KV_RANK, HEADS * (MLA_NOPE + MLA_V))


def _pad_lanes(v, n=LANES):
    return jnp.pad(v, (0, n - v.shape[0])).reshape(1, n)


def _super_blocks(b):
    _, r, c = b.shape
    per = S5_GROUPS // S5_SUPER
    b = b.reshape(S5_SUPER, per, r, c)
    eye = jnp.eye(per, dtype=b.dtype)
    return (b[:, :, :, None, :] * eye[None, :, None, :, None]).reshape(S5_SUPER, per * r, per * c)


def _layer_params(l, w, small):
    p = {k: small[k][l] for k in small}
    q = {}
    q["norm_g"] = p["norm_g"].reshape(1, D_MODEL)
    q["mla_q_a_norm"] = p["mla_q_a_norm"].reshape(1, 256)
    q["mla_kv_a_norm"] = p["mla_kv_a_norm"].reshape(1, 128)
    q["mla_q_norm"] = _pad_lanes(p["mla_q_norm"])
    q["mla_k_norm"] = _pad_lanes(p["mla_k_norm"])
    q["fox_b_f"] = _pad_lanes(p["fox_b_f"])
    q["fox_q_norm"] = p["fox_q_norm"].reshape(1, 64)
    q["fox_k_norm"] = p["fox_k_norm"].reshape(1, 64)
    q["s5_d"] = p["s5_d"].reshape(1, 512)
    q["s5_b_glu"] = p["s5_b_glu"].reshape(1, 512)
    rep = lambda z: jnp.repeat(z, S5_GROUP, axis=0)
    q["lr16"], q["li16"] = rep(p["s5_lambda_re"]), rep(p["s5_lambda_im"])
    q["ldt16"] = rep(jnp.broadcast_to(p["s5_log_dt"][:, None], (S5_GROUPS, S5_STATE)))
    q["br2"] = p["s5_b_re"].transpose(0, 2, 1).reshape(512, 64)
    q["bi2"] = p["s5_b_im"].transpose(0, 2, 1).reshape(512, 64)
    q["c_re"], q["c_im"] = p["s5_c_re"], p["s5_c_im"]
    q["w_in"] = _win_pad(w["w_in"], f"l{l}_w_in_pad")
    q["wq"] = _pad_wq(w["mla_w_q_up"])
    q["wkv"] = _pad_wkv(w["mla_w_kv_up"])
    q["wglu"] = w["s5_w_glu"]
    q["wo"] = w["w_branch_out"]
    q["wout"] = w["w_out"]
    return q


def _layer_fwd(l, x, tabs, q, rides=None):
    rides = rides or {}
    n = lambda s: f"l{l}_{s}"
    sv = {"x": x}
    h = _norm_fwd(x, q["norm_g"], n("norm_fwd"))
    sv["h"] = h
    seg = [_mm(h, q["w_in"], "nn", n(f"proj{k}"), b_cols=(SEG_OFF[k], SEG_W[k])) for k in range(5)]
    sv["seg"] = seg
    mq, mk, mv = _mla_prep_fwd(seg[0], tabs, q, q["wq"], q["wkv"], n("mla_prep_fwd"))
    o_mla, lse_mla = _attn_fwd(mq, mk, mv, 1.0 / math.sqrt(MLA_QK), True, n("mla_attn_fwd"), ride=rides.get("mla_attn_fwd"))
    sv.update(mq=mq, mk=mk, mv=mv, o_mla=o_mla, lse_mla=lse_mla)
    fq, fk, fv, c, ct = _fox_prep_fwd(seg[0], seg[1], q["fox_b_f"], q["fox_q_norm"], q["fox_k_norm"], n("fox_prep_fwd"))
    o_fox, lse_fox = _attn_fwd(fq, fk, fv, 1.0 / math.sqrt(FOX_DIM), False, n("fox_attn_fwd"), c=c, ct=ct,
                               ride=rides.get("fox_attn_fwd"))
    sv.update(fq=fq, fk=fk, fv=fv, c=c, ct=ct, o_fox=o_fox, lse_fox=lse_fox)
    a_re16, a_im16, bb_re, bb_im = _s5_params_fwd(q["lr16"], q["li16"], q["ldt16"], q["br2"], q["bi2"], n("s5_params_fwd"))
    a_re = a_re16.reshape(S5_GROUPS, S5_GROUP, S5_STATE)[:, 0, :].reshape(1, S5_LANES)
    a_im = a_im16.reshape(S5_GROUPS, S5_GROUP, S5_STATE)[:, 0, :].reshape(1, S5_LANES)
    wb_re = _super_blocks(bb_re.reshape(S5_GROUPS, S5_GROUP, S5_STATE)).astype(BF16)
    wb_im = _super_blocks(bb_im.reshape(S5_GROUPS, S5_GROUP, S5_STATE)).astype(BF16)
    wc_re = _super_blocks(q["c_re"].transpose(0, 2, 1)).astype(BF16)
    wc_im = _super_blocks(-q["c_im"].transpose(0, 2, 1)).astype(BF16)
    x_re, x_im, y0 = _s5_core_fwd(seg[2], wb_re, wb_im, wc_re, wc_im, a_re, a_im, n("s5_scan_fwd"), ride=rides.get("s5_scan_fwd"))
    y_s5 = _s5_post_fwd(y0, seg[2], q["s5_d"], q["wglu"], q["s5_b_glu"], n("s5_post_fwd"))
    sv.update(a_re=a_re, a_im=a_im, wb_re=wb_re, wb_im=wb_im, wc_re=wc_re, wc_im=wc_im, x_re=x_re, x_im=x_im, y0=y0, y_s5=y_s5)
    out = _gate_fwd(o_mla, o_fox, y_s5, seg[3], seg[4], x, q["wo"], q["wout"], n("gate_fwd"))
    return out, sv


def _layer_bwd(l, dout, tabs, q, sv, rides=None, g=None):
    rides = rides or {}
    n = lambda s: f"l{l}_{s}"
    seg = sv["seg"]
    g = {} if g is None else g
    (do_mla, do_fox, dy_s5, dseg3, dseg4, g["wo"], g["wout"]) = _gate_bwd(
        sv["o_mla"], sv["o_fox"], sv["y_s5"], seg[3], seg[4], q["wo"], q["wout"], dout, n("gate_bwd"))
    dy0, du_a, g["wglu"], g["s5_b_glu"], g["s5_d"] = _s5_post_bwd(sv["y0"], seg[2], q["s5_d"], q["wglu"], q["s5_b_glu"], dy_s5,
                                                                 n("s5_post_bwd"))
    dseg2, da_re, da_im, dbb_re, dbb_im, dc_re, dc_im = _s5_core_bwd(
        dy0, seg[2], du_a, sv["x_re"], sv["x_im"], sv["wb_re"], sv["wb_im"], sv["wc_re"], sv["wc_im"], sv["a_re"], -sv["a_im"],
        n("s5_scan_bwd"))
    g["s5_c_re"] = dc_re.reshape(S5_GROUPS, S5_GROUP, S5_STATE)
    g["s5_c_im"] = -dc_im.reshape(S5_GROUPS, S5_GROUP, S5_STATE)
    first = (jnp.arange(512) % S5_GROUP == 0).astype(F32)[:, None]
    da_re16 = jnp.repeat(da_re.reshape(S5_GROUPS, S5_STATE), S5_GROUP, axis=0) * first
    da_im16 = jnp.repeat(da_im.reshape(S5_GROUPS, S5_STATE), S5_GROUP, axis=0) * first
    dlr, dli, dldt, dbr2, dbi2 = _s5_params_bwd(q["lr16"], q["li16"], q["ldt16"], q["br2"], q["bi2"], da_re16, da_im16, dbb_re,
                                               dbb_im, n("s5_params_bwd"))
    g["s5_lambda_re"], g["s5_lambda_im"], g["s5_log_dt"] = dlr, dli, dldt.reshape(S5_GROUPS)
    g["s5_b_re"] = dbr2.reshape(S5_GROUPS, S5_GROUP, S5_STATE).transpose(0, 2, 1)
    g["s5_b_im"] = dbi2.reshape(S5_GROUPS, S5_GROUP, S5_STATE).transpose(0, 2, 1)
    dl_fox = _attn_delta(sv["fq"], sv["fk"], sv["fv"], do_fox, sv["lse_fox"], 1.0 / math.sqrt(FOX_DIM), False,
                         n("fox_attn_delta"), c=sv["c"], ct=sv["ct"], ride=rides.get("fox_attn_delta"))
    dfq, dfk, dfv, dck = _attn_bwd(sv["fq"], sv["fk"], sv["fv"], do_fox, sv["lse_fox"], dl_fox, 1.0 / math.sqrt(FOX_DIM), False,
                                   n("fox_attn_bwd"), c=sv["c"], ct=sv["ct"], ride=rides.get("fox_attn_bwd"))
    dseg1, dff, g["fox_q_norm"], g["fox_k_norm"], dbf = _fox_prep_bwd(seg[0], seg[1], q["fox_b_f"], q["fox_q_norm"], q["fox_k_norm"],
                                                                      dfq, dfk, dfv, dck, n("fox_prep_bwd"))
    g["fox_b_f"] = dbf[0, :HEADS]
    dl_mla = _attn_delta(sv["mq"], sv["mk"], sv["mv"], do_mla, sv["lse_mla"], 1.0 / math.sqrt(MLA_QK), True, n("mla_attn_delta"),
                         ride=rides.get("mla_attn_delta"))
    dmq, dmk, dmv = _attn_bwd(sv["mq"], sv["mk"], sv["mv"], do_mla, sv["lse_mla"], dl_mla, 1.0 / math.sqrt(MLA_QK), True,
                              n("mla_attn_bwd"), ride=rides.get("mla_attn_bwd"))
    dseg0, dqan, dkvan, dqn, dkn, g["wq"], g["wkv"] = _mla_prep_bwd(seg[0], tabs, q, q["wq"], q["wkv"], dmq, dmk, dmv, dff,
                                                                   n("mla_prep_bwd"))
    g["mla_q_a_norm"], g["mla_kv_a_norm"] = dqan, dkvan
    g["mla_q_norm"], g["mla_k_norm"] = dqn[0, :MLA_QK], dkn[0, :MLA_QK]
    dsegs = [dseg0, dseg1, dseg2, dseg3, dseg4]
    dh = None
    for k in range(5):
        dh = _mm(dsegs[k], q["w_in"], "nt", n(f"dh{k}"), acc=dh, b_cols=(SEG_OFF[k], SEG_W[k]))
    g["w_in"] = [_mm(sv["h"], dsegs[k], "tn", n(f"dwin{k}")) for k in range(5)]
    dx, g["norm_g"] = _norm_bwd(sv["x"], q["norm_g"], dh, dout, n("norm_bwd"))
    return dx, g


MESH = pl.DeviceIdType.MESH
ANY = pl.BlockSpec(memory_space=pl.ANY)


def _all_gather(blocks, name):
    n = len(blocks)

    def body(*refs):
        x_refs, out_refs = refs[:n], refs[n:2 * n]
        send_sems, recv_sems, local_sems = refs[2 * n:]
        x, y, c = lax.axis_index("x"), lax.axis_index("y"), lax.axis_index("c")
        me, sibling = (x, y, c), (x, y, 1 - c)
        chips = [(1 - x, y), (x, 1 - y), (1 - x, 1 - y)]

        def slot(a, px, py, pc):
            return out_refs[a].at[4 * px + 2 * py + pc]

        def copy(a, k, blk, to, src=None):
            return pltpu.make_async_remote_copy(src_ref=slot(a, *blk) if src is None else src, dst_ref=slot(a, *blk),
                                                send_sem=send_sems.at[7 * a + k], recv_sem=recv_sems.at[7 * a + k],
                                                device_id=to, device_id_type=MESH)

        mine = [pltpu.make_async_copy(x_refs[a], slot(a, *me), local_sems.at[a]) for a in range(n)]
        for cp in mine:
            cp.start()
        first = []
        for j, chip in enumerate(chips):
            first += [copy(a, 1 + j, me, (*chip, c), src=x_refs[a]) for a in range(n)]
        first += [copy(a, 0, me, sibling, src=x_refs[a]) for a in range(n)]
        for cp in first:
            cp.start()
        passed = []
        for j, chip in enumerate(chips):
            for a in range(n):
                copy(a, 1 + j, (*chip, c), me).wait_recv()
                passed.append(copy(a, 4 + j, (*chip, c), sibling))
                passed[-1].start()
        for a in range(n):
            copy(a, 0, sibling, me).wait_recv()
        for j, chip in enumerate(chips):
            for a in range(n):
                copy(a, 4 + j, (*chip, 1 - c), me).wait_recv()
        for cp in first + passed:
            cp.wait_send()
        for cp in mine:
            cp.wait()

    return pl.pallas_call(
        body, name=name, out_shape=[jax.ShapeDtypeStruct((N_DEV,) + b.shape, b.dtype) for b in blocks],
        in_specs=[ANY] * n, out_specs=[ANY] * n,
        scratch_shapes=[pltpu.SemaphoreType.DMA((7 * n,)), pltpu.SemaphoreType.DMA((7 * n,)), pltpu.SemaphoreType.DMA((n,))],
    )(*blocks)


def _place():
    x, y, c = lax.axis_index("x"), lax.axis_index("y"), lax.axis_index("c")
    return x, y, c, [(1 - x, y), (x, 1 - y), (1 - x, 1 - y)]


def _remote(src, dst, send, recv, k, to):
    return pltpu.make_async_remote_copy(src_ref=src, dst_ref=dst, send_sem=send.at[k], recv_sem=recv.at[k], device_id=to,
                                        device_id_type=MESH)


def _plan_gather_ici(blocks):
    n = len(blocks)

    def copies(in_refs, out_refs, send, recv, local):
        x, y, c, chips = _place()
        mine = 4 * x + 2 * y + c
        loc = [pltpu.make_async_copy(in_refs[a], out_refs[a].at[mine], local.at[a]) for a in range(n)]
        rem = [_remote(in_refs[a], out_refs[a].at[mine], send, recv, 3 * a + j, (px, py, c))
               for j, (px, py) in enumerate(chips) for a in range(n)]
        return rem, loc

    return _Plan(blocks, [jax.ShapeDtypeStruct((N_DEV,) + b.shape, b.dtype) for b in blocks], 3 * n, n, copies)


def _plan_gather_d2d(gathered):
    n = len(gathered)

    def copies(in_refs, out_refs, send, recv, local):
        x, y, c, _ = _place()
        rem = [_remote(in_refs[a].at[2 * j + c], out_refs[a].at[2 * j + c], send, recv, 4 * a + j, (x, y, 1 - c))
               for a in range(n) for j in range(4)]
        return rem, []

    return _Plan(gathered, [jax.ShapeDtypeStruct(g.shape, g.dtype) for g in gathered], 4 * n, 0, copies,
                 aliases={a: a for a in range(n)})


def _plan_reduce_sibling(parts):
    n = len(parts)

    def copies(in_refs, out_refs, send, recv, local):
        x, y, c, _ = _place()
        rem = [_remote(in_refs[a].at[2 * j + (1 - c)], out_refs[a].at[j], send, recv, 4 * a + j, (x, y, 1 - c))
               for a in range(n) for j in range(4)]
        return rem, []

    return _Plan(parts, [jax.ShapeDtypeStruct((4,) + p.shape[1:], p.dtype) for p in parts], 4 * n, 0, copies)


def _plan_reduce_chips(sums):
    n = len(sums)

    def copies(in_refs, out_refs, send, recv, local):
        x, y, c, chips = _place()
        mine = 2 * x + y
        loc = [pltpu.make_async_copy(in_refs[a].at[mine], out_refs[a].at[mine], local.at[a]) for a in range(n)]
        rem = [_remote(in_refs[a].at[2 * px + py], out_refs[a].at[mine], send, recv, 3 * a + k, (px, py, c))
               for k, (px, py) in enumerate(chips) for a in range(n)]
        return rem, loc

    return _Plan(sums, [jax.ShapeDtypeStruct(p.shape, p.dtype) for p in sums], 3 * n, n, copies)


def _add_sibling(parts, got, name):
    _, r, cc = parts.shape
    tr = _pick(r, (512, 256, 128, 64, 32, 16))
    c = lax.axis_index("c")

    def body(c_ref, p_ref, g_ref, o_ref):
        o_ref[...] = (p_ref[...] + g_ref[...]).astype(BF16)

    return pl.pallas_call(
        body, name=name, out_shape=jax.ShapeDtypeStruct((4, r, cc), BF16),
        grid_spec=pltpu.PrefetchScalarGridSpec(
            num_scalar_prefetch=1, grid=(4, r // tr),
            in_specs=[pl.BlockSpec((1, tr, cc), lambda j, i, cr: (2 * j + cr[0], i, 0)),
                      pl.BlockSpec((1, tr, cc), lambda j, i, cr: (j, i, 0))],
            out_specs=pl.BlockSpec((1, tr, cc), lambda j, i, cr: (j, i, 0))),
        compiler_params=_vmem(48),
    )(c.reshape(1).astype(jnp.int32), parts, got)


def _sum_leading(parts, name):
    k, r, cc = parts.shape
    tr = _pick(r, (512, 256, 128, 64, 32, 16, 8))

    def body(p_ref, o_ref):
        acc = p_ref[0]
        for j in range(1, k):
            acc = acc + p_ref[j]
        o_ref[...] = acc

    return pl.pallas_call(
        body, name=name, out_shape=jax.ShapeDtypeStruct((r, cc), F32), grid=(r // tr,),
        in_specs=[pl.BlockSpec((k, tr, cc), lambda i: (0, i, 0))], out_specs=pl.BlockSpec((tr, cc), lambda i: (i, 0)),
    )(parts)


def _adamw_math(w, g, m, v):
    nm = ADAM_B1 * m + (1.0 - ADAM_B1) * g
    nv = ADAM_B2 * v + (1.0 - ADAM_B2) * jnp.square(g)
    m_hat = nm / (1.0 - ADAM_B1 ** ADAM_STEP)
    v_hat = nv / (1.0 - ADAM_B2 ** ADAM_STEP)
    return -ADAM_LR * (m_hat / (jnp.sqrt(v_hat) + ADAM_EPS) + ADAM_WD * w), nm, nv


def _adamw_sum(w, contribs, m, v, name, ride=None):
    nl = len(contribs)
    k, r, cc = contribs[0].shape
    tr = _pick(r, (256, 128, 64, 32, 16))
    nb = r // tr

    def body(w_ref, *rest):
        c_refs = rest[:nl]
        m_ref, v_ref, g_ref, d_ref, nm_ref, nv_ref = rest[nl:]
        for li in range(nl):
            @pl.when(pl.program_id(0) == li)
            def _(c_ref=c_refs[li]):
                g = c_ref[0].astype(F32)
                for j in range(1, k):
                    g = g + c_ref[j].astype(F32)
                g_ref[...] = g
                d_ref[...], nm_ref[...], nv_ref[...] = _adamw_math(w_ref[...], g, m_ref[...], v_ref[...])

    spec = pl.BlockSpec((tr, cc), lambda l, i: (l * nb + i, 0))
    cspec = pl.BlockSpec((k, tr, cc), lambda l, i: (0, i, 0))
    return _call_with_ride(body, name, (nl, nb), [w, *contribs, m, v], [spec] + [cspec] * nl + [spec, spec],
                           [jax.ShapeDtypeStruct(w.shape, F32)] * 4, [spec] * 4, ("parallel", "parallel"), 48, ride)


def _adamw_many(ws, gs, ms, vs, name):
    n = len(ws)

    def body(*refs):
        w_r, g_r, m_r, v_r = refs[:n], refs[n:2 * n], refs[2 * n:3 * n], refs[3 * n:4 * n]
        d_r, nm_r, nv_r = refs[4 * n:5 * n], refs[5 * n:6 * n], refs[6 * n:7 * n]
        for a in range(n):
            d_r[a][...], nm_r[a][...], nv_r[a][...] = _adamw_math(w_r[a][...], g_r[a][...], m_r[a][...], v_r[a][...])

    shapes = [jax.ShapeDtypeStruct(w.shape, F32) for w in ws]
    res = pl.pallas_call(body, name=name, out_shape=shapes * 3,
                         compiler_params=pltpu.CompilerParams(vmem_limit_bytes=56 * 1024 * 1024))(*ws, *gs, *ms, *vs)
    return res[:n], res[n:2 * n], res[2 * n:]


def _pack_rows(flat, lanes, row_mult):
    n = flat.shape[-1]
    rows = -(-n // lanes)
    rows = -(-rows // row_mult) * row_mult
    pad = rows * lanes - n
    if pad:
        flat = jnp.pad(flat, [(0, 0)] * (flat.ndim - 1) + [(0, pad)])
    return flat.reshape(flat.shape[:-1] + (rows, lanes))


def _rope_tables(positions):
    inv = 1.0 / (ROPE_THETA ** (jnp.arange(0, MLA_ROPE, 2, dtype=F32) / MLA_ROPE))
    ang = positions.astype(F32)[:, None] * inv
    cos, sin = jnp.cos(ang), jnp.sin(ang)
    s = positions.shape[0]
    z = lambda n: jnp.zeros((s, n), F32)
    c = jnp.concatenate([jnp.ones((s, 64), F32), cos, cos, z(32)], axis=1)
    sa = jnp.concatenate([z(64), -sin, z(48)], axis=1)
    sb = jnp.concatenate([z(80), sin, z(32)], axis=1)
    return c, sa, sb


def _full_weights(gathered):
    full = {}
    for k, g in zip(SHARDED, gathered):
        _, r, c = g.shape
        if k == "w_in":
            full[k] = g
        else:
            full[k] = g.transpose(1, 0, 2).reshape(r, N_DEV * c) if k in COL_SHARDED else g.reshape(N_DEV * r, c)
    return full


EARLY = ("s5_w_glu", "w_branch_out", "w_out")
LATE = ("w_in", "mla_w_q_up", "mla_w_kv_up")


def _owner_major(g, names, tag):
    parts = []
    for k in names:
        if k == "w_in":
            parts.append(_win_unpad(g["w_in"], f"{tag}_w_in_unpad"))
            continue
        big = {"mla_w_q_up": lambda: _unpad_wq(g["wq"]), "mla_w_kv_up": lambda: _unpad_wkv(g["wkv"]), "s5_w_glu": lambda: g["wglu"],
               "w_branch_out": lambda: g["wo"], "w_out": lambda: g["wout"]}[k]()
        r, c = big.shape
        if k in COL_SHARDED:
            parts.append(big.reshape(r, N_DEV, c // N_DEV).transpose(1, 0, 2))
        else:
            parts.append(big.reshape(N_DEV, r // N_DEV, c))
    return parts


def _device_step(x, positions, target, shards, small):
    tabs = _rope_tables(positions)
    box = {}
    full0 = _full_weights(_all_gather(shards[0], "gather_weights_l0"))
    q0 = _layer_params(0, full0, small)
    rides = {
        "mla_attn_fwd": _Ride(lambda: _plan_gather_ici(shards[1][:1]), lambda o: box.update(ici_a=o)),
        "fox_attn_fwd": _Ride(lambda: _join([_plan_gather_ici(shards[1][1:]), _plan_gather_d2d(box["ici_a"])]),
                              lambda o: box.update(ici_b=o[:5], w1_a=o[5:])),
        "s5_scan_fwd": _Ride(lambda: _plan_gather_d2d(box["ici_b"]), lambda o: box.update(w1_b=o)),
    }
    h, sv0 = _layer_fwd(0, x, tabs, q0, rides)
    q1 = _layer_params(1, _full_weights(list(box["w1_a"]) + list(box["w1_b"])), small)
    h, sv1 = _layer_fwd(1, h, tabs, q1)
    loss, d = _loss_head(h, target, "loss_head")
    d, g1 = _layer_bwd(1, d, tabs, q1, sv1)
    parts1 = _owner_major(g1, SHARDED, "l1")

    def chips_plan(names, parts, got, tag):
        return _plan_reduce_chips([_add_sibling(p, g, f"reduce_add_{tag}_{k}") for k, p, g in zip(names, parts, got)])

    g0 = {}
    rides = {
        "fox_attn_delta": _Ride(lambda: _plan_reduce_sibling(parts1), lambda o: box.update(got1=o)),
        "fox_attn_bwd": _Ride(lambda: chips_plan(SHARDED, parts1, box["got1"], "l1"), lambda o: box.update(contribs1=o)),
        "mla_attn_delta": _Ride(lambda: _plan_reduce_sibling(box.setdefault("early0", _owner_major(g0, EARLY, "l0"))),
                                lambda o: box.update(got0e=o)),
        "mla_attn_bwd": _Ride(lambda: chips_plan(EARLY, box["early0"], box["got0e"], "l0"), lambda o: box.update(contribs0e=o)),
    }
    d, _ = _layer_bwd(0, d, tabs, q0, sv0, rides, g0)
    late0 = _owner_major(g0, LATE, "l0")
    got0 = _run_plan(_plan_reduce_sibling(late0), "reduce_sibling_l0")
    contribs0 = list(_run_plan(chips_plan(LATE, late0, got0, "l0"), "reduce_chips_l0")) + list(box["contribs0e"])
    return loss[0, 0], d, [g0, g1], [contribs0, box["contribs1"]]


def kernel(x, positions, norm_g, w_in, mla_q_a_norm, mla_w_q_up, mla_kv_a_norm, mla_w_kv_up, mla_q_norm, mla_k_norm, fox_b_f, fox_q_norm, fox_k_norm, s5_lambda_re, s5_lambda_im, s5_log_dt, s5_b_re, s5_b_im, s5_c_re, s5_c_im, s5_d, s5_w_glu, s5_b_glu, w_branch_out, w_out, loss_target, m_norm_g, m_w_in, m_mla_q_a_norm, m_mla_w_q_up, m_mla_kv_a_norm, m_mla_w_kv_up, m_mla_q_norm, m_mla_k_norm, m_fox_b_f, m_fox_q_norm, m_fox_k_norm, m_s5_lambda_re, m_s5_lambda_im, m_s5_log_dt, m_s5_b_re, m_s5_b_im, m_s5_c_re, m_s5_c_im, m_s5_d, m_s5_w_glu, m_s5_b_glu, m_w_branch_out, m_w_out, v_norm_g, v_w_in, v_mla_q_a_norm, v_mla_w_q_up, v_mla_kv_a_norm, v_mla_w_kv_up, v_mla_q_norm, v_mla_k_norm, v_fox_b_f, v_fox_q_norm, v_fox_k_norm, v_s5_lambda_re, v_s5_lambda_im, v_s5_log_dt, v_s5_b_re, v_s5_b_im, v_s5_c_re, v_s5_c_im, v_s5_d, v_s5_w_glu, v_s5_b_glu, v_w_branch_out, v_w_out):
    env = dict(locals())
    wts = {k: env[k] for k in WEIGHTS}
    mom = {k: env["m_" + k] for k in WEIGHTS}
    var = {k: env["v_" + k] for k in WEIGHTS}

    shards = [[wts[k][l].astype(BF16) for k in SHARDED] for l in range(DEPTH)]
    small = {k: wts[k] for k in SMALL}
    loss, dx, grads, contribs = _device_step(x[0], positions[0], loss_target[0], shards, small)
    loss = lax.psum(loss, ("x", "y", "c"))

    two_d = {k: (wts[k].shape[0] * wts[k].shape[1], wts[k].shape[2]) for k in SHARDED}
    sm = {k: jnp.stack([g[k] for g in grads]).reshape(wts[k].shape) for k in SMALL}
    small_block = _pack_rows(jnp.concatenate([sm[k].reshape(-1) for k in SMALL]), LANES, 256)
    box = {}
    rides = {SHARDED[0]: _Ride(lambda: _plan_gather_ici([small_block]), lambda o: box.update(ici=o)),
             SHARDED[1]: _Ride(lambda: _plan_gather_d2d(box["ici"]), lambda o: box.update(all=o))}
    grad_out, delta_out, m_out, v_out = {}, {}, {}, {}
    for i, k in enumerate(SHARDED):
        shp = wts[k].shape
        res = _adamw_sum(wts[k].reshape(two_d[k]), [contribs[l][i] for l in range(DEPTH)], mom[k].reshape(two_d[k]),
                         var[k].reshape(two_d[k]), f"adamw_{k}", ride=rides.get(k))
        grad_out[k], delta_out[k], m_out[k], v_out[k] = (z.reshape(shp) for z in res)
    g_small = _sum_leading(box["all"][0], "sum_small_grads").reshape(-1)
    off = 0
    for k in SMALL:
        cnt = int(np.prod(wts[k].shape))
        grad_out[k] = g_small[off:off + cnt].reshape(wts[k].shape)
        off += cnt
    flat2 = lambda a: a.reshape(-1, a.shape[-1])
    d_s, m_s, v_s = _adamw_many([flat2(wts[k]) for k in SMALL], [flat2(grad_out[k]) for k in SMALL],
                                [flat2(mom[k]) for k in SMALL], [flat2(var[k]) for k in SMALL], "adamw_small")
    for i, k in enumerate(SMALL):
        delta_out[k], m_out[k], v_out[k] = (z[i].reshape(wts[k].shape) for z in (d_s, m_s, v_s))

    return (loss, dx[None], *[grad_out[k] for k in WEIGHTS], *[delta_out[k] for k in WEIGHTS],
            *[m_out[k] for k in WEIGHTS], *[v_out[k] for k in WEIGHTS])
```

```python
import functools
import math

import jax
import jax.numpy as jnp
import numpy as np
from jax import lax
from jax.experimental import pallas as pl
from jax.experimental.pallas import tpu as pltpu

F32 = jnp.float32
BF16 = jnp.bfloat16

D_MODEL = 1024
DEPTH = 2
CHUNK = 64
EPS = 1e-6
HEADS = 8
MLA_NOPE, MLA_ROPE, MLA_V = 64, 32, 64
MLA_Q_RANK, MLA_KV_RANK = 256, 128
MLA_QK = MLA_NOPE + MLA_ROPE
ROPE_THETA = 10000.0
FOX_DIM = 64
S5_WIDTH, S5_GROUP, S5_GROUPS, S5_STATE = 512, 16, 32, 64
S5_LANES = S5_GROUPS * S5_STATE
IN_WIDTH = 7080
N_DEV = 8
LANES = 128
SUBLANES = 8

ADAM_LR, ADAM_B1, ADAM_B2, ADAM_EPS, ADAM_WD, ADAM_STEP = 0.001, 0.9, 0.999, 1e-08, 0.01, 10

SEG_W = (512, 1536, 512, 1536, 3072)
SEG_OFF = (0, 512, 2048, 2560, 4096)
PAD_IN = 7168
NEG = -1e30

SHARDED = ("w_in", "mla_w_q_up", "mla_w_kv_up", "s5_w_glu", "w_branch_out", "w_out")
COL_SHARDED = ("w_in", "mla_w_q_up", "mla_w_kv_up")
SMALL = ("norm_g", "mla_q_a_norm", "mla_kv_a_norm", "mla_q_norm", "mla_k_norm", "fox_b_f", "fox_q_norm", "fox_k_norm",
         "s5_lambda_re", "s5_lambda_im", "s5_log_dt", "s5_b_re", "s5_b_im", "s5_c_re", "s5_c_im", "s5_d", "s5_b_glu")
WEIGHTS = ("norm_g", "w_in", "mla_q_a_norm", "mla_w_q_up", "mla_kv_a_norm", "mla_w_kv_up", "mla_q_norm", "mla_k_norm",
           "fox_b_f", "fox_q_norm", "fox_k_norm", "s5_lambda_re", "s5_lambda_im", "s5_log_dt", "s5_b_re", "s5_b_im",
           "s5_c_re", "s5_c_im", "s5_d", "s5_w_glu", "s5_b_glu", "w_branch_out", "w_out")
PACK_LANES = 512


def _pick(n, cands):
    for c in cands:
        if n % c == 0:
            return c
    return n


def _vmem(mb):
    return pltpu.CompilerParams(vmem_limit_bytes=mb * 1024 * 1024)


def _dot(a, b, dims):
    return lax.dot_general(a.astype(BF16), b.astype(BF16), (dims, ((), ())), preferred_element_type=F32)


def _nn(a, b):
    return _dot(a, b, ((1,), (0,)))


def _nt(a, b):
    return _dot(a, b, ((1,), (1,)))


def _tn(a, b):
    return _dot(a, b, ((0,), (0,)))


def _rms(x, g, n=None):
    n = x.shape[-1] if n is None else n
    return x * lax.rsqrt(jnp.sum(x * x, axis=-1, keepdims=True) / n + EPS) * g


def _rope(t, c, sa, sb):
    return t * c + pltpu.roll(t, LANES - 16, 1) * sa + pltpu.roll(t, 16, 1) * sb


def _rope_t(d, c, sa, sb):
    return d * c + pltpu.roll(d * sa, 16, 1) + pltpu.roll(d * sb, LANES - 16, 1)


def _mm(a, b, mode, name, acc=None, b_cols=None):
    if mode == "tn":
        kd, m = a.shape
    else:
        m, kd = a.shape
    b_off, b_w = b_cols if b_cols is not None else (0, b.shape[1])
    n = b.shape[0] if mode == "nt" else b_w
    tm, tn, tk = _pick(m, (1024, 512, 256, 128)), _pick(n, (1024, 512, 256, 128)), _pick(kd, (1024, 512, 256, 128))
    nk = kd // tk
    if mode == "tn":
        a_spec = pl.BlockSpec((tk, tm), lambda i, j, k: (k, i))
    else:
        a_spec = pl.BlockSpec((tm, tk), lambda i, j, k: (i, k))
    if mode == "nt":
        assert b_off % tk == 0
        b_spec = pl.BlockSpec((tn, tk), lambda i, j, k: (j, k + b_off // tk))
    else:
        assert b_off % tn == 0
        b_spec = pl.BlockSpec((tk, tn), lambda i, j, k: (k, j + b_off // tn))
    dims = {"nn": ((1,), (0,)), "nt": ((1,), (1,)), "tn": ((0,), (0,))}[mode]
    o_spec = pl.BlockSpec((tm, tn), lambda i, j, k: (i, j))
    has_acc = acc is not None

    def body(*refs):
        if has_acc:
            a_ref, b_ref, c_ref, o_ref = refs
        else:
            a_ref, b_ref, o_ref = refs
        k = pl.program_id(2)
        prod = _dot(a_ref[...], b_ref[...], dims)

        @pl.when(k == 0)
        def _():
            o_ref[...] = prod + c_ref[...] if has_acc else prod

        @pl.when(k > 0)
        def _():
            o_ref[...] += prod

    ins = [a, b] + ([acc] if has_acc else [])
    in_specs = [a_spec, b_spec] + ([o_spec] if has_acc else [])
    return pl.pallas_call(
        body, name=name, grid=(m // tm, n // tn, nk), in_specs=in_specs, out_specs=o_spec,
        out_shape=jax.ShapeDtypeStruct((m, n), F32),
        compiler_params=pltpu.CompilerParams(dimension_semantics=("parallel", "parallel", "arbitrary"),
                                             vmem_limit_bytes=48 * 1024 * 1024),
    )(*ins)


def _stage(name, fn, n_steps, ins, outs, accs=(), scratch=(), vmem_mb=48):
    n_in, n_out, n_acc = len(ins), len(outs), len(accs)

    def body(*refs):
        in_refs = refs[:n_in]
        out_refs = refs[n_in:n_in + n_out]
        acc_refs = refs[n_in + n_out:n_in + n_out + n_acc]
        scr = refs[n_in + n_out + n_acc:]
        if n_acc:
            @pl.when(pl.program_id(0) == 0)
            def _():
                for r in acc_refs:
                    r[...] = jnp.zeros(r.shape, r.dtype)
        fn(in_refs, out_refs, acc_refs, scr)

    acc_specs = [pl.BlockSpec(a.shape, functools.partial(lambda i, nd: (0,) * nd, nd=len(a.shape))) for a in accs]
    res = pl.pallas_call(
        body, name=name, grid=(n_steps,),
        in_specs=[s for _, s in ins], out_specs=[s for _, s in outs] + acc_specs,
        out_shape=[s for s, _ in outs] + list(accs), scratch_shapes=list(scratch),
        compiler_params=pltpu.CompilerParams(dimension_semantics=("arbitrary",),
                                             vmem_limit_bytes=vmem_mb * 1024 * 1024),
    )(*[a for a, _ in ins])
    return res


def _rows(ts, w, j=0):
    return pl.BlockSpec((ts, w), lambda i: (i, j))


def _rows_rev(ts, w, n, j=0):
    return pl.BlockSpec((ts, w), lambda i: (n - 1 - i, j))


def _heads(ts, d):
    return pl.BlockSpec((HEADS, ts, d), lambda i: (0, i, 0))


def _heads_rev(ts, d, n):
    return pl.BlockSpec((HEADS, ts, d), lambda i: (0, n - 1 - i, 0))


def _full(shape):
    nd = len(shape)
    return pl.BlockSpec(tuple(shape), lambda i: (0,) * nd)


def _sds(shape, dtype=F32):
    return jax.ShapeDtypeStruct(tuple(shape), dtype)


def _norm_fwd(x, g, name):
    s = x.shape[0]
    ts = _pick(s, (256, 128))

    def fn(ins, outs, accs, scr):
        outs[0][...] = _rms(ins[0][...], ins[1][...]).astype(BF16)

    return _stage(name, fn, s // ts, [(x, _rows(ts, D_MODEL)), (g, _full(g.shape))],
                  [(_sds((s, D_MODEL), BF16), _rows(ts, D_MODEL))])[0]


def _norm_bwd(x, g, dh, dres, name):
    s = x.shape[0]
    ts = _pick(s, (256, 128))

    def fn(ins, outs, accs, scr):
        _, vjp = jax.vjp(_rms, ins[0][...], ins[1][...])
        dx, dg = vjp(ins[2][...])
        outs[0][...] = dx + ins[3][...]
        accs[0][...] += dg

    r = _stage(name, fn, s // ts,
               [(x, _rows(ts, D_MODEL)), (g, _full(g.shape)), (dh, _rows(ts, D_MODEL)), (dres, _rows(ts, D_MODEL))],
               [(_sds((s, D_MODEL)), _rows(ts, D_MODEL))], accs=[_sds((1, D_MODEL))])
    return r[0], r[1]


def _mla_q(qraw, c, sa, sb, qn):
    return _rms(_rope(qraw, c, sa, sb), qn, MLA_QK)


def _mla_prep_fwd(seg0, tabs, p, wq, wkv, name):
    s = seg0.shape[0]
    ts = _pick(s, (256, 128))

    def fn(ins, outs, accs, scr):
        blk, cos, sa, sb, qan, kvan, qn, kn, wq_r, wkv_r = ins
        b = blk[...]
        cq, ckv, kt = b[:, :256], b[:, 256:384], b[:, 384:512]
        lane = lax.broadcasted_iota(jnp.int32, kt.shape, 1)
        kpe = jnp.where(lane >= 64, kt, 0.0)
        q_raw = _nn(_rms(cq, qan[...]), wq_r[...])
        kv_raw = _nn(_rms(ckv, kvan[...]), wkv_r[...])
        c, a, bb = cos[...], sa[...], sb[...]
        for h in range(HEADS):
            outs[0][h] = _mla_q(q_raw[:, LANES * h:LANES * (h + 1)], c, a, bb, qn[...]).astype(BF16)
            outs[1][h] = _mla_q(kv_raw[:, LANES * h:LANES * (h + 1)] + kpe, c, a, bb, kn[...]).astype(BF16)
            outs[2][h] = kv_raw[:, 1024 + 64 * h:1024 + 64 * (h + 1)].astype(BF16)

    consts = [p["mla_q_a_norm"], p["mla_kv_a_norm"], p["mla_q_norm"], p["mla_k_norm"], wq, wkv]
    return _stage(name, fn, s // ts,
                  [(seg0, _rows(ts, 512))] + [(t, _rows(ts, LANES)) for t in tabs] + [(a, _full(a.shape)) for a in consts],
                  [(_sds((HEADS, s, LANES), BF16), _heads(ts, LANES)), (_sds((HEADS, s, LANES), BF16), _heads(ts, LANES)),
                   (_sds((HEADS, s, 64), BF16), _heads(ts, 64))])


def _mla_prep_bwd(seg0, tabs, p, wq, wkv, dq, dk, dv, dff, name):
    s = seg0.shape[0]
    ts = _pick(s, (256, 128))

    def fn(ins, outs, accs, scr):
        blk, cos, sa, sb, qan, kvan, qn, kn, wq_r, wkv_r, dq_r, dk_r, dv_r, dff_r = ins
        dqan, dkvan, dqn, dkn, dwq, dwkv = accs
        dqraw_s, dkvraw_s = scr
        b = blk[...]
        cq, ckv, kt = b[:, :256], b[:, 256:384], b[:, 384:512]
        lane = lax.broadcasted_iota(jnp.int32, kt.shape, 1)
        kpe = jnp.where(lane >= 64, kt, 0.0)
        cqn, vjp_cq = jax.vjp(_rms, cq, qan[...])
        ckvn, vjp_ckv = jax.vjp(_rms, ckv, kvan[...])
        q_raw = _nn(cqn, wq_r[...])
        kv_raw = _nn(ckvn, wkv_r[...])
        c, a, bb = cos[...], sa[...], sb[...]

        def head_bwd(raw, gain, d):
            t = _rope(raw, c, a, bb)
            _, vjp = jax.vjp(functools.partial(_rms, n=MLA_QK), t, gain)
            dt, dgain = vjp(d)
            return _rope_t(dt, c, a, bb), dgain

        dkpe = jnp.zeros(kt.shape, F32)
        for h in range(HEADS):
            dqh, dg = head_bwd(q_raw[:, LANES * h:LANES * (h + 1)], qn[...], dq_r[h])
            dqn[...] += dg
            dqraw_s[:, LANES * h:LANES * (h + 1)] = dqh
            dkh, dg = head_bwd(kv_raw[:, LANES * h:LANES * (h + 1)] + kpe, kn[...], dk_r[h])
            dkn[...] += dg
            dkvraw_s[:, LANES * h:LANES * (h + 1)] = dkh
            dkpe = dkpe + dkh
            dkvraw_s[:, 1024 + 64 * h:1024 + 64 * (h + 1)] = dv_r[h]
        dq_raw = dqraw_s[...]
        dkv_raw = dkvraw_s[...]
        dwq[...] += _tn(cqn, dq_raw)
        dwkv[...] += _tn(ckvn, dkv_raw)
        dcq, dg = vjp_cq(_nt(dq_raw, wq_r[...]))
        dqan[...] += dg
        dckv, dg = vjp_ckv(_nt(dkv_raw, wkv_r[...]))
        dkvan[...] += dg
        outs[0][:, 0:256] = dcq.astype(BF16)
        outs[0][:, 256:384] = dckv.astype(BF16)
        outs[0][:, 384:512] = (jnp.where(lane >= 64, dkpe, 0.0) + dff_r[...]).astype(BF16)

    consts = [p["mla_q_a_norm"], p["mla_kv_a_norm"], p["mla_q_norm"], p["mla_k_norm"], wq, wkv]
    return _stage(name, fn, s // ts,
                  [(seg0, _rows(ts, 512))] + [(t, _rows(ts, LANES)) for t in tabs] + [(a, _full(a.shape)) for a in consts]
                  + [(dq, _heads(ts, LANES)), (dk, _heads(ts, LANES)), (dv, _heads(ts, 64)), (dff, _rows(ts, LANES))],
                  [(_sds((s, 512), BF16), _rows(ts, 512))],
                  accs=[_sds((1, 256)), _sds((1, 128)), _sds((1, LANES)), _sds((1, LANES)), _sds(wq.shape), _sds(wkv.shape)],
                  scratch=[pltpu.VMEM((ts, 1024), F32), pltpu.VMEM((ts, 1536), F32)])


def _fox_prep_fwd(seg0, seg1, bf, qn, kn, name):
    s = seg0.shape[0]
    ts = _pick(s, (256, 128))
    steps = int(math.log2(ts))

    def fn(ins, outs, accs, scr):
        kt_r, x_r, bf_r, qn_r, kn_r = ins
        carry = scr[0]

        @pl.when(pl.program_id(0) == 0)
        def _():
            carry[...] = jnp.zeros(carry.shape, F32)

        x = x_r[...]
        for h in range(HEADS):
            outs[0][h] = _rms(x[:, 64 * h:64 * (h + 1)], qn_r[...]).astype(BF16)
            outs[1][h] = _rms(x[:, 512 + 64 * h:512 + 64 * (h + 1)], kn_r[...]).astype(BF16)
            outs[2][h] = x[:, 1024 + 64 * h:1024 + 64 * (h + 1)].astype(BF16)
        kt = kt_r[...]
        lane = lax.broadcasted_iota(jnp.int32, kt.shape, 1)
        row = lax.broadcasted_iota(jnp.int32, kt.shape, 0)
        cs = jnp.where(lane < HEADS, jax.nn.log_sigmoid(kt + bf_r[...]), 0.0)
        for k in range(steps):
            sh = 1 << k
            cs = cs + jnp.where(row >= sh, pltpu.roll(cs, sh, 0), 0.0)
        cs = cs + carry[0:1, :]
        outs[3][...] = cs
        outs[4][...] = cs.T[0:HEADS, :]
        carry[0:1, :] = cs[ts - 1:ts, :]

    return _stage(name, fn, s // ts,
                  [(seg0, _rows(ts, LANES, 3)), (seg1, _rows(ts, 1536)), (bf, _full(bf.shape)), (qn, _full(qn.shape)),
                   (kn, _full(kn.shape))],
                  [(_sds((HEADS, s, 64), BF16), _heads(ts, 64)), (_sds((HEADS, s, 64), BF16), _heads(ts, 64)),
                   (_sds((HEADS, s, 64), BF16), _heads(ts, 64)), (_sds((s, LANES)), _rows(ts, LANES)),
                   (_sds((HEADS, s)), pl.BlockSpec((HEADS, ts), lambda i: (0, i)))],
                  scratch=[pltpu.VMEM((SUBLANES, LANES), F32)])


def _fox_prep_bwd(seg0, seg1, bf, qn, kn, dq, dk, dv, dck, name):
    s = seg0.shape[0]
    ts = _pick(s, (256, 128))
    n = s // ts
    steps = int(math.log2(ts))

    def fn(ins, outs, accs, scr):
        kt_r, x_r, bf_r, qn_r, kn_r, dq_r, dk_r, dv_r, dck_r = ins
        dqn, dkn, dbf = accs
        carry, dbuf = scr

        @pl.when(pl.program_id(0) == 0)
        def _():
            carry[...] = jnp.zeros(carry.shape, F32)

        x = x_r[...]
        for h in range(HEADS):
            _, vjp = jax.vjp(_rms, x[:, 64 * h:64 * (h + 1)], qn_r[...])
            d, dg = vjp(dq_r[h])
            dbuf[:, 64 * h:64 * (h + 1)] = d
            dqn[...] += dg
            _, vjp = jax.vjp(_rms, x[:, 512 + 64 * h:512 + 64 * (h + 1)], kn_r[...])
            d, dg = vjp(dk_r[h])
            dbuf[:, 512 + 64 * h:512 + 64 * (h + 1)] = d
            dkn[...] += dg
            dbuf[:, 1024 + 64 * h:1024 + 64 * (h + 1)] = dv_r[h]
        outs[0][...] = dbuf[...].astype(BF16)
        dc = dck_r[...].reshape(HEADS, ts)
        dc = jnp.concatenate([dc, jnp.zeros((LANES - HEADS, ts), F32)], axis=0).T
        row = lax.broadcasted_iota(jnp.int32, dc.shape, 0)
        lane = lax.broadcasted_iota(jnp.int32, dc.shape, 1)
        for k in range(steps):
            sh = 1 << k
            dc = dc + jnp.where(row < ts - sh, pltpu.roll(dc, ts - sh, 0), 0.0)
        dc = dc + carry[0:1, :]
        carry[0:1, :] = dc[0:1, :]
        dff = jnp.where(lane < HEADS, dc * jax.nn.sigmoid(-(kt_r[...] + bf_r[...])), 0.0)
        outs[1][...] = dff
        dbf[...] += jnp.sum(dff, axis=0, keepdims=True)

    return _stage(name, fn, n,
                  [(seg0, _rows_rev(ts, LANES, n, 3)), (seg1, _rows_rev(ts, 1536, n)), (bf, _full(bf.shape)),
                   (qn, _full(qn.shape)), (kn, _full(kn.shape)), (dq, _heads_rev(ts, 64, n)), (dk, _heads_rev(ts, 64, n)),
                   (dv, _heads_rev(ts, 64, n)), (dck, pl.BlockSpec((HEADS, 1, ts), lambda i: (0, 0, n - 1 - i)))],
                  [(_sds((s, 1536), BF16), _rows_rev(ts, 1536, n)), (_sds((s, LANES)), _rows_rev(ts, LANES, n))],
                  accs=[_sds((1, 64)), _sds((1, 64)), _sds((1, LANES))],
                  scratch=[pltpu.VMEM((SUBLANES, LANES), F32), pltpu.VMEM((ts, 1536), F32)])


def _allowed(i, j, t, chunk_causal):
    qpos = i * t + lax.broadcasted_iota(jnp.int32, (t, t), 0)
    kpos = j * t + lax.broadcasted_iota(jnp.int32, (t, t), 1)
    if chunk_causal:
        return (kpos // CHUNK) <= (qpos // CHUNK)
    return kpos <= qpos


def _pick_col(c_blk, h):
    lane = lax.broadcasted_iota(jnp.int32, c_blk.shape, 1)
    return jnp.sum(jnp.where(lane == h, c_blk, 0.0), axis=1, keepdims=True)


class _Ride:
    def __init__(self, make, take):
        self.make, self.take = make, take


class _Plan:
    def __init__(self, ins, out_shapes, n_remote, n_local, copies, aliases=None):
        self.ins, self.out_shapes, self.n_remote, self.n_local = list(ins), list(out_shapes), n_remote, n_local
        self.copies, self.aliases = copies, dict(aliases or {})

    def scratch(self):
        return [pltpu.SemaphoreType.DMA((self.n_remote,)), pltpu.SemaphoreType.DMA((self.n_remote,)),
                pltpu.SemaphoreType.DMA((max(self.n_local, 1),))]

    def start(self, in_refs, out_refs, sems):
        remote, local = self.copies(in_refs, out_refs, *sems)
        for cp in local + remote:
            cp.start()

    def wait(self, in_refs, out_refs, sems):
        remote, local = self.copies(in_refs, out_refs, *sems)
        for cp in remote:
            cp.wait()
        for cp in local:
            cp.wait()


class _Off:
    def __init__(self, ref, off):
        self.ref, self.off, self.at = ref, off, self

    def __getitem__(self, k):
        return self.ref.at[k + self.off]


def _join(plans):
    ins = [a for p in plans for a in p.ins]
    outs = [o for p in plans for o in p.out_shapes]
    aliases, i0, o0 = {}, 0, 0
    for p in plans:
        aliases.update({i0 + i: o0 + o for i, o in p.aliases.items()})
        i0, o0 = i0 + len(p.ins), o0 + len(p.out_shapes)

    def copies(in_refs, out_refs, send, recv, local):
        rem, loc, i0, o0, r0, l0 = [], [], 0, 0, 0, 0
        for p in plans:
            r, l = p.copies(in_refs[i0:i0 + len(p.ins)], out_refs[o0:o0 + len(p.out_shapes)], _Off(send, r0), _Off(recv, r0),
                            _Off(local, l0))
            rem, loc = rem + r, loc + l
            i0, o0, r0, l0 = i0 + len(p.ins), o0 + len(p.out_shapes), r0 + p.n_remote, l0 + p.n_local
        return rem, loc

    return _Plan(ins, outs, sum(p.n_remote for p in plans), sum(p.n_local for p in plans), copies, aliases)


def _call_with_ride(core, name, grid, ins, in_specs, out_shape, out_specs, semantics, vmem_mb, ride, scratch=()):
    n_in, n_out, n_scr = len(ins), len(out_shape), len(scratch)
    if ride is None:
        return pl.pallas_call(
            core, name=name, grid=grid, in_specs=in_specs, out_specs=out_specs, out_shape=out_shape,
            scratch_shapes=list(scratch),
            compiler_params=pltpu.CompilerParams(dimension_semantics=semantics, vmem_limit_bytes=vmem_mb * 1024 * 1024),
        )(*ins)
    plan = ride.make()
    ci, co = len(plan.ins), len(plan.out_shapes)

    def body(*refs):
        c_in = refs[n_in:n_in + ci]
        a_out = refs[n_in + ci:n_in + ci + n_out]
        c_out = refs[n_in + ci + n_out:n_in + ci + n_out + co]
        own = refs[n_in + ci + n_out + co:n_in + ci + n_out + co + n_scr]
        sems = refs[n_in + ci + n_out + co + n_scr:]
        ids = [pl.program_id(d) for d in range(len(grid))]
        first = functools.reduce(jnp.logical_and, [i == 0 for i in ids])
        last = functools.reduce(jnp.logical_and, [i == g - 1 for i, g in zip(ids, grid)])

        @pl.when(first)
        def _():
            plan.start(c_in, c_out, sems)

        core(*refs[:n_in], *a_out, *own)

        @pl.when(last)
        def _():
            plan.wait(c_in, c_out, sems)

    res = pl.pallas_call(
        body, name=name, grid=grid, in_specs=list(in_specs) + [ANY] * ci, out_specs=list(out_specs) + [ANY] * co,
        out_shape=list(out_shape) + plan.out_shapes, scratch_shapes=list(scratch) + plan.scratch(),
        input_output_aliases={n_in + i: n_out + o for i, o in plan.aliases.items()},
        compiler_params=pltpu.CompilerParams(dimension_semantics=("arbitrary",) * len(grid),
                                             vmem_limit_bytes=vmem_mb * 1024 * 1024),
    )(*ins, *plan.ins)
    ride.take(res[n_out:])
    return res[:n_out]


def _run_plan(plan, name):
    ci = len(plan.ins)

    def body(*refs):
        c_in, c_out, sems = refs[:ci], refs[ci:ci + len(plan.out_shapes)], refs[ci + len(plan.out_shapes):]
        plan.start(c_in, c_out, sems)
        plan.wait(c_in, c_out, sems)

    return pl.pallas_call(body, name=name, in_specs=[ANY] * ci, out_specs=[ANY] * len(plan.out_shapes),
                          out_shape=plan.out_shapes, scratch_shapes=plan.scratch(),
                          input_output_aliases=plan.aliases)(*plan.ins)


def _attn_fwd(q, k, v, scale, chunk_causal, name, c=None, ct=None, hps=8, ride=None):
    _, s, dk = q.shape
    dv = v.shape[2]
    t = _pick(s, (256, 128))
    bias = c is not None

    def body(*refs):
        if bias:
            q_ref, k_ref, v_ref, c_ref, ct_ref, o_ref, lse_ref = refs
        else:
            q_ref, k_ref, v_ref, o_ref, lse_ref = refs
        hp, i = pl.program_id(0), pl.program_id(1)
        qb = [q_ref[e] for e in range(hps)]
        cq = [_pick_col(c_ref[...], hp * hps + e) if bias else None for e in range(hps)]

        def step(j, carry, diagonal):
            off = pl.multiple_of(j * t, t)
            out = []
            for e in range(hps):
                m, l, acc = carry[e]
                sc = _nt(qb[e], k_ref[e, pl.ds(off, t), :]) * scale
                if bias:
                    sc = sc + (cq[e] - ct_ref[pl.ds(hp * hps + e, 1), pl.ds(off, t)])
                if diagonal:
                    sc = jnp.where(_allowed(i, j, t, chunk_causal), sc, NEG)
                m_new = jnp.maximum(m, jnp.max(sc, axis=1, keepdims=True))
                pr = jnp.exp(sc - m_new)
                alpha = jnp.exp(m - m_new)
                out.append((m_new, alpha * l + jnp.sum(pr, axis=1, keepdims=True),
                            alpha * acc + _nn(pr, v_ref[e, pl.ds(off, t), :])))
            return tuple(out)

        init = tuple((jnp.full((t, 1), NEG, F32), jnp.zeros((t, 1), F32), jnp.zeros((t, dv), F32)) for _ in range(hps))
        res = step(i, lax.fori_loop(0, i, functools.partial(step, diagonal=False), init), True)
        for e in range(hps):
            m, l, acc = res[e]
            o_ref[e] = acc / l
            lse_ref[e] = m + jnp.log(l)

    ins = [q, k, v] + ([c, ct] if bias else [])
    in_specs = [pl.BlockSpec((hps, t, dk), lambda h, i: (h, i, 0)), pl.BlockSpec((hps, s, dk), lambda h, i: (h, 0, 0)),
                pl.BlockSpec((hps, s, dv), lambda h, i: (h, 0, 0))]
    if bias:
        in_specs += [pl.BlockSpec((t, LANES), lambda h, i: (i, 0)), pl.BlockSpec((HEADS, s), lambda h, i: (0, 0))]
    return _call_with_ride(
        body, name, (HEADS // hps, s // t), ins, in_specs, [_sds((HEADS, s, dv)), _sds((HEADS, s, 1))],
        [pl.BlockSpec((hps, t, dv), lambda h, i: (h, i, 0)), pl.BlockSpec((hps, t, 1), lambda h, i: (h, i, 0))],
        ("parallel", "parallel"), 48, ride)


def _attn_delta(q, k, v, do, lse, scale, chunk_causal, name, c=None, ct=None, hps=4, ride=None):
    _, s, dk = q.shape
    dv = v.shape[2]
    t = _pick(s, (256, 128))
    bias = c is not None

    def body(*refs):
        if bias:
            q_ref, k_ref, v_ref, do_ref, lse_ref, c_ref, ct_ref, dl_ref = refs
        else:
            q_ref, k_ref, v_ref, do_ref, lse_ref, dl_ref = refs
        hp, i = pl.program_id(0), pl.program_id(1)
        qb, dob, lse_b = ([r[e] for e in range(hps)] for r in (q_ref, do_ref, lse_ref))
        cq = [_pick_col(c_ref[...], hp * hps + e) if bias else None for e in range(hps)]

        def step(j, acc, diagonal):
            off = pl.multiple_of(j * t, t)
            out = []
            for e in range(hps):
                sc = _nt(qb[e], k_ref[e, pl.ds(off, t), :]) * scale
                if bias:
                    sc = sc + (cq[e] - ct_ref[pl.ds(hp * hps + e, 1), pl.ds(off, t)])
                pr = jnp.exp(sc - lse_b[e])
                if diagonal:
                    pr = jnp.where(_allowed(i, j, t, chunk_causal), pr, 0.0)
                out.append(acc[e] + jnp.sum(pr * _nt(dob[e], v_ref[e, pl.ds(off, t), :]), axis=1, keepdims=True))
            return tuple(out)

        init = tuple(jnp.zeros((t, 1), F32) for _ in range(hps))
        res = step(i, lax.fori_loop(0, i, functools.partial(step, diagonal=False), init), True)
        for e in range(hps):
            dl_ref[e] = res[e]

    ins = [q, k, v, do, lse] + ([c, ct] if bias else [])
    in_specs = [pl.BlockSpec((hps, t, dk), lambda h, i: (h, i, 0)), pl.BlockSpec((hps, s, dk), lambda h, i: (h, 0, 0)),
                pl.BlockSpec((hps, s, dv), lambda h, i: (h, 0, 0)), pl.BlockSpec((hps, t, dv), lambda h, i: (h, i, 0)),
                pl.BlockSpec((hps, t, 1), lambda h, i: (h, i, 0))]
    if bias:
        in_specs += [pl.BlockSpec((t, LANES), lambda h, i: (i, 0)), pl.BlockSpec((HEADS, s), lambda h, i: (0, 0))]
    return _call_with_ride(body, name, (HEADS // hps, s // t), ins, in_specs, [_sds((HEADS, s, 1))],
                           [pl.BlockSpec((hps, t, 1), lambda h, i: (h, i, 0))], ("parallel", "parallel"), 48, ride)[0]


def _attn_bwd(q, k, v, do, lse, delta, scale, chunk_causal, name, c=None, ct=None, hps=8, ride=None):
    _, s, dk = q.shape
    dv = v.shape[2]
    t = _pick(s, (256, 128))
    n = s // t
    bias = c is not None

    def body(*refs):
        if bias:
            q_ref, k_ref, v_ref, do_ref, lse_ref, dl_ref, c_ref, ct_ref, dq_ref, dk_ref, dv_ref, dck_ref = refs
        else:
            q_ref, k_ref, v_ref, do_ref, lse_ref, dl_ref, dq_ref, dk_ref, dv_ref = refs
        hp, j = pl.program_id(0), pl.program_id(1)

        @pl.when(j == 0)
        def _():
            dq_ref[...] = jnp.zeros(dq_ref.shape, F32)

        kb, vb = [k_ref[e] for e in range(hps)], [v_ref[e] for e in range(hps)]
        joff = pl.multiple_of(j * t, t)
        ck = [ct_ref[pl.ds(hp * hps + e, 1), pl.ds(joff, t)] if bias else None for e in range(hps)]

        def step(i, carry, diagonal):
            off = pl.multiple_of(i * t, t)
            out = []
            for e in range(hps):
                dk_acc, dv_acc, dck_acc = carry[e]
                qb = q_ref[e, pl.ds(off, t), :]
                dob = do_ref[e, pl.ds(off, t), :]
                sc = _nt(qb, kb[e]) * scale
                if bias:
                    sc = sc + (_pick_col(c_ref[pl.ds(off, t), :], hp * hps + e) - ck[e])
                pr = jnp.exp(sc - lse_ref[e, pl.ds(off, t), :])
                if diagonal:
                    pr = jnp.where(_allowed(i, j, t, chunk_causal), pr, 0.0)
                dv_acc = dv_acc + _tn(pr, dob)
                ds = pr * (_nt(dob, vb[e]) - dl_ref[e, pl.ds(off, t), :])
                dq_ref[e, pl.ds(off, t), :] += _nn(ds, kb[e]) * scale
                dk_acc = dk_acc + _tn(ds, qb) * scale
                if bias:
                    dck_acc = dck_acc - jnp.sum(ds, axis=0, keepdims=True)
                out.append((dk_acc, dv_acc, dck_acc))
            return tuple(out)

        zero = tuple((jnp.zeros((t, dk), F32), jnp.zeros((t, dv), F32), jnp.zeros((1, t), F32)) for _ in range(hps))
        res = lax.fori_loop(j + 1, n, functools.partial(step, diagonal=False), step(j, zero, True))
        for e in range(hps):
            dk_ref[e], dv_ref[e] = res[e][0], res[e][1]
            if bias:
                dck_ref[e] = res[e][2]

    ins = [q, k, v, do, lse, delta] + ([c, ct] if bias else [])
    full = lambda d: pl.BlockSpec((hps, s, d), lambda h, j: (h, 0, 0))
    blk = lambda d: pl.BlockSpec((hps, t, d), lambda h, j: (h, j, 0))
    in_specs = [full(dk), blk(dk), blk(dv), full(dv), full(1), full(1)]
    if bias:
        in_specs += [pl.BlockSpec((s, LANES), lambda h, j: (0, 0)), pl.BlockSpec((HEADS, s), lambda h, j: (0, 0))]
    out_specs = [full(dk), blk(dk), blk(dv)]
    out_shape = [_sds((HEADS, s, dk)), _sds((HEADS, s, dk)), _sds((HEADS, s, dv))]
    if bias:
        out_specs.append(pl.BlockSpec((hps, 1, t), lambda h, j: (h, 0, j)))
        out_shape.append(_sds((HEADS, 1, s)))
    return _call_with_ride(body, name, (HEADS // hps, n), ins, in_specs, out_shape, out_specs, ("parallel", "arbitrary"), 56,
                           ride)


def _s5_disc(lr, li, ldt, br, bi):
    dt = jnp.exp(ldt)
    mag = jnp.exp(lr * dt)
    a_re = mag * jnp.cos(li * dt)
    a_im = mag * jnp.sin(li * dt)
    den = lr * lr + li * li
    f_re = ((a_re - 1.0) * lr + a_im * li) / den
    f_im = (a_im * lr - (a_re - 1.0) * li) / den
    return a_re, a_im, f_re * br - f_im * bi, f_re * bi + f_im * br


def _s5_params_fwd(lr16, li16, ldt16, br2, bi2, name):
    def body(a, b, c, d, e, o0, o1, o2, o3):
        r = _s5_disc(a[...], b[...], c[...], d[...], e[...])
        o0[...], o1[...], o2[...], o3[...] = r

    return pl.pallas_call(body, name=name, out_shape=[_sds((512, 64))] * 4)(lr16, li16, ldt16, br2, bi2)


def _s5_params_bwd(lr16, li16, ldt16, br2, bi2, da_re16, da_im16, dbb_re, dbb_im, name):
    def body(a, b, c, d, e, g0, g1, g2, g3, o_lr, o_li, o_dt, o_br, o_bi):
        _, vjp = jax.vjp(_s5_disc, a[...], b[...], c[...], d[...], e[...])
        dlr, dli, dldt, dbr, dbi = vjp((g0[...], g1[...], g2[...], g3[...]))
        grp = lambda z: z.reshape(S5_GROUPS, S5_GROUP, S5_STATE).sum(axis=1)
        o_lr[...] = grp(dlr)
        o_li[...] = grp(dli)
        o_dt[...] = jnp.sum(grp(dldt), axis=1, keepdims=True)
        o_br[...] = dbr
        o_bi[...] = dbi

    return pl.pallas_call(
        body, name=name, out_shape=[_sds((32, 64)), _sds((32, 64)), _sds((32, 1)), _sds((512, 64)), _sds((512, 64))],
    )(lr16, li16, ldt16, br2, bi2, da_re16, da_im16, dbb_re, dbb_im)


def _cmul(ar, ai, br, bi):
    return ar * br - ai * bi, ar * bi + ai * br


S5_SUPER = 4


def _scan_loop(bre_r, bim_r, ar, ai, ore_r, oim_r, reverse, xre_r=None, xim_r=None):
    s, lw = bre_r.shape
    nt = s // SUBLANES
    with_da = xre_r is not None
    shp = (SUBLANES, lw)
    row = lax.broadcasted_iota(jnp.int32, shp, 0)
    pows = [(ar, ai)]
    for _ in range(SUBLANES - 1):
        pows.append(_cmul(pows[-1][0], pows[-1][1], ar, ai))
    cm_r, cm_i = jnp.zeros(shp, F32), jnp.zeros(shp, F32)
    for r in range(SUBLANES):
        e = (SUBLANES - 1 - r) if reverse else r
        cm_r = jnp.where(row == r, jnp.broadcast_to(pows[e][0], shp), cm_r)
        cm_i = jnp.where(row == r, jnp.broadcast_to(pows[e][1], shp), cm_i)
    steps = [(1, pows[0]), (2, pows[1]), (4, pows[3])]

    def tile(it, carry):
        if with_da:
            c_r, c_i, acc_r, acc_i = carry
        else:
            c_r, c_i = carry
        r = (nt - 1 - it) if reverse else it
        off = pl.multiple_of(r * SUBLANES, SUBLANES)
        xr, xi = bre_r[pl.ds(off, SUBLANES), :], bim_r[pl.ds(off, SUBLANES), :]
        for sh, (pr, pi) in steps:
            if reverse:
                keep = row < SUBLANES - sh
                sr = jnp.where(keep, pltpu.roll(xr, SUBLANES - sh, 0), 0.0)
                si = jnp.where(keep, pltpu.roll(xi, SUBLANES - sh, 0), 0.0)
            else:
                keep = row >= sh
                sr = jnp.where(keep, pltpu.roll(xr, sh, 0), 0.0)
                si = jnp.where(keep, pltpu.roll(xi, sh, 0), 0.0)
            mr, mi = _cmul(pr, pi, sr, si)
            xr, xi = xr + mr, xi + mi
        mr, mi = _cmul(cm_r, cm_i, c_r, c_i)
        xr, xi = xr + mr, xi + mi
        ore_r[pl.ds(off, SUBLANES), :] = xr
        oim_r[pl.ds(off, SUBLANES), :] = xi
        edge = 0 if reverse else SUBLANES - 1
        c_r, c_i = xr[edge:edge + 1, :], xi[edge:edge + 1, :]
        if not with_da:
            return c_r, c_i
        fr, fi = xre_r[pl.ds(off, SUBLANES), :], xim_r[pl.ds(off, SUBLANES), :]
        poff = pl.multiple_of(jnp.maximum(r - 1, 0) * SUBLANES, SUBLANES)
        live = (r > 0).astype(F32)
        pr_last = xre_r[pl.ds(poff, SUBLANES), :][SUBLANES - 1:SUBLANES, :] * live
        pi_last = xim_r[pl.ds(poff, SUBLANES), :][SUBLANES - 1:SUBLANES, :] * live
        sr = jnp.where(row >= 1, pltpu.roll(fr, 1, 0), jnp.broadcast_to(pr_last, shp))
        si = jnp.where(row >= 1, pltpu.roll(fi, 1, 0), jnp.broadcast_to(pi_last, shp))
        return c_r, c_i, acc_r + xr * sr + xi * si, acc_i + xi * sr - xr * si

    z1 = jnp.zeros((1, lw), F32)
    if not with_da:
        lax.fori_loop(0, nt, tile, (z1, z1))
        return None
    _, _, acc_r, acc_i = lax.fori_loop(0, nt, tile, (z1, z1, jnp.zeros(shp, F32), jnp.zeros(shp, F32)))
    return jnp.sum(acc_r, axis=0, keepdims=True), jnp.sum(acc_i, axis=0, keepdims=True)


S5_ROWS = 512


def _group_compact(p):
    grp = lax.broadcasted_iota(jnp.int32, (LANES, S5_STATE), 0) // S5_GROUP
    out = jnp.zeros((LANES, S5_STATE), F32)
    for j in range(LANES // S5_GROUP):
        out = jnp.where(grp == j, p[:, S5_STATE * j:S5_STATE * (j + 1)], out)
    return out


def _s5_core_fwd(u, wb_re, wb_im, wc_re, wc_im, a_re, a_im, name, ride=None):
    s = u.shape[0]
    lw = S5_LANES // S5_SUPER
    rows = _pick(s, (S5_ROWS, 256, 128))

    def body(u_r, wbr, wbi, wcr, wci, are_r, aim_r, xre_r, xim_r, y_r, bre_s, bim_s):
        for r0 in range(0, s, rows):
            ub = u_r[r0:r0 + rows, :]
            bre_s[r0:r0 + rows, :] = _nn(ub, wbr[0])
            bim_s[r0:r0 + rows, :] = _nn(ub, wbi[0])
        _scan_loop(bre_s, bim_s, are_r[...], aim_r[...], xre_r, xim_r, False)
        for r0 in range(0, s, rows):
            y_r[r0:r0 + rows, :] = _nn(xre_r[r0:r0 + rows, :], wcr[0]) + _nn(xim_r[r0:r0 + rows, :], wci[0])

    nar = pl.BlockSpec((s, LANES), lambda k: (0, k))
    wide = pl.BlockSpec((s, lw), lambda k: (0, k))
    one = pl.BlockSpec((1, lw), lambda k: (0, k))
    wb = pl.BlockSpec((1, LANES, lw), lambda k: (k, 0, 0))
    wc = pl.BlockSpec((1, lw, LANES), lambda k: (k, 0, 0))
    return _call_with_ride(body, name, (S5_SUPER,), [u, wb_re, wb_im, wc_re, wc_im, a_re, a_im], [nar, wb, wb, wc, wc, one, one],
                           [_sds((s, S5_LANES)), _sds((s, S5_LANES)), _sds((s, S5_WIDTH))], [wide, wide, nar], ("parallel",), 56,
                           ride, scratch=[pltpu.VMEM((s, lw), F32), pltpu.VMEM((s, lw), F32)])


def _s5_core_bwd(dy0, u, du_a, x_re, x_im, wb_re, wb_im, wc_re, wc_im, a_re, a_im_neg, name):
    s = u.shape[0]
    lw = S5_LANES // S5_SUPER
    rows = _pick(s, (S5_ROWS, 256, 128))

    def body(dy_r, u_r, dua_r, xre_r, xim_r, wbr, wbi, wcr, wci, are_r, aim_r, du_r, dare_r, daim_r, dbr_r, dbi_r, dcr_r, dci_r,
             dre_s, dim_s, gre_s, gim_s):
        for r0 in range(0, s, rows):
            dyb = dy_r[r0:r0 + rows, :]
            dre_s[r0:r0 + rows, :] = _nt(dyb, wcr[0])
            dim_s[r0:r0 + rows, :] = _nt(dyb, wci[0])
        dare_r[...], daim_r[...] = _scan_loop(dre_s, dim_s, are_r[...], aim_r[...], gre_s, gim_s, True, xre_r, xim_r)
        acc = [jnp.zeros((LANES, lw), F32) for _ in range(4)]
        for r0 in range(0, s, rows):
            sl = slice(r0, r0 + rows)
            gr, gi, ub, dyb = gre_s[sl, :], gim_s[sl, :], u_r[sl, :], dy_r[sl, :]
            du_r[sl, :] = dua_r[sl, :] + _nt(gr, wbr[0]) + _nt(gi, wbi[0])
            acc = [acc[0] + _tn(ub, gr), acc[1] + _tn(ub, gi), acc[2] + _tn(dyb, xre_r[sl, :]), acc[3] + _tn(dyb, xim_r[sl, :])]
        dbr_r[...], dbi_r[...], dcr_r[...], dci_r[...] = (_group_compact(a) for a in acc)

    nar = pl.BlockSpec((s, LANES), lambda k: (0, k))
    wide = pl.BlockSpec((s, lw), lambda k: (0, k))
    one = pl.BlockSpec((1, lw), lambda k: (0, k))
    wb = pl.BlockSpec((1, LANES, lw), lambda k: (k, 0, 0))
    wc = pl.BlockSpec((1, lw, LANES), lambda k: (k, 0, 0))
    blk = pl.BlockSpec((LANES, S5_STATE), lambda k: (k, 0))
    return _call_with_ride(
        body, name, (S5_SUPER,), [dy0, u, du_a, x_re, x_im, wb_re, wb_im, wc_re, wc_im, a_re, a_im_neg],
        [nar, nar, nar, wide, wide, wb, wb, wc, wc, one, one],
        [_sds((s, S5_WIDTH)), _sds((1, S5_LANES)), _sds((1, S5_LANES))] + [_sds((S5_WIDTH, S5_STATE))] * 4,
        [nar, one, one, blk, blk, blk, blk], ("parallel",), 60, None, scratch=[pltpu.VMEM((s, lw), F32)] * 4)


def _s5_seg1(y0, u, d):
    return jax.nn.gelu(y0 + d * u)


def _s5_seg2(z, t, b):
    return z * jax.nn.sigmoid(t + b)


def _s5_post_fwd(y0, seg2, d, wglu, bglu, name):
    s = y0.shape[0]
    ts = _pick(s, (256, 128))

    def fn(ins, outs, accs, scr):
        z = _s5_seg1(ins[0][...], ins[1][...], ins[2][...])
        outs[0][...] = _s5_seg2(z, _nn(z, ins[3][...]), ins[4][...])

    return _stage(name, fn, s // ts,
                  [(y0, _rows(ts, 512)), (seg2, _rows(ts, 512)), (d, _full(d.shape)), (wglu, _full(wglu.shape)),
                   (bglu, _full(bglu.shape))], [(_sds((s, 512)), _rows(ts, 512))])[0]


def _s5_post_bwd(y0, seg2, d, wglu, bglu, dy, name):
    s = y0.shape[0]
    ts = _pick(s, (256, 128))

    def fn(ins, outs, accs, scr):
        y0_r, u_r, d_r, w_r, b_r, dy_r = ins
        z, vjp1 = jax.vjp(_s5_seg1, y0_r[...], u_r[...], d_r[...])
        t = _nn(z, w_r[...])
        _, vjp2 = jax.vjp(_s5_seg2, z, t, b_r[...])
        dz, dt, db = vjp2(dy_r[...])
        accs[0][...] += _tn(z, dt)
        accs[1][...] += db
        dy0, du, dd = vjp1(dz + _nt(dt, w_r[...]))
        accs[2][...] += dd
        outs[0][...] = dy0
        outs[1][...] = du

    return _stage(name, fn, s // ts,
                  [(y0, _rows(ts, 512)), (seg2, _rows(ts, 512)), (d, _full(d.shape)), (wglu, _full(wglu.shape)),
                   (bglu, _full(bglu.shape)), (dy, _rows(ts, 512))],
                  [(_sds((s, 512)), _rows(ts, 512)), (_sds((s, 512)), _rows(ts, 512))],
                  accs=[_sds((512, 512)), _sds((1, 512)), _sds((1, 512))])


def _gate_a(y, g):
    return y * jax.nn.silu(g)


def _gate_m(o0, o1, o2, m0, m1, m2):
    return jax.nn.sigmoid(m0) * o0 + jax.nn.sigmoid(m1) * o1 + jax.nn.sigmoid(m2) * o2


def _assemble(ybuf, o_mla, o_fox, y_s5):
    for h in range(HEADS):
        ybuf[:, 64 * h:64 * (h + 1)] = o_mla[h]
        ybuf[:, 512 + 64 * h:512 + 64 * (h + 1)] = o_fox[h]
    ybuf[:, 1024:1536] = y_s5[...]


def _gate_fwd(o_mla, o_fox, y_s5, seg3, seg4, x, wo, wout, name):
    s = x.shape[0]
    ts = _pick(s, (256, 128))

    def fn(ins, outs, accs, scr):
        om, of, ys, g_r, m_r, x_r, wo_r, wout_r = ins
        ybuf = scr[0]
        _assemble(ybuf, om, of, ys)
        a = _gate_a(ybuf[...], g_r[...])
        o = [_nn(a[:, 512 * b:512 * (b + 1)], wo_r[512 * b:512 * (b + 1), :]) for b in range(3)]
        merged = _gate_m(o[0], o[1], o[2], m_r[:, 0:1024], m_r[:, 1024:2048], m_r[:, 2048:3072])
        outs[0][...] = x_r[...] + _nn(merged, wout_r[...])

    return _stage(name, fn, s // ts,
                  [(o_mla, _heads(ts, 64)), (o_fox, _heads(ts, 64)), (y_s5, _rows(ts, 512)), (seg3, _rows(ts, 1536)),
                   (seg4, _rows(ts, 3072)), (x, _rows(ts, D_MODEL)), (wo, _full(wo.shape)), (wout, _full(wout.shape))],
                  [(_sds((s, D_MODEL)), _rows(ts, D_MODEL))], scratch=[pltpu.VMEM((ts, 1536), F32)])[0]


def _gate_bwd(o_mla, o_fox, y_s5, seg3, seg4, wo, wout, dout, name):
    s = dout.shape[0]
    ts = _pick(s, (128,))

    def fn(ins, outs, accs, scr):
        om, of, ys, g_r, m_r, wo_r, wout_r, dout_r = ins
        do_mla, do_fox, dys, dg_r, dm_r, dl_mla = outs
        dwo, dwout = accs
        ybuf, dabuf = scr
        _assemble(ybuf, om, of, ys)
        a, vjp_a = jax.vjp(_gate_a, ybuf[...], g_r[...])
        o = [_nn(a[:, 512 * b:512 * (b + 1)], wo_r[512 * b:512 * (b + 1), :]) for b in range(3)]
        ms = [m_r[:, 1024 * b:1024 * (b + 1)] for b in range(3)]
        merged, vjp_m = jax.vjp(_gate_m, *o, *ms)
        dout_v = dout_r[...]
        dwout[...] += _tn(merged, dout_v)
        cts = vjp_m(_nt(dout_v, wout_r[...]))
        for b in range(3):
            dm_r[:, 1024 * b:1024 * (b + 1)] = cts[3 + b].astype(BF16)
            dwo[512 * b:512 * (b + 1), :] += _tn(a[:, 512 * b:512 * (b + 1)], cts[b])
            dabuf[:, 512 * b:512 * (b + 1)] = _nt(cts[b], wo_r[512 * b:512 * (b + 1), :])
        dy, dg = vjp_a(dabuf[...])
        dg_r[...] = dg.astype(BF16)
        dys[...] = dy[:, 1024:1536]
        for h in range(HEADS):
            d = dy[:, 64 * h:64 * (h + 1)]
            do_mla[h] = d
            dl_mla[h] = jnp.sum(d * om[h], axis=1, keepdims=True)
            do_fox[h] = dy[:, 512 + 64 * h:512 + 64 * (h + 1)]

    return _stage(name, fn, s // ts,
                  [(o_mla, _heads(ts, 64)), (o_fox, _heads(ts, 64)), (y_s5, _rows(ts, 512)), (seg3, _rows(ts, 1536)),
                   (seg4, _rows(ts, 3072)), (wo, _full(wo.shape)), (wout, _full(wout.shape)), (dout, _rows(ts, D_MODEL))],
                  [(_sds((HEADS, s, 64)), _heads(ts, 64)), (_sds((HEADS, s, 64)), _heads(ts, 64)), (_sds((s, 512)), _rows(ts, 512)),
                   (_sds((s, 1536), BF16), _rows(ts, 1536)), (_sds((s, 3072), BF16), _rows(ts, 3072)),
                   (_sds((HEADS, s, 1)), _heads(ts, 1))],
                  accs=[_sds(wo.shape), _sds(wout.shape)], scratch=[pltpu.VMEM((ts, 1536), F32), pltpu.VMEM((ts, 1536), F32)],
                  vmem_mb=56)


def _loss_head(y, target, name):
    s = y.shape[0]
    ts = _pick(s, (256, 128))

    def fn(ins, outs, accs, scr):
        e = ins[0][...] - ins[1][...]
        outs[0][...] = e / D_MODEL
        accs[0][...] += 0.5 * jnp.sum(jnp.sum(e * e, axis=1, keepdims=True) / D_MODEL, axis=0, keepdims=True)

    r = _stage(name, fn, s // ts, [(y, _rows(ts, D_MODEL)), (target, _rows(ts, D_MODEL))],
               [(_sds((s, D_MODEL)), _rows(ts, D_MODEL))], accs=[_sds((1, 1))])
    return r[1], r[0]


IN_RANGES = ((0, 384, 0), (384, 416, 448), (416, 1952, 512), (1952, 1960, 384), (1960, IN_WIDTH, 2048))
SHARD_W = IN_WIDTH // N_DEV


def _win_pieces(d):
    lo, hi = SHARD_W * d, SHARD_W * (d + 1)
    out = []
    for a, b, p in IN_RANGES:
        s, e = max(a, lo), min(b, hi)
        while s < e:
            pad = p + (s - a)
            k = max(i for i in range(5) if SEG_OFF[i] <= pad)
            w = min(e - s, SEG_OFF[k] + SEG_W[k] - pad)
            out.append((s - lo, w, k, pad - SEG_OFF[k]))
            s += w
    return out


def _win_pad(g, name):
    _, r, _ = g.shape
    tr = 64

    def body(g_ref, o_ref):
        o_ref[...] = jnp.zeros(o_ref.shape, o_ref.dtype)
        for d in range(N_DEV):
            for dst, w, k, src in _win_pieces(d):
                o_ref[:, SEG_OFF[k] + src:SEG_OFF[k] + src + w] = g_ref[d, :, dst:dst + w]

    return pl.pallas_call(
        body, name=name, grid=(r // tr,), in_specs=[pl.BlockSpec((N_DEV, tr, SHARD_W), lambda i: (0, i, 0))],
        out_specs=pl.BlockSpec((tr, PAD_IN), lambda i: (i, 0)), out_shape=jax.ShapeDtypeStruct((r, PAD_IN), g.dtype),
        compiler_params=pltpu.CompilerParams(dimension_semantics=("parallel",)),
    )(g)


def _win_unpad(dsegs, name):
    r = dsegs[0].shape[0]
    tr = 64

    def body(*refs):
        o_ref = refs[5]
        for d in range(N_DEV):
            for dst, w, k, src in _win_pieces(d):
                o_ref[d, :, dst:dst + w] = refs[k][:, src:src + w]

    return pl.pallas_call(
        body, name=name, grid=(r // tr,), in_specs=[pl.BlockSpec((tr, SEG_W[k]), lambda i: (i, 0)) for k in range(5)],
        out_specs=pl.BlockSpec((N_DEV, tr, SHARD_W), lambda i: (0, i, 0)),
        out_shape=jax.ShapeDtypeStruct((N_DEV, r, SHARD_W), dsegs[0].dtype),
        compiler_params=pltpu.CompilerParams(dimension_semantics=("parallel",)),
    )(*dsegs)


def _pad_wq(w):
    w = w.reshape(MLA_Q_RANK, HEADS, MLA_QK)
    return jnp.pad(w, ((0, 0), (0, 0), (0, LANES - MLA_QK))).reshape(MLA_Q_RANK, HEADS * LANES)


def _unpad_wq(d):
    return d.reshape(MLA_Q_RANK, HEADS, LANES)[:, :, :MLA_QK].reshape(MLA_Q_RANK, HEADS * MLA_QK)


def _pad_wkv(w):
    w = w.reshape(MLA_KV_RANK, HEADS, MLA_NOPE + MLA_V)
    k = jnp.pad(w[:, :, :MLA_NOPE], ((0, 0), (0, 0), (0, LANES - MLA_NOPE))).reshape(MLA_KV_RANK, HEADS * LANES)
    return jnp.concatenate([k, w[:, :, MLA_NOPE:].reshape(MLA_KV_RANK, HEADS * MLA_V)], axis=1)


def _unpad_wkv(d):
    k = d[:, :HEADS * LANES].reshape(MLA_KV_RANK, HEADS, LANES)[:, :, :MLA_NOPE]
    v = d[:, HEADS * LANES:].reshape(MLA_KV_RANK, HEADS, MLA_V)
    return jnp.concatenate([k, v], axis=2).reshape(MLA_KV_RANK, HEADS * (MLA_NOPE + MLA_V))


def _pad_lanes(v, n=LANES):
    return jnp.pad(v, (0, n - v.shape[0])).reshape(1, n)


def _super_blocks(b):
    _, r, c = b.shape
    per = S5_GROUPS // S5_SUPER
    b = b.reshape(S5_SUPER, per, r, c)
    eye = jnp.eye(per, dtype=b.dtype)
    return (b[:, :, :, None, :] * eye[None, :, None, :, None]).reshape(S5_SUPER, per * r, per * c)


def _layer_params(l, w, small):
    p = {k: small[k][l] for k in small}
    q = {}
    q["norm_g"] = p["norm_g"].reshape(1, D_MODEL)
    q["mla_q_a_norm"] = p["mla_q_a_norm"].reshape(1, 256)
    q["mla_kv_a_norm"] = p["mla_kv_a_norm"].reshape(1, 128)
    q["mla_q_norm"] = _pad_lanes(p["mla_q_norm"])
    q["mla_k_norm"] = _pad_lanes(p["mla_k_norm"])
    q["fox_b_f"] = _pad_lanes(p["fox_b_f"])
    q["fox_q_norm"] = p["fox_q_norm"].reshape(1, 64)
    q["fox_k_norm"] = p["fox_k_norm"].reshape(1, 64)
    q["s5_d"] = p["s5_d"].reshape(1, 512)
    q["s5_b_glu"] = p["s5_b_glu"].reshape(1, 512)
    rep = lambda z: jnp.repeat(z, S5_GROUP, axis=0)
    q["lr16"], q["li16"] = rep(p["s5_lambda_re"]), rep(p["s5_lambda_im"])
    q["ldt16"] = rep(jnp.broadcast_to(p["s5_log_dt"][:, None], (S5_GROUPS, S5_STATE)))
    q["br2"] = p["s5_b_re"].transpose(0, 2, 1).reshape(512, 64)
    q["bi2"] = p["s5_b_im"].transpose(0, 2, 1).reshape(512, 64)
    q["c_re"], q["c_im"] = p["s5_c_re"], p["s5_c_im"]
    q["w_in"] = _win_pad(w["w_in"], f"l{l}_w_in_pad")
    q["wq"] = _pad_wq(w["mla_w_q_up"])
    q["wkv"] = _pad_wkv(w["mla_w_kv_up"])
    q.update(_late_weights(w))
    return q


def _late_weights(w):
    return {q: w[k] for q, k in (("wglu", "s5_w_glu"), ("wo", "w_branch_out"), ("wout", "w_out")) if k in w}


def _layer_fwd(l, x, tabs, q, rides=None):
    rides = rides or {}
    n = lambda s: f"l{l}_{s}"
    sv = {"x": x}
    h = _norm_fwd(x, q["norm_g"], n("norm_fwd"))
    sv["h"] = h
    seg = [_mm(h, q["w_in"], "nn", n(f"proj{k}"), b_cols=(SEG_OFF[k], SEG_W[k])) for k in range(5)]
    sv["seg"] = seg
    mq, mk, mv = _mla_prep_fwd(seg[0], tabs, q, q["wq"], q["wkv"], n("mla_prep_fwd"))
    o_mla, lse_mla = _attn_fwd(mq, mk, mv, 1.0 / math.sqrt(MLA_QK), True, n("mla_attn_fwd"), ride=rides.get("mla_attn_fwd"))
    sv.update(mq=mq, mk=mk, mv=mv, o_mla=o_mla, lse_mla=lse_mla)
    fq, fk, fv, c, ct = _fox_prep_fwd(seg[0], seg[1], q["fox_b_f"], q["fox_q_norm"], q["fox_k_norm"], n("fox_prep_fwd"))
    o_fox, lse_fox = _attn_fwd(fq, fk, fv, 1.0 / math.sqrt(FOX_DIM), False, n("fox_attn_fwd"), c=c, ct=ct,
                               ride=rides.get("fox_attn_fwd"))
    sv.update(fq=fq, fk=fk, fv=fv, c=c, ct=ct, o_fox=o_fox, lse_fox=lse_fox)
    a_re16, a_im16, bb_re, bb_im = _s5_params_fwd(q["lr16"], q["li16"], q["ldt16"], q["br2"], q["bi2"], n("s5_params_fwd"))
    a_re = a_re16.reshape(S5_GROUPS, S5_GROUP, S5_STATE)[:, 0, :].reshape(1, S5_LANES)
    a_im = a_im16.reshape(S5_GROUPS, S5_GROUP, S5_STATE)[:, 0, :].reshape(1, S5_LANES)
    wb_re = _super_blocks(bb_re.reshape(S5_GROUPS, S5_GROUP, S5_STATE)).astype(BF16)
    wb_im = _super_blocks(bb_im.reshape(S5_GROUPS, S5_GROUP, S5_STATE)).astype(BF16)
    wc_re = _super_blocks(q["c_re"].transpose(0, 2, 1)).astype(BF16)
    wc_im = _super_blocks(-q["c_im"].transpose(0, 2, 1)).astype(BF16)
    x_re, x_im, y0 = _s5_core_fwd(seg[2], wb_re, wb_im, wc_re, wc_im, a_re, a_im, n("s5_scan_fwd"), ride=rides.get("s5_scan_fwd"))
    y_s5 = _s5_post_fwd(y0, seg[2], q["s5_d"], q["wglu"], q["s5_b_glu"], n("s5_post_fwd"))
    sv.update(a_re=a_re, a_im=a_im, wb_re=wb_re, wb_im=wb_im, wc_re=wc_re, wc_im=wc_im, x_re=x_re, x_im=x_im, y0=y0, y_s5=y_s5)
    out = _gate_fwd(o_mla, o_fox, y_s5, seg[3], seg[4], x, q["wo"], q["wout"], n("gate_fwd"))
    return out, sv


def _layer_bwd(l, dout, tabs, q, sv, rides=None, g=None):
    rides = rides or {}
    n = lambda s: f"l{l}_{s}"
    seg = sv["seg"]
    g = {} if g is None else g
    (do_mla, do_fox, dy_s5, dseg3, dseg4, dl_mla, g["wo"], g["wout"]) = _gate_bwd(
        sv["o_mla"], sv["o_fox"], sv["y_s5"], seg[3], seg[4], q["wo"], q["wout"], dout, n("gate_bwd"))
    dy0, du_a, g["wglu"], g["s5_b_glu"], g["s5_d"] = _s5_post_bwd(sv["y0"], seg[2], q["s5_d"], q["wglu"], q["s5_b_glu"], dy_s5,
                                                                 n("s5_post_bwd"))
    dseg2, da_re, da_im, dbb_re, dbb_im, dc_re, dc_im = _s5_core_bwd(
        dy0, seg[2], du_a, sv["x_re"], sv["x_im"], sv["wb_re"], sv["wb_im"], sv["wc_re"], sv["wc_im"], sv["a_re"], -sv["a_im"],
        n("s5_scan_bwd"))
    g["s5_c_re"] = dc_re.reshape(S5_GROUPS, S5_GROUP, S5_STATE)
    g["s5_c_im"] = -dc_im.reshape(S5_GROUPS, S5_GROUP, S5_STATE)
    first = (jnp.arange(512) % S5_GROUP == 0).astype(F32)[:, None]
    da_re16 = jnp.repeat(da_re.reshape(S5_GROUPS, S5_STATE), S5_GROUP, axis=0) * first
    da_im16 = jnp.repeat(da_im.reshape(S5_GROUPS, S5_STATE), S5_GROUP, axis=0) * first
    dlr, dli, dldt, dbr2, dbi2 = _s5_params_bwd(q["lr16"], q["li16"], q["ldt16"], q["br2"], q["bi2"], da_re16, da_im16, dbb_re,
                                               dbb_im, n("s5_params_bwd"))
    g["s5_lambda_re"], g["s5_lambda_im"], g["s5_log_dt"] = dlr, dli, dldt.reshape(S5_GROUPS)
    g["s5_b_re"] = dbr2.reshape(S5_GROUPS, S5_GROUP, S5_STATE).transpose(0, 2, 1)
    g["s5_b_im"] = dbi2.reshape(S5_GROUPS, S5_GROUP, S5_STATE).transpose(0, 2, 1)
    dl_fox = _attn_delta(sv["fq"], sv["fk"], sv["fv"], do_fox, sv["lse_fox"], 1.0 / math.sqrt(FOX_DIM), False,
                         n("fox_attn_delta"), c=sv["c"], ct=sv["ct"], ride=rides.get("fox_attn_delta"))
    dfq, dfk, dfv, dck = _attn_bwd(sv["fq"], sv["fk"], sv["fv"], do_fox, sv["lse_fox"], dl_fox, 1.0 / math.sqrt(FOX_DIM), False,
                                   n("fox_attn_bwd"), c=sv["c"], ct=sv["ct"], ride=rides.get("fox_attn_bwd"))
    dseg1, dff, g["fox_q_norm"], g["fox_k_norm"], dbf = _fox_prep_bwd(seg[0], seg[1], q["fox_b_f"], q["fox_q_norm"], q["fox_k_norm"],
                                                                      dfq, dfk, dfv, dck, n("fox_prep_bwd"))
    g["fox_b_f"] = dbf[0, :HEADS]
    dmq, dmk, dmv = _attn_bwd(sv["mq"], sv["mk"], sv["mv"], do_mla, sv["lse_mla"], dl_mla, 1.0 / math.sqrt(MLA_QK), True,
                              n("mla_attn_bwd"), ride=rides.get("mla_attn_bwd"))
    dseg0, dqan, dkvan, dqn, dkn, g["wq"], g["wkv"] = _mla_prep_bwd(seg[0], tabs, q, q["wq"], q["wkv"], dmq, dmk, dmv, dff,
                                                                   n("mla_prep_bwd"))
    g["mla_q_a_norm"], g["mla_kv_a_norm"] = dqan, dkvan
    g["mla_q_norm"], g["mla_k_norm"] = dqn[0, :MLA_QK], dkn[0, :MLA_QK]
    dsegs = [dseg0, dseg1, dseg2, dseg3, dseg4]
    dh = None
    for k in range(5):
        dh = _mm(dsegs[k], q["w_in"], "nt", n(f"dh{k}"), acc=dh, b_cols=(SEG_OFF[k], SEG_W[k]))
    g["w_in"] = [_mm(sv["h"], dsegs[k], "tn", n(f"dwin{k}")) for k in range(5)]
    dx, g["norm_g"] = _norm_bwd(sv["x"], q["norm_g"], dh, dout, n("norm_bwd"))
    return dx, g


MESH = pl.DeviceIdType.MESH
ANY = pl.BlockSpec(memory_space=pl.ANY)


def _all_gather(blocks, name):
    n = len(blocks)

    def body(*refs):
        x_refs, out_refs = refs[:n], refs[n:2 * n]
        send_sems, recv_sems, local_sems = refs[2 * n:]
        x, y, c = lax.axis_index("x"), lax.axis_index("y"), lax.axis_index("c")
        me, sibling = (x, y, c), (x, y, 1 - c)
        chips = [(1 - x, y), (x, 1 - y), (1 - x, 1 - y)]

        def slot(a, px, py, pc):
            return out_refs[a].at[4 * px + 2 * py + pc]

        def copy(a, k, blk, to, src=None):
            return pltpu.make_async_remote_copy(src_ref=slot(a, *blk) if src is None else src, dst_ref=slot(a, *blk),
                                                send_sem=send_sems.at[7 * a + k], recv_sem=recv_sems.at[7 * a + k],
                                                device_id=to, device_id_type=MESH)

        mine = [pltpu.make_async_copy(x_refs[a], slot(a, *me), local_sems.at[a]) for a in range(n)]
        for cp in mine:
            cp.start()
        first = []
        for j, chip in enumerate(chips):
            first += [copy(a, 1 + j, me, (*chip, c), src=x_refs[a]) for a in range(n)]
        first += [copy(a, 0, me, sibling, src=x_refs[a]) for a in range(n)]
        for cp in first:
            cp.start()
        passed = []
        for j, chip in enumerate(chips):
            for a in range(n):
                copy(a, 1 + j, (*chip, c), me).wait_recv()
                passed.append(copy(a, 4 + j, (*chip, c), sibling))
                passed[-1].start()
        for a in range(n):
            copy(a, 0, sibling, me).wait_recv()
        for j, chip in enumerate(chips):
            for a in range(n):
                copy(a, 4 + j, (*chip, 1 - c), me).wait_recv()
        for cp in first + passed:
            cp.wait_send()
        for cp in mine:
            cp.wait()

    return pl.pallas_call(
        body, name=name, out_shape=[jax.ShapeDtypeStruct((N_DEV,) + b.shape, b.dtype) for b in blocks],
        in_specs=[ANY] * n, out_specs=[ANY] * n,
        scratch_shapes=[pltpu.SemaphoreType.DMA((7 * n,)), pltpu.SemaphoreType.DMA((7 * n,)), pltpu.SemaphoreType.DMA((n,))],
    )(*blocks)


def _place():
    x, y, c = lax.axis_index("x"), lax.axis_index("y"), lax.axis_index("c")
    return x, y, c, [(1 - x, y), (x, 1 - y), (1 - x, 1 - y)]


def _remote(src, dst, send, recv, k, to):
    return pltpu.make_async_remote_copy(src_ref=src, dst_ref=dst, send_sem=send.at[k], recv_sem=recv.at[k], device_id=to,
                                        device_id_type=MESH)


def _plan_gather_ici(blocks):
    n = len(blocks)

    def copies(in_refs, out_refs, send, recv, local):
        x, y, c, chips = _place()
        mine = 4 * x + 2 * y + c
        loc = [pltpu.make_async_copy(in_refs[a], out_refs[a].at[mine], local.at[a]) for a in range(n)]
        rem = [_remote(in_refs[a], out_refs[a].at[mine], send, recv, 3 * a + j, (px, py, c))
               for j, (px, py) in enumerate(chips) for a in range(n)]
        return rem, loc

    return _Plan(blocks, [jax.ShapeDtypeStruct((N_DEV,) + b.shape, b.dtype) for b in blocks], 3 * n, n, copies)


def _plan_gather_d2d(gathered):
    n = len(gathered)

    def copies(in_refs, out_refs, send, recv, local):
        x, y, c, _ = _place()
        rem = [_remote(in_refs[a].at[2 * j + c], out_refs[a].at[2 * j + c], send, recv, 4 * a + j, (x, y, 1 - c))
               for a in range(n) for j in range(4)]
        return rem, []

    return _Plan(gathered, [jax.ShapeDtypeStruct(g.shape, g.dtype) for g in gathered], 4 * n, 0, copies,
                 aliases={a: a for a in range(n)})


def _plan_reduce_sibling(parts):
    n = len(parts)

    def copies(in_refs, out_refs, send, recv, local):
        x, y, c, _ = _place()
        rem = [_remote(in_refs[a].at[2 * j + (1 - c)], out_refs[a].at[j], send, recv, 4 * a + j, (x, y, 1 - c))
               for a in range(n) for j in range(4)]
        return rem, []

    return _Plan(parts, [jax.ShapeDtypeStruct((4,) + p.shape[1:], p.dtype) for p in parts], 4 * n, 0, copies)


def _plan_reduce_chips(sums):
    n = len(sums)

    def copies(in_refs, out_refs, send, recv, local):
        x, y, c, chips = _place()
        mine = 2 * x + y
        loc = [pltpu.make_async_copy(in_refs[a].at[mine], out_refs[a].at[mine], local.at[a]) for a in range(n)]
        rem = [_remote(in_refs[a].at[2 * px + py], out_refs[a].at[mine], send, recv, 3 * a + k, (px, py, c))
               for k, (px, py) in enumerate(chips) for a in range(n)]
        return rem, loc

    return _Plan(sums, [jax.ShapeDtypeStruct(p.shape, p.dtype) for p in sums], 3 * n, n, copies)


def _add_sibling(parts, got, name):
    _, r, cc = parts.shape
    tr = _pick(r, (512, 256, 128, 64, 32, 16))
    c = lax.axis_index("c")

    def body(c_ref, p_ref, g_ref, o_ref):
        o_ref[...] = (p_ref[...] + g_ref[...]).astype(BF16)

    return pl.pallas_call(
        body, name=name, out_shape=jax.ShapeDtypeStruct((4, r, cc), BF16),
        grid_spec=pltpu.PrefetchScalarGridSpec(
            num_scalar_prefetch=1, grid=(4, r // tr),
            in_specs=[pl.BlockSpec((1, tr, cc), lambda j, i, cr: (2 * j + cr[0], i, 0)),
                      pl.BlockSpec((1, tr, cc), lambda j, i, cr: (j, i, 0))],
            out_specs=pl.BlockSpec((1, tr, cc), lambda j, i, cr: (j, i, 0))),
        compiler_params=_vmem(48),
    )(c.reshape(1).astype(jnp.int32), parts, got)


def _sum_leading(parts, name):
    k, r, cc = parts.shape
    tr = _pick(r, (512, 256, 128, 64, 32, 16, 8))

    def body(p_ref, o_ref):
        acc = p_ref[0]
        for j in range(1, k):
            acc = acc + p_ref[j]
        o_ref[...] = acc

    return pl.pallas_call(
        body, name=name, out_shape=jax.ShapeDtypeStruct((r, cc), F32), grid=(r // tr,),
        in_specs=[pl.BlockSpec((k, tr, cc), lambda i: (0, i, 0))], out_specs=pl.BlockSpec((tr, cc), lambda i: (i, 0)),
    )(parts)


def _adamw_math(w, g, m, v):
    nm = ADAM_B1 * m + (1.0 - ADAM_B1) * g
    nv = ADAM_B2 * v + (1.0 - ADAM_B2) * jnp.square(g)
    m_hat = nm / (1.0 - ADAM_B1 ** ADAM_STEP)
    v_hat = nv / (1.0 - ADAM_B2 ** ADAM_STEP)
    return -ADAM_LR * (m_hat / (jnp.sqrt(v_hat) + ADAM_EPS) + ADAM_WD * w), nm, nv


def _adamw_sum(w, contribs, m, v, name, ride=None):
    nl = len(contribs)
    k, r, cc = contribs[0].shape
    tr = _pick(r, (256, 128, 64, 32, 16))
    nb = r // tr

    def body(w_ref, *rest):
        c_refs = rest[:nl]
        m_ref, v_ref, g_ref, d_ref, nm_ref, nv_ref = rest[nl:]
        for li in range(nl):
            @pl.when(pl.program_id(0) == li)
            def _(c_ref=c_refs[li]):
                g = c_ref[0].astype(F32)
                for j in range(1, k):
                    g = g + c_ref[j].astype(F32)
                g_ref[...] = g
                d_ref[...], nm_ref[...], nv_ref[...] = _adamw_math(w_ref[...], g, m_ref[...], v_ref[...])

    spec = pl.BlockSpec((tr, cc), lambda l, i: (l * nb + i, 0))
    cspec = pl.BlockSpec((k, tr, cc), lambda l, i: (0, i, 0))
    return _call_with_ride(body, name, (nl, nb), [w, *contribs, m, v], [spec] + [cspec] * nl + [spec, spec],
                           [jax.ShapeDtypeStruct(w.shape, F32)] * 4, [spec] * 4, ("parallel", "parallel"), 48, ride)


def _adamw_many(ws, gs, ms, vs, name):
    n = len(ws)

    def body(*refs):
        w_r, g_r, m_r, v_r = refs[:n], refs[n:2 * n], refs[2 * n:3 * n], refs[3 * n:4 * n]
        d_r, nm_r, nv_r = refs[4 * n:5 * n], refs[5 * n:6 * n], refs[6 * n:7 * n]
        for a in range(n):
            d_r[a][...], nm_r[a][...], nv_r[a][...] = _adamw_math(w_r[a][...], g_r[a][...], m_r[a][...], v_r[a][...])

    shapes = [jax.ShapeDtypeStruct(w.shape, F32) for w in ws]
    res = pl.pallas_call(body, name=name, out_shape=shapes * 3,
                         compiler_params=pltpu.CompilerParams(vmem_limit_bytes=56 * 1024 * 1024))(*ws, *gs, *ms, *vs)
    return res[:n], res[n:2 * n], res[2 * n:]


def _pack_rows(flat, lanes, row_mult):
    n = flat.shape[-1]
    rows = -(-n // lanes)
    rows = -(-rows // row_mult) * row_mult
    pad = rows * lanes - n
    if pad:
        flat = jnp.pad(flat, [(0, 0)] * (flat.ndim - 1) + [(0, pad)])
    return flat.reshape(flat.shape[:-1] + (rows, lanes))


def _rope_tables(positions):
    inv = 1.0 / (ROPE_THETA ** (jnp.arange(0, MLA_ROPE, 2, dtype=F32) / MLA_ROPE))
    ang = positions.astype(F32)[:, None] * inv
    cos, sin = jnp.cos(ang), jnp.sin(ang)
    s = positions.shape[0]
    z = lambda n: jnp.zeros((s, n), F32)
    c = jnp.concatenate([jnp.ones((s, 64), F32), cos, cos, z(32)], axis=1)
    sa = jnp.concatenate([z(64), -sin, z(48)], axis=1)
    sb = jnp.concatenate([z(80), sin, z(32)], axis=1)
    return c, sa, sb


def _full_weights(gathered, names=SHARDED):
    full = {}
    for k, g in zip(names, gathered):
        _, r, c = g.shape
        if k == "w_in":
            full[k] = g
        else:
            full[k] = g.transpose(1, 0, 2).reshape(r, N_DEV * c) if k in COL_SHARDED else g.reshape(N_DEV * r, c)
    return full


EARLY = ("s5_w_glu", "w_branch_out", "w_out")
LATE = ("w_in", "mla_w_q_up", "mla_w_kv_up")


def _owner_major(g, names, tag):
    parts = []
    for k in names:
        if k == "w_in":
            parts.append(_win_unpad(g["w_in"], f"{tag}_w_in_unpad"))
            continue
        big = {"mla_w_q_up": lambda: _unpad_wq(g["wq"]), "mla_w_kv_up": lambda: _unpad_wkv(g["wkv"]), "s5_w_glu": lambda: g["wglu"],
               "w_branch_out": lambda: g["wo"], "w_out": lambda: g["wout"]}[k]()
        r, c = big.shape
        if k in COL_SHARDED:
            parts.append(big.reshape(r, N_DEV, c // N_DEV).transpose(1, 0, 2))
        else:
            parts.append(big.reshape(N_DEV, r // N_DEV, c))
    return parts


def _device_step(x, positions, target, shards, small):
    tabs = _rope_tables(positions)
    box = {}
    q0 = _layer_params(0, _full_weights(_all_gather(shards[0][:3], "gather_weights_l0"), LATE), small)

    def arrived(o):
        box.update(w1_b=o[:5])
        q0.update(_late_weights(_full_weights(o[5:], EARLY)))

    rides = {
        "mla_attn_fwd": _Ride(lambda: _plan_gather_ici(shards[1][:1]), lambda o: box.update(ici_a=o)),
        "fox_attn_fwd": _Ride(lambda: _join([_plan_gather_ici(shards[1][1:]), _plan_gather_d2d(box["ici_a"]),
                                             _plan_gather_ici(shards[0][3:])]),
                              lambda o: box.update(ici_b=o[:5], w1_a=o[5:6], ici_0=o[6:])),
        "s5_scan_fwd": _Ride(lambda: _join([_plan_gather_d2d(box["ici_b"]), _plan_gather_d2d(box["ici_0"])]), arrived),
    }
    h, sv0 = _layer_fwd(0, x, tabs, q0, rides)
    q1 = _layer_params(1, _full_weights(list(box["w1_a"]) + list(box["w1_b"])), small)
    h, sv1 = _layer_fwd(1, h, tabs, q1)
    loss, d = _loss_head(h, target, "loss_head")
    d, g1 = _layer_bwd(1, d, tabs, q1, sv1)
    parts1 = _owner_major(g1, SHARDED, "l1")

    def chips_plan(names, parts, got, tag):
        return _plan_reduce_chips([_add_sibling(p, g, f"reduce_add_{tag}_{k}") for k, p, g in zip(names, parts, got)])

    g0 = {}
    rides = {
        "fox_attn_delta": _Ride(lambda: _plan_reduce_sibling(parts1), lambda o: box.update(got1=o)),
        "fox_attn_bwd": _Ride(lambda: _join([chips_plan(SHARDED, parts1, box["got1"], "l1"),
                                             _plan_reduce_sibling(box.setdefault("early0", _owner_major(g0, EARLY, "l0")))]),
                              lambda o: box.update(contribs1=o[:6], got0e=o[6:])),
        "mla_attn_bwd": _Ride(lambda: chips_plan(EARLY, box["early0"], box["got0e"], "l0"), lambda o: box.update(contribs0e=o)),
    }
    d, _ = _layer_bwd(0, d, tabs, q0, sv0, rides, g0)
    late0 = _owner_major(g0, LATE, "l0")
    got0 = _run_plan(_plan_reduce_sibling(late0), "reduce_sibling_l0")
    contribs0 = list(_run_plan(chips_plan(LATE, late0, got0, "l0"), "reduce_chips_l0")) + list(box["contribs0e"])
    return loss[0, 0], d, [g0, g1], [contribs0, box["contribs1"]]


def kernel(x, positions, norm_g, w_in, mla_q_a_norm, mla_w_q_up, mla_kv_a_norm, mla_w_kv_up, mla_q_norm, mla_k_norm, fox_b_f, fox_q_norm, fox_k_norm, s5_lambda_re, s5_lambda_im, s5_log_dt, s5_b_re, s5_b_im, s5_c_re, s5_c_im, s5_d, s5_w_glu, s5_b_glu, w_branch_out, w_out, loss_target, m_norm_g, m_w_in, m_mla_q_a_norm, m_mla_w_q_up, m_mla_kv_a_norm, m_mla_w_kv_up, m_mla_q_norm, m_mla_k_norm, m_fox_b_f, m_fox_q_norm, m_fox_k_norm, m_s5_lambda_re, m_s5_lambda_im, m_s5_log_dt, m_s5_b_re, m_s5_b_im, m_s5_c_re, m_s5_c_im, m_s5_d, m_s5_w_glu, m_s5_b_glu, m_w_branch_out, m_w_out, v_norm_g, v_w_in, v_mla_q_a_norm, v_mla_w_q_up, v_mla_kv_a_norm, v_mla_w_kv_up, v_mla_q_norm, v_mla_k_norm, v_fox_b_f, v_fox_q_norm, v_fox_k_norm, v_s5_lambda_re, v_s5_lambda_im, v_s5_log_dt, v_s5_b_re, v_s5_b_im, v_s5_c_re, v_s5_c_im, v_s5_d, v_s5_w_glu, v_s5_b_glu, v_w_branch_out, v_w_out):
    env = dict(locals())
    wts = {k: env[k] for k in WEIGHTS}
    mom = {k: env["m_" + k] for k in WEIGHTS}
    var = {k: env["v_" + k] for k in WEIGHTS}

    shards = [[wts[k][l].astype(BF16) for k in SHARDED] for l in range(DEPTH)]
    small = {k: wts[k] for k in SMALL}
    loss, dx, grads, contribs = _device_step(x[0], positions[0], loss_target[0], shards, small)
    loss = lax.psum(loss, ("x", "y", "c"))

    two_d = {k: (wts[k].shape[0] * wts[k].shape[1], wts[k].shape[2]) for k in SHARDED}
    sm = {k: jnp.stack([g[k] for g in grads]).reshape(wts[k].shape) for k in SMALL}
    small_block = _pack_rows(jnp.concatenate([sm[k].reshape(-1) for k in SMALL]), LANES, 256)
    box = {}
    rides = {SHARDED[0]: _Ride(lambda: _plan_gather_ici([small_block]), lambda o: box.update(ici=o)),
             SHARDED[1]: _Ride(lambda: _plan_gather_d2d(box["ici"]), lambda o: box.update(all=o))}
    grad_out, delta_out, m_out, v_out = {}, {}, {}, {}
    for i, k in enumerate(SHARDED):
        shp = wts[k].shape
        res = _adamw_sum(wts[k].reshape(two_d[k]), [contribs[l][i] for l in range(DEPTH)], mom[k].reshape(two_d[k]),
                         var[k].reshape(two_d[k]), f"adamw_{k}", ride=rides.get(k))
        grad_out[k], delta_out[k], m_out[k], v_out[k] = (z.reshape(shp) for z in res)
    g_small = _sum_leading(box["all"][0], "sum_small_grads").reshape(-1)
    off = 0
    for k in SMALL:
        cnt = int(np.prod(wts[k].shape))
        grad_out[k] = g_small[off:off + cnt].reshape(wts[k].shape)
        off += cnt
    flat2 = lambda a: a.reshape(-1, a.shape[-1])
    d_s, m_s, v_s = _adamw_many([flat2(wts[k]) for k in SMALL], [flat2(grad_out[k]) for k in SMALL],
                                [flat2(mom[k]) for k in SMALL], [flat2(var[k]) for k in SMALL], "adamw_small")
    for i, k in enumerate(SMALL):
        delta_out[k], m_out[k], v_out[k] = (z[i].reshape(wts[k].shape) for z in (d_s, m_s, v_s))

    return (loss, dx[None], *[grad_out[k] for k in WEIGHTS], *[delta_out[k] for k in WEIGHTS],
            *[m_out[k] for k in WEIGHTS], *[v_out[k] for k in WEIGHTS])
```

```python
import functools
import math

import jax
import jax.numpy as jnp
import numpy as np
from jax import lax
from jax.experimental import pallas as pl
from jax.experimental.pallas import tpu as pltpu

F32 = jnp.float32
BF16 = jnp.bfloat16

D_MODEL = 1024
DEPTH = 2
CHUNK = 64
EPS = 1e-6
HEADS = 8
MLA_NOPE, MLA_ROPE, MLA_V = 64, 32, 64
MLA_Q_RANK, MLA_KV_RANK = 256, 128
MLA_QK = MLA_NOPE + MLA_ROPE
ROPE_THETA = 10000.0
FOX_DIM = 64
S5_WIDTH, S5_GROUP, S5_GROUPS, S5_STATE = 512, 16, 32, 64
S5_LANES = S5_GROUPS * S5_STATE
IN_WIDTH = 7080
N_DEV = 8
LANES = 128
SUBLANES = 8

ADAM_LR, ADAM_B1, ADAM_B2, ADAM_EPS, ADAM_WD, ADAM_STEP = 0.001, 0.9, 0.999, 1e-08, 0.01, 10

SEG_W = (512, 1536, 512, 1536, 3072)
SEG_OFF = (0, 512, 2048, 2560, 4096)
PAD_IN = 7168
NEG = -1e30

SHARDED = ("w_in", "mla_w_q_up", "mla_w_kv_up", "s5_w_glu", "w_branch_out", "w_out")
COL_SHARDED = ("w_in", "mla_w_q_up", "mla_w_kv_up")
SMALL = ("norm_g", "mla_q_a_norm", "mla_kv_a_norm", "mla_q_norm", "mla_k_norm", "fox_b_f", "fox_q_norm", "fox_k_norm",
         "s5_lambda_re", "s5_lambda_im", "s5_log_dt", "s5_b_re", "s5_b_im", "s5_c_re", "s5_c_im", "s5_d", "s5_b_glu")
WEIGHTS = ("norm_g", "w_in", "mla_q_a_norm", "mla_w_q_up", "mla_kv_a_norm", "mla_w_kv_up", "mla_q_norm", "mla_k_norm",
           "fox_b_f", "fox_q_norm", "fox_k_norm", "s5_lambda_re", "s5_lambda_im", "s5_log_dt", "s5_b_re", "s5_b_im",
           "s5_c_re", "s5_c_im", "s5_d", "s5_w_glu", "s5_b_glu", "w_branch_out", "w_out")


def _pick(n, cands):
    for c in cands:
        if n % c == 0:
            return c
    return n


def _vmem(mb):
    return pltpu.CompilerParams(vmem_limit_bytes=mb * 1024 * 1024)


def _dot(a, b, dims):
    return lax.dot_general(a.astype(BF16), b.astype(BF16), (dims, ((), ())), preferred_element_type=F32)


def _nn(a, b):
    return _dot(a, b, ((1,), (0,)))


def _nt(a, b):
    return _dot(a, b, ((1,), (1,)))


def _tn(a, b):
    return _dot(a, b, ((0,), (0,)))


def _rms(x, g, n=None):
    n = x.shape[-1] if n is None else n
    return x * lax.rsqrt(jnp.sum(x * x, axis=-1, keepdims=True) / n + EPS) * g


def _rope(t, c, sa, sb):
    return t * c + pltpu.roll(t, LANES - 16, 1) * sa + pltpu.roll(t, 16, 1) * sb


def _rope_t(d, c, sa, sb):
    return d * c + pltpu.roll(d * sa, 16, 1) + pltpu.roll(d * sb, LANES - 16, 1)


def _mm(a, b, mode, name, acc=None, b_cols=None, ride=None):
    if mode == "tn":
        kd, m = a.shape
    else:
        m, kd = a.shape
    b_off, b_w = b_cols if b_cols is not None else (0, b.shape[1])
    n = b.shape[0] if mode == "nt" else b_w
    tm, tn, tk = _pick(m, (1024, 512, 256, 128)), _pick(n, (1024, 512, 256, 128)), _pick(kd, (1024, 512, 256, 128))
    nk = kd // tk
    if mode == "tn":
        a_spec = pl.BlockSpec((tk, tm), lambda i, j, k: (k, i))
    else:
        a_spec = pl.BlockSpec((tm, tk), lambda i, j, k: (i, k))
    if mode == "nt":
        assert b_off % tk == 0
        b_spec = pl.BlockSpec((tn, tk), lambda i, j, k: (j, k + b_off // tk))
    else:
        assert b_off % tn == 0
        b_spec = pl.BlockSpec((tk, tn), lambda i, j, k: (k, j + b_off // tn))
    dims = {"nn": ((1,), (0,)), "nt": ((1,), (1,)), "tn": ((0,), (0,))}[mode]
    o_spec = pl.BlockSpec((tm, tn), lambda i, j, k: (i, j))
    has_acc = acc is not None

    def body(*refs):
        if has_acc:
            a_ref, b_ref, c_ref, o_ref = refs
        else:
            a_ref, b_ref, o_ref = refs
        k = pl.program_id(2)
        prod = _dot(a_ref[...], b_ref[...], dims)

        @pl.when(k == 0)
        def _():
            o_ref[...] = prod + c_ref[...] if has_acc else prod

        @pl.when(k > 0)
        def _():
            o_ref[...] += prod

    ins = [a, b] + ([acc] if has_acc else [])
    in_specs = [a_spec, b_spec] + ([o_spec] if has_acc else [])
    return _call_with_ride(body, name, (m // tm, n // tn, nk), ins, in_specs, [jax.ShapeDtypeStruct((m, n), F32)], [o_spec],
                           ("parallel", "parallel", "arbitrary"), 48, ride)[0]


def _stage(name, fn, n_steps, ins, outs, accs=(), scratch=(), vmem_mb=48):
    n_in, n_out, n_acc = len(ins), len(outs), len(accs)

    def body(*refs):
        in_refs = refs[:n_in]
        out_refs = refs[n_in:n_in + n_out]
        acc_refs = refs[n_in + n_out:n_in + n_out + n_acc]
        scr = refs[n_in + n_out + n_acc:]
        if n_acc:
            @pl.when(pl.program_id(0) == 0)
            def _():
                for r in acc_refs:
                    r[...] = jnp.zeros(r.shape, r.dtype)
        fn(in_refs, out_refs, acc_refs, scr)

    acc_specs = [pl.BlockSpec(a.shape, functools.partial(lambda i, nd: (0,) * nd, nd=len(a.shape))) for a in accs]
    res = pl.pallas_call(
        body, name=name, grid=(n_steps,),
        in_specs=[s for _, s in ins], out_specs=[s for _, s in outs] + acc_specs,
        out_shape=[s for s, _ in outs] + list(accs), scratch_shapes=list(scratch),
        compiler_params=pltpu.CompilerParams(dimension_semantics=("arbitrary",),
                                             vmem_limit_bytes=vmem_mb * 1024 * 1024),
    )(*[a for a, _ in ins])
    return res


def _rows(ts, w, j=0):
    return pl.BlockSpec((ts, w), lambda i: (i, j))


def _rows_rev(ts, w, n, j=0):
    return pl.BlockSpec((ts, w), lambda i: (n - 1 - i, j))


def _heads(ts, d):
    return pl.BlockSpec((HEADS, ts, d), lambda i: (0, i, 0))


def _heads_rev(ts, d, n):
    return pl.BlockSpec((HEADS, ts, d), lambda i: (0, n - 1 - i, 0))


def _full(shape):
    nd = len(shape)
    return pl.BlockSpec(tuple(shape), lambda i: (0,) * nd)


def _sds(shape, dtype=F32):
    return jax.ShapeDtypeStruct(tuple(shape), dtype)


def _norm_fwd(x, g, name):
    s = x.shape[0]
    ts = _pick(s, (256, 128))

    def fn(ins, outs, accs, scr):
        outs[0][...] = _rms(ins[0][...], ins[1][...]).astype(BF16)

    return _stage(name, fn, s // ts, [(x, _rows(ts, D_MODEL)), (g, _full(g.shape))],
                  [(_sds((s, D_MODEL), BF16), _rows(ts, D_MODEL))])[0]


def _norm_bwd(x, g, dh, dres, name):
    s = x.shape[0]
    ts = _pick(s, (256, 128))

    def fn(ins, outs, accs, scr):
        _, vjp = jax.vjp(_rms, ins[0][...], ins[1][...])
        dx, dg = vjp(ins[2][...])
        outs[0][...] = dx + ins[3][...]
        accs[0][...] += dg

    r = _stage(name, fn, s // ts,
               [(x, _rows(ts, D_MODEL)), (g, _full(g.shape)), (dh, _rows(ts, D_MODEL)), (dres, _rows(ts, D_MODEL))],
               [(_sds((s, D_MODEL)), _rows(ts, D_MODEL))], accs=[_sds((1, D_MODEL))])
    return r[0], r[1]


def _mla_q(qraw, c, sa, sb, qn):
    return _rms(_rope(qraw, c, sa, sb), qn, MLA_QK)


def _mla_prep_fwd(seg0, tabs, p, wq, wkv, name):
    s = seg0.shape[0]
    ts = _pick(s, (256, 128))

    def fn(ins, outs, accs, scr):
        blk, cos, sa, sb, qan, kvan, qn, kn, wq_r, wkv_r = ins
        b = blk[...]
        cq, ckv, kt = b[:, :256], b[:, 256:384], b[:, 384:512]
        lane = lax.broadcasted_iota(jnp.int32, kt.shape, 1)
        kpe = jnp.where(lane >= 64, kt, 0.0)
        q_raw = _nn(_rms(cq, qan[...]), wq_r[...])
        kv_raw = _nn(_rms(ckv, kvan[...]), wkv_r[...])
        c, a, bb = cos[...], sa[...], sb[...]
        for h in range(HEADS):
            outs[0][h] = _mla_q(q_raw[:, LANES * h:LANES * (h + 1)], c, a, bb, qn[...]).astype(BF16)
            outs[1][h] = _mla_q(kv_raw[:, LANES * h:LANES * (h + 1)] + kpe, c, a, bb, kn[...]).astype(BF16)
            outs[2][h] = kv_raw[:, 1024 + 64 * h:1024 + 64 * (h + 1)].astype(BF16)

    consts = [p["mla_q_a_norm"], p["mla_kv_a_norm"], p["mla_q_norm"], p["mla_k_norm"], wq, wkv]
    return _stage(name, fn, s // ts,
                  [(seg0, _rows(ts, 512))] + [(t, _rows(ts, LANES)) for t in tabs] + [(a, _full(a.shape)) for a in consts],
                  [(_sds((HEADS, s, LANES), BF16), _heads(ts, LANES)), (_sds((HEADS, s, LANES), BF16), _heads(ts, LANES)),
                   (_sds((HEADS, s, 64), BF16), _heads(ts, 64))])


def _mla_prep_bwd(seg0, tabs, p, wq, wkv, dq, dk, dv, dff, name):
    s = seg0.shape[0]
    ts = _pick(s, (256, 128))

    def fn(ins, outs, accs, scr):
        blk, cos, sa, sb, qan, kvan, qn, kn, wq_r, wkv_r, dq_r, dk_r, dv_r, dff_r = ins
        dqan, dkvan, dqn, dkn, dwq, dwkv = accs
        dqraw_s, dkvraw_s = scr
        b = blk[...]
        cq, ckv, kt = b[:, :256], b[:, 256:384], b[:, 384:512]
        lane = lax.broadcasted_iota(jnp.int32, kt.shape, 1)
        kpe = jnp.where(lane >= 64, kt, 0.0)
        cqn, vjp_cq = jax.vjp(_rms, cq, qan[...])
        ckvn, vjp_ckv = jax.vjp(_rms, ckv, kvan[...])
        q_raw = _nn(cqn, wq_r[...])
        kv_raw = _nn(ckvn, wkv_r[...])
        c, a, bb = cos[...], sa[...], sb[...]

        def head_bwd(raw, gain, d):
            t = _rope(raw, c, a, bb)
            _, vjp = jax.vjp(functools.partial(_rms, n=MLA_QK), t, gain)
            dt, dgain = vjp(d)
            return _rope_t(dt, c, a, bb), dgain

        dkpe = jnp.zeros(kt.shape, F32)
        for h in range(HEADS):
            dqh, dg = head_bwd(q_raw[:, LANES * h:LANES * (h + 1)], qn[...], dq_r[h])
            dqn[...] += dg
            dqraw_s[:, LANES * h:LANES * (h + 1)] = dqh
            dkh, dg = head_bwd(kv_raw[:, LANES * h:LANES * (h + 1)] + kpe, kn[...], dk_r[h])
            dkn[...] += dg
            dkvraw_s[:, LANES * h:LANES * (h + 1)] = dkh
            dkpe = dkpe + dkh
            dkvraw_s[:, 1024 + 64 * h:1024 + 64 * (h + 1)] = dv_r[h]
        dq_raw = dqraw_s[...]
        dkv_raw = dkvraw_s[...]
        dwq[...] += _tn(cqn, dq_raw)
        dwkv[...] += _tn(ckvn, dkv_raw)
        dcq, dg = vjp_cq(_nt(dq_raw, wq_r[...]))
        dqan[...] += dg
        dckv, dg = vjp_ckv(_nt(dkv_raw, wkv_r[...]))
        dkvan[...] += dg
        outs[0][:, 0:256] = dcq.astype(BF16)
        outs[0][:, 256:384] = dckv.astype(BF16)
        outs[0][:, 384:512] = (jnp.where(lane >= 64, dkpe, 0.0) + dff_r[...]).astype(BF16)

    consts = [p["mla_q_a_norm"], p["mla_kv_a_norm"], p["mla_q_norm"], p["mla_k_norm"], wq, wkv]
    return _stage(name, fn, s // ts,
                  [(seg0, _rows(ts, 512))] + [(t, _rows(ts, LANES)) for t in tabs] + [(a, _full(a.shape)) for a in consts]
                  + [(dq, _heads(ts, LANES)), (dk, _heads(ts, LANES)), (dv, _heads(ts, 64)), (dff, _rows(ts, LANES))],
                  [(_sds((s, 512), BF16), _rows(ts, 512))],
                  accs=[_sds((1, 256)), _sds((1, 128)), _sds((1, LANES)), _sds((1, LANES)), _sds(wq.shape), _sds(wkv.shape)],
                  scratch=[pltpu.VMEM((ts, 1024), F32), pltpu.VMEM((ts, 1536), F32)])


def _fox_prep_fwd(seg0, seg1, bf, qn, kn, name):
    s = seg0.shape[0]
    ts = _pick(s, (256, 128))
    steps = int(math.log2(ts))

    def fn(ins, outs, accs, scr):
        kt_r, x_r, bf_r, qn_r, kn_r = ins
        carry = scr[0]

        @pl.when(pl.program_id(0) == 0)
        def _():
            carry[...] = jnp.zeros(carry.shape, F32)

        x = x_r[...]
        for h in range(HEADS):
            outs[0][h] = _rms(x[:, 64 * h:64 * (h + 1)], qn_r[...]).astype(BF16)
            outs[1][h] = _rms(x[:, 512 + 64 * h:512 + 64 * (h + 1)], kn_r[...]).astype(BF16)
            outs[2][h] = x[:, 1024 + 64 * h:1024 + 64 * (h + 1)].astype(BF16)
        kt = kt_r[...]
        lane = lax.broadcasted_iota(jnp.int32, kt.shape, 1)
        row = lax.broadcasted_iota(jnp.int32, kt.shape, 0)
        cs = jnp.where(lane < HEADS, jax.nn.log_sigmoid(kt + bf_r[...]), 0.0)
        for k in range(steps):
            sh = 1 << k
            cs = cs + jnp.where(row >= sh, pltpu.roll(cs, sh, 0), 0.0)
        cs = cs + carry[0:1, :]
        outs[3][...] = cs
        outs[4][...] = cs.T[0:HEADS, :]
        carry[0:1, :] = cs[ts - 1:ts, :]

    return _stage(name, fn, s // ts,
                  [(seg0, _rows(ts, LANES, 3)), (seg1, _rows(ts, 1536)), (bf, _full(bf.shape)), (qn, _full(qn.shape)),
                   (kn, _full(kn.shape))],
                  [(_sds((HEADS, s, 64), BF16), _heads(ts, 64)), (_sds((HEADS, s, 64), BF16), _heads(ts, 64)),
                   (_sds((HEADS, s, 64), BF16), _heads(ts, 64)), (_sds((s, LANES)), _rows(ts, LANES)),
                   (_sds((HEADS, s)), pl.BlockSpec((HEADS, ts), lambda i: (0, i)))],
                  scratch=[pltpu.VMEM((SUBLANES, LANES), F32)])


def _fox_prep_bwd(seg0, seg1, bf, qn, kn, dq, dk, dv, dck, name):
    s = seg0.shape[0]
    ts = _pick(s, (256, 128))
    n = s // ts
    steps = int(math.log2(ts))

    def fn(ins, outs, accs, scr):
        kt_r, x_r, bf_r, qn_r, kn_r, dq_r, dk_r, dv_r, dck_r = ins
        dqn, dkn, dbf = accs
        carry, dbuf = scr

        @pl.when(pl.program_id(0) == 0)
        def _():
            carry[...] = jnp.zeros(carry.shape, F32)

        x = x_r[...]
        for h in range(HEADS):
            _, vjp = jax.vjp(_rms, x[:, 64 * h:64 * (h + 1)], qn_r[...])
            d, dg = vjp(dq_r[h])
            dbuf[:, 64 * h:64 * (h + 1)] = d
            dqn[...] += dg
            _, vjp = jax.vjp(_rms, x[:, 512 + 64 * h:512 + 64 * (h + 1)], kn_r[...])
            d, dg = vjp(dk_r[h])
            dbuf[:, 512 + 64 * h:512 + 64 * (h + 1)] = d
            dkn[...] += dg
            dbuf[:, 1024 + 64 * h:1024 + 64 * (h + 1)] = dv_r[h]
        outs[0][...] = dbuf[...].astype(BF16)
        dc = dck_r[...].reshape(HEADS, ts)
        dc = jnp.concatenate([dc, jnp.zeros((LANES - HEADS, ts), F32)], axis=0).T
        row = lax.broadcasted_iota(jnp.int32, dc.shape, 0)
        lane = lax.broadcasted_iota(jnp.int32, dc.shape, 1)
        for k in range(steps):
            sh = 1 << k
            dc = dc + jnp.where(row < ts - sh, pltpu.roll(dc, ts - sh, 0), 0.0)
        dc = dc + carry[0:1, :]
        carry[0:1, :] = dc[0:1, :]
        dff = jnp.where(lane < HEADS, dc * jax.nn.sigmoid(-(kt_r[...] + bf_r[...])), 0.0)
        outs[1][...] = dff
        dbf[...] += jnp.sum(dff, axis=0, keepdims=True)

    return _stage(name, fn, n,
                  [(seg0, _rows_rev(ts, LANES, n, 3)), (seg1, _rows_rev(ts, 1536, n)), (bf, _full(bf.shape)),
                   (qn, _full(qn.shape)), (kn, _full(kn.shape)), (dq, _heads_rev(ts, 64, n)), (dk, _heads_rev(ts, 64, n)),
                   (dv, _heads_rev(ts, 64, n)), (dck, pl.BlockSpec((HEADS, 1, ts), lambda i: (0, 0, n - 1 - i)))],
                  [(_sds((s, 1536), BF16), _rows_rev(ts, 1536, n)), (_sds((s, LANES)), _rows_rev(ts, LANES, n))],
                  accs=[_sds((1, 64)), _sds((1, 64)), _sds((1, LANES))],
                  scratch=[pltpu.VMEM((SUBLANES, LANES), F32), pltpu.VMEM((ts, 1536), F32)])


def _allowed(i, j, t, chunk_causal):
    qpos = i * t + lax.broadcasted_iota(jnp.int32, (t, t), 0)
    kpos = j * t + lax.broadcasted_iota(jnp.int32, (t, t), 1)
    if chunk_causal:
        return (kpos // CHUNK) <= (qpos // CHUNK)
    return kpos <= qpos


def _pick_col(c_blk, h):
    lane = lax.broadcasted_iota(jnp.int32, c_blk.shape, 1)
    return jnp.sum(jnp.where(lane == h, c_blk, 0.0), axis=1, keepdims=True)


class _Ride:
    def __init__(self, make, take):
        self.make, self.take = make, take


class _Plan:
    def __init__(self, ins, out_shapes, n_remote, n_local, copies, aliases=None):
        self.ins, self.out_shapes, self.n_remote, self.n_local = list(ins), list(out_shapes), n_remote, n_local
        self.copies, self.aliases = copies, dict(aliases or {})

    def scratch(self):
        return [pltpu.SemaphoreType.DMA((self.n_remote,)), pltpu.SemaphoreType.DMA((self.n_remote,)),
                pltpu.SemaphoreType.DMA((max(self.n_local, 1),))]

    def start(self, in_refs, out_refs, sems):
        remote, local = self.copies(in_refs, out_refs, *sems)
        for cp in local + remote:
            cp.start()

    def wait(self, in_refs, out_refs, sems):
        remote, local = self.copies(in_refs, out_refs, *sems)
        for cp in remote:
            cp.wait()
        for cp in local:
            cp.wait()


class _Off:
    def __init__(self, ref, off):
        self.ref, self.off, self.at = ref, off, self

    def __getitem__(self, k):
        return self.ref.at[k + self.off]


def _join(plans):
    ins = [a for p in plans for a in p.ins]
    outs = [o for p in plans for o in p.out_shapes]
    aliases, i0, o0 = {}, 0, 0
    for p in plans:
        aliases.update({i0 + i: o0 + o for i, o in p.aliases.items()})
        i0, o0 = i0 + len(p.ins), o0 + len(p.out_shapes)

    def copies(in_refs, out_refs, send, recv, local):
        rem, loc, i0, o0, r0, l0 = [], [], 0, 0, 0, 0
        for p in plans:
            r, l = p.copies(in_refs[i0:i0 + len(p.ins)], out_refs[o0:o0 + len(p.out_shapes)], _Off(send, r0), _Off(recv, r0),
                            _Off(local, l0))
            rem, loc = rem + r, loc + l
            i0, o0, r0, l0 = i0 + len(p.ins), o0 + len(p.out_shapes), r0 + p.n_remote, l0 + p.n_local
        return rem, loc

    return _Plan(ins, outs, sum(p.n_remote for p in plans), sum(p.n_local for p in plans), copies, aliases)


def _call_with_ride(core, name, grid, ins, in_specs, out_shape, out_specs, semantics, vmem_mb, ride, scratch=()):
    n_in, n_out, n_scr = len(ins), len(out_shape), len(scratch)
    if ride is None:
        return pl.pallas_call(
            core, name=name, grid=grid, in_specs=in_specs, out_specs=out_specs, out_shape=out_shape,
            scratch_shapes=list(scratch),
            compiler_params=pltpu.CompilerParams(dimension_semantics=semantics, vmem_limit_bytes=vmem_mb * 1024 * 1024),
        )(*ins)
    plan = ride.make()
    ci, co = len(plan.ins), len(plan.out_shapes)

    def body(*refs):
        c_in = refs[n_in:n_in + ci]
        a_out = refs[n_in + ci:n_in + ci + n_out]
        c_out = refs[n_in + ci + n_out:n_in + ci + n_out + co]
        own = refs[n_in + ci + n_out + co:n_in + ci + n_out + co + n_scr]
        sems = refs[n_in + ci + n_out + co + n_scr:]
        ids = [pl.program_id(d) for d in range(len(grid))]
        first = functools.reduce(jnp.logical_and, [i == 0 for i in ids])
        last = functools.reduce(jnp.logical_and, [i == g - 1 for i, g in zip(ids, grid)])

        @pl.when(first)
        def _():
            plan.start(c_in, c_out, sems)

        core(*refs[:n_in], *a_out, *own)

        @pl.when(last)
        def _():
            plan.wait(c_in, c_out, sems)

    res = pl.pallas_call(
        body, name=name, grid=grid, in_specs=list(in_specs) + [ANY] * ci, out_specs=list(out_specs) + [ANY] * co,
        out_shape=list(out_shape) + plan.out_shapes, scratch_shapes=list(scratch) + plan.scratch(),
        input_output_aliases={n_in + i: n_out + o for i, o in plan.aliases.items()},
        compiler_params=pltpu.CompilerParams(dimension_semantics=("arbitrary",) * len(grid),
                                             vmem_limit_bytes=vmem_mb * 1024 * 1024),
    )(*ins, *plan.ins)
    ride.take(res[n_out:])
    return res[:n_out]


def _attn_fwd(q, k, v, scale, chunk_causal, name, c=None, ct=None, hps=8, ride=None):
    _, s, dk = q.shape
    dv = v.shape[2]
    t = _pick(s, (256, 128))
    bias = c is not None

    def body(*refs):
        if bias:
            q_ref, k_ref, v_ref, c_ref, ct_ref, o_ref, lse_ref = refs
        else:
            q_ref, k_ref, v_ref, o_ref, lse_ref = refs
        hp, i = pl.program_id(0), pl.program_id(1)
        qb = [q_ref[e] for e in range(hps)]
        cq = [_pick_col(c_ref[...], hp * hps + e) if bias else None for e in range(hps)]

        def step(j, carry, diagonal):
            off = pl.multiple_of(j * t, t)
            out = []
            for e in range(hps):
                m, l, acc = carry[e]
                sc = _nt(qb[e], k_ref[e, pl.ds(off, t), :]) * scale
                if bias:
                    sc = sc + (cq[e] - ct_ref[pl.ds(hp * hps + e, 1), pl.ds(off, t)])
                if diagonal:
                    sc = jnp.where(_allowed(i, j, t, chunk_causal), sc, NEG)
                m_new = jnp.maximum(m, jnp.max(sc, axis=1, keepdims=True))
                pr = jnp.exp(sc - m_new)
                alpha = jnp.exp(m - m_new)
                out.append((m_new, alpha * l + jnp.sum(pr, axis=1, keepdims=True),
                            alpha * acc + _nn(pr, v_ref[e, pl.ds(off, t), :])))
            return tuple(out)

        init = tuple((jnp.full((t, 1), NEG, F32), jnp.zeros((t, 1), F32), jnp.zeros((t, dv), F32)) for _ in range(hps))
        res = step(i, lax.fori_loop(0, i, functools.partial(step, diagonal=False), init), True)
        for e in range(hps):
            m, l, acc = res[e]
            o_ref[e] = acc / l
            lse_ref[e] = m + jnp.log(l)

    ins = [q, k, v] + ([c, ct] if bias else [])
    in_specs = [pl.BlockSpec((hps, t, dk), lambda h, i: (h, i, 0)), pl.BlockSpec((hps, s, dk), lambda h, i: (h, 0, 0)),
                pl.BlockSpec((hps, s, dv), lambda h, i: (h, 0, 0))]
    if bias:
        in_specs += [pl.BlockSpec((t, LANES), lambda h, i: (i, 0)), pl.BlockSpec((HEADS, s), lambda h, i: (0, 0))]
    return _call_with_ride(
        body, name, (HEADS // hps, s // t), ins, in_specs, [_sds((HEADS, s, dv)), _sds((HEADS, s, 1))],
        [pl.BlockSpec((hps, t, dv), lambda h, i: (h, i, 0)), pl.BlockSpec((hps, t, 1), lambda h, i: (h, i, 0))],
        ("parallel", "parallel"), 48, ride)


def _attn_delta(q, k, v, do, lse, scale, chunk_causal, name, c=None, ct=None, hps=4, ride=None):
    _, s, dk = q.shape
    dv = v.shape[2]
    t = _pick(s, (256, 128))
    bias = c is not None

    def body(*refs):
        if bias:
            q_ref, k_ref, v_ref, do_ref, lse_ref, c_ref, ct_ref, dl_ref = refs
        else:
            q_ref, k_ref, v_ref, do_ref, lse_ref, dl_ref = refs
        hp, i = pl.program_id(0), pl.program_id(1)
        qb, dob, lse_b = ([r[e] for e in range(hps)] for r in (q_ref, do_ref, lse_ref))
        cq = [_pick_col(c_ref[...], hp * hps + e) if bias else None for e in range(hps)]

        def step(j, acc, diagonal):
            off = pl.multiple_of(j * t, t)
            out = []
            for e in range(hps):
                sc = _nt(qb[e], k_ref[e, pl.ds(off, t), :]) * scale
                if bias:
                    sc = sc + (cq[e] - ct_ref[pl.ds(hp * hps + e, 1), pl.ds(off, t)])
                pr = jnp.exp(sc - lse_b[e])
                if diagonal:
                    pr = jnp.where(_allowed(i, j, t, chunk_causal), pr, 0.0)
                out.append(acc[e] + jnp.sum(pr * _nt(dob[e], v_ref[e, pl.ds(off, t), :]), axis=1, keepdims=True))
            return tuple(out)

        init = tuple(jnp.zeros((t, 1), F32) for _ in range(hps))
        res = step(i, lax.fori_loop(0, i, functools.partial(step, diagonal=False), init), True)
        for e in range(hps):
            dl_ref[e] = res[e]

    ins = [q, k, v, do, lse] + ([c, ct] if bias else [])
    in_specs = [pl.BlockSpec((hps, t, dk), lambda h, i: (h, i, 0)), pl.BlockSpec((hps, s, dk), lambda h, i: (h, 0, 0)),
                pl.BlockSpec((hps, s, dv), lambda h, i: (h, 0, 0)), pl.BlockSpec((hps, t, dv), lambda h, i: (h, i, 0)),
                pl.BlockSpec((hps, t, 1), lambda h, i: (h, i, 0))]
    if bias:
        in_specs += [pl.BlockSpec((t, LANES), lambda h, i: (i, 0)), pl.BlockSpec((HEADS, s), lambda h, i: (0, 0))]
    return _call_with_ride(body, name, (HEADS // hps, s // t), ins, in_specs, [_sds((HEADS, s, 1))],
                           [pl.BlockSpec((hps, t, 1), lambda h, i: (h, i, 0))], ("parallel", "parallel"), 48, ride)[0]


def _attn_bwd(q, k, v, do, lse, delta, scale, chunk_causal, name, c=None, ct=None, hps=8, ride=None):
    _, s, dk = q.shape
    dv = v.shape[2]
    t = _pick(s, (256, 128))
    n = s // t
    bias = c is not None

    def body(*refs):
        if bias:
            q_ref, k_ref, v_ref, do_ref, lse_ref, dl_ref, c_ref, ct_ref, dq_ref, dk_ref, dv_ref, dck_ref = refs
        else:
            q_ref, k_ref, v_ref, do_ref, lse_ref, dl_ref, dq_ref, dk_ref, dv_ref = refs
        hp, j = pl.program_id(0), pl.program_id(1)

        @pl.when(j == 0)
        def _():
            dq_ref[...] = jnp.zeros(dq_ref.shape, F32)

        kb, vb = [k_ref[e] for e in range(hps)], [v_ref[e] for e in range(hps)]
        joff = pl.multiple_of(j * t, t)
        ck = [ct_ref[pl.ds(hp * hps + e, 1), pl.ds(joff, t)] if bias else None for e in range(hps)]

        def step(i, carry, diagonal):
            off = pl.multiple_of(i * t, t)
            out = []
            for e in range(hps):
                dk_acc, dv_acc, dck_acc = carry[e]
                qb = q_ref[e, pl.ds(off, t), :]
                dob = do_ref[e, pl.ds(off, t), :]
                sc = _nt(qb, kb[e]) * scale
                if bias:
                    sc = sc + (_pick_col(c_ref[pl.ds(off, t), :], hp * hps + e) - ck[e])
                pr = jnp.exp(sc - lse_ref[e, pl.ds(off, t), :])
                if diagonal:
                    pr = jnp.where(_allowed(i, j, t, chunk_causal), pr, 0.0)
                dv_acc = dv_acc + _tn(pr, dob)
                ds = pr * (_nt(dob, vb[e]) - dl_ref[e, pl.ds(off, t), :])
                dq_ref[e, pl.ds(off, t), :] += _nn(ds, kb[e]) * scale
                dk_acc = dk_acc + _tn(ds, qb) * scale
                if bias:
                    dck_acc = dck_acc - jnp.sum(ds, axis=0, keepdims=True)
                out.append((dk_acc, dv_acc, dck_acc))
            return tuple(out)

        zero = tuple((jnp.zeros((t, dk), F32), jnp.zeros((t, dv), F32), jnp.zeros((1, t), F32)) for _ in range(hps))
        res = lax.fori_loop(j + 1, n, functools.partial(step, diagonal=False), step(j, zero, True))
        for e in range(hps):
            dk_ref[e], dv_ref[e] = res[e][0], res[e][1]
            if bias:
                dck_ref[e] = res[e][2]

    ins = [q, k, v, do, lse, delta] + ([c, ct] if bias else [])
    full = lambda d: pl.BlockSpec((hps, s, d), lambda h, j: (h, 0, 0))
    blk = lambda d: pl.BlockSpec((hps, t, d), lambda h, j: (h, j, 0))
    in_specs = [full(dk), blk(dk), blk(dv), full(dv), full(1), full(1)]
    if bias:
        in_specs += [pl.BlockSpec((s, LANES), lambda h, j: (0, 0)), pl.BlockSpec((HEADS, s), lambda h, j: (0, 0))]
    out_specs = [full(dk), blk(dk), blk(dv)]
    out_shape = [_sds((HEADS, s, dk)), _sds((HEADS, s, dk)), _sds((HEADS, s, dv))]
    if bias:
        out_specs.append(pl.BlockSpec((hps, 1, t), lambda h, j: (h, 0, j)))
        out_shape.append(_sds((HEADS, 1, s)))
    return _call_with_ride(body, name, (HEADS // hps, n), ins, in_specs, out_shape, out_specs, ("parallel", "arbitrary"), 56,
                           ride)


def _s5_disc(lr, li, ldt, br, bi):
    dt = jnp.exp(ldt)
    mag = jnp.exp(lr * dt)
    a_re = mag * jnp.cos(li * dt)
    a_im = mag * jnp.sin(li * dt)
    den = lr * lr + li * li
    f_re = ((a_re - 1.0) * lr + a_im * li) / den
    f_im = (a_im * lr - (a_re - 1.0) * li) / den
    return a_re, a_im, f_re * br - f_im * bi, f_re * bi + f_im * br


def _s5_params_fwd(lr16, li16, ldt16, br2, bi2, name):
    def body(a, b, c, d, e, o0, o1, o2, o3):
        r = _s5_disc(a[...], b[...], c[...], d[...], e[...])
        o0[...], o1[...], o2[...], o3[...] = r

    return pl.pallas_call(body, name=name, out_shape=[_sds((512, 64))] * 4)(lr16, li16, ldt16, br2, bi2)


def _s5_params_bwd(lr16, li16, ldt16, br2, bi2, da_re16, da_im16, dbb_re, dbb_im, name):
    def body(a, b, c, d, e, g0, g1, g2, g3, o_lr, o_li, o_dt, o_br, o_bi):
        _, vjp = jax.vjp(_s5_disc, a[...], b[...], c[...], d[...], e[...])
        dlr, dli, dldt, dbr, dbi = vjp((g0[...], g1[...], g2[...], g3[...]))
        grp = lambda z: z.reshape(S5_GROUPS, S5_GROUP, S5_STATE).sum(axis=1)
        o_lr[...] = grp(dlr)
        o_li[...] = grp(dli)
        o_dt[...] = jnp.sum(grp(dldt), axis=1, keepdims=True)
        o_br[...] = dbr
        o_bi[...] = dbi

    return pl.pallas_call(
        body, name=name, out_shape=[_sds((32, 64)), _sds((32, 64)), _sds((32, 1)), _sds((512, 64)), _sds((512, 64))],
    )(lr16, li16, ldt16, br2, bi2, da_re16, da_im16, dbb_re, dbb_im)


def _cmul(ar, ai, br, bi):
    return ar * br - ai * bi, ar * bi + ai * br


S5_SUPER = 4


def _scan_loop(bre_r, bim_r, ar, ai, ore_r, oim_r, reverse, xre_r=None, xim_r=None):
    s, lw = bre_r.shape
    nt = s // SUBLANES
    with_da = xre_r is not None
    shp = (SUBLANES, lw)
    row = lax.broadcasted_iota(jnp.int32, shp, 0)
    pows = [(ar, ai)]
    for _ in range(SUBLANES - 1):
        pows.append(_cmul(pows[-1][0], pows[-1][1], ar, ai))
    cm_r, cm_i = jnp.zeros(shp, F32), jnp.zeros(shp, F32)
    for r in range(SUBLANES):
        e = (SUBLANES - 1 - r) if reverse else r
        cm_r = jnp.where(row == r, jnp.broadcast_to(pows[e][0], shp), cm_r)
        cm_i = jnp.where(row == r, jnp.broadcast_to(pows[e][1], shp), cm_i)
    steps = [(1, pows[0]), (2, pows[1]), (4, pows[3])]

    def tile(it, carry):
        if with_da:
            c_r, c_i, acc_r, acc_i = carry
        else:
            c_r, c_i = carry
        r = (nt - 1 - it) if reverse else it
        off = pl.multiple_of(r * SUBLANES, SUBLANES)
        xr, xi = bre_r[pl.ds(off, SUBLANES), :], bim_r[pl.ds(off, SUBLANES), :]
        for sh, (pr, pi) in steps:
            if reverse:
                keep = row < SUBLANES - sh
                sr = jnp.where(keep, pltpu.roll(xr, SUBLANES - sh, 0), 0.0)
                si = jnp.where(keep, pltpu.roll(xi, SUBLANES - sh, 0), 0.0)
            else:
                keep = row >= sh
                sr = jnp.where(keep, pltpu.roll(xr, sh, 0), 0.0)
                si = jnp.where(keep, pltpu.roll(xi, sh, 0), 0.0)
            mr, mi = _cmul(pr, pi, sr, si)
            xr, xi = xr + mr, xi + mi
        mr, mi = _cmul(cm_r, cm_i, c_r, c_i)
        xr, xi = xr + mr, xi + mi
        ore_r[pl.ds(off, SUBLANES), :] = xr
        oim_r[pl.ds(off, SUBLANES), :] = xi
        edge = 0 if reverse else SUBLANES - 1
        c_r, c_i = xr[edge:edge + 1, :], xi[edge:edge + 1, :]
        if not with_da:
            return c_r, c_i
        fr, fi = xre_r[pl.ds(off, SUBLANES), :], xim_r[pl.ds(off, SUBLANES), :]
        poff = pl.multiple_of(jnp.maximum(r - 1, 0) * SUBLANES, SUBLANES)
        live = (r > 0).astype(F32)
        pr_last = xre_r[pl.ds(poff, SUBLANES), :][SUBLANES - 1:SUBLANES, :] * live
        pi_last = xim_r[pl.ds(poff, SUBLANES), :][SUBLANES - 1:SUBLANES, :] * live
        sr = jnp.where(row >= 1, pltpu.roll(fr, 1, 0), jnp.broadcast_to(pr_last, shp))
        si = jnp.where(row >= 1, pltpu.roll(fi, 1, 0), jnp.broadcast_to(pi_last, shp))
        return c_r, c_i, acc_r + xr * sr + xi * si, acc_i + xi * sr - xr * si

    z1 = jnp.zeros((1, lw), F32)
    if not with_da:
        lax.fori_loop(0, nt, tile, (z1, z1))
        return None
    _, _, acc_r, acc_i = lax.fori_loop(0, nt, tile, (z1, z1, jnp.zeros(shp, F32), jnp.zeros(shp, F32)))
    return jnp.sum(acc_r, axis=0, keepdims=True), jnp.sum(acc_i, axis=0, keepdims=True)


S5_ROWS = 512


def _group_compact(p):
    grp = lax.broadcasted_iota(jnp.int32, (LANES, S5_STATE), 0) // S5_GROUP
    out = jnp.zeros((LANES, S5_STATE), F32)
    for j in range(LANES // S5_GROUP):
        out = jnp.where(grp == j, p[:, S5_STATE * j:S5_STATE * (j + 1)], out)
    return out


def _s5_core_fwd(u, wb_re, wb_im, wc_re, wc_im, a_re, a_im, name, ride=None):
    s = u.shape[0]
    lw = S5_LANES // S5_SUPER
    rows = _pick(s, (S5_ROWS, 256, 128))

    def body(u_r, wbr, wbi, wcr, wci, are_r, aim_r, xre_r, xim_r, y_r, bre_s, bim_s):
        for r0 in range(0, s, rows):
            ub = u_r[r0:r0 + rows, :]
            bre_s[r0:r0 + rows, :] = _nn(ub, wbr[0])
            bim_s[r0:r0 + rows, :] = _nn(ub, wbi[0])
        _scan_loop(bre_s, bim_s, are_r[...], aim_r[...], xre_r, xim_r, False)
        for r0 in range(0, s, rows):
            y_r[r0:r0 + rows, :] = _nn(xre_r[r0:r0 + rows, :], wcr[0]) + _nn(xim_r[r0:r0 + rows, :], wci[0])

    nar = pl.BlockSpec((s, LANES), lambda k: (0, k))
    wide = pl.BlockSpec((s, lw), lambda k: (0, k))
    one = pl.BlockSpec((1, lw), lambda k: (0, k))
    wb = pl.BlockSpec((1, LANES, lw), lambda k: (k, 0, 0))
    wc = pl.BlockSpec((1, lw, LANES), lambda k: (k, 0, 0))
    return _call_with_ride(body, name, (S5_SUPER,), [u, wb_re, wb_im, wc_re, wc_im, a_re, a_im], [nar, wb, wb, wc, wc, one, one],
                           [_sds((s, S5_LANES)), _sds((s, S5_LANES)), _sds((s, S5_WIDTH))], [wide, wide, nar], ("parallel",), 56,
                           ride, scratch=[pltpu.VMEM((s, lw), F32), pltpu.VMEM((s, lw), F32)])


def _s5_core_bwd(dy0, u, du_a, x_re, x_im, wb_re, wb_im, wc_re, wc_im, a_re, a_im_neg, name):
    s = u.shape[0]
    lw = S5_LANES // S5_SUPER
    rows = _pick(s, (S5_ROWS, 256, 128))

    def body(dy_r, u_r, dua_r, xre_r, xim_r, wbr, wbi, wcr, wci, are_r, aim_r, du_r, dare_r, daim_r, dbr_r, dbi_r, dcr_r, dci_r,
             dre_s, dim_s, gre_s, gim_s):
        for r0 in range(0, s, rows):
            dyb = dy_r[r0:r0 + rows, :]
            dre_s[r0:r0 + rows, :] = _nt(dyb, wcr[0])
            dim_s[r0:r0 + rows, :] = _nt(dyb, wci[0])
        dare_r[...], daim_r[...] = _scan_loop(dre_s, dim_s, are_r[...], aim_r[...], gre_s, gim_s, True, xre_r, xim_r)
        acc = [jnp.zeros((LANES, lw), F32) for _ in range(4)]
        for r0 in range(0, s, rows):
            sl = slice(r0, r0 + rows)
            gr, gi, ub, dyb = gre_s[sl, :], gim_s[sl, :], u_r[sl, :], dy_r[sl, :]
            du_r[sl, :] = dua_r[sl, :] + _nt(gr, wbr[0]) + _nt(gi, wbi[0])
            acc = [acc[0] + _tn(ub, gr), acc[1] + _tn(ub, gi), acc[2] + _tn(dyb, xre_r[sl, :]), acc[3] + _tn(dyb, xim_r[sl, :])]
        dbr_r[...], dbi_r[...], dcr_r[...], dci_r[...] = (_group_compact(a) for a in acc)

    nar = pl.BlockSpec((s, LANES), lambda k: (0, k))
    wide = pl.BlockSpec((s, lw), lambda k: (0, k))
    one = pl.BlockSpec((1, lw), lambda k: (0, k))
    wb = pl.BlockSpec((1, LANES, lw), lambda k: (k, 0, 0))
    wc = pl.BlockSpec((1, lw, LANES), lambda k: (k, 0, 0))
    blk = pl.BlockSpec((LANES, S5_STATE), lambda k: (k, 0))
    return _call_with_ride(
        body, name, (S5_SUPER,), [dy0, u, du_a, x_re, x_im, wb_re, wb_im, wc_re, wc_im, a_re, a_im_neg],
        [nar, nar, nar, wide, wide, wb, wb, wc, wc, one, one],
        [_sds((s, S5_WIDTH)), _sds((1, S5_LANES)), _sds((1, S5_LANES))] + [_sds((S5_WIDTH, S5_STATE))] * 4,
        [nar, one, one, blk, blk, blk, blk], ("parallel",), 60, None, scratch=[pltpu.VMEM((s, lw), F32)] * 4)


def _s5_seg1(y0, u, d):
    return jax.nn.gelu(y0 + d * u)


def _s5_seg2(z, t, b):
    return z * jax.nn.sigmoid(t + b)


def _s5_post_fwd(y0, seg2, d, wglu, bglu, name):
    s = y0.shape[0]
    ts = _pick(s, (256, 128))

    def fn(ins, outs, accs, scr):
        z = _s5_seg1(ins[0][...], ins[1][...], ins[2][...])
        outs[0][...] = _s5_seg2(z, _nn(z, ins[3][...]), ins[4][...])

    return _stage(name, fn, s // ts,
                  [(y0, _rows(ts, 512)), (seg2, _rows(ts, 512)), (d, _full(d.shape)), (wglu, _full(wglu.shape)),
                   (bglu, _full(bglu.shape))], [(_sds((s, 512)), _rows(ts, 512))])[0]


def _s5_post_bwd(y0, seg2, d, wglu, bglu, dy, name):
    s = y0.shape[0]
    ts = _pick(s, (256, 128))

    def fn(ins, outs, accs, scr):
        y0_r, u_r, d_r, w_r, b_r, dy_r = ins
        z, vjp1 = jax.vjp(_s5_seg1, y0_r[...], u_r[...], d_r[...])
        t = _nn(z, w_r[...])
        _, vjp2 = jax.vjp(_s5_seg2, z, t, b_r[...])
        dz, dt, db = vjp2(dy_r[...])
        accs[0][...] += _tn(z, dt)
        accs[1][...] += db
        dy0, du, dd = vjp1(dz + _nt(dt, w_r[...]))
        accs[2][...] += dd
        outs[0][...] = dy0
        outs[1][...] = du

    return _stage(name, fn, s // ts,
                  [(y0, _rows(ts, 512)), (seg2, _rows(ts, 512)), (d, _full(d.shape)), (wglu, _full(wglu.shape)),
                   (bglu, _full(bglu.shape)), (dy, _rows(ts, 512))],
                  [(_sds((s, 512)), _rows(ts, 512)), (_sds((s, 512)), _rows(ts, 512))],
                  accs=[_sds((512, 512)), _sds((1, 512)), _sds((1, 512))])


def _gate_a(y, g):
    return y * jax.nn.silu(g)


def _gate_m(o0, o1, o2, m0, m1, m2):
    return jax.nn.sigmoid(m0) * o0 + jax.nn.sigmoid(m1) * o1 + jax.nn.sigmoid(m2) * o2


def _assemble(ybuf, o_mla, o_fox, y_s5):
    for h in range(HEADS):
        ybuf[:, 64 * h:64 * (h + 1)] = o_mla[h]
        ybuf[:, 512 + 64 * h:512 + 64 * (h + 1)] = o_fox[h]
    ybuf[:, 1024:1536] = y_s5[...]


def _gate_fwd(o_mla, o_fox, y_s5, seg3, seg4, x, wo, wout, name):
    s = x.shape[0]
    ts = _pick(s, (256, 128))

    def fn(ins, outs, accs, scr):
        om, of, ys, g_r, m_r, x_r, wo_r, wout_r = ins
        ybuf = scr[0]
        _assemble(ybuf, om, of, ys)
        a = _gate_a(ybuf[...], g_r[...])
        o = [_nn(a[:, 512 * b:512 * (b + 1)], wo_r[512 * b:512 * (b + 1), :]) for b in range(3)]
        merged = _gate_m(o[0], o[1], o[2], m_r[:, 0:1024], m_r[:, 1024:2048], m_r[:, 2048:3072])
        outs[0][...] = x_r[...] + _nn(merged, wout_r[...])

    return _stage(name, fn, s // ts,
                  [(o_mla, _heads(ts, 64)), (o_fox, _heads(ts, 64)), (y_s5, _rows(ts, 512)), (seg3, _rows(ts, 1536)),
                   (seg4, _rows(ts, 3072)), (x, _rows(ts, D_MODEL)), (wo, _full(wo.shape)), (wout, _full(wout.shape))],
                  [(_sds((s, D_MODEL)), _rows(ts, D_MODEL))], scratch=[pltpu.VMEM((ts, 1536), F32)])[0]


def _gate_bwd(o_mla, o_fox, y_s5, seg3, seg4, wo, wout, dout, name):
    s = dout.shape[0]
    ts = _pick(s, (128,))

    def fn(ins, outs, accs, scr):
        om, of, ys, g_r, m_r, wo_r, wout_r, dout_r = ins
        do_mla, do_fox, dys, dg_r, dm_r, dl_mla = outs
        dwo, dwout = accs
        ybuf, dabuf = scr
        _assemble(ybuf, om, of, ys)
        a, vjp_a = jax.vjp(_gate_a, ybuf[...], g_r[...])
        o = [_nn(a[:, 512 * b:512 * (b + 1)], wo_r[512 * b:512 * (b + 1), :]) for b in range(3)]
        ms = [m_r[:, 1024 * b:1024 * (b + 1)] for b in range(3)]
        merged, vjp_m = jax.vjp(_gate_m, *o, *ms)
        dout_v = dout_r[...]
        dwout[...] += _tn(merged, dout_v)
        cts = vjp_m(_nt(dout_v, wout_r[...]))
        for b in range(3):
            dm_r[:, 1024 * b:1024 * (b + 1)] = cts[3 + b].astype(BF16)
            dwo[512 * b:512 * (b + 1), :] += _tn(a[:, 512 * b:512 * (b + 1)], cts[b])
            dabuf[:, 512 * b:512 * (b + 1)] = _nt(cts[b], wo_r[512 * b:512 * (b + 1), :])
        dy, dg = vjp_a(dabuf[...])
        dg_r[...] = dg.astype(BF16)
        dys[...] = dy[:, 1024:1536]
        for h in range(HEADS):
            d = dy[:, 64 * h:64 * (h + 1)]
            do_mla[h] = d
            dl_mla[h] = jnp.sum(d * om[h], axis=1, keepdims=True)
            do_fox[h] = dy[:, 512 + 64 * h:512 + 64 * (h + 1)]

    return _stage(name, fn, s // ts,
                  [(o_mla, _heads(ts, 64)), (o_fox, _heads(ts, 64)), (y_s5, _rows(ts, 512)), (seg3, _rows(ts, 1536)),
                   (seg4, _rows(ts, 3072)), (wo, _full(wo.shape)), (wout, _full(wout.shape)), (dout, _rows(ts, D_MODEL))],
                  [(_sds((HEADS, s, 64)), _heads(ts, 64)), (_sds((HEADS, s, 64)), _heads(ts, 64)), (_sds((s, 512)), _rows(ts, 512)),
                   (_sds((s, 1536), BF16), _rows(ts, 1536)), (_sds((s, 3072), BF16), _rows(ts, 3072)),
                   (_sds((HEADS, s, 1)), _heads(ts, 1))],
                  accs=[_sds(wo.shape), _sds(wout.shape)], scratch=[pltpu.VMEM((ts, 1536), F32), pltpu.VMEM((ts, 1536), F32)],
                  vmem_mb=56)


def _loss_head(y, target, name):
    s = y.shape[0]
    ts = _pick(s, (256, 128))

    def fn(ins, outs, accs, scr):
        e = ins[0][...] - ins[1][...]
        outs[0][...] = e / D_MODEL
        accs[0][...] += 0.5 * jnp.sum(jnp.sum(e * e, axis=1, keepdims=True) / D_MODEL, axis=0, keepdims=True)

    r = _stage(name, fn, s // ts, [(y, _rows(ts, D_MODEL)), (target, _rows(ts, D_MODEL))],
               [(_sds((s, D_MODEL)), _rows(ts, D_MODEL))], accs=[_sds((1, 1))])
    return r[1], r[0]


IN_RANGES = ((0, 384, 0), (384, 416, 448), (416, 1952, 512), (1952, 1960, 384), (1960, IN_WIDTH, 2048))
SHARD_W = IN_WIDTH // N_DEV


def _win_pieces(d):
    lo, hi = SHARD_W * d, SHARD_W * (d + 1)
    out = []
    for a, b, p in IN_RANGES:
        s, e = max(a, lo), min(b, hi)
        while s < e:
            pad = p + (s - a)
            k = max(i for i in range(5) if SEG_OFF[i] <= pad)
            w = min(e - s, SEG_OFF[k] + SEG_W[k] - pad)
            out.append((s - lo, w, k, pad - SEG_OFF[k]))
            s += w
    return out


def _win_pad(g, name):
    _, r, _ = g.shape
    tr = 64

    def body(g_ref, o_ref):
        o_ref[...] = jnp.zeros(o_ref.shape, o_ref.dtype)
        for d in range(N_DEV):
            for dst, w, k, src in _win_pieces(d):
                o_ref[:, SEG_OFF[k] + src:SEG_OFF[k] + src + w] = g_ref[d, :, dst:dst + w]

    return pl.pallas_call(
        body, name=name, grid=(r // tr,), in_specs=[pl.BlockSpec((N_DEV, tr, SHARD_W), lambda i: (0, i, 0))],
        out_specs=pl.BlockSpec((tr, PAD_IN), lambda i: (i, 0)), out_shape=jax.ShapeDtypeStruct((r, PAD_IN), g.dtype),
        compiler_params=pltpu.CompilerParams(dimension_semantics=("parallel",)),
    )(g)


def _win_unpad(dsegs, name):
    r = dsegs[0].shape[0]
    tr = 64

    def body(*refs):
        o_ref = refs[5]
        for d in range(N_DEV):
            for dst, w, k, src in _win_pieces(d):
                o_ref[d, :, dst:dst + w] = refs[k][:, src:src + w]

    return pl.pallas_call(
        body, name=name, grid=(r // tr,), in_specs=[pl.BlockSpec((tr, SEG_W[k]), lambda i: (i, 0)) for k in range(5)],
        out_specs=pl.BlockSpec((N_DEV, tr, SHARD_W), lambda i: (0, i, 0)),
        out_shape=jax.ShapeDtypeStruct((N_DEV, r, SHARD_W), dsegs[0].dtype),
        compiler_params=pltpu.CompilerParams(dimension_semantics=("parallel",)),
    )(*dsegs)


def _pad_wq(w):
    w = w.reshape(MLA_Q_RANK, HEADS, MLA_QK)
    return jnp.pad(w, ((0, 0), (0, 0), (0, LANES - MLA_QK))).reshape(MLA_Q_RANK, HEADS * LANES)


def _unpad_wq(d):
    return d.reshape(MLA_Q_RANK, HEADS, LANES)[:, :, :MLA_QK].reshape(MLA_Q_RANK, HEADS * MLA_QK)


def _pad_wkv(w):
    w = w.reshape(MLA_KV_RANK, HEADS, MLA_NOPE + MLA_V)
    k = jnp.pad(w[:, :, :MLA_NOPE], ((0, 0), (0, 0), (0, LANES - MLA_NOPE))).reshape(MLA_KV_RANK, HEADS * LANES)
    return jnp.concatenate([k, w[:, :, MLA_NOPE:].reshape(MLA_KV_RANK, HEADS * MLA_V)], axis=1)


def _unpad_wkv(d):
    k = d[:, :HEADS * LANES].reshape(MLA_KV_RANK, HEADS, LANES)[:, :, :MLA_NOPE]
    v = d[:, HEADS * LANES:].reshape(MLA_KV_RANK, HEADS, MLA_V)
    return jnp.concatenate([k, v], axis=2).reshape(MLA_KV_RANK, HEADS * (MLA_NOPE + MLA_V))


def _pad_lanes(v, n=LANES):
    return jnp.pad(v, (0, n - v.shape[0])).reshape(1, n)


def _super_blocks(b):
    _, r, c = b.shape
    per = S5_GROUPS // S5_SUPER
    b = b.reshape(S5_SUPER, per, r, c)
    eye = jnp.eye(per, dtype=b.dtype)
    return (b[:, :, :, None, :] * eye[None, :, None, :, None]).reshape(S5_SUPER, per * r, per * c)


def _layer_params(l, w, small):
    p = {k: small[k][l] for k in small}
    q = {}
    q["norm_g"] = p["norm_g"].reshape(1, D_MODEL)
    q["mla_q_a_norm"] = p["mla_q_a_norm"].reshape(1, 256)
    q["mla_kv_a_norm"] = p["mla_kv_a_norm"].reshape(1, 128)
    q["mla_q_norm"] = _pad_lanes(p["mla_q_norm"])
    q["mla_k_norm"] = _pad_lanes(p["mla_k_norm"])
    q["fox_b_f"] = _pad_lanes(p["fox_b_f"])
    q["fox_q_norm"] = p["fox_q_norm"].reshape(1, 64)
    q["fox_k_norm"] = p["fox_k_norm"].reshape(1, 64)
    q["s5_d"] = p["s5_d"].reshape(1, 512)
    q["s5_b_glu"] = p["s5_b_glu"].reshape(1, 512)
    rep = lambda z: jnp.repeat(z, S5_GROUP, axis=0)
    q["lr16"], q["li16"] = rep(p["s5_lambda_re"]), rep(p["s5_lambda_im"])
    q["ldt16"] = rep(jnp.broadcast_to(p["s5_log_dt"][:, None], (S5_GROUPS, S5_STATE)))
    q["br2"] = p["s5_b_re"].transpose(0, 2, 1).reshape(512, 64)
    q["bi2"] = p["s5_b_im"].transpose(0, 2, 1).reshape(512, 64)
    q["c_re"], q["c_im"] = p["s5_c_re"], p["s5_c_im"]
    q["w_in"] = _win_pad(w["w_in"], f"l{l}_w_in_pad")
    q["wq"] = _pad_wq(w["mla_w_q_up"])
    q["wkv"] = _pad_wkv(w["mla_w_kv_up"])
    q.update(_late_weights(w))
    return q


def _late_weights(w):
    return {q: w[k] for q, k in (("wglu", "s5_w_glu"), ("wo", "w_branch_out"), ("wout", "w_out")) if k in w}


def _layer_fwd(l, x, tabs, q, rides=None):
    rides = rides or {}
    n = lambda s: f"l{l}_{s}"
    sv = {"x": x}
    h = _norm_fwd(x, q["norm_g"], n("norm_fwd"))
    sv["h"] = h
    seg = [_mm(h, q["w_in"], "nn", n(f"proj{k}"), b_cols=(SEG_OFF[k], SEG_W[k])) for k in range(5)]
    sv["seg"] = seg
    mq, mk, mv = _mla_prep_fwd(seg[0], tabs, q, q["wq"], q["wkv"], n("mla_prep_fwd"))
    o_mla, lse_mla = _attn_fwd(mq, mk, mv, 1.0 / math.sqrt(MLA_QK), True, n("mla_attn_fwd"), ride=rides.get("mla_attn_fwd"))
    sv.update(mq=mq, mk=mk, mv=mv, o_mla=o_mla, lse_mla=lse_mla)
    fq, fk, fv, c, ct = _fox_prep_fwd(seg[0], seg[1], q["fox_b_f"], q["fox_q_norm"], q["fox_k_norm"], n("fox_prep_fwd"))
    o_fox, lse_fox = _attn_fwd(fq, fk, fv, 1.0 / math.sqrt(FOX_DIM), False, n("fox_attn_fwd"), c=c, ct=ct,
                               ride=rides.get("fox_attn_fwd"))
    sv.update(fq=fq, fk=fk, fv=fv, c=c, ct=ct, o_fox=o_fox, lse_fox=lse_fox)
    a_re16, a_im16, bb_re, bb_im = _s5_params_fwd(q["lr16"], q["li16"], q["ldt16"], q["br2"], q["bi2"], n("s5_params_fwd"))
    a_re = a_re16.reshape(S5_GROUPS, S5_GROUP, S5_STATE)[:, 0, :].reshape(1, S5_LANES)
    a_im = a_im16.reshape(S5_GROUPS, S5_GROUP, S5_STATE)[:, 0, :].reshape(1, S5_LANES)
    wb_re = _super_blocks(bb_re.reshape(S5_GROUPS, S5_GROUP, S5_STATE)).astype(BF16)
    wb_im = _super_blocks(bb_im.reshape(S5_GROUPS, S5_GROUP, S5_STATE)).astype(BF16)
    wc_re = _super_blocks(q["c_re"].transpose(0, 2, 1)).astype(BF16)
    wc_im = _super_blocks(-q["c_im"].transpose(0, 2, 1)).astype(BF16)
    x_re, x_im, y0 = _s5_core_fwd(seg[2], wb_re, wb_im, wc_re, wc_im, a_re, a_im, n("s5_scan_fwd"), ride=rides.get("s5_scan_fwd"))
    y_s5 = _s5_post_fwd(y0, seg[2], q["s5_d"], q["wglu"], q["s5_b_glu"], n("s5_post_fwd"))
    sv.update(a_re=a_re, a_im=a_im, wb_re=wb_re, wb_im=wb_im, wc_re=wc_re, wc_im=wc_im, x_re=x_re, x_im=x_im, y0=y0, y_s5=y_s5)
    out = _gate_fwd(o_mla, o_fox, y_s5, seg[3], seg[4], x, q["wo"], q["wout"], n("gate_fwd"))
    return out, sv


def _layer_bwd(l, dout, tabs, q, sv, rides=None, g=None):
    rides = rides or {}
    n = lambda s: f"l{l}_{s}"
    seg = sv["seg"]
    g = {} if g is None else g
    (do_mla, do_fox, dy_s5, dseg3, dseg4, dl_mla, g["wo"], g["wout"]) = _gate_bwd(
        sv["o_mla"], sv["o_fox"], sv["y_s5"], seg[3], seg[4], q["wo"], q["wout"], dout, n("gate_bwd"))
    dy0, du_a, g["wglu"], g["s5_b_glu"], g["s5_d"] = _s5_post_bwd(sv["y0"], seg[2], q["s5_d"], q["wglu"], q["s5_b_glu"], dy_s5,
                                                                 n("s5_post_bwd"))
    dseg2, da_re, da_im, dbb_re, dbb_im, dc_re, dc_im = _s5_core_bwd(
        dy0, seg[2], du_a, sv["x_re"], sv["x_im"], sv["wb_re"], sv["wb_im"], sv["wc_re"], sv["wc_im"], sv["a_re"], -sv["a_im"],
        n("s5_scan_bwd"))
    g["s5_c_re"] = dc_re.reshape(S5_GROUPS, S5_GROUP, S5_STATE)
    g["s5_c_im"] = -dc_im.reshape(S5_GROUPS, S5_GROUP, S5_STATE)
    first = (jnp.arange(512) % S5_GROUP == 0).astype(F32)[:, None]
    da_re16 = jnp.repeat(da_re.reshape(S5_GROUPS, S5_STATE), S5_GROUP, axis=0) * first
    da_im16 = jnp.repeat(da_im.reshape(S5_GROUPS, S5_STATE), S5_GROUP, axis=0) * first
    dlr, dli, dldt, dbr2, dbi2 = _s5_params_bwd(q["lr16"], q["li16"], q["ldt16"], q["br2"], q["bi2"], da_re16, da_im16, dbb_re,
                                               dbb_im, n("s5_params_bwd"))
    g["s5_lambda_re"], g["s5_lambda_im"], g["s5_log_dt"] = dlr, dli, dldt.reshape(S5_GROUPS)
    g["s5_b_re"] = dbr2.reshape(S5_GROUPS, S5_GROUP, S5_STATE).transpose(0, 2, 1)
    g["s5_b_im"] = dbi2.reshape(S5_GROUPS, S5_GROUP, S5_STATE).transpose(0, 2, 1)
    dl_fox = _attn_delta(sv["fq"], sv["fk"], sv["fv"], do_fox, sv["lse_fox"], 1.0 / math.sqrt(FOX_DIM), False,
                         n("fox_attn_delta"), c=sv["c"], ct=sv["ct"], ride=rides.get("fox_attn_delta"))
    dfq, dfk, dfv, dck = _attn_bwd(sv["fq"], sv["fk"], sv["fv"], do_fox, sv["lse_fox"], dl_fox, 1.0 / math.sqrt(FOX_DIM), False,
                                   n("fox_attn_bwd"), c=sv["c"], ct=sv["ct"], ride=rides.get("fox_attn_bwd"))
    dseg1, dff, g["fox_q_norm"], g["fox_k_norm"], dbf = _fox_prep_bwd(seg[0], seg[1], q["fox_b_f"], q["fox_q_norm"], q["fox_k_norm"],
                                                                      dfq, dfk, dfv, dck, n("fox_prep_bwd"))
    g["fox_b_f"] = dbf[0, :HEADS]
    dmq, dmk, dmv = _attn_bwd(sv["mq"], sv["mk"], sv["mv"], do_mla, sv["lse_mla"], dl_mla, 1.0 / math.sqrt(MLA_QK), True,
                              n("mla_attn_bwd"), ride=rides.get("mla_attn_bwd"))
    dseg0, dqan, dkvan, dqn, dkn, g["wq"], g["wkv"] = _mla_prep_bwd(seg[0], tabs, q, q["wq"], q["wkv"], dmq, dmk, dmv, dff,
                                                                   n("mla_prep_bwd"))
    g["mla_q_a_norm"], g["mla_kv_a_norm"] = dqan, dkvan
    g["mla_q_norm"], g["mla_k_norm"] = dqn[0, :MLA_QK], dkn[0, :MLA_QK]
    dsegs = [dseg0, dseg1, dseg2, dseg3, dseg4]
    g["w_in"] = [_mm(sv["h"], dsegs[k], "tn", n(f"dwin{k}")) for k in range(5)]
    dh = None
    for k in range(5):
        dh = _mm(dsegs[k], q["w_in"], "nt", n(f"dh{k}"), acc=dh, b_cols=(SEG_OFF[k], SEG_W[k]), ride=rides.get(f"dh{k}"))
    dx, g["norm_g"] = _norm_bwd(sv["x"], q["norm_g"], dh, dout, n("norm_bwd"))
    return dx, g


MESH = pl.DeviceIdType.MESH
ANY = pl.BlockSpec(memory_space=pl.ANY)


def _all_gather(blocks, name):
    n = len(blocks)

    def body(*refs):
        x_refs, out_refs = refs[:n], refs[n:2 * n]
        send_sems, recv_sems, local_sems = refs[2 * n:]
        x, y, c = lax.axis_index("x"), lax.axis_index("y"), lax.axis_index("c")
        me, sibling = (x, y, c), (x, y, 1 - c)
        chips = [(1 - x, y), (x, 1 - y), (1 - x, 1 - y)]

        def slot(a, px, py, pc):
            return out_refs[a].at[4 * px + 2 * py + pc]

        def copy(a, k, blk, to, src=None):
            return pltpu.make_async_remote_copy(src_ref=slot(a, *blk) if src is None else src, dst_ref=slot(a, *blk),
                                                send_sem=send_sems.at[7 * a + k], recv_sem=recv_sems.at[7 * a + k],
                                                device_id=to, device_id_type=MESH)

        mine = [pltpu.make_async_copy(x_refs[a], slot(a, *me), local_sems.at[a]) for a in range(n)]
        for cp in mine:
            cp.start()
        first = []
        for j, chip in enumerate(chips):
            first += [copy(a, 1 + j, me, (*chip, c), src=x_refs[a]) for a in range(n)]
        first += [copy(a, 0, me, sibling, src=x_refs[a]) for a in range(n)]
        for cp in first:
            cp.start()
        passed = []
        for j, chip in enumerate(chips):
            for a in range(n):
                copy(a, 1 + j, (*chip, c), me).wait_recv()
                passed.append(copy(a, 4 + j, (*chip, c), sibling))
                passed[-1].start()
        for a in range(n):
            copy(a, 0, sibling, me).wait_recv()
        for j, chip in enumerate(chips):
            for a in range(n):
                copy(a, 4 + j, (*chip, 1 - c), me).wait_recv()
        for cp in first + passed:
            cp.wait_send()
        for cp in mine:
            cp.wait()

    return pl.pallas_call(
        body, name=name, out_shape=[jax.ShapeDtypeStruct((N_DEV,) + b.shape, b.dtype) for b in blocks],
        in_specs=[ANY] * n, out_specs=[ANY] * n,
        scratch_shapes=[pltpu.SemaphoreType.DMA((7 * n,)), pltpu.SemaphoreType.DMA((7 * n,)), pltpu.SemaphoreType.DMA((n,))],
    )(*blocks)


def _place():
    x, y, c = lax.axis_index("x"), lax.axis_index("y"), lax.axis_index("c")
    return x, y, c, [(1 - x, y), (x, 1 - y), (1 - x, 1 - y)]


def _remote(src, dst, send, recv, k, to):
    return pltpu.make_async_remote_copy(src_ref=src, dst_ref=dst, send_sem=send.at[k], recv_sem=recv.at[k], device_id=to,
                                        device_id_type=MESH)


def _plan_gather_ici(blocks):
    n = len(blocks)

    def copies(in_refs, out_refs, send, recv, local):
        x, y, c, chips = _place()
        mine = 4 * x + 2 * y + c
        loc = [pltpu.make_async_copy(in_refs[a], out_refs[a].at[mine], local.at[a]) for a in range(n)]
        rem = [_remote(in_refs[a], out_refs[a].at[mine], send, recv, 3 * a + j, (px, py, c))
               for j, (px, py) in enumerate(chips) for a in range(n)]
        return rem, loc

    return _Plan(blocks, [jax.ShapeDtypeStruct((N_DEV,) + b.shape, b.dtype) for b in blocks], 3 * n, n, copies)


def _plan_gather_d2d(gathered):
    n = len(gathered)

    def copies(in_refs, out_refs, send, recv, local):
        x, y, c, _ = _place()
        rem = [_remote(in_refs[a].at[2 * j + c], out_refs[a].at[2 * j + c], send, recv, 4 * a + j, (x, y, 1 - c))
               for a in range(n) for j in range(4)]
        return rem, []

    return _Plan(gathered, [jax.ShapeDtypeStruct(g.shape, g.dtype) for g in gathered], 4 * n, 0, copies,
                 aliases={a: a for a in range(n)})


def _plan_reduce_sibling(parts):
    n = len(parts)

    def copies(in_refs, out_refs, send, recv, local):
        x, y, c, _ = _place()
        rem = [_remote(in_refs[a].at[2 * j + (1 - c)], out_refs[a].at[j], send, recv, 4 * a + j, (x, y, 1 - c))
               for a in range(n) for j in range(4)]
        return rem, []

    return _Plan(parts, [jax.ShapeDtypeStruct((4,) + p.shape[1:], p.dtype) for p in parts], 4 * n, 0, copies)


def _plan_reduce_chips(sums):
    n = len(sums)

    def copies(in_refs, out_refs, send, recv, local):
        x, y, c, chips = _place()
        mine = 2 * x + y
        loc = [pltpu.make_async_copy(in_refs[a].at[mine], out_refs[a].at[mine], local.at[a]) for a in range(n)]
        rem = [_remote(in_refs[a].at[2 * px + py], out_refs[a].at[mine], send, recv, 3 * a + k, (px, py, c))
               for k, (px, py) in enumerate(chips) for a in range(n)]
        return rem, loc

    return _Plan(sums, [jax.ShapeDtypeStruct(p.shape, p.dtype) for p in sums], 3 * n, n, copies)


def _add_sibling(parts, got, name):
    _, r, cc = parts.shape
    tr = _pick(r, (512, 256, 128, 64, 32, 16))
    c = lax.axis_index("c")

    def body(c_ref, p_ref, g_ref, o_ref):
        o_ref[...] = (p_ref[...] + g_ref[...]).astype(BF16)

    return pl.pallas_call(
        body, name=name, out_shape=jax.ShapeDtypeStruct((4, r, cc), BF16),
        grid_spec=pltpu.PrefetchScalarGridSpec(
            num_scalar_prefetch=1, grid=(4, r // tr),
            in_specs=[pl.BlockSpec((1, tr, cc), lambda j, i, cr: (2 * j + cr[0], i, 0)),
                      pl.BlockSpec((1, tr, cc), lambda j, i, cr: (j, i, 0))],
            out_specs=pl.BlockSpec((1, tr, cc), lambda j, i, cr: (j, i, 0))),
        compiler_params=_vmem(48),
    )(c.reshape(1).astype(jnp.int32), parts, got)


def _sum_leading(parts, name):
    k, r, cc = parts.shape
    tr = _pick(r, (512, 256, 128, 64, 32, 16, 8))

    def body(p_ref, o_ref):
        acc = p_ref[0]
        for j in range(1, k):
            acc = acc + p_ref[j]
        o_ref[...] = acc

    return pl.pallas_call(
        body, name=name, out_shape=jax.ShapeDtypeStruct((r, cc), F32), grid=(r // tr,),
        in_specs=[pl.BlockSpec((k, tr, cc), lambda i: (0, i, 0))], out_specs=pl.BlockSpec((tr, cc), lambda i: (i, 0)),
    )(parts)


def _adamw_math(w, g, m, v):
    nm = ADAM_B1 * m + (1.0 - ADAM_B1) * g
    nv = ADAM_B2 * v + (1.0 - ADAM_B2) * jnp.square(g)
    m_hat = nm / (1.0 - ADAM_B1 ** ADAM_STEP)
    v_hat = nv / (1.0 - ADAM_B2 ** ADAM_STEP)
    return -ADAM_LR * (m_hat / (jnp.sqrt(v_hat) + ADAM_EPS) + ADAM_WD * w), nm, nv


def _adamw_sum(w, contribs, m, v, name, ride=None):
    nl = len(contribs)
    k, r, cc = contribs[0].shape
    tr = _pick(r, (256, 128, 64, 32, 16))
    nb = r // tr

    def body(w_ref, *rest):
        c_refs = rest[:nl]
        m_ref, v_ref, g_ref, d_ref, nm_ref, nv_ref = rest[nl:]
        for li in range(nl):
            @pl.when(pl.program_id(0) == li)
            def _(c_ref=c_refs[li]):
                g = c_ref[0].astype(F32)
                for j in range(1, k):
                    g = g + c_ref[j].astype(F32)
                g_ref[...] = g
                d_ref[...], nm_ref[...], nv_ref[...] = _adamw_math(w_ref[...], g, m_ref[...], v_ref[...])

    spec = pl.BlockSpec((tr, cc), lambda l, i: (l * nb + i, 0))
    cspec = pl.BlockSpec((k, tr, cc), lambda l, i: (0, i, 0))
    return _call_with_ride(body, name, (nl, nb), [w, *contribs, m, v], [spec] + [cspec] * nl + [spec, spec],
                           [jax.ShapeDtypeStruct(w.shape, F32)] * 4, [spec] * 4, ("parallel", "parallel"), 48, ride)


def _adamw_many(ws, gs, ms, vs, name):
    n = len(ws)

    def body(*refs):
        w_r, g_r, m_r, v_r = refs[:n], refs[n:2 * n], refs[2 * n:3 * n], refs[3 * n:4 * n]
        d_r, nm_r, nv_r = refs[4 * n:5 * n], refs[5 * n:6 * n], refs[6 * n:7 * n]
        for a in range(n):
            d_r[a][...], nm_r[a][...], nv_r[a][...] = _adamw_math(w_r[a][...], g_r[a][...], m_r[a][...], v_r[a][...])

    shapes = [jax.ShapeDtypeStruct(w.shape, F32) for w in ws]
    res = pl.pallas_call(body, name=name, out_shape=shapes * 3,
                         compiler_params=pltpu.CompilerParams(vmem_limit_bytes=56 * 1024 * 1024))(*ws, *gs, *ms, *vs)
    return res[:n], res[n:2 * n], res[2 * n:]


def _pack_rows(flat, lanes, row_mult):
    n = flat.shape[-1]
    rows = -(-n // lanes)
    rows = -(-rows // row_mult) * row_mult
    pad = rows * lanes - n
    if pad:
        flat = jnp.pad(flat, [(0, 0)] * (flat.ndim - 1) + [(0, pad)])
    return flat.reshape(flat.shape[:-1] + (rows, lanes))


def _rope_tables(positions):
    inv = 1.0 / (ROPE_THETA ** (jnp.arange(0, MLA_ROPE, 2, dtype=F32) / MLA_ROPE))
    ang = positions.astype(F32)[:, None] * inv
    cos, sin = jnp.cos(ang), jnp.sin(ang)
    s = positions.shape[0]
    z = lambda n: jnp.zeros((s, n), F32)
    c = jnp.concatenate([jnp.ones((s, 64), F32), cos, cos, z(32)], axis=1)
    sa = jnp.concatenate([z(64), -sin, z(48)], axis=1)
    sb = jnp.concatenate([z(80), sin, z(32)], axis=1)
    return c, sa, sb


def _full_weights(gathered, names=SHARDED):
    full = {}
    for k, g in zip(names, gathered):
        _, r, c = g.shape
        if k == "w_in":
            full[k] = g
        else:
            full[k] = g.transpose(1, 0, 2).reshape(r, N_DEV * c) if k in COL_SHARDED else g.reshape(N_DEV * r, c)
    return full


EARLY = ("s5_w_glu", "w_branch_out", "w_out")
LATE = ("w_in", "mla_w_q_up", "mla_w_kv_up")


def _owner_major(g, names, tag):
    parts = []
    for k in names:
        if k == "w_in":
            parts.append(_win_unpad(g["w_in"], f"{tag}_w_in_unpad"))
            continue
        big = {"mla_w_q_up": lambda: _unpad_wq(g["wq"]), "mla_w_kv_up": lambda: _unpad_wkv(g["wkv"]), "s5_w_glu": lambda: g["wglu"],
               "w_branch_out": lambda: g["wo"], "w_out": lambda: g["wout"]}[k]()
        r, c = big.shape
        if k in COL_SHARDED:
            parts.append(big.reshape(r, N_DEV, c // N_DEV).transpose(1, 0, 2))
        else:
            parts.append(big.reshape(N_DEV, r // N_DEV, c))
    return parts


def _device_step(x, positions, target, shards, small):
    tabs = _rope_tables(positions)
    box = {}
    q0 = _layer_params(0, _full_weights(_all_gather(shards[0][:3], "gather_weights_l0"), LATE), small)

    def arrived(o):
        box.update(w1_b=o[:5])
        q0.update(_late_weights(_full_weights(o[5:], EARLY)))

    rides = {
        "mla_attn_fwd": _Ride(lambda: _plan_gather_ici(shards[1][:1]), lambda o: box.update(ici_a=o)),
        "fox_attn_fwd": _Ride(lambda: _join([_plan_gather_ici(shards[1][1:]), _plan_gather_d2d(box["ici_a"]),
                                             _plan_gather_ici(shards[0][3:])]),
                              lambda o: box.update(ici_b=o[:5], w1_a=o[5:6], ici_0=o[6:])),
        "s5_scan_fwd": _Ride(lambda: _join([_plan_gather_d2d(box["ici_b"]), _plan_gather_d2d(box["ici_0"])]), arrived),
    }
    h, sv0 = _layer_fwd(0, x, tabs, q0, rides)
    q1 = _layer_params(1, _full_weights(list(box["w1_a"]) + list(box["w1_b"])), small)
    h, sv1 = _layer_fwd(1, h, tabs, q1)
    loss, d = _loss_head(h, target, "loss_head")
    d, g1 = _layer_bwd(1, d, tabs, q1, sv1)
    parts1 = _owner_major(g1, SHARDED, "l1")

    def chips_plan(names, parts, got, tag):
        return _plan_reduce_chips([_add_sibling(p, g, f"reduce_add_{tag}_{k}") for k, p, g in zip(names, parts, got)])

    g0 = {}
    rides = {
        "fox_attn_delta": _Ride(lambda: _plan_reduce_sibling(parts1), lambda o: box.update(got1=o)),
        "fox_attn_bwd": _Ride(lambda: _join([chips_plan(SHARDED, parts1, box["got1"], "l1"),
                                             _plan_reduce_sibling(box.setdefault("early0", _owner_major(g0, EARLY, "l0")))]),
                              lambda o: box.update(contribs1=o[:6], got0e=o[6:])),
        "mla_attn_bwd": _Ride(lambda: chips_plan(EARLY, box["early0"], box["got0e"], "l0"), lambda o: box.update(contribs0e=o)),
        "dh1": _Ride(lambda: _plan_reduce_sibling(box.setdefault("late0", _owner_major(g0, LATE, "l0"))),
                     lambda o: box.update(got0l=o)),
        "dh4": _Ride(lambda: chips_plan(LATE, box["late0"], box["got0l"], "l0"), lambda o: box.update(contribs0l=o)),
    }
    d, _ = _layer_bwd(0, d, tabs, q0, sv0, rides, g0)
    contribs0 = list(box["contribs0l"]) + list(box["contribs0e"])
    return loss[0, 0], d, [g0, g1], [contribs0, box["contribs1"]]


def kernel(x, positions, norm_g, w_in, mla_q_a_norm, mla_w_q_up, mla_kv_a_norm, mla_w_kv_up, mla_q_norm, mla_k_norm, fox_b_f, fox_q_norm, fox_k_norm, s5_lambda_re, s5_lambda_im, s5_log_dt, s5_b_re, s5_b_im, s5_c_re, s5_c_im, s5_d, s5_w_glu, s5_b_glu, w_branch_out, w_out, loss_target, m_norm_g, m_w_in, m_mla_q_a_norm, m_mla_w_q_up, m_mla_kv_a_norm, m_mla_w_kv_up, m_mla_q_norm, m_mla_k_norm, m_fox_b_f, m_fox_q_norm, m_fox_k_norm, m_s5_lambda_re, m_s5_lambda_im, m_s5_log_dt, m_s5_b_re, m_s5_b_im, m_s5_c_re, m_s5_c_im, m_s5_d, m_s5_w_glu, m_s5_b_glu, m_w_branch_out, m_w_out, v_norm_g, v_w_in, v_mla_q_a_norm, v_mla_w_q_up, v_mla_kv_a_norm, v_mla_w_kv_up, v_mla_q_norm, v_mla_k_norm, v_fox_b_f, v_fox_q_norm, v_fox_k_norm, v_s5_lambda_re, v_s5_lambda_im, v_s5_log_dt, v_s5_b_re, v_s5_b_im, v_s5_c_re, v_s5_c_im, v_s5_d, v_s5_w_glu, v_s5_b_glu, v_w_branch_out, v_w_out):
    env = dict(locals())
    wts = {k: env[k] for k in WEIGHTS}
    mom = {k: env["m_" + k] for k in WEIGHTS}
    var = {k: env["v_" + k] for k in WEIGHTS}

    shards = [[wts[k][l].astype(BF16) for k in SHARDED] for l in range(DEPTH)]
    small = {k: wts[k] for k in SMALL}
    loss, dx, grads, contribs = _device_step(x[0], positions[0], loss_target[0], shards, small)
    loss = lax.psum(loss, ("x", "y", "c"))

    two_d = {k: (wts[k].shape[0] * wts[k].shape[1], wts[k].shape[2]) for k in SHARDED}
    sm = {k: jnp.stack([g[k] for g in grads]).reshape(wts[k].shape) for k in SMALL}
    small_block = _pack_rows(jnp.concatenate([sm[k].reshape(-1) for k in SMALL]), LANES, 256)
    box = {}
    rides = {SHARDED[0]: _Ride(lambda: _plan_gather_ici([small_block]), lambda o: box.update(ici=o)),
             SHARDED[1]: _Ride(lambda: _plan_gather_d2d(box["ici"]), lambda o: box.update(all=o))}
    grad_out, delta_out, m_out, v_out = {}, {}, {}, {}
    for i, k in enumerate(SHARDED):
        shp = wts[k].shape
        res = _adamw_sum(wts[k].reshape(two_d[k]), [contribs[l][i] for l in range(DEPTH)], mom[k].reshape(two_d[k]),
                         var[k].reshape(two_d[k]), f"adamw_{k}", ride=rides.get(k))
        grad_out[k], delta_out[k], m_out[k], v_out[k] = (z.reshape(shp) for z in res)
    g_small = _sum_leading(box["all"][0], "sum_small_grads").reshape(-1)
    off = 0
    for k in SMALL:
        cnt = int(np.prod(wts[k].shape))
        grad_out[k] = g_small[off:off + cnt].reshape(wts[k].shape)
        off += cnt
    flat2 = lambda a: a.reshape(-1, a.shape[-1])
    d_s, m_s, v_s = _adamw_many([flat2(wts[k]) for k in SMALL], [flat2(grad_out[k]) for k in SMALL],
                                [flat2(mom[k]) for k in SMALL], [flat2(var[k]) for k in SMALL], "adamw_small")
    for i, k in enumerate(SMALL):
        delta_out[k], m_out[k], v_out[k] = (z[i].reshape(wts[k].shape) for z in (d_s, m_s, v_s))

    return (loss, dx[None], *[grad_out[k] for k in WEIGHTS], *[delta_out[k] for k in WEIGHTS],
            *[m_out[k] for k in WEIGHTS], *[v_out[k] for k in WEIGHTS])
```

```python
import functools
import math

import jax
import jax.numpy as jnp
import numpy as np
from jax import lax
from jax.experimental import pallas as pl
from jax.experimental.pallas import tpu as pltpu

F32 = jnp.float32
BF16 = jnp.bfloat16

D_MODEL = 1024
DEPTH = 2
CHUNK = 64
EPS = 1e-6
HEADS = 8
MLA_NOPE, MLA_ROPE, MLA_V = 64, 32, 64
MLA_Q_RANK, MLA_KV_RANK = 256, 128
MLA_QK = MLA_NOPE + MLA_ROPE
ROPE_THETA = 10000.0
FOX_DIM = 64
S5_WIDTH, S5_GROUP, S5_GROUPS, S5_STATE = 512, 16, 32, 64
S5_LANES = S5_GROUPS * S5_STATE
IN_WIDTH = 7080
N_DEV = 8
LANES = 128
SUBLANES = 8

ADAM_LR, ADAM_B1, ADAM_B2, ADAM_EPS, ADAM_WD, ADAM_STEP = 0.001, 0.9, 0.999, 1e-08, 0.01, 10

SEG_W = (512, 1536, 512, 1536, 3072)
SEG_OFF = (0, 512, 2048, 2560, 4096)
PAD_IN = 7168
NEG = -1e30

SHARDED = ("w_in", "mla_w_q_up", "mla_w_kv_up", "s5_w_glu", "w_branch_out", "w_out")
COL_SHARDED = ("w_in", "mla_w_q_up", "mla_w_kv_up")
SMALL = ("norm_g", "mla_q_a_norm", "mla_kv_a_norm", "mla_q_norm", "mla_k_norm", "fox_b_f", "fox_q_norm", "fox_k_norm",
         "s5_lambda_re", "s5_lambda_im", "s5_log_dt", "s5_b_re", "s5_b_im", "s5_c_re", "s5_c_im", "s5_d", "s5_b_glu")
WEIGHTS = ("norm_g", "w_in", "mla_q_a_norm", "mla_w_q_up", "mla_kv_a_norm", "mla_w_kv_up", "mla_q_norm", "mla_k_norm",
           "fox_b_f", "fox_q_norm", "fox_k_norm", "s5_lambda_re", "s5_lambda_im", "s5_log_dt", "s5_b_re", "s5_b_im",
           "s5_c_re", "s5_c_im", "s5_d", "s5_w_glu", "s5_b_glu", "w_branch_out", "w_out")


def _pick(n, cands):
    for c in cands:
        if n % c == 0:
            return c
    return n


def _vmem(mb):
    return pltpu.CompilerParams(vmem_limit_bytes=mb * 1024 * 1024)


def _dot(a, b, dims):
    return lax.dot_general(a.astype(BF16), b.astype(BF16), (dims, ((), ())), preferred_element_type=F32)


def _nn(a, b):
    return _dot(a, b, ((1,), (0,)))


def _nt(a, b):
    return _dot(a, b, ((1,), (1,)))


def _tn(a, b):
    return _dot(a, b, ((0,), (0,)))


def _rms(x, g, n=None):
    n = x.shape[-1] if n is None else n
    return x * lax.rsqrt(jnp.sum(x * x, axis=-1, keepdims=True) / n + EPS) * g


def _rope(t, c, sa, sb):
    return t * c + pltpu.roll(t, LANES - 16, 1) * sa + pltpu.roll(t, 16, 1) * sb


def _rope_t(d, c, sa, sb):
    return d * c + pltpu.roll(d * sa, 16, 1) + pltpu.roll(d * sb, LANES - 16, 1)


def _mm(a, b, mode, name, acc=None, b_cols=None, ride=None):
    if mode == "tn":
        kd, m = a.shape
    else:
        m, kd = a.shape
    b_off, b_w = b_cols if b_cols is not None else (0, b.shape[1])
    n = b.shape[0] if mode == "nt" else b_w
    tm, tn, tk = _pick(m, (1024, 512, 256, 128)), _pick(n, (1024, 512, 256, 128)), _pick(kd, (1024, 512, 256, 128))
    nk = kd // tk
    if mode == "tn":
        a_spec = pl.BlockSpec((tk, tm), lambda i, j, k: (k, i))
    else:
        a_spec = pl.BlockSpec((tm, tk), lambda i, j, k: (i, k))
    if mode == "nt":
        assert b_off % tk == 0
        b_spec = pl.BlockSpec((tn, tk), lambda i, j, k: (j, k + b_off // tk))
    else:
        assert b_off % tn == 0
        b_spec = pl.BlockSpec((tk, tn), lambda i, j, k: (k, j + b_off // tn))
    dims = {"nn": ((1,), (0,)), "nt": ((1,), (1,)), "tn": ((0,), (0,))}[mode]
    o_spec = pl.BlockSpec((tm, tn), lambda i, j, k: (i, j))
    has_acc = acc is not None

    def body(*refs):
        if has_acc:
            a_ref, b_ref, c_ref, o_ref = refs
        else:
            a_ref, b_ref, o_ref = refs
        k = pl.program_id(2)
        prod = _dot(a_ref[...], b_ref[...], dims)

        @pl.when(k == 0)
        def _():
            o_ref[...] = prod + c_ref[...] if has_acc else prod

        @pl.when(k > 0)
        def _():
            o_ref[...] += prod

    ins = [a, b] + ([acc] if has_acc else [])
    in_specs = [a_spec, b_spec] + ([o_spec] if has_acc else [])
    return _call_with_ride(body, name, (m // tm, n // tn, nk), ins, in_specs, [jax.ShapeDtypeStruct((m, n), F32)], [o_spec],
                           ("parallel", "parallel", "arbitrary"), 48, ride)[0]


def _stage(name, fn, n_steps, ins, outs, accs=(), scratch=(), vmem_mb=48):
    n_in, n_out, n_acc = len(ins), len(outs), len(accs)

    def body(*refs):
        in_refs = refs[:n_in]
        out_refs = refs[n_in:n_in + n_out]
        acc_refs = refs[n_in + n_out:n_in + n_out + n_acc]
        scr = refs[n_in + n_out + n_acc:]
        if n_acc:
            @pl.when(pl.program_id(0) == 0)
            def _():
                for r in acc_refs:
                    r[...] = jnp.zeros(r.shape, r.dtype)
        fn(in_refs, out_refs, acc_refs, scr)

    acc_specs = [pl.BlockSpec(a.shape, functools.partial(lambda i, nd: (0,) * nd, nd=len(a.shape))) for a in accs]
    res = pl.pallas_call(
        body, name=name, grid=(n_steps,),
        in_specs=[s for _, s in ins], out_specs=[s for _, s in outs] + acc_specs,
        out_shape=[s for s, _ in outs] + list(accs), scratch_shapes=list(scratch),
        compiler_params=pltpu.CompilerParams(dimension_semantics=("arbitrary",),
                                             vmem_limit_bytes=vmem_mb * 1024 * 1024),
    )(*[a for a, _ in ins])
    return res


def _rows(ts, w, j=0):
    return pl.BlockSpec((ts, w), lambda i: (i, j))


def _rows_rev(ts, w, n, j=0):
    return pl.BlockSpec((ts, w), lambda i: (n - 1 - i, j))


def _heads(ts, d):
    return pl.BlockSpec((HEADS, ts, d), lambda i: (0, i, 0))


def _heads_rev(ts, d, n):
    return pl.BlockSpec((HEADS, ts, d), lambda i: (0, n - 1 - i, 0))


def _full(shape):
    nd = len(shape)
    return pl.BlockSpec(tuple(shape), lambda i: (0,) * nd)


def _sds(shape, dtype=F32):
    return jax.ShapeDtypeStruct(tuple(shape), dtype)


def _norm_fwd(x, g, name):
    s = x.shape[0]
    ts = _pick(s, (256, 128))

    def fn(ins, outs, accs, scr):
        outs[0][...] = _rms(ins[0][...], ins[1][...]).astype(BF16)

    return _stage(name, fn, s // ts, [(x, _rows(ts, D_MODEL)), (g, _full(g.shape))],
                  [(_sds((s, D_MODEL), BF16), _rows(ts, D_MODEL))])[0]


def _norm_bwd(x, g, dh, dres, name):
    s = x.shape[0]
    ts = _pick(s, (256, 128))

    def fn(ins, outs, accs, scr):
        _, vjp = jax.vjp(_rms, ins[0][...], ins[1][...])
        dx, dg = vjp(ins[2][...])
        outs[0][...] = dx + ins[3][...]
        accs[0][...] += dg

    r = _stage(name, fn, s // ts,
               [(x, _rows(ts, D_MODEL)), (g, _full(g.shape)), (dh, _rows(ts, D_MODEL)), (dres, _rows(ts, D_MODEL))],
               [(_sds((s, D_MODEL)), _rows(ts, D_MODEL))], accs=[_sds((1, D_MODEL))])
    return r[0], r[1]


def _mla_q(qraw, c, sa, sb, qn):
    return _rms(_rope(qraw, c, sa, sb), qn, MLA_QK)


def _mla_prep_fwd(seg0, tabs, p, wq, wkv, name):
    s = seg0.shape[0]
    ts = _pick(s, (256, 128))

    def fn(ins, outs, accs, scr):
        blk, cos, sa, sb, qan, kvan, qn, kn, wq_r, wkv_r = ins
        b = blk[...]
        cq, ckv, kt = b[:, :256], b[:, 256:384], b[:, 384:512]
        lane = lax.broadcasted_iota(jnp.int32, kt.shape, 1)
        kpe = jnp.where(lane >= 64, kt, 0.0)
        q_raw = _nn(_rms(cq, qan[...]), wq_r[...])
        kv_raw = _nn(_rms(ckv, kvan[...]), wkv_r[...])
        c, a, bb = cos[...], sa[...], sb[...]
        for h in range(HEADS):
            outs[0][h] = _mla_q(q_raw[:, LANES * h:LANES * (h + 1)], c, a, bb, qn[...]).astype(BF16)
            outs[1][h] = _mla_q(kv_raw[:, LANES * h:LANES * (h + 1)] + kpe, c, a, bb, kn[...]).astype(BF16)
            outs[2][h] = kv_raw[:, 1024 + 64 * h:1024 + 64 * (h + 1)].astype(BF16)

    consts = [p["mla_q_a_norm"], p["mla_kv_a_norm"], p["mla_q_norm"], p["mla_k_norm"], wq, wkv]
    return _stage(name, fn, s // ts,
                  [(seg0, _rows(ts, 512))] + [(t, _rows(ts, LANES)) for t in tabs] + [(a, _full(a.shape)) for a in consts],
                  [(_sds((HEADS, s, LANES), BF16), _heads(ts, LANES)), (_sds((HEADS, s, LANES), BF16), _heads(ts, LANES)),
                   (_sds((HEADS, s, 64), BF16), _heads(ts, 64))])


def _mla_prep_bwd(seg0, tabs, p, wq, wkv, dq, dk, dv, dff, name):
    s = seg0.shape[0]
    ts = _pick(s, (256, 128))

    def fn(ins, outs, accs, scr):
        blk, cos, sa, sb, qan, kvan, qn, kn, wq_r, wkv_r, dq_r, dk_r, dv_r, dff_r = ins
        dqan, dkvan, dqn, dkn, dwq, dwkv = accs
        dqraw_s, dkvraw_s = scr
        b = blk[...]
        cq, ckv, kt = b[:, :256], b[:, 256:384], b[:, 384:512]
        lane = lax.broadcasted_iota(jnp.int32, kt.shape, 1)
        kpe = jnp.where(lane >= 64, kt, 0.0)
        cqn, vjp_cq = jax.vjp(_rms, cq, qan[...])
        ckvn, vjp_ckv = jax.vjp(_rms, ckv, kvan[...])
        q_raw = _nn(cqn, wq_r[...])
        kv_raw = _nn(ckvn, wkv_r[...])
        c, a, bb = cos[...], sa[...], sb[...]

        def head_bwd(raw, gain, d):
            t = _rope(raw, c, a, bb)
            _, vjp = jax.vjp(functools.partial(_rms, n=MLA_QK), t, gain)
            dt, dgain = vjp(d)
            return _rope_t(dt, c, a, bb), dgain

        dkpe = jnp.zeros(kt.shape, F32)
        for h in range(HEADS):
            dqh, dg = head_bwd(q_raw[:, LANES * h:LANES * (h + 1)], qn[...], dq_r[h])
            dqn[...] += dg
            dqraw_s[:, LANES * h:LANES * (h + 1)] = dqh
            dkh, dg = head_bwd(kv_raw[:, LANES * h:LANES * (h + 1)] + kpe, kn[...], dk_r[h])
            dkn[...] += dg
            dkvraw_s[:, LANES * h:LANES * (h + 1)] = dkh
            dkpe = dkpe + dkh
            dkvraw_s[:, 1024 + 64 * h:1024 + 64 * (h + 1)] = dv_r[h]
        dq_raw = dqraw_s[...]
        dkv_raw = dkvraw_s[...]
        dwq[...] += _tn(cqn, dq_raw)
        dwkv[...] += _tn(ckvn, dkv_raw)
        dcq, dg = vjp_cq(_nt(dq_raw, wq_r[...]))
        dqan[...] += dg
        dckv, dg = vjp_ckv(_nt(dkv_raw, wkv_r[...]))
        dkvan[...] += dg
        outs[0][:, 0:256] = dcq.astype(BF16)
        outs[0][:, 256:384] = dckv.astype(BF16)
        outs[0][:, 384:512] = (jnp.where(lane >= 64, dkpe, 0.0) + dff_r[...]).astype(BF16)

    consts = [p["mla_q_a_norm"], p["mla_kv_a_norm"], p["mla_q_norm"], p["mla_k_norm"], wq, wkv]
    return _stage(name, fn, s // ts,
                  [(seg0, _rows(ts, 512))] + [(t, _rows(ts, LANES)) for t in tabs] + [(a, _full(a.shape)) for a in consts]
                  + [(dq, _heads(ts, LANES)), (dk, _heads(ts, LANES)), (dv, _heads(ts, 64)), (dff, _rows(ts, LANES))],
                  [(_sds((s, 512), BF16), _rows(ts, 512))],
                  accs=[_sds((1, 256)), _sds((1, 128)), _sds((1, LANES)), _sds((1, LANES)), _sds(wq.shape), _sds(wkv.shape)],
                  scratch=[pltpu.VMEM((ts, 1024), F32), pltpu.VMEM((ts, 1536), F32)])


def _fox_prep_fwd(seg0, seg1, bf, qn, kn, name):
    s = seg0.shape[0]
    ts = _pick(s, (256, 128))
    steps = int(math.log2(ts))

    def fn(ins, outs, accs, scr):
        kt_r, x_r, bf_r, qn_r, kn_r = ins
        carry = scr[0]

        @pl.when(pl.program_id(0) == 0)
        def _():
            carry[...] = jnp.zeros(carry.shape, F32)

        x = x_r[...]
        for h in range(HEADS):
            outs[0][h] = _rms(x[:, 64 * h:64 * (h + 1)], qn_r[...]).astype(BF16)
            outs[1][h] = _rms(x[:, 512 + 64 * h:512 + 64 * (h + 1)], kn_r[...]).astype(BF16)
            outs[2][h] = x[:, 1024 + 64 * h:1024 + 64 * (h + 1)].astype(BF16)
        kt = kt_r[...]
        lane = lax.broadcasted_iota(jnp.int32, kt.shape, 1)
        row = lax.broadcasted_iota(jnp.int32, kt.shape, 0)
        cs = jnp.where(lane < HEADS, jax.nn.log_sigmoid(kt + bf_r[...]), 0.0)
        for k in range(steps):
            sh = 1 << k
            cs = cs + jnp.where(row >= sh, pltpu.roll(cs, sh, 0), 0.0)
        cs = cs + carry[0:1, :]
        outs[3][...] = cs
        outs[4][...] = cs.T[0:HEADS, :]
        carry[0:1, :] = cs[ts - 1:ts, :]

    return _stage(name, fn, s // ts,
                  [(seg0, _rows(ts, LANES, 3)), (seg1, _rows(ts, 1536)), (bf, _full(bf.shape)), (qn, _full(qn.shape)),
                   (kn, _full(kn.shape))],
                  [(_sds((HEADS, s, 64), BF16), _heads(ts, 64)), (_sds((HEADS, s, 64), BF16), _heads(ts, 64)),
                   (_sds((HEADS, s, 64), BF16), _heads(ts, 64)), (_sds((s, LANES)), _rows(ts, LANES)),
                   (_sds((HEADS, s)), pl.BlockSpec((HEADS, ts), lambda i: (0, i)))],
                  scratch=[pltpu.VMEM((SUBLANES, LANES), F32)])


def _fox_prep_bwd(seg0, seg1, bf, qn, kn, dq, dk, dv, dck, name):
    s = seg0.shape[0]
    ts = _pick(s, (256, 128))
    n = s // ts
    steps = int(math.log2(ts))

    def fn(ins, outs, accs, scr):
        kt_r, x_r, bf_r, qn_r, kn_r, dq_r, dk_r, dv_r, dck_r = ins
        dqn, dkn, dbf = accs
        carry, dbuf = scr

        @pl.when(pl.program_id(0) == 0)
        def _():
            carry[...] = jnp.zeros(carry.shape, F32)

        x = x_r[...]
        for h in range(HEADS):
            _, vjp = jax.vjp(_rms, x[:, 64 * h:64 * (h + 1)], qn_r[...])
            d, dg = vjp(dq_r[h])
            dbuf[:, 64 * h:64 * (h + 1)] = d
            dqn[...] += dg
            _, vjp = jax.vjp(_rms, x[:, 512 + 64 * h:512 + 64 * (h + 1)], kn_r[...])
            d, dg = vjp(dk_r[h])
            dbuf[:, 512 + 64 * h:512 + 64 * (h + 1)] = d
            dkn[...] += dg
            dbuf[:, 1024 + 64 * h:1024 + 64 * (h + 1)] = dv_r[h]
        outs[0][...] = dbuf[...].astype(BF16)
        dc = dck_r[...].reshape(HEADS, ts)
        dc = jnp.concatenate([dc, jnp.zeros((LANES - HEADS, ts), F32)], axis=0).T
        row = lax.broadcasted_iota(jnp.int32, dc.shape, 0)
        lane = lax.broadcasted_iota(jnp.int32, dc.shape, 1)
        for k in range(steps):
            sh = 1 << k
            dc = dc + jnp.where(row < ts - sh, pltpu.roll(dc, ts - sh, 0), 0.0)
        dc = dc + carry[0:1, :]
        carry[0:1, :] = dc[0:1, :]
        dff = jnp.where(lane < HEADS, dc * jax.nn.sigmoid(-(kt_r[...] + bf_r[...])), 0.0)
        outs[1][...] = dff
        dbf[...] += jnp.sum(dff, axis=0, keepdims=True)

    return _stage(name, fn, n,
                  [(seg0, _rows_rev(ts, LANES, n, 3)), (seg1, _rows_rev(ts, 1536, n)), (bf, _full(bf.shape)),
                   (qn, _full(qn.shape)), (kn, _full(kn.shape)), (dq, _heads_rev(ts, 64, n)), (dk, _heads_rev(ts, 64, n)),
                   (dv, _heads_rev(ts, 64, n)), (dck, pl.BlockSpec((HEADS, 1, ts), lambda i: (0, 0, n - 1 - i)))],
                  [(_sds((s, 1536), BF16), _rows_rev(ts, 1536, n)), (_sds((s, LANES)), _rows_rev(ts, LANES, n))],
                  accs=[_sds((1, 64)), _sds((1, 64)), _sds((1, LANES))],
                  scratch=[pltpu.VMEM((SUBLANES, LANES), F32), pltpu.VMEM((ts, 1536), F32)])


def _allowed(i, j, t, chunk_causal):
    qpos = i * t + lax.broadcasted_iota(jnp.int32, (t, t), 0)
    kpos = j * t + lax.broadcasted_iota(jnp.int32, (t, t), 1)
    if chunk_causal:
        return (kpos // CHUNK) <= (qpos // CHUNK)
    return kpos <= qpos


def _pick_col(c_blk, h):
    lane = lax.broadcasted_iota(jnp.int32, c_blk.shape, 1)
    return jnp.sum(jnp.where(lane == h, c_blk, 0.0), axis=1, keepdims=True)


class _Ride:
    def __init__(self, make, take):
        self.make, self.take = make, take


class _Plan:
    def __init__(self, ins, out_shapes, n_remote, n_local, copies, aliases=None):
        self.ins, self.out_shapes, self.n_remote, self.n_local = list(ins), list(out_shapes), n_remote, n_local
        self.copies, self.aliases = copies, dict(aliases or {})

    def scratch(self):
        return [pltpu.SemaphoreType.DMA((self.n_remote,)), pltpu.SemaphoreType.DMA((self.n_remote,)),
                pltpu.SemaphoreType.DMA((max(self.n_local, 1),))]

    def start(self, in_refs, out_refs, sems):
        remote, local = self.copies(in_refs, out_refs, *sems)
        for cp in local + remote:
            cp.start()

    def wait(self, in_refs, out_refs, sems):
        remote, local = self.copies(in_refs, out_refs, *sems)
        for cp in remote:
            cp.wait()
        for cp in local:
            cp.wait()


class _Off:
    def __init__(self, ref, off):
        self.ref, self.off, self.at = ref, off, self

    def __getitem__(self, k):
        return self.ref.at[k + self.off]


def _join(plans):
    ins = [a for p in plans for a in p.ins]
    outs = [o for p in plans for o in p.out_shapes]
    aliases, i0, o0 = {}, 0, 0
    for p in plans:
        aliases.update({i0 + i: o0 + o for i, o in p.aliases.items()})
        i0, o0 = i0 + len(p.ins), o0 + len(p.out_shapes)

    def copies(in_refs, out_refs, send, recv, local):
        rem, loc, i0, o0, r0, l0 = [], [], 0, 0, 0, 0
        for p in plans:
            r, l = p.copies(in_refs[i0:i0 + len(p.ins)], out_refs[o0:o0 + len(p.out_shapes)], _Off(send, r0), _Off(recv, r0),
                            _Off(local, l0))
            rem, loc = rem + r, loc + l
            i0, o0, r0, l0 = i0 + len(p.ins), o0 + len(p.out_shapes), r0 + p.n_remote, l0 + p.n_local
        return rem, loc

    return _Plan(ins, outs, sum(p.n_remote for p in plans), sum(p.n_local for p in plans), copies, aliases)


def _call_with_ride(core, name, grid, ins, in_specs, out_shape, out_specs, semantics, vmem_mb, ride, scratch=()):
    n_in, n_out, n_scr = len(ins), len(out_shape), len(scratch)
    if ride is None:
        return pl.pallas_call(
            core, name=name, grid=grid, in_specs=in_specs, out_specs=out_specs, out_shape=out_shape,
            scratch_shapes=list(scratch),
            compiler_params=pltpu.CompilerParams(dimension_semantics=semantics, vmem_limit_bytes=vmem_mb * 1024 * 1024),
        )(*ins)
    plan = ride.make()
    ci, co = len(plan.ins), len(plan.out_shapes)

    def body(*refs):
        c_in = refs[n_in:n_in + ci]
        a_out = refs[n_in + ci:n_in + ci + n_out]
        c_out = refs[n_in + ci + n_out:n_in + ci + n_out + co]
        own = refs[n_in + ci + n_out + co:n_in + ci + n_out + co + n_scr]
        sems = refs[n_in + ci + n_out + co + n_scr:]
        ids = [pl.program_id(d) for d in range(len(grid))]
        first = functools.reduce(jnp.logical_and, [i == 0 for i in ids])
        last = functools.reduce(jnp.logical_and, [i == g - 1 for i, g in zip(ids, grid)])

        @pl.when(first)
        def _():
            plan.start(c_in, c_out, sems)

        core(*refs[:n_in], *a_out, *own)

        @pl.when(last)
        def _():
            plan.wait(c_in, c_out, sems)

    res = pl.pallas_call(
        body, name=name, grid=grid, in_specs=list(in_specs) + [ANY] * ci, out_specs=list(out_specs) + [ANY] * co,
        out_shape=list(out_shape) + plan.out_shapes, scratch_shapes=list(scratch) + plan.scratch(),
        input_output_aliases={n_in + i: n_out + o for i, o in plan.aliases.items()},
        compiler_params=pltpu.CompilerParams(dimension_semantics=("arbitrary",) * len(grid),
                                             vmem_limit_bytes=vmem_mb * 1024 * 1024),
    )(*ins, *plan.ins)
    ride.take(res[n_out:])
    return res[:n_out]


def _attn_fwd(q, k, v, scale, chunk_causal, name, c=None, ct=None, hps=8, ride=None):
    _, s, dk = q.shape
    dv = v.shape[2]
    t = _pick(s, (256, 128))
    bias = c is not None

    def body(*refs):
        if bias:
            q_ref, k_ref, v_ref, c_ref, ct_ref, o_ref, lse_ref = refs
        else:
            q_ref, k_ref, v_ref, o_ref, lse_ref = refs
        hp, i = pl.program_id(0), pl.program_id(1)
        qb = [q_ref[e] for e in range(hps)]
        cq = [_pick_col(c_ref[...], hp * hps + e) if bias else None for e in range(hps)]

        def step(j, carry, diagonal):
            off = pl.multiple_of(j * t, t)
            out = []
            for e in range(hps):
                m, l, acc = carry[e]
                sc = _nt(qb[e], k_ref[e, pl.ds(off, t), :]) * scale
                if bias:
                    sc = sc + (cq[e] - ct_ref[pl.ds(hp * hps + e, 1), pl.ds(off, t)])
                if diagonal:
                    sc = jnp.where(_allowed(i, j, t, chunk_causal), sc, NEG)
                m_new = jnp.maximum(m, jnp.max(sc, axis=1, keepdims=True))
                pr = jnp.exp(sc - m_new)
                alpha = jnp.exp(m - m_new)
                out.append((m_new, alpha * l + jnp.sum(pr, axis=1, keepdims=True),
                            alpha * acc + _nn(pr, v_ref[e, pl.ds(off, t), :])))
            return tuple(out)

        init = tuple((jnp.full((t, 1), NEG, F32), jnp.zeros((t, 1), F32), jnp.zeros((t, dv), F32)) for _ in range(hps))
        res = step(i, lax.fori_loop(0, i, functools.partial(step, diagonal=False), init), True)
        for e in range(hps):
            m, l, acc = res[e]
            o_ref[e] = acc / l
            lse_ref[e] = m + jnp.log(l)

    ins = [q, k, v] + ([c, ct] if bias else [])
    in_specs = [pl.BlockSpec((hps, t, dk), lambda h, i: (h, i, 0)), pl.BlockSpec((hps, s, dk), lambda h, i: (h, 0, 0)),
                pl.BlockSpec((hps, s, dv), lambda h, i: (h, 0, 0))]
    if bias:
        in_specs += [pl.BlockSpec((t, LANES), lambda h, i: (i, 0)), pl.BlockSpec((HEADS, s), lambda h, i: (0, 0))]
    return _call_with_ride(
        body, name, (HEADS // hps, s // t), ins, in_specs, [_sds((HEADS, s, dv)), _sds((HEADS, s, 1))],
        [pl.BlockSpec((hps, t, dv), lambda h, i: (h, i, 0)), pl.BlockSpec((hps, t, 1), lambda h, i: (h, i, 0))],
        ("parallel", "parallel"), 48, ride)


def _attn_delta(q, k, v, do, lse, scale, chunk_causal, name, c=None, ct=None, hps=4, ride=None):
    _, s, dk = q.shape
    dv = v.shape[2]
    t = _pick(s, (256, 128))
    bias = c is not None

    def body(*refs):
        if bias:
            q_ref, k_ref, v_ref, do_ref, lse_ref, c_ref, ct_ref, dl_ref = refs
        else:
            q_ref, k_ref, v_ref, do_ref, lse_ref, dl_ref = refs
        hp, i = pl.program_id(0), pl.program_id(1)
        qb, dob, lse_b = ([r[e] for e in range(hps)] for r in (q_ref, do_ref, lse_ref))
        cq = [_pick_col(c_ref[...], hp * hps + e) if bias else None for e in range(hps)]

        def step(j, acc, diagonal):
            off = pl.multiple_of(j * t, t)
            out = []
            for e in range(hps):
                sc = _nt(qb[e], k_ref[e, pl.ds(off, t), :]) * scale
                if bias:
                    sc = sc + (cq[e] - ct_ref[pl.ds(hp * hps + e, 1), pl.ds(off, t)])
                pr = jnp.exp(sc - lse_b[e])
                if diagonal:
                    pr = jnp.where(_allowed(i, j, t, chunk_causal), pr, 0.0)
                out.append(acc[e] + jnp.sum(pr * _nt(dob[e], v_ref[e, pl.ds(off, t), :]), axis=1, keepdims=True))
            return tuple(out)

        init = tuple(jnp.zeros((t, 1), F32) for _ in range(hps))
        res = step(i, lax.fori_loop(0, i, functools.partial(step, diagonal=False), init), True)
        for e in range(hps):
            dl_ref[e] = res[e]

    ins = [q, k, v, do, lse] + ([c, ct] if bias else [])
    in_specs = [pl.BlockSpec((hps, t, dk), lambda h, i: (h, i, 0)), pl.BlockSpec((hps, s, dk), lambda h, i: (h, 0, 0)),
                pl.BlockSpec((hps, s, dv), lambda h, i: (h, 0, 0)), pl.BlockSpec((hps, t, dv), lambda h, i: (h, i, 0)),
                pl.BlockSpec((hps, t, 1), lambda h, i: (h, i, 0))]
    if bias:
        in_specs += [pl.BlockSpec((t, LANES), lambda h, i: (i, 0)), pl.BlockSpec((HEADS, s), lambda h, i: (0, 0))]
    return _call_with_ride(body, name, (HEADS // hps, s // t), ins, in_specs, [_sds((HEADS, s, 1))],
                           [pl.BlockSpec((hps, t, 1), lambda h, i: (h, i, 0))], ("parallel", "parallel"), 48, ride)[0]


def _attn_bwd(q, k, v, do, lse, delta, scale, chunk_causal, name, c=None, ct=None, hps=8, ride=None):
    _, s, dk = q.shape
    dv = v.shape[2]
    t = _pick(s, (256, 128))
    n = s // t
    bias = c is not None

    def body(*refs):
        if bias:
            q_ref, k_ref, v_ref, do_ref, lse_ref, dl_ref, c_ref, ct_ref, dq_ref, dk_ref, dv_ref, dck_ref = refs
        else:
            q_ref, k_ref, v_ref, do_ref, lse_ref, dl_ref, dq_ref, dk_ref, dv_ref = refs
        hp, j = pl.program_id(0), pl.program_id(1)

        @pl.when(j == 0)
        def _():
            dq_ref[...] = jnp.zeros(dq_ref.shape, F32)

        kb, vb = [k_ref[e] for e in range(hps)], [v_ref[e] for e in range(hps)]
        joff = pl.multiple_of(j * t, t)
        ck = [ct_ref[pl.ds(hp * hps + e, 1), pl.ds(joff, t)] if bias else None for e in range(hps)]

        def step(i, carry, diagonal):
            off = pl.multiple_of(i * t, t)
            out = []
            for e in range(hps):
                dk_acc, dv_acc, dck_acc = carry[e]
                qb = q_ref[e, pl.ds(off, t), :]
                dob = do_ref[e, pl.ds(off, t), :]
                sc = _nt(qb, kb[e]) * scale
                if bias:
                    sc = sc + (_pick_col(c_ref[pl.ds(off, t), :], hp * hps + e) - ck[e])
                pr = jnp.exp(sc - lse_ref[e, pl.ds(off, t), :])
                if diagonal:
                    pr = jnp.where(_allowed(i, j, t, chunk_causal), pr, 0.0)
                dv_acc = dv_acc + _tn(pr, dob)
                ds = pr * (_nt(dob, vb[e]) - dl_ref[e, pl.ds(off, t), :])
                dq_ref[e, pl.ds(off, t), :] += _nn(ds, kb[e]) * scale
                dk_acc = dk_acc + _tn(ds, qb) * scale
                if bias:
                    dck_acc = dck_acc - jnp.sum(ds, axis=0, keepdims=True)
                out.append((dk_acc, dv_acc, dck_acc))
            return tuple(out)

        zero = tuple((jnp.zeros((t, dk), F32), jnp.zeros((t, dv), F32), jnp.zeros((1, t), F32)) for _ in range(hps))
        res = lax.fori_loop(j + 1, n, functools.partial(step, diagonal=False), step(j, zero, True))
        for e in range(hps):
            dk_ref[e], dv_ref[e] = res[e][0], res[e][1]
            if bias:
                dck_ref[e] = res[e][2]

    ins = [q, k, v, do, lse, delta] + ([c, ct] if bias else [])
    full = lambda d: pl.BlockSpec((hps, s, d), lambda h, j: (h, 0, 0))
    blk = lambda d: pl.BlockSpec((hps, t, d), lambda h, j: (h, j, 0))
    in_specs = [full(dk), blk(dk), blk(dv), full(dv), full(1), full(1)]
    if bias:
        in_specs += [pl.BlockSpec((s, LANES), lambda h, j: (0, 0)), pl.BlockSpec((HEADS, s), lambda h, j: (0, 0))]
    out_specs = [full(dk), blk(dk), blk(dv)]
    out_shape = [_sds((HEADS, s, dk)), _sds((HEADS, s, dk)), _sds((HEADS, s, dv))]
    if bias:
        out_specs.append(pl.BlockSpec((hps, 1, t), lambda h, j: (h, 0, j)))
        out_shape.append(_sds((HEADS, 1, s)))
    return _call_with_ride(body, name, (HEADS // hps, n), ins, in_specs, out_shape, out_specs, ("parallel", "arbitrary"), 56,
                           ride)


def _s5_disc(lr, li, ldt, br, bi):
    dt = jnp.exp(ldt)
    mag = jnp.exp(lr * dt)
    a_re = mag * jnp.cos(li * dt)
    a_im = mag * jnp.sin(li * dt)
    den = lr * lr + li * li
    f_re = ((a_re - 1.0) * lr + a_im * li) / den
    f_im = (a_im * lr - (a_re - 1.0) * li) / den
    return a_re, a_im, f_re * br - f_im * bi, f_re * bi + f_im * br


def _s5_params_fwd(lr16, li16, ldt16, br2, bi2, name):
    def body(a, b, c, d, e, o0, o1, o2, o3):
        r = _s5_disc(a[...], b[...], c[...], d[...], e[...])
        o0[...], o1[...], o2[...], o3[...] = r

    return pl.pallas_call(body, name=name, out_shape=[_sds((512, 64))] * 4)(lr16, li16, ldt16, br2, bi2)


def _s5_params_bwd(lr16, li16, ldt16, br2, bi2, da_re16, da_im16, dbb_re, dbb_im, name):
    def body(a, b, c, d, e, g0, g1, g2, g3, o_lr, o_li, o_dt, o_br, o_bi):
        _, vjp = jax.vjp(_s5_disc, a[...], b[...], c[...], d[...], e[...])
        dlr, dli, dldt, dbr, dbi = vjp((g0[...], g1[...], g2[...], g3[...]))
        grp = lambda z: z.reshape(S5_GROUPS, S5_GROUP, S5_STATE).sum(axis=1)
        o_lr[...] = grp(dlr)
        o_li[...] = grp(dli)
        o_dt[...] = jnp.sum(grp(dldt), axis=1, keepdims=True)
        o_br[...] = dbr
        o_bi[...] = dbi

    return pl.pallas_call(
        body, name=name, out_shape=[_sds((32, 64)), _sds((32, 64)), _sds((32, 1)), _sds((512, 64)), _sds((512, 64))],
    )(lr16, li16, ldt16, br2, bi2, da_re16, da_im16, dbb_re, dbb_im)


def _cmul(ar, ai, br, bi):
    return ar * br - ai * bi, ar * bi + ai * br


S5_SUPER = 4


def _scan_loop(bre_r, bim_r, ar, ai, ore_r, oim_r, reverse, xre_r=None, xim_r=None):
    s, lw = bre_r.shape
    nt = s // SUBLANES
    with_da = xre_r is not None
    shp = (SUBLANES, lw)
    row = lax.broadcasted_iota(jnp.int32, shp, 0)
    pows = [(ar, ai)]
    for _ in range(SUBLANES - 1):
        pows.append(_cmul(pows[-1][0], pows[-1][1], ar, ai))
    cm_r, cm_i = jnp.zeros(shp, F32), jnp.zeros(shp, F32)
    for r in range(SUBLANES):
        e = (SUBLANES - 1 - r) if reverse else r
        cm_r = jnp.where(row == r, jnp.broadcast_to(pows[e][0], shp), cm_r)
        cm_i = jnp.where(row == r, jnp.broadcast_to(pows[e][1], shp), cm_i)
    steps = [(1, pows[0]), (2, pows[1]), (4, pows[3])]

    def tile(it, carry):
        if with_da:
            c_r, c_i, acc_r, acc_i = carry
        else:
            c_r, c_i = carry
        r = (nt - 1 - it) if reverse else it
        off = pl.multiple_of(r * SUBLANES, SUBLANES)
        xr, xi = bre_r[pl.ds(off, SUBLANES), :], bim_r[pl.ds(off, SUBLANES), :]
        for sh, (pr, pi) in steps:
            if reverse:
                keep = row < SUBLANES - sh
                sr = jnp.where(keep, pltpu.roll(xr, SUBLANES - sh, 0), 0.0)
                si = jnp.where(keep, pltpu.roll(xi, SUBLANES - sh, 0), 0.0)
            else:
                keep = row >= sh
                sr = jnp.where(keep, pltpu.roll(xr, sh, 0), 0.0)
                si = jnp.where(keep, pltpu.roll(xi, sh, 0), 0.0)
            mr, mi = _cmul(pr, pi, sr, si)
            xr, xi = xr + mr, xi + mi
        mr, mi = _cmul(cm_r, cm_i, c_r, c_i)
        xr, xi = xr + mr, xi + mi
        ore_r[pl.ds(off, SUBLANES), :] = xr
        oim_r[pl.ds(off, SUBLANES), :] = xi
        edge = 0 if reverse else SUBLANES - 1
        c_r, c_i = xr[edge:edge + 1, :], xi[edge:edge + 1, :]
        if not with_da:
            return c_r, c_i
        fr, fi = xre_r[pl.ds(off, SUBLANES), :], xim_r[pl.ds(off, SUBLANES), :]
        poff = pl.multiple_of(jnp.maximum(r - 1, 0) * SUBLANES, SUBLANES)
        live = (r > 0).astype(F32)
        pr_last = xre_r[pl.ds(poff, SUBLANES), :][SUBLANES - 1:SUBLANES, :] * live
        pi_last = xim_r[pl.ds(poff, SUBLANES), :][SUBLANES - 1:SUBLANES, :] * live
        sr = jnp.where(row >= 1, pltpu.roll(fr, 1, 0), jnp.broadcast_to(pr_last, shp))
        si = jnp.where(row >= 1, pltpu.roll(fi, 1, 0), jnp.broadcast_to(pi_last, shp))
        return c_r, c_i, acc_r + xr * sr + xi * si, acc_i + xi * sr - xr * si

    z1 = jnp.zeros((1, lw), F32)
    if not with_da:
        lax.fori_loop(0, nt, tile, (z1, z1))
        return None
    _, _, acc_r, acc_i = lax.fori_loop(0, nt, tile, (z1, z1, jnp.zeros(shp, F32), jnp.zeros(shp, F32)))
    return jnp.sum(acc_r, axis=0, keepdims=True), jnp.sum(acc_i, axis=0, keepdims=True)


S5_ROWS = 512


def _group_compact(p):
    grp = lax.broadcasted_iota(jnp.int32, (LANES, S5_STATE), 0) // S5_GROUP
    out = jnp.zeros((LANES, S5_STATE), F32)
    for j in range(LANES // S5_GROUP):
        out = jnp.where(grp == j, p[:, S5_STATE * j:S5_STATE * (j + 1)], out)
    return out


def _s5_core_fwd(u, wb_re, wb_im, wc_re, wc_im, a_re, a_im, name, ride=None):
    s = u.shape[0]
    lw = S5_LANES // S5_SUPER
    rows = _pick(s, (S5_ROWS, 256, 128))

    def body(u_r, wbr, wbi, wcr, wci, are_r, aim_r, xre_r, xim_r, y_r, bre_s, bim_s):
        for r0 in range(0, s, rows):
            ub = u_r[r0:r0 + rows, :]
            bre_s[r0:r0 + rows, :] = _nn(ub, wbr[0])
            bim_s[r0:r0 + rows, :] = _nn(ub, wbi[0])
        _scan_loop(bre_s, bim_s, are_r[...], aim_r[...], xre_r, xim_r, False)
        for r0 in range(0, s, rows):
            y_r[r0:r0 + rows, :] = _nn(xre_r[r0:r0 + rows, :], wcr[0]) + _nn(xim_r[r0:r0 + rows, :], wci[0])

    nar = pl.BlockSpec((s, LANES), lambda k: (0, k))
    wide = pl.BlockSpec((s, lw), lambda k: (0, k))
    one = pl.BlockSpec((1, lw), lambda k: (0, k))
    wb = pl.BlockSpec((1, LANES, lw), lambda k: (k, 0, 0))
    wc = pl.BlockSpec((1, lw, LANES), lambda k: (k, 0, 0))
    return _call_with_ride(body, name, (S5_SUPER,), [u, wb_re, wb_im, wc_re, wc_im, a_re, a_im], [nar, wb, wb, wc, wc, one, one],
                           [_sds((s, S5_LANES)), _sds((s, S5_LANES)), _sds((s, S5_WIDTH))], [wide, wide, nar], ("parallel",), 56,
                           ride, scratch=[pltpu.VMEM((s, lw), F32), pltpu.VMEM((s, lw), F32)])


def _s5_core_bwd(dy0, u, du_a, x_re, x_im, wb_re, wb_im, wc_re, wc_im, a_re, a_im_neg, name):
    s = u.shape[0]
    lw = S5_LANES // S5_SUPER
    rows = _pick(s, (S5_ROWS, 256, 128))

    def body(dy_r, u_r, dua_r, xre_r, xim_r, wbr, wbi, wcr, wci, are_r, aim_r, du_r, dare_r, daim_r, dbr_r, dbi_r, dcr_r, dci_r,
             dre_s, dim_s, gre_s, gim_s):
        for r0 in range(0, s, rows):
            dyb = dy_r[r0:r0 + rows, :]
            dre_s[r0:r0 + rows, :] = _nt(dyb, wcr[0])
            dim_s[r0:r0 + rows, :] = _nt(dyb, wci[0])
        dare_r[...], daim_r[...] = _scan_loop(dre_s, dim_s, are_r[...], aim_r[...], gre_s, gim_s, True, xre_r, xim_r)
        acc = [jnp.zeros((LANES, lw), F32) for _ in range(4)]
        for r0 in range(0, s, rows):
            sl = slice(r0, r0 + rows)
            gr, gi, ub, dyb = gre_s[sl, :], gim_s[sl, :], u_r[sl, :], dy_r[sl, :]
            du_r[sl, :] = dua_r[sl, :] + _nt(gr, wbr[0]) + _nt(gi, wbi[0])
            acc = [acc[0] + _tn(ub, gr), acc[1] + _tn(ub, gi), acc[2] + _tn(dyb, xre_r[sl, :]), acc[3] + _tn(dyb, xim_r[sl, :])]
        dbr_r[...], dbi_r[...], dcr_r[...], dci_r[...] = (_group_compact(a) for a in acc)

    nar = pl.BlockSpec((s, LANES), lambda k: (0, k))
    wide = pl.BlockSpec((s, lw), lambda k: (0, k))
    one = pl.BlockSpec((1, lw), lambda k: (0, k))
    wb = pl.BlockSpec((1, LANES, lw), lambda k: (k, 0, 0))
    wc = pl.BlockSpec((1, lw, LANES), lambda k: (k, 0, 0))
    blk = pl.BlockSpec((LANES, S5_STATE), lambda k: (k, 0))
    return _call_with_ride(
        body, name, (S5_SUPER,), [dy0, u, du_a, x_re, x_im, wb_re, wb_im, wc_re, wc_im, a_re, a_im_neg],
        [nar, nar, nar, wide, wide, wb, wb, wc, wc, one, one],
        [_sds((s, S5_WIDTH)), _sds((1, S5_LANES)), _sds((1, S5_LANES))] + [_sds((S5_WIDTH, S5_STATE))] * 4,
        [nar, one, one, blk, blk, blk, blk], ("parallel",), 60, None, scratch=[pltpu.VMEM((s, lw), F32)] * 4)


def _s5_seg1(y0, u, d):
    return jax.nn.gelu(y0 + d * u)


def _s5_seg2(z, t, b):
    return z * jax.nn.sigmoid(t + b)


def _s5_post_fwd(y0, seg2, d, wglu, bglu, name):
    s = y0.shape[0]
    ts = _pick(s, (256, 128))

    def fn(ins, outs, accs, scr):
        z = _s5_seg1(ins[0][...], ins[1][...], ins[2][...])
        outs[0][...] = _s5_seg2(z, _nn(z, ins[3][...]), ins[4][...])

    return _stage(name, fn, s // ts,
                  [(y0, _rows(ts, 512)), (seg2, _rows(ts, 512)), (d, _full(d.shape)), (wglu, _full(wglu.shape)),
                   (bglu, _full(bglu.shape))], [(_sds((s, 512)), _rows(ts, 512))])[0]


def _s5_post_bwd(y0, seg2, d, wglu, bglu, dy, name):
    s = y0.shape[0]
    ts = _pick(s, (256, 128))

    def fn(ins, outs, accs, scr):
        y0_r, u_r, d_r, w_r, b_r, dy_r = ins
        z, vjp1 = jax.vjp(_s5_seg1, y0_r[...], u_r[...], d_r[...])
        t = _nn(z, w_r[...])
        _, vjp2 = jax.vjp(_s5_seg2, z, t, b_r[...])
        dz, dt, db = vjp2(dy_r[...])
        accs[0][...] += _tn(z, dt)
        accs[1][...] += db
        dy0, du, dd = vjp1(dz + _nt(dt, w_r[...]))
        accs[2][...] += dd
        outs[0][...] = dy0
        outs[1][...] = du

    return _stage(name, fn, s // ts,
                  [(y0, _rows(ts, 512)), (seg2, _rows(ts, 512)), (d, _full(d.shape)), (wglu, _full(wglu.shape)),
                   (bglu, _full(bglu.shape)), (dy, _rows(ts, 512))],
                  [(_sds((s, 512)), _rows(ts, 512)), (_sds((s, 512)), _rows(ts, 512))],
                  accs=[_sds((512, 512)), _sds((1, 512)), _sds((1, 512))])


def _gate_a(y, g):
    return y * jax.nn.silu(g)


def _gate_m(o0, o1, o2, m0, m1, m2):
    return jax.nn.sigmoid(m0) * o0 + jax.nn.sigmoid(m1) * o1 + jax.nn.sigmoid(m2) * o2


def _assemble(ybuf, o_mla, o_fox, y_s5):
    for h in range(HEADS):
        ybuf[:, 64 * h:64 * (h + 1)] = o_mla[h]
        ybuf[:, 512 + 64 * h:512 + 64 * (h + 1)] = o_fox[h]
    ybuf[:, 1024:1536] = y_s5[...]


def _gate_fwd(o_mla, o_fox, y_s5, seg3, seg4, x, wo, wout, name):
    s = x.shape[0]
    ts = _pick(s, (256, 128))

    def fn(ins, outs, accs, scr):
        om, of, ys, g_r, m_r, x_r, wo_r, wout_r = ins
        ybuf = scr[0]
        _assemble(ybuf, om, of, ys)
        a = _gate_a(ybuf[...], g_r[...])
        o = [_nn(a[:, 512 * b:512 * (b + 1)], wo_r[512 * b:512 * (b + 1), :]) for b in range(3)]
        merged = _gate_m(o[0], o[1], o[2], m_r[:, 0:1024], m_r[:, 1024:2048], m_r[:, 2048:3072])
        outs[0][...] = x_r[...] + _nn(merged, wout_r[...])

    return _stage(name, fn, s // ts,
                  [(o_mla, _heads(ts, 64)), (o_fox, _heads(ts, 64)), (y_s5, _rows(ts, 512)), (seg3, _rows(ts, 1536)),
                   (seg4, _rows(ts, 3072)), (x, _rows(ts, D_MODEL)), (wo, _full(wo.shape)), (wout, _full(wout.shape))],
                  [(_sds((s, D_MODEL)), _rows(ts, D_MODEL))], scratch=[pltpu.VMEM((ts, 1536), F32)])[0]


def _gate_bwd(o_mla, o_fox, y_s5, seg3, seg4, wo, wout, dout, name):
    s = dout.shape[0]
    ts = _pick(s, (128,))

    def fn(ins, outs, accs, scr):
        om, of, ys, g_r, m_r, wo_r, wout_r, dout_r = ins
        do_mla, do_fox, dys, dg_r, dm_r, dl_mla = outs
        dwo, dwout = accs
        ybuf, dabuf = scr
        _assemble(ybuf, om, of, ys)
        a, vjp_a = jax.vjp(_gate_a, ybuf[...], g_r[...])
        o = [_nn(a[:, 512 * b:512 * (b + 1)], wo_r[512 * b:512 * (b + 1), :]) for b in range(3)]
        ms = [m_r[:, 1024 * b:1024 * (b + 1)] for b in range(3)]
        merged, vjp_m = jax.vjp(_gate_m, *o, *ms)
        dout_v = dout_r[...]
        dwout[...] += _tn(merged, dout_v)
        cts = vjp_m(_nt(dout_v, wout_r[...]))
        for b in range(3):
            dm_r[:, 1024 * b:1024 * (b + 1)] = cts[3 + b].astype(BF16)
            dwo[512 * b:512 * (b + 1), :] += _tn(a[:, 512 * b:512 * (b + 1)], cts[b])
            dabuf[:, 512 * b:512 * (b + 1)] = _nt(cts[b], wo_r[512 * b:512 * (b + 1), :])
        dy, dg = vjp_a(dabuf[...])
        dg_r[...] = dg.astype(BF16)
        dys[...] = dy[:, 1024:1536]
        for h in range(HEADS):
            d = dy[:, 64 * h:64 * (h + 1)]
            do_mla[h] = d
            dl_mla[h] = jnp.sum(d * om[h], axis=1, keepdims=True)
            do_fox[h] = dy[:, 512 + 64 * h:512 + 64 * (h + 1)]

    return _stage(name, fn, s // ts,
                  [(o_mla, _heads(ts, 64)), (o_fox, _heads(ts, 64)), (y_s5, _rows(ts, 512)), (seg3, _rows(ts, 1536)),
                   (seg4, _rows(ts, 3072)), (wo, _full(wo.shape)), (wout, _full(wout.shape)), (dout, _rows(ts, D_MODEL))],
                  [(_sds((HEADS, s, 64)), _heads(ts, 64)), (_sds((HEADS, s, 64)), _heads(ts, 64)), (_sds((s, 512)), _rows(ts, 512)),
                   (_sds((s, 1536), BF16), _rows(ts, 1536)), (_sds((s, 3072), BF16), _rows(ts, 3072)),
                   (_sds((HEADS, s, 1)), _heads(ts, 1))],
                  accs=[_sds(wo.shape), _sds(wout.shape)], scratch=[pltpu.VMEM((ts, 1536), F32), pltpu.VMEM((ts, 1536), F32)],
                  vmem_mb=56)


def _loss_head(y, target, name):
    s = y.shape[0]
    ts = _pick(s, (256, 128))

    def fn(ins, outs, accs, scr):
        e = ins[0][...] - ins[1][...]
        outs[0][...] = e / D_MODEL
        accs[0][...] += 0.5 * jnp.sum(jnp.sum(e * e, axis=1, keepdims=True) / D_MODEL, axis=0, keepdims=True)

    r = _stage(name, fn, s // ts, [(y, _rows(ts, D_MODEL)), (target, _rows(ts, D_MODEL))],
               [(_sds((s, D_MODEL)), _rows(ts, D_MODEL))], accs=[_sds((1, 1))])
    return r[1], r[0]


IN_RANGES = ((0, 384, 0), (384, 416, 448), (416, 1952, 512), (1952, 1960, 384), (1960, IN_WIDTH, 2048))
SHARD_W = IN_WIDTH // N_DEV


def _win_pieces(d):
    lo, hi = SHARD_W * d, SHARD_W * (d + 1)
    out = []
    for a, b, p in IN_RANGES:
        s, e = max(a, lo), min(b, hi)
        while s < e:
            pad = p + (s - a)
            k = max(i for i in range(5) if SEG_OFF[i] <= pad)
            w = min(e - s, SEG_OFF[k] + SEG_W[k] - pad)
            out.append((s - lo, w, k, pad - SEG_OFF[k]))
            s += w
    return out


def _win_pad(g, name):
    _, r, _ = g.shape
    tr = 64

    def body(g_ref, o_ref):
        o_ref[...] = jnp.zeros(o_ref.shape, o_ref.dtype)
        for d in range(N_DEV):
            for dst, w, k, src in _win_pieces(d):
                o_ref[:, SEG_OFF[k] + src:SEG_OFF[k] + src + w] = g_ref[d, :, dst:dst + w]

    return pl.pallas_call(
        body, name=name, grid=(r // tr,), in_specs=[pl.BlockSpec((N_DEV, tr, SHARD_W), lambda i: (0, i, 0))],
        out_specs=pl.BlockSpec((tr, PAD_IN), lambda i: (i, 0)), out_shape=jax.ShapeDtypeStruct((r, PAD_IN), g.dtype),
        compiler_params=pltpu.CompilerParams(dimension_semantics=("parallel",)),
    )(g)


def _win_unpad(dsegs, name, part=0, parts=1):
    r = dsegs[0].shape[0] // parts
    tr = 64
    i0 = part * (r // tr)

    def body(*refs):
        o_ref = refs[5]
        for d in range(N_DEV):
            for dst, w, k, src in _win_pieces(d):
                o_ref[d, :, dst:dst + w] = refs[k][:, src:src + w]

    return pl.pallas_call(
        body, name=name, grid=(r // tr,), in_specs=[pl.BlockSpec((tr, SEG_W[k]), lambda i: (i + i0, 0)) for k in range(5)],
        out_specs=pl.BlockSpec((N_DEV, tr, SHARD_W), lambda i: (0, i, 0)),
        out_shape=jax.ShapeDtypeStruct((N_DEV, r, SHARD_W), dsegs[0].dtype),
        compiler_params=pltpu.CompilerParams(dimension_semantics=("parallel",)),
    )(*dsegs)


def _pad_wq(w):
    w = w.reshape(MLA_Q_RANK, HEADS, MLA_QK)
    return jnp.pad(w, ((0, 0), (0, 0), (0, LANES - MLA_QK))).reshape(MLA_Q_RANK, HEADS * LANES)


def _unpad_wq(d):
    return d.reshape(MLA_Q_RANK, HEADS, LANES)[:, :, :MLA_QK].reshape(MLA_Q_RANK, HEADS * MLA_QK)


def _pad_wkv(w):
    w = w.reshape(MLA_KV_RANK, HEADS, MLA_NOPE + MLA_V)
    k = jnp.pad(w[:, :, :MLA_NOPE], ((0, 0), (0, 0), (0, LANES - MLA_NOPE))).reshape(MLA_KV_RANK, HEADS * LANES)
    return jnp.concatenate([k, w[:, :, MLA_NOPE:].reshape(MLA_KV_RANK, HEADS * MLA_V)], axis=1)


def _unpad_wkv(d):
    k = d[:, :HEADS * LANES].reshape(MLA_KV_RANK, HEADS, LANES)[:, :, :MLA_NOPE]
    v = d[:, HEADS * LANES:].reshape(MLA_KV_RANK, HEADS, MLA_V)
    return jnp.concatenate([k, v], axis=2).reshape(MLA_KV_RANK, HEADS * (MLA_NOPE + MLA_V))


def _pad_lanes(v, n=LANES):
    return jnp.pad(v, (0, n - v.shape[0])).reshape(1, n)


def _super_blocks(b):
    _, r, c = b.shape
    per = S5_GROUPS // S5_SUPER
    b = b.reshape(S5_SUPER, per, r, c)
    eye = jnp.eye(per, dtype=b.dtype)
    return (b[:, :, :, None, :] * eye[None, :, None, :, None]).reshape(S5_SUPER, per * r, per * c)


def _layer_params(l, w, small):
    p = {k: small[k][l] for k in small}
    q = {}
    q["norm_g"] = p["norm_g"].reshape(1, D_MODEL)
    q["mla_q_a_norm"] = p["mla_q_a_norm"].reshape(1, 256)
    q["mla_kv_a_norm"] = p["mla_kv_a_norm"].reshape(1, 128)
    q["mla_q_norm"] = _pad_lanes(p["mla_q_norm"])
    q["mla_k_norm"] = _pad_lanes(p["mla_k_norm"])
    q["fox_b_f"] = _pad_lanes(p["fox_b_f"])
    q["fox_q_norm"] = p["fox_q_norm"].reshape(1, 64)
    q["fox_k_norm"] = p["fox_k_norm"].reshape(1, 64)
    q["s5_d"] = p["s5_d"].reshape(1, 512)
    q["s5_b_glu"] = p["s5_b_glu"].reshape(1, 512)
    rep = lambda z: jnp.repeat(z, S5_GROUP, axis=0)
    q["lr16"], q["li16"] = rep(p["s5_lambda_re"]), rep(p["s5_lambda_im"])
    q["ldt16"] = rep(jnp.broadcast_to(p["s5_log_dt"][:, None], (S5_GROUPS, S5_STATE)))
    q["br2"] = p["s5_b_re"].transpose(0, 2, 1).reshape(512, 64)
    q["bi2"] = p["s5_b_im"].transpose(0, 2, 1).reshape(512, 64)
    q["c_re"], q["c_im"] = p["s5_c_re"], p["s5_c_im"]
    q["w_in"] = _win_pad(w["w_in"], f"l{l}_w_in_pad")
    q["wq"] = _pad_wq(w["mla_w_q_up"])
    q["wkv"] = _pad_wkv(w["mla_w_kv_up"])
    q.update(_late_weights(w))
    return q


def _late_weights(w):
    return {q: w[k] for q, k in (("wglu", "s5_w_glu"), ("wo", "w_branch_out"), ("wout", "w_out")) if k in w}


def _layer_fwd(l, x, tabs, q, rides=None):
    rides = rides or {}
    n = lambda s: f"l{l}_{s}"
    sv = {"x": x}
    h = _norm_fwd(x, q["norm_g"], n("norm_fwd"))
    sv["h"] = h
    seg = [_mm(h, q["w_in"], "nn", n(f"proj{k}"), b_cols=(SEG_OFF[k], SEG_W[k])) for k in range(5)]
    sv["seg"] = seg
    mq, mk, mv = _mla_prep_fwd(seg[0], tabs, q, q["wq"], q["wkv"], n("mla_prep_fwd"))
    o_mla, lse_mla = _attn_fwd(mq, mk, mv, 1.0 / math.sqrt(MLA_QK), True, n("mla_attn_fwd"), ride=rides.get("mla_attn_fwd"))
    sv.update(mq=mq, mk=mk, mv=mv, o_mla=o_mla, lse_mla=lse_mla)
    fq, fk, fv, c, ct = _fox_prep_fwd(seg[0], seg[1], q["fox_b_f"], q["fox_q_norm"], q["fox_k_norm"], n("fox_prep_fwd"))
    o_fox, lse_fox = _attn_fwd(fq, fk, fv, 1.0 / math.sqrt(FOX_DIM), False, n("fox_attn_fwd"), c=c, ct=ct,
                               ride=rides.get("fox_attn_fwd"))
    sv.update(fq=fq, fk=fk, fv=fv, c=c, ct=ct, o_fox=o_fox, lse_fox=lse_fox)
    a_re16, a_im16, bb_re, bb_im = _s5_params_fwd(q["lr16"], q["li16"], q["ldt16"], q["br2"], q["bi2"], n("s5_params_fwd"))
    a_re = a_re16.reshape(S5_GROUPS, S5_GROUP, S5_STATE)[:, 0, :].reshape(1, S5_LANES)
    a_im = a_im16.reshape(S5_GROUPS, S5_GROUP, S5_STATE)[:, 0, :].reshape(1, S5_LANES)
    wb_re = _super_blocks(bb_re.reshape(S5_GROUPS, S5_GROUP, S5_STATE)).astype(BF16)
    wb_im = _super_blocks(bb_im.reshape(S5_GROUPS, S5_GROUP, S5_STATE)).astype(BF16)
    wc_re = _super_blocks(q["c_re"].transpose(0, 2, 1)).astype(BF16)
    wc_im = _super_blocks(-q["c_im"].transpose(0, 2, 1)).astype(BF16)
    x_re, x_im, y0 = _s5_core_fwd(seg[2], wb_re, wb_im, wc_re, wc_im, a_re, a_im, n("s5_scan_fwd"), ride=rides.get("s5_scan_fwd"))
    y_s5 = _s5_post_fwd(y0, seg[2], q["s5_d"], q["wglu"], q["s5_b_glu"], n("s5_post_fwd"))
    sv.update(a_re=a_re, a_im=a_im, wb_re=wb_re, wb_im=wb_im, wc_re=wc_re, wc_im=wc_im, x_re=x_re, x_im=x_im, y0=y0, y_s5=y_s5)
    out = _gate_fwd(o_mla, o_fox, y_s5, seg[3], seg[4], x, q["wo"], q["wout"], n("gate_fwd"))
    return out, sv


def _layer_bwd(l, dout, tabs, q, sv, rides=None, g=None):
    rides = rides or {}
    n = lambda s: f"l{l}_{s}"
    seg = sv["seg"]
    g = {} if g is None else g
    (do_mla, do_fox, dy_s5, dseg3, dseg4, dl_mla, g["wo"], g["wout"]) = _gate_bwd(
        sv["o_mla"], sv["o_fox"], sv["y_s5"], seg[3], seg[4], q["wo"], q["wout"], dout, n("gate_bwd"))
    dy0, du_a, g["wglu"], g["s5_b_glu"], g["s5_d"] = _s5_post_bwd(sv["y0"], seg[2], q["s5_d"], q["wglu"], q["s5_b_glu"], dy_s5,
                                                                 n("s5_post_bwd"))
    dseg2, da_re, da_im, dbb_re, dbb_im, dc_re, dc_im = _s5_core_bwd(
        dy0, seg[2], du_a, sv["x_re"], sv["x_im"], sv["wb_re"], sv["wb_im"], sv["wc_re"], sv["wc_im"], sv["a_re"], -sv["a_im"],
        n("s5_scan_bwd"))
    g["s5_c_re"] = dc_re.reshape(S5_GROUPS, S5_GROUP, S5_STATE)
    g["s5_c_im"] = -dc_im.reshape(S5_GROUPS, S5_GROUP, S5_STATE)
    first = (jnp.arange(512) % S5_GROUP == 0).astype(F32)[:, None]
    da_re16 = jnp.repeat(da_re.reshape(S5_GROUPS, S5_STATE), S5_GROUP, axis=0) * first
    da_im16 = jnp.repeat(da_im.reshape(S5_GROUPS, S5_STATE), S5_GROUP, axis=0) * first
    dlr, dli, dldt, dbr2, dbi2 = _s5_params_bwd(q["lr16"], q["li16"], q["ldt16"], q["br2"], q["bi2"], da_re16, da_im16, dbb_re,
                                               dbb_im, n("s5_params_bwd"))
    g["s5_lambda_re"], g["s5_lambda_im"], g["s5_log_dt"] = dlr, dli, dldt.reshape(S5_GROUPS)
    g["s5_b_re"] = dbr2.reshape(S5_GROUPS, S5_GROUP, S5_STATE).transpose(0, 2, 1)
    g["s5_b_im"] = dbi2.reshape(S5_GROUPS, S5_GROUP, S5_STATE).transpose(0, 2, 1)
    dl_fox = _attn_delta(sv["fq"], sv["fk"], sv["fv"], do_fox, sv["lse_fox"], 1.0 / math.sqrt(FOX_DIM), False,
                         n("fox_attn_delta"), c=sv["c"], ct=sv["ct"], ride=rides.get("fox_attn_delta"))
    dfq, dfk, dfv, dck = _attn_bwd(sv["fq"], sv["fk"], sv["fv"], do_fox, sv["lse_fox"], dl_fox, 1.0 / math.sqrt(FOX_DIM), False,
                                   n("fox_attn_bwd"), c=sv["c"], ct=sv["ct"], ride=rides.get("fox_attn_bwd"))
    dseg1, dff, g["fox_q_norm"], g["fox_k_norm"], dbf = _fox_prep_bwd(seg[0], seg[1], q["fox_b_f"], q["fox_q_norm"], q["fox_k_norm"],
                                                                      dfq, dfk, dfv, dck, n("fox_prep_bwd"))
    g["fox_b_f"] = dbf[0, :HEADS]
    dmq, dmk, dmv = _attn_bwd(sv["mq"], sv["mk"], sv["mv"], do_mla, sv["lse_mla"], dl_mla, 1.0 / math.sqrt(MLA_QK), True,
                              n("mla_attn_bwd"), ride=rides.get("mla_attn_bwd"))
    dseg0, dqan, dkvan, dqn, dkn, g["wq"], g["wkv"] = _mla_prep_bwd(seg[0], tabs, q, q["wq"], q["wkv"], dmq, dmk, dmv, dff,
                                                                   n("mla_prep_bwd"))
    g["mla_q_a_norm"], g["mla_kv_a_norm"] = dqan, dkvan
    g["mla_q_norm"], g["mla_k_norm"] = dqn[0, :MLA_QK], dkn[0, :MLA_QK]
    dsegs = [dseg0, dseg1, dseg2, dseg3, dseg4]
    g["w_in"] = [_mm(sv["h"], dsegs[k], "tn", n(f"dwin{k}")) for k in range(5)]
    dh = None
    for k in range(5):
        dh = _mm(dsegs[k], q["w_in"], "nt", n(f"dh{k}"), acc=dh, b_cols=(SEG_OFF[k], SEG_W[k]), ride=rides.get(f"dh{k}"))
    dx, g["norm_g"] = _norm_bwd(sv["x"], q["norm_g"], dh, dout, n("norm_bwd"))
    return dx, g


MESH = pl.DeviceIdType.MESH
ANY = pl.BlockSpec(memory_space=pl.ANY)


def _all_gather(blocks, name):
    n = len(blocks)

    def body(*refs):
        x_refs, out_refs = refs[:n], refs[n:2 * n]
        send_sems, recv_sems, local_sems = refs[2 * n:]
        x, y, c = lax.axis_index("x"), lax.axis_index("y"), lax.axis_index("c")
        me, sibling = (x, y, c), (x, y, 1 - c)
        chips = [(1 - x, y), (x, 1 - y), (1 - x, 1 - y)]

        def slot(a, px, py, pc):
            return out_refs[a].at[4 * px + 2 * py + pc]

        def copy(a, k, blk, to, src=None):
            return pltpu.make_async_remote_copy(src_ref=slot(a, *blk) if src is None else src, dst_ref=slot(a, *blk),
                                                send_sem=send_sems.at[7 * a + k], recv_sem=recv_sems.at[7 * a + k],
                                                device_id=to, device_id_type=MESH)

        mine = [pltpu.make_async_copy(x_refs[a], slot(a, *me), local_sems.at[a]) for a in range(n)]
        for cp in mine:
            cp.start()
        first = []
        for j, chip in enumerate(chips):
            first += [copy(a, 1 + j, me, (*chip, c), src=x_refs[a]) for a in range(n)]
        first += [copy(a, 0, me, sibling, src=x_refs[a]) for a in range(n)]
        for cp in first:
            cp.start()
        passed = []
        for j, chip in enumerate(chips):
            for a in range(n):
                copy(a, 1 + j, (*chip, c), me).wait_recv()
                passed.append(copy(a, 4 + j, (*chip, c), sibling))
                passed[-1].start()
        for a in range(n):
            copy(a, 0, sibling, me).wait_recv()
        for j, chip in enumerate(chips):
            for a in range(n):
                copy(a, 4 + j, (*chip, 1 - c), me).wait_recv()
        for cp in first + passed:
            cp.wait_send()
        for cp in mine:
            cp.wait()

    return pl.pallas_call(
        body, name=name, out_shape=[jax.ShapeDtypeStruct((N_DEV,) + b.shape, b.dtype) for b in blocks],
        in_specs=[ANY] * n, out_specs=[ANY] * n,
        scratch_shapes=[pltpu.SemaphoreType.DMA((7 * n,)), pltpu.SemaphoreType.DMA((7 * n,)), pltpu.SemaphoreType.DMA((n,))],
    )(*blocks)


def _place():
    x, y, c = lax.axis_index("x"), lax.axis_index("y"), lax.axis_index("c")
    return x, y, c, [(1 - x, y), (x, 1 - y), (1 - x, 1 - y)]


def _remote(src, dst, send, recv, k, to):
    return pltpu.make_async_remote_copy(src_ref=src, dst_ref=dst, send_sem=send.at[k], recv_sem=recv.at[k], device_id=to,
                                        device_id_type=MESH)


def _plan_gather_ici(blocks):
    n = len(blocks)

    def copies(in_refs, out_refs, send, recv, local):
        x, y, c, chips = _place()
        mine = 4 * x + 2 * y + c
        loc = [pltpu.make_async_copy(in_refs[a], out_refs[a].at[mine], local.at[a]) for a in range(n)]
        rem = [_remote(in_refs[a], out_refs[a].at[mine], send, recv, 3 * a + j, (px, py, c))
               for j, (px, py) in enumerate(chips) for a in range(n)]
        return rem, loc

    return _Plan(blocks, [jax.ShapeDtypeStruct((N_DEV,) + b.shape, b.dtype) for b in blocks], 3 * n, n, copies)


def _plan_gather_d2d(gathered):
    n = len(gathered)

    def copies(in_refs, out_refs, send, recv, local):
        x, y, c, _ = _place()
        rem = [_remote(in_refs[a].at[2 * j + c], out_refs[a].at[2 * j + c], send, recv, 4 * a + j, (x, y, 1 - c))
               for a in range(n) for j in range(4)]
        return rem, []

    return _Plan(gathered, [jax.ShapeDtypeStruct(g.shape, g.dtype) for g in gathered], 4 * n, 0, copies,
                 aliases={a: a for a in range(n)})


def _plan_reduce_sibling(parts):
    n = len(parts)

    def copies(in_refs, out_refs, send, recv, local):
        x, y, c, _ = _place()
        rem = [_remote(in_refs[a].at[2 * j + (1 - c)], out_refs[a].at[j], send, recv, 4 * a + j, (x, y, 1 - c))
               for a in range(n) for j in range(4)]
        return rem, []

    return _Plan(parts, [jax.ShapeDtypeStruct((4,) + p.shape[1:], p.dtype) for p in parts], 4 * n, 0, copies)


def _plan_reduce_chips(sums):
    n = len(sums)

    def copies(in_refs, out_refs, send, recv, local):
        x, y, c, chips = _place()
        mine = 2 * x + y
        loc = [pltpu.make_async_copy(in_refs[a].at[mine], out_refs[a].at[mine], local.at[a]) for a in range(n)]
        rem = [_remote(in_refs[a].at[2 * px + py], out_refs[a].at[mine], send, recv, 3 * a + k, (px, py, c))
               for k, (px, py) in enumerate(chips) for a in range(n)]
        return rem, loc

    return _Plan(sums, [jax.ShapeDtypeStruct(p.shape, p.dtype) for p in sums], 3 * n, n, copies)


def _add_sibling(parts, got, name):
    _, r, cc = parts.shape
    tr = _pick(r, (512, 256, 128, 64, 32, 16))
    c = lax.axis_index("c")

    def body(c_ref, p_ref, g_ref, o_ref):
        o_ref[...] = (p_ref[...] + g_ref[...]).astype(BF16)

    return pl.pallas_call(
        body, name=name, out_shape=jax.ShapeDtypeStruct((4, r, cc), BF16),
        grid_spec=pltpu.PrefetchScalarGridSpec(
            num_scalar_prefetch=1, grid=(4, r // tr),
            in_specs=[pl.BlockSpec((1, tr, cc), lambda j, i, cr: (2 * j + cr[0], i, 0)),
                      pl.BlockSpec((1, tr, cc), lambda j, i, cr: (j, i, 0))],
            out_specs=pl.BlockSpec((1, tr, cc), lambda j, i, cr: (j, i, 0))),
        compiler_params=_vmem(48),
    )(c.reshape(1).astype(jnp.int32), parts, got)


def _sum_leading(parts, name):
    k, r, cc = parts.shape
    tr = _pick(r, (512, 256, 128, 64, 32, 16, 8))

    def body(p_ref, o_ref):
        acc = p_ref[0]
        for j in range(1, k):
            acc = acc + p_ref[j]
        o_ref[...] = acc

    return pl.pallas_call(
        body, name=name, out_shape=jax.ShapeDtypeStruct((r, cc), F32), grid=(r // tr,),
        in_specs=[pl.BlockSpec((k, tr, cc), lambda i: (0, i, 0))], out_specs=pl.BlockSpec((tr, cc), lambda i: (i, 0)),
    )(parts)


def _adamw_math(w, g, m, v):
    nm = ADAM_B1 * m + (1.0 - ADAM_B1) * g
    nv = ADAM_B2 * v + (1.0 - ADAM_B2) * jnp.square(g)
    m_hat = nm / (1.0 - ADAM_B1 ** ADAM_STEP)
    v_hat = nv / (1.0 - ADAM_B2 ** ADAM_STEP)
    return -ADAM_LR * (m_hat / (jnp.sqrt(v_hat) + ADAM_EPS) + ADAM_WD * w), nm, nv


def _adamw_sum(w, contribs, m, v, name, ride=None):
    nl = len(contribs)
    k, r, cc = contribs[0].shape
    tr = _pick(r, (256, 128, 64, 32, 16))
    nb = r // tr

    def body(w_ref, *rest):
        c_refs = rest[:nl]
        m_ref, v_ref, g_ref, d_ref, nm_ref, nv_ref = rest[nl:]
        for li in range(nl):
            @pl.when(pl.program_id(0) == li)
            def _(c_ref=c_refs[li]):
                g = c_ref[0].astype(F32)
                for j in range(1, k):
                    g = g + c_ref[j].astype(F32)
                g_ref[...] = g
                d_ref[...], nm_ref[...], nv_ref[...] = _adamw_math(w_ref[...], g, m_ref[...], v_ref[...])

    spec = pl.BlockSpec((tr, cc), lambda l, i: (l * nb + i, 0))
    cspec = pl.BlockSpec((k, tr, cc), lambda l, i: (0, i, 0))
    return _call_with_ride(body, name, (nl, nb), [w, *contribs, m, v], [spec] + [cspec] * nl + [spec, spec],
                           [jax.ShapeDtypeStruct(w.shape, F32)] * 4, [spec] * 4, ("parallel", "parallel"), 48, ride)


def _adamw_many(ws, gs, ms, vs, name):
    n = len(ws)

    def body(*refs):
        w_r, g_r, m_r, v_r = refs[:n], refs[n:2 * n], refs[2 * n:3 * n], refs[3 * n:4 * n]
        d_r, nm_r, nv_r = refs[4 * n:5 * n], refs[5 * n:6 * n], refs[6 * n:7 * n]
        for a in range(n):
            d_r[a][...], nm_r[a][...], nv_r[a][...] = _adamw_math(w_r[a][...], g_r[a][...], m_r[a][...], v_r[a][...])

    shapes = [jax.ShapeDtypeStruct(w.shape, F32) for w in ws]
    res = pl.pallas_call(body, name=name, out_shape=shapes * 3,
                         compiler_params=pltpu.CompilerParams(vmem_limit_bytes=56 * 1024 * 1024))(*ws, *gs, *ms, *vs)
    return res[:n], res[n:2 * n], res[2 * n:]


def _pack_rows(flat, lanes, row_mult):
    n = flat.shape[-1]
    rows = -(-n // lanes)
    rows = -(-rows // row_mult) * row_mult
    pad = rows * lanes - n
    if pad:
        flat = jnp.pad(flat, [(0, 0)] * (flat.ndim - 1) + [(0, pad)])
    return flat.reshape(flat.shape[:-1] + (rows, lanes))


def _rope_tables(positions):
    inv = 1.0 / (ROPE_THETA ** (jnp.arange(0, MLA_ROPE, 2, dtype=F32) / MLA_ROPE))
    ang = positions.astype(F32)[:, None] * inv
    cos, sin = jnp.cos(ang), jnp.sin(ang)
    s = positions.shape[0]
    z = lambda n: jnp.zeros((s, n), F32)
    c = jnp.concatenate([jnp.ones((s, 64), F32), cos, cos, z(32)], axis=1)
    sa = jnp.concatenate([z(64), -sin, z(48)], axis=1)
    sb = jnp.concatenate([z(80), sin, z(32)], axis=1)
    return c, sa, sb


def _full_weights(gathered, names=SHARDED):
    full = {}
    for k, g in zip(names, gathered):
        _, r, c = g.shape
        if k == "w_in":
            full[k] = g
        else:
            full[k] = g.transpose(1, 0, 2).reshape(r, N_DEV * c) if k in COL_SHARDED else g.reshape(N_DEV * r, c)
    return full


EARLY = ("s5_w_glu", "w_branch_out", "w_out")
LATE = ("w_in", "mla_w_q_up", "mla_w_kv_up")
LATE_A, LATE_B = ("w_in_rows0", "mla_w_q_up", "mla_w_kv_up"), ("w_in_rows1",)


def _owner_major(g, names, tag):
    parts = []
    for k in names:
        if k == "w_in":
            parts.append(_win_unpad(g["w_in"], f"{tag}_w_in_unpad"))
            continue
        big = {"mla_w_q_up": lambda: _unpad_wq(g["wq"]), "mla_w_kv_up": lambda: _unpad_wkv(g["wkv"]), "s5_w_glu": lambda: g["wglu"],
               "w_branch_out": lambda: g["wo"], "w_out": lambda: g["wout"]}[k]()
        r, c = big.shape
        if k in COL_SHARDED:
            parts.append(big.reshape(r, N_DEV, c // N_DEV).transpose(1, 0, 2))
        else:
            parts.append(big.reshape(N_DEV, r // N_DEV, c))
    return parts


def _device_step(x, positions, target, shards, small):
    tabs = _rope_tables(positions)
    box = {}
    q0 = _layer_params(0, _full_weights(_all_gather(shards[0][:3], "gather_weights_l0"), LATE), small)

    def arrived(o):
        box.update(w1_b=o[:5])
        q0.update(_late_weights(_full_weights(o[5:], EARLY)))

    rides = {
        "mla_attn_fwd": _Ride(lambda: _plan_gather_ici(shards[1][:1]), lambda o: box.update(ici_a=o)),
        "fox_attn_fwd": _Ride(lambda: _join([_plan_gather_ici(shards[1][1:]), _plan_gather_d2d(box["ici_a"]),
                                             _plan_gather_ici(shards[0][3:])]),
                              lambda o: box.update(ici_b=o[:5], w1_a=o[5:6], ici_0=o[6:])),
        "s5_scan_fwd": _Ride(lambda: _join([_plan_gather_d2d(box["ici_b"]), _plan_gather_d2d(box["ici_0"])]), arrived),
    }
    h, sv0 = _layer_fwd(0, x, tabs, q0, rides)
    q1 = _layer_params(1, _full_weights(list(box["w1_a"]) + list(box["w1_b"])), small)
    h, sv1 = _layer_fwd(1, h, tabs, q1)
    loss, d = _loss_head(h, target, "loss_head")
    d, g1 = _layer_bwd(1, d, tabs, q1, sv1)
    parts1 = _owner_major(g1, SHARDED, "l1")

    def chips_plan(names, parts, got, tag):
        return _plan_reduce_chips([_add_sibling(p, g, f"reduce_add_{tag}_{k}") for k, p, g in zip(names, parts, got)])

    g0 = {}
    rides = {
        "fox_attn_delta": _Ride(lambda: _plan_reduce_sibling(parts1), lambda o: box.update(got1=o)),
        "fox_attn_bwd": _Ride(lambda: _join([chips_plan(SHARDED, parts1, box["got1"], "l1"),
                                             _plan_reduce_sibling(box.setdefault("early0", _owner_major(g0, EARLY, "l0")))]),
                              lambda o: box.update(contribs1=o[:6], got0e=o[6:])),
        "mla_attn_bwd": _Ride(lambda: chips_plan(EARLY, box["early0"], box["got0e"], "l0"), lambda o: box.update(contribs0e=o)),
        "dh1": _Ride(lambda: _plan_reduce_sibling(box.setdefault("late0", late_parts())), lambda o: box.update(got0l=o)),
        "dh3": _Ride(lambda: chips_plan(LATE_A, box["late0"][:3], box["got0l"][:3], "l0"), lambda o: box.update(contribs0a=o)),
        "dh4": _Ride(lambda: chips_plan(LATE_B, box["late0"][3:], box["got0l"][3:], "l0"), lambda o: box.update(contribs0b=o)),
    }

    def late_parts():
        slabs = [_win_unpad(g0["w_in"], f"l0_w_in_unpad{i}", i, 2) for i in range(2)]
        return [slabs[0]] + _owner_major(g0, LATE[1:], "l0") + [slabs[1]]

    d, _ = _layer_bwd(0, d, tabs, q0, sv0, rides, g0)
    w_in0 = jnp.concatenate([box["contribs0a"][0], box["contribs0b"][0]], axis=1)
    contribs0 = [w_in0] + list(box["contribs0a"][1:]) + list(box["contribs0e"])
    return loss[0, 0], d, [g0, g1], [contribs0, box["contribs1"]]


def kernel(x, positions, norm_g, w_in, mla_q_a_norm, mla_w_q_up, mla_kv_a_norm, mla_w_kv_up, mla_q_norm, mla_k_norm, fox_b_f, fox_q_norm, fox_k_norm, s5_lambda_re, s5_lambda_im, s5_log_dt, s5_b_re, s5_b_im, s5_c_re, s5_c_im, s5_d, s5_w_glu, s5_b_glu, w_branch_out, w_out, loss_target, m_norm_g, m_w_in, m_mla_q_a_norm, m_mla_w_q_up, m_mla_kv_a_norm, m_mla_w_kv_up, m_mla_q_norm, m_mla_k_norm, m_fox_b_f, m_fox_q_norm, m_fox_k_norm, m_s5_lambda_re, m_s5_lambda_im, m_s5_log_dt, m_s5_b_re, m_s5_b_im, m_s5_c_re, m_s5_c_im, m_s5_d, m_s5_w_glu, m_s5_b_glu, m_w_branch_out, m_w_out, v_norm_g, v_w_in, v_mla_q_a_norm, v_mla_w_q_up, v_mla_kv_a_norm, v_mla_w_kv_up, v_mla_q_norm, v_mla_k_norm, v_fox_b_f, v_fox_q_norm, v_fox_k_norm, v_s5_lambda_re, v_s5_lambda_im, v_s5_log_dt, v_s5_b_re, v_s5_b_im, v_s5_c_re, v_s5_c_im, v_s5_d, v_s5_w_glu, v_s5_b_glu, v_w_branch_out, v_w_out):
    env = dict(locals())
    wts = {k: env[k] for k in WEIGHTS}
    mom = {k: env["m_" + k] for k in WEIGHTS}
    var = {k: env["v_" + k] for k in WEIGHTS}

    shards = [[wts[k][l].astype(BF16) for k in SHARDED] for l in range(DEPTH)]
    small = {k: wts[k] for k in SMALL}
    loss, dx, grads, contribs = _device_step(x[0], positions[0], loss_target[0], shards, small)
    loss = lax.psum(loss, ("x", "y", "c"))

    two_d = {k: (wts[k].shape[0] * wts[k].shape[1], wts[k].shape[2]) for k in SHARDED}
    sm = {k: jnp.stack([g[k] for g in grads]).reshape(wts[k].shape) for k in SMALL}
    small_block = _pack_rows(jnp.concatenate([sm[k].reshape(-1) for k in SMALL]), LANES, 512)
    half = small_block.shape[0] // 2
    halves = [small_block[:half], small_block[half:]]
    box = {}
    rides = {"w_in": _Ride(lambda: _plan_gather_ici(halves[:1]), lambda o: box.update(ici_a=o)),
             "mla_w_q_up": _Ride(lambda: _plan_gather_d2d(box["ici_a"]), lambda o: box.update(all_a=o)),
             "w_branch_out": _Ride(lambda: _plan_gather_ici(halves[1:]), lambda o: box.update(ici_b=o)),
             "w_out": _Ride(lambda: _plan_gather_d2d(box["ici_b"]), lambda o: box.update(all_b=o))}
    grad_out, delta_out, m_out, v_out = {}, {}, {}, {}
    for i, k in enumerate(SHARDED):
        shp = wts[k].shape
        res = _adamw_sum(wts[k].reshape(two_d[k]), [contribs[l][i] for l in range(DEPTH)], mom[k].reshape(two_d[k]),
                         var[k].reshape(two_d[k]), f"adamw_{k}", ride=rides.get(k))
        grad_out[k], delta_out[k], m_out[k], v_out[k] = (z.reshape(shp) for z in res)
    g_small = jnp.concatenate([_sum_leading(box["all_a"][0], "sum_small_grads_a"),
                               _sum_leading(box["all_b"][0], "sum_small_grads_b")]).reshape(-1)
    off = 0
    for k in SMALL:
        cnt = int(np.prod(wts[k].shape))
        grad_out[k] = g_small[off:off + cnt].reshape(wts[k].shape)
        off += cnt
    flat2 = lambda a: a.reshape(-1, a.shape[-1])
    d_s, m_s, v_s = _adamw_many([flat2(wts[k]) for k in SMALL], [flat2(grad_out[k]) for k in SMALL],
                                [flat2(mom[k]) for k in SMALL], [flat2(var[k]) for k in SMALL], "adamw_small")
    for i, k in enumerate(SMALL):
        delta_out[k], m_out[k], v_out[k] = (z[i].reshape(wts[k].shape) for z in (d_s, m_s, v_s))

    return (loss, dx[None], *[grad_out[k] for k in WEIGHTS], *[delta_out[k] for k in WEIGHTS],
            *[m_out[k] for k in WEIGHTS], *[v_out[k] for k in WEIGHTS])
```

```python
import functools
import math

import jax
import jax.numpy as jnp
import numpy as np
from jax import lax
from jax.experimental import pallas as pl
from jax.experimental.pallas import tpu as pltpu

F32 = jnp.float32
BF16 = jnp.bfloat16

D_MODEL = 1024
DEPTH = 2
CHUNK = 64
EPS = 1e-6
HEADS = 8
MLA_NOPE, MLA_ROPE, MLA_V = 64, 32, 64
MLA_Q_RANK, MLA_KV_RANK = 256, 128
MLA_QK = MLA_NOPE + MLA_ROPE
ROPE_THETA = 10000.0
FOX_DIM = 64
S5_WIDTH, S5_GROUP, S5_GROUPS, S5_STATE = 512, 16, 32, 64
S5_LANES = S5_GROUPS * S5_STATE
IN_WIDTH = 7080
N_DEV = 8
LANES = 128
SUBLANES = 8

ADAM_LR, ADAM_B1, ADAM_B2, ADAM_EPS, ADAM_WD, ADAM_STEP = 0.001, 0.9, 0.999, 1e-08, 0.01, 10

SEG_W = (512, 1536, 512, 4608)
SEG_OFF = (0, 512, 2048, 2560)
NSEG = len(SEG_W)
GATES_W = 1536
PAD_IN = 7168
NEG = -1e30

SHARDED = ("w_in", "mla_w_q_up", "mla_w_kv_up", "s5_w_glu", "w_branch_out", "w_out")
COL_SHARDED = ("w_in", "mla_w_q_up", "mla_w_kv_up")
SMALL = ("norm_g", "mla_q_a_norm", "mla_kv_a_norm", "mla_q_norm", "mla_k_norm", "fox_b_f", "fox_q_norm", "fox_k_norm",
         "s5_lambda_re", "s5_lambda_im", "s5_log_dt", "s5_b_re", "s5_b_im", "s5_c_re", "s5_c_im", "s5_d", "s5_b_glu")
WEIGHTS = ("norm_g", "w_in", "mla_q_a_norm", "mla_w_q_up", "mla_kv_a_norm", "mla_w_kv_up", "mla_q_norm", "mla_k_norm",
           "fox_b_f", "fox_q_norm", "fox_k_norm", "s5_lambda_re", "s5_lambda_im", "s5_log_dt", "s5_b_re", "s5_b_im",
           "s5_c_re", "s5_c_im", "s5_d", "s5_w_glu", "s5_b_glu", "w_branch_out", "w_out")


def _pick(n, cands):
    for c in cands:
        if n % c == 0:
            return c
    return n


def _vmem(mb):
    return pltpu.CompilerParams(vmem_limit_bytes=mb * 1024 * 1024)


def _dot(a, b, dims):
    return lax.dot_general(a.astype(BF16), b.astype(BF16), (dims, ((), ())), preferred_element_type=F32)


def _nn(a, b):
    return _dot(a, b, ((1,), (0,)))


def _nt(a, b):
    return _dot(a, b, ((1,), (1,)))


def _tn(a, b):
    return _dot(a, b, ((0,), (0,)))


def _rms(x, g, n=None):
    n = x.shape[-1] if n is None else n
    return x * lax.rsqrt(jnp.sum(x * x, axis=-1, keepdims=True) / n + EPS) * g


def _rope(t, c, sa, sb):
    return t * c + pltpu.roll(t, LANES - 16, 1) * sa + pltpu.roll(t, 16, 1) * sb


def _rope_t(d, c, sa, sb):
    return d * c + pltpu.roll(d * sa, 16, 1) + pltpu.roll(d * sb, LANES - 16, 1)


def _mm(a, b, mode, name, acc=None, b_cols=None, ride=None):
    if mode == "tn":
        kd, m = a.shape
    else:
        m, kd = a.shape
    b_off, b_w = b_cols if b_cols is not None else (0, b.shape[1])
    n = b.shape[0] if mode == "nt" else b_w
    tm, tn, tk = _pick(m, (1024, 512, 256, 128)), _pick(n, (1024, 512, 256, 128)), _pick(kd, (1024, 512, 256, 128))
    nk = kd // tk
    if mode == "tn":
        a_spec = pl.BlockSpec((tk, tm), lambda i, j, k: (k, i))
    else:
        a_spec = pl.BlockSpec((tm, tk), lambda i, j, k: (i, k))
    if mode == "nt":
        assert b_off % tk == 0
        b_spec = pl.BlockSpec((tn, tk), lambda i, j, k: (j, k + b_off // tk))
    else:
        assert b_off % tn == 0
        b_spec = pl.BlockSpec((tk, tn), lambda i, j, k: (k, j + b_off // tn))
    dims = {"nn": ((1,), (0,)), "nt": ((1,), (1,)), "tn": ((0,), (0,))}[mode]
    o_spec = pl.BlockSpec((tm, tn), lambda i, j, k: (i, j))
    has_acc = acc is not None

    def body(*refs):
        if has_acc:
            a_ref, b_ref, c_ref, o_ref = refs
        else:
            a_ref, b_ref, o_ref = refs
        k = pl.program_id(2)
        prod = _dot(a_ref[...], b_ref[...], dims)

        @pl.when(k == 0)
        def _():
            o_ref[...] = prod + c_ref[...] if has_acc else prod

        @pl.when(k > 0)
        def _():
            o_ref[...] += prod

    ins = [a, b] + ([acc] if has_acc else [])
    in_specs = [a_spec, b_spec] + ([o_spec] if has_acc else [])
    return _call_with_ride(body, name, (m // tm, n // tn, nk), ins, in_specs, [jax.ShapeDtypeStruct((m, n), F32)], [o_spec],
                           ("parallel", "parallel", "arbitrary"), 48, ride)[0]


def _stage(name, fn, n_steps, ins, outs, accs=(), scratch=(), vmem_mb=48):
    n_in, n_out, n_acc = len(ins), len(outs), len(accs)

    def body(*refs):
        in_refs = refs[:n_in]
        out_refs = refs[n_in:n_in + n_out]
        acc_refs = refs[n_in + n_out:n_in + n_out + n_acc]
        scr = refs[n_in + n_out + n_acc:]
        if n_acc:
            @pl.when(pl.program_id(0) == 0)
            def _():
                for r in acc_refs:
                    r[...] = jnp.zeros(r.shape, r.dtype)
        fn(in_refs, out_refs, acc_refs, scr)

    acc_specs = [pl.BlockSpec(a.shape, functools.partial(lambda i, nd: (0,) * nd, nd=len(a.shape))) for a in accs]
    res = pl.pallas_call(
        body, name=name, grid=(n_steps,),
        in_specs=[s for _, s in ins], out_specs=[s for _, s in outs] + acc_specs,
        out_shape=[s for s, _ in outs] + list(accs), scratch_shapes=list(scratch),
        compiler_params=pltpu.CompilerParams(dimension_semantics=("arbitrary",),
                                             vmem_limit_bytes=vmem_mb * 1024 * 1024),
    )(*[a for a, _ in ins])
    return res


def _rows(ts, w, j=0):
    return pl.BlockSpec((ts, w), lambda i: (i, j))


def _rows_rev(ts, w, n, j=0):
    return pl.BlockSpec((ts, w), lambda i: (n - 1 - i, j))


def _heads(ts, d):
    return pl.BlockSpec((HEADS, ts, d), lambda i: (0, i, 0))


def _heads_rev(ts, d, n):
    return pl.BlockSpec((HEADS, ts, d), lambda i: (0, n - 1 - i, 0))


def _full(shape):
    nd = len(shape)
    return pl.BlockSpec(tuple(shape), lambda i: (0,) * nd)


def _sds(shape, dtype=F32):
    return jax.ShapeDtypeStruct(tuple(shape), dtype)


def _norm_fwd(x, g, name):
    s = x.shape[0]
    ts = _pick(s, (256, 128))

    def fn(ins, outs, accs, scr):
        outs[0][...] = _rms(ins[0][...], ins[1][...]).astype(BF16)

    return _stage(name, fn, s // ts, [(x, _rows(ts, D_MODEL)), (g, _full(g.shape))],
                  [(_sds((s, D_MODEL), BF16), _rows(ts, D_MODEL))])[0]


def _norm_bwd(x, g, dh, dres, name):
    s = x.shape[0]
    ts = _pick(s, (256, 128))

    def fn(ins, outs, accs, scr):
        _, vjp = jax.vjp(_rms, ins[0][...], ins[1][...])
        dx, dg = vjp(ins[2][...])
        outs[0][...] = dx + ins[3][...]
        accs[0][...] += dg

    r = _stage(name, fn, s // ts,
               [(x, _rows(ts, D_MODEL)), (g, _full(g.shape)), (dh, _rows(ts, D_MODEL)), (dres, _rows(ts, D_MODEL))],
               [(_sds((s, D_MODEL)), _rows(ts, D_MODEL))], accs=[_sds((1, D_MODEL))])
    return r[0], r[1]


def _mla_q(qraw, c, sa, sb, qn):
    return _rms(_rope(qraw, c, sa, sb), qn, MLA_QK)


def _mla_prep_fwd(seg0, tabs, p, wq, wkv, name):
    s = seg0.shape[0]
    ts = _pick(s, (256, 128))

    def fn(ins, outs, accs, scr):
        blk, cos, sa, sb, qan, kvan, qn, kn, wq_r, wkv_r = ins
        b = blk[...]
        cq, ckv, kt = b[:, :256], b[:, 256:384], b[:, 384:512]
        lane = lax.broadcasted_iota(jnp.int32, kt.shape, 1)
        kpe = jnp.where(lane >= 64, kt, 0.0)
        q_raw = _nn(_rms(cq, qan[...]), wq_r[...])
        kv_raw = _nn(_rms(ckv, kvan[...]), wkv_r[...])
        c, a, bb = cos[...], sa[...], sb[...]
        for h in range(HEADS):
            outs[0][h] = _mla_q(q_raw[:, LANES * h:LANES * (h + 1)], c, a, bb, qn[...]).astype(BF16)
            outs[1][h] = _mla_q(kv_raw[:, LANES * h:LANES * (h + 1)] + kpe, c, a, bb, kn[...]).astype(BF16)
            outs[2][h] = kv_raw[:, 1024 + 64 * h:1024 + 64 * (h + 1)].astype(BF16)

    consts = [p["mla_q_a_norm"], p["mla_kv_a_norm"], p["mla_q_norm"], p["mla_k_norm"], wq, wkv]
    return _stage(name, fn, s // ts,
                  [(seg0, _rows(ts, 512))] + [(t, _rows(ts, LANES)) for t in tabs] + [(a, _full(a.shape)) for a in consts],
                  [(_sds((HEADS, s, LANES), BF16), _heads(ts, LANES)), (_sds((HEADS, s, LANES), BF16), _heads(ts, LANES)),
                   (_sds((HEADS, s, 64), BF16), _heads(ts, 64))])


def _mla_prep_bwd(seg0, tabs, p, wq, wkv, dq, dk, dv, dff, name):
    s = seg0.shape[0]
    ts = _pick(s, (256, 128))

    def fn(ins, outs, accs, scr):
        blk, cos, sa, sb, qan, kvan, qn, kn, wq_r, wkv_r, dq_r, dk_r, dv_r, dff_r = ins
        dqan, dkvan, dqn, dkn, dwq, dwkv = accs
        dqraw_s, dkvraw_s = scr
        b = blk[...]
        cq, ckv, kt = b[:, :256], b[:, 256:384], b[:, 384:512]
        lane = lax.broadcasted_iota(jnp.int32, kt.shape, 1)
        kpe = jnp.where(lane >= 64, kt, 0.0)
        cqn, vjp_cq = jax.vjp(_rms, cq, qan[...])
        ckvn, vjp_ckv = jax.vjp(_rms, ckv, kvan[...])
        q_raw = _nn(cqn, wq_r[...])
        kv_raw = _nn(ckvn, wkv_r[...])
        c, a, bb = cos[...], sa[...], sb[...]

        def head_bwd(raw, gain, d):
            t = _rope(raw, c, a, bb)
            _, vjp = jax.vjp(functools.partial(_rms, n=MLA_QK), t, gain)
            dt, dgain = vjp(d)
            return _rope_t(dt, c, a, bb), dgain

        dkpe = jnp.zeros(kt.shape, F32)
        for h in range(HEADS):
            dqh, dg = head_bwd(q_raw[:, LANES * h:LANES * (h + 1)], qn[...], dq_r[h])
            dqn[...] += dg
            dqraw_s[:, LANES * h:LANES * (h + 1)] = dqh
            dkh, dg = head_bwd(kv_raw[:, LANES * h:LANES * (h + 1)] + kpe, kn[...], dk_r[h])
            dkn[...] += dg
            dkvraw_s[:, LANES * h:LANES * (h + 1)] = dkh
            dkpe = dkpe + dkh
            dkvraw_s[:, 1024 + 64 * h:1024 + 64 * (h + 1)] = dv_r[h]
        dq_raw = dqraw_s[...]
        dkv_raw = dkvraw_s[...]
        dwq[...] += _tn(cqn, dq_raw)
        dwkv[...] += _tn(ckvn, dkv_raw)
        dcq, dg = vjp_cq(_nt(dq_raw, wq_r[...]))
        dqan[...] += dg
        dckv, dg = vjp_ckv(_nt(dkv_raw, wkv_r[...]))
        dkvan[...] += dg
        outs[0][:, 0:256] = dcq.astype(BF16)
        outs[0][:, 256:384] = dckv.astype(BF16)
        outs[0][:, 384:512] = (jnp.where(lane >= 64, dkpe, 0.0) + dff_r[...]).astype(BF16)

    consts = [p["mla_q_a_norm"], p["mla_kv_a_norm"], p["mla_q_norm"], p["mla_k_norm"], wq, wkv]
    return _stage(name, fn, s // ts,
                  [(seg0, _rows(ts, 512))] + [(t, _rows(ts, LANES)) for t in tabs] + [(a, _full(a.shape)) for a in consts]
                  + [(dq, _heads(ts, LANES)), (dk, _heads(ts, LANES)), (dv, _heads(ts, 64)), (dff, _rows(ts, LANES))],
                  [(_sds((s, 512), BF16), _rows(ts, 512))],
                  accs=[_sds((1, 256)), _sds((1, 128)), _sds((1, LANES)), _sds((1, LANES)), _sds(wq.shape), _sds(wkv.shape)],
                  scratch=[pltpu.VMEM((ts, 1024), F32), pltpu.VMEM((ts, 1536), F32)])


def _fox_prep_fwd(seg0, seg1, bf, qn, kn, name):
    s = seg0.shape[0]
    ts = _pick(s, (256, 128))
    steps = int(math.log2(ts))

    def fn(ins, outs, accs, scr):
        kt_r, x_r, bf_r, qn_r, kn_r = ins
        carry = scr[0]

        @pl.when(pl.program_id(0) == 0)
        def _():
            carry[...] = jnp.zeros(carry.shape, F32)

        x = x_r[...]
        for h in range(HEADS):
            outs[0][h] = _rms(x[:, 64 * h:64 * (h + 1)], qn_r[...]).astype(BF16)
            outs[1][h] = _rms(x[:, 512 + 64 * h:512 + 64 * (h + 1)], kn_r[...]).astype(BF16)
            outs[2][h] = x[:, 1024 + 64 * h:1024 + 64 * (h + 1)].astype(BF16)
        kt = kt_r[...]
        lane = lax.broadcasted_iota(jnp.int32, kt.shape, 1)
        row = lax.broadcasted_iota(jnp.int32, kt.shape, 0)
        cs = jnp.where(lane < HEADS, jax.nn.log_sigmoid(kt + bf_r[...]), 0.0)
        for k in range(steps):
            sh = 1 << k
            cs = cs + jnp.where(row >= sh, pltpu.roll(cs, sh, 0), 0.0)
        cs = cs + carry[0:1, :]
        outs[3][...] = cs
        outs[4][...] = cs.T[0:HEADS, :]
        carry[0:1, :] = cs[ts - 1:ts, :]

    return _stage(name, fn, s // ts,
                  [(seg0, _rows(ts, LANES, 3)), (seg1, _rows(ts, 1536)), (bf, _full(bf.shape)), (qn, _full(qn.shape)),
                   (kn, _full(kn.shape))],
                  [(_sds((HEADS, s, 64), BF16), _heads(ts, 64)), (_sds((HEADS, s, 64), BF16), _heads(ts, 64)),
                   (_sds((HEADS, s, 64), BF16), _heads(ts, 64)), (_sds((s, LANES)), _rows(ts, LANES)),
                   (_sds((HEADS, s)), pl.BlockSpec((HEADS, ts), lambda i: (0, i)))],
                  scratch=[pltpu.VMEM((SUBLANES, LANES), F32)])


def _fox_prep_bwd(seg0, seg1, bf, qn, kn, dq, dk, dv, dck, name):
    s = seg0.shape[0]
    ts = _pick(s, (256, 128))
    n = s // ts
    steps = int(math.log2(ts))

    def fn(ins, outs, accs, scr):
        kt_r, x_r, bf_r, qn_r, kn_r, dq_r, dk_r, dv_r, dck_r = ins
        dqn, dkn, dbf = accs
        carry, dbuf = scr

        @pl.when(pl.program_id(0) == 0)
        def _():
            carry[...] = jnp.zeros(carry.shape, F32)

        x = x_r[...]
        for h in range(HEADS):
            _, vjp = jax.vjp(_rms, x[:, 64 * h:64 * (h + 1)], qn_r[...])
            d, dg = vjp(dq_r[h])
            dbuf[:, 64 * h:64 * (h + 1)] = d
            dqn[...] += dg
            _, vjp = jax.vjp(_rms, x[:, 512 + 64 * h:512 + 64 * (h + 1)], kn_r[...])
            d, dg = vjp(dk_r[h])
            dbuf[:, 512 + 64 * h:512 + 64 * (h + 1)] = d
            dkn[...] += dg
            dbuf[:, 1024 + 64 * h:1024 + 64 * (h + 1)] = dv_r[h]
        outs[0][...] = dbuf[...].astype(BF16)
        dc = dck_r[...].reshape(HEADS, ts)
        dc = jnp.concatenate([dc, jnp.zeros((LANES - HEADS, ts), F32)], axis=0).T
        row = lax.broadcasted_iota(jnp.int32, dc.shape, 0)
        lane = lax.broadcasted_iota(jnp.int32, dc.shape, 1)
        for k in range(steps):
            sh = 1 << k
            dc = dc + jnp.where(row < ts - sh, pltpu.roll(dc, ts - sh, 0), 0.0)
        dc = dc + carry[0:1, :]
        carry[0:1, :] = dc[0:1, :]
        dff = jnp.where(lane < HEADS, dc * jax.nn.sigmoid(-(kt_r[...] + bf_r[...])), 0.0)
        outs[1][...] = dff
        dbf[...] += jnp.sum(dff, axis=0, keepdims=True)

    return _stage(name, fn, n,
                  [(seg0, _rows_rev(ts, LANES, n, 3)), (seg1, _rows_rev(ts, 1536, n)), (bf, _full(bf.shape)),
                   (qn, _full(qn.shape)), (kn, _full(kn.shape)), (dq, _heads_rev(ts, 64, n)), (dk, _heads_rev(ts, 64, n)),
                   (dv, _heads_rev(ts, 64, n)), (dck, pl.BlockSpec((HEADS, 1, ts), lambda i: (0, 0, n - 1 - i)))],
                  [(_sds((s, 1536), BF16), _rows_rev(ts, 1536, n)), (_sds((s, LANES)), _rows_rev(ts, LANES, n))],
                  accs=[_sds((1, 64)), _sds((1, 64)), _sds((1, LANES))],
                  scratch=[pltpu.VMEM((SUBLANES, LANES), F32), pltpu.VMEM((ts, 1536), F32)])


def _allowed(i, j, t, chunk_causal):
    qpos = i * t + lax.broadcasted_iota(jnp.int32, (t, t), 0)
    kpos = j * t + lax.broadcasted_iota(jnp.int32, (t, t), 1)
    if chunk_causal:
        return (kpos // CHUNK) <= (qpos // CHUNK)
    return kpos <= qpos


def _pick_col(c_blk, h):
    lane = lax.broadcasted_iota(jnp.int32, c_blk.shape, 1)
    return jnp.sum(jnp.where(lane == h, c_blk, 0.0), axis=1, keepdims=True)


class _Ride:
    def __init__(self, make, take):
        self.make, self.take = make, take


class _Plan:
    def __init__(self, ins, out_shapes, n_remote, n_local, copies, aliases=None):
        self.ins, self.out_shapes, self.n_remote, self.n_local = list(ins), list(out_shapes), n_remote, n_local
        self.copies, self.aliases = copies, dict(aliases or {})

    def scratch(self):
        return [pltpu.SemaphoreType.DMA((self.n_remote,)), pltpu.SemaphoreType.DMA((self.n_remote,)),
                pltpu.SemaphoreType.DMA((max(self.n_local, 1),))]

    def start(self, in_refs, out_refs, sems):
        remote, local = self.copies(in_refs, out_refs, *sems)
        for cp in local + remote:
            cp.start()

    def wait(self, in_refs, out_refs, sems):
        remote, local = self.copies(in_refs, out_refs, *sems)
        for cp in remote:
            cp.wait()
        for cp in local:
            cp.wait()


class _Off:
    def __init__(self, ref, off):
        self.ref, self.off, self.at = ref, off, self

    def __getitem__(self, k):
        return self.ref.at[k + self.off]


def _join(plans):
    ins = [a for p in plans for a in p.ins]
    outs = [o for p in plans for o in p.out_shapes]
    aliases, i0, o0 = {}, 0, 0
    for p in plans:
        aliases.update({i0 + i: o0 + o for i, o in p.aliases.items()})
        i0, o0 = i0 + len(p.ins), o0 + len(p.out_shapes)

    def copies(in_refs, out_refs, send, recv, local):
        rem, loc, i0, o0, r0, l0 = [], [], 0, 0, 0, 0
        for p in plans:
            r, l = p.copies(in_refs[i0:i0 + len(p.ins)], out_refs[o0:o0 + len(p.out_shapes)], _Off(send, r0), _Off(recv, r0),
                            _Off(local, l0))
            rem, loc = rem + r, loc + l
            i0, o0, r0, l0 = i0 + len(p.ins), o0 + len(p.out_shapes), r0 + p.n_remote, l0 + p.n_local
        return rem, loc

    return _Plan(ins, outs, sum(p.n_remote for p in plans), sum(p.n_local for p in plans), copies, aliases)


def _call_with_ride(core, name, grid, ins, in_specs, out_shape, out_specs, semantics, vmem_mb, ride, scratch=()):
    n_in, n_out, n_scr = len(ins), len(out_shape), len(scratch)
    if ride is None:
        return pl.pallas_call(
            core, name=name, grid=grid, in_specs=in_specs, out_specs=out_specs, out_shape=out_shape,
            scratch_shapes=list(scratch),
            compiler_params=pltpu.CompilerParams(dimension_semantics=semantics, vmem_limit_bytes=vmem_mb * 1024 * 1024),
        )(*ins)
    plan = ride.make()
    ci, co = len(plan.ins), len(plan.out_shapes)

    def body(*refs):
        c_in = refs[n_in:n_in + ci]
        a_out = refs[n_in + ci:n_in + ci + n_out]
        c_out = refs[n_in + ci + n_out:n_in + ci + n_out + co]
        own = refs[n_in + ci + n_out + co:n_in + ci + n_out + co + n_scr]
        sems = refs[n_in + ci + n_out + co + n_scr:]
        ids = [pl.program_id(d) for d in range(len(grid))]
        first = functools.reduce(jnp.logical_and, [i == 0 for i in ids])
        last = functools.reduce(jnp.logical_and, [i == g - 1 for i, g in zip(ids, grid)])

        @pl.when(first)
        def _():
            plan.start(c_in, c_out, sems)

        core(*refs[:n_in], *a_out, *own)

        @pl.when(last)
        def _():
            plan.wait(c_in, c_out, sems)

    res = pl.pallas_call(
        body, name=name, grid=grid, in_specs=list(in_specs) + [ANY] * ci, out_specs=list(out_specs) + [ANY] * co,
        out_shape=list(out_shape) + plan.out_shapes, scratch_shapes=list(scratch) + plan.scratch(),
        input_output_aliases={n_in + i: n_out + o for i, o in plan.aliases.items()},
        compiler_params=pltpu.CompilerParams(dimension_semantics=("arbitrary",) * len(grid),
                                             vmem_limit_bytes=vmem_mb * 1024 * 1024),
    )(*ins, *plan.ins)
    ride.take(res[n_out:])
    return res[:n_out]


def _attn_fwd(q, k, v, scale, chunk_causal, name, c=None, ct=None, hps=8, ride=None):
    _, s, dk = q.shape
    dv = v.shape[2]
    t = _pick(s, (256, 128))
    bias = c is not None

    def body(*refs):
        if bias:
            q_ref, k_ref, v_ref, c_ref, ct_ref, o_ref, lse_ref = refs
        else:
            q_ref, k_ref, v_ref, o_ref, lse_ref = refs
        hp, i = pl.program_id(0), pl.program_id(1)
        qb = [q_ref[e] for e in range(hps)]
        cq = [_pick_col(c_ref[...], hp * hps + e) if bias else None for e in range(hps)]

        def step(j, carry, diagonal):
            off = pl.multiple_of(j * t, t)
            out = []
            for e in range(hps):
                m, l, acc = carry[e]
                sc = _nt(qb[e], k_ref[e, pl.ds(off, t), :]) * scale
                if bias:
                    sc = sc + (cq[e] - ct_ref[pl.ds(hp * hps + e, 1), pl.ds(off, t)])
                if diagonal:
                    sc = jnp.where(_allowed(i, j, t, chunk_causal), sc, NEG)
                m_new = jnp.maximum(m, jnp.max(sc, axis=1, keepdims=True))
                pr = jnp.exp(sc - m_new)
                alpha = jnp.exp(m - m_new)
                out.append((m_new, alpha * l + jnp.sum(pr, axis=1, keepdims=True),
                            alpha * acc + _nn(pr, v_ref[e, pl.ds(off, t), :])))
            return tuple(out)

        init = tuple((jnp.full((t, 1), NEG, F32), jnp.zeros((t, 1), F32), jnp.zeros((t, dv), F32)) for _ in range(hps))
        res = step(i, lax.fori_loop(0, i, functools.partial(step, diagonal=False), init), True)
        for e in range(hps):
            m, l, acc = res[e]
            o_ref[e] = acc / l
            lse_ref[e] = m + jnp.log(l)

    ins = [q, k, v] + ([c, ct] if bias else [])
    in_specs = [pl.BlockSpec((hps, t, dk), lambda h, i: (h, i, 0)), pl.BlockSpec((hps, s, dk), lambda h, i: (h, 0, 0)),
                pl.BlockSpec((hps, s, dv), lambda h, i: (h, 0, 0))]
    if bias:
        in_specs += [pl.BlockSpec((t, LANES), lambda h, i: (i, 0)), pl.BlockSpec((HEADS, s), lambda h, i: (0, 0))]
    return _call_with_ride(
        body, name, (HEADS // hps, s // t), ins, in_specs, [_sds((HEADS, s, dv)), _sds((HEADS, s, 1))],
        [pl.BlockSpec((hps, t, dv), lambda h, i: (h, i, 0)), pl.BlockSpec((hps, t, 1), lambda h, i: (h, i, 0))],
        ("parallel", "parallel"), 48, ride)


def _attn_delta(q, k, v, do, lse, scale, chunk_causal, name, c=None, ct=None, hps=4, ride=None):
    _, s, dk = q.shape
    dv = v.shape[2]
    t = _pick(s, (256, 128))
    bias = c is not None

    def body(*refs):
        if bias:
            q_ref, k_ref, v_ref, do_ref, lse_ref, c_ref, ct_ref, dl_ref = refs
        else:
            q_ref, k_ref, v_ref, do_ref, lse_ref, dl_ref = refs
        hp, i = pl.program_id(0), pl.program_id(1)
        qb, dob, lse_b = ([r[e] for e in range(hps)] for r in (q_ref, do_ref, lse_ref))
        cq = [_pick_col(c_ref[...], hp * hps + e) if bias else None for e in range(hps)]

        def step(j, acc, diagonal):
            off = pl.multiple_of(j * t, t)
            out = []
            for e in range(hps):
                sc = _nt(qb[e], k_ref[e, pl.ds(off, t), :]) * scale
                if bias:
                    sc = sc + (cq[e] - ct_ref[pl.ds(hp * hps + e, 1), pl.ds(off, t)])
                pr = jnp.exp(sc - lse_b[e])
                if diagonal:
                    pr = jnp.where(_allowed(i, j, t, chunk_causal), pr, 0.0)
                out.append(acc[e] + jnp.sum(pr * _nt(dob[e], v_ref[e, pl.ds(off, t), :]), axis=1, keepdims=True))
            return tuple(out)

        init = tuple(jnp.zeros((t, 1), F32) for _ in range(hps))
        res = step(i, lax.fori_loop(0, i, functools.partial(step, diagonal=False), init), True)
        for e in range(hps):
            dl_ref[e] = res[e]

    ins = [q, k, v, do, lse] + ([c, ct] if bias else [])
    in_specs = [pl.BlockSpec((hps, t, dk), lambda h, i: (h, i, 0)), pl.BlockSpec((hps, s, dk), lambda h, i: (h, 0, 0)),
                pl.BlockSpec((hps, s, dv), lambda h, i: (h, 0, 0)), pl.BlockSpec((hps, t, dv), lambda h, i: (h, i, 0)),
                pl.BlockSpec((hps, t, 1), lambda h, i: (h, i, 0))]
    if bias:
        in_specs += [pl.BlockSpec((t, LANES), lambda h, i: (i, 0)), pl.BlockSpec((HEADS, s), lambda h, i: (0, 0))]
    return _call_with_ride(body, name, (HEADS // hps, s // t), ins, in_specs, [_sds((HEADS, s, 1))],
                           [pl.BlockSpec((hps, t, 1), lambda h, i: (h, i, 0))], ("parallel", "parallel"), 48, ride)[0]


def _attn_bwd(q, k, v, do, lse, delta, scale, chunk_causal, name, c=None, ct=None, hps=8, ride=None):
    _, s, dk = q.shape
    dv = v.shape[2]
    t = _pick(s, (256, 128))
    n = s // t
    bias = c is not None

    def body(*refs):
        if bias:
            q_ref, k_ref, v_ref, do_ref, lse_ref, dl_ref, c_ref, ct_ref, dq_ref, dk_ref, dv_ref, dck_ref = refs
        else:
            q_ref, k_ref, v_ref, do_ref, lse_ref, dl_ref, dq_ref, dk_ref, dv_ref = refs
        hp, j = pl.program_id(0), pl.program_id(1)

        @pl.when(j == 0)
        def _():
            dq_ref[...] = jnp.zeros(dq_ref.shape, F32)

        kb, vb = [k_ref[e] for e in range(hps)], [v_ref[e] for e in range(hps)]
        joff = pl.multiple_of(j * t, t)
        ck = [ct_ref[pl.ds(hp * hps + e, 1), pl.ds(joff, t)] if bias else None for e in range(hps)]

        def step(i, carry, diagonal):
            off = pl.multiple_of(i * t, t)
            out = []
            for e in range(hps):
                dk_acc, dv_acc, dck_acc = carry[e]
                qb = q_ref[e, pl.ds(off, t), :]
                dob = do_ref[e, pl.ds(off, t), :]
                sc = _nt(qb, kb[e]) * scale
                if bias:
                    sc = sc + (_pick_col(c_ref[pl.ds(off, t), :], hp * hps + e) - ck[e])
                pr = jnp.exp(sc - lse_ref[e, pl.ds(off, t), :])
                if diagonal:
                    pr = jnp.where(_allowed(i, j, t, chunk_causal), pr, 0.0)
                dv_acc = dv_acc + _tn(pr, dob)
                ds = pr * (_nt(dob, vb[e]) - dl_ref[e, pl.ds(off, t), :])
                dq_ref[e, pl.ds(off, t), :] += _nn(ds, kb[e]) * scale
                dk_acc = dk_acc + _tn(ds, qb) * scale
                if bias:
                    dck_acc = dck_acc - jnp.sum(ds, axis=0, keepdims=True)
                out.append((dk_acc, dv_acc, dck_acc))
            return tuple(out)

        zero = tuple((jnp.zeros((t, dk), F32), jnp.zeros((t, dv), F32), jnp.zeros((1, t), F32)) for _ in range(hps))
        res = lax.fori_loop(j + 1, n, functools.partial(step, diagonal=False), step(j, zero, True))
        for e in range(hps):
            dk_ref[e], dv_ref[e] = res[e][0], res[e][1]
            if bias:
                dck_ref[e] = res[e][2]

    ins = [q, k, v, do, lse, delta] + ([c, ct] if bias else [])
    full = lambda d: pl.BlockSpec((hps, s, d), lambda h, j: (h, 0, 0))
    blk = lambda d: pl.BlockSpec((hps, t, d), lambda h, j: (h, j, 0))
    in_specs = [full(dk), blk(dk), blk(dv), full(dv), full(1), full(1)]
    if bias:
        in_specs += [pl.BlockSpec((s, LANES), lambda h, j: (0, 0)), pl.BlockSpec((HEADS, s), lambda h, j: (0, 0))]
    out_specs = [full(dk), blk(dk), blk(dv)]
    out_shape = [_sds((HEADS, s, dk)), _sds((HEADS, s, dk)), _sds((HEADS, s, dv))]
    if bias:
        out_specs.append(pl.BlockSpec((hps, 1, t), lambda h, j: (h, 0, j)))
        out_shape.append(_sds((HEADS, 1, s)))
    return _call_with_ride(body, name, (HEADS // hps, n), ins, in_specs, out_shape, out_specs, ("parallel", "arbitrary"), 56,
                           ride)


def _s5_disc(lr, li, ldt, br, bi):
    dt = jnp.exp(ldt)
    mag = jnp.exp(lr * dt)
    a_re = mag * jnp.cos(li * dt)
    a_im = mag * jnp.sin(li * dt)
    den = lr * lr + li * li
    f_re = ((a_re - 1.0) * lr + a_im * li) / den
    f_im = (a_im * lr - (a_re - 1.0) * li) / den
    return a_re, a_im, f_re * br - f_im * bi, f_re * bi + f_im * br


def _s5_params_fwd(lr16, li16, ldt16, br2, bi2, name):
    def body(a, b, c, d, e, o0, o1, o2, o3):
        r = _s5_disc(a[...], b[...], c[...], d[...], e[...])
        o0[...], o1[...], o2[...], o3[...] = r

    return pl.pallas_call(body, name=name, out_shape=[_sds((512, 64))] * 4)(lr16, li16, ldt16, br2, bi2)


def _s5_params_bwd(lr16, li16, ldt16, br2, bi2, da_re16, da_im16, dbb_re, dbb_im, name):
    def body(a, b, c, d, e, g0, g1, g2, g3, o_lr, o_li, o_dt, o_br, o_bi):
        _, vjp = jax.vjp(_s5_disc, a[...], b[...], c[...], d[...], e[...])
        dlr, dli, dldt, dbr, dbi = vjp((g0[...], g1[...], g2[...], g3[...]))
        grp = lambda z: z.reshape(S5_GROUPS, S5_GROUP, S5_STATE).sum(axis=1)
        o_lr[...] = grp(dlr)
        o_li[...] = grp(dli)
        o_dt[...] = jnp.sum(grp(dldt), axis=1, keepdims=True)
        o_br[...] = dbr
        o_bi[...] = dbi

    return pl.pallas_call(
        body, name=name, out_shape=[_sds((32, 64)), _sds((32, 64)), _sds((32, 1)), _sds((512, 64)), _sds((512, 64))],
    )(lr16, li16, ldt16, br2, bi2, da_re16, da_im16, dbb_re, dbb_im)


def _cmul(ar, ai, br, bi):
    return ar * br - ai * bi, ar * bi + ai * br


S5_SUPER = 4


def _scan_loop(bre_r, bim_r, ar, ai, ore_r, oim_r, reverse, xre_r=None, xim_r=None):
    s, lw = bre_r.shape
    nt = s // SUBLANES
    with_da = xre_r is not None
    shp = (SUBLANES, lw)
    row = lax.broadcasted_iota(jnp.int32, shp, 0)
    pows = [(ar, ai)]
    for _ in range(SUBLANES - 1):
        pows.append(_cmul(pows[-1][0], pows[-1][1], ar, ai))
    cm_r, cm_i = jnp.zeros(shp, F32), jnp.zeros(shp, F32)
    for r in range(SUBLANES):
        e = (SUBLANES - 1 - r) if reverse else r
        cm_r = jnp.where(row == r, jnp.broadcast_to(pows[e][0], shp), cm_r)
        cm_i = jnp.where(row == r, jnp.broadcast_to(pows[e][1], shp), cm_i)
    steps = [(1, pows[0]), (2, pows[1]), (4, pows[3])]

    def tile(it, carry):
        if with_da:
            c_r, c_i, acc_r, acc_i = carry
        else:
            c_r, c_i = carry
        r = (nt - 1 - it) if reverse else it
        off = pl.multiple_of(r * SUBLANES, SUBLANES)
        xr, xi = bre_r[pl.ds(off, SUBLANES), :], bim_r[pl.ds(off, SUBLANES), :]
        for sh, (pr, pi) in steps:
            if reverse:
                keep = row < SUBLANES - sh
                sr = jnp.where(keep, pltpu.roll(xr, SUBLANES - sh, 0), 0.0)
                si = jnp.where(keep, pltpu.roll(xi, SUBLANES - sh, 0), 0.0)
            else:
                keep = row >= sh
                sr = jnp.where(keep, pltpu.roll(xr, sh, 0), 0.0)
                si = jnp.where(keep, pltpu.roll(xi, sh, 0), 0.0)
            mr, mi = _cmul(pr, pi, sr, si)
            xr, xi = xr + mr, xi + mi
        mr, mi = _cmul(cm_r, cm_i, c_r, c_i)
        xr, xi = xr + mr, xi + mi
        ore_r[pl.ds(off, SUBLANES), :] = xr
        oim_r[pl.ds(off, SUBLANES), :] = xi
        edge = 0 if reverse else SUBLANES - 1
        c_r, c_i = xr[edge:edge + 1, :], xi[edge:edge + 1, :]
        if not with_da:
            return c_r, c_i
        fr, fi = xre_r[pl.ds(off, SUBLANES), :], xim_r[pl.ds(off, SUBLANES), :]
        poff = pl.multiple_of(jnp.maximum(r - 1, 0) * SUBLANES, SUBLANES)
        live = (r > 0).astype(F32)
        pr_last = xre_r[pl.ds(poff, SUBLANES), :][SUBLANES - 1:SUBLANES, :] * live
        pi_last = xim_r[pl.ds(poff, SUBLANES), :][SUBLANES - 1:SUBLANES, :] * live
        sr = jnp.where(row >= 1, pltpu.roll(fr, 1, 0), jnp.broadcast_to(pr_last, shp))
        si = jnp.where(row >= 1, pltpu.roll(fi, 1, 0), jnp.broadcast_to(pi_last, shp))
        return c_r, c_i, acc_r + xr * sr + xi * si, acc_i + xi * sr - xr * si

    z1 = jnp.zeros((1, lw), F32)
    if not with_da:
        lax.fori_loop(0, nt, tile, (z1, z1))
        return None
    _, _, acc_r, acc_i = lax.fori_loop(0, nt, tile, (z1, z1, jnp.zeros(shp, F32), jnp.zeros(shp, F32)))
    return jnp.sum(acc_r, axis=0, keepdims=True), jnp.sum(acc_i, axis=0, keepdims=True)


S5_ROWS = 512


def _group_compact(p):
    grp = lax.broadcasted_iota(jnp.int32, (LANES, S5_STATE), 0) // S5_GROUP
    out = jnp.zeros((LANES, S5_STATE), F32)
    for j in range(LANES // S5_GROUP):
        out = jnp.where(grp == j, p[:, S5_STATE * j:S5_STATE * (j + 1)], out)
    return out


def _s5_core_fwd(u, wb_re, wb_im, wc_re, wc_im, a_re, a_im, name, ride=None):
    s = u.shape[0]
    lw = S5_LANES // S5_SUPER
    rows = _pick(s, (S5_ROWS, 256, 128))

    def body(u_r, wbr, wbi, wcr, wci, are_r, aim_r, xre_r, xim_r, y_r, bre_s, bim_s):
        for r0 in range(0, s, rows):
            ub = u_r[r0:r0 + rows, :]
            bre_s[r0:r0 + rows, :] = _nn(ub, wbr[0])
            bim_s[r0:r0 + rows, :] = _nn(ub, wbi[0])
        _scan_loop(bre_s, bim_s, are_r[...], aim_r[...], xre_r, xim_r, False)
        for r0 in range(0, s, rows):
            y_r[r0:r0 + rows, :] = _nn(xre_r[r0:r0 + rows, :], wcr[0]) + _nn(xim_r[r0:r0 + rows, :], wci[0])

    nar = pl.BlockSpec((s, LANES), lambda k: (0, k))
    wide = pl.BlockSpec((s, lw), lambda k: (0, k))
    one = pl.BlockSpec((1, lw), lambda k: (0, k))
    wb = pl.BlockSpec((1, LANES, lw), lambda k: (k, 0, 0))
    wc = pl.BlockSpec((1, lw, LANES), lambda k: (k, 0, 0))
    return _call_with_ride(body, name, (S5_SUPER,), [u, wb_re, wb_im, wc_re, wc_im, a_re, a_im], [nar, wb, wb, wc, wc, one, one],
                           [_sds((s, S5_LANES)), _sds((s, S5_LANES)), _sds((s, S5_WIDTH))], [wide, wide, nar], ("parallel",), 56,
                           ride, scratch=[pltpu.VMEM((s, lw), F32), pltpu.VMEM((s, lw), F32)])


def _s5_core_bwd(dy0, u, du_a, x_re, x_im, wb_re, wb_im, wc_re, wc_im, a_re, a_im_neg, name):
    s = u.shape[0]
    lw = S5_LANES // S5_SUPER
    rows = _pick(s, (S5_ROWS, 256, 128))

    def body(dy_r, u_r, dua_r, xre_r, xim_r, wbr, wbi, wcr, wci, are_r, aim_r, du_r, dare_r, daim_r, dbr_r, dbi_r, dcr_r, dci_r,
             dre_s, dim_s, gre_s, gim_s):
        for r0 in range(0, s, rows):
            dyb = dy_r[r0:r0 + rows, :]
            dre_s[r0:r0 + rows, :] = _nt(dyb, wcr[0])
            dim_s[r0:r0 + rows, :] = _nt(dyb, wci[0])
        dare_r[...], daim_r[...] = _scan_loop(dre_s, dim_s, are_r[...], aim_r[...], gre_s, gim_s, True, xre_r, xim_r)
        acc = [jnp.zeros((LANES, lw), F32) for _ in range(4)]
        for r0 in range(0, s, rows):
            sl = slice(r0, r0 + rows)
            gr, gi, ub, dyb = gre_s[sl, :], gim_s[sl, :], u_r[sl, :], dy_r[sl, :]
            du_r[sl, :] = dua_r[sl, :] + _nt(gr, wbr[0]) + _nt(gi, wbi[0])
            acc = [acc[0] + _tn(ub, gr), acc[1] + _tn(ub, gi), acc[2] + _tn(dyb, xre_r[sl, :]), acc[3] + _tn(dyb, xim_r[sl, :])]
        dbr_r[...], dbi_r[...], dcr_r[...], dci_r[...] = (_group_compact(a) for a in acc)

    nar = pl.BlockSpec((s, LANES), lambda k: (0, k))
    wide = pl.BlockSpec((s, lw), lambda k: (0, k))
    one = pl.BlockSpec((1, lw), lambda k: (0, k))
    wb = pl.BlockSpec((1, LANES, lw), lambda k: (k, 0, 0))
    wc = pl.BlockSpec((1, lw, LANES), lambda k: (k, 0, 0))
    blk = pl.BlockSpec((LANES, S5_STATE), lambda k: (k, 0))
    return _call_with_ride(
        body, name, (S5_SUPER,), [dy0, u, du_a, x_re, x_im, wb_re, wb_im, wc_re, wc_im, a_re, a_im_neg],
        [nar, nar, nar, wide, wide, wb, wb, wc, wc, one, one],
        [_sds((s, S5_WIDTH)), _sds((1, S5_LANES)), _sds((1, S5_LANES))] + [_sds((S5_WIDTH, S5_STATE))] * 4,
        [nar, one, one, blk, blk, blk, blk], ("parallel",), 60, None, scratch=[pltpu.VMEM((s, lw), F32)] * 4)


def _s5_seg1(y0, u, d):
    return jax.nn.gelu(y0 + d * u)


def _s5_seg2(z, t, b):
    return z * jax.nn.sigmoid(t + b)


def _s5_post_fwd(y0, seg2, d, wglu, bglu, name):
    s = y0.shape[0]
    ts = _pick(s, (256, 128))

    def fn(ins, outs, accs, scr):
        z = _s5_seg1(ins[0][...], ins[1][...], ins[2][...])
        outs[0][...] = _s5_seg2(z, _nn(z, ins[3][...]), ins[4][...])

    return _stage(name, fn, s // ts,
                  [(y0, _rows(ts, 512)), (seg2, _rows(ts, 512)), (d, _full(d.shape)), (wglu, _full(wglu.shape)),
                   (bglu, _full(bglu.shape))], [(_sds((s, 512)), _rows(ts, 512))])[0]


def _s5_post_bwd(y0, seg2, d, wglu, bglu, dy, name):
    s = y0.shape[0]
    ts = _pick(s, (256, 128))

    def fn(ins, outs, accs, scr):
        y0_r, u_r, d_r, w_r, b_r, dy_r = ins
        z, vjp1 = jax.vjp(_s5_seg1, y0_r[...], u_r[...], d_r[...])
        t = _nn(z, w_r[...])
        _, vjp2 = jax.vjp(_s5_seg2, z, t, b_r[...])
        dz, dt, db = vjp2(dy_r[...])
        accs[0][...] += _tn(z, dt)
        accs[1][...] += db
        dy0, du, dd = vjp1(dz + _nt(dt, w_r[...]))
        accs[2][...] += dd
        outs[0][...] = dy0
        outs[1][...] = du

    return _stage(name, fn, s // ts,
                  [(y0, _rows(ts, 512)), (seg2, _rows(ts, 512)), (d, _full(d.shape)), (wglu, _full(wglu.shape)),
                   (bglu, _full(bglu.shape)), (dy, _rows(ts, 512))],
                  [(_sds((s, 512)), _rows(ts, 512)), (_sds((s, 512)), _rows(ts, 512))],
                  accs=[_sds((512, 512)), _sds((1, 512)), _sds((1, 512))])


def _gate_a(y, g):
    return y * jax.nn.silu(g)


def _gate_m(o0, o1, o2, m0, m1, m2):
    return jax.nn.sigmoid(m0) * o0 + jax.nn.sigmoid(m1) * o1 + jax.nn.sigmoid(m2) * o2


def _assemble(ybuf, o_mla, o_fox, y_s5):
    for h in range(HEADS):
        ybuf[:, 64 * h:64 * (h + 1)] = o_mla[h]
        ybuf[:, 512 + 64 * h:512 + 64 * (h + 1)] = o_fox[h]
    ybuf[:, 1024:1536] = y_s5[...]


def _gate_fwd(o_mla, o_fox, y_s5, seg3, x, wo, wout, name):
    s = x.shape[0]
    ts = _pick(s, (256, 128))

    def fn(ins, outs, accs, scr):
        om, of, ys, gm_r, x_r, wo_r, wout_r = ins
        ybuf = scr[0]
        _assemble(ybuf, om, of, ys)
        a = _gate_a(ybuf[...], gm_r[:, :GATES_W])
        o = [_nn(a[:, 512 * b:512 * (b + 1)], wo_r[512 * b:512 * (b + 1), :]) for b in range(3)]
        merged = _gate_m(o[0], o[1], o[2], *[gm_r[:, GATES_W + 1024 * b:GATES_W + 1024 * (b + 1)] for b in range(3)])
        outs[0][...] = x_r[...] + _nn(merged, wout_r[...])

    return _stage(name, fn, s // ts,
                  [(o_mla, _heads(ts, 64)), (o_fox, _heads(ts, 64)), (y_s5, _rows(ts, 512)), (seg3, _rows(ts, SEG_W[3])),
                   (x, _rows(ts, D_MODEL)), (wo, _full(wo.shape)), (wout, _full(wout.shape))],
                  [(_sds((s, D_MODEL)), _rows(ts, D_MODEL))], scratch=[pltpu.VMEM((ts, 1536), F32)])[0]


def _gate_bwd(o_mla, o_fox, y_s5, seg3, wo, wout, dout, name):
    s = dout.shape[0]
    ts = _pick(s, (128,))

    def fn(ins, outs, accs, scr):
        om, of, ys, gm_r, wo_r, wout_r, dout_r = ins
        do_mla, do_fox, dys, dgm_r, dl_mla = outs
        dwo, dwout = accs
        ybuf, dabuf = scr
        _assemble(ybuf, om, of, ys)
        a, vjp_a = jax.vjp(_gate_a, ybuf[...], gm_r[:, :GATES_W])
        o = [_nn(a[:, 512 * b:512 * (b + 1)], wo_r[512 * b:512 * (b + 1), :]) for b in range(3)]
        ms = [gm_r[:, GATES_W + 1024 * b:GATES_W + 1024 * (b + 1)] for b in range(3)]
        merged, vjp_m = jax.vjp(_gate_m, *o, *ms)
        dout_v = dout_r[...]
        dwout[...] += _tn(merged, dout_v)
        cts = vjp_m(_nt(dout_v, wout_r[...]))
        for b in range(3):
            dgm_r[:, GATES_W + 1024 * b:GATES_W + 1024 * (b + 1)] = cts[3 + b].astype(BF16)
            dwo[512 * b:512 * (b + 1), :] += _tn(a[:, 512 * b:512 * (b + 1)], cts[b])
            dabuf[:, 512 * b:512 * (b + 1)] = _nt(cts[b], wo_r[512 * b:512 * (b + 1), :])
        dy, dg = vjp_a(dabuf[...])
        dgm_r[:, :GATES_W] = dg.astype(BF16)
        dys[...] = dy[:, 1024:1536]
        for h in range(HEADS):
            d = dy[:, 64 * h:64 * (h + 1)]
            do_mla[h] = d
            dl_mla[h] = jnp.sum(d * om[h], axis=1, keepdims=True)
            do_fox[h] = dy[:, 512 + 64 * h:512 + 64 * (h + 1)]

    return _stage(name, fn, s // ts,
                  [(o_mla, _heads(ts, 64)), (o_fox, _heads(ts, 64)), (y_s5, _rows(ts, 512)), (seg3, _rows(ts, SEG_W[3])),
                   (wo, _full(wo.shape)), (wout, _full(wout.shape)), (dout, _rows(ts, D_MODEL))],
                  [(_sds((HEADS, s, 64)), _heads(ts, 64)), (_sds((HEADS, s, 64)), _heads(ts, 64)), (_sds((s, 512)), _rows(ts, 512)),
                   (_sds((s, SEG_W[3]), BF16), _rows(ts, SEG_W[3])), (_sds((HEADS, s, 1)), _heads(ts, 1))],
                  accs=[_sds(wo.shape), _sds(wout.shape)], scratch=[pltpu.VMEM((ts, 1536), F32), pltpu.VMEM((ts, 1536), F32)],
                  vmem_mb=56)


def _loss_head(y, target, name):
    s = y.shape[0]
    ts = _pick(s, (256, 128))

    def fn(ins, outs, accs, scr):
        e = ins[0][...] - ins[1][...]
        outs[0][...] = e / D_MODEL
        accs[0][...] += 0.5 * jnp.sum(jnp.sum(e * e, axis=1, keepdims=True) / D_MODEL, axis=0, keepdims=True)

    r = _stage(name, fn, s // ts, [(y, _rows(ts, D_MODEL)), (target, _rows(ts, D_MODEL))],
               [(_sds((s, D_MODEL)), _rows(ts, D_MODEL))], accs=[_sds((1, 1))])
    return r[1], r[0]


IN_RANGES = ((0, 384, 0), (384, 416, 448), (416, 1952, 512), (1952, 1960, 384), (1960, IN_WIDTH, 2048))
SHARD_W = IN_WIDTH // N_DEV


def _win_pieces(d):
    lo, hi = SHARD_W * d, SHARD_W * (d + 1)
    out = []
    for a, b, p in IN_RANGES:
        s, e = max(a, lo), min(b, hi)
        while s < e:
            pad = p + (s - a)
            k = max(i for i in range(NSEG) if SEG_OFF[i] <= pad)
            w = min(e - s, SEG_OFF[k] + SEG_W[k] - pad)
            out.append((s - lo, w, k, pad - SEG_OFF[k]))
            s += w
    return out


def _win_pad(g, name):
    _, r, _ = g.shape
    tr = 64

    def body(g_ref, o_ref):
        o_ref[...] = jnp.zeros(o_ref.shape, o_ref.dtype)
        for d in range(N_DEV):
            for dst, w, k, src in _win_pieces(d):
                o_ref[:, SEG_OFF[k] + src:SEG_OFF[k] + src + w] = g_ref[d, :, dst:dst + w]

    return pl.pallas_call(
        body, name=name, grid=(r // tr,), in_specs=[pl.BlockSpec((N_DEV, tr, SHARD_W), lambda i: (0, i, 0))],
        out_specs=pl.BlockSpec((tr, PAD_IN), lambda i: (i, 0)), out_shape=jax.ShapeDtypeStruct((r, PAD_IN), g.dtype),
        compiler_params=pltpu.CompilerParams(dimension_semantics=("parallel",)),
    )(g)


def _win_unpad(dsegs, name):
    r = dsegs[0].shape[0]
    tr = 64

    def body(*refs):
        o_ref = refs[NSEG]
        for d in range(N_DEV):
            for dst, w, k, src in _win_pieces(d):
                o_ref[d, :, dst:dst + w] = refs[k][:, src:src + w]

    return pl.pallas_call(
        body, name=name, grid=(r // tr,), in_specs=[pl.BlockSpec((tr, SEG_W[k]), lambda i: (i, 0)) for k in range(NSEG)],
        out_specs=pl.BlockSpec((N_DEV, tr, SHARD_W), lambda i: (0, i, 0)),
        out_shape=jax.ShapeDtypeStruct((N_DEV, r, SHARD_W), dsegs[0].dtype),
        compiler_params=pltpu.CompilerParams(dimension_semantics=("parallel",)),
    )(*dsegs)


def _pad_wq(w):
    w = w.reshape(MLA_Q_RANK, HEADS, MLA_QK)
    return jnp.pad(w, ((0, 0), (0, 0), (0, LANES - MLA_QK))).reshape(MLA_Q_RANK, HEADS * LANES)


def _unpad_wq(d):
    return d.reshape(MLA_Q_RANK, HEADS, LANES)[:, :, :MLA_QK].reshape(MLA_Q_RANK, HEADS * MLA_QK)


def _pad_wkv(w):
    w = w.reshape(MLA_KV_RANK, HEADS, MLA_NOPE + MLA_V)
    k = jnp.pad(w[:, :, :MLA_NOPE], ((0, 0), (0, 0), (0, LANES - MLA_NOPE))).reshape(MLA_KV_RANK, HEADS * LANES)
    return jnp.concatenate([k, w[:, :, MLA_NOPE:].reshape(MLA_KV_RANK, HEADS * MLA_V)], axis=1)


def _unpad_wkv(d):
    k = d[:, :HEADS * LANES].reshape(MLA_KV_RANK, HEADS, LANES)[:, :, :MLA_NOPE]
    v = d[:, HEADS * LANES:].reshape(MLA_KV_RANK, HEADS, MLA_V)
    return jnp.concatenate([k, v], axis=2).reshape(MLA_KV_RANK, HEADS * (MLA_NOPE + MLA_V))


def _pad_lanes(v, n=LANES):
    return jnp.pad(v, (0, n - v.shape[0])).reshape(1, n)


def _super_blocks(b):
    _, r, c = b.shape
    per = S5_GROUPS // S5_SUPER
    b = b.reshape(S5_SUPER, per, r, c)
    eye = jnp.eye(per, dtype=b.dtype)
    return (b[:, :, :, None, :] * eye[None, :, None, :, None]).reshape(S5_SUPER, per * r, per * c)


def _layer_params(l, w, small):
    p = {k: small[k][l] for k in small}
    q = {}
    q["norm_g"] = p["norm_g"].reshape(1, D_MODEL)
    q["mla_q_a_norm"] = p["mla_q_a_norm"].reshape(1, 256)
    q["mla_kv_a_norm"] = p["mla_kv_a_norm"].reshape(1, 128)
    q["mla_q_norm"] = _pad_lanes(p["mla_q_norm"])
    q["mla_k_norm"] = _pad_lanes(p["mla_k_norm"])
    q["fox_b_f"] = _pad_lanes(p["fox_b_f"])
    q["fox_q_norm"] = p["fox_q_norm"].reshape(1, 64)
    q["fox_k_norm"] = p["fox_k_norm"].reshape(1, 64)
    q["s5_d"] = p["s5_d"].reshape(1, 512)
    q["s5_b_glu"] = p["s5_b_glu"].reshape(1, 512)
    rep = lambda z: jnp.repeat(z, S5_GROUP, axis=0)
    q["lr16"], q["li16"] = rep(p["s5_lambda_re"]), rep(p["s5_lambda_im"])
    q["ldt16"] = rep(jnp.broadcast_to(p["s5_log_dt"][:, None], (S5_GROUPS, S5_STATE)))
    q["br2"] = p["s5_b_re"].transpose(0, 2, 1).reshape(512, 64)
    q["bi2"] = p["s5_b_im"].transpose(0, 2, 1).reshape(512, 64)
    q["c_re"], q["c_im"] = p["s5_c_re"], p["s5_c_im"]
    q["w_in"] = _win_pad(w["w_in"], f"l{l}_w_in_pad")
    q["wq"] = _pad_wq(w["mla_w_q_up"])
    q["wkv"] = _pad_wkv(w["mla_w_kv_up"])
    q.update(_late_weights(w))
    return q


def _late_weights(w):
    return {q: w[k] for q, k in (("wglu", "s5_w_glu"), ("wo", "w_branch_out"), ("wout", "w_out")) if k in w}


def _layer_fwd(l, x, tabs, q, rides=None):
    rides = rides or {}
    n = lambda s: f"l{l}_{s}"
    sv = {"x": x}
    h = _norm_fwd(x, q["norm_g"], n("norm_fwd"))
    sv["h"] = h
    seg = [_mm(h, q["w_in"], "nn", n(f"proj{k}"), b_cols=(SEG_OFF[k], SEG_W[k])) for k in range(NSEG)]
    sv["seg"] = seg
    mq, mk, mv = _mla_prep_fwd(seg[0], tabs, q, q["wq"], q["wkv"], n("mla_prep_fwd"))
    o_mla, lse_mla = _attn_fwd(mq, mk, mv, 1.0 / math.sqrt(MLA_QK), True, n("mla_attn_fwd"), ride=rides.get("mla_attn_fwd"))
    sv.update(mq=mq, mk=mk, mv=mv, o_mla=o_mla, lse_mla=lse_mla)
    fq, fk, fv, c, ct = _fox_prep_fwd(seg[0], seg[1], q["fox_b_f"], q["fox_q_norm"], q["fox_k_norm"], n("fox_prep_fwd"))
    o_fox, lse_fox = _attn_fwd(fq, fk, fv, 1.0 / math.sqrt(FOX_DIM), False, n("fox_attn_fwd"), c=c, ct=ct,
                               ride=rides.get("fox_attn_fwd"))
    sv.update(fq=fq, fk=fk, fv=fv, c=c, ct=ct, o_fox=o_fox, lse_fox=lse_fox)
    a_re16, a_im16, bb_re, bb_im = _s5_params_fwd(q["lr16"], q["li16"], q["ldt16"], q["br2"], q["bi2"], n("s5_params_fwd"))
    a_re = a_re16.reshape(S5_GROUPS, S5_GROUP, S5_STATE)[:, 0, :].reshape(1, S5_LANES)
    a_im = a_im16.reshape(S5_GROUPS, S5_GROUP, S5_STATE)[:, 0, :].reshape(1, S5_LANES)
    wb_re = _super_blocks(bb_re.reshape(S5_GROUPS, S5_GROUP, S5_STATE)).astype(BF16)
    wb_im = _super_blocks(bb_im.reshape(S5_GROUPS, S5_GROUP, S5_STATE)).astype(BF16)
    wc_re = _super_blocks(q["c_re"].transpose(0, 2, 1)).astype(BF16)
    wc_im = _super_blocks(-q["c_im"].transpose(0, 2, 1)).astype(BF16)
    x_re, x_im, y0 = _s5_core_fwd(seg[2], wb_re, wb_im, wc_re, wc_im, a_re, a_im, n("s5_scan_fwd"), ride=rides.get("s5_scan_fwd"))
    y_s5 = _s5_post_fwd(y0, seg[2], q["s5_d"], q["wglu"], q["s5_b_glu"], n("s5_post_fwd"))
    sv.update(a_re=a_re, a_im=a_im, wb_re=wb_re, wb_im=wb_im, wc_re=wc_re, wc_im=wc_im, x_re=x_re, x_im=x_im, y0=y0, y_s5=y_s5)
    out = _gate_fwd(o_mla, o_fox, y_s5, seg[3], x, q["wo"], q["wout"], n("gate_fwd"))
    return out, sv


def _layer_bwd(l, dout, tabs, q, sv, rides=None, g=None):
    rides = rides or {}
    n = lambda s: f"l{l}_{s}"
    seg = sv["seg"]
    g = {} if g is None else g
    (do_mla, do_fox, dy_s5, dseg3, dl_mla, g["wo"], g["wout"]) = _gate_bwd(
        sv["o_mla"], sv["o_fox"], sv["y_s5"], seg[3], q["wo"], q["wout"], dout, n("gate_bwd"))
    dy0, du_a, g["wglu"], g["s5_b_glu"], g["s5_d"] = _s5_post_bwd(sv["y0"], seg[2], q["s5_d"], q["wglu"], q["s5_b_glu"], dy_s5,
                                                                 n("s5_post_bwd"))
    dseg2, da_re, da_im, dbb_re, dbb_im, dc_re, dc_im = _s5_core_bwd(
        dy0, seg[2], du_a, sv["x_re"], sv["x_im"], sv["wb_re"], sv["wb_im"], sv["wc_re"], sv["wc_im"], sv["a_re"], -sv["a_im"],
        n("s5_scan_bwd"))
    g["s5_c_re"] = dc_re.reshape(S5_GROUPS, S5_GROUP, S5_STATE)
    g["s5_c_im"] = -dc_im.reshape(S5_GROUPS, S5_GROUP, S5_STATE)
    first = (jnp.arange(512) % S5_GROUP == 0).astype(F32)[:, None]
    da_re16 = jnp.repeat(da_re.reshape(S5_GROUPS, S5_STATE), S5_GROUP, axis=0) * first
    da_im16 = jnp.repeat(da_im.reshape(S5_GROUPS, S5_STATE), S5_GROUP, axis=0) * first
    dlr, dli, dldt, dbr2, dbi2 = _s5_params_bwd(q["lr16"], q["li16"], q["ldt16"], q["br2"], q["bi2"], da_re16, da_im16, dbb_re,
                                               dbb_im, n("s5_params_bwd"))
    g["s5_lambda_re"], g["s5_lambda_im"], g["s5_log_dt"] = dlr, dli, dldt.reshape(S5_GROUPS)
    g["s5_b_re"] = dbr2.reshape(S5_GROUPS, S5_GROUP, S5_STATE).transpose(0, 2, 1)
    g["s5_b_im"] = dbi2.reshape(S5_GROUPS, S5_GROUP, S5_STATE).transpose(0, 2, 1)
    dl_fox = _attn_delta(sv["fq"], sv["fk"], sv["fv"], do_fox, sv["lse_fox"], 1.0 / math.sqrt(FOX_DIM), False,
                         n("fox_attn_delta"), c=sv["c"], ct=sv["ct"], ride=rides.get("fox_attn_delta"))
    dfq, dfk, dfv, dck = _attn_bwd(sv["fq"], sv["fk"], sv["fv"], do_fox, sv["lse_fox"], dl_fox, 1.0 / math.sqrt(FOX_DIM), False,
                                   n("fox_attn_bwd"), c=sv["c"], ct=sv["ct"], ride=rides.get("fox_attn_bwd"))
    dseg1, dff, g["fox_q_norm"], g["fox_k_norm"], dbf = _fox_prep_bwd(seg[0], seg[1], q["fox_b_f"], q["fox_q_norm"], q["fox_k_norm"],
                                                                      dfq, dfk, dfv, dck, n("fox_prep_bwd"))
    g["fox_b_f"] = dbf[0, :HEADS]
    dmq, dmk, dmv = _attn_bwd(sv["mq"], sv["mk"], sv["mv"], do_mla, sv["lse_mla"], dl_mla, 1.0 / math.sqrt(MLA_QK), True,
                              n("mla_attn_bwd"), ride=rides.get("mla_attn_bwd"))
    dseg0, dqan, dkvan, dqn, dkn, g["wq"], g["wkv"] = _mla_prep_bwd(seg[0], tabs, q, q["wq"], q["wkv"], dmq, dmk, dmv, dff,
                                                                   n("mla_prep_bwd"))
    g["mla_q_a_norm"], g["mla_kv_a_norm"] = dqan, dkvan
    g["mla_q_norm"], g["mla_k_norm"] = dqn[0, :MLA_QK], dkn[0, :MLA_QK]
    dsegs = [dseg0, dseg1, dseg2, dseg3]
    g["w_in"] = [_mm(sv["h"], dsegs[k], "tn", n(f"dwin{k}")) for k in range(NSEG)]
    dh = None
    for k in range(NSEG):
        dh = _mm(dsegs[k], q["w_in"], "nt", n(f"dh{k}"), acc=dh, b_cols=(SEG_OFF[k], SEG_W[k]), ride=rides.get(f"dh{k}"))
    dx, g["norm_g"] = _norm_bwd(sv["x"], q["norm_g"], dh, dout, n("norm_bwd"))
    return dx, g


MESH = pl.DeviceIdType.MESH
ANY = pl.BlockSpec(memory_space=pl.ANY)


def _all_gather(blocks, name):
    n = len(blocks)

    def body(*refs):
        x_refs, out_refs = refs[:n], refs[n:2 * n]
        send_sems, recv_sems, local_sems = refs[2 * n:]
        x, y, c = lax.axis_index("x"), lax.axis_index("y"), lax.axis_index("c")
        me, sibling = (x, y, c), (x, y, 1 - c)
        chips = [(1 - x, y), (x, 1 - y), (1 - x, 1 - y)]

        def slot(a, px, py, pc):
            return out_refs[a].at[4 * px + 2 * py + pc]

        def copy(a, k, blk, to, src=None):
            return pltpu.make_async_remote_copy(src_ref=slot(a, *blk) if src is None else src, dst_ref=slot(a, *blk),
                                                send_sem=send_sems.at[7 * a + k], recv_sem=recv_sems.at[7 * a + k],
                                                device_id=to, device_id_type=MESH)

        mine = [pltpu.make_async_copy(x_refs[a], slot(a, *me), local_sems.at[a]) for a in range(n)]
        for cp in mine:
            cp.start()
        first = []
        for j, chip in enumerate(chips):
            first += [copy(a, 1 + j, me, (*chip, c), src=x_refs[a]) for a in range(n)]
        first += [copy(a, 0, me, sibling, src=x_refs[a]) for a in range(n)]
        for cp in first:
            cp.start()
        passed = []
        for j, chip in enumerate(chips):
            for a in range(n):
                copy(a, 1 + j, (*chip, c), me).wait_recv()
                passed.append(copy(a, 4 + j, (*chip, c), sibling))
                passed[-1].start()
        for a in range(n):
            copy(a, 0, sibling, me).wait_recv()
        for j, chip in enumerate(chips):
            for a in range(n):
                copy(a, 4 + j, (*chip, 1 - c), me).wait_recv()
        for cp in first + passed:
            cp.wait_send()
        for cp in mine:
            cp.wait()

    return pl.pallas_call(
        body, name=name, out_shape=[jax.ShapeDtypeStruct((N_DEV,) + b.shape, b.dtype) for b in blocks],
        in_specs=[ANY] * n, out_specs=[ANY] * n,
        scratch_shapes=[pltpu.SemaphoreType.DMA((7 * n,)), pltpu.SemaphoreType.DMA((7 * n,)), pltpu.SemaphoreType.DMA((n,))],
    )(*blocks)


def _place():
    x, y, c = lax.axis_index("x"), lax.axis_index("y"), lax.axis_index("c")
    return x, y, c, [(1 - x, y), (x, 1 - y), (1 - x, 1 - y)]


def _remote(src, dst, send, recv, k, to):
    return pltpu.make_async_remote_copy(src_ref=src, dst_ref=dst, send_sem=send.at[k], recv_sem=recv.at[k], device_id=to,
                                        device_id_type=MESH)


def _plan_gather_ici(blocks):
    n = len(blocks)

    def copies(in_refs, out_refs, send, recv, local):
        x, y, c, chips = _place()
        mine = 4 * x + 2 * y + c
        loc = [pltpu.make_async_copy(in_refs[a], out_refs[a].at[mine], local.at[a]) for a in range(n)]
        rem = [_remote(in_refs[a], out_refs[a].at[mine], send, recv, 3 * a + j, (px, py, c))
               for j, (px, py) in enumerate(chips) for a in range(n)]
        return rem, loc

    return _Plan(blocks, [jax.ShapeDtypeStruct((N_DEV,) + b.shape, b.dtype) for b in blocks], 3 * n, n, copies)


def _plan_gather_d2d(gathered):
    n = len(gathered)

    def copies(in_refs, out_refs, send, recv, local):
        x, y, c, _ = _place()
        rem = [_remote(in_refs[a].at[2 * j + c], out_refs[a].at[2 * j + c], send, recv, 4 * a + j, (x, y, 1 - c))
               for a in range(n) for j in range(4)]
        return rem, []

    return _Plan(gathered, [jax.ShapeDtypeStruct(g.shape, g.dtype) for g in gathered], 4 * n, 0, copies,
                 aliases={a: a for a in range(n)})


def _plan_reduce_sibling(parts):
    n = len(parts)

    def copies(in_refs, out_refs, send, recv, local):
        x, y, c, _ = _place()
        rem = [_remote(in_refs[a].at[2 * j + (1 - c)], out_refs[a].at[j], send, recv, 4 * a + j, (x, y, 1 - c))
               for a in range(n) for j in range(4)]
        return rem, []

    return _Plan(parts, [jax.ShapeDtypeStruct((4,) + p.shape[1:], p.dtype) for p in parts], 4 * n, 0, copies)


def _plan_reduce_chips(sums):
    n = len(sums)

    def copies(in_refs, out_refs, send, recv, local):
        x, y, c, chips = _place()
        mine = 2 * x + y
        loc = [pltpu.make_async_copy(in_refs[a].at[mine], out_refs[a].at[mine], local.at[a]) for a in range(n)]
        rem = [_remote(in_refs[a].at[2 * px + py], out_refs[a].at[mine], send, recv, 3 * a + k, (px, py, c))
               for k, (px, py) in enumerate(chips) for a in range(n)]
        return rem, loc

    return _Plan(sums, [jax.ShapeDtypeStruct(p.shape, p.dtype) for p in sums], 3 * n, n, copies)


def _add_sibling(parts, got, name):
    _, r, cc = parts.shape
    tr = _pick(r, (512, 256, 128, 64, 32, 16))
    c = lax.axis_index("c")

    def body(c_ref, p_ref, g_ref, o_ref):
        o_ref[...] = (p_ref[...] + g_ref[...]).astype(BF16)

    return pl.pallas_call(
        body, name=name, out_shape=jax.ShapeDtypeStruct((4, r, cc), BF16),
        grid_spec=pltpu.PrefetchScalarGridSpec(
            num_scalar_prefetch=1, grid=(4, r // tr),
            in_specs=[pl.BlockSpec((1, tr, cc), lambda j, i, cr: (2 * j + cr[0], i, 0)),
                      pl.BlockSpec((1, tr, cc), lambda j, i, cr: (j, i, 0))],
            out_specs=pl.BlockSpec((1, tr, cc), lambda j, i, cr: (j, i, 0))),
        compiler_params=_vmem(48),
    )(c.reshape(1).astype(jnp.int32), parts, got)


def _sum_leading(parts, name):
    k, r, cc = parts.shape
    tr = _pick(r, (512, 256, 128, 64, 32, 16, 8))

    def body(p_ref, o_ref):
        acc = p_ref[0]
        for j in range(1, k):
            acc = acc + p_ref[j]
        o_ref[...] = acc

    return pl.pallas_call(
        body, name=name, out_shape=jax.ShapeDtypeStruct((r, cc), F32), grid=(r // tr,),
        in_specs=[pl.BlockSpec((k, tr, cc), lambda i: (0, i, 0))], out_specs=pl.BlockSpec((tr, cc), lambda i: (i, 0)),
    )(parts)


def _adamw_math(w, g, m, v):
    nm = ADAM_B1 * m + (1.0 - ADAM_B1) * g
    nv = ADAM_B2 * v + (1.0 - ADAM_B2) * jnp.square(g)
    m_hat = nm / (1.0 - ADAM_B1 ** ADAM_STEP)
    v_hat = nv / (1.0 - ADAM_B2 ** ADAM_STEP)
    return -ADAM_LR * (m_hat / (jnp.sqrt(v_hat) + ADAM_EPS) + ADAM_WD * w), nm, nv


def _adamw_sum(w, contribs, m, v, name, ride=None):
    nl = len(contribs)
    k, r, cc = contribs[0].shape
    tr = _pick(r, (256, 128, 64, 32, 16))
    nb = r // tr

    def body(w_ref, *rest):
        c_refs = rest[:nl]
        m_ref, v_ref, g_ref, d_ref, nm_ref, nv_ref = rest[nl:]
        for li in range(nl):
            @pl.when(pl.program_id(0) == li)
            def _(c_ref=c_refs[li]):
                g = c_ref[0].astype(F32)
                for j in range(1, k):
                    g = g + c_ref[j].astype(F32)
                g_ref[...] = g
                d_ref[...], nm_ref[...], nv_ref[...] = _adamw_math(w_ref[...], g, m_ref[...], v_ref[...])

    spec = pl.BlockSpec((tr, cc), lambda l, i: (l * nb + i, 0))
    cspec = pl.BlockSpec((k, tr, cc), lambda l, i: (0, i, 0))
    return _call_with_ride(body, name, (nl, nb), [w, *contribs, m, v], [spec] + [cspec] * nl + [spec, spec],
                           [jax.ShapeDtypeStruct(w.shape, F32)] * 4, [spec] * 4, ("parallel", "parallel"), 48, ride)


def _adamw_many(ws, gs, ms, vs, name):
    n = len(ws)

    def body(*refs):
        w_r, g_r, m_r, v_r = refs[:n], refs[n:2 * n], refs[2 * n:3 * n], refs[3 * n:4 * n]
        d_r, nm_r, nv_r = refs[4 * n:5 * n], refs[5 * n:6 * n], refs[6 * n:7 * n]
        for a in range(n):
            d_r[a][...], nm_r[a][...], nv_r[a][...] = _adamw_math(w_r[a][...], g_r[a][...], m_r[a][...], v_r[a][...])

    shapes = [jax.ShapeDtypeStruct(w.shape, F32) for w in ws]
    res = pl.pallas_call(body, name=name, out_shape=shapes * 3,
                         compiler_params=pltpu.CompilerParams(vmem_limit_bytes=56 * 1024 * 1024))(*ws, *gs, *ms, *vs)
    return res[:n], res[n:2 * n], res[2 * n:]


def _pack_rows(flat, lanes, row_mult):
    n = flat.shape[-1]
    rows = -(-n // lanes)
    rows = -(-rows // row_mult) * row_mult
    pad = rows * lanes - n
    if pad:
        flat = jnp.pad(flat, [(0, 0)] * (flat.ndim - 1) + [(0, pad)])
    return flat.reshape(flat.shape[:-1] + (rows, lanes))


def _rope_tables(positions):
    inv = 1.0 / (ROPE_THETA ** (jnp.arange(0, MLA_ROPE, 2, dtype=F32) / MLA_ROPE))
    ang = positions.astype(F32)[:, None] * inv
    cos, sin = jnp.cos(ang), jnp.sin(ang)
    s = positions.shape[0]
    z = lambda n: jnp.zeros((s, n), F32)
    c = jnp.concatenate([jnp.ones((s, 64), F32), cos, cos, z(32)], axis=1)
    sa = jnp.concatenate([z(64), -sin, z(48)], axis=1)
    sb = jnp.concatenate([z(80), sin, z(32)], axis=1)
    return c, sa, sb


def _full_weights(gathered, names=SHARDED):
    full = {}
    for k, g in zip(names, gathered):
        _, r, c = g.shape
        if k == "w_in":
            full[k] = g
        else:
            full[k] = g.transpose(1, 0, 2).reshape(r, N_DEV * c) if k in COL_SHARDED else g.reshape(N_DEV * r, c)
    return full


EARLY = ("s5_w_glu", "w_branch_out", "w_out")
LATE = ("w_in", "mla_w_q_up", "mla_w_kv_up")


def _owner_major(g, names, tag):
    parts = []
    for k in names:
        if k == "w_in":
            parts.append(_win_unpad(g["w_in"], f"{tag}_w_in_unpad"))
            continue
        big = {"mla_w_q_up": lambda: _unpad_wq(g["wq"]), "mla_w_kv_up": lambda: _unpad_wkv(g["wkv"]), "s5_w_glu": lambda: g["wglu"],
               "w_branch_out": lambda: g["wo"], "w_out": lambda: g["wout"]}[k]()
        r, c = big.shape
        if k in COL_SHARDED:
            parts.append(big.reshape(r, N_DEV, c // N_DEV).transpose(1, 0, 2))
        else:
            parts.append(big.reshape(N_DEV, r // N_DEV, c))
    return parts


def _device_step(x, positions, target, shards, small):
    tabs = _rope_tables(positions)
    box = {}
    q0 = _layer_params(0, _full_weights(_all_gather(shards[0][:3], "gather_weights_l0"), LATE), small)

    def arrived(o):
        box.update(w1_b=o[:5])
        q0.update(_late_weights(_full_weights(o[5:], EARLY)))

    rides = {
        "mla_attn_fwd": _Ride(lambda: _plan_gather_ici(shards[1][:1]), lambda o: box.update(ici_a=o)),
        "fox_attn_fwd": _Ride(lambda: _join([_plan_gather_ici(shards[1][1:]), _plan_gather_d2d(box["ici_a"]),
                                             _plan_gather_ici(shards[0][3:])]),
                              lambda o: box.update(ici_b=o[:5], w1_a=o[5:6], ici_0=o[6:])),
        "s5_scan_fwd": _Ride(lambda: _join([_plan_gather_d2d(box["ici_b"]), _plan_gather_d2d(box["ici_0"])]), arrived),
    }
    h, sv0 = _layer_fwd(0, x, tabs, q0, rides)
    q1 = _layer_params(1, _full_weights(list(box["w1_a"]) + list(box["w1_b"])), small)
    h, sv1 = _layer_fwd(1, h, tabs, q1)
    loss, d = _loss_head(h, target, "loss_head")
    d, g1 = _layer_bwd(1, d, tabs, q1, sv1)
    parts1 = _owner_major(g1, SHARDED, "l1")

    def chips_plan(names, parts, got, tag):
        return _plan_reduce_chips([_add_sibling(p, g, f"reduce_add_{tag}_{k}") for k, p, g in zip(names, parts, got)])

    g0 = {}
    rides = {
        "fox_attn_delta": _Ride(lambda: _plan_reduce_sibling(parts1), lambda o: box.update(got1=o)),
        "fox_attn_bwd": _Ride(lambda: _join([chips_plan(SHARDED, parts1, box["got1"], "l1"),
                                             _plan_reduce_sibling(box.setdefault("early0", _owner_major(g0, EARLY, "l0")))]),
                              lambda o: box.update(contribs1=o[:6], got0e=o[6:])),
        "mla_attn_bwd": _Ride(lambda: chips_plan(EARLY, box["early0"], box["got0e"], "l0"), lambda o: box.update(contribs0e=o)),
        "dh1": _Ride(lambda: _plan_reduce_sibling(box.setdefault("late0", _owner_major(g0, LATE, "l0"))),
                     lambda o: box.update(got0l=o)),
        "dh3": _Ride(lambda: chips_plan(LATE, box["late0"], box["got0l"], "l0"), lambda o: box.update(contribs0l=o)),
    }
    d, _ = _layer_bwd(0, d, tabs, q0, sv0, rides, g0)
    contribs0 = list(box["contribs0l"]) + list(box["contribs0e"])
    return loss[0, 0], d, [g0, g1], [contribs0, box["contribs1"]]


def kernel(x, positions, norm_g, w_in, mla_q_a_norm, mla_w_q_up, mla_kv_a_norm, mla_w_kv_up, mla_q_norm, mla_k_norm, fox_b_f, fox_q_norm, fox_k_norm, s5_lambda_re, s5_lambda_im, s5_log_dt, s5_b_re, s5_b_im, s5_c_re, s5_c_im, s5_d, s5_w_glu, s5_b_glu, w_branch_out, w_out, loss_target, m_norm_g, m_w_in, m_mla_q_a_norm, m_mla_w_q_up, m_mla_kv_a_norm, m_mla_w_kv_up, m_mla_q_norm, m_mla_k_norm, m_fox_b_f, m_fox_q_norm, m_fox_k_norm, m_s5_lambda_re, m_s5_lambda_im, m_s5_log_dt, m_s5_b_re, m_s5_b_im, m_s5_c_re, m_s5_c_im, m_s5_d, m_s5_w_glu, m_s5_b_glu, m_w_branch_out, m_w_out, v_norm_g, v_w_in, v_mla_q_a_norm, v_mla_w_q_up, v_mla_kv_a_norm, v_mla_w_kv_up, v_mla_q_norm, v_mla_k_norm, v_fox_b_f, v_fox_q_norm, v_fox_k_norm, v_s5_lambda_re, v_s5_lambda_im, v_s5_log_dt, v_s5_b_re, v_s5_b_im, v_s5_c_re, v_s5_c_im, v_s5_d, v_s5_w_glu, v_s5_b_glu, v_w_branch_out, v_w_out):
    env = dict(locals())
    wts = {k: env[k] for k in WEIGHTS}
    mom = {k: env["m_" + k] for k in WEIGHTS}
    var = {k: env["v_" + k] for k in WEIGHTS}

    shards = [[wts[k][l].astype(BF16) for k in SHARDED] for l in range(DEPTH)]
    small = {k: wts[k] for k in SMALL}
    loss, dx, grads, contribs = _device_step(x[0], positions[0], loss_target[0], shards, small)
    loss = lax.psum(loss, ("x", "y", "c"))

    two_d = {k: (wts[k].shape[0] * wts[k].shape[1], wts[k].shape[2]) for k in SHARDED}
    sm = {k: jnp.stack([g[k] for g in grads]).reshape(wts[k].shape) for k in SMALL}
    small_block = _pack_rows(jnp.concatenate([sm[k].reshape(-1) for k in SMALL]), LANES, 256)
    box = {}
    rides = {SHARDED[0]: _Ride(lambda: _plan_gather_ici([small_block]), lambda o: box.update(ici=o)),
             SHARDED[1]: _Ride(lambda: _plan_gather_d2d(box["ici"]), lambda o: box.update(all=o))}
    grad_out, delta_out, m_out, v_out = {}, {}, {}, {}
    for i, k in enumerate(SHARDED):
        shp = wts[k].shape
        res = _adamw_sum(wts[k].reshape(two_d[k]), [contribs[l][i] for l in range(DEPTH)], mom[k].reshape(two_d[k]),
                         var[k].reshape(two_d[k]), f"adamw_{k}", ride=rides.get(k))
        grad_out[k], delta_out[k], m_out[k], v_out[k] = (z.reshape(shp) for z in res)
    g_small = _sum_leading(box["all"][0], "sum_small_grads").reshape(-1)
    off = 0
    for k in SMALL:
        cnt = int(np.prod(wts[k].shape))
        grad_out[k] = g_small[off:off + cnt].reshape(wts[k].shape)
        off += cnt
    flat2 = lambda a: a.reshape(-1, a.shape[-1])
    d_s, m_s, v_s = _adamw_many([flat2(wts[k]) for k in SMALL], [flat2(grad_out[k]) for k in SMALL],
                                [flat2(mom[k]) for k in SMALL], [flat2(var[k]) for k in SMALL], "adamw_small")
    for i, k in enumerate(SMALL):
        delta_out[k], m_out[k], v_out[k] = (z[i].reshape(wts[k].shape) for z in (d_s, m_s, v_s))

    return (loss, dx[None], *[grad_out[k] for k in WEIGHTS], *[delta_out[k] for k in WEIGHTS],
            *[m_out[k] for k in WEIGHTS], *[v_out[k] for k in WEIGHTS])
```

```python
import functools
import math

import jax
import jax.numpy as jnp
import numpy as np
from jax import lax
from jax.experimental import pallas as pl
from jax.experimental.pallas import tpu as pltpu

F32 = jnp.float32
BF16 = jnp.bfloat16

D_MODEL = 1024
DEPTH = 2
CHUNK = 64
EPS = 1e-6
HEADS = 8
MLA_NOPE, MLA_ROPE, MLA_V = 64, 32, 64
MLA_Q_RANK, MLA_KV_RANK = 256, 128
MLA_QK = MLA_NOPE + MLA_ROPE
ROPE_THETA = 10000.0
FOX_DIM = 64
S5_WIDTH, S5_GROUP, S5_GROUPS, S5_STATE = 512, 16, 32, 64
S5_LANES = S5_GROUPS * S5_STATE
IN_WIDTH = 7080
N_DEV = 8
LANES = 128
SUBLANES = 8

ADAM_LR, ADAM_B1, ADAM_B2, ADAM_EPS, ADAM_WD, ADAM_STEP = 0.001, 0.9, 0.999, 1e-08, 0.01, 10

SEG_W = (512, 1536, 512, 4608)
SEG_OFF = (0, 512, 2048, 2560)
NSEG = len(SEG_W)
GATES_W = 1536
PAD_IN = 7168
NEG = -1e30

SHARDED = ("w_in", "mla_w_q_up", "mla_w_kv_up", "s5_w_glu", "w_branch_out", "w_out")
COL_SHARDED = ("w_in", "mla_w_q_up", "mla_w_kv_up")
SMALL = ("norm_g", "mla_q_a_norm", "mla_kv_a_norm", "mla_q_norm", "mla_k_norm", "fox_b_f", "fox_q_norm", "fox_k_norm",
         "s5_lambda_re", "s5_lambda_im", "s5_log_dt", "s5_b_re", "s5_b_im", "s5_c_re", "s5_c_im", "s5_d", "s5_b_glu")
WEIGHTS = ("norm_g", "w_in", "mla_q_a_norm", "mla_w_q_up", "mla_kv_a_norm", "mla_w_kv_up", "mla_q_norm", "mla_k_norm",
           "fox_b_f", "fox_q_norm", "fox_k_norm", "s5_lambda_re", "s5_lambda_im", "s5_log_dt", "s5_b_re", "s5_b_im",
           "s5_c_re", "s5_c_im", "s5_d", "s5_w_glu", "s5_b_glu", "w_branch_out", "w_out")


def _pick(n, cands):
    for c in cands:
        if n % c == 0:
            return c
    return n


def _vmem(mb):
    return pltpu.CompilerParams(vmem_limit_bytes=mb * 1024 * 1024)


def _dot(a, b, dims):
    return lax.dot_general(a.astype(BF16), b.astype(BF16), (dims, ((), ())), preferred_element_type=F32)


def _nn(a, b):
    return _dot(a, b, ((1,), (0,)))


def _nt(a, b):
    return _dot(a, b, ((1,), (1,)))


def _tn(a, b):
    return _dot(a, b, ((0,), (0,)))


def _rms(x, g, n=None):
    n = x.shape[-1] if n is None else n
    return x * lax.rsqrt(jnp.sum(x * x, axis=-1, keepdims=True) / n + EPS) * g


def _rope(t, c, sa, sb):
    return t * c + pltpu.roll(t, LANES - 16, 1) * sa + pltpu.roll(t, 16, 1) * sb


def _rope_t(d, c, sa, sb):
    return d * c + pltpu.roll(d * sa, 16, 1) + pltpu.roll(d * sb, LANES - 16, 1)


def _mm(a, b, mode, name, acc=None, b_cols=None, ride=None):
    if mode == "tn":
        kd, m = a.shape
    else:
        m, kd = a.shape
    b_off, b_w = b_cols if b_cols is not None else (0, b.shape[1])
    n = b.shape[0] if mode == "nt" else b_w
    tm, tn, tk = _pick(m, (1024, 512, 256, 128)), _pick(n, (1024, 512, 256, 128)), _pick(kd, (1024, 512, 256, 128))
    nk = kd // tk
    if mode == "tn":
        a_spec = pl.BlockSpec((tk, tm), lambda i, j, k: (k, i))
    else:
        a_spec = pl.BlockSpec((tm, tk), lambda i, j, k: (i, k))
    if mode == "nt":
        assert b_off % tk == 0
        b_spec = pl.BlockSpec((tn, tk), lambda i, j, k: (j, k + b_off // tk))
    else:
        assert b_off % tn == 0
        b_spec = pl.BlockSpec((tk, tn), lambda i, j, k: (k, j + b_off // tn))
    dims = {"nn": ((1,), (0,)), "nt": ((1,), (1,)), "tn": ((0,), (0,))}[mode]
    o_spec = pl.BlockSpec((tm, tn), lambda i, j, k: (i, j))
    has_acc = acc is not None

    def body(*refs):
        if has_acc:
            a_ref, b_ref, c_ref, o_ref = refs
        else:
            a_ref, b_ref, o_ref = refs
        k = pl.program_id(2)
        prod = _dot(a_ref[...], b_ref[...], dims)

        @pl.when(k == 0)
        def _():
            o_ref[...] = prod + c_ref[...] if has_acc else prod

        @pl.when(k > 0)
        def _():
            o_ref[...] += prod

    ins = [a, b] + ([acc] if has_acc else [])
    in_specs = [a_spec, b_spec] + ([o_spec] if has_acc else [])
    return _call_with_ride(body, name, (m // tm, n // tn, nk), ins, in_specs, [jax.ShapeDtypeStruct((m, n), F32)], [o_spec],
                           ("parallel", "parallel", "arbitrary"), 48, ride)[0]


def _stage(name, fn, n_steps, ins, outs, accs=(), scratch=(), vmem_mb=48):
    n_in, n_out, n_acc = len(ins), len(outs), len(accs)

    def body(*refs):
        in_refs = refs[:n_in]
        out_refs = refs[n_in:n_in + n_out]
        acc_refs = refs[n_in + n_out:n_in + n_out + n_acc]
        scr = refs[n_in + n_out + n_acc:]
        if n_acc:
            @pl.when(pl.program_id(0) == 0)
            def _():
                for r in acc_refs:
                    r[...] = jnp.zeros(r.shape, r.dtype)
        fn(in_refs, out_refs, acc_refs, scr)

    acc_specs = [pl.BlockSpec(a.shape, functools.partial(lambda i, nd: (0,) * nd, nd=len(a.shape))) for a in accs]
    res = pl.pallas_call(
        body, name=name, grid=(n_steps,),
        in_specs=[s for _, s in ins], out_specs=[s for _, s in outs] + acc_specs,
        out_shape=[s for s, _ in outs] + list(accs), scratch_shapes=list(scratch),
        compiler_params=pltpu.CompilerParams(dimension_semantics=("arbitrary",),
                                             vmem_limit_bytes=vmem_mb * 1024 * 1024),
    )(*[a for a, _ in ins])
    return res


def _rows(ts, w, j=0):
    return pl.BlockSpec((ts, w), lambda i: (i, j))


def _rows_rev(ts, w, n, j=0):
    return pl.BlockSpec((ts, w), lambda i: (n - 1 - i, j))


def _heads(ts, d):
    return pl.BlockSpec((HEADS, ts, d), lambda i: (0, i, 0))


def _heads_rev(ts, d, n):
    return pl.BlockSpec((HEADS, ts, d), lambda i: (0, n - 1 - i, 0))


def _full(shape):
    nd = len(shape)
    return pl.BlockSpec(tuple(shape), lambda i: (0,) * nd)


def _sds(shape, dtype=F32):
    return jax.ShapeDtypeStruct(tuple(shape), dtype)


def _norm_fwd(x, g, name):
    s = x.shape[0]
    ts = _pick(s, (256, 128))

    def fn(ins, outs, accs, scr):
        outs[0][...] = _rms(ins[0][...], ins[1][...]).astype(BF16)

    return _stage(name, fn, s // ts, [(x, _rows(ts, D_MODEL)), (g, _full(g.shape))],
                  [(_sds((s, D_MODEL), BF16), _rows(ts, D_MODEL))])[0]


def _norm_bwd(x, g, dh, dres, name):
    s = x.shape[0]
    ts = _pick(s, (256, 128))

    def fn(ins, outs, accs, scr):
        _, vjp = jax.vjp(_rms, ins[0][...], ins[1][...])
        dx, dg = vjp(ins[2][...])
        outs[0][...] = dx + ins[3][...]
        accs[0][...] += dg

    r = _stage(name, fn, s // ts,
               [(x, _rows(ts, D_MODEL)), (g, _full(g.shape)), (dh, _rows(ts, D_MODEL)), (dres, _rows(ts, D_MODEL))],
               [(_sds((s, D_MODEL)), _rows(ts, D_MODEL))], accs=[_sds((1, D_MODEL))])
    return r[0], r[1]


def _mla_q(qraw, c, sa, sb, qn):
    return _rms(_rope(qraw, c, sa, sb), qn, MLA_QK)


def _mla_prep_fwd(seg0, tabs, p, wq, wkv, name):
    s = seg0.shape[0]
    ts = _pick(s, (256, 128))

    def fn(ins, outs, accs, scr):
        blk, cos, sa, sb, qan, kvan, qn, kn, wq_r, wkv_r = ins
        b = blk[...]
        cq, ckv, kt = b[:, :256], b[:, 256:384], b[:, 384:512]
        lane = lax.broadcasted_iota(jnp.int32, kt.shape, 1)
        kpe = jnp.where(lane >= 64, kt, 0.0)
        q_raw = _nn(_rms(cq, qan[...]), wq_r[...])
        kv_raw = _nn(_rms(ckv, kvan[...]), wkv_r[...])
        c, a, bb = cos[...], sa[...], sb[...]
        for h in range(HEADS):
            outs[0][h] = _mla_q(q_raw[:, LANES * h:LANES * (h + 1)], c, a, bb, qn[...]).astype(BF16)
            outs[1][h] = _mla_q(kv_raw[:, LANES * h:LANES * (h + 1)] + kpe, c, a, bb, kn[...]).astype(BF16)
            outs[2][h] = kv_raw[:, 1024 + 64 * h:1024 + 64 * (h + 1)].astype(BF16)

    consts = [p["mla_q_a_norm"], p["mla_kv_a_norm"], p["mla_q_norm"], p["mla_k_norm"], wq, wkv]
    return _stage(name, fn, s // ts,
                  [(seg0, _rows(ts, 512))] + [(t, _rows(ts, LANES)) for t in tabs] + [(a, _full(a.shape)) for a in consts],
                  [(_sds((HEADS, s, LANES), BF16), _heads(ts, LANES)), (_sds((HEADS, s, LANES), BF16), _heads(ts, LANES)),
                   (_sds((HEADS, s, 64), BF16), _heads(ts, 64))])


def _mla_prep_bwd(seg0, tabs, p, wq, wkv, dq, dk, dv, dff, name):
    s = seg0.shape[0]
    ts = _pick(s, (256, 128))

    def fn(ins, outs, accs, scr):
        blk, cos, sa, sb, qan, kvan, qn, kn, wq_r, wkv_r, dq_r, dk_r, dv_r, dff_r = ins
        dqan, dkvan, dqn, dkn, dwq, dwkv = accs
        dqraw_s, dkvraw_s = scr
        b = blk[...]
        cq, ckv, kt = b[:, :256], b[:, 256:384], b[:, 384:512]
        lane = lax.broadcasted_iota(jnp.int32, kt.shape, 1)
        kpe = jnp.where(lane >= 64, kt, 0.0)
        cqn, vjp_cq = jax.vjp(_rms, cq, qan[...])
        ckvn, vjp_ckv = jax.vjp(_rms, ckv, kvan[...])
        q_raw = _nn(cqn, wq_r[...])
        kv_raw = _nn(ckvn, wkv_r[...])
        c, a, bb = cos[...], sa[...], sb[...]

        def head_bwd(raw, gain, d):
            t = _rope(raw, c, a, bb)
            _, vjp = jax.vjp(functools.partial(_rms, n=MLA_QK), t, gain)
            dt, dgain = vjp(d)
            return _rope_t(dt, c, a, bb), dgain

        dkpe = jnp.zeros(kt.shape, F32)
        for h in range(HEADS):
            dqh, dg = head_bwd(q_raw[:, LANES * h:LANES * (h + 1)], qn[...], dq_r[h])
            dqn[...] += dg
            dqraw_s[:, LANES * h:LANES * (h + 1)] = dqh
            dkh, dg = head_bwd(kv_raw[:, LANES * h:LANES * (h + 1)] + kpe, kn[...], dk_r[h])
            dkn[...] += dg
            dkvraw_s[:, LANES * h:LANES * (h + 1)] = dkh
            dkpe = dkpe + dkh
            dkvraw_s[:, 1024 + 64 * h:1024 + 64 * (h + 1)] = dv_r[h]
        dq_raw = dqraw_s[...]
        dkv_raw = dkvraw_s[...]
        dwq[...] += _tn(cqn, dq_raw)
        dwkv[...] += _tn(ckvn, dkv_raw)
        dcq, dg = vjp_cq(_nt(dq_raw, wq_r[...]))
        dqan[...] += dg
        dckv, dg = vjp_ckv(_nt(dkv_raw, wkv_r[...]))
        dkvan[...] += dg
        outs[0][:, 0:256] = dcq.astype(BF16)
        outs[0][:, 256:384] = dckv.astype(BF16)
        outs[0][:, 384:512] = (jnp.where(lane >= 64, dkpe, 0.0) + dff_r[...]).astype(BF16)

    consts = [p["mla_q_a_norm"], p["mla_kv_a_norm"], p["mla_q_norm"], p["mla_k_norm"], wq, wkv]
    return _stage(name, fn, s // ts,
                  [(seg0, _rows(ts, 512))] + [(t, _rows(ts, LANES)) for t in tabs] + [(a, _full(a.shape)) for a in consts]
                  + [(dq, _heads(ts, LANES)), (dk, _heads(ts, LANES)), (dv, _heads(ts, 64)), (dff, _rows(ts, LANES))],
                  [(_sds((s, 512), BF16), _rows(ts, 512))],
                  accs=[_sds((1, 256)), _sds((1, 128)), _sds((1, LANES)), _sds((1, LANES)), _sds(wq.shape), _sds(wkv.shape)],
                  scratch=[pltpu.VMEM((ts, 1024), F32), pltpu.VMEM((ts, 1536), F32)])


def _fox_prep_fwd(seg0, seg1, bf, qn, kn, name):
    s = seg0.shape[0]
    ts = _pick(s, (256, 128))
    steps = int(math.log2(ts))

    def fn(ins, outs, accs, scr):
        kt_r, x_r, bf_r, qn_r, kn_r = ins
        carry = scr[0]

        @pl.when(pl.program_id(0) == 0)
        def _():
            carry[...] = jnp.zeros(carry.shape, F32)

        x = x_r[...]
        for h in range(HEADS):
            outs[0][h] = _rms(x[:, 64 * h:64 * (h + 1)], qn_r[...]).astype(BF16)
            outs[1][h] = _rms(x[:, 512 + 64 * h:512 + 64 * (h + 1)], kn_r[...]).astype(BF16)
            outs[2][h] = x[:, 1024 + 64 * h:1024 + 64 * (h + 1)].astype(BF16)
        kt = kt_r[...]
        lane = lax.broadcasted_iota(jnp.int32, kt.shape, 1)
        row = lax.broadcasted_iota(jnp.int32, kt.shape, 0)
        cs = jnp.where(lane < HEADS, jax.nn.log_sigmoid(kt + bf_r[...]), 0.0)
        for k in range(steps):
            sh = 1 << k
            cs = cs + jnp.where(row >= sh, pltpu.roll(cs, sh, 0), 0.0)
        cs = cs + carry[0:1, :]
        outs[3][...] = cs
        outs[4][...] = cs.T[0:HEADS, :]
        carry[0:1, :] = cs[ts - 1:ts, :]

    return _stage(name, fn, s // ts,
                  [(seg0, _rows(ts, LANES, 3)), (seg1, _rows(ts, 1536)), (bf, _full(bf.shape)), (qn, _full(qn.shape)),
                   (kn, _full(kn.shape))],
                  [(_sds((HEADS, s, 64), BF16), _heads(ts, 64)), (_sds((HEADS, s, 64), BF16), _heads(ts, 64)),
                   (_sds((HEADS, s, 64), BF16), _heads(ts, 64)), (_sds((s, LANES)), _rows(ts, LANES)),
                   (_sds((HEADS, s)), pl.BlockSpec((HEADS, ts), lambda i: (0, i)))],
                  scratch=[pltpu.VMEM((SUBLANES, LANES), F32)])


def _fox_prep_bwd(seg0, seg1, bf, qn, kn, dq, dk, dv, dck, name):
    s = seg0.shape[0]
    ts = _pick(s, (256, 128))
    n = s // ts
    steps = int(math.log2(ts))

    def fn(ins, outs, accs, scr):
        kt_r, x_r, bf_r, qn_r, kn_r, dq_r, dk_r, dv_r, dck_r = ins
        dqn, dkn, dbf = accs
        carry, dbuf = scr

        @pl.when(pl.program_id(0) == 0)
        def _():
            carry[...] = jnp.zeros(carry.shape, F32)

        x = x_r[...]
        for h in range(HEADS):
            _, vjp = jax.vjp(_rms, x[:, 64 * h:64 * (h + 1)], qn_r[...])
            d, dg = vjp(dq_r[h])
            dbuf[:, 64 * h:64 * (h + 1)] = d
            dqn[...] += dg
            _, vjp = jax.vjp(_rms, x[:, 512 + 64 * h:512 + 64 * (h + 1)], kn_r[...])
            d, dg = vjp(dk_r[h])
            dbuf[:, 512 + 64 * h:512 + 64 * (h + 1)] = d
            dkn[...] += dg
            dbuf[:, 1024 + 64 * h:1024 + 64 * (h + 1)] = dv_r[h]
        outs[0][...] = dbuf[...].astype(BF16)
        dc = dck_r[...].reshape(HEADS, ts)
        dc = jnp.concatenate([dc, jnp.zeros((LANES - HEADS, ts), F32)], axis=0).T
        row = lax.broadcasted_iota(jnp.int32, dc.shape, 0)
        lane = lax.broadcasted_iota(jnp.int32, dc.shape, 1)
        for k in range(steps):
            sh = 1 << k
            dc = dc + jnp.where(row < ts - sh, pltpu.roll(dc, ts - sh, 0), 0.0)
        dc = dc + carry[0:1, :]
        carry[0:1, :] = dc[0:1, :]
        dff = jnp.where(lane < HEADS, dc * jax.nn.sigmoid(-(kt_r[...] + bf_r[...])), 0.0)
        outs[1][...] = dff
        dbf[...] += jnp.sum(dff, axis=0, keepdims=True)

    return _stage(name, fn, n,
                  [(seg0, _rows_rev(ts, LANES, n, 3)), (seg1, _rows_rev(ts, 1536, n)), (bf, _full(bf.shape)),
                   (qn, _full(qn.shape)), (kn, _full(kn.shape)), (dq, _heads_rev(ts, 64, n)), (dk, _heads_rev(ts, 64, n)),
                   (dv, _heads_rev(ts, 64, n)), (dck, pl.BlockSpec((HEADS, 1, ts), lambda i: (0, 0, n - 1 - i)))],
                  [(_sds((s, 1536), BF16), _rows_rev(ts, 1536, n)), (_sds((s, LANES)), _rows_rev(ts, LANES, n))],
                  accs=[_sds((1, 64)), _sds((1, 64)), _sds((1, LANES))],
                  scratch=[pltpu.VMEM((SUBLANES, LANES), F32), pltpu.VMEM((ts, 1536), F32)])


def _allowed(i, j, t, chunk_causal):
    qpos = i * t + lax.broadcasted_iota(jnp.int32, (t, t), 0)
    kpos = j * t + lax.broadcasted_iota(jnp.int32, (t, t), 1)
    if chunk_causal:
        return (kpos // CHUNK) <= (qpos // CHUNK)
    return kpos <= qpos


def _pick_col(c_blk, h):
    lane = lax.broadcasted_iota(jnp.int32, c_blk.shape, 1)
    return jnp.sum(jnp.where(lane == h, c_blk, 0.0), axis=1, keepdims=True)


class _Ride:
    def __init__(self, make, take):
        self.make, self.take = make, take


class _Plan:
    def __init__(self, ins, out_shapes, n_remote, n_local, copies, aliases=None):
        self.ins, self.out_shapes, self.n_remote, self.n_local = list(ins), list(out_shapes), n_remote, n_local
        self.copies, self.aliases = copies, dict(aliases or {})

    def scratch(self):
        return [pltpu.SemaphoreType.DMA((self.n_remote,)), pltpu.SemaphoreType.DMA((self.n_remote,)),
                pltpu.SemaphoreType.DMA((max(self.n_local, 1),))]

    def start(self, in_refs, out_refs, sems):
        remote, local = self.copies(in_refs, out_refs, *sems)
        for cp in local + remote:
            cp.start()

    def wait(self, in_refs, out_refs, sems):
        remote, local = self.copies(in_refs, out_refs, *sems)
        for cp in remote:
            cp.wait()
        for cp in local:
            cp.wait()


class _Off:
    def __init__(self, ref, off):
        self.ref, self.off, self.at = ref, off, self

    def __getitem__(self, k):
        return self.ref.at[k + self.off]


def _join(plans):
    ins = [a for p in plans for a in p.ins]
    outs = [o for p in plans for o in p.out_shapes]
    aliases, i0, o0 = {}, 0, 0
    for p in plans:
        aliases.update({i0 + i: o0 + o for i, o in p.aliases.items()})
        i0, o0 = i0 + len(p.ins), o0 + len(p.out_shapes)

    def copies(in_refs, out_refs, send, recv, local):
        rem, loc, i0, o0, r0, l0 = [], [], 0, 0, 0, 0
        for p in plans:
            r, l = p.copies(in_refs[i0:i0 + len(p.ins)], out_refs[o0:o0 + len(p.out_shapes)], _Off(send, r0), _Off(recv, r0),
                            _Off(local, l0))
            rem, loc = rem + r, loc + l
            i0, o0, r0, l0 = i0 + len(p.ins), o0 + len(p.out_shapes), r0 + p.n_remote, l0 + p.n_local
        return rem, loc

    return _Plan(ins, outs, sum(p.n_remote for p in plans), sum(p.n_local for p in plans), copies, aliases)


def _call_with_ride(core, name, grid, ins, in_specs, out_shape, out_specs, semantics, vmem_mb, ride, scratch=()):
    n_in, n_out, n_scr = len(ins), len(out_shape), len(scratch)
    if ride is None:
        return pl.pallas_call(
            core, name=name, grid=grid, in_specs=in_specs, out_specs=out_specs, out_shape=out_shape,
            scratch_shapes=list(scratch),
            compiler_params=pltpu.CompilerParams(dimension_semantics=semantics, vmem_limit_bytes=vmem_mb * 1024 * 1024),
        )(*ins)
    plan = ride.make()
    ci, co = len(plan.ins), len(plan.out_shapes)

    def body(*refs):
        c_in = refs[n_in:n_in + ci]
        a_out = refs[n_in + ci:n_in + ci + n_out]
        c_out = refs[n_in + ci + n_out:n_in + ci + n_out + co]
        own = refs[n_in + ci + n_out + co:n_in + ci + n_out + co + n_scr]
        sems = refs[n_in + ci + n_out + co + n_scr:]
        ids = [pl.program_id(d) for d in range(len(grid))]
        first = functools.reduce(jnp.logical_and, [i == 0 for i in ids])
        last = functools.reduce(jnp.logical_and, [i == g - 1 for i, g in zip(ids, grid)])

        @pl.when(first)
        def _():
            plan.start(c_in, c_out, sems)

        core(*refs[:n_in], *a_out, *own)

        @pl.when(last)
        def _():
            plan.wait(c_in, c_out, sems)

    res = pl.pallas_call(
        body, name=name, grid=grid, in_specs=list(in_specs) + [ANY] * ci, out_specs=list(out_specs) + [ANY] * co,
        out_shape=list(out_shape) + plan.out_shapes, scratch_shapes=list(scratch) + plan.scratch(),
        input_output_aliases={n_in + i: n_out + o for i, o in plan.aliases.items()},
        compiler_params=pltpu.CompilerParams(dimension_semantics=("arbitrary",) * len(grid),
                                             vmem_limit_bytes=vmem_mb * 1024 * 1024),
    )(*ins, *plan.ins)
    ride.take(res[n_out:])
    return res[:n_out]


def _attn_fwd(q, k, v, scale, chunk_causal, name, c=None, ct=None, hps=8, ride=None):
    _, s, dk = q.shape
    dv = v.shape[2]
    t = _pick(s, (256, 128))
    bias = c is not None

    def body(*refs):
        if bias:
            q_ref, k_ref, v_ref, c_ref, ct_ref, o_ref, lse_ref = refs
        else:
            q_ref, k_ref, v_ref, o_ref, lse_ref = refs
        hp, i = pl.program_id(0), pl.program_id(1)
        qb = [q_ref[e] for e in range(hps)]
        cq = [_pick_col(c_ref[...], hp * hps + e) if bias else None for e in range(hps)]

        def step(j, carry, diagonal):
            off = pl.multiple_of(j * t, t)
            out = []
            for e in range(hps):
                m, l, acc = carry[e]
                sc = _nt(qb[e], k_ref[e, pl.ds(off, t), :]) * scale
                if bias:
                    sc = sc + (cq[e] - ct_ref[pl.ds(hp * hps + e, 1), pl.ds(off, t)])
                if diagonal:
                    sc = jnp.where(_allowed(i, j, t, chunk_causal), sc, NEG)
                m_new = jnp.maximum(m, jnp.max(sc, axis=1, keepdims=True))
                pr = jnp.exp(sc - m_new)
                alpha = jnp.exp(m - m_new)
                out.append((m_new, alpha * l + jnp.sum(pr, axis=1, keepdims=True),
                            alpha * acc + _nn(pr, v_ref[e, pl.ds(off, t), :])))
            return tuple(out)

        init = tuple((jnp.full((t, 1), NEG, F32), jnp.zeros((t, 1), F32), jnp.zeros((t, dv), F32)) for _ in range(hps))
        res = step(i, lax.fori_loop(0, i, functools.partial(step, diagonal=False), init), True)
        for e in range(hps):
            m, l, acc = res[e]
            o_ref[e] = acc / l
            lse_ref[e] = m + jnp.log(l)

    ins = [q, k, v] + ([c, ct] if bias else [])
    in_specs = [pl.BlockSpec((hps, t, dk), lambda h, i: (h, i, 0)), pl.BlockSpec((hps, s, dk), lambda h, i: (h, 0, 0)),
                pl.BlockSpec((hps, s, dv), lambda h, i: (h, 0, 0))]
    if bias:
        in_specs += [pl.BlockSpec((t, LANES), lambda h, i: (i, 0)), pl.BlockSpec((HEADS, s), lambda h, i: (0, 0))]
    return _call_with_ride(
        body, name, (HEADS // hps, s // t), ins, in_specs, [_sds((HEADS, s, dv)), _sds((HEADS, s, 1))],
        [pl.BlockSpec((hps, t, dv), lambda h, i: (h, i, 0)), pl.BlockSpec((hps, t, 1), lambda h, i: (h, i, 0))],
        ("parallel", "parallel"), 48, ride)


def _attn_delta(q, k, v, do, lse, scale, chunk_causal, name, c=None, ct=None, hps=4, ride=None):
    _, s, dk = q.shape
    dv = v.shape[2]
    t = _pick(s, (256, 128))
    bias = c is not None

    def body(*refs):
        if bias:
            q_ref, k_ref, v_ref, do_ref, lse_ref, c_ref, ct_ref, dl_ref = refs
        else:
            q_ref, k_ref, v_ref, do_ref, lse_ref, dl_ref = refs
        hp, i = pl.program_id(0), pl.program_id(1)
        qb, dob, lse_b = ([r[e] for e in range(hps)] for r in (q_ref, do_ref, lse_ref))
        cq = [_pick_col(c_ref[...], hp * hps + e) if bias else None for e in range(hps)]

        def step(j, acc, diagonal):
            off = pl.multiple_of(j * t, t)
            out = []
            for e in range(hps):
                sc = _nt(qb[e], k_ref[e, pl.ds(off, t), :]) * scale
                if bias:
                    sc = sc + (cq[e] - ct_ref[pl.ds(hp * hps + e, 1), pl.ds(off, t)])
                pr = jnp.exp(sc - lse_b[e])
                if diagonal:
                    pr = jnp.where(_allowed(i, j, t, chunk_causal), pr, 0.0)
                out.append(acc[e] + jnp.sum(pr * _nt(dob[e], v_ref[e, pl.ds(off, t), :]), axis=1, keepdims=True))
            return tuple(out)

        init = tuple(jnp.zeros((t, 1), F32) for _ in range(hps))
        res = step(i, lax.fori_loop(0, i, functools.partial(step, diagonal=False), init), True)
        for e in range(hps):
            dl_ref[e] = res[e]

    ins = [q, k, v, do, lse] + ([c, ct] if bias else [])
    in_specs = [pl.BlockSpec((hps, t, dk), lambda h, i: (h, i, 0)), pl.BlockSpec((hps, s, dk), lambda h, i: (h, 0, 0)),
                pl.BlockSpec((hps, s, dv), lambda h, i: (h, 0, 0)), pl.BlockSpec((hps, t, dv), lambda h, i: (h, i, 0)),
                pl.BlockSpec((hps, t, 1), lambda h, i: (h, i, 0))]
    if bias:
        in_specs += [pl.BlockSpec((t, LANES), lambda h, i: (i, 0)), pl.BlockSpec((HEADS, s), lambda h, i: (0, 0))]
    return _call_with_ride(body, name, (HEADS // hps, s // t), ins, in_specs, [_sds((HEADS, s, 1))],
                           [pl.BlockSpec((hps, t, 1), lambda h, i: (h, i, 0))], ("parallel", "parallel"), 48, ride)[0]


def _attn_bwd_rows(q, k, v, do, lse, scale, chunk_causal, name, c=None, ct=None, hps=4, ride=None):
    _, s, dk = q.shape
    dv = v.shape[2]
    t = _pick(s, (256, 128))
    n = s // t
    bias = c is not None

    def body(*refs):
        if bias:
            q_ref, k_ref, v_ref, do_ref, lse_ref, c_ref, ct_ref, dq_ref, dk_ref, dv_ref, dck_ref, p_s, dp_s = refs
        else:
            q_ref, k_ref, v_ref, do_ref, lse_ref, dq_ref, dk_ref, dv_ref, p_s, dp_s = refs
        hp, i = pl.program_id(0), pl.program_id(1)

        @pl.when(i == 0)
        def _():
            dk_ref[...] = jnp.zeros(dk_ref.shape, F32)
            dv_ref[...] = jnp.zeros(dv_ref.shape, F32)
            if bias:
                dck_ref[...] = jnp.zeros(dck_ref.shape, F32)

        qb = [q_ref[e] for e in range(hps)]
        dob = [do_ref[e] for e in range(hps)]
        lse_b = [lse_ref[e] for e in range(hps)]
        cq = [_pick_col(c_ref[...], hp * hps + e) if bias else None for e in range(hps)]

        def sweep1(j, acc, diagonal):
            off = pl.multiple_of(j * t, t)
            out = []
            for e in range(hps):
                sc = _nt(qb[e], k_ref[e, pl.ds(off, t), :]) * scale
                if bias:
                    sc = sc + (cq[e] - ct_ref[pl.ds(hp * hps + e, 1), pl.ds(off, t)])
                pr = jnp.exp(sc - lse_b[e])
                if diagonal:
                    pr = jnp.where(_allowed(i, j, t, chunk_causal), pr, 0.0)
                dp = _nt(dob[e], v_ref[e, pl.ds(off, t), :])
                p_s[e, j] = pr
                dp_s[e, j] = dp
                out.append(acc[e] + jnp.sum(pr * dp, axis=1, keepdims=True))
            return tuple(out)

        zero = tuple(jnp.zeros((t, 1), F32) for _ in range(hps))
        delta = sweep1(i, lax.fori_loop(0, i, functools.partial(sweep1, diagonal=False), zero), True)

        def sweep2(j, dq_acc):
            off = pl.multiple_of(j * t, t)
            out = []
            for e in range(hps):
                pr = p_s[e, j]
                ds = pr * (dp_s[e, j] - delta[e])
                kb = k_ref[e, pl.ds(off, t), :]
                dv_ref[e, pl.ds(off, t), :] += _tn(pr, dob[e])
                dk_ref[e, pl.ds(off, t), :] += _tn(ds, qb[e]) * scale
                if bias:
                    dck_ref[e, j] -= jnp.sum(ds, axis=0, keepdims=True)
                out.append(dq_acc[e] + _nn(ds, kb))
            return tuple(out)

        dq = lax.fori_loop(0, i + 1, sweep2, tuple(jnp.zeros((t, dk), F32) for _ in range(hps)))
        for e in range(hps):
            dq_ref[e] = dq[e] * scale

    full = lambda d: pl.BlockSpec((hps, s, d), lambda h, i: (h, 0, 0))
    blk = lambda d: pl.BlockSpec((hps, t, d), lambda h, i: (h, i, 0))
    ins = [q, k, v, do, lse] + ([c, ct] if bias else [])
    in_specs = [blk(dk), full(dk), full(dv), blk(dv), blk(1)]
    out_specs = [blk(dk), full(dk), full(dv)]
    out_shape = [_sds((HEADS, s, dk)), _sds((HEADS, s, dk)), _sds((HEADS, s, dv))]
    if bias:
        in_specs += [pl.BlockSpec((t, LANES), lambda h, i: (i, 0)), pl.BlockSpec((HEADS, s), lambda h, i: (0, 0))]
        out_specs.append(pl.BlockSpec((hps, n, 1, t), lambda h, i: (h, 0, 0, 0)))
        out_shape.append(_sds((HEADS, n, 1, t)))
    res = _call_with_ride(body, name, (HEADS // hps, n), ins, in_specs, out_shape, out_specs, ("parallel", "arbitrary"), 56,
                          ride, scratch=[pltpu.VMEM((hps, n, t, t), F32), pltpu.VMEM((hps, n, t, t), F32)])
    return (*res[:3], res[3].reshape(HEADS, 1, s)) if bias else tuple(res)


def _attn_bwd(q, k, v, do, lse, delta, scale, chunk_causal, name, c=None, ct=None, hps=8, ride=None):
    _, s, dk = q.shape
    dv = v.shape[2]
    t = _pick(s, (256, 128))
    n = s // t
    bias = c is not None

    def body(*refs):
        if bias:
            q_ref, k_ref, v_ref, do_ref, lse_ref, dl_ref, c_ref, ct_ref, dq_ref, dk_ref, dv_ref, dck_ref = refs
        else:
            q_ref, k_ref, v_ref, do_ref, lse_ref, dl_ref, dq_ref, dk_ref, dv_ref = refs
        hp, j = pl.program_id(0), pl.program_id(1)

        @pl.when(j == 0)
        def _():
            dq_ref[...] = jnp.zeros(dq_ref.shape, F32)

        kb, vb = [k_ref[e] for e in range(hps)], [v_ref[e] for e in range(hps)]
        joff = pl.multiple_of(j * t, t)
        ck = [ct_ref[pl.ds(hp * hps + e, 1), pl.ds(joff, t)] if bias else None for e in range(hps)]

        def step(i, carry, diagonal):
            off = pl.multiple_of(i * t, t)
            out = []
            for e in range(hps):
                dk_acc, dv_acc, dck_acc = carry[e]
                qb = q_ref[e, pl.ds(off, t), :]
                dob = do_ref[e, pl.ds(off, t), :]
                sc = _nt(qb, kb[e]) * scale
                if bias:
                    sc = sc + (_pick_col(c_ref[pl.ds(off, t), :], hp * hps + e) - ck[e])
                pr = jnp.exp(sc - lse_ref[e, pl.ds(off, t), :])
                if diagonal:
                    pr = jnp.where(_allowed(i, j, t, chunk_causal), pr, 0.0)
                dv_acc = dv_acc + _tn(pr, dob)
                ds = pr * (_nt(dob, vb[e]) - dl_ref[e, pl.ds(off, t), :])
                dq_ref[e, pl.ds(off, t), :] += _nn(ds, kb[e]) * scale
                dk_acc = dk_acc + _tn(ds, qb) * scale
                if bias:
                    dck_acc = dck_acc - jnp.sum(ds, axis=0, keepdims=True)
                out.append((dk_acc, dv_acc, dck_acc))
            return tuple(out)

        zero = tuple((jnp.zeros((t, dk), F32), jnp.zeros((t, dv), F32), jnp.zeros((1, t), F32)) for _ in range(hps))
        res = lax.fori_loop(j + 1, n, functools.partial(step, diagonal=False), step(j, zero, True))
        for e in range(hps):
            dk_ref[e], dv_ref[e] = res[e][0], res[e][1]
            if bias:
                dck_ref[e] = res[e][2]

    ins = [q, k, v, do, lse, delta] + ([c, ct] if bias else [])
    full = lambda d: pl.BlockSpec((hps, s, d), lambda h, j: (h, 0, 0))
    blk = lambda d: pl.BlockSpec((hps, t, d), lambda h, j: (h, j, 0))
    in_specs = [full(dk), blk(dk), blk(dv), full(dv), full(1), full(1)]
    if bias:
        in_specs += [pl.BlockSpec((s, LANES), lambda h, j: (0, 0)), pl.BlockSpec((HEADS, s), lambda h, j: (0, 0))]
    out_specs = [full(dk), blk(dk), blk(dv)]
    out_shape = [_sds((HEADS, s, dk)), _sds((HEADS, s, dk)), _sds((HEADS, s, dv))]
    if bias:
        out_specs.append(pl.BlockSpec((hps, 1, t), lambda h, j: (h, 0, j)))
        out_shape.append(_sds((HEADS, 1, s)))
    return _call_with_ride(body, name, (HEADS // hps, n), ins, in_specs, out_shape, out_specs, ("parallel", "arbitrary"), 56,
                           ride)


def _s5_disc(lr, li, ldt, br, bi):
    dt = jnp.exp(ldt)
    mag = jnp.exp(lr * dt)
    a_re = mag * jnp.cos(li * dt)
    a_im = mag * jnp.sin(li * dt)
    den = lr * lr + li * li
    f_re = ((a_re - 1.0) * lr + a_im * li) / den
    f_im = (a_im * lr - (a_re - 1.0) * li) / den
    return a_re, a_im, f_re * br - f_im * bi, f_re * bi + f_im * br


def _s5_params_fwd(lr16, li16, ldt16, br2, bi2, name):
    def body(a, b, c, d, e, o0, o1, o2, o3):
        r = _s5_disc(a[...], b[...], c[...], d[...], e[...])
        o0[...], o1[...], o2[...], o3[...] = r

    return pl.pallas_call(body, name=name, out_shape=[_sds((512, 64))] * 4)(lr16, li16, ldt16, br2, bi2)


def _s5_params_bwd(lr16, li16, ldt16, br2, bi2, da_re16, da_im16, dbb_re, dbb_im, name):
    def body(a, b, c, d, e, g0, g1, g2, g3, o_lr, o_li, o_dt, o_br, o_bi):
        _, vjp = jax.vjp(_s5_disc, a[...], b[...], c[...], d[...], e[...])
        dlr, dli, dldt, dbr, dbi = vjp((g0[...], g1[...], g2[...], g3[...]))
        grp = lambda z: z.reshape(S5_GROUPS, S5_GROUP, S5_STATE).sum(axis=1)
        o_lr[...] = grp(dlr)
        o_li[...] = grp(dli)
        o_dt[...] = jnp.sum(grp(dldt), axis=1, keepdims=True)
        o_br[...] = dbr
        o_bi[...] = dbi

    return pl.pallas_call(
        body, name=name, out_shape=[_sds((32, 64)), _sds((32, 64)), _sds((32, 1)), _sds((512, 64)), _sds((512, 64))],
    )(lr16, li16, ldt16, br2, bi2, da_re16, da_im16, dbb_re, dbb_im)


def _cmul(ar, ai, br, bi):
    return ar * br - ai * bi, ar * bi + ai * br


S5_SUPER = 4


def _scan_loop(bre_r, bim_r, ar, ai, ore_r, oim_r, reverse, xre_r=None, xim_r=None):
    s, lw = bre_r.shape
    nt = s // SUBLANES
    with_da = xre_r is not None
    shp = (SUBLANES, lw)
    row = lax.broadcasted_iota(jnp.int32, shp, 0)
    pows = [(ar, ai)]
    for _ in range(SUBLANES - 1):
        pows.append(_cmul(pows[-1][0], pows[-1][1], ar, ai))
    cm_r, cm_i = jnp.zeros(shp, F32), jnp.zeros(shp, F32)
    for r in range(SUBLANES):
        e = (SUBLANES - 1 - r) if reverse else r
        cm_r = jnp.where(row == r, jnp.broadcast_to(pows[e][0], shp), cm_r)
        cm_i = jnp.where(row == r, jnp.broadcast_to(pows[e][1], shp), cm_i)
    steps = [(1, pows[0]), (2, pows[1]), (4, pows[3])]

    def tile(it, carry):
        if with_da:
            c_r, c_i, acc_r, acc_i = carry
        else:
            c_r, c_i = carry
        r = (nt - 1 - it) if reverse else it
        off = pl.multiple_of(r * SUBLANES, SUBLANES)
        xr, xi = bre_r[pl.ds(off, SUBLANES), :], bim_r[pl.ds(off, SUBLANES), :]
        for sh, (pr, pi) in steps:
            if reverse:
                keep = row < SUBLANES - sh
                sr = jnp.where(keep, pltpu.roll(xr, SUBLANES - sh, 0), 0.0)
                si = jnp.where(keep, pltpu.roll(xi, SUBLANES - sh, 0), 0.0)
            else:
                keep = row >= sh
                sr = jnp.where(keep, pltpu.roll(xr, sh, 0), 0.0)
                si = jnp.where(keep, pltpu.roll(xi, sh, 0), 0.0)
            mr, mi = _cmul(pr, pi, sr, si)
            xr, xi = xr + mr, xi + mi
        mr, mi = _cmul(cm_r, cm_i, c_r, c_i)
        xr, xi = xr + mr, xi + mi
        ore_r[pl.ds(off, SUBLANES), :] = xr
        oim_r[pl.ds(off, SUBLANES), :] = xi
        edge = 0 if reverse else SUBLANES - 1
        c_r, c_i = xr[edge:edge + 1, :], xi[edge:edge + 1, :]
        if not with_da:
            return c_r, c_i
        fr, fi = xre_r[pl.ds(off, SUBLANES), :], xim_r[pl.ds(off, SUBLANES), :]
        poff = pl.multiple_of(jnp.maximum(r - 1, 0) * SUBLANES, SUBLANES)
        live = (r > 0).astype(F32)
        pr_last = xre_r[pl.ds(poff, SUBLANES), :][SUBLANES - 1:SUBLANES, :] * live
        pi_last = xim_r[pl.ds(poff, SUBLANES), :][SUBLANES - 1:SUBLANES, :] * live
        sr = jnp.where(row >= 1, pltpu.roll(fr, 1, 0), jnp.broadcast_to(pr_last, shp))
        si = jnp.where(row >= 1, pltpu.roll(fi, 1, 0), jnp.broadcast_to(pi_last, shp))
        return c_r, c_i, acc_r + xr * sr + xi * si, acc_i + xi * sr - xr * si

    z1 = jnp.zeros((1, lw), F32)
    if not with_da:
        lax.fori_loop(0, nt, tile, (z1, z1))
        return None
    _, _, acc_r, acc_i = lax.fori_loop(0, nt, tile, (z1, z1, jnp.zeros(shp, F32), jnp.zeros(shp, F32)))
    return jnp.sum(acc_r, axis=0, keepdims=True), jnp.sum(acc_i, axis=0, keepdims=True)


S5_ROWS = 512


def _group_compact(p):
    grp = lax.broadcasted_iota(jnp.int32, (LANES, S5_STATE), 0) // S5_GROUP
    out = jnp.zeros((LANES, S5_STATE), F32)
    for j in range(LANES // S5_GROUP):
        out = jnp.where(grp == j, p[:, S5_STATE * j:S5_STATE * (j + 1)], out)
    return out


def _s5_core_fwd(u, wb_re, wb_im, wc_re, wc_im, a_re, a_im, name, ride=None):
    s = u.shape[0]
    lw = S5_LANES // S5_SUPER
    rows = _pick(s, (S5_ROWS, 256, 128))

    def body(u_r, wbr, wbi, wcr, wci, are_r, aim_r, xre_r, xim_r, y_r, bre_s, bim_s):
        for r0 in range(0, s, rows):
            ub = u_r[r0:r0 + rows, :]
            bre_s[r0:r0 + rows, :] = _nn(ub, wbr[0])
            bim_s[r0:r0 + rows, :] = _nn(ub, wbi[0])
        _scan_loop(bre_s, bim_s, are_r[...], aim_r[...], xre_r, xim_r, False)
        for r0 in range(0, s, rows):
            y_r[r0:r0 + rows, :] = _nn(xre_r[r0:r0 + rows, :], wcr[0]) + _nn(xim_r[r0:r0 + rows, :], wci[0])

    nar = pl.BlockSpec((s, LANES), lambda k: (0, k))
    wide = pl.BlockSpec((s, lw), lambda k: (0, k))
    one = pl.BlockSpec((1, lw), lambda k: (0, k))
    wb = pl.BlockSpec((1, LANES, lw), lambda k: (k, 0, 0))
    wc = pl.BlockSpec((1, lw, LANES), lambda k: (k, 0, 0))
    return _call_with_ride(body, name, (S5_SUPER,), [u, wb_re, wb_im, wc_re, wc_im, a_re, a_im], [nar, wb, wb, wc, wc, one, one],
                           [_sds((s, S5_LANES)), _sds((s, S5_LANES)), _sds((s, S5_WIDTH))], [wide, wide, nar], ("parallel",), 56,
                           ride, scratch=[pltpu.VMEM((s, lw), F32), pltpu.VMEM((s, lw), F32)])


def _s5_core_bwd(dy0, u, du_a, x_re, x_im, wb_re, wb_im, wc_re, wc_im, a_re, a_im_neg, name, ride=None):
    s = u.shape[0]
    lw = S5_LANES // S5_SUPER
    rows = _pick(s, (S5_ROWS, 256, 128))

    def body(dy_r, u_r, dua_r, xre_r, xim_r, wbr, wbi, wcr, wci, are_r, aim_r, du_r, dare_r, daim_r, dbr_r, dbi_r, dcr_r, dci_r,
             dre_s, dim_s, gre_s, gim_s):
        for r0 in range(0, s, rows):
            dyb = dy_r[r0:r0 + rows, :]
            dre_s[r0:r0 + rows, :] = _nt(dyb, wcr[0])
            dim_s[r0:r0 + rows, :] = _nt(dyb, wci[0])
        dare_r[...], daim_r[...] = _scan_loop(dre_s, dim_s, are_r[...], aim_r[...], gre_s, gim_s, True, xre_r, xim_r)
        acc = [jnp.zeros((LANES, lw), F32) for _ in range(4)]
        for r0 in range(0, s, rows):
            sl = slice(r0, r0 + rows)
            gr, gi, ub, dyb = gre_s[sl, :], gim_s[sl, :], u_r[sl, :], dy_r[sl, :]
            du_r[sl, :] = dua_r[sl, :] + _nt(gr, wbr[0]) + _nt(gi, wbi[0])
            acc = [acc[0] + _tn(ub, gr), acc[1] + _tn(ub, gi), acc[2] + _tn(dyb, xre_r[sl, :]), acc[3] + _tn(dyb, xim_r[sl, :])]
        dbr_r[...], dbi_r[...], dcr_r[...], dci_r[...] = (_group_compact(a) for a in acc)

    nar = pl.BlockSpec((s, LANES), lambda k: (0, k))
    wide = pl.BlockSpec((s, lw), lambda k: (0, k))
    one = pl.BlockSpec((1, lw), lambda k: (0, k))
    wb = pl.BlockSpec((1, LANES, lw), lambda k: (k, 0, 0))
    wc = pl.BlockSpec((1, lw, LANES), lambda k: (k, 0, 0))
    blk = pl.BlockSpec((LANES, S5_STATE), lambda k: (k, 0))
    return _call_with_ride(
        body, name, (S5_SUPER,), [dy0, u, du_a, x_re, x_im, wb_re, wb_im, wc_re, wc_im, a_re, a_im_neg],
        [nar, nar, nar, wide, wide, wb, wb, wc, wc, one, one],
        [_sds((s, S5_WIDTH)), _sds((1, S5_LANES)), _sds((1, S5_LANES))] + [_sds((S5_WIDTH, S5_STATE))] * 4,
        [nar, one, one, blk, blk, blk, blk], ("parallel",), 60, ride, scratch=[pltpu.VMEM((s, lw), F32)] * 4)


def _s5_seg1(y0, u, d):
    return jax.nn.gelu(y0 + d * u)


def _s5_seg2(z, t, b):
    return z * jax.nn.sigmoid(t + b)


def _s5_post_fwd(y0, seg2, d, wglu, bglu, name):
    s = y0.shape[0]
    ts = _pick(s, (256, 128))

    def fn(ins, outs, accs, scr):
        z = _s5_seg1(ins[0][...], ins[1][...], ins[2][...])
        outs[0][...] = _s5_seg2(z, _nn(z, ins[3][...]), ins[4][...])

    return _stage(name, fn, s // ts,
                  [(y0, _rows(ts, 512)), (seg2, _rows(ts, 512)), (d, _full(d.shape)), (wglu, _full(wglu.shape)),
                   (bglu, _full(bglu.shape))], [(_sds((s, 512)), _rows(ts, 512))])[0]


def _s5_post_bwd(y0, seg2, d, wglu, bglu, dy, name):
    s = y0.shape[0]
    ts = _pick(s, (256, 128))

    def fn(ins, outs, accs, scr):
        y0_r, u_r, d_r, w_r, b_r, dy_r = ins
        z, vjp1 = jax.vjp(_s5_seg1, y0_r[...], u_r[...], d_r[...])
        t = _nn(z, w_r[...])
        _, vjp2 = jax.vjp(_s5_seg2, z, t, b_r[...])
        dz, dt, db = vjp2(dy_r[...])
        accs[0][...] += _tn(z, dt)
        accs[1][...] += db
        dy0, du, dd = vjp1(dz + _nt(dt, w_r[...]))
        accs[2][...] += dd
        outs[0][...] = dy0
        outs[1][...] = du

    return _stage(name, fn, s // ts,
                  [(y0, _rows(ts, 512)), (seg2, _rows(ts, 512)), (d, _full(d.shape)), (wglu, _full(wglu.shape)),
                   (bglu, _full(bglu.shape)), (dy, _rows(ts, 512))],
                  [(_sds((s, 512)), _rows(ts, 512)), (_sds((s, 512)), _rows(ts, 512))],
                  accs=[_sds((512, 512)), _sds((1, 512)), _sds((1, 512))])


def _gate_a(y, g):
    return y * jax.nn.silu(g)


def _gate_m(o0, o1, o2, m0, m1, m2):
    return jax.nn.sigmoid(m0) * o0 + jax.nn.sigmoid(m1) * o1 + jax.nn.sigmoid(m2) * o2


def _assemble(ybuf, o_mla, o_fox, y_s5):
    for h in range(HEADS):
        ybuf[:, 64 * h:64 * (h + 1)] = o_mla[h]
        ybuf[:, 512 + 64 * h:512 + 64 * (h + 1)] = o_fox[h]
    ybuf[:, 1024:1536] = y_s5[...]


def _gate_fwd(o_mla, o_fox, y_s5, seg3, x, wo, wout, name):
    s = x.shape[0]
    ts = _pick(s, (256, 128))

    def fn(ins, outs, accs, scr):
        om, of, ys, gm_r, x_r, wo_r, wout_r = ins
        ybuf = scr[0]
        _assemble(ybuf, om, of, ys)
        a = _gate_a(ybuf[...], gm_r[:, :GATES_W])
        o = [_nn(a[:, 512 * b:512 * (b + 1)], wo_r[512 * b:512 * (b + 1), :]) for b in range(3)]
        merged = _gate_m(o[0], o[1], o[2], *[gm_r[:, GATES_W + 1024 * b:GATES_W + 1024 * (b + 1)] for b in range(3)])
        outs[0][...] = x_r[...] + _nn(merged, wout_r[...])

    return _stage(name, fn, s // ts,
                  [(o_mla, _heads(ts, 64)), (o_fox, _heads(ts, 64)), (y_s5, _rows(ts, 512)), (seg3, _rows(ts, SEG_W[3])),
                   (x, _rows(ts, D_MODEL)), (wo, _full(wo.shape)), (wout, _full(wout.shape))],
                  [(_sds((s, D_MODEL)), _rows(ts, D_MODEL))], scratch=[pltpu.VMEM((ts, 1536), F32)])[0]


def _gate_bwd(o_mla, o_fox, y_s5, seg3, wo, wout, dout, name):
    s = dout.shape[0]
    ts = _pick(s, (128,))

    def fn(ins, outs, accs, scr):
        om, of, ys, gm_r, wo_r, wout_r, dout_r = ins
        do_mla, do_fox, dys, dgm_r = outs
        dwo, dwout = accs
        ybuf, dabuf = scr
        _assemble(ybuf, om, of, ys)
        a, vjp_a = jax.vjp(_gate_a, ybuf[...], gm_r[:, :GATES_W])
        o = [_nn(a[:, 512 * b:512 * (b + 1)], wo_r[512 * b:512 * (b + 1), :]) for b in range(3)]
        ms = [gm_r[:, GATES_W + 1024 * b:GATES_W + 1024 * (b + 1)] for b in range(3)]
        merged, vjp_m = jax.vjp(_gate_m, *o, *ms)
        dout_v = dout_r[...]
        dwout[...] += _tn(merged, dout_v)
        cts = vjp_m(_nt(dout_v, wout_r[...]))
        for b in range(3):
            dgm_r[:, GATES_W + 1024 * b:GATES_W + 1024 * (b + 1)] = cts[3 + b].astype(BF16)
            dwo[512 * b:512 * (b + 1), :] += _tn(a[:, 512 * b:512 * (b + 1)], cts[b])
            dabuf[:, 512 * b:512 * (b + 1)] = _nt(cts[b], wo_r[512 * b:512 * (b + 1), :])
        dy, dg = vjp_a(dabuf[...])
        dgm_r[:, :GATES_W] = dg.astype(BF16)
        dys[...] = dy[:, 1024:1536]
        for h in range(HEADS):
            do_mla[h] = dy[:, 64 * h:64 * (h + 1)]
            do_fox[h] = dy[:, 512 + 64 * h:512 + 64 * (h + 1)]

    return _stage(name, fn, s // ts,
                  [(o_mla, _heads(ts, 64)), (o_fox, _heads(ts, 64)), (y_s5, _rows(ts, 512)), (seg3, _rows(ts, SEG_W[3])),
                   (wo, _full(wo.shape)), (wout, _full(wout.shape)), (dout, _rows(ts, D_MODEL))],
                  [(_sds((HEADS, s, 64)), _heads(ts, 64)), (_sds((HEADS, s, 64)), _heads(ts, 64)), (_sds((s, 512)), _rows(ts, 512)),
                   (_sds((s, SEG_W[3]), BF16), _rows(ts, SEG_W[3]))],
                  accs=[_sds(wo.shape), _sds(wout.shape)], scratch=[pltpu.VMEM((ts, 1536), F32), pltpu.VMEM((ts, 1536), F32)],
                  vmem_mb=56)


def _loss_head(y, target, name):
    s = y.shape[0]
    ts = _pick(s, (256, 128))

    def fn(ins, outs, accs, scr):
        e = ins[0][...] - ins[1][...]
        outs[0][...] = e / D_MODEL
        accs[0][...] += 0.5 * jnp.sum(jnp.sum(e * e, axis=1, keepdims=True) / D_MODEL, axis=0, keepdims=True)

    r = _stage(name, fn, s // ts, [(y, _rows(ts, D_MODEL)), (target, _rows(ts, D_MODEL))],
               [(_sds((s, D_MODEL)), _rows(ts, D_MODEL))], accs=[_sds((1, 1))])
    return r[1], r[0]


IN_RANGES = ((0, 384, 0), (384, 416, 448), (416, 1952, 512), (1952, 1960, 384), (1960, IN_WIDTH, 2048))
SHARD_W = IN_WIDTH // N_DEV


def _win_pieces(d):
    lo, hi = SHARD_W * d, SHARD_W * (d + 1)
    out = []
    for a, b, p in IN_RANGES:
        s, e = max(a, lo), min(b, hi)
        while s < e:
            pad = p + (s - a)
            k = max(i for i in range(NSEG) if SEG_OFF[i] <= pad)
            w = min(e - s, SEG_OFF[k] + SEG_W[k] - pad)
            out.append((s - lo, w, k, pad - SEG_OFF[k]))
            s += w
    return out


def _win_pad(g, name):
    _, r, _ = g.shape
    tr = 64

    def body(g_ref, o_ref):
        o_ref[...] = jnp.zeros(o_ref.shape, o_ref.dtype)
        for d in range(N_DEV):
            for dst, w, k, src in _win_pieces(d):
                o_ref[:, SEG_OFF[k] + src:SEG_OFF[k] + src + w] = g_ref[d, :, dst:dst + w]

    return pl.pallas_call(
        body, name=name, grid=(r // tr,), in_specs=[pl.BlockSpec((N_DEV, tr, SHARD_W), lambda i: (0, i, 0))],
        out_specs=pl.BlockSpec((tr, PAD_IN), lambda i: (i, 0)), out_shape=jax.ShapeDtypeStruct((r, PAD_IN), g.dtype),
        compiler_params=pltpu.CompilerParams(dimension_semantics=("parallel",)),
    )(g)


def _win_unpad(dsegs, name):
    r = dsegs[0].shape[0]
    tr = 64

    def body(*refs):
        o_ref = refs[NSEG]
        for d in range(N_DEV):
            for dst, w, k, src in _win_pieces(d):
                o_ref[d, :, dst:dst + w] = refs[k][:, src:src + w]

    return pl.pallas_call(
        body, name=name, grid=(r // tr,), in_specs=[pl.BlockSpec((tr, SEG_W[k]), lambda i: (i, 0)) for k in range(NSEG)],
        out_specs=pl.BlockSpec((N_DEV, tr, SHARD_W), lambda i: (0, i, 0)),
        out_shape=jax.ShapeDtypeStruct((N_DEV, r, SHARD_W), dsegs[0].dtype),
        compiler_params=pltpu.CompilerParams(dimension_semantics=("parallel",)),
    )(*dsegs)


def _pad_wq(w):
    w = w.reshape(MLA_Q_RANK, HEADS, MLA_QK)
    return jnp.pad(w, ((0, 0), (0, 0), (0, LANES - MLA_QK))).reshape(MLA_Q_RANK, HEADS * LANES)


def _unpad_wq(d):
    return d.reshape(MLA_Q_RANK, HEADS, LANES)[:, :, :MLA_QK].reshape(MLA_Q_RANK, HEADS * MLA_QK)


def _pad_wkv(w):
    w = w.reshape(MLA_KV_RANK, HEADS, MLA_NOPE + MLA_V)
    k = jnp.pad(w[:, :, :MLA_NOPE], ((0, 0), (0, 0), (0, LANES - MLA_NOPE))).reshape(MLA_KV_RANK, HEADS * LANES)
    return jnp.concatenate([k, w[:, :, MLA_NOPE:].reshape(MLA_KV_RANK, HEADS * MLA_V)], axis=1)


def _unpad_wkv(d):
    k = d[:, :HEADS * LANES].reshape(MLA_KV_RANK, HEADS, LANES)[:, :, :MLA_NOPE]
    v = d[:, HEADS * LANES:].reshape(MLA_KV_RANK, HEADS, MLA_V)
    return jnp.concatenate([k, v], axis=2).reshape(MLA_KV_RANK, HEADS * (MLA_NOPE + MLA_V))


def _pad_lanes(v, n=LANES):
    return jnp.pad(v, (0, n - v.shape[0])).reshape(1, n)


def _super_blocks(b):
    _, r, c = b.shape
    per = S5_GROUPS // S5_SUPER
    b = b.reshape(S5_SUPER, per, r, c)
    eye = jnp.eye(per, dtype=b.dtype)
    return (b[:, :, :, None, :] * eye[None, :, None, :, None]).reshape(S5_SUPER, per * r, per * c)


def _layer_params(l, w, small):
    p = {k: small[k][l] for k in small}
    q = {}
    q["norm_g"] = p["norm_g"].reshape(1, D_MODEL)
    q["mla_q_a_norm"] = p["mla_q_a_norm"].reshape(1, 256)
    q["mla_kv_a_norm"] = p["mla_kv_a_norm"].reshape(1, 128)
    q["mla_q_norm"] = _pad_lanes(p["mla_q_norm"])
    q["mla_k_norm"] = _pad_lanes(p["mla_k_norm"])
    q["fox_b_f"] = _pad_lanes(p["fox_b_f"])
    q["fox_q_norm"] = p["fox_q_norm"].reshape(1, 64)
    q["fox_k_norm"] = p["fox_k_norm"].reshape(1, 64)
    q["s5_d"] = p["s5_d"].reshape(1, 512)
    q["s5_b_glu"] = p["s5_b_glu"].reshape(1, 512)
    rep = lambda z: jnp.repeat(z, S5_GROUP, axis=0)
    q["lr16"], q["li16"] = rep(p["s5_lambda_re"]), rep(p["s5_lambda_im"])
    q["ldt16"] = rep(jnp.broadcast_to(p["s5_log_dt"][:, None], (S5_GROUPS, S5_STATE)))
    q["br2"] = p["s5_b_re"].transpose(0, 2, 1).reshape(512, 64)
    q["bi2"] = p["s5_b_im"].transpose(0, 2, 1).reshape(512, 64)
    q["c_re"], q["c_im"] = p["s5_c_re"], p["s5_c_im"]
    q["w_in"] = _win_pad(w["w_in"], f"l{l}_w_in_pad")
    q["wq"] = _pad_wq(w["mla_w_q_up"])
    q["wkv"] = _pad_wkv(w["mla_w_kv_up"])
    q.update(_late_weights(w))
    return q


def _late_weights(w):
    return {q: w[k] for q, k in (("wglu", "s5_w_glu"), ("wo", "w_branch_out"), ("wout", "w_out")) if k in w}


def _layer_fwd(l, x, tabs, q, rides=None):
    rides = rides or {}
    n = lambda s: f"l{l}_{s}"
    sv = {"x": x}
    h = _norm_fwd(x, q["norm_g"], n("norm_fwd"))
    sv["h"] = h
    seg = [_mm(h, q["w_in"], "nn", n(f"proj{k}"), b_cols=(SEG_OFF[k], SEG_W[k])) for k in range(NSEG)]
    sv["seg"] = seg
    mq, mk, mv = _mla_prep_fwd(seg[0], tabs, q, q["wq"], q["wkv"], n("mla_prep_fwd"))
    o_mla, lse_mla = _attn_fwd(mq, mk, mv, 1.0 / math.sqrt(MLA_QK), True, n("mla_attn_fwd"), ride=rides.get("mla_attn_fwd"))
    sv.update(mq=mq, mk=mk, mv=mv, o_mla=o_mla, lse_mla=lse_mla)
    fq, fk, fv, c, ct = _fox_prep_fwd(seg[0], seg[1], q["fox_b_f"], q["fox_q_norm"], q["fox_k_norm"], n("fox_prep_fwd"))
    o_fox, lse_fox = _attn_fwd(fq, fk, fv, 1.0 / math.sqrt(FOX_DIM), False, n("fox_attn_fwd"), c=c, ct=ct,
                               ride=rides.get("fox_attn_fwd"))
    sv.update(fq=fq, fk=fk, fv=fv, c=c, ct=ct, o_fox=o_fox, lse_fox=lse_fox)
    a_re16, a_im16, bb_re, bb_im = _s5_params_fwd(q["lr16"], q["li16"], q["ldt16"], q["br2"], q["bi2"], n("s5_params_fwd"))
    a_re = a_re16.reshape(S5_GROUPS, S5_GROUP, S5_STATE)[:, 0, :].reshape(1, S5_LANES)
    a_im = a_im16.reshape(S5_GROUPS, S5_GROUP, S5_STATE)[:, 0, :].reshape(1, S5_LANES)
    wb_re = _super_blocks(bb_re.reshape(S5_GROUPS, S5_GROUP, S5_STATE)).astype(BF16)
    wb_im = _super_blocks(bb_im.reshape(S5_GROUPS, S5_GROUP, S5_STATE)).astype(BF16)
    wc_re = _super_blocks(q["c_re"].transpose(0, 2, 1)).astype(BF16)
    wc_im = _super_blocks(-q["c_im"].transpose(0, 2, 1)).astype(BF16)
    x_re, x_im, y0 = _s5_core_fwd(seg[2], wb_re, wb_im, wc_re, wc_im, a_re, a_im, n("s5_scan_fwd"), ride=rides.get("s5_scan_fwd"))
    y_s5 = _s5_post_fwd(y0, seg[2], q["s5_d"], q["wglu"], q["s5_b_glu"], n("s5_post_fwd"))
    sv.update(a_re=a_re, a_im=a_im, wb_re=wb_re, wb_im=wb_im, wc_re=wc_re, wc_im=wc_im, x_re=x_re, x_im=x_im, y0=y0, y_s5=y_s5)
    out = _gate_fwd(o_mla, o_fox, y_s5, seg[3], x, q["wo"], q["wout"], n("gate_fwd"))
    return out, sv


def _layer_bwd(l, dout, tabs, q, sv, rides=None, g=None):
    rides = rides or {}
    n = lambda s: f"l{l}_{s}"
    seg = sv["seg"]
    g = {} if g is None else g
    (do_mla, do_fox, dy_s5, dseg3, g["wo"], g["wout"]) = _gate_bwd(
        sv["o_mla"], sv["o_fox"], sv["y_s5"], seg[3], q["wo"], q["wout"], dout, n("gate_bwd"))
    dy0, du_a, g["wglu"], g["s5_b_glu"], g["s5_d"] = _s5_post_bwd(sv["y0"], seg[2], q["s5_d"], q["wglu"], q["s5_b_glu"], dy_s5,
                                                                 n("s5_post_bwd"))
    dseg2, da_re, da_im, dbb_re, dbb_im, dc_re, dc_im = _s5_core_bwd(
        dy0, seg[2], du_a, sv["x_re"], sv["x_im"], sv["wb_re"], sv["wb_im"], sv["wc_re"], sv["wc_im"], sv["a_re"], -sv["a_im"],
        n("s5_scan_bwd"), ride=rides.get("s5_scan_bwd"))
    g["s5_c_re"] = dc_re.reshape(S5_GROUPS, S5_GROUP, S5_STATE)
    g["s5_c_im"] = -dc_im.reshape(S5_GROUPS, S5_GROUP, S5_STATE)
    first = (jnp.arange(512) % S5_GROUP == 0).astype(F32)[:, None]
    da_re16 = jnp.repeat(da_re.reshape(S5_GROUPS, S5_STATE), S5_GROUP, axis=0) * first
    da_im16 = jnp.repeat(da_im.reshape(S5_GROUPS, S5_STATE), S5_GROUP, axis=0) * first
    dlr, dli, dldt, dbr2, dbi2 = _s5_params_bwd(q["lr16"], q["li16"], q["ldt16"], q["br2"], q["bi2"], da_re16, da_im16, dbb_re,
                                               dbb_im, n("s5_params_bwd"))
    g["s5_lambda_re"], g["s5_lambda_im"], g["s5_log_dt"] = dlr, dli, dldt.reshape(S5_GROUPS)
    g["s5_b_re"] = dbr2.reshape(S5_GROUPS, S5_GROUP, S5_STATE).transpose(0, 2, 1)
    g["s5_b_im"] = dbi2.reshape(S5_GROUPS, S5_GROUP, S5_STATE).transpose(0, 2, 1)
    dfq, dfk, dfv, dck = _attn_bwd_rows(sv["fq"], sv["fk"], sv["fv"], do_fox, sv["lse_fox"], 1.0 / math.sqrt(FOX_DIM), False,
                                        n("fox_attn_bwd"), sv["c"], sv["ct"], ride=rides.get("fox_attn_bwd"))
    dseg1, dff, g["fox_q_norm"], g["fox_k_norm"], dbf = _fox_prep_bwd(seg[0], seg[1], q["fox_b_f"], q["fox_q_norm"], q["fox_k_norm"],
                                                                      dfq, dfk, dfv, dck, n("fox_prep_bwd"))
    g["fox_b_f"] = dbf[0, :HEADS]
    dmq, dmk, dmv = _attn_bwd_rows(sv["mq"], sv["mk"], sv["mv"], do_mla, sv["lse_mla"], 1.0 / math.sqrt(MLA_QK), True,
                                   n("mla_attn_bwd"), ride=rides.get("mla_attn_bwd"))
    dseg0, dqan, dkvan, dqn, dkn, g["wq"], g["wkv"] = _mla_prep_bwd(seg[0], tabs, q, q["wq"], q["wkv"], dmq, dmk, dmv, dff,
                                                                   n("mla_prep_bwd"))
    g["mla_q_a_norm"], g["mla_kv_a_norm"] = dqan, dkvan
    g["mla_q_norm"], g["mla_k_norm"] = dqn[0, :MLA_QK], dkn[0, :MLA_QK]
    dsegs = [dseg0, dseg1, dseg2, dseg3]
    g["w_in"] = [_mm(sv["h"], dsegs[k], "tn", n(f"dwin{k}")) for k in range(NSEG)]
    dh = None
    for k in range(NSEG):
        dh = _mm(dsegs[k], q["w_in"], "nt", n(f"dh{k}"), acc=dh, b_cols=(SEG_OFF[k], SEG_W[k]), ride=rides.get(f"dh{k}"))
    dx, g["norm_g"] = _norm_bwd(sv["x"], q["norm_g"], dh, dout, n("norm_bwd"))
    return dx, g


MESH = pl.DeviceIdType.MESH
ANY = pl.BlockSpec(memory_space=pl.ANY)


def _all_gather(blocks, name):
    n = len(blocks)

    def body(*refs):
        x_refs, out_refs = refs[:n], refs[n:2 * n]
        send_sems, recv_sems, local_sems = refs[2 * n:]
        x, y, c = lax.axis_index("x"), lax.axis_index("y"), lax.axis_index("c")
        me, sibling = (x, y, c), (x, y, 1 - c)
        chips = [(1 - x, y), (x, 1 - y), (1 - x, 1 - y)]

        def slot(a, px, py, pc):
            return out_refs[a].at[4 * px + 2 * py + pc]

        def copy(a, k, blk, to, src=None):
            return pltpu.make_async_remote_copy(src_ref=slot(a, *blk) if src is None else src, dst_ref=slot(a, *blk),
                                                send_sem=send_sems.at[7 * a + k], recv_sem=recv_sems.at[7 * a + k],
                                                device_id=to, device_id_type=MESH)

        mine = [pltpu.make_async_copy(x_refs[a], slot(a, *me), local_sems.at[a]) for a in range(n)]
        for cp in mine:
            cp.start()
        first = []
        for j, chip in enumerate(chips):
            first += [copy(a, 1 + j, me, (*chip, c), src=x_refs[a]) for a in range(n)]
        first += [copy(a, 0, me, sibling, src=x_refs[a]) for a in range(n)]
        for cp in first:
            cp.start()
        passed = []
        for j, chip in enumerate(chips):
            for a in range(n):
                copy(a, 1 + j, (*chip, c), me).wait_recv()
                passed.append(copy(a, 4 + j, (*chip, c), sibling))
                passed[-1].start()
        for a in range(n):
            copy(a, 0, sibling, me).wait_recv()
        for j, chip in enumerate(chips):
            for a in range(n):
                copy(a, 4 + j, (*chip, 1 - c), me).wait_recv()
        for cp in first + passed:
            cp.wait_send()
        for cp in mine:
            cp.wait()

    return pl.pallas_call(
        body, name=name, out_shape=[jax.ShapeDtypeStruct((N_DEV,) + b.shape, b.dtype) for b in blocks],
        in_specs=[ANY] * n, out_specs=[ANY] * n,
        scratch_shapes=[pltpu.SemaphoreType.DMA((7 * n,)), pltpu.SemaphoreType.DMA((7 * n,)), pltpu.SemaphoreType.DMA((n,))],
    )(*blocks)


def _place():
    x, y, c = lax.axis_index("x"), lax.axis_index("y"), lax.axis_index("c")
    return x, y, c, [(1 - x, y), (x, 1 - y), (1 - x, 1 - y)]


def _remote(src, dst, send, recv, k, to):
    return pltpu.make_async_remote_copy(src_ref=src, dst_ref=dst, send_sem=send.at[k], recv_sem=recv.at[k], device_id=to,
                                        device_id_type=MESH)


def _plan_gather_ici(blocks):
    n = len(blocks)

    def copies(in_refs, out_refs, send, recv, local):
        x, y, c, chips = _place()
        mine = 4 * x + 2 * y + c
        loc = [pltpu.make_async_copy(in_refs[a], out_refs[a].at[mine], local.at[a]) for a in range(n)]
        rem = [_remote(in_refs[a], out_refs[a].at[mine], send, recv, 3 * a + j, (px, py, c))
               for j, (px, py) in enumerate(chips) for a in range(n)]
        return rem, loc

    return _Plan(blocks, [jax.ShapeDtypeStruct((N_DEV,) + b.shape, b.dtype) for b in blocks], 3 * n, n, copies)


def _plan_gather_d2d(gathered):
    n = len(gathered)

    def copies(in_refs, out_refs, send, recv, local):
        x, y, c, _ = _place()
        rem = [_remote(in_refs[a].at[2 * j + c], out_refs[a].at[2 * j + c], send, recv, 4 * a + j, (x, y, 1 - c))
               for a in range(n) for j in range(4)]
        return rem, []

    return _Plan(gathered, [jax.ShapeDtypeStruct(g.shape, g.dtype) for g in gathered], 4 * n, 0, copies,
                 aliases={a: a for a in range(n)})


def _plan_reduce_sibling(parts):
    n = len(parts)

    def copies(in_refs, out_refs, send, recv, local):
        x, y, c, _ = _place()
        rem = [_remote(in_refs[a].at[2 * j + (1 - c)], out_refs[a].at[j], send, recv, 4 * a + j, (x, y, 1 - c))
               for a in range(n) for j in range(4)]
        return rem, []

    return _Plan(parts, [jax.ShapeDtypeStruct((4,) + p.shape[1:], p.dtype) for p in parts], 4 * n, 0, copies)


def _plan_reduce_chips(sums):
    n = len(sums)

    def copies(in_refs, out_refs, send, recv, local):
        x, y, c, chips = _place()
        mine = 2 * x + y
        loc = [pltpu.make_async_copy(in_refs[a].at[mine], out_refs[a].at[mine], local.at[a]) for a in range(n)]
        rem = [_remote(in_refs[a].at[2 * px + py], out_refs[a].at[mine], send, recv, 3 * a + k, (px, py, c))
               for k, (px, py) in enumerate(chips) for a in range(n)]
        return rem, loc

    return _Plan(sums, [jax.ShapeDtypeStruct(p.shape, p.dtype) for p in sums], 3 * n, n, copies)


def _add_sibling(parts, got, name):
    _, r, cc = parts.shape
    tr = _pick(r, (512, 256, 128, 64, 32, 16))
    c = lax.axis_index("c")

    def body(c_ref, p_ref, g_ref, o_ref):
        o_ref[...] = (p_ref[...] + g_ref[...]).astype(BF16)

    return pl.pallas_call(
        body, name=name, out_shape=jax.ShapeDtypeStruct((4, r, cc), BF16),
        grid_spec=pltpu.PrefetchScalarGridSpec(
            num_scalar_prefetch=1, grid=(4, r // tr),
            in_specs=[pl.BlockSpec((1, tr, cc), lambda j, i, cr: (2 * j + cr[0], i, 0)),
                      pl.BlockSpec((1, tr, cc), lambda j, i, cr: (j, i, 0))],
            out_specs=pl.BlockSpec((1, tr, cc), lambda j, i, cr: (j, i, 0))),
        compiler_params=_vmem(48),
    )(c.reshape(1).astype(jnp.int32), parts, got)


def _sum_leading(parts, name):
    k, r, cc = parts.shape
    tr = _pick(r, (512, 256, 128, 64, 32, 16, 8))

    def body(p_ref, o_ref):
        acc = p_ref[0]
        for j in range(1, k):
            acc = acc + p_ref[j]
        o_ref[...] = acc

    return pl.pallas_call(
        body, name=name, out_shape=jax.ShapeDtypeStruct((r, cc), F32), grid=(r // tr,),
        in_specs=[pl.BlockSpec((k, tr, cc), lambda i: (0, i, 0))], out_specs=pl.BlockSpec((tr, cc), lambda i: (i, 0)),
    )(parts)


def _adamw_math(w, g, m, v):
    nm = ADAM_B1 * m + (1.0 - ADAM_B1) * g
    nv = ADAM_B2 * v + (1.0 - ADAM_B2) * jnp.square(g)
    m_hat = nm / (1.0 - ADAM_B1 ** ADAM_STEP)
    v_hat = nv / (1.0 - ADAM_B2 ** ADAM_STEP)
    return -ADAM_LR * (m_hat / (jnp.sqrt(v_hat) + ADAM_EPS) + ADAM_WD * w), nm, nv


def _adamw_sum(w, contribs, m, v, name, ride=None):
    nl = len(contribs)
    k, r, cc = contribs[0].shape
    tr = _pick(r, (256, 128, 64, 32, 16))
    nb = r // tr

    def body(w_ref, *rest):
        c_refs = rest[:nl]
        m_ref, v_ref, g_ref, d_ref, nm_ref, nv_ref = rest[nl:]
        for li in range(nl):
            @pl.when(pl.program_id(0) == li)
            def _(c_ref=c_refs[li]):
                g = c_ref[0].astype(F32)
                for j in range(1, k):
                    g = g + c_ref[j].astype(F32)
                g_ref[...] = g
                d_ref[...], nm_ref[...], nv_ref[...] = _adamw_math(w_ref[...], g, m_ref[...], v_ref[...])

    spec = pl.BlockSpec((tr, cc), lambda l, i: (l * nb + i, 0))
    cspec = pl.BlockSpec((k, tr, cc), lambda l, i: (0, i, 0))
    return _call_with_ride(body, name, (nl, nb), [w, *contribs, m, v], [spec] + [cspec] * nl + [spec, spec],
                           [jax.ShapeDtypeStruct(w.shape, F32)] * 4, [spec] * 4, ("parallel", "parallel"), 48, ride)


def _adamw_many(ws, gs, ms, vs, name):
    n = len(ws)

    def body(*refs):
        w_r, g_r, m_r, v_r = refs[:n], refs[n:2 * n], refs[2 * n:3 * n], refs[3 * n:4 * n]
        d_r, nm_r, nv_r = refs[4 * n:5 * n], refs[5 * n:6 * n], refs[6 * n:7 * n]
        for a in range(n):
            d_r[a][...], nm_r[a][...], nv_r[a][...] = _adamw_math(w_r[a][...], g_r[a][...], m_r[a][...], v_r[a][...])

    shapes = [jax.ShapeDtypeStruct(w.shape, F32) for w in ws]
    res = pl.pallas_call(body, name=name, out_shape=shapes * 3,
                         compiler_params=pltpu.CompilerParams(vmem_limit_bytes=56 * 1024 * 1024))(*ws, *gs, *ms, *vs)
    return res[:n], res[n:2 * n], res[2 * n:]


def _pack_rows(flat, lanes, row_mult):
    n = flat.shape[-1]
    rows = -(-n // lanes)
    rows = -(-rows // row_mult) * row_mult
    pad = rows * lanes - n
    if pad:
        flat = jnp.pad(flat, [(0, 0)] * (flat.ndim - 1) + [(0, pad)])
    return flat.reshape(flat.shape[:-1] + (rows, lanes))


def _rope_tables(positions):
    inv = 1.0 / (ROPE_THETA ** (jnp.arange(0, MLA_ROPE, 2, dtype=F32) / MLA_ROPE))
    ang = positions.astype(F32)[:, None] * inv
    cos, sin = jnp.cos(ang), jnp.sin(ang)
    s = positions.shape[0]
    z = lambda n: jnp.zeros((s, n), F32)
    c = jnp.concatenate([jnp.ones((s, 64), F32), cos, cos, z(32)], axis=1)
    sa = jnp.concatenate([z(64), -sin, z(48)], axis=1)
    sb = jnp.concatenate([z(80), sin, z(32)], axis=1)
    return c, sa, sb


def _full_weights(gathered, names=SHARDED):
    full = {}
    for k, g in zip(names, gathered):
        _, r, c = g.shape
        if k == "w_in":
            full[k] = g
        else:
            full[k] = g.transpose(1, 0, 2).reshape(r, N_DEV * c) if k in COL_SHARDED else g.reshape(N_DEV * r, c)
    return full


EARLY = ("s5_w_glu", "w_branch_out", "w_out")
LATE = ("w_in", "mla_w_q_up", "mla_w_kv_up")


def _owner_major(g, names, tag):
    parts = []
    for k in names:
        if k == "w_in":
            parts.append(_win_unpad(g["w_in"], f"{tag}_w_in_unpad"))
            continue
        big = {"mla_w_q_up": lambda: _unpad_wq(g["wq"]), "mla_w_kv_up": lambda: _unpad_wkv(g["wkv"]), "s5_w_glu": lambda: g["wglu"],
               "w_branch_out": lambda: g["wo"], "w_out": lambda: g["wout"]}[k]()
        r, c = big.shape
        if k in COL_SHARDED:
            parts.append(big.reshape(r, N_DEV, c // N_DEV).transpose(1, 0, 2))
        else:
            parts.append(big.reshape(N_DEV, r // N_DEV, c))
    return parts


def _device_step(x, positions, target, shards, small):
    tabs = _rope_tables(positions)
    box = {}
    q0 = _layer_params(0, _full_weights(_all_gather(shards[0][:3], "gather_weights_l0"), LATE), small)

    def arrived(o):
        box.update(w1_b=o[:5])
        q0.update(_late_weights(_full_weights(o[5:], EARLY)))

    rides = {
        "mla_attn_fwd": _Ride(lambda: _plan_gather_ici(shards[1][:1]), lambda o: box.update(ici_a=o)),
        "fox_attn_fwd": _Ride(lambda: _join([_plan_gather_ici(shards[1][1:]), _plan_gather_d2d(box["ici_a"]),
                                             _plan_gather_ici(shards[0][3:])]),
                              lambda o: box.update(ici_b=o[:5], w1_a=o[5:6], ici_0=o[6:])),
        "s5_scan_fwd": _Ride(lambda: _join([_plan_gather_d2d(box["ici_b"]), _plan_gather_d2d(box["ici_0"])]), arrived),
    }
    h, sv0 = _layer_fwd(0, x, tabs, q0, rides)
    q1 = _layer_params(1, _full_weights(list(box["w1_a"]) + list(box["w1_b"])), small)
    h, sv1 = _layer_fwd(1, h, tabs, q1)
    loss, d = _loss_head(h, target, "loss_head")
    d, g1 = _layer_bwd(1, d, tabs, q1, sv1)
    parts1 = _owner_major(g1, SHARDED, "l1")

    def chips_plan(names, parts, got, tag):
        return _plan_reduce_chips([_add_sibling(p, g, f"reduce_add_{tag}_{k}") for k, p, g in zip(names, parts, got)])

    g0 = {}
    rides = {
        "s5_scan_bwd": _Ride(lambda: _plan_reduce_sibling(parts1), lambda o: box.update(got1=o)),
        "fox_attn_bwd": _Ride(lambda: _join([chips_plan(SHARDED, parts1, box["got1"], "l1"),
                                             _plan_reduce_sibling(box.setdefault("early0", _owner_major(g0, EARLY, "l0")))]),
                              lambda o: box.update(contribs1=o[:6], got0e=o[6:])),
        "mla_attn_bwd": _Ride(lambda: chips_plan(EARLY, box["early0"], box["got0e"], "l0"), lambda o: box.update(contribs0e=o)),
        "dh1": _Ride(lambda: _plan_reduce_sibling(box.setdefault("late0", _owner_major(g0, LATE, "l0"))),
                     lambda o: box.update(got0l=o)),
        "dh3": _Ride(lambda: chips_plan(LATE, box["late0"], box["got0l"], "l0"), lambda o: box.update(contribs0l=o)),
    }
    d, _ = _layer_bwd(0, d, tabs, q0, sv0, rides, g0)
    contribs0 = list(box["contribs0l"]) + list(box["contribs0e"])
    return loss[0, 0], d, [g0, g1], [contribs0, box["contribs1"]]


def kernel(x, positions, norm_g, w_in, mla_q_a_norm, mla_w_q_up, mla_kv_a_norm, mla_w_kv_up, mla_q_norm, mla_k_norm, fox_b_f, fox_q_norm, fox_k_norm, s5_lambda_re, s5_lambda_im, s5_log_dt, s5_b_re, s5_b_im, s5_c_re, s5_c_im, s5_d, s5_w_glu, s5_b_glu, w_branch_out, w_out, loss_target, m_norm_g, m_w_in, m_mla_q_a_norm, m_mla_w_q_up, m_mla_kv_a_norm, m_mla_w_kv_up, m_mla_q_norm, m_mla_k_norm, m_fox_b_f, m_fox_q_norm, m_fox_k_norm, m_s5_lambda_re, m_s5_lambda_im, m_s5_log_dt, m_s5_b_re, m_s5_b_im, m_s5_c_re, m_s5_c_im, m_s5_d, m_s5_w_glu, m_s5_b_glu, m_w_branch_out, m_w_out, v_norm_g, v_w_in, v_mla_q_a_norm, v_mla_w_q_up, v_mla_kv_a_norm, v_mla_w_kv_up, v_mla_q_norm, v_mla_k_norm, v_fox_b_f, v_fox_q_norm, v_fox_k_norm, v_s5_lambda_re, v_s5_lambda_im, v_s5_log_dt, v_s5_b_re, v_s5_b_im, v_s5_c_re, v_s5_c_im, v_s5_d, v_s5_w_glu, v_s5_b_glu, v_w_branch_out, v_w_out):
    env = dict(locals())
    wts = {k: env[k] for k in WEIGHTS}
    mom = {k: env["m_" + k] for k in WEIGHTS}
    var = {k: env["v_" + k] for k in WEIGHTS}

    shards = [[wts[k][l].astype(BF16) for k in SHARDED] for l in range(DEPTH)]
    small = {k: wts[k] for k in SMALL}
    loss, dx, grads, contribs = _device_step(x[0], positions[0], loss_target[0], shards, small)
    loss = lax.psum(loss, ("x", "y", "c"))

    two_d = {k: (wts[k].shape[0] * wts[k].shape[1], wts[k].shape[2]) for k in SHARDED}
    sm = {k: jnp.stack([g[k] for g in grads]).reshape(wts[k].shape) for k in SMALL}
    small_block = _pack_rows(jnp.concatenate([sm[k].reshape(-1) for k in SMALL]), LANES, 256)
    box = {}
    rides = {SHARDED[0]: _Ride(lambda: _plan_gather_ici([small_block]), lambda o: box.update(ici=o)),
             SHARDED[1]: _Ride(lambda: _plan_gather_d2d(box["ici"]), lambda o: box.update(all=o))}
    grad_out, delta_out, m_out, v_out = {}, {}, {}, {}
    for i, k in enumerate(SHARDED):
        shp = wts[k].shape
        res = _adamw_sum(wts[k].reshape(two_d[k]), [contribs[l][i] for l in range(DEPTH)], mom[k].reshape(two_d[k]),
                         var[k].reshape(two_d[k]), f"adamw_{k}", ride=rides.get(k))
        grad_out[k], delta_out[k], m_out[k], v_out[k] = (z.reshape(shp) for z in res)
    g_small = _sum_leading(box["all"][0], "sum_small_grads").reshape(-1)
    off = 0
    for k in SMALL:
        cnt = int(np.prod(wts[k].shape))
        grad_out[k] = g_small[off:off + cnt].reshape(wts[k].shape)
        off += cnt
    flat2 = lambda a: a.reshape(-1, a.shape[-1])
    d_s, m_s, v_s = _adamw_many([flat2(wts[k]) for k in SMALL], [flat2(grad_out[k]) for k in SMALL],
                                [flat2(mom[k]) for k in SMALL], [flat2(var[k]) for k in SMALL], "adamw_small")
    for i, k in enumerate(SMALL):
        delta_out[k], m_out[k], v_out[k] = (z[i].reshape(wts[k].shape) for z in (d_s, m_s, v_s))

    return (loss, dx[None], *[grad_out[k] for k in WEIGHTS], *[delta_out[k] for k in WEIGHTS],
            *[m_out[k] for k in WEIGHTS], *[v_out[k] for k in WEIGHTS])
```

```python
import functools
import math

import jax
import jax.numpy as jnp
import numpy as np
from jax import lax
from jax.experimental import pallas as pl
from jax.experimental.pallas import tpu as pltpu

F32 = jnp.float32
BF16 = jnp.bfloat16

D_MODEL = 1024
DEPTH = 2
CHUNK = 64
EPS = 1e-6
HEADS = 8
MLA_NOPE, MLA_ROPE, MLA_V = 64, 32, 64
MLA_Q_RANK, MLA_KV_RANK = 256, 128
MLA_QK = MLA_NOPE + MLA_ROPE
ROPE_THETA = 10000.0
FOX_DIM = 64
S5_WIDTH, S5_GROUP, S5_GROUPS, S5_STATE = 512, 16, 32, 64
S5_LANES = S5_GROUPS * S5_STATE
IN_WIDTH = 7080
N_DEV = 8
LANES = 128
SUBLANES = 8

ADAM_LR, ADAM_B1, ADAM_B2, ADAM_EPS, ADAM_WD, ADAM_STEP = 0.001, 0.9, 0.999, 1e-08, 0.01, 10

SEG_W = (512, 1536, 512, 4608)
SEG_OFF = (0, 512, 2048, 2560)
NSEG = len(SEG_W)
GATES_W = 1536
PAD_IN = 7168
NEG = -1e30

SHARDED = ("w_in", "mla_w_q_up", "mla_w_kv_up", "s5_w_glu", "w_branch_out", "w_out")
COL_SHARDED = ("w_in", "mla_w_q_up", "mla_w_kv_up")
SMALL = ("norm_g", "mla_q_a_norm", "mla_kv_a_norm", "mla_q_norm", "mla_k_norm", "fox_b_f", "fox_q_norm", "fox_k_norm",
         "s5_lambda_re", "s5_lambda_im", "s5_log_dt", "s5_b_re", "s5_b_im", "s5_c_re", "s5_c_im", "s5_d", "s5_b_glu")
WEIGHTS = ("norm_g", "w_in", "mla_q_a_norm", "mla_w_q_up", "mla_kv_a_norm", "mla_w_kv_up", "mla_q_norm", "mla_k_norm",
           "fox_b_f", "fox_q_norm", "fox_k_norm", "s5_lambda_re", "s5_lambda_im", "s5_log_dt", "s5_b_re", "s5_b_im",
           "s5_c_re", "s5_c_im", "s5_d", "s5_w_glu", "s5_b_glu", "w_branch_out", "w_out")


def _pick(n, cands):
    for c in cands:
        if n % c == 0:
            return c
    return n


def _vmem(mb):
    return pltpu.CompilerParams(vmem_limit_bytes=mb * 1024 * 1024)


def _dot(a, b, dims):
    return lax.dot_general(a.astype(BF16), b.astype(BF16), (dims, ((), ())), preferred_element_type=F32)


def _nn(a, b):
    return _dot(a, b, ((1,), (0,)))


def _nt(a, b):
    return _dot(a, b, ((1,), (1,)))


def _tn(a, b):
    return _dot(a, b, ((0,), (0,)))


def _rms(x, g, n=None):
    n = x.shape[-1] if n is None else n
    return x * lax.rsqrt(jnp.sum(x * x, axis=-1, keepdims=True) / n + EPS) * g


def _rope(t, c, sa, sb):
    return t * c + pltpu.roll(t, LANES - 16, 1) * sa + pltpu.roll(t, 16, 1) * sb


def _rope_t(d, c, sa, sb):
    return d * c + pltpu.roll(d * sa, 16, 1) + pltpu.roll(d * sb, LANES - 16, 1)


def _mm(a, b, mode, name, acc=None, b_cols=None, ride=None):
    if mode == "tn":
        kd, m = a.shape
    else:
        m, kd = a.shape
    b_off, b_w = b_cols if b_cols is not None else (0, b.shape[1])
    n = b.shape[0] if mode == "nt" else b_w
    tm, tn, tk = _pick(m, (1024, 512, 256, 128)), _pick(n, (1024, 512, 256, 128)), _pick(kd, (1024, 512, 256, 128))
    nk = kd // tk
    if mode == "tn":
        a_spec = pl.BlockSpec((tk, tm), lambda i, j, k: (k, i))
    else:
        a_spec = pl.BlockSpec((tm, tk), lambda i, j, k: (i, k))
    if mode == "nt":
        assert b_off % tk == 0
        b_spec = pl.BlockSpec((tn, tk), lambda i, j, k: (j, k + b_off // tk))
    else:
        assert b_off % tn == 0
        b_spec = pl.BlockSpec((tk, tn), lambda i, j, k: (k, j + b_off // tn))
    dims = {"nn": ((1,), (0,)), "nt": ((1,), (1,)), "tn": ((0,), (0,))}[mode]
    o_spec = pl.BlockSpec((tm, tn), lambda i, j, k: (i, j))
    has_acc = acc is not None

    def body(*refs):
        if has_acc:
            a_ref, b_ref, c_ref, o_ref = refs
        else:
            a_ref, b_ref, o_ref = refs
        k = pl.program_id(2)
        prod = _dot(a_ref[...], b_ref[...], dims)

        @pl.when(k == 0)
        def _():
            o_ref[...] = prod + c_ref[...] if has_acc else prod

        @pl.when(k > 0)
        def _():
            o_ref[...] += prod

    ins = [a, b] + ([acc] if has_acc else [])
    in_specs = [a_spec, b_spec] + ([o_spec] if has_acc else [])
    return _call_with_ride(body, name, (m // tm, n // tn, nk), ins, in_specs, [jax.ShapeDtypeStruct((m, n), F32)], [o_spec],
                           ("parallel", "parallel", "arbitrary"), 48, ride)[0]


def _stage(name, fn, n_steps, ins, outs, accs=(), scratch=(), vmem_mb=48):
    n_in, n_out, n_acc = len(ins), len(outs), len(accs)

    def body(*refs):
        in_refs = refs[:n_in]
        out_refs = refs[n_in:n_in + n_out]
        acc_refs = refs[n_in + n_out:n_in + n_out + n_acc]
        scr = refs[n_in + n_out + n_acc:]
        if n_acc:
            @pl.when(pl.program_id(0) == 0)
            def _():
                for r in acc_refs:
                    r[...] = jnp.zeros(r.shape, r.dtype)
        fn(in_refs, out_refs, acc_refs, scr)

    acc_specs = [pl.BlockSpec(a.shape, functools.partial(lambda i, nd: (0,) * nd, nd=len(a.shape))) for a in accs]
    res = pl.pallas_call(
        body, name=name, grid=(n_steps,),
        in_specs=[s for _, s in ins], out_specs=[s for _, s in outs] + acc_specs,
        out_shape=[s for s, _ in outs] + list(accs), scratch_shapes=list(scratch),
        compiler_params=pltpu.CompilerParams(dimension_semantics=("arbitrary",),
                                             vmem_limit_bytes=vmem_mb * 1024 * 1024),
    )(*[a for a, _ in ins])
    return res


def _rows(ts, w, j=0):
    return pl.BlockSpec((ts, w), lambda i: (i, j))


def _rows_rev(ts, w, n, j=0):
    return pl.BlockSpec((ts, w), lambda i: (n - 1 - i, j))


def _heads(ts, d):
    return pl.BlockSpec((HEADS, ts, d), lambda i: (0, i, 0))


def _heads_rev(ts, d, n):
    return pl.BlockSpec((HEADS, ts, d), lambda i: (0, n - 1 - i, 0))


def _full(shape):
    nd = len(shape)
    return pl.BlockSpec(tuple(shape), lambda i: (0,) * nd)


def _sds(shape, dtype=F32):
    return jax.ShapeDtypeStruct(tuple(shape), dtype)


def _norm_fwd(x, g, name):
    s = x.shape[0]
    ts = _pick(s, (256, 128))

    def fn(ins, outs, accs, scr):
        outs[0][...] = _rms(ins[0][...], ins[1][...]).astype(BF16)

    return _stage(name, fn, s // ts, [(x, _rows(ts, D_MODEL)), (g, _full(g.shape))],
                  [(_sds((s, D_MODEL), BF16), _rows(ts, D_MODEL))])[0]


def _norm_bwd(x, g, dh, dres, name):
    s = x.shape[0]
    ts = _pick(s, (256, 128))

    def fn(ins, outs, accs, scr):
        _, vjp = jax.vjp(_rms, ins[0][...], ins[1][...])
        dx, dg = vjp(ins[2][...])
        outs[0][...] = dx + ins[3][...]
        accs[0][...] += dg

    r = _stage(name, fn, s // ts,
               [(x, _rows(ts, D_MODEL)), (g, _full(g.shape)), (dh, _rows(ts, D_MODEL)), (dres, _rows(ts, D_MODEL))],
               [(_sds((s, D_MODEL)), _rows(ts, D_MODEL))], accs=[_sds((1, D_MODEL))])
    return r[0], r[1]


def _mla_q(qraw, c, sa, sb, qn):
    return _rms(_rope(qraw, c, sa, sb), qn, MLA_QK)


def _mla_prep_fwd(seg0, tabs, p, wq, wkv, name):
    s = seg0.shape[0]
    ts = _pick(s, (256, 128))

    def fn(ins, outs, accs, scr):
        blk, cos, sa, sb, qan, kvan, qn, kn, wq_r, wkv_r = ins
        b = blk[...]
        cq, ckv, kt = b[:, :256], b[:, 256:384], b[:, 384:512]
        lane = lax.broadcasted_iota(jnp.int32, kt.shape, 1)
        kpe = jnp.where(lane >= 64, kt, 0.0)
        q_raw = _nn(_rms(cq, qan[...]), wq_r[...])
        kv_raw = _nn(_rms(ckv, kvan[...]), wkv_r[...])
        c, a, bb = cos[...], sa[...], sb[...]
        for h in range(HEADS):
            outs[0][h] = _mla_q(q_raw[:, LANES * h:LANES * (h + 1)], c, a, bb, qn[...]).astype(BF16)
            outs[1][h] = _mla_q(kv_raw[:, LANES * h:LANES * (h + 1)] + kpe, c, a, bb, kn[...]).astype(BF16)
            outs[2][h] = kv_raw[:, 1024 + 64 * h:1024 + 64 * (h + 1)].astype(BF16)

    consts = [p["mla_q_a_norm"], p["mla_kv_a_norm"], p["mla_q_norm"], p["mla_k_norm"], wq, wkv]
    return _stage(name, fn, s // ts,
                  [(seg0, _rows(ts, 512))] + [(t, _rows(ts, LANES)) for t in tabs] + [(a, _full(a.shape)) for a in consts],
                  [(_sds((HEADS, s, LANES), BF16), _heads(ts, LANES)), (_sds((HEADS, s, LANES), BF16), _heads(ts, LANES)),
                   (_sds((HEADS, s, 64), BF16), _heads(ts, 64))])


def _mla_prep_bwd(seg0, tabs, p, wq, wkv, dq, dk, dv, dff, name):
    s = seg0.shape[0]
    ts = _pick(s, (256, 128))

    def fn(ins, outs, accs, scr):
        blk, cos, sa, sb, qan, kvan, qn, kn, wq_r, wkv_r, dq_r, dk_r, dv_r, dff_r = ins
        dqan, dkvan, dqn, dkn, dwq, dwkv = accs
        dqraw_s, dkvraw_s = scr
        b = blk[...]
        cq, ckv, kt = b[:, :256], b[:, 256:384], b[:, 384:512]
        lane = lax.broadcasted_iota(jnp.int32, kt.shape, 1)
        kpe = jnp.where(lane >= 64, kt, 0.0)
        cqn, vjp_cq = jax.vjp(_rms, cq, qan[...])
        ckvn, vjp_ckv = jax.vjp(_rms, ckv, kvan[...])
        q_raw = _nn(cqn, wq_r[...])
        kv_raw = _nn(ckvn, wkv_r[...])
        c, a, bb = cos[...], sa[...], sb[...]

        def head_bwd(raw, gain, d):
            t = _rope(raw, c, a, bb)
            _, vjp = jax.vjp(functools.partial(_rms, n=MLA_QK), t, gain)
            dt, dgain = vjp(d)
            return _rope_t(dt, c, a, bb), dgain

        dkpe = jnp.zeros(kt.shape, F32)
        for h in range(HEADS):
            dqh, dg = head_bwd(q_raw[:, LANES * h:LANES * (h + 1)], qn[...], dq_r[h])
            dqn[...] += dg
            dqraw_s[:, LANES * h:LANES * (h + 1)] = dqh
            dkh, dg = head_bwd(kv_raw[:, LANES * h:LANES * (h + 1)] + kpe, kn[...], dk_r[h])
            dkn[...] += dg
            dkvraw_s[:, LANES * h:LANES * (h + 1)] = dkh
            dkpe = dkpe + dkh
            dkvraw_s[:, 1024 + 64 * h:1024 + 64 * (h + 1)] = dv_r[h]
        dq_raw = dqraw_s[...]
        dkv_raw = dkvraw_s[...]
        dwq[...] += _tn(cqn, dq_raw)
        dwkv[...] += _tn(ckvn, dkv_raw)
        dcq, dg = vjp_cq(_nt(dq_raw, wq_r[...]))
        dqan[...] += dg
        dckv, dg = vjp_ckv(_nt(dkv_raw, wkv_r[...]))
        dkvan[...] += dg
        outs[0][:, 0:256] = dcq.astype(BF16)
        outs[0][:, 256:384] = dckv.astype(BF16)
        outs[0][:, 384:512] = (jnp.where(lane >= 64, dkpe, 0.0) + dff_r[...]).astype(BF16)

    consts = [p["mla_q_a_norm"], p["mla_kv_a_norm"], p["mla_q_norm"], p["mla_k_norm"], wq, wkv]
    return _stage(name, fn, s // ts,
                  [(seg0, _rows(ts, 512))] + [(t, _rows(ts, LANES)) for t in tabs] + [(a, _full(a.shape)) for a in consts]
                  + [(dq, _heads(ts, LANES)), (dk, _heads(ts, LANES)), (dv, _heads(ts, 64)), (dff, _rows(ts, LANES))],
                  [(_sds((s, 512), BF16), _rows(ts, 512))],
                  accs=[_sds((1, 256)), _sds((1, 128)), _sds((1, LANES)), _sds((1, LANES)), _sds(wq.shape), _sds(wkv.shape)],
                  scratch=[pltpu.VMEM((ts, 1024), F32), pltpu.VMEM((ts, 1536), F32)])


def _fox_prep_fwd(seg0, seg1, bf, qn, kn, name):
    s = seg0.shape[0]
    ts = _pick(s, (256, 128))
    steps = int(math.log2(ts))

    def fn(ins, outs, accs, scr):
        kt_r, x_r, bf_r, qn_r, kn_r = ins
        carry = scr[0]

        @pl.when(pl.program_id(0) == 0)
        def _():
            carry[...] = jnp.zeros(carry.shape, F32)

        x = x_r[...]
        for h in range(HEADS):
            outs[0][h] = _rms(x[:, 64 * h:64 * (h + 1)], qn_r[...]).astype(BF16)
            outs[1][h] = _rms(x[:, 512 + 64 * h:512 + 64 * (h + 1)], kn_r[...]).astype(BF16)
            outs[2][h] = x[:, 1024 + 64 * h:1024 + 64 * (h + 1)].astype(BF16)
        kt = kt_r[...]
        lane = lax.broadcasted_iota(jnp.int32, kt.shape, 1)
        row = lax.broadcasted_iota(jnp.int32, kt.shape, 0)
        cs = jnp.where(lane < HEADS, jax.nn.log_sigmoid(kt + bf_r[...]), 0.0)
        for k in range(steps):
            sh = 1 << k
            cs = cs + jnp.where(row >= sh, pltpu.roll(cs, sh, 0), 0.0)
        cs = cs + carry[0:1, :]
        outs[3][...] = cs
        outs[4][...] = cs.T[0:HEADS, :]
        carry[0:1, :] = cs[ts - 1:ts, :]

    return _stage(name, fn, s // ts,
                  [(seg0, _rows(ts, LANES, 3)), (seg1, _rows(ts, 1536)), (bf, _full(bf.shape)), (qn, _full(qn.shape)),
                   (kn, _full(kn.shape))],
                  [(_sds((HEADS, s, 64), BF16), _heads(ts, 64)), (_sds((HEADS, s, 64), BF16), _heads(ts, 64)),
                   (_sds((HEADS, s, 64), BF16), _heads(ts, 64)), (_sds((s, LANES)), _rows(ts, LANES)),
                   (_sds((HEADS, s)), pl.BlockSpec((HEADS, ts), lambda i: (0, i)))],
                  scratch=[pltpu.VMEM((SUBLANES, LANES), F32)])


def _fox_prep_bwd(seg0, seg1, bf, qn, kn, dq, dk, dv, dck, name):
    s = seg0.shape[0]
    ts = _pick(s, (256, 128))
    n = s // ts
    steps = int(math.log2(ts))

    def fn(ins, outs, accs, scr):
        kt_r, x_r, bf_r, qn_r, kn_r, dq_r, dk_r, dv_r, dck_r = ins
        dqn, dkn, dbf = accs
        carry, dbuf = scr

        @pl.when(pl.program_id(0) == 0)
        def _():
            carry[...] = jnp.zeros(carry.shape, F32)

        x = x_r[...]
        for h in range(HEADS):
            _, vjp = jax.vjp(_rms, x[:, 64 * h:64 * (h + 1)], qn_r[...])
            d, dg = vjp(dq_r[h])
            dbuf[:, 64 * h:64 * (h + 1)] = d
            dqn[...] += dg
            _, vjp = jax.vjp(_rms, x[:, 512 + 64 * h:512 + 64 * (h + 1)], kn_r[...])
            d, dg = vjp(dk_r[h])
            dbuf[:, 512 + 64 * h:512 + 64 * (h + 1)] = d
            dkn[...] += dg
            dbuf[:, 1024 + 64 * h:1024 + 64 * (h + 1)] = dv_r[h]
        outs[0][...] = dbuf[...].astype(BF16)
        dc = dck_r[...].reshape(HEADS, ts)
        dc = jnp.concatenate([dc, jnp.zeros((LANES - HEADS, ts), F32)], axis=0).T
        row = lax.broadcasted_iota(jnp.int32, dc.shape, 0)
        lane = lax.broadcasted_iota(jnp.int32, dc.shape, 1)
        for k in range(steps):
            sh = 1 << k
            dc = dc + jnp.where(row < ts - sh, pltpu.roll(dc, ts - sh, 0), 0.0)
        dc = dc + carry[0:1, :]
        carry[0:1, :] = dc[0:1, :]
        dff = jnp.where(lane < HEADS, dc * jax.nn.sigmoid(-(kt_r[...] + bf_r[...])), 0.0)
        outs[1][...] = dff
        dbf[...] += jnp.sum(dff, axis=0, keepdims=True)

    return _stage(name, fn, n,
                  [(seg0, _rows_rev(ts, LANES, n, 3)), (seg1, _rows_rev(ts, 1536, n)), (bf, _full(bf.shape)),
                   (qn, _full(qn.shape)), (kn, _full(kn.shape)), (dq, _heads_rev(ts, 64, n)), (dk, _heads_rev(ts, 64, n)),
                   (dv, _heads_rev(ts, 64, n)), (dck, pl.BlockSpec((HEADS, 1, ts), lambda i: (0, 0, n - 1 - i)))],
                  [(_sds((s, 1536), BF16), _rows_rev(ts, 1536, n)), (_sds((s, LANES)), _rows_rev(ts, LANES, n))],
                  accs=[_sds((1, 64)), _sds((1, 64)), _sds((1, LANES))],
                  scratch=[pltpu.VMEM((SUBLANES, LANES), F32), pltpu.VMEM((ts, 1536), F32)])


def _allowed(i, j, t, chunk_causal):
    qpos = i * t + lax.broadcasted_iota(jnp.int32, (t, t), 0)
    kpos = j * t + lax.broadcasted_iota(jnp.int32, (t, t), 1)
    if chunk_causal:
        return (kpos // CHUNK) <= (qpos // CHUNK)
    return kpos <= qpos


def _pick_col(c_blk, h):
    lane = lax.broadcasted_iota(jnp.int32, c_blk.shape, 1)
    return jnp.sum(jnp.where(lane == h, c_blk, 0.0), axis=1, keepdims=True)


class _Ride:
    def __init__(self, make, take):
        self.make, self.take = make, take


class _Plan:
    def __init__(self, ins, out_shapes, n_remote, n_local, copies, aliases=None):
        self.ins, self.out_shapes, self.n_remote, self.n_local = list(ins), list(out_shapes), n_remote, n_local
        self.copies, self.aliases = copies, dict(aliases or {})

    def scratch(self):
        return [pltpu.SemaphoreType.DMA((self.n_remote,)), pltpu.SemaphoreType.DMA((self.n_remote,)),
                pltpu.SemaphoreType.DMA((max(self.n_local, 1),))]

    def start(self, in_refs, out_refs, sems):
        remote, local = self.copies(in_refs, out_refs, *sems)
        for cp in local + remote:
            cp.start()

    def wait(self, in_refs, out_refs, sems):
        remote, local = self.copies(in_refs, out_refs, *sems)
        for cp in remote:
            cp.wait()
        for cp in local:
            cp.wait()


class _Off:
    def __init__(self, ref, off):
        self.ref, self.off, self.at = ref, off, self

    def __getitem__(self, k):
        return self.ref.at[k + self.off]


def _join(plans):
    ins = [a for p in plans for a in p.ins]
    outs = [o for p in plans for o in p.out_shapes]
    aliases, i0, o0 = {}, 0, 0
    for p in plans:
        aliases.update({i0 + i: o0 + o for i, o in p.aliases.items()})
        i0, o0 = i0 + len(p.ins), o0 + len(p.out_shapes)

    def copies(in_refs, out_refs, send, recv, local):
        rem, loc, i0, o0, r0, l0 = [], [], 0, 0, 0, 0
        for p in plans:
            r, l = p.copies(in_refs[i0:i0 + len(p.ins)], out_refs[o0:o0 + len(p.out_shapes)], _Off(send, r0), _Off(recv, r0),
                            _Off(local, l0))
            rem, loc = rem + r, loc + l
            i0, o0, r0, l0 = i0 + len(p.ins), o0 + len(p.out_shapes), r0 + p.n_remote, l0 + p.n_local
        return rem, loc

    return _Plan(ins, outs, sum(p.n_remote for p in plans), sum(p.n_local for p in plans), copies, aliases)


def _call_with_ride(core, name, grid, ins, in_specs, out_shape, out_specs, semantics, vmem_mb, ride, scratch=()):
    n_in, n_out, n_scr = len(ins), len(out_shape), len(scratch)
    if ride is None:
        return pl.pallas_call(
            core, name=name, grid=grid, in_specs=in_specs, out_specs=out_specs, out_shape=out_shape,
            scratch_shapes=list(scratch),
            compiler_params=pltpu.CompilerParams(dimension_semantics=semantics, vmem_limit_bytes=vmem_mb * 1024 * 1024),
        )(*ins)
    plan = ride.make()
    ci, co = len(plan.ins), len(plan.out_shapes)

    def body(*refs):
        c_in = refs[n_in:n_in + ci]
        a_out = refs[n_in + ci:n_in + ci + n_out]
        c_out = refs[n_in + ci + n_out:n_in + ci + n_out + co]
        own = refs[n_in + ci + n_out + co:n_in + ci + n_out + co + n_scr]
        sems = refs[n_in + ci + n_out + co + n_scr:]
        ids = [pl.program_id(d) for d in range(len(grid))]
        first = functools.reduce(jnp.logical_and, [i == 0 for i in ids])
        last = functools.reduce(jnp.logical_and, [i == g - 1 for i, g in zip(ids, grid)])

        @pl.when(first)
        def _():
            plan.start(c_in, c_out, sems)

        core(*refs[:n_in], *a_out, *own)

        @pl.when(last)
        def _():
            plan.wait(c_in, c_out, sems)

    res = pl.pallas_call(
        body, name=name, grid=grid, in_specs=list(in_specs) + [ANY] * ci, out_specs=list(out_specs) + [ANY] * co,
        out_shape=list(out_shape) + plan.out_shapes, scratch_shapes=list(scratch) + plan.scratch(),
        input_output_aliases={n_in + i: n_out + o for i, o in plan.aliases.items()},
        compiler_params=pltpu.CompilerParams(dimension_semantics=("arbitrary",) * len(grid),
                                             vmem_limit_bytes=vmem_mb * 1024 * 1024),
    )(*ins, *plan.ins)
    ride.take(res[n_out:])
    return res[:n_out]


def _attn_fwd(q, k, v, scale, chunk_causal, name, c=None, ct=None, hps=8, ride=None):
    _, s, dk = q.shape
    dv = v.shape[2]
    t = _pick(s, (256, 128))
    bias = c is not None

    def body(*refs):
        if bias:
            q_ref, k_ref, v_ref, c_ref, ct_ref, o_ref, lse_ref = refs
        else:
            q_ref, k_ref, v_ref, o_ref, lse_ref = refs
        hp, i = pl.program_id(0), pl.program_id(1)
        qb = [q_ref[e] for e in range(hps)]
        cq = [_pick_col(c_ref[...], hp * hps + e) if bias else None for e in range(hps)]

        def step(j, carry, diagonal):
            off = pl.multiple_of(j * t, t)
            out = []
            for e in range(hps):
                m, l, acc = carry[e]
                sc = _nt(qb[e], k_ref[e, pl.ds(off, t), :]) * scale
                if bias:
                    sc = sc + (cq[e] - ct_ref[pl.ds(hp * hps + e, 1), pl.ds(off, t)])
                if diagonal:
                    sc = jnp.where(_allowed(i, j, t, chunk_causal), sc, NEG)
                m_new = jnp.maximum(m, jnp.max(sc, axis=1, keepdims=True))
                pr = jnp.exp(sc - m_new)
                alpha = jnp.exp(m - m_new)
                out.append((m_new, alpha * l + jnp.sum(pr, axis=1, keepdims=True),
                            alpha * acc + _nn(pr, v_ref[e, pl.ds(off, t), :])))
            return tuple(out)

        init = tuple((jnp.full((t, 1), NEG, F32), jnp.zeros((t, 1), F32), jnp.zeros((t, dv), F32)) for _ in range(hps))
        res = step(i, lax.fori_loop(0, i, functools.partial(step, diagonal=False), init), True)
        for e in range(hps):
            m, l, acc = res[e]
            o_ref[e] = acc / l
            lse_ref[e] = m + jnp.log(l)

    ins = [q, k, v] + ([c, ct] if bias else [])
    in_specs = [pl.BlockSpec((hps, t, dk), lambda h, i: (h, i, 0)), pl.BlockSpec((hps, s, dk), lambda h, i: (h, 0, 0)),
                pl.BlockSpec((hps, s, dv), lambda h, i: (h, 0, 0))]
    if bias:
        in_specs += [pl.BlockSpec((t, LANES), lambda h, i: (i, 0)), pl.BlockSpec((HEADS, s), lambda h, i: (0, 0))]
    return _call_with_ride(
        body, name, (HEADS // hps, s // t), ins, in_specs, [_sds((HEADS, s, dv)), _sds((HEADS, s, 1))],
        [pl.BlockSpec((hps, t, dv), lambda h, i: (h, i, 0)), pl.BlockSpec((hps, t, 1), lambda h, i: (h, i, 0))],
        ("parallel", "parallel"), 48, ride)


def _attn_bwd_rows(q, k, v, do, lse, scale, chunk_causal, name, c=None, ct=None, hps=4, ride=None):
    _, s, dk = q.shape
    dv = v.shape[2]
    t = _pick(s, (256, 128))
    n = s // t
    bias = c is not None

    def body(*refs):
        if bias:
            q_ref, k_ref, v_ref, do_ref, lse_ref, c_ref, ct_ref, dq_ref, dk_ref, dv_ref, dck_ref, p_s, dp_s = refs
        else:
            q_ref, k_ref, v_ref, do_ref, lse_ref, dq_ref, dk_ref, dv_ref, p_s, dp_s = refs
        hp, i = pl.program_id(0), pl.program_id(1)

        @pl.when(i == 0)
        def _():
            dk_ref[...] = jnp.zeros(dk_ref.shape, F32)
            dv_ref[...] = jnp.zeros(dv_ref.shape, F32)
            if bias:
                dck_ref[...] = jnp.zeros(dck_ref.shape, F32)

        qb = [q_ref[e] for e in range(hps)]
        dob = [do_ref[e] for e in range(hps)]
        lse_b = [lse_ref[e] for e in range(hps)]
        cq = [_pick_col(c_ref[...], hp * hps + e) if bias else None for e in range(hps)]

        def sweep1(j, acc, diagonal):
            off = pl.multiple_of(j * t, t)
            out = []
            for e in range(hps):
                sc = _nt(qb[e], k_ref[e, pl.ds(off, t), :]) * scale
                if bias:
                    sc = sc + (cq[e] - ct_ref[pl.ds(hp * hps + e, 1), pl.ds(off, t)])
                pr = jnp.exp(sc - lse_b[e])
                if diagonal:
                    pr = jnp.where(_allowed(i, j, t, chunk_causal), pr, 0.0)
                dp = _nt(dob[e], v_ref[e, pl.ds(off, t), :])
                p_s[e, j] = pr
                dp_s[e, j] = dp
                out.append(acc[e] + jnp.sum(pr * dp, axis=1, keepdims=True))
            return tuple(out)

        zero = tuple(jnp.zeros((t, 1), F32) for _ in range(hps))
        delta = sweep1(i, lax.fori_loop(0, i, functools.partial(sweep1, diagonal=False), zero), True)

        def sweep2(j, dq_acc):
            off = pl.multiple_of(j * t, t)
            out = []
            for e in range(hps):
                pr = p_s[e, j]
                ds = pr * (dp_s[e, j] - delta[e])
                kb = k_ref[e, pl.ds(off, t), :]
                dv_ref[e, pl.ds(off, t), :] += _tn(pr, dob[e])
                dk_ref[e, pl.ds(off, t), :] += _tn(ds, qb[e]) * scale
                if bias:
                    dck_ref[e, j] -= jnp.sum(ds, axis=0, keepdims=True)
                out.append(dq_acc[e] + _nn(ds, kb))
            return tuple(out)

        dq = lax.fori_loop(0, i + 1, sweep2, tuple(jnp.zeros((t, dk), F32) for _ in range(hps)))
        for e in range(hps):
            dq_ref[e] = dq[e] * scale

    full = lambda d: pl.BlockSpec((hps, s, d), lambda h, i: (h, 0, 0))
    blk = lambda d: pl.BlockSpec((hps, t, d), lambda h, i: (h, i, 0))
    ins = [q, k, v, do, lse] + ([c, ct] if bias else [])
    in_specs = [blk(dk), full(dk), full(dv), blk(dv), blk(1)]
    out_specs = [blk(dk), full(dk), full(dv)]
    out_shape = [_sds((HEADS, s, dk)), _sds((HEADS, s, dk)), _sds((HEADS, s, dv))]
    if bias:
        in_specs += [pl.BlockSpec((t, LANES), lambda h, i: (i, 0)), pl.BlockSpec((HEADS, s), lambda h, i: (0, 0))]
        out_specs.append(pl.BlockSpec((hps, n, 1, t), lambda h, i: (h, 0, 0, 0)))
        out_shape.append(_sds((HEADS, n, 1, t)))
    res = _call_with_ride(body, name, (HEADS // hps, n), ins, in_specs, out_shape, out_specs, ("parallel", "arbitrary"), 56,
                          ride, scratch=[pltpu.VMEM((hps, n, t, t), F32), pltpu.VMEM((hps, n, t, t), F32)])
    return (*res[:3], res[3].reshape(HEADS, 1, s)) if bias else tuple(res)


def _s5_disc(lr, li, ldt, br, bi):
    dt = jnp.exp(ldt)
    mag = jnp.exp(lr * dt)
    a_re = mag * jnp.cos(li * dt)
    a_im = mag * jnp.sin(li * dt)
    den = lr * lr + li * li
    f_re = ((a_re - 1.0) * lr + a_im * li) / den
    f_im = (a_im * lr - (a_re - 1.0) * li) / den
    return a_re, a_im, f_re * br - f_im * bi, f_re * bi + f_im * br


def _s5_params_fwd(lr16, li16, ldt16, br2, bi2, name):
    def body(a, b, c, d, e, o0, o1, o2, o3):
        r = _s5_disc(a[...], b[...], c[...], d[...], e[...])
        o0[...], o1[...], o2[...], o3[...] = r

    return pl.pallas_call(body, name=name, out_shape=[_sds((512, 64))] * 4)(lr16, li16, ldt16, br2, bi2)


def _s5_params_bwd(lr16, li16, ldt16, br2, bi2, da_re16, da_im16, dbb_re, dbb_im, name):
    def body(a, b, c, d, e, g0, g1, g2, g3, o_lr, o_li, o_dt, o_br, o_bi):
        _, vjp = jax.vjp(_s5_disc, a[...], b[...], c[...], d[...], e[...])
        dlr, dli, dldt, dbr, dbi = vjp((g0[...], g1[...], g2[...], g3[...]))
        grp = lambda z: z.reshape(S5_GROUPS, S5_GROUP, S5_STATE).sum(axis=1)
        o_lr[...] = grp(dlr)
        o_li[...] = grp(dli)
        o_dt[...] = jnp.sum(grp(dldt), axis=1, keepdims=True)
        o_br[...] = dbr
        o_bi[...] = dbi

    return pl.pallas_call(
        body, name=name, out_shape=[_sds((32, 64)), _sds((32, 64)), _sds((32, 1)), _sds((512, 64)), _sds((512, 64))],
    )(lr16, li16, ldt16, br2, bi2, da_re16, da_im16, dbb_re, dbb_im)


def _cmul(ar, ai, br, bi):
    return ar * br - ai * bi, ar * bi + ai * br


S5_SUPER = 4


def _scan_loop(bre_r, bim_r, ar, ai, ore_r, oim_r, reverse, xre_r=None, xim_r=None):
    s, lw = bre_r.shape
    nt = s // SUBLANES
    with_da = xre_r is not None
    shp = (SUBLANES, lw)
    row = lax.broadcasted_iota(jnp.int32, shp, 0)
    pows = [(ar, ai)]
    for _ in range(SUBLANES - 1):
        pows.append(_cmul(pows[-1][0], pows[-1][1], ar, ai))
    cm_r, cm_i = jnp.zeros(shp, F32), jnp.zeros(shp, F32)
    for r in range(SUBLANES):
        e = (SUBLANES - 1 - r) if reverse else r
        cm_r = jnp.where(row == r, jnp.broadcast_to(pows[e][0], shp), cm_r)
        cm_i = jnp.where(row == r, jnp.broadcast_to(pows[e][1], shp), cm_i)
    steps = [(1, pows[0]), (2, pows[1]), (4, pows[3])]

    def tile(it, carry):
        if with_da:
            c_r, c_i, acc_r, acc_i = carry
        else:
            c_r, c_i = carry
        r = (nt - 1 - it) if reverse else it
        off = pl.multiple_of(r * SUBLANES, SUBLANES)
        xr, xi = bre_r[pl.ds(off, SUBLANES), :], bim_r[pl.ds(off, SUBLANES), :]
        for sh, (pr, pi) in steps:
            if reverse:
                keep = row < SUBLANES - sh
                sr = jnp.where(keep, pltpu.roll(xr, SUBLANES - sh, 0), 0.0)
                si = jnp.where(keep, pltpu.roll(xi, SUBLANES - sh, 0), 0.0)
            else:
                keep = row >= sh
                sr = jnp.where(keep, pltpu.roll(xr, sh, 0), 0.0)
                si = jnp.where(keep, pltpu.roll(xi, sh, 0), 0.0)
            mr, mi = _cmul(pr, pi, sr, si)
            xr, xi = xr + mr, xi + mi
        mr, mi = _cmul(cm_r, cm_i, c_r, c_i)
        xr, xi = xr + mr, xi + mi
        ore_r[pl.ds(off, SUBLANES), :] = xr
        oim_r[pl.ds(off, SUBLANES), :] = xi
        edge = 0 if reverse else SUBLANES - 1
        c_r, c_i = xr[edge:edge + 1, :], xi[edge:edge + 1, :]
        if not with_da:
            return c_r, c_i
        fr, fi = xre_r[pl.ds(off, SUBLANES), :], xim_r[pl.ds(off, SUBLANES), :]
        poff = pl.multiple_of(jnp.maximum(r - 1, 0) * SUBLANES, SUBLANES)
        live = (r > 0).astype(F32)
        pr_last = xre_r[pl.ds(poff, SUBLANES), :][SUBLANES - 1:SUBLANES, :] * live
        pi_last = xim_r[pl.ds(poff, SUBLANES), :][SUBLANES - 1:SUBLANES, :] * live
        sr = jnp.where(row >= 1, pltpu.roll(fr, 1, 0), jnp.broadcast_to(pr_last, shp))
        si = jnp.where(row >= 1, pltpu.roll(fi, 1, 0), jnp.broadcast_to(pi_last, shp))
        return c_r, c_i, acc_r + xr * sr + xi * si, acc_i + xi * sr - xr * si

    z1 = jnp.zeros((1, lw), F32)
    if not with_da:
        lax.fori_loop(0, nt, tile, (z1, z1))
        return None
    _, _, acc_r, acc_i = lax.fori_loop(0, nt, tile, (z1, z1, jnp.zeros(shp, F32), jnp.zeros(shp, F32)))
    return jnp.sum(acc_r, axis=0, keepdims=True), jnp.sum(acc_i, axis=0, keepdims=True)


S5_ROWS = 512


def _group_compact(p):
    grp = lax.broadcasted_iota(jnp.int32, (LANES, S5_STATE), 0) // S5_GROUP
    out = jnp.zeros((LANES, S5_STATE), F32)
    for j in range(LANES // S5_GROUP):
        out = jnp.where(grp == j, p[:, S5_STATE * j:S5_STATE * (j + 1)], out)
    return out


def _s5_core_fwd(u, wb_re, wb_im, wc_re, wc_im, a_re, a_im, name, ride=None):
    s = u.shape[0]
    lw = S5_LANES // S5_SUPER
    rows = _pick(s, (S5_ROWS, 256, 128))

    def body(u_r, wbr, wbi, wcr, wci, are_r, aim_r, xre_r, xim_r, y_r, bre_s, bim_s):
        for r0 in range(0, s, rows):
            ub = u_r[r0:r0 + rows, :]
            bre_s[r0:r0 + rows, :] = _nn(ub, wbr[0])
            bim_s[r0:r0 + rows, :] = _nn(ub, wbi[0])
        _scan_loop(bre_s, bim_s, are_r[...], aim_r[...], xre_r, xim_r, False)
        for r0 in range(0, s, rows):
            y_r[r0:r0 + rows, :] = _nn(xre_r[r0:r0 + rows, :], wcr[0]) + _nn(xim_r[r0:r0 + rows, :], wci[0])

    nar = pl.BlockSpec((s, LANES), lambda k: (0, k))
    wide = pl.BlockSpec((s, lw), lambda k: (0, k))
    one = pl.BlockSpec((1, lw), lambda k: (0, k))
    wb = pl.BlockSpec((1, LANES, lw), lambda k: (k, 0, 0))
    wc = pl.BlockSpec((1, lw, LANES), lambda k: (k, 0, 0))
    return _call_with_ride(body, name, (S5_SUPER,), [u, wb_re, wb_im, wc_re, wc_im, a_re, a_im], [nar, wb, wb, wc, wc, one, one],
                           [_sds((s, S5_LANES)), _sds((s, S5_LANES)), _sds((s, S5_WIDTH))], [wide, wide, nar], ("parallel",), 56,
                           ride, scratch=[pltpu.VMEM((s, lw), F32), pltpu.VMEM((s, lw), F32)])


def _s5_core_bwd(dy0, u, du_a, x_re, x_im, wb_re, wb_im, wc_re, wc_im, a_re, a_im_neg, name, ride=None):
    s = u.shape[0]
    lw = S5_LANES // S5_SUPER
    rows = _pick(s, (S5_ROWS, 256, 128))

    def body(dy_r, u_r, dua_r, xre_r, xim_r, wbr, wbi, wcr, wci, are_r, aim_r, du_r, dare_r, daim_r, dbr_r, dbi_r, dcr_r, dci_r,
             dre_s, dim_s, gre_s, gim_s):
        for r0 in range(0, s, rows):
            dyb = dy_r[r0:r0 + rows, :]
            dre_s[r0:r0 + rows, :] = _nt(dyb, wcr[0])
            dim_s[r0:r0 + rows, :] = _nt(dyb, wci[0])
        dare_r[...], daim_r[...] = _scan_loop(dre_s, dim_s, are_r[...], aim_r[...], gre_s, gim_s, True, xre_r, xim_r)
        acc = [jnp.zeros((LANES, lw), F32) for _ in range(4)]
        for r0 in range(0, s, rows):
            sl = slice(r0, r0 + rows)
            gr, gi, ub, dyb = gre_s[sl, :], gim_s[sl, :], u_r[sl, :], dy_r[sl, :]
            du_r[sl, :] = dua_r[sl, :] + _nt(gr, wbr[0]) + _nt(gi, wbi[0])
            acc = [acc[0] + _tn(ub, gr), acc[1] + _tn(ub, gi), acc[2] + _tn(dyb, xre_r[sl, :]), acc[3] + _tn(dyb, xim_r[sl, :])]
        dbr_r[...], dbi_r[...], dcr_r[...], dci_r[...] = (_group_compact(a) for a in acc)

    nar = pl.BlockSpec((s, LANES), lambda k: (0, k))
    wide = pl.BlockSpec((s, lw), lambda k: (0, k))
    one = pl.BlockSpec((1, lw), lambda k: (0, k))
    wb = pl.BlockSpec((1, LANES, lw), lambda k: (k, 0, 0))
    wc = pl.BlockSpec((1, lw, LANES), lambda k: (k, 0, 0))
    blk = pl.BlockSpec((LANES, S5_STATE), lambda k: (k, 0))
    return _call_with_ride(
        body, name, (S5_SUPER,), [dy0, u, du_a, x_re, x_im, wb_re, wb_im, wc_re, wc_im, a_re, a_im_neg],
        [nar, nar, nar, wide, wide, wb, wb, wc, wc, one, one],
        [_sds((s, S5_WIDTH)), _sds((1, S5_LANES)), _sds((1, S5_LANES))] + [_sds((S5_WIDTH, S5_STATE))] * 4,
        [nar, one, one, blk, blk, blk, blk], ("parallel",), 60, ride, scratch=[pltpu.VMEM((s, lw), F32)] * 4)


def _s5_seg1(y0, u, d):
    return jax.nn.gelu(y0 + d * u)


def _s5_seg2(z, t, b):
    return z * jax.nn.sigmoid(t + b)


def _s5_post_fwd(y0, seg2, d, wglu, bglu, name):
    s = y0.shape[0]
    ts = _pick(s, (256, 128))

    def fn(ins, outs, accs, scr):
        z = _s5_seg1(ins[0][...], ins[1][...], ins[2][...])
        outs[0][...] = _s5_seg2(z, _nn(z, ins[3][...]), ins[4][...])

    return _stage(name, fn, s // ts,
                  [(y0, _rows(ts, 512)), (seg2, _rows(ts, 512)), (d, _full(d.shape)), (wglu, _full(wglu.shape)),
                   (bglu, _full(bglu.shape))], [(_sds((s, 512)), _rows(ts, 512))])[0]


def _s5_post_bwd(y0, seg2, d, wglu, bglu, dy, name):
    s = y0.shape[0]
    ts = _pick(s, (256, 128))

    def fn(ins, outs, accs, scr):
        y0_r, u_r, d_r, w_r, b_r, dy_r = ins
        z, vjp1 = jax.vjp(_s5_seg1, y0_r[...], u_r[...], d_r[...])
        t = _nn(z, w_r[...])
        _, vjp2 = jax.vjp(_s5_seg2, z, t, b_r[...])
        dz, dt, db = vjp2(dy_r[...])
        accs[0][...] += _tn(z, dt)
        accs[1][...] += db
        dy0, du, dd = vjp1(dz + _nt(dt, w_r[...]))
        accs[2][...] += dd
        outs[0][...] = dy0
        outs[1][...] = du

    return _stage(name, fn, s // ts,
                  [(y0, _rows(ts, 512)), (seg2, _rows(ts, 512)), (d, _full(d.shape)), (wglu, _full(wglu.shape)),
                   (bglu, _full(bglu.shape)), (dy, _rows(ts, 512))],
                  [(_sds((s, 512)), _rows(ts, 512)), (_sds((s, 512)), _rows(ts, 512))],
                  accs=[_sds((512, 512)), _sds((1, 512)), _sds((1, 512))])


def _gate_a(y, g):
    return y * jax.nn.silu(g)


def _gate_m(o0, o1, o2, m0, m1, m2):
    return jax.nn.sigmoid(m0) * o0 + jax.nn.sigmoid(m1) * o1 + jax.nn.sigmoid(m2) * o2


def _assemble(ybuf, o_mla, o_fox, y_s5):
    for h in range(HEADS):
        ybuf[:, 64 * h:64 * (h + 1)] = o_mla[h]
        ybuf[:, 512 + 64 * h:512 + 64 * (h + 1)] = o_fox[h]
    ybuf[:, 1024:1536] = y_s5[...]


def _gate_fwd(o_mla, o_fox, y_s5, seg3, x, wo, wout, name):
    s = x.shape[0]
    ts = _pick(s, (256, 128))

    def fn(ins, outs, accs, scr):
        om, of, ys, gm_r, x_r, wo_r, wout_r = ins
        ybuf = scr[0]
        _assemble(ybuf, om, of, ys)
        a = _gate_a(ybuf[...], gm_r[:, :GATES_W])
        o = [_nn(a[:, 512 * b:512 * (b + 1)], wo_r[512 * b:512 * (b + 1), :]) for b in range(3)]
        merged = _gate_m(o[0], o[1], o[2], *[gm_r[:, GATES_W + 1024 * b:GATES_W + 1024 * (b + 1)] for b in range(3)])
        outs[0][...] = x_r[...] + _nn(merged, wout_r[...])

    return _stage(name, fn, s // ts,
                  [(o_mla, _heads(ts, 64)), (o_fox, _heads(ts, 64)), (y_s5, _rows(ts, 512)), (seg3, _rows(ts, SEG_W[3])),
                   (x, _rows(ts, D_MODEL)), (wo, _full(wo.shape)), (wout, _full(wout.shape))],
                  [(_sds((s, D_MODEL)), _rows(ts, D_MODEL))], scratch=[pltpu.VMEM((ts, 1536), F32)])[0]


def _gate_bwd(o_mla, o_fox, y_s5, seg3, wo, wout, dout, name):
    s = dout.shape[0]
    ts = _pick(s, (128,))

    def fn(ins, outs, accs, scr):
        om, of, ys, gm_r, wo_r, wout_r, dout_r = ins
        do_mla, do_fox, dys, dgm_r = outs
        dwo, dwout = accs
        ybuf, dabuf = scr
        _assemble(ybuf, om, of, ys)
        a, vjp_a = jax.vjp(_gate_a, ybuf[...], gm_r[:, :GATES_W])
        o = [_nn(a[:, 512 * b:512 * (b + 1)], wo_r[512 * b:512 * (b + 1), :]) for b in range(3)]
        ms = [gm_r[:, GATES_W + 1024 * b:GATES_W + 1024 * (b + 1)] for b in range(3)]
        merged, vjp_m = jax.vjp(_gate_m, *o, *ms)
        dout_v = dout_r[...]
        dwout[...] += _tn(merged, dout_v)
        cts = vjp_m(_nt(dout_v, wout_r[...]))
        for b in range(3):
            dgm_r[:, GATES_W + 1024 * b:GATES_W + 1024 * (b + 1)] = cts[3 + b].astype(BF16)
            dwo[512 * b:512 * (b + 1), :] += _tn(a[:, 512 * b:512 * (b + 1)], cts[b])
            dabuf[:, 512 * b:512 * (b + 1)] = _nt(cts[b], wo_r[512 * b:512 * (b + 1), :])
        dy, dg = vjp_a(dabuf[...])
        dgm_r[:, :GATES_W] = dg.astype(BF16)
        dys[...] = dy[:, 1024:1536]
        for h in range(HEADS):
            do_mla[h] = dy[:, 64 * h:64 * (h + 1)]
            do_fox[h] = dy[:, 512 + 64 * h:512 + 64 * (h + 1)]

    return _stage(name, fn, s // ts,
                  [(o_mla, _heads(ts, 64)), (o_fox, _heads(ts, 64)), (y_s5, _rows(ts, 512)), (seg3, _rows(ts, SEG_W[3])),
                   (wo, _full(wo.shape)), (wout, _full(wout.shape)), (dout, _rows(ts, D_MODEL))],
                  [(_sds((HEADS, s, 64)), _heads(ts, 64)), (_sds((HEADS, s, 64)), _heads(ts, 64)), (_sds((s, 512)), _rows(ts, 512)),
                   (_sds((s, SEG_W[3]), BF16), _rows(ts, SEG_W[3]))],
                  accs=[_sds(wo.shape), _sds(wout.shape)], scratch=[pltpu.VMEM((ts, 1536), F32), pltpu.VMEM((ts, 1536), F32)],
                  vmem_mb=56)


def _loss_head(y, target, name):
    s = y.shape[0]
    ts = _pick(s, (256, 128))

    def fn(ins, outs, accs, scr):
        e = ins[0][...] - ins[1][...]
        outs[0][...] = e / D_MODEL
        accs[0][...] += 0.5 * jnp.sum(jnp.sum(e * e, axis=1, keepdims=True) / D_MODEL, axis=0, keepdims=True)

    r = _stage(name, fn, s // ts, [(y, _rows(ts, D_MODEL)), (target, _rows(ts, D_MODEL))],
               [(_sds((s, D_MODEL)), _rows(ts, D_MODEL))], accs=[_sds((1, 1))])
    return r[1], r[0]


IN_RANGES = ((0, 384, 0), (384, 416, 448), (416, 1952, 512), (1952, 1960, 384), (1960, IN_WIDTH, 2048))
SHARD_W = IN_WIDTH // N_DEV


def _win_pieces(d):
    lo, hi = SHARD_W * d, SHARD_W * (d + 1)
    out = []
    for a, b, p in IN_RANGES:
        s, e = max(a, lo), min(b, hi)
        while s < e:
            pad = p + (s - a)
            k = max(i for i in range(NSEG) if SEG_OFF[i] <= pad)
            w = min(e - s, SEG_OFF[k] + SEG_W[k] - pad)
            out.append((s - lo, w, k, pad - SEG_OFF[k]))
            s += w
    return out


def _win_pad(g, name):
    _, r, _ = g.shape
    tr = 64

    def body(g_ref, o_ref):
        o_ref[...] = jnp.zeros(o_ref.shape, o_ref.dtype)
        for d in range(N_DEV):
            for dst, w, k, src in _win_pieces(d):
                o_ref[:, SEG_OFF[k] + src:SEG_OFF[k] + src + w] = g_ref[d, :, dst:dst + w]

    return pl.pallas_call(
        body, name=name, grid=(r // tr,), in_specs=[pl.BlockSpec((N_DEV, tr, SHARD_W), lambda i: (0, i, 0))],
        out_specs=pl.BlockSpec((tr, PAD_IN), lambda i: (i, 0)), out_shape=jax.ShapeDtypeStruct((r, PAD_IN), g.dtype),
        compiler_params=pltpu.CompilerParams(dimension_semantics=("parallel",)),
    )(g)


def _win_unpad(dsegs, name):
    r = dsegs[0].shape[0]
    tr = 64

    def body(*refs):
        o_ref = refs[NSEG]
        for d in range(N_DEV):
            for dst, w, k, src in _win_pieces(d):
                o_ref[d, :, dst:dst + w] = refs[k][:, src:src + w]

    return pl.pallas_call(
        body, name=name, grid=(r // tr,), in_specs=[pl.BlockSpec((tr, SEG_W[k]), lambda i: (i, 0)) for k in range(NSEG)],
        out_specs=pl.BlockSpec((N_DEV, tr, SHARD_W), lambda i: (0, i, 0)),
        out_shape=jax.ShapeDtypeStruct((N_DEV, r, SHARD_W), dsegs[0].dtype),
        compiler_params=pltpu.CompilerParams(dimension_semantics=("parallel",)),
    )(*dsegs)


def _pad_wq(w):
    w = w.reshape(MLA_Q_RANK, HEADS, MLA_QK)
    return jnp.pad(w, ((0, 0), (0, 0), (0, LANES - MLA_QK))).reshape(MLA_Q_RANK, HEADS * LANES)


def _unpad_wq(d):
    return d.reshape(MLA_Q_RANK, HEADS, LANES)[:, :, :MLA_QK].reshape(MLA_Q_RANK, HEADS * MLA_QK)


def _pad_wkv(w):
    w = w.reshape(MLA_KV_RANK, HEADS, MLA_NOPE + MLA_V)
    k = jnp.pad(w[:, :, :MLA_NOPE], ((0, 0), (0, 0), (0, LANES - MLA_NOPE))).reshape(MLA_KV_RANK, HEADS * LANES)
    return jnp.concatenate([k, w[:, :, MLA_NOPE:].reshape(MLA_KV_RANK, HEADS * MLA_V)], axis=1)


def _unpad_wkv(d):
    k = d[:, :HEADS * LANES].reshape(MLA_KV_RANK, HEADS, LANES)[:, :, :MLA_NOPE]
    v = d[:, HEADS * LANES:].reshape(MLA_KV_RANK, HEADS, MLA_V)
    return jnp.concatenate([k, v], axis=2).reshape(MLA_KV_RANK, HEADS * (MLA_NOPE + MLA_V))


def _pad_lanes(v, n=LANES):
    return jnp.pad(v, (0, n - v.shape[0])).reshape(1, n)


def _super_blocks(b):
    _, r, c = b.shape
    per = S5_GROUPS // S5_SUPER
    b = b.reshape(S5_SUPER, per, r, c)
    eye = jnp.eye(per, dtype=b.dtype)
    return (b[:, :, :, None, :] * eye[None, :, None, :, None]).reshape(S5_SUPER, per * r, per * c)


def _layer_params(l, w, small):
    p = {k: small[k][l] for k in small}
    q = {}
    q["norm_g"] = p["norm_g"].reshape(1, D_MODEL)
    q["mla_q_a_norm"] = p["mla_q_a_norm"].reshape(1, 256)
    q["mla_kv_a_norm"] = p["mla_kv_a_norm"].reshape(1, 128)
    q["mla_q_norm"] = _pad_lanes(p["mla_q_norm"])
    q["mla_k_norm"] = _pad_lanes(p["mla_k_norm"])
    q["fox_b_f"] = _pad_lanes(p["fox_b_f"])
    q["fox_q_norm"] = p["fox_q_norm"].reshape(1, 64)
    q["fox_k_norm"] = p["fox_k_norm"].reshape(1, 64)
    q["s5_d"] = p["s5_d"].reshape(1, 512)
    q["s5_b_glu"] = p["s5_b_glu"].reshape(1, 512)
    rep = lambda z: jnp.repeat(z, S5_GROUP, axis=0)
    q["lr16"], q["li16"] = rep(p["s5_lambda_re"]), rep(p["s5_lambda_im"])
    q["ldt16"] = rep(jnp.broadcast_to(p["s5_log_dt"][:, None], (S5_GROUPS, S5_STATE)))
    q["br2"] = p["s5_b_re"].transpose(0, 2, 1).reshape(512, 64)
    q["bi2"] = p["s5_b_im"].transpose(0, 2, 1).reshape(512, 64)
    q["c_re"], q["c_im"] = p["s5_c_re"], p["s5_c_im"]
    q["w_in"] = _win_pad(w["w_in"], f"l{l}_w_in_pad")
    q["wq"] = _pad_wq(w["mla_w_q_up"])
    q["wkv"] = _pad_wkv(w["mla_w_kv_up"])
    q.update(_late_weights(w))
    return q


def _late_weights(w):
    return {q: w[k] for q, k in (("wglu", "s5_w_glu"), ("wo", "w_branch_out"), ("wout", "w_out")) if k in w}


def _layer_fwd(l, x, tabs, q, rides=None):
    rides = rides or {}
    n = lambda s: f"l{l}_{s}"
    sv = {"x": x}
    h = _norm_fwd(x, q["norm_g"], n("norm_fwd"))
    sv["h"] = h
    seg = [_mm(h, q["w_in"], "nn", n(f"proj{k}"), b_cols=(SEG_OFF[k], SEG_W[k])) for k in range(NSEG)]
    sv["seg"] = seg
    mq, mk, mv = _mla_prep_fwd(seg[0], tabs, q, q["wq"], q["wkv"], n("mla_prep_fwd"))
    o_mla, lse_mla = _attn_fwd(mq, mk, mv, 1.0 / math.sqrt(MLA_QK), True, n("mla_attn_fwd"), ride=rides.get("mla_attn_fwd"))
    sv.update(mq=mq, mk=mk, mv=mv, o_mla=o_mla, lse_mla=lse_mla)
    fq, fk, fv, c, ct = _fox_prep_fwd(seg[0], seg[1], q["fox_b_f"], q["fox_q_norm"], q["fox_k_norm"], n("fox_prep_fwd"))
    o_fox, lse_fox = _attn_fwd(fq, fk, fv, 1.0 / math.sqrt(FOX_DIM), False, n("fox_attn_fwd"), c=c, ct=ct,
                               ride=rides.get("fox_attn_fwd"))
    sv.update(fq=fq, fk=fk, fv=fv, c=c, ct=ct, o_fox=o_fox, lse_fox=lse_fox)
    a_re16, a_im16, bb_re, bb_im = _s5_params_fwd(q["lr16"], q["li16"], q["ldt16"], q["br2"], q["bi2"], n("s5_params_fwd"))
    a_re = a_re16.reshape(S5_GROUPS, S5_GROUP, S5_STATE)[:, 0, :].reshape(1, S5_LANES)
    a_im = a_im16.reshape(S5_GROUPS, S5_GROUP, S5_STATE)[:, 0, :].reshape(1, S5_LANES)
    wb_re = _super_blocks(bb_re.reshape(S5_GROUPS, S5_GROUP, S5_STATE)).astype(BF16)
    wb_im = _super_blocks(bb_im.reshape(S5_GROUPS, S5_GROUP, S5_STATE)).astype(BF16)
    wc_re = _super_blocks(q["c_re"].transpose(0, 2, 1)).astype(BF16)
    wc_im = _super_blocks(-q["c_im"].transpose(0, 2, 1)).astype(BF16)
    x_re, x_im, y0 = _s5_core_fwd(seg[2], wb_re, wb_im, wc_re, wc_im, a_re, a_im, n("s5_scan_fwd"), ride=rides.get("s5_scan_fwd"))
    y_s5 = _s5_post_fwd(y0, seg[2], q["s5_d"], q["wglu"], q["s5_b_glu"], n("s5_post_fwd"))
    sv.update(a_re=a_re, a_im=a_im, wb_re=wb_re, wb_im=wb_im, wc_re=wc_re, wc_im=wc_im, x_re=x_re, x_im=x_im, y0=y0, y_s5=y_s5)
    out = _gate_fwd(o_mla, o_fox, y_s5, seg[3], x, q["wo"], q["wout"], n("gate_fwd"))
    return out, sv


def _layer_bwd(l, dout, tabs, q, sv, rides=None, g=None):
    rides = rides or {}
    n = lambda s: f"l{l}_{s}"
    seg = sv["seg"]
    g = {} if g is None else g
    (do_mla, do_fox, dy_s5, dseg3, g["wo"], g["wout"]) = _gate_bwd(
        sv["o_mla"], sv["o_fox"], sv["y_s5"], seg[3], q["wo"], q["wout"], dout, n("gate_bwd"))
    dy0, du_a, g["wglu"], g["s5_b_glu"], g["s5_d"] = _s5_post_bwd(sv["y0"], seg[2], q["s5_d"], q["wglu"], q["s5_b_glu"], dy_s5,
                                                                 n("s5_post_bwd"))
    dseg2, da_re, da_im, dbb_re, dbb_im, dc_re, dc_im = _s5_core_bwd(
        dy0, seg[2], du_a, sv["x_re"], sv["x_im"], sv["wb_re"], sv["wb_im"], sv["wc_re"], sv["wc_im"], sv["a_re"], -sv["a_im"],
        n("s5_scan_bwd"), ride=rides.get("s5_scan_bwd"))
    g["s5_c_re"] = dc_re.reshape(S5_GROUPS, S5_GROUP, S5_STATE)
    g["s5_c_im"] = -dc_im.reshape(S5_GROUPS, S5_GROUP, S5_STATE)
    first = (jnp.arange(512) % S5_GROUP == 0).astype(F32)[:, None]
    da_re16 = jnp.repeat(da_re.reshape(S5_GROUPS, S5_STATE), S5_GROUP, axis=0) * first
    da_im16 = jnp.repeat(da_im.reshape(S5_GROUPS, S5_STATE), S5_GROUP, axis=0) * first
    dlr, dli, dldt, dbr2, dbi2 = _s5_params_bwd(q["lr16"], q["li16"], q["ldt16"], q["br2"], q["bi2"], da_re16, da_im16, dbb_re,
                                               dbb_im, n("s5_params_bwd"))
    g["s5_lambda_re"], g["s5_lambda_im"], g["s5_log_dt"] = dlr, dli, dldt.reshape(S5_GROUPS)
    g["s5_b_re"] = dbr2.reshape(S5_GROUPS, S5_GROUP, S5_STATE).transpose(0, 2, 1)
    g["s5_b_im"] = dbi2.reshape(S5_GROUPS, S5_GROUP, S5_STATE).transpose(0, 2, 1)
    dfq, dfk, dfv, dck = _attn_bwd_rows(sv["fq"], sv["fk"], sv["fv"], do_fox, sv["lse_fox"], 1.0 / math.sqrt(FOX_DIM), False,
                                        n("fox_attn_bwd"), sv["c"], sv["ct"], ride=rides.get("fox_attn_bwd"))
    dseg1, dff, g["fox_q_norm"], g["fox_k_norm"], dbf = _fox_prep_bwd(seg[0], seg[1], q["fox_b_f"], q["fox_q_norm"], q["fox_k_norm"],
                                                                      dfq, dfk, dfv, dck, n("fox_prep_bwd"))
    g["fox_b_f"] = dbf[0, :HEADS]
    dmq, dmk, dmv = _attn_bwd_rows(sv["mq"], sv["mk"], sv["mv"], do_mla, sv["lse_mla"], 1.0 / math.sqrt(MLA_QK), True,
                                   n("mla_attn_bwd"), ride=rides.get("mla_attn_bwd"))
    dseg0, dqan, dkvan, dqn, dkn, g["wq"], g["wkv"] = _mla_prep_bwd(seg[0], tabs, q, q["wq"], q["wkv"], dmq, dmk, dmv, dff,
                                                                   n("mla_prep_bwd"))
    g["mla_q_a_norm"], g["mla_kv_a_norm"] = dqan, dkvan
    g["mla_q_norm"], g["mla_k_norm"] = dqn[0, :MLA_QK], dkn[0, :MLA_QK]
    dsegs = [dseg0, dseg1, dseg2, dseg3]
    g["w_in"] = [_mm(sv["h"], dsegs[k], "tn", n(f"dwin{k}")) for k in range(NSEG)]
    dh = None
    for k in range(NSEG):
        dh = _mm(dsegs[k], q["w_in"], "nt", n(f"dh{k}"), acc=dh, b_cols=(SEG_OFF[k], SEG_W[k]), ride=rides.get(f"dh{k}"))
    dx, g["norm_g"] = _norm_bwd(sv["x"], q["norm_g"], dh, dout, n("norm_bwd"))
    return dx, g


MESH = pl.DeviceIdType.MESH
ANY = pl.BlockSpec(memory_space=pl.ANY)


def _all_gather(blocks, name):
    n = len(blocks)

    def body(*refs):
        x_refs, out_refs = refs[:n], refs[n:2 * n]
        send_sems, recv_sems, local_sems = refs[2 * n:]
        x, y, c = lax.axis_index("x"), lax.axis_index("y"), lax.axis_index("c")
        me, sibling = (x, y, c), (x, y, 1 - c)
        chips = [(1 - x, y), (x, 1 - y), (1 - x, 1 - y)]

        def slot(a, px, py, pc):
            return out_refs[a].at[4 * px + 2 * py + pc]

        def copy(a, k, blk, to, src=None):
            return pltpu.make_async_remote_copy(src_ref=slot(a, *blk) if src is None else src, dst_ref=slot(a, *blk),
                                                send_sem=send_sems.at[7 * a + k], recv_sem=recv_sems.at[7 * a + k],
                                                device_id=to, device_id_type=MESH)

        mine = [pltpu.make_async_copy(x_refs[a], slot(a, *me), local_sems.at[a]) for a in range(n)]
        for cp in mine:
            cp.start()
        first = []
        for j, chip in enumerate(chips):
            first += [copy(a, 1 + j, me, (*chip, c), src=x_refs[a]) for a in range(n)]
        first += [copy(a, 0, me, sibling, src=x_refs[a]) for a in range(n)]
        for cp in first:
            cp.start()
        passed = []
        for j, chip in enumerate(chips):
            for a in range(n):
                copy(a, 1 + j, (*chip, c), me).wait_recv()
                passed.append(copy(a, 4 + j, (*chip, c), sibling))
                passed[-1].start()
        for a in range(n):
            copy(a, 0, sibling, me).wait_recv()
        for j, chip in enumerate(chips):
            for a in range(n):
                copy(a, 4 + j, (*chip, 1 - c), me).wait_recv()
        for cp in first + passed:
            cp.wait_send()
        for cp in mine:
            cp.wait()

    return pl.pallas_call(
        body, name=name, out_shape=[jax.ShapeDtypeStruct((N_DEV,) + b.shape, b.dtype) for b in blocks],
        in_specs=[ANY] * n, out_specs=[ANY] * n,
        scratch_shapes=[pltpu.SemaphoreType.DMA((7 * n,)), pltpu.SemaphoreType.DMA((7 * n,)), pltpu.SemaphoreType.DMA((n,))],
    )(*blocks)


def _place():
    x, y, c = lax.axis_index("x"), lax.axis_index("y"), lax.axis_index("c")
    return x, y, c, [(1 - x, y), (x, 1 - y), (1 - x, 1 - y)]


def _remote(src, dst, send, recv, k, to):
    return pltpu.make_async_remote_copy(src_ref=src, dst_ref=dst, send_sem=send.at[k], recv_sem=recv.at[k], device_id=to,
                                        device_id_type=MESH)


def _plan_gather_ici(blocks):
    n = len(blocks)

    def copies(in_refs, out_refs, send, recv, local):
        x, y, c, chips = _place()
        mine = 4 * x + 2 * y + c
        loc = [pltpu.make_async_copy(in_refs[a], out_refs[a].at[mine], local.at[a]) for a in range(n)]
        rem = [_remote(in_refs[a], out_refs[a].at[mine], send, recv, 3 * a + j, (px, py, c))
               for j, (px, py) in enumerate(chips) for a in range(n)]
        return rem, loc

    return _Plan(blocks, [jax.ShapeDtypeStruct((N_DEV,) + b.shape, b.dtype) for b in blocks], 3 * n, n, copies)


def _plan_gather_d2d(gathered):
    n = len(gathered)

    def copies(in_refs, out_refs, send, recv, local):
        x, y, c, _ = _place()
        rem = [_remote(in_refs[a].at[2 * j + c], out_refs[a].at[2 * j + c], send, recv, 4 * a + j, (x, y, 1 - c))
               for a in range(n) for j in range(4)]
        return rem, []

    return _Plan(gathered, [jax.ShapeDtypeStruct(g.shape, g.dtype) for g in gathered], 4 * n, 0, copies,
                 aliases={a: a for a in range(n)})


def _plan_reduce_sibling(parts):
    n = len(parts)

    def copies(in_refs, out_refs, send, recv, local):
        x, y, c, _ = _place()
        rem = [_remote(in_refs[a].at[2 * j + (1 - c)], out_refs[a].at[j], send, recv, 4 * a + j, (x, y, 1 - c))
               for a in range(n) for j in range(4)]
        return rem, []

    return _Plan(parts, [jax.ShapeDtypeStruct((4,) + p.shape[1:], p.dtype) for p in parts], 4 * n, 0, copies)


def _plan_reduce_chips(sums):
    n = len(sums)

    def copies(in_refs, out_refs, send, recv, local):
        x, y, c, chips = _place()
        mine = 2 * x + y
        loc = [pltpu.make_async_copy(in_refs[a].at[mine], out_refs[a].at[mine], local.at[a]) for a in range(n)]
        rem = [_remote(in_refs[a].at[2 * px + py], out_refs[a].at[mine], send, recv, 3 * a + k, (px, py, c))
               for k, (px, py) in enumerate(chips) for a in range(n)]
        return rem, loc

    return _Plan(sums, [jax.ShapeDtypeStruct(p.shape, p.dtype) for p in sums], 3 * n, n, copies)


def _add_sibling(parts, got, name):
    _, r, cc = parts.shape
    tr = _pick(r, (512, 256, 128, 64, 32, 16))
    c = lax.axis_index("c")

    def body(c_ref, p_ref, g_ref, o_ref):
        o_ref[...] = (p_ref[...] + g_ref[...]).astype(BF16)

    return pl.pallas_call(
        body, name=name, out_shape=jax.ShapeDtypeStruct((4, r, cc), BF16),
        grid_spec=pltpu.PrefetchScalarGridSpec(
            num_scalar_prefetch=1, grid=(4, r // tr),
            in_specs=[pl.BlockSpec((1, tr, cc), lambda j, i, cr: (2 * j + cr[0], i, 0)),
                      pl.BlockSpec((1, tr, cc), lambda j, i, cr: (j, i, 0))],
            out_specs=pl.BlockSpec((1, tr, cc), lambda j, i, cr: (j, i, 0))),
        compiler_params=_vmem(48),
    )(c.reshape(1).astype(jnp.int32), parts, got)


def _sum_leading(parts, name):
    k, r, cc = parts.shape
    tr = _pick(r, (512, 256, 128, 64, 32, 16, 8))

    def body(p_ref, o_ref):
        acc = p_ref[0]
        for j in range(1, k):
            acc = acc + p_ref[j]
        o_ref[...] = acc

    return pl.pallas_call(
        body, name=name, out_shape=jax.ShapeDtypeStruct((r, cc), F32), grid=(r // tr,),
        in_specs=[pl.BlockSpec((k, tr, cc), lambda i: (0, i, 0))], out_specs=pl.BlockSpec((tr, cc), lambda i: (i, 0)),
    )(parts)


def _adamw_math(w, g, m, v):
    nm = ADAM_B1 * m + (1.0 - ADAM_B1) * g
    nv = ADAM_B2 * v + (1.0 - ADAM_B2) * jnp.square(g)
    m_hat = nm / (1.0 - ADAM_B1 ** ADAM_STEP)
    v_hat = nv / (1.0 - ADAM_B2 ** ADAM_STEP)
    return -ADAM_LR * (m_hat / (jnp.sqrt(v_hat) + ADAM_EPS) + ADAM_WD * w), nm, nv


def _adamw_sum(w, contribs, m, v, name, ride=None):
    nl = len(contribs)
    k, r, cc = contribs[0].shape
    tr = _pick(r, (256, 128, 64, 32, 16))
    nb = r // tr

    def body(w_ref, *rest):
        c_refs = rest[:nl]
        m_ref, v_ref, g_ref, d_ref, nm_ref, nv_ref = rest[nl:]
        for li in range(nl):
            @pl.when(pl.program_id(0) == li)
            def _(c_ref=c_refs[li]):
                g = c_ref[0].astype(F32)
                for j in range(1, k):
                    g = g + c_ref[j].astype(F32)
                g_ref[...] = g
                d_ref[...], nm_ref[...], nv_ref[...] = _adamw_math(w_ref[...], g, m_ref[...], v_ref[...])

    spec = pl.BlockSpec((tr, cc), lambda l, i: (l * nb + i, 0))
    cspec = pl.BlockSpec((k, tr, cc), lambda l, i: (0, i, 0))
    return _call_with_ride(body, name, (nl, nb), [w, *contribs, m, v], [spec] + [cspec] * nl + [spec, spec],
                           [jax.ShapeDtypeStruct(w.shape, F32)] * 4, [spec] * 4, ("parallel", "parallel"), 48, ride)


def _adamw_many(ws, gs, ms, vs, name):
    n = len(ws)

    def body(*refs):
        w_r, g_r, m_r, v_r = refs[:n], refs[n:2 * n], refs[2 * n:3 * n], refs[3 * n:4 * n]
        d_r, nm_r, nv_r = refs[4 * n:5 * n], refs[5 * n:6 * n], refs[6 * n:7 * n]
        for a in range(n):
            d_r[a][...], nm_r[a][...], nv_r[a][...] = _adamw_math(w_r[a][...], g_r[a][...], m_r[a][...], v_r[a][...])

    shapes = [jax.ShapeDtypeStruct(w.shape, F32) for w in ws]
    res = pl.pallas_call(body, name=name, out_shape=shapes * 3,
                         compiler_params=pltpu.CompilerParams(vmem_limit_bytes=56 * 1024 * 1024))(*ws, *gs, *ms, *vs)
    return res[:n], res[n:2 * n], res[2 * n:]


def _pack_rows(flat, lanes, row_mult):
    n = flat.shape[-1]
    rows = -(-n // lanes)
    rows = -(-rows // row_mult) * row_mult
    pad = rows * lanes - n
    if pad:
        flat = jnp.pad(flat, [(0, 0)] * (flat.ndim - 1) + [(0, pad)])
    return flat.reshape(flat.shape[:-1] + (rows, lanes))


def _rope_tables(positions):
    inv = 1.0 / (ROPE_THETA ** (jnp.arange(0, MLA_ROPE, 2, dtype=F32) / MLA_ROPE))
    ang = positions.astype(F32)[:, None] * inv
    cos, sin = jnp.cos(ang), jnp.sin(ang)
    s = positions.shape[0]
    z = lambda n: jnp.zeros((s, n), F32)
    c = jnp.concatenate([jnp.ones((s, 64), F32), cos, cos, z(32)], axis=1)
    sa = jnp.concatenate([z(64), -sin, z(48)], axis=1)
    sb = jnp.concatenate([z(80), sin, z(32)], axis=1)
    return c, sa, sb


def _full_weights(gathered, names=SHARDED):
    full = {}
    for k, g in zip(names, gathered):
        _, r, c = g.shape
        if k == "w_in":
            full[k] = g
        else:
            full[k] = g.transpose(1, 0, 2).reshape(r, N_DEV * c) if k in COL_SHARDED else g.reshape(N_DEV * r, c)
    return full


EARLY = ("s5_w_glu", "w_branch_out", "w_out")
LATE = ("w_in", "mla_w_q_up", "mla_w_kv_up")


def _owner_major(g, names, tag):
    parts = []
    for k in names:
        if k == "w_in":
            parts.append(_win_unpad(g["w_in"], f"{tag}_w_in_unpad"))
            continue
        big = {"mla_w_q_up": lambda: _unpad_wq(g["wq"]), "mla_w_kv_up": lambda: _unpad_wkv(g["wkv"]), "s5_w_glu": lambda: g["wglu"],
               "w_branch_out": lambda: g["wo"], "w_out": lambda: g["wout"]}[k]()
        r, c = big.shape
        if k in COL_SHARDED:
            parts.append(big.reshape(r, N_DEV, c // N_DEV).transpose(1, 0, 2))
        else:
            parts.append(big.reshape(N_DEV, r // N_DEV, c))
    return parts


def _device_step(x, positions, target, shards, small):
    tabs = _rope_tables(positions)
    box = {}
    q0 = _layer_params(0, _full_weights(_all_gather(shards[0][:3], "gather_weights_l0"), LATE), small)

    def arrived(o):
        box.update(w1_b=o[:5])
        q0.update(_late_weights(_full_weights(o[5:], EARLY)))

    rides = {
        "mla_attn_fwd": _Ride(lambda: _plan_gather_ici(shards[1][:1]), lambda o: box.update(ici_a=o)),
        "fox_attn_fwd": _Ride(lambda: _join([_plan_gather_ici(shards[1][1:]), _plan_gather_d2d(box["ici_a"]),
                                             _plan_gather_ici(shards[0][3:])]),
                              lambda o: box.update(ici_b=o[:5], w1_a=o[5:6], ici_0=o[6:])),
        "s5_scan_fwd": _Ride(lambda: _join([_plan_gather_d2d(box["ici_b"]), _plan_gather_d2d(box["ici_0"])]), arrived),
    }
    h, sv0 = _layer_fwd(0, x, tabs, q0, rides)
    q1 = _layer_params(1, _full_weights(list(box["w1_a"]) + list(box["w1_b"])), small)
    h, sv1 = _layer_fwd(1, h, tabs, q1)
    loss, d = _loss_head(h, target, "loss_head")
    d, g1 = _layer_bwd(1, d, tabs, q1, sv1)
    parts1 = _owner_major(g1, SHARDED, "l1")

    def chips_plan(names, parts, got, tag):
        return _plan_reduce_chips([_add_sibling(p, g, f"reduce_add_{tag}_{k}") for k, p, g in zip(names, parts, got)])

    g0 = {}
    rides = {
        "s5_scan_bwd": _Ride(lambda: _join([_plan_reduce_sibling(parts1),
                                            _plan_reduce_sibling(box.setdefault("early0", _owner_major(g0, EARLY, "l0")))]),
                             lambda o: box.update(got1=o[:6], got0e=o[6:])),
        "fox_attn_bwd": _Ride(lambda: chips_plan(SHARDED, parts1, box["got1"], "l1"), lambda o: box.update(contribs1=o)),
        "mla_attn_bwd": _Ride(lambda: chips_plan(EARLY, box["early0"], box["got0e"], "l0"), lambda o: box.update(contribs0e=o)),
        "dh1": _Ride(lambda: _plan_reduce_sibling(box.setdefault("late0", _owner_major(g0, LATE, "l0"))),
                     lambda o: box.update(got0l=o)),
        "dh3": _Ride(lambda: chips_plan(LATE, box["late0"], box["got0l"], "l0"), lambda o: box.update(contribs0l=o)),
    }
    d, _ = _layer_bwd(0, d, tabs, q0, sv0, rides, g0)
    contribs0 = list(box["contribs0l"]) + list(box["contribs0e"])
    return loss[0, 0], d, [g0, g1], [contribs0, box["contribs1"]]


def kernel(x, positions, norm_g, w_in, mla_q_a_norm, mla_w_q_up, mla_kv_a_norm, mla_w_kv_up, mla_q_norm, mla_k_norm, fox_b_f, fox_q_norm, fox_k_norm, s5_lambda_re, s5_lambda_im, s5_log_dt, s5_b_re, s5_b_im, s5_c_re, s5_c_im, s5_d, s5_w_glu, s5_b_glu, w_branch_out, w_out, loss_target, m_norm_g, m_w_in, m_mla_q_a_norm, m_mla_w_q_up, m_mla_kv_a_norm, m_mla_w_kv_up, m_mla_q_norm, m_mla_k_norm, m_fox_b_f, m_fox_q_norm, m_fox_k_norm, m_s5_lambda_re, m_s5_lambda_im, m_s5_log_dt, m_s5_b_re, m_s5_b_im, m_s5_c_re, m_s5_c_im, m_s5_d, m_s5_w_glu, m_s5_b_glu, m_w_branch_out, m_w_out, v_norm_g, v_w_in, v_mla_q_a_norm, v_mla_w_q_up, v_mla_kv_a_norm, v_mla_w_kv_up, v_mla_q_norm, v_mla_k_norm, v_fox_b_f, v_fox_q_norm, v_fox_k_norm, v_s5_lambda_re, v_s5_lambda_im, v_s5_log_dt, v_s5_b_re, v_s5_b_im, v_s5_c_re, v_s5_c_im, v_s5_d, v_s5_w_glu, v_s5_b_glu, v_w_branch_out, v_w_out):
    env = dict(locals())
    wts = {k: env[k] for k in WEIGHTS}
    mom = {k: env["m_" + k] for k in WEIGHTS}
    var = {k: env["v_" + k] for k in WEIGHTS}

    shards = [[wts[k][l].astype(BF16) for k in SHARDED] for l in range(DEPTH)]
    small = {k: wts[k] for k in SMALL}
    loss, dx, grads, contribs = _device_step(x[0], positions[0], loss_target[0], shards, small)
    loss = lax.psum(loss, ("x", "y", "c"))

    two_d = {k: (wts[k].shape[0] * wts[k].shape[1], wts[k].shape[2]) for k in SHARDED}
    sm = {k: jnp.stack([g[k] for g in grads]).reshape(wts[k].shape) for k in SMALL}
    small_block = _pack_rows(jnp.concatenate([sm[k].reshape(-1) for k in SMALL]), LANES, 256)
    box = {}
    rides = {SHARDED[0]: _Ride(lambda: _plan_gather_ici([small_block]), lambda o: box.update(ici=o)),
             SHARDED[1]: _Ride(lambda: _plan_gather_d2d(box["ici"]), lambda o: box.update(all=o))}
    grad_out, delta_out, m_out, v_out = {}, {}, {}, {}
    for i, k in enumerate(SHARDED):
        shp = wts[k].shape
        res = _adamw_sum(wts[k].reshape(two_d[k]), [contribs[l][i] for l in range(DEPTH)], mom[k].reshape(two_d[k]),
                         var[k].reshape(two_d[k]), f"adamw_{k}", ride=rides.get(k))
        grad_out[k], delta_out[k], m_out[k], v_out[k] = (z.reshape(shp) for z in res)
    g_small = _sum_leading(box["all"][0], "sum_small_grads").reshape(-1)
    off = 0
    for k in SMALL:
        cnt = int(np.prod(wts[k].shape))
        grad_out[k] = g_small[off:off + cnt].reshape(wts[k].shape)
        off += cnt
    flat2 = lambda a: a.reshape(-1, a.shape[-1])
    d_s, m_s, v_s = _adamw_many([flat2(wts[k]) for k in SMALL], [flat2(grad_out[k]) for k in SMALL],
                                [flat2(mom[k]) for k in SMALL], [flat2(var[k]) for k in SMALL], "adamw_small")
    for i, k in enumerate(SMALL):
        delta_out[k], m_out[k], v_out[k] = (z[i].reshape(wts[k].shape) for z in (d_s, m_s, v_s))

    return (loss, dx[None], *[grad_out[k] for k in WEIGHTS], *[delta_out[k] for k in WEIGHTS],
            *[m_out[k] for k in WEIGHTS], *[v_out[k] for k in WEIGHTS])
```

```python
import functools
import math

import jax
import jax.numpy as jnp
import numpy as np
from jax import lax
from jax.experimental import pallas as pl
from jax.experimental.pallas import tpu as pltpu

F32 = jnp.float32
BF16 = jnp.bfloat16

D_MODEL = 1024
DEPTH = 2
CHUNK = 64
EPS = 1e-6
HEADS = 8
MLA_NOPE, MLA_ROPE, MLA_V = 64, 32, 64
MLA_Q_RANK, MLA_KV_RANK = 256, 128
MLA_QK = MLA_NOPE + MLA_ROPE
ROPE_THETA = 10000.0
FOX_DIM = 64
S5_WIDTH, S5_GROUP, S5_GROUPS, S5_STATE = 512, 16, 32, 64
S5_LANES = S5_GROUPS * S5_STATE
IN_WIDTH = 7080
N_DEV = 8
LANES = 128
SUBLANES = 8

ADAM_LR, ADAM_B1, ADAM_B2, ADAM_EPS, ADAM_WD, ADAM_STEP = 0.001, 0.9, 0.999, 1e-08, 0.01, 10

SEG_W = (512, 1536, 512, 4608)
SEG_OFF = (0, 512, 2048, 2560)
NSEG = len(SEG_W)
GATES_W = 1536
PAD_IN = 7168
NEG = -1e30

SHARDED = ("w_in", "mla_w_q_up", "mla_w_kv_up", "s5_w_glu", "w_branch_out", "w_out")
COL_SHARDED = ("w_in", "mla_w_q_up", "mla_w_kv_up")
SMALL = ("norm_g", "mla_q_a_norm", "mla_kv_a_norm", "mla_q_norm", "mla_k_norm", "fox_b_f", "fox_q_norm", "fox_k_norm",
         "s5_lambda_re", "s5_lambda_im", "s5_log_dt", "s5_b_re", "s5_b_im", "s5_c_re", "s5_c_im", "s5_d", "s5_b_glu")
WEIGHTS = ("norm_g", "w_in", "mla_q_a_norm", "mla_w_q_up", "mla_kv_a_norm", "mla_w_kv_up", "mla_q_norm", "mla_k_norm",
           "fox_b_f", "fox_q_norm", "fox_k_norm", "s5_lambda_re", "s5_lambda_im", "s5_log_dt", "s5_b_re", "s5_b_im",
           "s5_c_re", "s5_c_im", "s5_d", "s5_w_glu", "s5_b_glu", "w_branch_out", "w_out")


def _pick(n, cands):
    for c in cands:
        if n % c == 0:
            return c
    return n


def _vmem(mb):
    return pltpu.CompilerParams(vmem_limit_bytes=mb * 1024 * 1024)


def _dot(a, b, dims):
    return lax.dot_general(a.astype(BF16), b.astype(BF16), (dims, ((), ())), preferred_element_type=F32)


def _nn(a, b):
    return _dot(a, b, ((1,), (0,)))


def _nt(a, b):
    return _dot(a, b, ((1,), (1,)))


def _tn(a, b):
    return _dot(a, b, ((0,), (0,)))


def _rms(x, g, n=None):
    n = x.shape[-1] if n is None else n
    return x * lax.rsqrt(jnp.sum(x * x, axis=-1, keepdims=True) / n + EPS) * g


def _rope(t, c, sa, sb):
    return t * c + pltpu.roll(t, LANES - 16, 1) * sa + pltpu.roll(t, 16, 1) * sb


def _rope_t(d, c, sa, sb):
    return d * c + pltpu.roll(d * sa, 16, 1) + pltpu.roll(d * sb, LANES - 16, 1)


def _mm(a, b, mode, name, acc=None, b_cols=None, ride=None):
    if mode == "tn":
        kd, m = a.shape
    else:
        m, kd = a.shape
    b_off, b_w = b_cols if b_cols is not None else (0, b.shape[1])
    n = b.shape[0] if mode == "nt" else b_w
    tm, tn, tk = _pick(m, (1024, 512, 256, 128)), _pick(n, (1024, 512, 256, 128)), _pick(kd, (1024, 512, 256, 128))
    nk = kd // tk
    if mode == "tn":
        a_spec = pl.BlockSpec((tk, tm), lambda i, j, k: (k, i))
    else:
        a_spec = pl.BlockSpec((tm, tk), lambda i, j, k: (i, k))
    if mode == "nt":
        assert b_off % tk == 0
        b_spec = pl.BlockSpec((tn, tk), lambda i, j, k: (j, k + b_off // tk))
    else:
        assert b_off % tn == 0
        b_spec = pl.BlockSpec((tk, tn), lambda i, j, k: (k, j + b_off // tn))
    dims = {"nn": ((1,), (0,)), "nt": ((1,), (1,)), "tn": ((0,), (0,))}[mode]
    o_spec = pl.BlockSpec((tm, tn), lambda i, j, k: (i, j))
    has_acc = acc is not None

    def body(*refs):
        if has_acc:
            a_ref, b_ref, c_ref, o_ref = refs
        else:
            a_ref, b_ref, o_ref = refs
        k = pl.program_id(2)
        prod = _dot(a_ref[...], b_ref[...], dims)

        @pl.when(k == 0)
        def _():
            o_ref[...] = prod + c_ref[...] if has_acc else prod

        @pl.when(k > 0)
        def _():
            o_ref[...] += prod

    ins = [a, b] + ([acc] if has_acc else [])
    in_specs = [a_spec, b_spec] + ([o_spec] if has_acc else [])
    return _call_with_ride(body, name, (m // tm, n // tn, nk), ins, in_specs, [jax.ShapeDtypeStruct((m, n), F32)], [o_spec],
                           ("parallel", "parallel", "arbitrary"), 48, ride)[0]


def _stage(name, fn, n_steps, ins, outs, accs=(), scratch=(), vmem_mb=48):
    n_in, n_out, n_acc = len(ins), len(outs), len(accs)

    def body(*refs):
        in_refs = refs[:n_in]
        out_refs = refs[n_in:n_in + n_out]
        acc_refs = refs[n_in + n_out:n_in + n_out + n_acc]
        scr = refs[n_in + n_out + n_acc:]
        if n_acc:
            @pl.when(pl.program_id(0) == 0)
            def _():
                for r in acc_refs:
                    r[...] = jnp.zeros(r.shape, r.dtype)
        fn(in_refs, out_refs, acc_refs, scr)

    acc_specs = [pl.BlockSpec(a.shape, functools.partial(lambda i, nd: (0,) * nd, nd=len(a.shape))) for a in accs]
    res = pl.pallas_call(
        body, name=name, grid=(n_steps,),
        in_specs=[s for _, s in ins], out_specs=[s for _, s in outs] + acc_specs,
        out_shape=[s for s, _ in outs] + list(accs), scratch_shapes=list(scratch),
        compiler_params=pltpu.CompilerParams(dimension_semantics=("arbitrary",),
                                             vmem_limit_bytes=vmem_mb * 1024 * 1024),
    )(*[a for a, _ in ins])
    return res


def _rows(ts, w, j=0):
    return pl.BlockSpec((ts, w), lambda i: (i, j))


def _rows_rev(ts, w, n, j=0):
    return pl.BlockSpec((ts, w), lambda i: (n - 1 - i, j))


def _heads(ts, d):
    return pl.BlockSpec((HEADS, ts, d), lambda i: (0, i, 0))


def _heads_rev(ts, d, n):
    return pl.BlockSpec((HEADS, ts, d), lambda i: (0, n - 1 - i, 0))


def _full(shape):
    nd = len(shape)
    return pl.BlockSpec(tuple(shape), lambda i: (0,) * nd)


def _sds(shape, dtype=F32):
    return jax.ShapeDtypeStruct(tuple(shape), dtype)


def _norm_fwd(x, g, name):
    s = x.shape[0]
    ts = _pick(s, (256, 128))

    def fn(ins, outs, accs, scr):
        outs[0][...] = _rms(ins[0][...], ins[1][...]).astype(BF16)

    return _stage(name, fn, s // ts, [(x, _rows(ts, D_MODEL)), (g, _full(g.shape))],
                  [(_sds((s, D_MODEL), BF16), _rows(ts, D_MODEL))])[0]


def _norm_bwd(x, g, dh, dres, name):
    s = x.shape[0]
    ts = _pick(s, (256, 128))

    def fn(ins, outs, accs, scr):
        _, vjp = jax.vjp(_rms, ins[0][...], ins[1][...])
        dx, dg = vjp(ins[2][...])
        outs[0][...] = dx + ins[3][...]
        accs[0][...] += dg

    r = _stage(name, fn, s // ts,
               [(x, _rows(ts, D_MODEL)), (g, _full(g.shape)), (dh, _rows(ts, D_MODEL)), (dres, _rows(ts, D_MODEL))],
               [(_sds((s, D_MODEL)), _rows(ts, D_MODEL))], accs=[_sds((1, D_MODEL))])
    return r[0], r[1]


def _mla_q(qraw, c, sa, sb, qn):
    return _rms(_rope(qraw, c, sa, sb), qn, MLA_QK)


def _mla_prep_fwd(seg0, tabs, p, wq, wkv, name):
    s = seg0.shape[0]
    ts = _pick(s, (256, 128))

    def fn(ins, outs, accs, scr):
        blk, cos, sa, sb, qan, kvan, qn, kn, wq_r, wkv_r = ins
        b = blk[...]
        cq, ckv, kt = b[:, :256], b[:, 256:384], b[:, 384:512]
        lane = lax.broadcasted_iota(jnp.int32, kt.shape, 1)
        kpe = jnp.where(lane >= 64, kt, 0.0)
        q_raw = _nn(_rms(cq, qan[...]), wq_r[...])
        kv_raw = _nn(_rms(ckv, kvan[...]), wkv_r[...])
        c, a, bb = cos[...], sa[...], sb[...]
        for h in range(HEADS):
            outs[0][h] = _mla_q(q_raw[:, LANES * h:LANES * (h + 1)], c, a, bb, qn[...]).astype(BF16)
            outs[1][h] = _mla_q(kv_raw[:, LANES * h:LANES * (h + 1)] + kpe, c, a, bb, kn[...]).astype(BF16)
            outs[2][h] = kv_raw[:, 1024 + 64 * h:1024 + 64 * (h + 1)].astype(BF16)

    consts = [p["mla_q_a_norm"], p["mla_kv_a_norm"], p["mla_q_norm"], p["mla_k_norm"], wq, wkv]
    return _stage(name, fn, s // ts,
                  [(seg0, _rows(ts, 512))] + [(t, _rows(ts, LANES)) for t in tabs] + [(a, _full(a.shape)) for a in consts],
                  [(_sds((HEADS, s, LANES), BF16), _heads(ts, LANES)), (_sds((HEADS, s, LANES), BF16), _heads(ts, LANES)),
                   (_sds((HEADS, s, 64), BF16), _heads(ts, 64))])


def _mla_prep_bwd(seg0, tabs, p, wq, wkv, dq, dk, dv, dff, name):
    s = seg0.shape[0]
    ts = _pick(s, (256, 128))

    def fn(ins, outs, accs, scr):
        blk, cos, sa, sb, qan, kvan, qn, kn, wq_r, wkv_r, dq_r, dk_r, dv_r, dff_r = ins
        dqan, dkvan, dqn, dkn, dwq, dwkv = accs
        dqraw_s, dkvraw_s = scr
        b = blk[...]
        cq, ckv, kt = b[:, :256], b[:, 256:384], b[:, 384:512]
        lane = lax.broadcasted_iota(jnp.int32, kt.shape, 1)
        kpe = jnp.where(lane >= 64, kt, 0.0)
        cqn, vjp_cq = jax.vjp(_rms, cq, qan[...])
        ckvn, vjp_ckv = jax.vjp(_rms, ckv, kvan[...])
        q_raw = _nn(cqn, wq_r[...])
        kv_raw = _nn(ckvn, wkv_r[...])
        c, a, bb = cos[...], sa[...], sb[...]

        def head_bwd(raw, gain, d):
            t = _rope(raw, c, a, bb)
            _, vjp = jax.vjp(functools.partial(_rms, n=MLA_QK), t, gain)
            dt, dgain = vjp(d)
            return _rope_t(dt, c, a, bb), dgain

        dkpe = jnp.zeros(kt.shape, F32)
        for h in range(HEADS):
            dqh, dg = head_bwd(q_raw[:, LANES * h:LANES * (h + 1)], qn[...], dq_r[h])
            dqn[...] += dg
            dqraw_s[:, LANES * h:LANES * (h + 1)] = dqh
            dkh, dg = head_bwd(kv_raw[:, LANES * h:LANES * (h + 1)] + kpe, kn[...], dk_r[h])
            dkn[...] += dg
            dkvraw_s[:, LANES * h:LANES * (h + 1)] = dkh
            dkpe = dkpe + dkh
            dkvraw_s[:, 1024 + 64 * h:1024 + 64 * (h + 1)] = dv_r[h]
        dq_raw = dqraw_s[...]
        dkv_raw = dkvraw_s[...]
        dwq[...] += _tn(cqn, dq_raw)
        dwkv[...] += _tn(ckvn, dkv_raw)
        dcq, dg = vjp_cq(_nt(dq_raw, wq_r[...]))
        dqan[...] += dg
        dckv, dg = vjp_ckv(_nt(dkv_raw, wkv_r[...]))
        dkvan[...] += dg
        outs[0][:, 0:256] = dcq.astype(BF16)
        outs[0][:, 256:384] = dckv.astype(BF16)
        outs[0][:, 384:512] = (jnp.where(lane >= 64, dkpe, 0.0) + dff_r[...]).astype(BF16)

    consts = [p["mla_q_a_norm"], p["mla_kv_a_norm"], p["mla_q_norm"], p["mla_k_norm"], wq, wkv]
    return _stage(name, fn, s // ts,
                  [(seg0, _rows(ts, 512))] + [(t, _rows(ts, LANES)) for t in tabs] + [(a, _full(a.shape)) for a in consts]
                  + [(dq, _heads(ts, LANES)), (dk, _heads(ts, LANES)), (dv, _heads(ts, 64)), (dff, _rows(ts, LANES))],
                  [(_sds((s, 512), BF16), _rows(ts, 512))],
                  accs=[_sds((1, 256)), _sds((1, 128)), _sds((1, LANES)), _sds((1, LANES)), _sds(wq.shape), _sds(wkv.shape)],
                  scratch=[pltpu.VMEM((ts, 1024), F32), pltpu.VMEM((ts, 1536), F32)])


def _fox_prep_fwd(seg0, seg1, bf, qn, kn, name):
    s = seg0.shape[0]
    ts = _pick(s, (256, 128))
    steps = int(math.log2(ts))

    def fn(ins, outs, accs, scr):
        kt_r, x_r, bf_r, qn_r, kn_r = ins
        carry = scr[0]

        @pl.when(pl.program_id(0) == 0)
        def _():
            carry[...] = jnp.zeros(carry.shape, F32)

        x = x_r[...]
        for h in range(HEADS):
            outs[0][h] = _rms(x[:, 64 * h:64 * (h + 1)], qn_r[...]).astype(BF16)
            outs[1][h] = _rms(x[:, 512 + 64 * h:512 + 64 * (h + 1)], kn_r[...]).astype(BF16)
            outs[2][h] = x[:, 1024 + 64 * h:1024 + 64 * (h + 1)].astype(BF16)
        kt = kt_r[...]
        lane = lax.broadcasted_iota(jnp.int32, kt.shape, 1)
        row = lax.broadcasted_iota(jnp.int32, kt.shape, 0)
        cs = jnp.where(lane < HEADS, jax.nn.log_sigmoid(kt + bf_r[...]), 0.0)
        for k in range(steps):
            sh = 1 << k
            cs = cs + jnp.where(row >= sh, pltpu.roll(cs, sh, 0), 0.0)
        cs = cs + carry[0:1, :]
        outs[3][...] = cs
        outs[4][...] = cs.T[0:HEADS, :]
        carry[0:1, :] = cs[ts - 1:ts, :]

    return _stage(name, fn, s // ts,
                  [(seg0, _rows(ts, LANES, 3)), (seg1, _rows(ts, 1536)), (bf, _full(bf.shape)), (qn, _full(qn.shape)),
                   (kn, _full(kn.shape))],
                  [(_sds((HEADS, s, 64), BF16), _heads(ts, 64)), (_sds((HEADS, s, 64), BF16), _heads(ts, 64)),
                   (_sds((HEADS, s, 64), BF16), _heads(ts, 64)), (_sds((s, LANES)), _rows(ts, LANES)),
                   (_sds((HEADS, s)), pl.BlockSpec((HEADS, ts), lambda i: (0, i)))],
                  scratch=[pltpu.VMEM((SUBLANES, LANES), F32)])


def _fox_prep_bwd(seg0, seg1, bf, qn, kn, dq, dk, dv, dck, name):
    s = seg0.shape[0]
    ts = _pick(s, (256, 128))
    n = s // ts
    steps = int(math.log2(ts))

    def fn(ins, outs, accs, scr):
        kt_r, x_r, bf_r, qn_r, kn_r, dq_r, dk_r, dv_r, dck_r = ins
        dqn, dkn, dbf = accs
        carry, dbuf = scr

        @pl.when(pl.program_id(0) == 0)
        def _():
            carry[...] = jnp.zeros(carry.shape, F32)

        x = x_r[...]
        for h in range(HEADS):
            _, vjp = jax.vjp(_rms, x[:, 64 * h:64 * (h + 1)], qn_r[...])
            d, dg = vjp(dq_r[h])
            dbuf[:, 64 * h:64 * (h + 1)] = d
            dqn[...] += dg
            _, vjp = jax.vjp(_rms, x[:, 512 + 64 * h:512 + 64 * (h + 1)], kn_r[...])
            d, dg = vjp(dk_r[h])
            dbuf[:, 512 + 64 * h:512 + 64 * (h + 1)] = d
            dkn[...] += dg
            dbuf[:, 1024 + 64 * h:1024 + 64 * (h + 1)] = dv_r[h]
        outs[0][...] = dbuf[...].astype(BF16)
        dc = dck_r[...].reshape(HEADS, ts)
        dc = jnp.concatenate([dc, jnp.zeros((LANES - HEADS, ts), F32)], axis=0).T
        row = lax.broadcasted_iota(jnp.int32, dc.shape, 0)
        lane = lax.broadcasted_iota(jnp.int32, dc.shape, 1)
        for k in range(steps):
            sh = 1 << k
            dc = dc + jnp.where(row < ts - sh, pltpu.roll(dc, ts - sh, 0), 0.0)
        dc = dc + carry[0:1, :]
        carry[0:1, :] = dc[0:1, :]
        dff = jnp.where(lane < HEADS, dc * jax.nn.sigmoid(-(kt_r[...] + bf_r[...])), 0.0)
        outs[1][...] = dff
        dbf[...] += jnp.sum(dff, axis=0, keepdims=True)

    return _stage(name, fn, n,
                  [(seg0, _rows_rev(ts, LANES, n, 3)), (seg1, _rows_rev(ts, 1536, n)), (bf, _full(bf.shape)),
                   (qn, _full(qn.shape)), (kn, _full(kn.shape)), (dq, _heads_rev(ts, 64, n)), (dk, _heads_rev(ts, 64, n)),
                   (dv, _heads_rev(ts, 64, n)), (dck, pl.BlockSpec((HEADS, 1, ts), lambda i: (0, 0, n - 1 - i)))],
                  [(_sds((s, 1536), BF16), _rows_rev(ts, 1536, n)), (_sds((s, LANES)), _rows_rev(ts, LANES, n))],
                  accs=[_sds((1, 64)), _sds((1, 64)), _sds((1, LANES))],
                  scratch=[pltpu.VMEM((SUBLANES, LANES), F32), pltpu.VMEM((ts, 1536), F32)])


def _allowed(i, j, t, chunk_causal):
    qpos = i * t + lax.broadcasted_iota(jnp.int32, (t, t), 0)
    kpos = j * t + lax.broadcasted_iota(jnp.int32, (t, t), 1)
    if chunk_causal:
        return (kpos // CHUNK) <= (qpos // CHUNK)
    return kpos <= qpos


def _pick_col(c_blk, h):
    lane = lax.broadcasted_iota(jnp.int32, c_blk.shape, 1)
    return jnp.sum(jnp.where(lane == h, c_blk, 0.0), axis=1, keepdims=True)


class _Ride:
    def __init__(self, make, take):
        self.make, self.take = make, take


class _Plan:
    def __init__(self, ins, out_shapes, n_remote, n_local, copies, aliases=None):
        self.ins, self.out_shapes, self.n_remote, self.n_local = list(ins), list(out_shapes), n_remote, n_local
        self.copies, self.aliases = copies, dict(aliases or {})

    def scratch(self):
        return [pltpu.SemaphoreType.DMA((self.n_remote,)), pltpu.SemaphoreType.DMA((self.n_remote,)),
                pltpu.SemaphoreType.DMA((max(self.n_local, 1),))]

    def start(self, in_refs, out_refs, sems):
        remote, local = self.copies(in_refs, out_refs, *sems)
        for cp in local + remote:
            cp.start()

    def wait(self, in_refs, out_refs, sems):
        remote, local = self.copies(in_refs, out_refs, *sems)
        for cp in remote:
            cp.wait()
        for cp in local:
            cp.wait()


class _Off:
    def __init__(self, ref, off):
        self.ref, self.off, self.at = ref, off, self

    def __getitem__(self, k):
        return self.ref.at[k + self.off]


def _join(plans):
    ins = [a for p in plans for a in p.ins]
    outs = [o for p in plans for o in p.out_shapes]
    aliases, i0, o0 = {}, 0, 0
    for p in plans:
        aliases.update({i0 + i: o0 + o for i, o in p.aliases.items()})
        i0, o0 = i0 + len(p.ins), o0 + len(p.out_shapes)

    def copies(in_refs, out_refs, send, recv, local):
        rem, loc, i0, o0, r0, l0 = [], [], 0, 0, 0, 0
        for p in plans:
            r, l = p.copies(in_refs[i0:i0 + len(p.ins)], out_refs[o0:o0 + len(p.out_shapes)], _Off(send, r0), _Off(recv, r0),
                            _Off(local, l0))
            rem, loc = rem + r, loc + l
            i0, o0, r0, l0 = i0 + len(p.ins), o0 + len(p.out_shapes), r0 + p.n_remote, l0 + p.n_local
        return rem, loc

    return _Plan(ins, outs, sum(p.n_remote for p in plans), sum(p.n_local for p in plans), copies, aliases)


def _call_with_ride(core, name, grid, ins, in_specs, out_shape, out_specs, semantics, vmem_mb, ride, scratch=()):
    n_in, n_out, n_scr = len(ins), len(out_shape), len(scratch)
    if ride is None:
        return pl.pallas_call(
            core, name=name, grid=grid, in_specs=in_specs, out_specs=out_specs, out_shape=out_shape,
            scratch_shapes=list(scratch),
            compiler_params=pltpu.CompilerParams(dimension_semantics=semantics, vmem_limit_bytes=vmem_mb * 1024 * 1024),
        )(*ins)
    plan = ride.make()
    ci, co = len(plan.ins), len(plan.out_shapes)

    def body(*refs):
        c_in = refs[n_in:n_in + ci]
        a_out = refs[n_in + ci:n_in + ci + n_out]
        c_out = refs[n_in + ci + n_out:n_in + ci + n_out + co]
        own = refs[n_in + ci + n_out + co:n_in + ci + n_out + co + n_scr]
        sems = refs[n_in + ci + n_out + co + n_scr:]
        ids = [pl.program_id(d) for d in range(len(grid))]
        first = functools.reduce(jnp.logical_and, [i == 0 for i in ids])
        last = functools.reduce(jnp.logical_and, [i == g - 1 for i, g in zip(ids, grid)])

        @pl.when(first)
        def _():
            plan.start(c_in, c_out, sems)

        core(*refs[:n_in], *a_out, *own)

        @pl.when(last)
        def _():
            plan.wait(c_in, c_out, sems)

    res = pl.pallas_call(
        body, name=name, grid=grid, in_specs=list(in_specs) + [ANY] * ci, out_specs=list(out_specs) + [ANY] * co,
        out_shape=list(out_shape) + plan.out_shapes, scratch_shapes=list(scratch) + plan.scratch(),
        input_output_aliases={n_in + i: n_out + o for i, o in plan.aliases.items()},
        compiler_params=pltpu.CompilerParams(dimension_semantics=("arbitrary",) * len(grid),
                                             vmem_limit_bytes=vmem_mb * 1024 * 1024),
    )(*ins, *plan.ins)
    ride.take(res[n_out:])
    return res[:n_out]


def _attn_fwd(q, k, v, scale, chunk_causal, name, c=None, ct=None, hps=8, ride=None):
    _, s, dk = q.shape
    dv = v.shape[2]
    t = _pick(s, (256, 128))
    bias = c is not None

    def body(*refs):
        if bias:
            q_ref, k_ref, v_ref, c_ref, ct_ref, o_ref, lse_ref = refs
        else:
            q_ref, k_ref, v_ref, o_ref, lse_ref = refs
        hp, i = pl.program_id(0), pl.program_id(1)
        qb = [q_ref[e] for e in range(hps)]
        cq = [_pick_col(c_ref[...], hp * hps + e) if bias else None for e in range(hps)]

        def step(j, carry, diagonal):
            off = pl.multiple_of(j * t, t)
            out = []
            for e in range(hps):
                m, l, acc = carry[e]
                sc = _nt(qb[e], k_ref[e, pl.ds(off, t), :]) * scale
                if bias:
                    sc = sc + (cq[e] - ct_ref[pl.ds(hp * hps + e, 1), pl.ds(off, t)])
                if diagonal:
                    sc = jnp.where(_allowed(i, j, t, chunk_causal), sc, NEG)
                m_new = jnp.maximum(m, jnp.max(sc, axis=1, keepdims=True))
                pr = jnp.exp(sc - m_new)
                alpha = jnp.exp(m - m_new)
                out.append((m_new, alpha * l + jnp.sum(pr, axis=1, keepdims=True),
                            alpha * acc + _nn(pr, v_ref[e, pl.ds(off, t), :])))
            return tuple(out)

        init = tuple((jnp.full((t, 1), NEG, F32), jnp.zeros((t, 1), F32), jnp.zeros((t, dv), F32)) for _ in range(hps))
        res = step(i, lax.fori_loop(0, i, functools.partial(step, diagonal=False), init), True)
        for e in range(hps):
            m, l, acc = res[e]
            o_ref[e] = acc / l
            lse_ref[e] = m + jnp.log(l)

    ins = [q, k, v] + ([c, ct] if bias else [])
    in_specs = [pl.BlockSpec((hps, t, dk), lambda h, i: (h, i, 0)), pl.BlockSpec((hps, s, dk), lambda h, i: (h, 0, 0)),
                pl.BlockSpec((hps, s, dv), lambda h, i: (h, 0, 0))]
    if bias:
        in_specs += [pl.BlockSpec((t, LANES), lambda h, i: (i, 0)), pl.BlockSpec((HEADS, s), lambda h, i: (0, 0))]
    return _call_with_ride(
        body, name, (HEADS // hps, s // t), ins, in_specs, [_sds((HEADS, s, dv)), _sds((HEADS, s, 1))],
        [pl.BlockSpec((hps, t, dv), lambda h, i: (h, i, 0)), pl.BlockSpec((hps, t, 1), lambda h, i: (h, i, 0))],
        ("parallel", "parallel"), 48, ride)


def _attn_bwd_rows(q, k, v, do, lse, scale, chunk_causal, name, c=None, ct=None, hps=4, ride=None):
    _, s, dk = q.shape
    dv = v.shape[2]
    t = _pick(s, (256, 128))
    n = s // t
    bias = c is not None

    def body(*refs):
        if bias:
            q_ref, k_ref, v_ref, do_ref, lse_ref, c_ref, ct_ref, dq_ref, dk_ref, dv_ref, dck_ref, p_s, dp_s = refs
        else:
            q_ref, k_ref, v_ref, do_ref, lse_ref, dq_ref, dk_ref, dv_ref, p_s, dp_s = refs
        hp, i = pl.program_id(0), pl.program_id(1)

        @pl.when(i == 0)
        def _():
            dk_ref[...] = jnp.zeros(dk_ref.shape, F32)
            dv_ref[...] = jnp.zeros(dv_ref.shape, F32)
            if bias:
                dck_ref[...] = jnp.zeros(dck_ref.shape, F32)

        qb = [q_ref[e] for e in range(hps)]
        dob = [do_ref[e] for e in range(hps)]
        lse_b = [lse_ref[e] for e in range(hps)]
        cq = [_pick_col(c_ref[...], hp * hps + e) if bias else None for e in range(hps)]

        def sweep1(j, acc, diagonal):
            off = pl.multiple_of(j * t, t)
            out = []
            for e in range(hps):
                sc = _nt(qb[e], k_ref[e, pl.ds(off, t), :]) * scale
                if bias:
                    sc = sc + (cq[e] - ct_ref[pl.ds(hp * hps + e, 1), pl.ds(off, t)])
                pr = jnp.exp(sc - lse_b[e])
                if diagonal:
                    pr = jnp.where(_allowed(i, j, t, chunk_causal), pr, 0.0)
                dp = _nt(dob[e], v_ref[e, pl.ds(off, t), :])
                p_s[e, j] = pr
                dp_s[e, j] = dp
                out.append(acc[e] + jnp.sum(pr * dp, axis=1, keepdims=True))
            return tuple(out)

        zero = tuple(jnp.zeros((t, 1), F32) for _ in range(hps))
        delta = sweep1(i, lax.fori_loop(0, i, functools.partial(sweep1, diagonal=False), zero), True)

        def sweep2(j, dq_acc):
            off = pl.multiple_of(j * t, t)
            out = []
            for e in range(hps):
                pr = p_s[e, j]
                ds = pr * (dp_s[e, j] - delta[e])
                kb = k_ref[e, pl.ds(off, t), :]
                dv_ref[e, pl.ds(off, t), :] += _tn(pr, dob[e])
                dk_ref[e, pl.ds(off, t), :] += _tn(ds, qb[e]) * scale
                if bias:
                    dck_ref[e, j] -= jnp.sum(ds, axis=0, keepdims=True)
                out.append(dq_acc[e] + _nn(ds, kb))
            return tuple(out)

        dq = lax.fori_loop(0, i + 1, sweep2, tuple(jnp.zeros((t, dk), F32) for _ in range(hps)))
        for e in range(hps):
            dq_ref[e] = dq[e] * scale

    full = lambda d: pl.BlockSpec((hps, s, d), lambda h, i: (h, 0, 0))
    blk = lambda d: pl.BlockSpec((hps, t, d), lambda h, i: (h, i, 0))
    ins = [q, k, v, do, lse] + ([c, ct] if bias else [])
    in_specs = [blk(dk), full(dk), full(dv), blk(dv), blk(1)]
    out_specs = [blk(dk), full(dk), full(dv)]
    out_shape = [_sds((HEADS, s, dk)), _sds((HEADS, s, dk)), _sds((HEADS, s, dv))]
    if bias:
        in_specs += [pl.BlockSpec((t, LANES), lambda h, i: (i, 0)), pl.BlockSpec((HEADS, s), lambda h, i: (0, 0))]
        out_specs.append(pl.BlockSpec((hps, n, 1, t), lambda h, i: (h, 0, 0, 0)))
        out_shape.append(_sds((HEADS, n, 1, t)))
    res = _call_with_ride(body, name, (HEADS // hps, n), ins, in_specs, out_shape, out_specs, ("parallel", "arbitrary"), 56,
                          ride, scratch=[pltpu.VMEM((hps, n, t, t), F32), pltpu.VMEM((hps, n, t, t), F32)])
    return (*res[:3], res[3].reshape(HEADS, 1, s)) if bias else tuple(res)


def _s5_disc(lr, li, ldt, br, bi):
    dt = jnp.exp(ldt)
    mag = jnp.exp(lr * dt)
    a_re = mag * jnp.cos(li * dt)
    a_im = mag * jnp.sin(li * dt)
    den = lr * lr + li * li
    f_re = ((a_re - 1.0) * lr + a_im * li) / den
    f_im = (a_im * lr - (a_re - 1.0) * li) / den
    return a_re, a_im, f_re * br - f_im * bi, f_re * bi + f_im * br


def _s5_params_fwd(lr16, li16, ldt16, br2, bi2, name):
    def body(a, b, c, d, e, o0, o1, o2, o3):
        r = _s5_disc(a[...], b[...], c[...], d[...], e[...])
        o0[...], o1[...], o2[...], o3[...] = r

    return pl.pallas_call(body, name=name, out_shape=[_sds((512, 64))] * 4)(lr16, li16, ldt16, br2, bi2)


def _s5_params_bwd(lr16, li16, ldt16, br2, bi2, da_re16, da_im16, dbb_re, dbb_im, name):
    def body(a, b, c, d, e, g0, g1, g2, g3, o_lr, o_li, o_dt, o_br, o_bi):
        _, vjp = jax.vjp(_s5_disc, a[...], b[...], c[...], d[...], e[...])
        dlr, dli, dldt, dbr, dbi = vjp((g0[...], g1[...], g2[...], g3[...]))
        grp = lambda z: z.reshape(S5_GROUPS, S5_GROUP, S5_STATE).sum(axis=1)
        o_lr[...] = grp(dlr)
        o_li[...] = grp(dli)
        o_dt[...] = jnp.sum(grp(dldt), axis=1, keepdims=True)
        o_br[...] = dbr
        o_bi[...] = dbi

    return pl.pallas_call(
        body, name=name, out_shape=[_sds((32, 64)), _sds((32, 64)), _sds((32, 1)), _sds((512, 64)), _sds((512, 64))],
    )(lr16, li16, ldt16, br2, bi2, da_re16, da_im16, dbb_re, dbb_im)


def _cmul(ar, ai, br, bi):
    return ar * br - ai * bi, ar * bi + ai * br


S5_SUPER = 4


def _scan_loop(bre_r, bim_r, ar, ai, ore_r, oim_r, reverse, xre_r=None, xim_r=None):
    s, lw = bre_r.shape
    nt = s // SUBLANES
    with_da = xre_r is not None
    shp = (SUBLANES, lw)
    row = lax.broadcasted_iota(jnp.int32, shp, 0)
    pows = [(ar, ai)]
    for _ in range(SUBLANES - 1):
        pows.append(_cmul(pows[-1][0], pows[-1][1], ar, ai))
    cm_r, cm_i = jnp.zeros(shp, F32), jnp.zeros(shp, F32)
    for r in range(SUBLANES):
        e = (SUBLANES - 1 - r) if reverse else r
        cm_r = jnp.where(row == r, jnp.broadcast_to(pows[e][0], shp), cm_r)
        cm_i = jnp.where(row == r, jnp.broadcast_to(pows[e][1], shp), cm_i)
    steps = [(1, pows[0]), (2, pows[1]), (4, pows[3])]

    def tile(it, carry):
        if with_da:
            c_r, c_i, acc_r, acc_i = carry
        else:
            c_r, c_i = carry
        r = (nt - 1 - it) if reverse else it
        off = pl.multiple_of(r * SUBLANES, SUBLANES)
        xr, xi = bre_r[pl.ds(off, SUBLANES), :], bim_r[pl.ds(off, SUBLANES), :]
        for sh, (pr, pi) in steps:
            if reverse:
                keep = row < SUBLANES - sh
                sr = jnp.where(keep, pltpu.roll(xr, SUBLANES - sh, 0), 0.0)
                si = jnp.where(keep, pltpu.roll(xi, SUBLANES - sh, 0), 0.0)
            else:
                keep = row >= sh
                sr = jnp.where(keep, pltpu.roll(xr, sh, 0), 0.0)
                si = jnp.where(keep, pltpu.roll(xi, sh, 0), 0.0)
            mr, mi = _cmul(pr, pi, sr, si)
            xr, xi = xr + mr, xi + mi
        mr, mi = _cmul(cm_r, cm_i, c_r, c_i)
        xr, xi = xr + mr, xi + mi
        ore_r[pl.ds(off, SUBLANES), :] = xr
        oim_r[pl.ds(off, SUBLANES), :] = xi
        edge = 0 if reverse else SUBLANES - 1
        c_r, c_i = xr[edge:edge + 1, :], xi[edge:edge + 1, :]
        if not with_da:
            return c_r, c_i
        fr, fi = xre_r[pl.ds(off, SUBLANES), :], xim_r[pl.ds(off, SUBLANES), :]
        poff = pl.multiple_of(jnp.maximum(r - 1, 0) * SUBLANES, SUBLANES)
        live = (r > 0).astype(F32)
        pr_last = xre_r[pl.ds(poff, SUBLANES), :][SUBLANES - 1:SUBLANES, :] * live
        pi_last = xim_r[pl.ds(poff, SUBLANES), :][SUBLANES - 1:SUBLANES, :] * live
        sr = jnp.where(row >= 1, pltpu.roll(fr, 1, 0), jnp.broadcast_to(pr_last, shp))
        si = jnp.where(row >= 1, pltpu.roll(fi, 1, 0), jnp.broadcast_to(pi_last, shp))
        return c_r, c_i, acc_r + xr * sr + xi * si, acc_i + xi * sr - xr * si

    z1 = jnp.zeros((1, lw), F32)
    if not with_da:
        lax.fori_loop(0, nt, tile, (z1, z1))
        return None
    _, _, acc_r, acc_i = lax.fori_loop(0, nt, tile, (z1, z1, jnp.zeros(shp, F32), jnp.zeros(shp, F32)))
    return jnp.sum(acc_r, axis=0, keepdims=True), jnp.sum(acc_i, axis=0, keepdims=True)


S5_ROWS = 512


def _group_compact(p):
    grp = lax.broadcasted_iota(jnp.int32, (LANES, S5_STATE), 0) // S5_GROUP
    out = jnp.zeros((LANES, S5_STATE), F32)
    for j in range(LANES // S5_GROUP):
        out = jnp.where(grp == j, p[:, S5_STATE * j:S5_STATE * (j + 1)], out)
    return out


def _s5_core_fwd(u, wb_re, wb_im, wc_re, wc_im, a_re, a_im, name, ride=None):
    s = u.shape[0]
    lw = S5_LANES // S5_SUPER
    rows = _pick(s, (S5_ROWS, 256, 128))

    def body(u_r, wbr, wbi, wcr, wci, are_r, aim_r, xre_r, xim_r, y_r, bre_s, bim_s):
        for r0 in range(0, s, rows):
            ub = u_r[r0:r0 + rows, :]
            bre_s[r0:r0 + rows, :] = _nn(ub, wbr[0])
            bim_s[r0:r0 + rows, :] = _nn(ub, wbi[0])
        _scan_loop(bre_s, bim_s, are_r[...], aim_r[...], xre_r, xim_r, False)
        for r0 in range(0, s, rows):
            y_r[r0:r0 + rows, :] = _nn(xre_r[r0:r0 + rows, :], wcr[0]) + _nn(xim_r[r0:r0 + rows, :], wci[0])

    nar = pl.BlockSpec((s, LANES), lambda k: (0, k))
    wide = pl.BlockSpec((s, lw), lambda k: (0, k))
    one = pl.BlockSpec((1, lw), lambda k: (0, k))
    wb = pl.BlockSpec((1, LANES, lw), lambda k: (k, 0, 0))
    wc = pl.BlockSpec((1, lw, LANES), lambda k: (k, 0, 0))
    return _call_with_ride(body, name, (S5_SUPER,), [u, wb_re, wb_im, wc_re, wc_im, a_re, a_im], [nar, wb, wb, wc, wc, one, one],
                           [_sds((s, S5_LANES)), _sds((s, S5_LANES)), _sds((s, S5_WIDTH))], [wide, wide, nar], ("parallel",), 56,
                           ride, scratch=[pltpu.VMEM((s, lw), F32), pltpu.VMEM((s, lw), F32)])


def _s5_core_bwd(dy0, u, du_a, x_re, x_im, wb_re, wb_im, wc_re, wc_im, a_re, a_im_neg, name, ride=None):
    s = u.shape[0]
    lw = S5_LANES // S5_SUPER
    rows = _pick(s, (S5_ROWS, 256, 128))

    def body(dy_r, u_r, dua_r, xre_r, xim_r, wbr, wbi, wcr, wci, are_r, aim_r, du_r, dare_r, daim_r, dbr_r, dbi_r, dcr_r, dci_r,
             dre_s, dim_s, gre_s, gim_s):
        for r0 in range(0, s, rows):
            dyb = dy_r[r0:r0 + rows, :]
            dre_s[r0:r0 + rows, :] = _nt(dyb, wcr[0])
            dim_s[r0:r0 + rows, :] = _nt(dyb, wci[0])
        dare_r[...], daim_r[...] = _scan_loop(dre_s, dim_s, are_r[...], aim_r[...], gre_s, gim_s, True, xre_r, xim_r)
        acc = [jnp.zeros((LANES, lw), F32) for _ in range(4)]
        for r0 in range(0, s, rows):
            sl = slice(r0, r0 + rows)
            gr, gi, ub, dyb = gre_s[sl, :], gim_s[sl, :], u_r[sl, :], dy_r[sl, :]
            du_r[sl, :] = dua_r[sl, :] + _nt(gr, wbr[0]) + _nt(gi, wbi[0])
            acc = [acc[0] + _tn(ub, gr), acc[1] + _tn(ub, gi), acc[2] + _tn(dyb, xre_r[sl, :]), acc[3] + _tn(dyb, xim_r[sl, :])]
        dbr_r[...], dbi_r[...], dcr_r[...], dci_r[...] = (_group_compact(a) for a in acc)

    nar = pl.BlockSpec((s, LANES), lambda k: (0, k))
    wide = pl.BlockSpec((s, lw), lambda k: (0, k))
    one = pl.BlockSpec((1, lw), lambda k: (0, k))
    wb = pl.BlockSpec((1, LANES, lw), lambda k: (k, 0, 0))
    wc = pl.BlockSpec((1, lw, LANES), lambda k: (k, 0, 0))
    blk = pl.BlockSpec((LANES, S5_STATE), lambda k: (k, 0))
    return _call_with_ride(
        body, name, (S5_SUPER,), [dy0, u, du_a, x_re, x_im, wb_re, wb_im, wc_re, wc_im, a_re, a_im_neg],
        [nar, nar, nar, wide, wide, wb, wb, wc, wc, one, one],
        [_sds((s, S5_WIDTH)), _sds((1, S5_LANES)), _sds((1, S5_LANES))] + [_sds((S5_WIDTH, S5_STATE))] * 4,
        [nar, one, one, blk, blk, blk, blk], ("parallel",), 60, ride, scratch=[pltpu.VMEM((s, lw), F32)] * 4)


def _s5_seg1(y0, u, d):
    return jax.nn.gelu(y0 + d * u)


def _s5_seg2(z, t, b):
    return z * jax.nn.sigmoid(t + b)


def _s5_post_fwd(y0, seg2, d, wglu, bglu, name):
    s = y0.shape[0]
    ts = _pick(s, (256, 128))

    def fn(ins, outs, accs, scr):
        z = _s5_seg1(ins[0][...], ins[1][...], ins[2][...])
        outs[0][...] = _s5_seg2(z, _nn(z, ins[3][...]), ins[4][...])

    return _stage(name, fn, s // ts,
                  [(y0, _rows(ts, 512)), (seg2, _rows(ts, 512)), (d, _full(d.shape)), (wglu, _full(wglu.shape)),
                   (bglu, _full(bglu.shape))], [(_sds((s, 512)), _rows(ts, 512))])[0]


def _s5_post_bwd(y0, seg2, d, wglu, bglu, dy, name):
    s = y0.shape[0]
    ts = _pick(s, (256, 128))

    def fn(ins, outs, accs, scr):
        y0_r, u_r, d_r, w_r, b_r, dy_r = ins
        z, vjp1 = jax.vjp(_s5_seg1, y0_r[...], u_r[...], d_r[...])
        t = _nn(z, w_r[...])
        _, vjp2 = jax.vjp(_s5_seg2, z, t, b_r[...])
        dz, dt, db = vjp2(dy_r[...])
        accs[0][...] += _tn(z, dt)
        accs[1][...] += db
        dy0, du, dd = vjp1(dz + _nt(dt, w_r[...]))
        accs[2][...] += dd
        outs[0][...] = dy0
        outs[1][...] = du

    return _stage(name, fn, s // ts,
                  [(y0, _rows(ts, 512)), (seg2, _rows(ts, 512)), (d, _full(d.shape)), (wglu, _full(wglu.shape)),
                   (bglu, _full(bglu.shape)), (dy, _rows(ts, 512))],
                  [(_sds((s, 512)), _rows(ts, 512)), (_sds((s, 512)), _rows(ts, 512))],
                  accs=[_sds((512, 512)), _sds((1, 512)), _sds((1, 512))])


def _gate_a(y, g):
    return y * jax.nn.silu(g)


def _gate_m(o0, o1, o2, m0, m1, m2):
    return jax.nn.sigmoid(m0) * o0 + jax.nn.sigmoid(m1) * o1 + jax.nn.sigmoid(m2) * o2


def _assemble(ybuf, o_mla, o_fox, y_s5):
    for h in range(HEADS):
        ybuf[:, 64 * h:64 * (h + 1)] = o_mla[h]
        ybuf[:, 512 + 64 * h:512 + 64 * (h + 1)] = o_fox[h]
    ybuf[:, 1024:1536] = y_s5[...]


def _gate_fwd(o_mla, o_fox, y_s5, seg3, x, wo, wout, name):
    s = x.shape[0]
    ts = _pick(s, (256, 128))

    def fn(ins, outs, accs, scr):
        om, of, ys, gm_r, x_r, wo_r, wout_r = ins
        ybuf = scr[0]
        _assemble(ybuf, om, of, ys)
        a = _gate_a(ybuf[...], gm_r[:, :GATES_W])
        o = [_nn(a[:, 512 * b:512 * (b + 1)], wo_r[512 * b:512 * (b + 1), :]) for b in range(3)]
        merged = _gate_m(o[0], o[1], o[2], *[gm_r[:, GATES_W + 1024 * b:GATES_W + 1024 * (b + 1)] for b in range(3)])
        outs[0][...] = x_r[...] + _nn(merged, wout_r[...])

    return _stage(name, fn, s // ts,
                  [(o_mla, _heads(ts, 64)), (o_fox, _heads(ts, 64)), (y_s5, _rows(ts, 512)), (seg3, _rows(ts, SEG_W[3])),
                   (x, _rows(ts, D_MODEL)), (wo, _full(wo.shape)), (wout, _full(wout.shape))],
                  [(_sds((s, D_MODEL)), _rows(ts, D_MODEL))], scratch=[pltpu.VMEM((ts, 1536), F32)])[0]


def _gate_bwd(o_mla, o_fox, y_s5, seg3, wo, wout, dout, name):
    s = dout.shape[0]
    ts = _pick(s, (128,))

    def fn(ins, outs, accs, scr):
        om, of, ys, gm_r, wo_r, wout_r, dout_r = ins
        do_mla, do_fox, dys, dgm_r = outs
        dwo, dwout = accs
        ybuf, dabuf = scr
        _assemble(ybuf, om, of, ys)
        a, vjp_a = jax.vjp(_gate_a, ybuf[...], gm_r[:, :GATES_W])
        o = [_nn(a[:, 512 * b:512 * (b + 1)], wo_r[512 * b:512 * (b + 1), :]) for b in range(3)]
        ms = [gm_r[:, GATES_W + 1024 * b:GATES_W + 1024 * (b + 1)] for b in range(3)]
        merged, vjp_m = jax.vjp(_gate_m, *o, *ms)
        dout_v = dout_r[...]
        dwout[...] += _tn(merged, dout_v)
        cts = vjp_m(_nt(dout_v, wout_r[...]))
        for b in range(3):
            dgm_r[:, GATES_W + 1024 * b:GATES_W + 1024 * (b + 1)] = cts[3 + b].astype(BF16)
            dwo[512 * b:512 * (b + 1), :] += _tn(a[:, 512 * b:512 * (b + 1)], cts[b])
            dabuf[:, 512 * b:512 * (b + 1)] = _nt(cts[b], wo_r[512 * b:512 * (b + 1), :])
        dy, dg = vjp_a(dabuf[...])
        dgm_r[:, :GATES_W] = dg.astype(BF16)
        dys[...] = dy[:, 1024:1536]
        for h in range(HEADS):
            do_mla[h] = dy[:, 64 * h:64 * (h + 1)]
            do_fox[h] = dy[:, 512 + 64 * h:512 + 64 * (h + 1)]

    return _stage(name, fn, s // ts,
                  [(o_mla, _heads(ts, 64)), (o_fox, _heads(ts, 64)), (y_s5, _rows(ts, 512)), (seg3, _rows(ts, SEG_W[3])),
                   (wo, _full(wo.shape)), (wout, _full(wout.shape)), (dout, _rows(ts, D_MODEL))],
                  [(_sds((HEADS, s, 64)), _heads(ts, 64)), (_sds((HEADS, s, 64)), _heads(ts, 64)), (_sds((s, 512)), _rows(ts, 512)),
                   (_sds((s, SEG_W[3]), BF16), _rows(ts, SEG_W[3]))],
                  accs=[_sds(wo.shape), _sds(wout.shape)], scratch=[pltpu.VMEM((ts, 1536), F32), pltpu.VMEM((ts, 1536), F32)],
                  vmem_mb=56)


def _loss_head(y, target, name):
    s = y.shape[0]
    ts = _pick(s, (256, 128))

    def fn(ins, outs, accs, scr):
        e = ins[0][...] - ins[1][...]
        outs[0][...] = e / D_MODEL
        accs[0][...] += 0.5 * jnp.sum(jnp.sum(e * e, axis=1, keepdims=True) / D_MODEL, axis=0, keepdims=True)

    r = _stage(name, fn, s // ts, [(y, _rows(ts, D_MODEL)), (target, _rows(ts, D_MODEL))],
               [(_sds((s, D_MODEL)), _rows(ts, D_MODEL))], accs=[_sds((1, 1))])
    return r[1], r[0]


IN_RANGES = ((0, 384, 0), (384, 416, 448), (416, 1952, 512), (1952, 1960, 384), (1960, IN_WIDTH, 2048))
SHARD_W = IN_WIDTH // N_DEV


def _win_pieces(d):
    lo, hi = SHARD_W * d, SHARD_W * (d + 1)
    out = []
    for a, b, p in IN_RANGES:
        s, e = max(a, lo), min(b, hi)
        while s < e:
            pad = p + (s - a)
            k = max(i for i in range(NSEG) if SEG_OFF[i] <= pad)
            w = min(e - s, SEG_OFF[k] + SEG_W[k] - pad)
            out.append((s - lo, w, k, pad - SEG_OFF[k]))
            s += w
    return out


def _win_pad(g, name):
    _, r, _ = g.shape
    tr = 64

    def body(g_ref, o_ref):
        o_ref[...] = jnp.zeros(o_ref.shape, o_ref.dtype)
        for d in range(N_DEV):
            for dst, w, k, src in _win_pieces(d):
                o_ref[:, SEG_OFF[k] + src:SEG_OFF[k] + src + w] = g_ref[d, :, dst:dst + w]

    return pl.pallas_call(
        body, name=name, grid=(r // tr,), in_specs=[pl.BlockSpec((N_DEV, tr, SHARD_W), lambda i: (0, i, 0))],
        out_specs=pl.BlockSpec((tr, PAD_IN), lambda i: (i, 0)), out_shape=jax.ShapeDtypeStruct((r, PAD_IN), g.dtype),
        compiler_params=pltpu.CompilerParams(dimension_semantics=("parallel",)),
    )(g)


def _win_unpad(dsegs, name):
    r = dsegs[0].shape[0]
    tr = 64

    def body(*refs):
        o_ref = refs[NSEG]
        for d in range(N_DEV):
            for dst, w, k, src in _win_pieces(d):
                o_ref[d, :, dst:dst + w] = refs[k][:, src:src + w]

    return pl.pallas_call(
        body, name=name, grid=(r // tr,), in_specs=[pl.BlockSpec((tr, SEG_W[k]), lambda i: (i, 0)) for k in range(NSEG)],
        out_specs=pl.BlockSpec((N_DEV, tr, SHARD_W), lambda i: (0, i, 0)),
        out_shape=jax.ShapeDtypeStruct((N_DEV, r, SHARD_W), dsegs[0].dtype),
        compiler_params=pltpu.CompilerParams(dimension_semantics=("parallel",)),
    )(*dsegs)


def _pad_wq(w):
    w = w.reshape(MLA_Q_RANK, HEADS, MLA_QK)
    return jnp.pad(w, ((0, 0), (0, 0), (0, LANES - MLA_QK))).reshape(MLA_Q_RANK, HEADS * LANES)


def _unpad_wq(d):
    return d.reshape(MLA_Q_RANK, HEADS, LANES)[:, :, :MLA_QK].reshape(MLA_Q_RANK, HEADS * MLA_QK)


def _pad_wkv(w):
    w = w.reshape(MLA_KV_RANK, HEADS, MLA_NOPE + MLA_V)
    k = jnp.pad(w[:, :, :MLA_NOPE], ((0, 0), (0, 0), (0, LANES - MLA_NOPE))).reshape(MLA_KV_RANK, HEADS * LANES)
    return jnp.concatenate([k, w[:, :, MLA_NOPE:].reshape(MLA_KV_RANK, HEADS * MLA_V)], axis=1)


def _unpad_wkv(d):
    k = d[:, :HEADS * LANES].reshape(MLA_KV_RANK, HEADS, LANES)[:, :, :MLA_NOPE]
    v = d[:, HEADS * LANES:].reshape(MLA_KV_RANK, HEADS, MLA_V)
    return jnp.concatenate([k, v], axis=2).reshape(MLA_KV_RANK, HEADS * (MLA_NOPE + MLA_V))


def _pad_lanes(v, n=LANES):
    return jnp.pad(v, (0, n - v.shape[0])).reshape(1, n)


def _super_blocks(b):
    _, r, c = b.shape
    per = S5_GROUPS // S5_SUPER
    b = b.reshape(S5_SUPER, per, r, c)
    eye = jnp.eye(per, dtype=b.dtype)
    return (b[:, :, :, None, :] * eye[None, :, None, :, None]).reshape(S5_SUPER, per * r, per * c)


def _layer_params(l, w, small):
    p = {k: small[k][l] for k in small}
    q = {}
    q["norm_g"] = p["norm_g"].reshape(1, D_MODEL)
    q["mla_q_a_norm"] = p["mla_q_a_norm"].reshape(1, 256)
    q["mla_kv_a_norm"] = p["mla_kv_a_norm"].reshape(1, 128)
    q["mla_q_norm"] = _pad_lanes(p["mla_q_norm"])
    q["mla_k_norm"] = _pad_lanes(p["mla_k_norm"])
    q["fox_b_f"] = _pad_lanes(p["fox_b_f"])
    q["fox_q_norm"] = p["fox_q_norm"].reshape(1, 64)
    q["fox_k_norm"] = p["fox_k_norm"].reshape(1, 64)
    q["s5_d"] = p["s5_d"].reshape(1, 512)
    q["s5_b_glu"] = p["s5_b_glu"].reshape(1, 512)
    rep = lambda z: jnp.repeat(z, S5_GROUP, axis=0)
    q["lr16"], q["li16"] = rep(p["s5_lambda_re"]), rep(p["s5_lambda_im"])
    q["ldt16"] = rep(jnp.broadcast_to(p["s5_log_dt"][:, None], (S5_GROUPS, S5_STATE)))
    q["br2"] = p["s5_b_re"].transpose(0, 2, 1).reshape(512, 64)
    q["bi2"] = p["s5_b_im"].transpose(0, 2, 1).reshape(512, 64)
    q["c_re"], q["c_im"] = p["s5_c_re"], p["s5_c_im"]
    q["w_in"] = _win_pad(w["w_in"], f"l{l}_w_in_pad")
    q["wq"] = _pad_wq(w["mla_w_q_up"])
    q["wkv"] = _pad_wkv(w["mla_w_kv_up"])
    q.update(_late_weights(w))
    return q


def _late_weights(w):
    return {q: w[k] for q, k in (("wglu", "s5_w_glu"), ("wo", "w_branch_out"), ("wout", "w_out")) if k in w}


def _layer_fwd(l, x, tabs, q, rides=None):
    rides = rides or {}
    n = lambda s: f"l{l}_{s}"
    sv = {"x": x}
    h = _norm_fwd(x, q["norm_g"], n("norm_fwd"))
    sv["h"] = h
    seg = [_mm(h, q["w_in"], "nn", n(f"proj{k}"), b_cols=(SEG_OFF[k], SEG_W[k])) for k in range(NSEG)]
    sv["seg"] = seg
    mq, mk, mv = _mla_prep_fwd(seg[0], tabs, q, q["wq"], q["wkv"], n("mla_prep_fwd"))
    o_mla, lse_mla = _attn_fwd(mq, mk, mv, 1.0 / math.sqrt(MLA_QK), True, n("mla_attn_fwd"), ride=rides.get("mla_attn_fwd"))
    sv.update(mq=mq, mk=mk, mv=mv, o_mla=o_mla, lse_mla=lse_mla)
    fq, fk, fv, c, ct = _fox_prep_fwd(seg[0], seg[1], q["fox_b_f"], q["fox_q_norm"], q["fox_k_norm"], n("fox_prep_fwd"))
    o_fox, lse_fox = _attn_fwd(fq, fk, fv, 1.0 / math.sqrt(FOX_DIM), False, n("fox_attn_fwd"), c=c, ct=ct,
                               ride=rides.get("fox_attn_fwd"))
    sv.update(fq=fq, fk=fk, fv=fv, c=c, ct=ct, o_fox=o_fox, lse_fox=lse_fox)
    a_re16, a_im16, bb_re, bb_im = _s5_params_fwd(q["lr16"], q["li16"], q["ldt16"], q["br2"], q["bi2"], n("s5_params_fwd"))
    a_re = a_re16.reshape(S5_GROUPS, S5_GROUP, S5_STATE)[:, 0, :].reshape(1, S5_LANES)
    a_im = a_im16.reshape(S5_GROUPS, S5_GROUP, S5_STATE)[:, 0, :].reshape(1, S5_LANES)
    wb_re = _super_blocks(bb_re.reshape(S5_GROUPS, S5_GROUP, S5_STATE)).astype(BF16)
    wb_im = _super_blocks(bb_im.reshape(S5_GROUPS, S5_GROUP, S5_STATE)).astype(BF16)
    wc_re = _super_blocks(q["c_re"].transpose(0, 2, 1)).astype(BF16)
    wc_im = _super_blocks(-q["c_im"].transpose(0, 2, 1)).astype(BF16)
    x_re, x_im, y0 = _s5_core_fwd(seg[2], wb_re, wb_im, wc_re, wc_im, a_re, a_im, n("s5_scan_fwd"), ride=rides.get("s5_scan_fwd"))
    y_s5 = _s5_post_fwd(y0, seg[2], q["s5_d"], q["wglu"], q["s5_b_glu"], n("s5_post_fwd"))
    sv.update(a_re=a_re, a_im=a_im, wb_re=wb_re, wb_im=wb_im, wc_re=wc_re, wc_im=wc_im, x_re=x_re, x_im=x_im, y0=y0, y_s5=y_s5)
    out = _gate_fwd(o_mla, o_fox, y_s5, seg[3], x, q["wo"], q["wout"], n("gate_fwd"))
    return out, sv


def _layer_bwd(l, dout, tabs, q, sv, rides=None, g=None):
    rides = rides or {}
    n = lambda s: f"l{l}_{s}"
    seg = sv["seg"]
    g = {} if g is None else g
    (do_mla, do_fox, dy_s5, dseg3, g["wo"], g["wout"]) = _gate_bwd(
        sv["o_mla"], sv["o_fox"], sv["y_s5"], seg[3], q["wo"], q["wout"], dout, n("gate_bwd"))
    dy0, du_a, g["wglu"], g["s5_b_glu"], g["s5_d"] = _s5_post_bwd(sv["y0"], seg[2], q["s5_d"], q["wglu"], q["s5_b_glu"], dy_s5,
                                                                 n("s5_post_bwd"))
    dseg2, da_re, da_im, dbb_re, dbb_im, dc_re, dc_im = _s5_core_bwd(
        dy0, seg[2], du_a, sv["x_re"], sv["x_im"], sv["wb_re"], sv["wb_im"], sv["wc_re"], sv["wc_im"], sv["a_re"], -sv["a_im"],
        n("s5_scan_bwd"), ride=rides.get("s5_scan_bwd"))
    g["s5_c_re"] = dc_re.reshape(S5_GROUPS, S5_GROUP, S5_STATE)
    g["s5_c_im"] = -dc_im.reshape(S5_GROUPS, S5_GROUP, S5_STATE)
    first = (jnp.arange(512) % S5_GROUP == 0).astype(F32)[:, None]
    da_re16 = jnp.repeat(da_re.reshape(S5_GROUPS, S5_STATE), S5_GROUP, axis=0) * first
    da_im16 = jnp.repeat(da_im.reshape(S5_GROUPS, S5_STATE), S5_GROUP, axis=0) * first
    dlr, dli, dldt, dbr2, dbi2 = _s5_params_bwd(q["lr16"], q["li16"], q["ldt16"], q["br2"], q["bi2"], da_re16, da_im16, dbb_re,
                                               dbb_im, n("s5_params_bwd"))
    g["s5_lambda_re"], g["s5_lambda_im"], g["s5_log_dt"] = dlr, dli, dldt.reshape(S5_GROUPS)
    g["s5_b_re"] = dbr2.reshape(S5_GROUPS, S5_GROUP, S5_STATE).transpose(0, 2, 1)
    g["s5_b_im"] = dbi2.reshape(S5_GROUPS, S5_GROUP, S5_STATE).transpose(0, 2, 1)
    dfq, dfk, dfv, dck = _attn_bwd_rows(sv["fq"], sv["fk"], sv["fv"], do_fox, sv["lse_fox"], 1.0 / math.sqrt(FOX_DIM), False,
                                        n("fox_attn_bwd"), sv["c"], sv["ct"], ride=rides.get("fox_attn_bwd"))
    dseg1, dff, g["fox_q_norm"], g["fox_k_norm"], dbf = _fox_prep_bwd(seg[0], seg[1], q["fox_b_f"], q["fox_q_norm"], q["fox_k_norm"],
                                                                      dfq, dfk, dfv, dck, n("fox_prep_bwd"))
    g["fox_b_f"] = dbf[0, :HEADS]
    dmq, dmk, dmv = _attn_bwd_rows(sv["mq"], sv["mk"], sv["mv"], do_mla, sv["lse_mla"], 1.0 / math.sqrt(MLA_QK), True,
                                   n("mla_attn_bwd"), ride=rides.get("mla_attn_bwd"))
    dseg0, dqan, dkvan, dqn, dkn, g["wq"], g["wkv"] = _mla_prep_bwd(seg[0], tabs, q, q["wq"], q["wkv"], dmq, dmk, dmv, dff,
                                                                   n("mla_prep_bwd"))
    g["mla_q_a_norm"], g["mla_kv_a_norm"] = dqan, dkvan
    g["mla_q_norm"], g["mla_k_norm"] = dqn[0, :MLA_QK], dkn[0, :MLA_QK]
    dsegs = [dseg0, dseg1, dseg2, dseg3]
    g["w_in"] = [_mm(sv["h"], dsegs[k], "tn", n(f"dwin{k}")) for k in range(NSEG)]
    dh = None
    for k in (3, 0, 2, 1):
        dh = _mm(dsegs[k], q["w_in"], "nt", n(f"dh{k}"), acc=dh, b_cols=(SEG_OFF[k], SEG_W[k]), ride=rides.get(f"dh{k}"))
    dx, g["norm_g"] = _norm_bwd(sv["x"], q["norm_g"], dh, dout, n("norm_bwd"))
    return dx, g


MESH = pl.DeviceIdType.MESH
ANY = pl.BlockSpec(memory_space=pl.ANY)


def _all_gather(blocks, name):
    n = len(blocks)

    def body(*refs):
        x_refs, out_refs = refs[:n], refs[n:2 * n]
        send_sems, recv_sems, local_sems = refs[2 * n:]
        x, y, c = lax.axis_index("x"), lax.axis_index("y"), lax.axis_index("c")
        me, sibling = (x, y, c), (x, y, 1 - c)
        chips = [(1 - x, y), (x, 1 - y), (1 - x, 1 - y)]

        def slot(a, px, py, pc):
            return out_refs[a].at[4 * px + 2 * py + pc]

        def copy(a, k, blk, to, src=None):
            return pltpu.make_async_remote_copy(src_ref=slot(a, *blk) if src is None else src, dst_ref=slot(a, *blk),
                                                send_sem=send_sems.at[7 * a + k], recv_sem=recv_sems.at[7 * a + k],
                                                device_id=to, device_id_type=MESH)

        mine = [pltpu.make_async_copy(x_refs[a], slot(a, *me), local_sems.at[a]) for a in range(n)]
        for cp in mine:
            cp.start()
        first = []
        for j, chip in enumerate(chips):
            first += [copy(a, 1 + j, me, (*chip, c), src=x_refs[a]) for a in range(n)]
        first += [copy(a, 0, me, sibling, src=x_refs[a]) for a in range(n)]
        for cp in first:
            cp.start()
        passed = []
        for j, chip in enumerate(chips):
            for a in range(n):
                copy(a, 1 + j, (*chip, c), me).wait_recv()
                passed.append(copy(a, 4 + j, (*chip, c), sibling))
                passed[-1].start()
        for a in range(n):
            copy(a, 0, sibling, me).wait_recv()
        for j, chip in enumerate(chips):
            for a in range(n):
                copy(a, 4 + j, (*chip, 1 - c), me).wait_recv()
        for cp in first + passed:
            cp.wait_send()
        for cp in mine:
            cp.wait()

    return pl.pallas_call(
        body, name=name, out_shape=[jax.ShapeDtypeStruct((N_DEV,) + b.shape, b.dtype) for b in blocks],
        in_specs=[ANY] * n, out_specs=[ANY] * n,
        scratch_shapes=[pltpu.SemaphoreType.DMA((7 * n,)), pltpu.SemaphoreType.DMA((7 * n,)), pltpu.SemaphoreType.DMA((n,))],
    )(*blocks)


def _place():
    x, y, c = lax.axis_index("x"), lax.axis_index("y"), lax.axis_index("c")
    return x, y, c, [(1 - x, y), (x, 1 - y), (1 - x, 1 - y)]


def _remote(src, dst, send, recv, k, to):
    return pltpu.make_async_remote_copy(src_ref=src, dst_ref=dst, send_sem=send.at[k], recv_sem=recv.at[k], device_id=to,
                                        device_id_type=MESH)


def _plan_gather_ici(blocks):
    n = len(blocks)

    def copies(in_refs, out_refs, send, recv, local):
        x, y, c, chips = _place()
        mine = 4 * x + 2 * y + c
        loc = [pltpu.make_async_copy(in_refs[a], out_refs[a].at[mine], local.at[a]) for a in range(n)]
        rem = [_remote(in_refs[a], out_refs[a].at[mine], send, recv, 3 * a + j, (px, py, c))
               for j, (px, py) in enumerate(chips) for a in range(n)]
        return rem, loc

    return _Plan(blocks, [jax.ShapeDtypeStruct((N_DEV,) + b.shape, b.dtype) for b in blocks], 3 * n, n, copies)


def _plan_gather_d2d(gathered):
    n = len(gathered)

    def copies(in_refs, out_refs, send, recv, local):
        x, y, c, _ = _place()
        rem = [_remote(in_refs[a].at[2 * j + c], out_refs[a].at[2 * j + c], send, recv, 4 * a + j, (x, y, 1 - c))
               for a in range(n) for j in range(4)]
        return rem, []

    return _Plan(gathered, [jax.ShapeDtypeStruct(g.shape, g.dtype) for g in gathered], 4 * n, 0, copies,
                 aliases={a: a for a in range(n)})


def _plan_reduce_sibling(parts):
    n = len(parts)

    def copies(in_refs, out_refs, send, recv, local):
        x, y, c, _ = _place()
        rem = [_remote(in_refs[a].at[2 * j + (1 - c)], out_refs[a].at[j], send, recv, 4 * a + j, (x, y, 1 - c))
               for a in range(n) for j in range(4)]
        return rem, []

    return _Plan(parts, [jax.ShapeDtypeStruct((4,) + p.shape[1:], p.dtype) for p in parts], 4 * n, 0, copies)


def _plan_reduce_chips(sums):
    n = len(sums)

    def copies(in_refs, out_refs, send, recv, local):
        x, y, c, chips = _place()
        mine = 2 * x + y
        loc = [pltpu.make_async_copy(in_refs[a].at[mine], out_refs[a].at[mine], local.at[a]) for a in range(n)]
        rem = [_remote(in_refs[a].at[2 * px + py], out_refs[a].at[mine], send, recv, 3 * a + k, (px, py, c))
               for k, (px, py) in enumerate(chips) for a in range(n)]
        return rem, loc

    return _Plan(sums, [jax.ShapeDtypeStruct(p.shape, p.dtype) for p in sums], 3 * n, n, copies)


def _add_sibling(parts, got, name):
    _, r, cc = parts.shape
    tr = _pick(r, (512, 256, 128, 64, 32, 16))
    c = lax.axis_index("c")

    def body(c_ref, p_ref, g_ref, o_ref):
        o_ref[...] = (p_ref[...] + g_ref[...]).astype(BF16)

    return pl.pallas_call(
        body, name=name, out_shape=jax.ShapeDtypeStruct((4, r, cc), BF16),
        grid_spec=pltpu.PrefetchScalarGridSpec(
            num_scalar_prefetch=1, grid=(4, r // tr),
            in_specs=[pl.BlockSpec((1, tr, cc), lambda j, i, cr: (2 * j + cr[0], i, 0)),
                      pl.BlockSpec((1, tr, cc), lambda j, i, cr: (j, i, 0))],
            out_specs=pl.BlockSpec((1, tr, cc), lambda j, i, cr: (j, i, 0))),
        compiler_params=_vmem(48),
    )(c.reshape(1).astype(jnp.int32), parts, got)


def _sum_leading(parts, name):
    k, r, cc = parts.shape
    tr = _pick(r, (512, 256, 128, 64, 32, 16, 8))

    def body(p_ref, o_ref):
        acc = p_ref[0]
        for j in range(1, k):
            acc = acc + p_ref[j]
        o_ref[...] = acc

    return pl.pallas_call(
        body, name=name, out_shape=jax.ShapeDtypeStruct((r, cc), F32), grid=(r // tr,),
        in_specs=[pl.BlockSpec((k, tr, cc), lambda i: (0, i, 0))], out_specs=pl.BlockSpec((tr, cc), lambda i: (i, 0)),
    )(parts)


def _adamw_math(w, g, m, v):
    nm = ADAM_B1 * m + (1.0 - ADAM_B1) * g
    nv = ADAM_B2 * v + (1.0 - ADAM_B2) * jnp.square(g)
    m_hat = nm / (1.0 - ADAM_B1 ** ADAM_STEP)
    v_hat = nv / (1.0 - ADAM_B2 ** ADAM_STEP)
    return -ADAM_LR * (m_hat / (jnp.sqrt(v_hat) + ADAM_EPS) + ADAM_WD * w), nm, nv


def _adamw_sum(w, contribs, m, v, name, ride=None):
    nl = len(contribs)
    k, r, cc = contribs[0].shape
    tr = _pick(r, (256, 128, 64, 32, 16))
    nb = r // tr

    def body(w_ref, *rest):
        c_refs = rest[:nl]
        m_ref, v_ref, g_ref, d_ref, nm_ref, nv_ref = rest[nl:]
        for li in range(nl):
            @pl.when(pl.program_id(0) == li)
            def _(c_ref=c_refs[li]):
                g = c_ref[0].astype(F32)
                for j in range(1, k):
                    g = g + c_ref[j].astype(F32)
                g_ref[...] = g
                d_ref[...], nm_ref[...], nv_ref[...] = _adamw_math(w_ref[...], g, m_ref[...], v_ref[...])

    spec = pl.BlockSpec((tr, cc), lambda l, i: (l * nb + i, 0))
    cspec = pl.BlockSpec((k, tr, cc), lambda l, i: (0, i, 0))
    return _call_with_ride(body, name, (nl, nb), [w, *contribs, m, v], [spec] + [cspec] * nl + [spec, spec],
                           [jax.ShapeDtypeStruct(w.shape, F32)] * 4, [spec] * 4, ("parallel", "parallel"), 48, ride)


def _adamw_many(ws, gs, ms, vs, name):
    n = len(ws)

    def body(*refs):
        w_r, g_r, m_r, v_r = refs[:n], refs[n:2 * n], refs[2 * n:3 * n], refs[3 * n:4 * n]
        d_r, nm_r, nv_r = refs[4 * n:5 * n], refs[5 * n:6 * n], refs[6 * n:7 * n]
        for a in range(n):
            d_r[a][...], nm_r[a][...], nv_r[a][...] = _adamw_math(w_r[a][...], g_r[a][...], m_r[a][...], v_r[a][...])

    shapes = [jax.ShapeDtypeStruct(w.shape, F32) for w in ws]
    res = pl.pallas_call(body, name=name, out_shape=shapes * 3,
                         compiler_params=pltpu.CompilerParams(vmem_limit_bytes=56 * 1024 * 1024))(*ws, *gs, *ms, *vs)
    return res[:n], res[n:2 * n], res[2 * n:]


def _pack_rows(flat, lanes, row_mult):
    n = flat.shape[-1]
    rows = -(-n // lanes)
    rows = -(-rows // row_mult) * row_mult
    pad = rows * lanes - n
    if pad:
        flat = jnp.pad(flat, [(0, 0)] * (flat.ndim - 1) + [(0, pad)])
    return flat.reshape(flat.shape[:-1] + (rows, lanes))


def _rope_tables(positions):
    inv = 1.0 / (ROPE_THETA ** (jnp.arange(0, MLA_ROPE, 2, dtype=F32) / MLA_ROPE))
    ang = positions.astype(F32)[:, None] * inv
    cos, sin = jnp.cos(ang), jnp.sin(ang)
    s = positions.shape[0]
    z = lambda n: jnp.zeros((s, n), F32)
    c = jnp.concatenate([jnp.ones((s, 64), F32), cos, cos, z(32)], axis=1)
    sa = jnp.concatenate([z(64), -sin, z(48)], axis=1)
    sb = jnp.concatenate([z(80), sin, z(32)], axis=1)
    return c, sa, sb


def _full_weights(gathered, names=SHARDED):
    full = {}
    for k, g in zip(names, gathered):
        _, r, c = g.shape
        if k == "w_in":
            full[k] = g
        else:
            full[k] = g.transpose(1, 0, 2).reshape(r, N_DEV * c) if k in COL_SHARDED else g.reshape(N_DEV * r, c)
    return full


EARLY = ("s5_w_glu", "w_branch_out", "w_out")
LATE = ("w_in", "mla_w_q_up", "mla_w_kv_up")


def _owner_major(g, names, tag):
    parts = []
    for k in names:
        if k == "w_in":
            parts.append(_win_unpad(g["w_in"], f"{tag}_w_in_unpad"))
            continue
        big = {"mla_w_q_up": lambda: _unpad_wq(g["wq"]), "mla_w_kv_up": lambda: _unpad_wkv(g["wkv"]), "s5_w_glu": lambda: g["wglu"],
               "w_branch_out": lambda: g["wo"], "w_out": lambda: g["wout"]}[k]()
        r, c = big.shape
        if k in COL_SHARDED:
            parts.append(big.reshape(r, N_DEV, c // N_DEV).transpose(1, 0, 2))
        else:
            parts.append(big.reshape(N_DEV, r // N_DEV, c))
    return parts


def _device_step(x, positions, target, shards, small):
    tabs = _rope_tables(positions)
    box = {}
    q0 = _layer_params(0, _full_weights(_all_gather(shards[0][:3], "gather_weights_l0"), LATE), small)

    def arrived(o):
        box.update(w1_b=o[:5])
        q0.update(_late_weights(_full_weights(o[5:], EARLY)))

    rides = {
        "mla_attn_fwd": _Ride(lambda: _plan_gather_ici(shards[1][:1]), lambda o: box.update(ici_a=o)),
        "fox_attn_fwd": _Ride(lambda: _join([_plan_gather_ici(shards[1][1:]), _plan_gather_d2d(box["ici_a"]),
                                             _plan_gather_ici(shards[0][3:])]),
                              lambda o: box.update(ici_b=o[:5], w1_a=o[5:6], ici_0=o[6:])),
        "s5_scan_fwd": _Ride(lambda: _join([_plan_gather_d2d(box["ici_b"]), _plan_gather_d2d(box["ici_0"])]), arrived),
    }
    h, sv0 = _layer_fwd(0, x, tabs, q0, rides)
    q1 = _layer_params(1, _full_weights(list(box["w1_a"]) + list(box["w1_b"])), small)
    h, sv1 = _layer_fwd(1, h, tabs, q1)
    loss, d = _loss_head(h, target, "loss_head")
    d, g1 = _layer_bwd(1, d, tabs, q1, sv1)
    parts1 = _owner_major(g1, SHARDED, "l1")

    def chips_plan(names, parts, got, tag):
        return _plan_reduce_chips([_add_sibling(p, g, f"reduce_add_{tag}_{k}") for k, p, g in zip(names, parts, got)])

    g0 = {}
    rides = {
        "s5_scan_bwd": _Ride(lambda: _join([_plan_reduce_sibling(parts1),
                                            _plan_reduce_sibling(box.setdefault("early0", _owner_major(g0, EARLY, "l0")))]),
                             lambda o: box.update(got1=o[:6], got0e=o[6:])),
        "fox_attn_bwd": _Ride(lambda: chips_plan(SHARDED[:1], parts1[:1], box["got1"][:1], "l1"), lambda o: box.update(contribs1a=o)),
        "mla_attn_bwd": _Ride(lambda: _join([chips_plan(SHARDED[1:], parts1[1:], box["got1"][1:], "l1"),
                                             chips_plan(EARLY, box["early0"], box["got0e"], "l0")]),
                              lambda o: box.update(contribs1b=o[:5], contribs0e=o[5:])),
        "dh3": _Ride(lambda: _plan_reduce_sibling(box.setdefault("late0", _owner_major(g0, LATE, "l0"))),
                     lambda o: box.update(got0l=o)),
        "dh1": _Ride(lambda: chips_plan(LATE, box["late0"], box["got0l"], "l0"), lambda o: box.update(contribs0l=o)),
    }
    d, _ = _layer_bwd(0, d, tabs, q0, sv0, rides, g0)
    contribs0 = list(box["contribs0l"]) + list(box["contribs0e"])
    return loss[0, 0], d, [g0, g1], [contribs0, list(box["contribs1a"]) + list(box["contribs1b"])]


def kernel(x, positions, norm_g, w_in, mla_q_a_norm, mla_w_q_up, mla_kv_a_norm, mla_w_kv_up, mla_q_norm, mla_k_norm, fox_b_f, fox_q_norm, fox_k_norm, s5_lambda_re, s5_lambda_im, s5_log_dt, s5_b_re, s5_b_im, s5_c_re, s5_c_im, s5_d, s5_w_glu, s5_b_glu, w_branch_out, w_out, loss_target, m_norm_g, m_w_in, m_mla_q_a_norm, m_mla_w_q_up, m_mla_kv_a_norm, m_mla_w_kv_up, m_mla_q_norm, m_mla_k_norm, m_fox_b_f, m_fox_q_norm, m_fox_k_norm, m_s5_lambda_re, m_s5_lambda_im, m_s5_log_dt, m_s5_b_re, m_s5_b_im, m_s5_c_re, m_s5_c_im, m_s5_d, m_s5_w_glu, m_s5_b_glu, m_w_branch_out, m_w_out, v_norm_g, v_w_in, v_mla_q_a_norm, v_mla_w_q_up, v_mla_kv_a_norm, v_mla_w_kv_up, v_mla_q_norm, v_mla_k_norm, v_fox_b_f, v_fox_q_norm, v_fox_k_norm, v_s5_lambda_re, v_s5_lambda_im, v_s5_log_dt, v_s5_b_re, v_s5_b_im, v_s5_c_re, v_s5_c_im, v_s5_d, v_s5_w_glu, v_s5_b_glu, v_w_branch_out, v_w_out):
    env = dict(locals())
    wts = {k: env[k] for k in WEIGHTS}
    mom = {k: env["m_" + k] for k in WEIGHTS}
    var = {k: env["v_" + k] for k in WEIGHTS}

    shards = [[wts[k][l].astype(BF16) for k in SHARDED] for l in range(DEPTH)]
    small = {k: wts[k] for k in SMALL}
    loss, dx, grads, contribs = _device_step(x[0], positions[0], loss_target[0], shards, small)
    loss = lax.psum(loss, ("x", "y", "c"))

    two_d = {k: (wts[k].shape[0] * wts[k].shape[1], wts[k].shape[2]) for k in SHARDED}
    sm = {k: jnp.stack([g[k] for g in grads]).reshape(wts[k].shape) for k in SMALL}
    small_block = _pack_rows(jnp.concatenate([sm[k].reshape(-1) for k in SMALL]), LANES, 256)
    box = {}
    rides = {SHARDED[0]: _Ride(lambda: _plan_gather_ici([small_block]), lambda o: box.update(ici=o)),
             SHARDED[1]: _Ride(lambda: _plan_gather_d2d(box["ici"]), lambda o: box.update(all=o))}
    grad_out, delta_out, m_out, v_out = {}, {}, {}, {}
    for i, k in enumerate(SHARDED):
        shp = wts[k].shape
        res = _adamw_sum(wts[k].reshape(two_d[k]), [contribs[l][i] for l in range(DEPTH)], mom[k].reshape(two_d[k]),
                         var[k].reshape(two_d[k]), f"adamw_{k}", ride=rides.get(k))
        grad_out[k], delta_out[k], m_out[k], v_out[k] = (z.reshape(shp) for z in res)
    g_small = _sum_leading(box["all"][0], "sum_small_grads").reshape(-1)
    off = 0
    for k in SMALL:
        cnt = int(np.prod(wts[k].shape))
        grad_out[k] = g_small[off:off + cnt].reshape(wts[k].shape)
        off += cnt
    flat2 = lambda a: a.reshape(-1, a.shape[-1])
    d_s, m_s, v_s = _adamw_many([flat2(wts[k]) for k in SMALL], [flat2(grad_out[k]) for k in SMALL],
                                [flat2(mom[k]) for k in SMALL], [flat2(var[k]) for k in SMALL], "adamw_small")
    for i, k in enumerate(SMALL):
        delta_out[k], m_out[k], v_out[k] = (z[i].reshape(wts[k].shape) for z in (d_s, m_s, v_s))

    return (loss, dx[None], *[grad_out[k] for k in WEIGHTS], *[delta_out[k] for k in WEIGHTS],
            *[m_out[k] for k in WEIGHTS], *[v_out[k] for k in WEIGHTS])
```

```python
import functools
import math

import jax
import jax.numpy as jnp
import numpy as np
from jax import lax
from jax.experimental import pallas as pl
from jax.experimental.pallas import tpu as pltpu

F32 = jnp.float32
BF16 = jnp.bfloat16

D_MODEL = 1024
DEPTH = 2
CHUNK = 64
EPS = 1e-6
HEADS = 8
MLA_NOPE, MLA_ROPE, MLA_V = 64, 32, 64
MLA_Q_RANK, MLA_KV_RANK = 256, 128
MLA_QK = MLA_NOPE + MLA_ROPE
ROPE_THETA = 10000.0
FOX_DIM = 64
S5_WIDTH, S5_GROUP, S5_GROUPS, S5_STATE = 512, 16, 32, 64
S5_LANES = S5_GROUPS * S5_STATE
IN_WIDTH = 7080
N_DEV = 8
LANES = 128
SUBLANES = 8

ADAM_LR, ADAM_B1, ADAM_B2, ADAM_EPS, ADAM_WD, ADAM_STEP = 0.001, 0.9, 0.999, 1e-08, 0.01, 10

SEG_W = (512, 1536, 512, 4608)
SEG_OFF = (0, 512, 2048, 2560)
NSEG = len(SEG_W)
GATES_W = 1536
PAD_IN = 7168
NEG = -1e30

SHARDED = ("w_in", "mla_w_q_up", "mla_w_kv_up", "s5_w_glu", "w_branch_out", "w_out")
COL_SHARDED = ("w_in", "mla_w_q_up", "mla_w_kv_up")
SMALL = ("norm_g", "mla_q_a_norm", "mla_kv_a_norm", "mla_q_norm", "mla_k_norm", "fox_b_f", "fox_q_norm", "fox_k_norm",
         "s5_lambda_re", "s5_lambda_im", "s5_log_dt", "s5_b_re", "s5_b_im", "s5_c_re", "s5_c_im", "s5_d", "s5_b_glu")
WEIGHTS = ("norm_g", "w_in", "mla_q_a_norm", "mla_w_q_up", "mla_kv_a_norm", "mla_w_kv_up", "mla_q_norm", "mla_k_norm",
           "fox_b_f", "fox_q_norm", "fox_k_norm", "s5_lambda_re", "s5_lambda_im", "s5_log_dt", "s5_b_re", "s5_b_im",
           "s5_c_re", "s5_c_im", "s5_d", "s5_w_glu", "s5_b_glu", "w_branch_out", "w_out")


def _pick(n, cands):
    for c in cands:
        if n % c == 0:
            return c
    return n


def _vmem(mb):
    return pltpu.CompilerParams(vmem_limit_bytes=mb * 1024 * 1024)


def _dot(a, b, dims):
    return lax.dot_general(a.astype(BF16), b.astype(BF16), (dims, ((), ())), preferred_element_type=F32)


def _nn(a, b):
    return _dot(a, b, ((1,), (0,)))


def _nt(a, b):
    return _dot(a, b, ((1,), (1,)))


def _tn(a, b):
    return _dot(a, b, ((0,), (0,)))


def _rms(x, g, n=None):
    n = x.shape[-1] if n is None else n
    return x * lax.rsqrt(jnp.sum(x * x, axis=-1, keepdims=True) / n + EPS) * g


def _rope(t, c, sa, sb):
    return t * c + pltpu.roll(t, LANES - 16, 1) * sa + pltpu.roll(t, 16, 1) * sb


def _rope_t(d, c, sa, sb):
    return d * c + pltpu.roll(d * sa, 16, 1) + pltpu.roll(d * sb, LANES - 16, 1)


def _mm(a, b, mode, name, acc=None, b_cols=None, ride=None):
    if mode == "tn":
        kd, m = a.shape
    else:
        m, kd = a.shape
    b_off, b_w = b_cols if b_cols is not None else (0, b.shape[1])
    n = b.shape[0] if mode == "nt" else b_w
    tm, tn, tk = _pick(m, (1024, 512, 256, 128)), _pick(n, (1024, 512, 256, 128)), _pick(kd, (1024, 512, 256, 128))
    nk = kd // tk
    if mode == "tn":
        a_spec = pl.BlockSpec((tk, tm), lambda i, j, k: (k, i))
    else:
        a_spec = pl.BlockSpec((tm, tk), lambda i, j, k: (i, k))
    if mode == "nt":
        assert b_off % tk == 0
        b_spec = pl.BlockSpec((tn, tk), lambda i, j, k: (j, k + b_off // tk))
    else:
        assert b_off % tn == 0
        b_spec = pl.BlockSpec((tk, tn), lambda i, j, k: (k, j + b_off // tn))
    dims = {"nn": ((1,), (0,)), "nt": ((1,), (1,)), "tn": ((0,), (0,))}[mode]
    o_spec = pl.BlockSpec((tm, tn), lambda i, j, k: (i, j))
    has_acc = acc is not None

    def body(*refs):
        if has_acc:
            a_ref, b_ref, c_ref, o_ref = refs
        else:
            a_ref, b_ref, o_ref = refs
        k = pl.program_id(2)
        prod = _dot(a_ref[...], b_ref[...], dims)

        @pl.when(k == 0)
        def _():
            o_ref[...] = prod + c_ref[...] if has_acc else prod

        @pl.when(k > 0)
        def _():
            o_ref[...] += prod

    ins = [a, b] + ([acc] if has_acc else [])
    in_specs = [a_spec, b_spec] + ([o_spec] if has_acc else [])
    return _call_with_ride(body, name, (m // tm, n // tn, nk), ins, in_specs, [jax.ShapeDtypeStruct((m, n), F32)], [o_spec],
                           ("parallel", "parallel", "arbitrary"), 48, ride)[0]


def _stage(name, fn, n_steps, ins, outs, accs=(), scratch=(), vmem_mb=48):
    n_in, n_out, n_acc = len(ins), len(outs), len(accs)

    def body(*refs):
        in_refs = refs[:n_in]
        out_refs = refs[n_in:n_in + n_out]
        acc_refs = refs[n_in + n_out:n_in + n_out + n_acc]
        scr = refs[n_in + n_out + n_acc:]
        if n_acc:
            @pl.when(pl.program_id(0) == 0)
            def _():
                for r in acc_refs:
                    r[...] = jnp.zeros(r.shape, r.dtype)
        fn(in_refs, out_refs, acc_refs, scr)

    acc_specs = [pl.BlockSpec(a.shape, functools.partial(lambda i, nd: (0,) * nd, nd=len(a.shape))) for a in accs]
    res = pl.pallas_call(
        body, name=name, grid=(n_steps,),
        in_specs=[s for _, s in ins], out_specs=[s for _, s in outs] + acc_specs,
        out_shape=[s for s, _ in outs] + list(accs), scratch_shapes=list(scratch),
        compiler_params=pltpu.CompilerParams(dimension_semantics=("arbitrary",),
                                             vmem_limit_bytes=vmem_mb * 1024 * 1024),
    )(*[a for a, _ in ins])
    return res


def _rows(ts, w, j=0):
    return pl.BlockSpec((ts, w), lambda i: (i, j))


def _rows_rev(ts, w, n, j=0):
    return pl.BlockSpec((ts, w), lambda i: (n - 1 - i, j))


def _heads(ts, d):
    return pl.BlockSpec((HEADS, ts, d), lambda i: (0, i, 0))


def _heads_rev(ts, d, n):
    return pl.BlockSpec((HEADS, ts, d), lambda i: (0, n - 1 - i, 0))


def _full(shape):
    nd = len(shape)
    return pl.BlockSpec(tuple(shape), lambda i: (0,) * nd)


def _sds(shape, dtype=F32):
    return jax.ShapeDtypeStruct(tuple(shape), dtype)


def _norm_fwd(x, g, name):
    s = x.shape[0]
    ts = _pick(s, (256, 128))

    def fn(ins, outs, accs, scr):
        outs[0][...] = _rms(ins[0][...], ins[1][...]).astype(BF16)

    return _stage(name, fn, s // ts, [(x, _rows(ts, D_MODEL)), (g, _full(g.shape))],
                  [(_sds((s, D_MODEL), BF16), _rows(ts, D_MODEL))])[0]


def _norm_bwd(x, g, dh, dres, name):
    s = x.shape[0]
    ts = _pick(s, (256, 128))

    def fn(ins, outs, accs, scr):
        _, vjp = jax.vjp(_rms, ins[0][...], ins[1][...])
        dx, dg = vjp(ins[2][...])
        outs[0][...] = dx + ins[3][...]
        accs[0][...] += dg

    r = _stage(name, fn, s // ts,
               [(x, _rows(ts, D_MODEL)), (g, _full(g.shape)), (dh, _rows(ts, D_MODEL)), (dres, _rows(ts, D_MODEL))],
               [(_sds((s, D_MODEL)), _rows(ts, D_MODEL))], accs=[_sds((1, D_MODEL))])
    return r[0], r[1]


def _mla_q(qraw, c, sa, sb, qn):
    return _rms(_rope(qraw, c, sa, sb), qn, MLA_QK)


def _mla_prep_fwd(seg0, tabs, p, wq, wkv, name):
    s = seg0.shape[0]
    ts = _pick(s, (256, 128))

    def fn(ins, outs, accs, scr):
        blk, cos, sa, sb, qan, kvan, qn, kn, wq_r, wkv_r = ins
        b = blk[...]
        cq, ckv, kt = b[:, :256], b[:, 256:384], b[:, 384:512]
        lane = lax.broadcasted_iota(jnp.int32, kt.shape, 1)
        kpe = jnp.where(lane >= 64, kt, 0.0)
        q_raw = _nn(_rms(cq, qan[...]), wq_r[...])
        kv_raw = _nn(_rms(ckv, kvan[...]), wkv_r[...])
        c, a, bb = cos[...], sa[...], sb[...]
        for h in range(HEADS):
            outs[0][h] = _mla_q(q_raw[:, LANES * h:LANES * (h + 1)], c, a, bb, qn[...]).astype(BF16)
            outs[1][h] = _mla_q(kv_raw[:, LANES * h:LANES * (h + 1)] + kpe, c, a, bb, kn[...]).astype(BF16)
            outs[2][h] = kv_raw[:, 1024 + 64 * h:1024 + 64 * (h + 1)].astype(BF16)

    consts = [p["mla_q_a_norm"], p["mla_kv_a_norm"], p["mla_q_norm"], p["mla_k_norm"], wq, wkv]
    return _stage(name, fn, s // ts,
                  [(seg0, _rows(ts, 512))] + [(t, _rows(ts, LANES)) for t in tabs] + [(a, _full(a.shape)) for a in consts],
                  [(_sds((HEADS, s, LANES), BF16), _heads(ts, LANES)), (_sds((HEADS, s, LANES), BF16), _heads(ts, LANES)),
                   (_sds((HEADS, s, 64), BF16), _heads(ts, 64))])


def _mla_prep_bwd(seg0, tabs, p, wq, wkv, dq, dk, dv, dff, name):
    s = seg0.shape[0]
    ts = _pick(s, (256, 128))

    def fn(ins, outs, accs, scr):
        blk, cos, sa, sb, qan, kvan, qn, kn, wq_r, wkv_r, dq_r, dk_r, dv_r, dff_r = ins
        dqan, dkvan, dqn, dkn, dwq, dwkv = accs
        dqraw_s, dkvraw_s = scr
        b = blk[...]
        cq, ckv, kt = b[:, :256], b[:, 256:384], b[:, 384:512]
        lane = lax.broadcasted_iota(jnp.int32, kt.shape, 1)
        kpe = jnp.where(lane >= 64, kt, 0.0)
        cqn, vjp_cq = jax.vjp(_rms, cq, qan[...])
        ckvn, vjp_ckv = jax.vjp(_rms, ckv, kvan[...])
        q_raw = _nn(cqn, wq_r[...])
        kv_raw = _nn(ckvn, wkv_r[...])
        c, a, bb = cos[...], sa[...], sb[...]

        def head_bwd(raw, gain, d):
            t = _rope(raw, c, a, bb)
            _, vjp = jax.vjp(functools.partial(_rms, n=MLA_QK), t, gain)
            dt, dgain = vjp(d)
            return _rope_t(dt, c, a, bb), dgain

        dkpe = jnp.zeros(kt.shape, F32)
        for h in range(HEADS):
            dqh, dg = head_bwd(q_raw[:, LANES * h:LANES * (h + 1)], qn[...], dq_r[h])
            dqn[...] += dg
            dqraw_s[:, LANES * h:LANES * (h + 1)] = dqh
            dkh, dg = head_bwd(kv_raw[:, LANES * h:LANES * (h + 1)] + kpe, kn[...], dk_r[h])
            dkn[...] += dg
            dkvraw_s[:, LANES * h:LANES * (h + 1)] = dkh
            dkpe = dkpe + dkh
            dkvraw_s[:, 1024 + 64 * h:1024 + 64 * (h + 1)] = dv_r[h]
        dq_raw = dqraw_s[...]
        dkv_raw = dkvraw_s[...]
        dwq[...] += _tn(cqn, dq_raw)
        dwkv[...] += _tn(ckvn, dkv_raw)
        dcq, dg = vjp_cq(_nt(dq_raw, wq_r[...]))
        dqan[...] += dg
        dckv, dg = vjp_ckv(_nt(dkv_raw, wkv_r[...]))
        dkvan[...] += dg
        outs[0][:, 0:256] = dcq.astype(BF16)
        outs[0][:, 256:384] = dckv.astype(BF16)
        outs[0][:, 384:512] = (jnp.where(lane >= 64, dkpe, 0.0) + dff_r[...]).astype(BF16)

    consts = [p["mla_q_a_norm"], p["mla_kv_a_norm"], p["mla_q_norm"], p["mla_k_norm"], wq, wkv]
    return _stage(name, fn, s // ts,
                  [(seg0, _rows(ts, 512))] + [(t, _rows(ts, LANES)) for t in tabs] + [(a, _full(a.shape)) for a in consts]
                  + [(dq, _heads(ts, LANES)), (dk, _heads(ts, LANES)), (dv, _heads(ts, 64)), (dff, _rows(ts, LANES))],
                  [(_sds((s, 512), BF16), _rows(ts, 512))],
                  accs=[_sds((1, 256)), _sds((1, 128)), _sds((1, LANES)), _sds((1, LANES)), _sds(wq.shape), _sds(wkv.shape)],
                  scratch=[pltpu.VMEM((ts, 1024), F32), pltpu.VMEM((ts, 1536), F32)])


def _fox_prep_fwd(seg0, seg1, bf, qn, kn, name):
    s = seg0.shape[0]
    ts = _pick(s, (256, 128))
    steps = int(math.log2(ts))

    def fn(ins, outs, accs, scr):
        kt_r, x_r, bf_r, qn_r, kn_r = ins
        carry = scr[0]

        @pl.when(pl.program_id(0) == 0)
        def _():
            carry[...] = jnp.zeros(carry.shape, F32)

        x = x_r[...]
        for h in range(HEADS):
            outs[0][h] = _rms(x[:, 64 * h:64 * (h + 1)], qn_r[...]).astype(BF16)
            outs[1][h] = _rms(x[:, 512 + 64 * h:512 + 64 * (h + 1)], kn_r[...]).astype(BF16)
            outs[2][h] = x[:, 1024 + 64 * h:1024 + 64 * (h + 1)].astype(BF16)
        kt = kt_r[...]
        lane = lax.broadcasted_iota(jnp.int32, kt.shape, 1)
        row = lax.broadcasted_iota(jnp.int32, kt.shape, 0)
        cs = jnp.where(lane < HEADS, jax.nn.log_sigmoid(kt + bf_r[...]), 0.0)
        for k in range(steps):
            sh = 1 << k
            cs = cs + jnp.where(row >= sh, pltpu.roll(cs, sh, 0), 0.0)
        cs = cs + carry[0:1, :]
        outs[3][...] = cs
        outs[4][...] = cs.T[0:HEADS, :]
        carry[0:1, :] = cs[ts - 1:ts, :]

    return _stage(name, fn, s // ts,
                  [(seg0, _rows(ts, LANES, 3)), (seg1, _rows(ts, 1536)), (bf, _full(bf.shape)), (qn, _full(qn.shape)),
                   (kn, _full(kn.shape))],
                  [(_sds((HEADS, s, 64), BF16), _heads(ts, 64)), (_sds((HEADS, s, 64), BF16), _heads(ts, 64)),
                   (_sds((HEADS, s, 64), BF16), _heads(ts, 64)), (_sds((s, LANES)), _rows(ts, LANES)),
                   (_sds((HEADS, s)), pl.BlockSpec((HEADS, ts), lambda i: (0, i)))],
                  scratch=[pltpu.VMEM((SUBLANES, LANES), F32)])


def _fox_prep_bwd(seg0, seg1, bf, qn, kn, dq, dk, dv, dck, name):
    s = seg0.shape[0]
    ts = _pick(s, (256, 128))
    n = s // ts
    steps = int(math.log2(ts))

    def fn(ins, outs, accs, scr):
        kt_r, x_r, bf_r, qn_r, kn_r, dq_r, dk_r, dv_r, dck_r = ins
        dqn, dkn, dbf = accs
        carry, dbuf = scr

        @pl.when(pl.program_id(0) == 0)
        def _():
            carry[...] = jnp.zeros(carry.shape, F32)

        x = x_r[...]
        for h in range(HEADS):
            _, vjp = jax.vjp(_rms, x[:, 64 * h:64 * (h + 1)], qn_r[...])
            d, dg = vjp(dq_r[h])
            dbuf[:, 64 * h:64 * (h + 1)] = d
            dqn[...] += dg
            _, vjp = jax.vjp(_rms, x[:, 512 + 64 * h:512 + 64 * (h + 1)], kn_r[...])
            d, dg = vjp(dk_r[h])
            dbuf[:, 512 + 64 * h:512 + 64 * (h + 1)] = d
            dkn[...] += dg
            dbuf[:, 1024 + 64 * h:1024 + 64 * (h + 1)] = dv_r[h]
        outs[0][...] = dbuf[...].astype(BF16)
        dc = dck_r[...].reshape(HEADS, ts)
        dc = jnp.concatenate([dc, jnp.zeros((LANES - HEADS, ts), F32)], axis=0).T
        row = lax.broadcasted_iota(jnp.int32, dc.shape, 0)
        lane = lax.broadcasted_iota(jnp.int32, dc.shape, 1)
        for k in range(steps):
            sh = 1 << k
            dc = dc + jnp.where(row < ts - sh, pltpu.roll(dc, ts - sh, 0), 0.0)
        dc = dc + carry[0:1, :]
        carry[0:1, :] = dc[0:1, :]
        dff = jnp.where(lane < HEADS, dc * jax.nn.sigmoid(-(kt_r[...] + bf_r[...])), 0.0)
        outs[1][...] = dff
        dbf[...] += jnp.sum(dff, axis=0, keepdims=True)

    return _stage(name, fn, n,
                  [(seg0, _rows_rev(ts, LANES, n, 3)), (seg1, _rows_rev(ts, 1536, n)), (bf, _full(bf.shape)),
                   (qn, _full(qn.shape)), (kn, _full(kn.shape)), (dq, _heads_rev(ts, 64, n)), (dk, _heads_rev(ts, 64, n)),
                   (dv, _heads_rev(ts, 64, n)), (dck, pl.BlockSpec((HEADS, 1, ts), lambda i: (0, 0, n - 1 - i)))],
                  [(_sds((s, 1536), BF16), _rows_rev(ts, 1536, n)), (_sds((s, LANES)), _rows_rev(ts, LANES, n))],
                  accs=[_sds((1, 64)), _sds((1, 64)), _sds((1, LANES))],
                  scratch=[pltpu.VMEM((SUBLANES, LANES), F32), pltpu.VMEM((ts, 1536), F32)])


def _allowed(i, j, t, chunk_causal):
    qpos = i * t + lax.broadcasted_iota(jnp.int32, (t, t), 0)
    kpos = j * t + lax.broadcasted_iota(jnp.int32, (t, t), 1)
    if chunk_causal:
        return (kpos // CHUNK) <= (qpos // CHUNK)
    return kpos <= qpos


def _pick_col(c_blk, h):
    lane = lax.broadcasted_iota(jnp.int32, c_blk.shape, 1)
    return jnp.sum(jnp.where(lane == h, c_blk, 0.0), axis=1, keepdims=True)


class _Ride:
    def __init__(self, make, take):
        self.make, self.take = make, take


class _Plan:
    def __init__(self, ins, out_shapes, n_remote, n_local, copies, aliases=None):
        self.ins, self.out_shapes, self.n_remote, self.n_local = list(ins), list(out_shapes), n_remote, n_local
        self.copies, self.aliases = copies, dict(aliases or {})

    def scratch(self):
        return [pltpu.SemaphoreType.DMA((self.n_remote,)), pltpu.SemaphoreType.DMA((self.n_remote,)),
                pltpu.SemaphoreType.DMA((max(self.n_local, 1),))]

    def start(self, in_refs, out_refs, sems):
        remote, local = self.copies(in_refs, out_refs, *sems)
        for cp in local + remote:
            cp.start()

    def wait(self, in_refs, out_refs, sems):
        remote, local = self.copies(in_refs, out_refs, *sems)
        for cp in remote:
            cp.wait()
        for cp in local:
            cp.wait()


class _Off:
    def __init__(self, ref, off):
        self.ref, self.off, self.at = ref, off, self

    def __getitem__(self, k):
        return self.ref.at[k + self.off]


def _join(plans):
    ins = [a for p in plans for a in p.ins]
    outs = [o for p in plans for o in p.out_shapes]
    aliases, i0, o0 = {}, 0, 0
    for p in plans:
        aliases.update({i0 + i: o0 + o for i, o in p.aliases.items()})
        i0, o0 = i0 + len(p.ins), o0 + len(p.out_shapes)

    def copies(in_refs, out_refs, send, recv, local):
        rem, loc, i0, o0, r0, l0 = [], [], 0, 0, 0, 0
        for p in plans:
            r, l = p.copies(in_refs[i0:i0 + len(p.ins)], out_refs[o0:o0 + len(p.out_shapes)], _Off(send, r0), _Off(recv, r0),
                            _Off(local, l0))
            rem, loc = rem + r, loc + l
            i0, o0, r0, l0 = i0 + len(p.ins), o0 + len(p.out_shapes), r0 + p.n_remote, l0 + p.n_local
        return rem, loc

    return _Plan(ins, outs, sum(p.n_remote for p in plans), sum(p.n_local for p in plans), copies, aliases)


def _call_with_ride(core, name, grid, ins, in_specs, out_shape, out_specs, semantics, vmem_mb, ride, scratch=()):
    n_in, n_out, n_scr = len(ins), len(out_shape), len(scratch)
    if ride is None:
        return pl.pallas_call(
            core, name=name, grid=grid, in_specs=in_specs, out_specs=out_specs, out_shape=out_shape,
            scratch_shapes=list(scratch),
            compiler_params=pltpu.CompilerParams(dimension_semantics=semantics, vmem_limit_bytes=vmem_mb * 1024 * 1024),
        )(*ins)
    plan = ride.make()
    ci, co = len(plan.ins), len(plan.out_shapes)

    def body(*refs):
        c_in = refs[n_in:n_in + ci]
        a_out = refs[n_in + ci:n_in + ci + n_out]
        c_out = refs[n_in + ci + n_out:n_in + ci + n_out + co]
        own = refs[n_in + ci + n_out + co:n_in + ci + n_out + co + n_scr]
        sems = refs[n_in + ci + n_out + co + n_scr:]
        ids = [pl.program_id(d) for d in range(len(grid))]
        first = functools.reduce(jnp.logical_and, [i == 0 for i in ids])
        last = functools.reduce(jnp.logical_and, [i == g - 1 for i, g in zip(ids, grid)])

        @pl.when(first)
        def _():
            plan.start(c_in, c_out, sems)

        core(*refs[:n_in], *a_out, *own)

        @pl.when(last)
        def _():
            plan.wait(c_in, c_out, sems)

    res = pl.pallas_call(
        body, name=name, grid=grid, in_specs=list(in_specs) + [ANY] * ci, out_specs=list(out_specs) + [ANY] * co,
        out_shape=list(out_shape) + plan.out_shapes, scratch_shapes=list(scratch) + plan.scratch(),
        input_output_aliases={n_in + i: n_out + o for i, o in plan.aliases.items()},
        compiler_params=pltpu.CompilerParams(dimension_semantics=("arbitrary",) * len(grid),
                                             vmem_limit_bytes=vmem_mb * 1024 * 1024),
    )(*ins, *plan.ins)
    ride.take(res[n_out:])
    return res[:n_out]


def _attn_fwd(q, k, v, scale, chunk_causal, name, c=None, ct=None, hps=8, ride=None):
    _, s, dk = q.shape
    dv = v.shape[2]
    t = _pick(s, (256, 128))
    n = s // t
    bias = c is not None

    def body(*refs):
        if bias:
            q_ref, k_ref, v_ref, c_ref, ct_ref, o_ref, lse_ref, s_s = refs
        else:
            q_ref, k_ref, v_ref, o_ref, lse_ref, s_s = refs
        hp, i = pl.program_id(0), pl.program_id(1)
        qb = [q_ref[e] for e in range(hps)]
        cq = [_pick_col(c_ref[...], hp * hps + e) if bias else None for e in range(hps)]

        def sweep1(j, m, diagonal):
            off = pl.multiple_of(j * t, t)
            out = []
            for e in range(hps):
                sc = _nt(qb[e], k_ref[e, pl.ds(off, t), :]) * scale
                if bias:
                    sc = sc + (cq[e] - ct_ref[pl.ds(hp * hps + e, 1), pl.ds(off, t)])
                if diagonal:
                    sc = jnp.where(_allowed(i, j, t, chunk_causal), sc, NEG)
                s_s[e, j] = sc
                out.append(jnp.maximum(m[e], jnp.max(sc, axis=1, keepdims=True)))
            return tuple(out)

        m0 = tuple(jnp.full((t, 1), NEG, F32) for _ in range(hps))
        m = sweep1(i, lax.fori_loop(0, i, functools.partial(sweep1, diagonal=False), m0), True)

        def sweep2(j, carry):
            off = pl.multiple_of(j * t, t)
            out = []
            for e in range(hps):
                l, acc = carry[e]
                pr = jnp.exp(s_s[e, j] - m[e])
                out.append((l + jnp.sum(pr, axis=1, keepdims=True), acc + _nn(pr, v_ref[e, pl.ds(off, t), :])))
            return tuple(out)

        init = tuple((jnp.zeros((t, 1), F32), jnp.zeros((t, dv), F32)) for _ in range(hps))
        res = lax.fori_loop(0, i + 1, sweep2, init)
        for e in range(hps):
            l, acc = res[e]
            o_ref[e] = acc / l
            lse_ref[e] = m[e] + jnp.log(l)

    ins = [q, k, v] + ([c, ct] if bias else [])
    in_specs = [pl.BlockSpec((hps, t, dk), lambda h, i: (h, i, 0)), pl.BlockSpec((hps, s, dk), lambda h, i: (h, 0, 0)),
                pl.BlockSpec((hps, s, dv), lambda h, i: (h, 0, 0))]
    if bias:
        in_specs += [pl.BlockSpec((t, LANES), lambda h, i: (i, 0)), pl.BlockSpec((HEADS, s), lambda h, i: (0, 0))]
    return _call_with_ride(
        body, name, (HEADS // hps, n), ins, in_specs, [_sds((HEADS, s, dv)), _sds((HEADS, s, 1))],
        [pl.BlockSpec((hps, t, dv), lambda h, i: (h, i, 0)), pl.BlockSpec((hps, t, 1), lambda h, i: (h, i, 0))],
        ("parallel", "parallel"), 56, ride, scratch=[pltpu.VMEM((hps, n, t, t), F32)])


def _attn_bwd_rows(q, k, v, do, lse, scale, chunk_causal, name, c=None, ct=None, hps=4, ride=None):
    _, s, dk = q.shape
    dv = v.shape[2]
    t = _pick(s, (256, 128))
    n = s // t
    bias = c is not None

    def body(*refs):
        if bias:
            q_ref, k_ref, v_ref, do_ref, lse_ref, c_ref, ct_ref, dq_ref, dk_ref, dv_ref, dck_ref, p_s, dp_s = refs
        else:
            q_ref, k_ref, v_ref, do_ref, lse_ref, dq_ref, dk_ref, dv_ref, p_s, dp_s = refs
        hp, i = pl.program_id(0), pl.program_id(1)

        @pl.when(i == 0)
        def _():
            dk_ref[...] = jnp.zeros(dk_ref.shape, F32)
            dv_ref[...] = jnp.zeros(dv_ref.shape, F32)
            if bias:
                dck_ref[...] = jnp.zeros(dck_ref.shape, F32)

        qb = [q_ref[e] for e in range(hps)]
        dob = [do_ref[e] for e in range(hps)]
        lse_b = [lse_ref[e] for e in range(hps)]
        cq = [_pick_col(c_ref[...], hp * hps + e) if bias else None for e in range(hps)]

        def sweep1(j, acc, diagonal):
            off = pl.multiple_of(j * t, t)
            out = []
            for e in range(hps):
                sc = _nt(qb[e], k_ref[e, pl.ds(off, t), :]) * scale
                if bias:
                    sc = sc + (cq[e] - ct_ref[pl.ds(hp * hps + e, 1), pl.ds(off, t)])
                pr = jnp.exp(sc - lse_b[e])
                if diagonal:
                    pr = jnp.where(_allowed(i, j, t, chunk_causal), pr, 0.0)
                dp = _nt(dob[e], v_ref[e, pl.ds(off, t), :])
                p_s[e, j] = pr
                dp_s[e, j] = dp
                out.append(acc[e] + jnp.sum(pr * dp, axis=1, keepdims=True))
            return tuple(out)

        zero = tuple(jnp.zeros((t, 1), F32) for _ in range(hps))
        delta = sweep1(i, lax.fori_loop(0, i, functools.partial(sweep1, diagonal=False), zero), True)

        def sweep2(j, dq_acc):
            off = pl.multiple_of(j * t, t)
            out = []
            for e in range(hps):
                pr = p_s[e, j]
                ds = pr * (dp_s[e, j] - delta[e])
                kb = k_ref[e, pl.ds(off, t), :]
                dv_ref[e, pl.ds(off, t), :] += _tn(pr, dob[e])
                dk_ref[e, pl.ds(off, t), :] += _tn(ds, qb[e]) * scale
                if bias:
                    dck_ref[e, j] -= jnp.sum(ds, axis=0, keepdims=True)
                out.append(dq_acc[e] + _nn(ds, kb))
            return tuple(out)

        dq = lax.fori_loop(0, i + 1, sweep2, tuple(jnp.zeros((t, dk), F32) for _ in range(hps)))
        for e in range(hps):
            dq_ref[e] = dq[e] * scale

    full = lambda d: pl.BlockSpec((hps, s, d), lambda h, i: (h, 0, 0))
    blk = lambda d: pl.BlockSpec((hps, t, d), lambda h, i: (h, i, 0))
    ins = [q, k, v, do, lse] + ([c, ct] if bias else [])
    in_specs = [blk(dk), full(dk), full(dv), blk(dv), blk(1)]
    out_specs = [blk(dk), full(dk), full(dv)]
    out_shape = [_sds((HEADS, s, dk)), _sds((HEADS, s, dk)), _sds((HEADS, s, dv))]
    if bias:
        in_specs += [pl.BlockSpec((t, LANES), lambda h, i: (i, 0)), pl.BlockSpec((HEADS, s), lambda h, i: (0, 0))]
        out_specs.append(pl.BlockSpec((hps, n, 1, t), lambda h, i: (h, 0, 0, 0)))
        out_shape.append(_sds((HEADS, n, 1, t)))
    res = _call_with_ride(body, name, (HEADS // hps, n), ins, in_specs, out_shape, out_specs, ("parallel", "arbitrary"), 56,
                          ride, scratch=[pltpu.VMEM((hps, n, t, t), F32), pltpu.VMEM((hps, n, t, t), F32)])
    return (*res[:3], res[3].reshape(HEADS, 1, s)) if bias else tuple(res)


def _s5_disc(lr, li, ldt, br, bi):
    dt = jnp.exp(ldt)
    mag = jnp.exp(lr * dt)
    a_re = mag * jnp.cos(li * dt)
    a_im = mag * jnp.sin(li * dt)
    den = lr * lr + li * li
    f_re = ((a_re - 1.0) * lr + a_im * li) / den
    f_im = (a_im * lr - (a_re - 1.0) * li) / den
    return a_re, a_im, f_re * br - f_im * bi, f_re * bi + f_im * br


def _s5_params_fwd(lr16, li16, ldt16, br2, bi2, name):
    def body(a, b, c, d, e, o0, o1, o2, o3):
        r = _s5_disc(a[...], b[...], c[...], d[...], e[...])
        o0[...], o1[...], o2[...], o3[...] = r

    return pl.pallas_call(body, name=name, out_shape=[_sds((512, 64))] * 4)(lr16, li16, ldt16, br2, bi2)


def _s5_params_bwd(lr16, li16, ldt16, br2, bi2, da_re16, da_im16, dbb_re, dbb_im, name):
    def body(a, b, c, d, e, g0, g1, g2, g3, o_lr, o_li, o_dt, o_br, o_bi):
        _, vjp = jax.vjp(_s5_disc, a[...], b[...], c[...], d[...], e[...])
        dlr, dli, dldt, dbr, dbi = vjp((g0[...], g1[...], g2[...], g3[...]))
        grp = lambda z: z.reshape(S5_GROUPS, S5_GROUP, S5_STATE).sum(axis=1)
        o_lr[...] = grp(dlr)
        o_li[...] = grp(dli)
        o_dt[...] = jnp.sum(grp(dldt), axis=1, keepdims=True)
        o_br[...] = dbr
        o_bi[...] = dbi

    return pl.pallas_call(
        body, name=name, out_shape=[_sds((32, 64)), _sds((32, 64)), _sds((32, 1)), _sds((512, 64)), _sds((512, 64))],
    )(lr16, li16, ldt16, br2, bi2, da_re16, da_im16, dbb_re, dbb_im)


def _cmul(ar, ai, br, bi):
    return ar * br - ai * bi, ar * bi + ai * br


S5_SUPER = 4


def _scan_loop(bre_r, bim_r, ar, ai, ore_r, oim_r, reverse, xre_r=None, xim_r=None):
    s, lw = bre_r.shape
    nt = s // SUBLANES
    with_da = xre_r is not None
    shp = (SUBLANES, lw)
    row = lax.broadcasted_iota(jnp.int32, shp, 0)
    pows = [(ar, ai)]
    for _ in range(SUBLANES - 1):
        pows.append(_cmul(pows[-1][0], pows[-1][1], ar, ai))
    cm_r, cm_i = jnp.zeros(shp, F32), jnp.zeros(shp, F32)
    for r in range(SUBLANES):
        e = (SUBLANES - 1 - r) if reverse else r
        cm_r = jnp.where(row == r, jnp.broadcast_to(pows[e][0], shp), cm_r)
        cm_i = jnp.where(row == r, jnp.broadcast_to(pows[e][1], shp), cm_i)
    steps = [(1, pows[0]), (2, pows[1]), (4, pows[3])]

    def tile(it, carry):
        if with_da:
            c_r, c_i, acc_r, acc_i = carry
        else:
            c_r, c_i = carry
        r = (nt - 1 - it) if reverse else it
        off = pl.multiple_of(r * SUBLANES, SUBLANES)
        xr, xi = bre_r[pl.ds(off, SUBLANES), :], bim_r[pl.ds(off, SUBLANES), :]
        for sh, (pr, pi) in steps:
            if reverse:
                keep = row < SUBLANES - sh
                sr = jnp.where(keep, pltpu.roll(xr, SUBLANES - sh, 0), 0.0)
                si = jnp.where(keep, pltpu.roll(xi, SUBLANES - sh, 0), 0.0)
            else:
                keep = row >= sh
                sr = jnp.where(keep, pltpu.roll(xr, sh, 0), 0.0)
                si = jnp.where(keep, pltpu.roll(xi, sh, 0), 0.0)
            mr, mi = _cmul(pr, pi, sr, si)
            xr, xi = xr + mr, xi + mi
        mr, mi = _cmul(cm_r, cm_i, c_r, c_i)
        xr, xi = xr + mr, xi + mi
        ore_r[pl.ds(off, SUBLANES), :] = xr
        oim_r[pl.ds(off, SUBLANES), :] = xi
        edge = 0 if reverse else SUBLANES - 1
        c_r, c_i = xr[edge:edge + 1, :], xi[edge:edge + 1, :]
        if not with_da:
            return c_r, c_i
        fr, fi = xre_r[pl.ds(off, SUBLANES), :], xim_r[pl.ds(off, SUBLANES), :]
        poff = pl.multiple_of(jnp.maximum(r - 1, 0) * SUBLANES, SUBLANES)
        live = (r > 0).astype(F32)
        pr_last = xre_r[pl.ds(poff, SUBLANES), :][SUBLANES - 1:SUBLANES, :] * live
        pi_last = xim_r[pl.ds(poff, SUBLANES), :][SUBLANES - 1:SUBLANES, :] * live
        sr = jnp.where(row >= 1, pltpu.roll(fr, 1, 0), jnp.broadcast_to(pr_last, shp))
        si = jnp.where(row >= 1, pltpu.roll(fi, 1, 0), jnp.broadcast_to(pi_last, shp))
        return c_r, c_i, acc_r + xr * sr + xi * si, acc_i + xi * sr - xr * si

    z1 = jnp.zeros((1, lw), F32)
    if not with_da:
        lax.fori_loop(0, nt, tile, (z1, z1))
        return None
    _, _, acc_r, acc_i = lax.fori_loop(0, nt, tile, (z1, z1, jnp.zeros(shp, F32), jnp.zeros(shp, F32)))
    return jnp.sum(acc_r, axis=0, keepdims=True), jnp.sum(acc_i, axis=0, keepdims=True)


S5_ROWS = 512


def _group_compact(p):
    grp = lax.broadcasted_iota(jnp.int32, (LANES, S5_STATE), 0) // S5_GROUP
    out = jnp.zeros((LANES, S5_STATE), F32)
    for j in range(LANES // S5_GROUP):
        out = jnp.where(grp == j, p[:, S5_STATE * j:S5_STATE * (j + 1)], out)
    return out


def _s5_core_fwd(u, wb_re, wb_im, wc_re, wc_im, a_re, a_im, name, ride=None):
    s = u.shape[0]
    lw = S5_LANES // S5_SUPER
    rows = _pick(s, (S5_ROWS, 256, 128))

    def body(u_r, wbr, wbi, wcr, wci, are_r, aim_r, xre_r, xim_r, y_r, bre_s, bim_s):
        for r0 in range(0, s, rows):
            ub = u_r[r0:r0 + rows, :]
            bre_s[r0:r0 + rows, :] = _nn(ub, wbr[0])
            bim_s[r0:r0 + rows, :] = _nn(ub, wbi[0])
        _scan_loop(bre_s, bim_s, are_r[...], aim_r[...], xre_r, xim_r, False)
        for r0 in range(0, s, rows):
            y_r[r0:r0 + rows, :] = _nn(xre_r[r0:r0 + rows, :], wcr[0]) + _nn(xim_r[r0:r0 + rows, :], wci[0])

    nar = pl.BlockSpec((s, LANES), lambda k: (0, k))
    wide = pl.BlockSpec((s, lw), lambda k: (0, k))
    one = pl.BlockSpec((1, lw), lambda k: (0, k))
    wb = pl.BlockSpec((1, LANES, lw), lambda k: (k, 0, 0))
    wc = pl.BlockSpec((1, lw, LANES), lambda k: (k, 0, 0))
    return _call_with_ride(body, name, (S5_SUPER,), [u, wb_re, wb_im, wc_re, wc_im, a_re, a_im], [nar, wb, wb, wc, wc, one, one],
                           [_sds((s, S5_LANES)), _sds((s, S5_LANES)), _sds((s, S5_WIDTH))], [wide, wide, nar], ("parallel",), 56,
                           ride, scratch=[pltpu.VMEM((s, lw), F32), pltpu.VMEM((s, lw), F32)])


def _s5_core_bwd(dy0, u, du_a, x_re, x_im, wb_re, wb_im, wc_re, wc_im, a_re, a_im_neg, name, ride=None):
    s = u.shape[0]
    lw = S5_LANES // S5_SUPER
    rows = _pick(s, (S5_ROWS, 256, 128))

    def body(dy_r, u_r, dua_r, xre_r, xim_r, wbr, wbi, wcr, wci, are_r, aim_r, du_r, dare_r, daim_r, dbr_r, dbi_r, dcr_r, dci_r,
             dre_s, dim_s, gre_s, gim_s):
        for r0 in range(0, s, rows):
            dyb = dy_r[r0:r0 + rows, :]
            dre_s[r0:r0 + rows, :] = _nt(dyb, wcr[0])
            dim_s[r0:r0 + rows, :] = _nt(dyb, wci[0])
        dare_r[...], daim_r[...] = _scan_loop(dre_s, dim_s, are_r[...], aim_r[...], gre_s, gim_s, True, xre_r, xim_r)
        acc = [jnp.zeros((LANES, lw), F32) for _ in range(4)]
        for r0 in range(0, s, rows):
            sl = slice(r0, r0 + rows)
            gr, gi, ub, dyb = gre_s[sl, :], gim_s[sl, :], u_r[sl, :], dy_r[sl, :]
            du_r[sl, :] = dua_r[sl, :] + _nt(gr, wbr[0]) + _nt(gi, wbi[0])
            acc = [acc[0] + _tn(ub, gr), acc[1] + _tn(ub, gi), acc[2] + _tn(dyb, xre_r[sl, :]), acc[3] + _tn(dyb, xim_r[sl, :])]
        dbr_r[...], dbi_r[...], dcr_r[...], dci_r[...] = (_group_compact(a) for a in acc)

    nar = pl.BlockSpec((s, LANES), lambda k: (0, k))
    wide = pl.BlockSpec((s, lw), lambda k: (0, k))
    one = pl.BlockSpec((1, lw), lambda k: (0, k))
    wb = pl.BlockSpec((1, LANES, lw), lambda k: (k, 0, 0))
    wc = pl.BlockSpec((1, lw, LANES), lambda k: (k, 0, 0))
    blk = pl.BlockSpec((LANES, S5_STATE), lambda k: (k, 0))
    return _call_with_ride(
        body, name, (S5_SUPER,), [dy0, u, du_a, x_re, x_im, wb_re, wb_im, wc_re, wc_im, a_re, a_im_neg],
        [nar, nar, nar, wide, wide, wb, wb, wc, wc, one, one],
        [_sds((s, S5_WIDTH)), _sds((1, S5_LANES)), _sds((1, S5_LANES))] + [_sds((S5_WIDTH, S5_STATE))] * 4,
        [nar, one, one, blk, blk, blk, blk], ("parallel",), 60, ride, scratch=[pltpu.VMEM((s, lw), F32)] * 4)


def _s5_seg1(y0, u, d):
    return jax.nn.gelu(y0 + d * u)


def _s5_seg2(z, t, b):
    return z * jax.nn.sigmoid(t + b)


def _s5_post_fwd(y0, seg2, d, wglu, bglu, name):
    s = y0.shape[0]
    ts = _pick(s, (256, 128))

    def fn(ins, outs, accs, scr):
        z = _s5_seg1(ins[0][...], ins[1][...], ins[2][...])
        outs[0][...] = _s5_seg2(z, _nn(z, ins[3][...]), ins[4][...])

    return _stage(name, fn, s // ts,
                  [(y0, _rows(ts, 512)), (seg2, _rows(ts, 512)), (d, _full(d.shape)), (wglu, _full(wglu.shape)),
                   (bglu, _full(bglu.shape))], [(_sds((s, 512)), _rows(ts, 512))])[0]


def _s5_post_bwd(y0, seg2, d, wglu, bglu, dy, name):
    s = y0.shape[0]
    ts = _pick(s, (256, 128))

    def fn(ins, outs, accs, scr):
        y0_r, u_r, d_r, w_r, b_r, dy_r = ins
        z, vjp1 = jax.vjp(_s5_seg1, y0_r[...], u_r[...], d_r[...])
        t = _nn(z, w_r[...])
        _, vjp2 = jax.vjp(_s5_seg2, z, t, b_r[...])
        dz, dt, db = vjp2(dy_r[...])
        accs[0][...] += _tn(z, dt)
        accs[1][...] += db
        dy0, du, dd = vjp1(dz + _nt(dt, w_r[...]))
        accs[2][...] += dd
        outs[0][...] = dy0
        outs[1][...] = du

    return _stage(name, fn, s // ts,
                  [(y0, _rows(ts, 512)), (seg2, _rows(ts, 512)), (d, _full(d.shape)), (wglu, _full(wglu.shape)),
                   (bglu, _full(bglu.shape)), (dy, _rows(ts, 512))],
                  [(_sds((s, 512)), _rows(ts, 512)), (_sds((s, 512)), _rows(ts, 512))],
                  accs=[_sds((512, 512)), _sds((1, 512)), _sds((1, 512))])


def _gate_a(y, g):
    return y * jax.nn.silu(g)


def _gate_m(o0, o1, o2, m0, m1, m2):
    return jax.nn.sigmoid(m0) * o0 + jax.nn.sigmoid(m1) * o1 + jax.nn.sigmoid(m2) * o2


def _assemble(ybuf, o_mla, o_fox, y_s5):
    for h in range(HEADS):
        ybuf[:, 64 * h:64 * (h + 1)] = o_mla[h]
        ybuf[:, 512 + 64 * h:512 + 64 * (h + 1)] = o_fox[h]
    ybuf[:, 1024:1536] = y_s5[...]


def _gate_fwd(o_mla, o_fox, y_s5, seg3, x, wo, wout, name):
    s = x.shape[0]
    ts = _pick(s, (256, 128))

    def fn(ins, outs, accs, scr):
        om, of, ys, gm_r, x_r, wo_r, wout_r = ins
        ybuf = scr[0]
        _assemble(ybuf, om, of, ys)
        a = _gate_a(ybuf[...], gm_r[:, :GATES_W])
        o = [_nn(a[:, 512 * b:512 * (b + 1)], wo_r[512 * b:512 * (b + 1), :]) for b in range(3)]
        merged = _gate_m(o[0], o[1], o[2], *[gm_r[:, GATES_W + 1024 * b:GATES_W + 1024 * (b + 1)] for b in range(3)])
        outs[0][...] = x_r[...] + _nn(merged, wout_r[...])

    return _stage(name, fn, s // ts,
                  [(o_mla, _heads(ts, 64)), (o_fox, _heads(ts, 64)), (y_s5, _rows(ts, 512)), (seg3, _rows(ts, SEG_W[3])),
                   (x, _rows(ts, D_MODEL)), (wo, _full(wo.shape)), (wout, _full(wout.shape))],
                  [(_sds((s, D_MODEL)), _rows(ts, D_MODEL))], scratch=[pltpu.VMEM((ts, 1536), F32)])[0]


def _gate_bwd(o_mla, o_fox, y_s5, seg3, wo, wout, dout, name):
    s = dout.shape[0]
    ts = _pick(s, (128,))

    def fn(ins, outs, accs, scr):
        om, of, ys, gm_r, wo_r, wout_r, dout_r = ins
        do_mla, do_fox, dys, dgm_r = outs
        dwo, dwout = accs
        ybuf, dabuf = scr
        _assemble(ybuf, om, of, ys)
        a, vjp_a = jax.vjp(_gate_a, ybuf[...], gm_r[:, :GATES_W])
        o = [_nn(a[:, 512 * b:512 * (b + 1)], wo_r[512 * b:512 * (b + 1), :]) for b in range(3)]
        ms = [gm_r[:, GATES_W + 1024 * b:GATES_W + 1024 * (b + 1)] for b in range(3)]
        merged, vjp_m = jax.vjp(_gate_m, *o, *ms)
        dout_v = dout_r[...]
        dwout[...] += _tn(merged, dout_v)
        cts = vjp_m(_nt(dout_v, wout_r[...]))
        for b in range(3):
            dgm_r[:, GATES_W + 1024 * b:GATES_W + 1024 * (b + 1)] = cts[3 + b].astype(BF16)
            dwo[512 * b:512 * (b + 1), :] += _tn(a[:, 512 * b:512 * (b + 1)], cts[b])
            dabuf[:, 512 * b:512 * (b + 1)] = _nt(cts[b], wo_r[512 * b:512 * (b + 1), :])
        dy, dg = vjp_a(dabuf[...])
        dgm_r[:, :GATES_W] = dg.astype(BF16)
        dys[...] = dy[:, 1024:1536]
        for h in range(HEADS):
            do_mla[h] = dy[:, 64 * h:64 * (h + 1)]
            do_fox[h] = dy[:, 512 + 64 * h:512 + 64 * (h + 1)]

    return _stage(name, fn, s // ts,
                  [(o_mla, _heads(ts, 64)), (o_fox, _heads(ts, 64)), (y_s5, _rows(ts, 512)), (seg3, _rows(ts, SEG_W[3])),
                   (wo, _full(wo.shape)), (wout, _full(wout.shape)), (dout, _rows(ts, D_MODEL))],
                  [(_sds((HEADS, s, 64)), _heads(ts, 64)), (_sds((HEADS, s, 64)), _heads(ts, 64)), (_sds((s, 512)), _rows(ts, 512)),
                   (_sds((s, SEG_W[3]), BF16), _rows(ts, SEG_W[3]))],
                  accs=[_sds(wo.shape), _sds(wout.shape)], scratch=[pltpu.VMEM((ts, 1536), F32), pltpu.VMEM((ts, 1536), F32)],
                  vmem_mb=56)


def _loss_head(y, target, name):
    s = y.shape[0]
    ts = _pick(s, (256, 128))

    def fn(ins, outs, accs, scr):
        e = ins[0][...] - ins[1][...]
        outs[0][...] = e / D_MODEL
        accs[0][...] += 0.5 * jnp.sum(jnp.sum(e * e, axis=1, keepdims=True) / D_MODEL, axis=0, keepdims=True)

    r = _stage(name, fn, s // ts, [(y, _rows(ts, D_MODEL)), (target, _rows(ts, D_MODEL))],
               [(_sds((s, D_MODEL)), _rows(ts, D_MODEL))], accs=[_sds((1, 1))])
    return r[1], r[0]


IN_RANGES = ((0, 384, 0), (384, 416, 448), (416, 1952, 512), (1952, 1960, 384), (1960, IN_WIDTH, 2048))
SHARD_W = IN_WIDTH // N_DEV


def _win_pieces(d):
    lo, hi = SHARD_W * d, SHARD_W * (d + 1)
    out = []
    for a, b, p in IN_RANGES:
        s, e = max(a, lo), min(b, hi)
        while s < e:
            pad = p + (s - a)
            k = max(i for i in range(NSEG) if SEG_OFF[i] <= pad)
            w = min(e - s, SEG_OFF[k] + SEG_W[k] - pad)
            out.append((s - lo, w, k, pad - SEG_OFF[k]))
            s += w
    return out


def _win_pad(g, name):
    _, r, _ = g.shape
    tr = 64

    def body(g_ref, o_ref):
        o_ref[...] = jnp.zeros(o_ref.shape, o_ref.dtype)
        for d in range(N_DEV):
            for dst, w, k, src in _win_pieces(d):
                o_ref[:, SEG_OFF[k] + src:SEG_OFF[k] + src + w] = g_ref[d, :, dst:dst + w]

    return pl.pallas_call(
        body, name=name, grid=(r // tr,), in_specs=[pl.BlockSpec((N_DEV, tr, SHARD_W), lambda i: (0, i, 0))],
        out_specs=pl.BlockSpec((tr, PAD_IN), lambda i: (i, 0)), out_shape=jax.ShapeDtypeStruct((r, PAD_IN), g.dtype),
        compiler_params=pltpu.CompilerParams(dimension_semantics=("parallel",)),
    )(g)


def _win_unpad(dsegs, name):
    r = dsegs[0].shape[0]
    tr = 64

    def body(*refs):
        o_ref = refs[NSEG]
        for d in range(N_DEV):
            for dst, w, k, src in _win_pieces(d):
                o_ref[d, :, dst:dst + w] = refs[k][:, src:src + w]

    return pl.pallas_call(
        body, name=name, grid=(r // tr,), in_specs=[pl.BlockSpec((tr, SEG_W[k]), lambda i: (i, 0)) for k in range(NSEG)],
        out_specs=pl.BlockSpec((N_DEV, tr, SHARD_W), lambda i: (0, i, 0)),
        out_shape=jax.ShapeDtypeStruct((N_DEV, r, SHARD_W), dsegs[0].dtype),
        compiler_params=pltpu.CompilerParams(dimension_semantics=("parallel",)),
    )(*dsegs)


def _pad_wq(w):
    w = w.reshape(MLA_Q_RANK, HEADS, MLA_QK)
    return jnp.pad(w, ((0, 0), (0, 0), (0, LANES - MLA_QK))).reshape(MLA_Q_RANK, HEADS * LANES)


def _unpad_wq(d):
    return d.reshape(MLA_Q_RANK, HEADS, LANES)[:, :, :MLA_QK].reshape(MLA_Q_RANK, HEADS * MLA_QK)


def _pad_wkv(w):
    w = w.reshape(MLA_KV_RANK, HEADS, MLA_NOPE + MLA_V)
    k = jnp.pad(w[:, :, :MLA_NOPE], ((0, 0), (0, 0), (0, LANES - MLA_NOPE))).reshape(MLA_KV_RANK, HEADS * LANES)
    return jnp.concatenate([k, w[:, :, MLA_NOPE:].reshape(MLA_KV_RANK, HEADS * MLA_V)], axis=1)


def _unpad_wkv(d):
    k = d[:, :HEADS * LANES].reshape(MLA_KV_RANK, HEADS, LANES)[:, :, :MLA_NOPE]
    v = d[:, HEADS * LANES:].reshape(MLA_KV_RANK, HEADS, MLA_V)
    return jnp.concatenate([k, v], axis=2).reshape(MLA_KV_RANK, HEADS * (MLA_NOPE + MLA_V))


def _pad_lanes(v, n=LANES):
    return jnp.pad(v, (0, n - v.shape[0])).reshape(1, n)


def _super_blocks(b):
    _, r, c = b.shape
    per = S5_GROUPS // S5_SUPER
    b = b.reshape(S5_SUPER, per, r, c)
    eye = jnp.eye(per, dtype=b.dtype)
    return (b[:, :, :, None, :] * eye[None, :, None, :, None]).reshape(S5_SUPER, per * r, per * c)


def _layer_params(l, w, small):
    p = {k: small[k][l] for k in small}
    q = {}
    q["norm_g"] = p["norm_g"].reshape(1, D_MODEL)
    q["mla_q_a_norm"] = p["mla_q_a_norm"].reshape(1, 256)
    q["mla_kv_a_norm"] = p["mla_kv_a_norm"].reshape(1, 128)
    q["mla_q_norm"] = _pad_lanes(p["mla_q_norm"])
    q["mla_k_norm"] = _pad_lanes(p["mla_k_norm"])
    q["fox_b_f"] = _pad_lanes(p["fox_b_f"])
    q["fox_q_norm"] = p["fox_q_norm"].reshape(1, 64)
    q["fox_k_norm"] = p["fox_k_norm"].reshape(1, 64)
    q["s5_d"] = p["s5_d"].reshape(1, 512)
    q["s5_b_glu"] = p["s5_b_glu"].reshape(1, 512)
    rep = lambda z: jnp.repeat(z, S5_GROUP, axis=0)
    q["lr16"], q["li16"] = rep(p["s5_lambda_re"]), rep(p["s5_lambda_im"])
    q["ldt16"] = rep(jnp.broadcast_to(p["s5_log_dt"][:, None], (S5_GROUPS, S5_STATE)))
    q["br2"] = p["s5_b_re"].transpose(0, 2, 1).reshape(512, 64)
    q["bi2"] = p["s5_b_im"].transpose(0, 2, 1).reshape(512, 64)
    q["c_re"], q["c_im"] = p["s5_c_re"], p["s5_c_im"]
    q["w_in"] = _win_pad(w["w_in"], f"l{l}_w_in_pad")
    q["wq"] = _pad_wq(w["mla_w_q_up"])
    q["wkv"] = _pad_wkv(w["mla_w_kv_up"])
    q.update(_late_weights(w))
    return q


def _late_weights(w):
    return {q: w[k] for q, k in (("wglu", "s5_w_glu"), ("wo", "w_branch_out"), ("wout", "w_out")) if k in w}


def _layer_fwd(l, x, tabs, q, rides=None):
    rides = rides or {}
    n = lambda s: f"l{l}_{s}"
    sv = {"x": x}
    h = _norm_fwd(x, q["norm_g"], n("norm_fwd"))
    sv["h"] = h
    seg = [_mm(h, q["w_in"], "nn", n(f"proj{k}"), b_cols=(SEG_OFF[k], SEG_W[k])) for k in range(NSEG)]
    sv["seg"] = seg
    mq, mk, mv = _mla_prep_fwd(seg[0], tabs, q, q["wq"], q["wkv"], n("mla_prep_fwd"))
    o_mla, lse_mla = _attn_fwd(mq, mk, mv, 1.0 / math.sqrt(MLA_QK), True, n("mla_attn_fwd"), ride=rides.get("mla_attn_fwd"))
    sv.update(mq=mq, mk=mk, mv=mv, o_mla=o_mla, lse_mla=lse_mla)
    fq, fk, fv, c, ct = _fox_prep_fwd(seg[0], seg[1], q["fox_b_f"], q["fox_q_norm"], q["fox_k_norm"], n("fox_prep_fwd"))
    o_fox, lse_fox = _attn_fwd(fq, fk, fv, 1.0 / math.sqrt(FOX_DIM), False, n("fox_attn_fwd"), c=c, ct=ct,
                               ride=rides.get("fox_attn_fwd"))
    sv.update(fq=fq, fk=fk, fv=fv, c=c, ct=ct, o_fox=o_fox, lse_fox=lse_fox)
    a_re16, a_im16, bb_re, bb_im = _s5_params_fwd(q["lr16"], q["li16"], q["ldt16"], q["br2"], q["bi2"], n("s5_params_fwd"))
    a_re = a_re16.reshape(S5_GROUPS, S5_GROUP, S5_STATE)[:, 0, :].reshape(1, S5_LANES)
    a_im = a_im16.reshape(S5_GROUPS, S5_GROUP, S5_STATE)[:, 0, :].reshape(1, S5_LANES)
    wb_re = _super_blocks(bb_re.reshape(S5_GROUPS, S5_GROUP, S5_STATE)).astype(BF16)
    wb_im = _super_blocks(bb_im.reshape(S5_GROUPS, S5_GROUP, S5_STATE)).astype(BF16)
    wc_re = _super_blocks(q["c_re"].transpose(0, 2, 1)).astype(BF16)
    wc_im = _super_blocks(-q["c_im"].transpose(0, 2, 1)).astype(BF16)
    x_re, x_im, y0 = _s5_core_fwd(seg[2], wb_re, wb_im, wc_re, wc_im, a_re, a_im, n("s5_scan_fwd"), ride=rides.get("s5_scan_fwd"))
    y_s5 = _s5_post_fwd(y0, seg[2], q["s5_d"], q["wglu"], q["s5_b_glu"], n("s5_post_fwd"))
    sv.update(a_re=a_re, a_im=a_im, wb_re=wb_re, wb_im=wb_im, wc_re=wc_re, wc_im=wc_im, x_re=x_re, x_im=x_im, y0=y0, y_s5=y_s5)
    out = _gate_fwd(o_mla, o_fox, y_s5, seg[3], x, q["wo"], q["wout"], n("gate_fwd"))
    return out, sv


def _layer_bwd(l, dout, tabs, q, sv, rides=None, g=None):
    rides = rides or {}
    n = lambda s: f"l{l}_{s}"
    seg = sv["seg"]
    g = {} if g is None else g
    (do_mla, do_fox, dy_s5, dseg3, g["wo"], g["wout"]) = _gate_bwd(
        sv["o_mla"], sv["o_fox"], sv["y_s5"], seg[3], q["wo"], q["wout"], dout, n("gate_bwd"))
    dy0, du_a, g["wglu"], g["s5_b_glu"], g["s5_d"] = _s5_post_bwd(sv["y0"], seg[2], q["s5_d"], q["wglu"], q["s5_b_glu"], dy_s5,
                                                                 n("s5_post_bwd"))
    dseg2, da_re, da_im, dbb_re, dbb_im, dc_re, dc_im = _s5_core_bwd(
        dy0, seg[2], du_a, sv["x_re"], sv["x_im"], sv["wb_re"], sv["wb_im"], sv["wc_re"], sv["wc_im"], sv["a_re"], -sv["a_im"],
        n("s5_scan_bwd"), ride=rides.get("s5_scan_bwd"))
    g["s5_c_re"] = dc_re.reshape(S5_GROUPS, S5_GROUP, S5_STATE)
    g["s5_c_im"] = -dc_im.reshape(S5_GROUPS, S5_GROUP, S5_STATE)
    first = (jnp.arange(512) % S5_GROUP == 0).astype(F32)[:, None]
    da_re16 = jnp.repeat(da_re.reshape(S5_GROUPS, S5_STATE), S5_GROUP, axis=0) * first
    da_im16 = jnp.repeat(da_im.reshape(S5_GROUPS, S5_STATE), S5_GROUP, axis=0) * first
    dlr, dli, dldt, dbr2, dbi2 = _s5_params_bwd(q["lr16"], q["li16"], q["ldt16"], q["br2"], q["bi2"], da_re16, da_im16, dbb_re,
                                               dbb_im, n("s5_params_bwd"))
    g["s5_lambda_re"], g["s5_lambda_im"], g["s5_log_dt"] = dlr, dli, dldt.reshape(S5_GROUPS)
    g["s5_b_re"] = dbr2.reshape(S5_GROUPS, S5_GROUP, S5_STATE).transpose(0, 2, 1)
    g["s5_b_im"] = dbi2.reshape(S5_GROUPS, S5_GROUP, S5_STATE).transpose(0, 2, 1)
    dfq, dfk, dfv, dck = _attn_bwd_rows(sv["fq"], sv["fk"], sv["fv"], do_fox, sv["lse_fox"], 1.0 / math.sqrt(FOX_DIM), False,
                                        n("fox_attn_bwd"), sv["c"], sv["ct"], ride=rides.get("fox_attn_bwd"))
    dseg1, dff, g["fox_q_norm"], g["fox_k_norm"], dbf = _fox_prep_bwd(seg[0], seg[1], q["fox_b_f"], q["fox_q_norm"], q["fox_k_norm"],
                                                                      dfq, dfk, dfv, dck, n("fox_prep_bwd"))
    g["fox_b_f"] = dbf[0, :HEADS]
    dmq, dmk, dmv = _attn_bwd_rows(sv["mq"], sv["mk"], sv["mv"], do_mla, sv["lse_mla"], 1.0 / math.sqrt(MLA_QK), True,
                                   n("mla_attn_bwd"), ride=rides.get("mla_attn_bwd"))
    dseg0, dqan, dkvan, dqn, dkn, g["wq"], g["wkv"] = _mla_prep_bwd(seg[0], tabs, q, q["wq"], q["wkv"], dmq, dmk, dmv, dff,
                                                                   n("mla_prep_bwd"))
    g["mla_q_a_norm"], g["mla_kv_a_norm"] = dqan, dkvan
    g["mla_q_norm"], g["mla_k_norm"] = dqn[0, :MLA_QK], dkn[0, :MLA_QK]
    dsegs = [dseg0, dseg1, dseg2, dseg3]
    g["w_in"] = [_mm(sv["h"], dsegs[k], "tn", n(f"dwin{k}")) for k in range(NSEG)]
    dh = None
    for k in range(NSEG):
        dh = _mm(dsegs[k], q["w_in"], "nt", n(f"dh{k}"), acc=dh, b_cols=(SEG_OFF[k], SEG_W[k]), ride=rides.get(f"dh{k}"))
    dx, g["norm_g"] = _norm_bwd(sv["x"], q["norm_g"], dh, dout, n("norm_bwd"))
    return dx, g


MESH = pl.DeviceIdType.MESH
ANY = pl.BlockSpec(memory_space=pl.ANY)


def _all_gather(blocks, name):
    n = len(blocks)

    def body(*refs):
        x_refs, out_refs = refs[:n], refs[n:2 * n]
        send_sems, recv_sems, local_sems = refs[2 * n:]
        x, y, c = lax.axis_index("x"), lax.axis_index("y"), lax.axis_index("c")
        me, sibling = (x, y, c), (x, y, 1 - c)
        chips = [(1 - x, y), (x, 1 - y), (1 - x, 1 - y)]

        def slot(a, px, py, pc):
            return out_refs[a].at[4 * px + 2 * py + pc]

        def copy(a, k, blk, to, src=None):
            return pltpu.make_async_remote_copy(src_ref=slot(a, *blk) if src is None else src, dst_ref=slot(a, *blk),
                                                send_sem=send_sems.at[7 * a + k], recv_sem=recv_sems.at[7 * a + k],
                                                device_id=to, device_id_type=MESH)

        mine = [pltpu.make_async_copy(x_refs[a], slot(a, *me), local_sems.at[a]) for a in range(n)]
        for cp in mine:
            cp.start()
        first = []
        for j, chip in enumerate(chips):
            first += [copy(a, 1 + j, me, (*chip, c), src=x_refs[a]) for a in range(n)]
        first += [copy(a, 0, me, sibling, src=x_refs[a]) for a in range(n)]
        for cp in first:
            cp.start()
        passed = []
        for j, chip in enumerate(chips):
            for a in range(n):
                copy(a, 1 + j, (*chip, c), me).wait_recv()
                passed.append(copy(a, 4 + j, (*chip, c), sibling))
                passed[-1].start()
        for a in range(n):
            copy(a, 0, sibling, me).wait_recv()
        for j, chip in enumerate(chips):
            for a in range(n):
                copy(a, 4 + j, (*chip, 1 - c), me).wait_recv()
        for cp in first + passed:
            cp.wait_send()
        for cp in mine:
            cp.wait()

    return pl.pallas_call(
        body, name=name, out_shape=[jax.ShapeDtypeStruct((N_DEV,) + b.shape, b.dtype) for b in blocks],
        in_specs=[ANY] * n, out_specs=[ANY] * n,
        scratch_shapes=[pltpu.SemaphoreType.DMA((7 * n,)), pltpu.SemaphoreType.DMA((7 * n,)), pltpu.SemaphoreType.DMA((n,))],
    )(*blocks)


def _place():
    x, y, c = lax.axis_index("x"), lax.axis_index("y"), lax.axis_index("c")
    return x, y, c, [(1 - x, y), (x, 1 - y), (1 - x, 1 - y)]


def _remote(src, dst, send, recv, k, to):
    return pltpu.make_async_remote_copy(src_ref=src, dst_ref=dst, send_sem=send.at[k], recv_sem=recv.at[k], device_id=to,
                                        device_id_type=MESH)


def _plan_gather_ici(blocks):
    n = len(blocks)

    def copies(in_refs, out_refs, send, recv, local):
        x, y, c, chips = _place()
        mine = 4 * x + 2 * y + c
        loc = [pltpu.make_async_copy(in_refs[a], out_refs[a].at[mine], local.at[a]) for a in range(n)]
        rem = [_remote(in_refs[a], out_refs[a].at[mine], send, recv, 3 * a + j, (px, py, c))
               for j, (px, py) in enumerate(chips) for a in range(n)]
        return rem, loc

    return _Plan(blocks, [jax.ShapeDtypeStruct((N_DEV,) + b.shape, b.dtype) for b in blocks], 3 * n, n, copies)


def _plan_gather_d2d(gathered):
    n = len(gathered)

    def copies(in_refs, out_refs, send, recv, local):
        x, y, c, _ = _place()
        rem = [_remote(in_refs[a].at[2 * j + c], out_refs[a].at[2 * j + c], send, recv, 4 * a + j, (x, y, 1 - c))
               for a in range(n) for j in range(4)]
        return rem, []

    return _Plan(gathered, [jax.ShapeDtypeStruct(g.shape, g.dtype) for g in gathered], 4 * n, 0, copies,
                 aliases={a: a for a in range(n)})


def _plan_reduce_sibling(parts):
    n = len(parts)

    def copies(in_refs, out_refs, send, recv, local):
        x, y, c, _ = _place()
        rem = [_remote(in_refs[a].at[2 * j + (1 - c)], out_refs[a].at[j], send, recv, 4 * a + j, (x, y, 1 - c))
               for a in range(n) for j in range(4)]
        return rem, []

    return _Plan(parts, [jax.ShapeDtypeStruct((4,) + p.shape[1:], p.dtype) for p in parts], 4 * n, 0, copies)


def _plan_reduce_chips(sums):
    n = len(sums)

    def copies(in_refs, out_refs, send, recv, local):
        x, y, c, chips = _place()
        mine = 2 * x + y
        loc = [pltpu.make_async_copy(in_refs[a].at[mine], out_refs[a].at[mine], local.at[a]) for a in range(n)]
        rem = [_remote(in_refs[a].at[2 * px + py], out_refs[a].at[mine], send, recv, 3 * a + k, (px, py, c))
               for k, (px, py) in enumerate(chips) for a in range(n)]
        return rem, loc

    return _Plan(sums, [jax.ShapeDtypeStruct(p.shape, p.dtype) for p in sums], 3 * n, n, copies)


def _add_sibling(parts, got, name):
    _, r, cc = parts.shape
    tr = _pick(r, (512, 256, 128, 64, 32, 16))
    c = lax.axis_index("c")

    def body(c_ref, p_ref, g_ref, o_ref):
        o_ref[...] = (p_ref[...] + g_ref[...]).astype(BF16)

    return pl.pallas_call(
        body, name=name, out_shape=jax.ShapeDtypeStruct((4, r, cc), BF16),
        grid_spec=pltpu.PrefetchScalarGridSpec(
            num_scalar_prefetch=1, grid=(4, r // tr),
            in_specs=[pl.BlockSpec((1, tr, cc), lambda j, i, cr: (2 * j + cr[0], i, 0)),
                      pl.BlockSpec((1, tr, cc), lambda j, i, cr: (j, i, 0))],
            out_specs=pl.BlockSpec((1, tr, cc), lambda j, i, cr: (j, i, 0))),
        compiler_params=_vmem(48),
    )(c.reshape(1).astype(jnp.int32), parts, got)


def _sum_leading(parts, name):
    k, r, cc = parts.shape
    tr = _pick(r, (512, 256, 128, 64, 32, 16, 8))

    def body(p_ref, o_ref):
        acc = p_ref[0]
        for j in range(1, k):
            acc = acc + p_ref[j]
        o_ref[...] = acc

    return pl.pallas_call(
        body, name=name, out_shape=jax.ShapeDtypeStruct((r, cc), F32), grid=(r // tr,),
        in_specs=[pl.BlockSpec((k, tr, cc), lambda i: (0, i, 0))], out_specs=pl.BlockSpec((tr, cc), lambda i: (i, 0)),
    )(parts)


def _adamw_math(w, g, m, v):
    nm = ADAM_B1 * m + (1.0 - ADAM_B1) * g
    nv = ADAM_B2 * v + (1.0 - ADAM_B2) * jnp.square(g)
    m_hat = nm / (1.0 - ADAM_B1 ** ADAM_STEP)
    v_hat = nv / (1.0 - ADAM_B2 ** ADAM_STEP)
    return -ADAM_LR * (m_hat / (jnp.sqrt(v_hat) + ADAM_EPS) + ADAM_WD * w), nm, nv


def _adamw_sum(w, contribs, m, v, name, ride=None):
    nl = len(contribs)
    k, r, cc = contribs[0].shape
    tr = _pick(r, (256, 128, 64, 32, 16))
    nb = r // tr

    def body(w_ref, *rest):
        c_refs = rest[:nl]
        m_ref, v_ref, g_ref, d_ref, nm_ref, nv_ref = rest[nl:]
        for li in range(nl):
            @pl.when(pl.program_id(0) == li)
            def _(c_ref=c_refs[li]):
                g = c_ref[0].astype(F32)
                for j in range(1, k):
                    g = g + c_ref[j].astype(F32)
                g_ref[...] = g
                d_ref[...], nm_ref[...], nv_ref[...] = _adamw_math(w_ref[...], g, m_ref[...], v_ref[...])

    spec = pl.BlockSpec((tr, cc), lambda l, i: (l * nb + i, 0))
    cspec = pl.BlockSpec((k, tr, cc), lambda l, i: (0, i, 0))
    return _call_with_ride(body, name, (nl, nb), [w, *contribs, m, v], [spec] + [cspec] * nl + [spec, spec],
                           [jax.ShapeDtypeStruct(w.shape, F32)] * 4, [spec] * 4, ("parallel", "parallel"), 48, ride)


def _adamw_many(ws, gs, ms, vs, name):
    n = len(ws)

    def body(*refs):
        w_r, g_r, m_r, v_r = refs[:n], refs[n:2 * n], refs[2 * n:3 * n], refs[3 * n:4 * n]
        d_r, nm_r, nv_r = refs[4 * n:5 * n], refs[5 * n:6 * n], refs[6 * n:7 * n]
        for a in range(n):
            d_r[a][...], nm_r[a][...], nv_r[a][...] = _adamw_math(w_r[a][...], g_r[a][...], m_r[a][...], v_r[a][...])

    shapes = [jax.ShapeDtypeStruct(w.shape, F32) for w in ws]
    res = pl.pallas_call(body, name=name, out_shape=shapes * 3,
                         compiler_params=pltpu.CompilerParams(vmem_limit_bytes=56 * 1024 * 1024))(*ws, *gs, *ms, *vs)
    return res[:n], res[n:2 * n], res[2 * n:]


def _pack_rows(flat, lanes, row_mult):
    n = flat.shape[-1]
    rows = -(-n // lanes)
    rows = -(-rows // row_mult) * row_mult
    pad = rows * lanes - n
    if pad:
        flat = jnp.pad(flat, [(0, 0)] * (flat.ndim - 1) + [(0, pad)])
    return flat.reshape(flat.shape[:-1] + (rows, lanes))


def _rope_tables(positions):
    inv = 1.0 / (ROPE_THETA ** (jnp.arange(0, MLA_ROPE, 2, dtype=F32) / MLA_ROPE))
    ang = positions.astype(F32)[:, None] * inv
    cos, sin = jnp.cos(ang), jnp.sin(ang)
    s = positions.shape[0]
    z = lambda n: jnp.zeros((s, n), F32)
    c = jnp.concatenate([jnp.ones((s, 64), F32), cos, cos, z(32)], axis=1)
    sa = jnp.concatenate([z(64), -sin, z(48)], axis=1)
    sb = jnp.concatenate([z(80), sin, z(32)], axis=1)
    return c, sa, sb


def _full_weights(gathered, names=SHARDED):
    full = {}
    for k, g in zip(names, gathered):
        _, r, c = g.shape
        if k == "w_in":
            full[k] = g
        else:
            full[k] = g.transpose(1, 0, 2).reshape(r, N_DEV * c) if k in COL_SHARDED else g.reshape(N_DEV * r, c)
    return full


EARLY = ("s5_w_glu", "w_branch_out", "w_out")
LATE = ("w_in", "mla_w_q_up", "mla_w_kv_up")


def _owner_major(g, names, tag):
    parts = []
    for k in names:
        if k == "w_in":
            parts.append(_win_unpad(g["w_in"], f"{tag}_w_in_unpad"))
            continue
        big = {"mla_w_q_up": lambda: _unpad_wq(g["wq"]), "mla_w_kv_up": lambda: _unpad_wkv(g["wkv"]), "s5_w_glu": lambda: g["wglu"],
               "w_branch_out": lambda: g["wo"], "w_out": lambda: g["wout"]}[k]()
        r, c = big.shape
        if k in COL_SHARDED:
            parts.append(big.reshape(r, N_DEV, c // N_DEV).transpose(1, 0, 2))
        else:
            parts.append(big.reshape(N_DEV, r // N_DEV, c))
    return parts


def _device_step(x, positions, target, shards, small):
    tabs = _rope_tables(positions)
    box = {}
    q0 = _layer_params(0, _full_weights(_all_gather(shards[0][:3], "gather_weights_l0"), LATE), small)

    def arrived(o):
        box.update(w1_b=o[:5])
        q0.update(_late_weights(_full_weights(o[5:], EARLY)))

    rides = {
        "mla_attn_fwd": _Ride(lambda: _plan_gather_ici(shards[1][:1]), lambda o: box.update(ici_a=o)),
        "fox_attn_fwd": _Ride(lambda: _join([_plan_gather_ici(shards[1][1:]), _plan_gather_d2d(box["ici_a"]),
                                             _plan_gather_ici(shards[0][3:])]),
                              lambda o: box.update(ici_b=o[:5], w1_a=o[5:6], ici_0=o[6:])),
        "s5_scan_fwd": _Ride(lambda: _join([_plan_gather_d2d(box["ici_b"]), _plan_gather_d2d(box["ici_0"])]), arrived),
    }
    h, sv0 = _layer_fwd(0, x, tabs, q0, rides)
    q1 = _layer_params(1, _full_weights(list(box["w1_a"]) + list(box["w1_b"])), small)
    h, sv1 = _layer_fwd(1, h, tabs, q1)
    loss, d = _loss_head(h, target, "loss_head")
    d, g1 = _layer_bwd(1, d, tabs, q1, sv1)
    parts1 = _owner_major(g1, SHARDED, "l1")

    def chips_plan(names, parts, got, tag):
        return _plan_reduce_chips([_add_sibling(p, g, f"reduce_add_{tag}_{k}") for k, p, g in zip(names, parts, got)])

    g0 = {}
    rides = {
        "s5_scan_bwd": _Ride(lambda: _join([_plan_reduce_sibling(parts1),
                                            _plan_reduce_sibling(box.setdefault("early0", _owner_major(g0, EARLY, "l0")))]),
                             lambda o: box.update(got1=o[:6], got0e=o[6:])),
        "fox_attn_bwd": _Ride(lambda: chips_plan(SHARDED, parts1, box["got1"], "l1"), lambda o: box.update(contribs1=o)),
        "mla_attn_bwd": _Ride(lambda: chips_plan(EARLY, box["early0"], box["got0e"], "l0"), lambda o: box.update(contribs0e=o)),
        "dh1": _Ride(lambda: _plan_reduce_sibling(box.setdefault("late0", _owner_major(g0, LATE, "l0"))),
                     lambda o: box.update(got0l=o)),
        "dh3": _Ride(lambda: chips_plan(LATE, box["late0"], box["got0l"], "l0"), lambda o: box.update(contribs0l=o)),
    }
    d, _ = _layer_bwd(0, d, tabs, q0, sv0, rides, g0)
    contribs0 = list(box["contribs0l"]) + list(box["contribs0e"])
    return loss[0, 0], d, [g0, g1], [contribs0, box["contribs1"]]


def kernel(x, positions, norm_g, w_in, mla_q_a_norm, mla_w_q_up, mla_kv_a_norm, mla_w_kv_up, mla_q_norm, mla_k_norm, fox_b_f, fox_q_norm, fox_k_norm, s5_lambda_re, s5_lambda_im, s5_log_dt, s5_b_re, s5_b_im, s5_c_re, s5_c_im, s5_d, s5_w_glu, s5_b_glu, w_branch_out, w_out, loss_target, m_norm_g, m_w_in, m_mla_q_a_norm, m_mla_w_q_up, m_mla_kv_a_norm, m_mla_w_kv_up, m_mla_q_norm, m_mla_k_norm, m_fox_b_f, m_fox_q_norm, m_fox_k_norm, m_s5_lambda_re, m_s5_lambda_im, m_s5_log_dt, m_s5_b_re, m_s5_b_im, m_s5_c_re, m_s5_c_im, m_s5_d, m_s5_w_glu, m_s5_b_glu, m_w_branch_out, m_w_out, v_norm_g, v_w_in, v_mla_q_a_norm, v_mla_w_q_up, v_mla_kv_a_norm, v_mla_w_kv_up, v_mla_q_norm, v_mla_k_norm, v_fox_b_f, v_fox_q_norm, v_fox_k_norm, v_s5_lambda_re, v_s5_lambda_im, v_s5_log_dt, v_s5_b_re, v_s5_b_im, v_s5_c_re, v_s5_c_im, v_s5_d, v_s5_w_glu, v_s5_b_glu, v_w_branch_out, v_w_out):
    env = dict(locals())
    wts = {k: env[k] for k in WEIGHTS}
    mom = {k: env["m_" + k] for k in WEIGHTS}
    var = {k: env["v_" + k] for k in WEIGHTS}

    shards = [[wts[k][l].astype(BF16) for k in SHARDED] for l in range(DEPTH)]
    small = {k: wts[k] for k in SMALL}
    loss, dx, grads, contribs = _device_step(x[0], positions[0], loss_target[0], shards, small)
    loss = lax.psum(loss, ("x", "y", "c"))

    two_d = {k: (wts[k].shape[0] * wts[k].shape[1], wts[k].shape[2]) for k in SHARDED}
    sm = {k: jnp.stack([g[k] for g in grads]).reshape(wts[k].shape) for k in SMALL}
    small_block = _pack_rows(jnp.concatenate([sm[k].reshape(-1) for k in SMALL]), LANES, 256)
    box = {}
    rides = {SHARDED[0]: _Ride(lambda: _plan_gather_ici([small_block]), lambda o: box.update(ici=o)),
             SHARDED[1]: _Ride(lambda: _plan_gather_d2d(box["ici"]), lambda o: box.update(all=o))}
    grad_out, delta_out, m_out, v_out = {}, {}, {}, {}
    for i, k in enumerate(SHARDED):
        shp = wts[k].shape
        res = _adamw_sum(wts[k].reshape(two_d[k]), [contribs[l][i] for l in range(DEPTH)], mom[k].reshape(two_d[k]),
                         var[k].reshape(two_d[k]), f"adamw_{k}", ride=rides.get(k))
        grad_out[k], delta_out[k], m_out[k], v_out[k] = (z.reshape(shp) for z in res)
    g_small = _sum_leading(box["all"][0], "sum_small_grads").reshape(-1)
    off = 0
    for k in SMALL:
        cnt = int(np.prod(wts[k].shape))
        grad_out[k] = g_small[off:off + cnt].reshape(wts[k].shape)
        off += cnt
    flat2 = lambda a: a.reshape(-1, a.shape[-1])
    d_s, m_s, v_s = _adamw_many([flat2(wts[k]) for k in SMALL], [flat2(grad_out[k]) for k in SMALL],
                                [flat2(mom[k]) for k in SMALL], [flat2(var[k]) for k in SMALL], "adamw_small")
    for i, k in enumerate(SMALL):
        delta_out[k], m_out[k], v_out[k] = (z[i].reshape(wts[k].shape) for z in (d_s, m_s, v_s))

    return (loss, dx[None], *[grad_out[k] for k in WEIGHTS], *[delta_out[k] for k in WEIGHTS],
            *[m_out[k] for k in WEIGHTS], *[v_out[k] for k in WEIGHTS])
```

```python
import functools
import math

import jax
import jax.numpy as jnp
import numpy as np
from jax import lax
from jax.experimental import pallas as pl
from jax.experimental.pallas import tpu as pltpu

F32 = jnp.float32
BF16 = jnp.bfloat16

D_MODEL = 1024
DEPTH = 2
CHUNK = 64
EPS = 1e-6
HEADS = 8
MLA_NOPE, MLA_ROPE, MLA_V = 64, 32, 64
MLA_Q_RANK, MLA_KV_RANK = 256, 128
MLA_QK = MLA_NOPE + MLA_ROPE
ROPE_THETA = 10000.0
FOX_DIM = 64
S5_WIDTH, S5_GROUP, S5_GROUPS, S5_STATE = 512, 16, 32, 64
S5_LANES = S5_GROUPS * S5_STATE
IN_WIDTH = 7080
N_DEV = 8
LANES = 128
SUBLANES = 8

ADAM_LR, ADAM_B1, ADAM_B2, ADAM_EPS, ADAM_WD, ADAM_STEP = 0.001, 0.9, 0.999, 1e-08, 0.01, 10

SEG_W = (512, 1536, 512, 4608)
SEG_OFF = (0, 512, 2048, 2560)
NSEG = len(SEG_W)
GATES_W = 1536
PAD_IN = 7168
NEG = -1e30

SHARDED = ("w_in", "mla_w_q_up", "mla_w_kv_up", "s5_w_glu", "w_branch_out", "w_out")
COL_SHARDED = ("w_in", "mla_w_q_up", "mla_w_kv_up")
SMALL = ("norm_g", "mla_q_a_norm", "mla_kv_a_norm", "mla_q_norm", "mla_k_norm", "fox_b_f", "fox_q_norm", "fox_k_norm",
         "s5_lambda_re", "s5_lambda_im", "s5_log_dt", "s5_b_re", "s5_b_im", "s5_c_re", "s5_c_im", "s5_d", "s5_b_glu")
WEIGHTS = ("norm_g", "w_in", "mla_q_a_norm", "mla_w_q_up", "mla_kv_a_norm", "mla_w_kv_up", "mla_q_norm", "mla_k_norm",
           "fox_b_f", "fox_q_norm", "fox_k_norm", "s5_lambda_re", "s5_lambda_im", "s5_log_dt", "s5_b_re", "s5_b_im",
           "s5_c_re", "s5_c_im", "s5_d", "s5_w_glu", "s5_b_glu", "w_branch_out", "w_out")


def _pick(n, cands):
    for c in cands:
        if n % c == 0:
            return c
    return n


def _vmem(mb):
    return pltpu.CompilerParams(vmem_limit_bytes=mb * 1024 * 1024)


def _dot(a, b, dims):
    return lax.dot_general(a.astype(BF16), b.astype(BF16), (dims, ((), ())), preferred_element_type=F32)


def _nn(a, b):
    return _dot(a, b, ((1,), (0,)))


def _nt(a, b):
    return _dot(a, b, ((1,), (1,)))


def _tn(a, b):
    return _dot(a, b, ((0,), (0,)))


def _rms(x, g, n=None):
    n = x.shape[-1] if n is None else n
    return x * lax.rsqrt(jnp.sum(x * x, axis=-1, keepdims=True) / n + EPS) * g


def _rope(t, c, sa, sb):
    return t * c + pltpu.roll(t, LANES - 16, 1) * sa + pltpu.roll(t, 16, 1) * sb


def _rope_t(d, c, sa, sb):
    return d * c + pltpu.roll(d * sa, 16, 1) + pltpu.roll(d * sb, LANES - 16, 1)


def _mm(a, b, mode, name, acc=None, b_cols=None, ride=None):
    if mode == "tn":
        kd, m = a.shape
    else:
        m, kd = a.shape
    b_off, b_w = b_cols if b_cols is not None else (0, b.shape[1])
    n = b.shape[0] if mode == "nt" else b_w
    tm, tn, tk = _pick(m, (1024, 512, 256, 128)), _pick(n, (1024, 512, 256, 128)), _pick(kd, (1024, 512, 256, 128))
    nk = kd // tk
    if mode == "tn":
        a_spec = pl.BlockSpec((tk, tm), lambda i, j, k: (k, i))
    else:
        a_spec = pl.BlockSpec((tm, tk), lambda i, j, k: (i, k))
    if mode == "nt":
        assert b_off % tk == 0
        b_spec = pl.BlockSpec((tn, tk), lambda i, j, k: (j, k + b_off // tk))
    else:
        assert b_off % tn == 0
        b_spec = pl.BlockSpec((tk, tn), lambda i, j, k: (k, j + b_off // tn))
    dims = {"nn": ((1,), (0,)), "nt": ((1,), (1,)), "tn": ((0,), (0,))}[mode]
    o_spec = pl.BlockSpec((tm, tn), lambda i, j, k: (i, j))
    has_acc = acc is not None

    def body(*refs):
        if has_acc:
            a_ref, b_ref, c_ref, o_ref = refs
        else:
            a_ref, b_ref, o_ref = refs
        k = pl.program_id(2)
        prod = _dot(a_ref[...], b_ref[...], dims)

        @pl.when(k == 0)
        def _():
            o_ref[...] = prod + c_ref[...] if has_acc else prod

        @pl.when(k > 0)
        def _():
            o_ref[...] += prod

    ins = [a, b] + ([acc] if has_acc else [])
    in_specs = [a_spec, b_spec] + ([o_spec] if has_acc else [])
    return _call_with_ride(body, name, (m // tm, n // tn, nk), ins, in_specs, [jax.ShapeDtypeStruct((m, n), F32)], [o_spec],
                           ("parallel", "parallel", "arbitrary"), 48, ride)[0]


def _stage(name, fn, n_steps, ins, outs, accs=(), scratch=(), vmem_mb=48, acc_in_scratch=False):
    n_in, n_out, n_acc, n_scr = len(ins), len(outs), len(accs), len(scratch)

    def body(*refs):
        in_refs = refs[:n_in]
        out_refs = refs[n_in:n_in + n_out]
        acc_out = refs[n_in + n_out:n_in + n_out + n_acc]
        scr = refs[n_in + n_out + n_acc:n_in + n_out + n_acc + n_scr]
        acc_refs = refs[n_in + n_out + n_acc + n_scr:] if acc_in_scratch else acc_out
        if n_acc:
            @pl.when(pl.program_id(0) == 0)
            def _():
                for r in acc_refs:
                    r[...] = jnp.zeros(r.shape, r.dtype)
        fn(in_refs, out_refs, acc_refs, scr)
        if acc_in_scratch:
            @pl.when(pl.program_id(0) == n_steps - 1)
            def _():
                for src, dst in zip(acc_refs, acc_out):
                    pltpu.sync_copy(src, dst)

    if acc_in_scratch:
        acc_specs = [pl.BlockSpec(memory_space=pl.ANY) for _ in accs]
        scratch = list(scratch) + [pltpu.VMEM(a.shape, a.dtype) for a in accs]
    else:
        acc_specs = [pl.BlockSpec(a.shape, functools.partial(lambda i, nd: (0,) * nd, nd=len(a.shape))) for a in accs]
    res = pl.pallas_call(
        body, name=name, grid=(n_steps,),
        in_specs=[s for _, s in ins], out_specs=[s for _, s in outs] + acc_specs,
        out_shape=[s for s, _ in outs] + list(accs), scratch_shapes=list(scratch),
        compiler_params=pltpu.CompilerParams(dimension_semantics=("arbitrary",),
                                             vmem_limit_bytes=vmem_mb * 1024 * 1024),
    )(*[a for a, _ in ins])
    return res


def _rows(ts, w, j=0):
    return pl.BlockSpec((ts, w), lambda i: (i, j))


def _rows_rev(ts, w, n, j=0):
    return pl.BlockSpec((ts, w), lambda i: (n - 1 - i, j))


def _heads(ts, d):
    return pl.BlockSpec((HEADS, ts, d), lambda i: (0, i, 0))


def _heads_rev(ts, d, n):
    return pl.BlockSpec((HEADS, ts, d), lambda i: (0, n - 1 - i, 0))


def _full(shape):
    nd = len(shape)
    return pl.BlockSpec(tuple(shape), lambda i: (0,) * nd)


def _sds(shape, dtype=F32):
    return jax.ShapeDtypeStruct(tuple(shape), dtype)


def _norm_fwd(x, g, name):
    s = x.shape[0]
    ts = _pick(s, (256, 128))

    def fn(ins, outs, accs, scr):
        outs[0][...] = _rms(ins[0][...], ins[1][...]).astype(BF16)

    return _stage(name, fn, s // ts, [(x, _rows(ts, D_MODEL)), (g, _full(g.shape))],
                  [(_sds((s, D_MODEL), BF16), _rows(ts, D_MODEL))])[0]


def _norm_bwd(x, g, dh, dres, name):
    s = x.shape[0]
    ts = _pick(s, (256, 128))

    def fn(ins, outs, accs, scr):
        _, vjp = jax.vjp(_rms, ins[0][...], ins[1][...])
        dx, dg = vjp(ins[2][...])
        outs[0][...] = dx + ins[3][...]
        accs[0][...] += dg

    r = _stage(name, fn, s // ts,
               [(x, _rows(ts, D_MODEL)), (g, _full(g.shape)), (dh, _rows(ts, D_MODEL)), (dres, _rows(ts, D_MODEL))],
               [(_sds((s, D_MODEL)), _rows(ts, D_MODEL))], accs=[_sds((1, D_MODEL))])
    return r[0], r[1]


def _mla_q(qraw, c, sa, sb, qn):
    return _rms(_rope(qraw, c, sa, sb), qn, MLA_QK)


def _mla_prep_fwd(seg0, tabs, p, wq, wkv, name):
    s = seg0.shape[0]
    ts = _pick(s, (256, 128))

    def fn(ins, outs, accs, scr):
        blk, cos, sa, sb, qan, kvan, qn, kn, wq_r, wkv_r = ins
        b = blk[...]
        cq, ckv, kt = b[:, :256], b[:, 256:384], b[:, 384:512]
        lane = lax.broadcasted_iota(jnp.int32, kt.shape, 1)
        kpe = jnp.where(lane >= 64, kt, 0.0)
        q_raw = _nn(_rms(cq, qan[...]), wq_r[...])
        kv_raw = _nn(_rms(ckv, kvan[...]), wkv_r[...])
        c, a, bb = cos[...], sa[...], sb[...]
        for h in range(HEADS):
            outs[0][h] = _mla_q(q_raw[:, LANES * h:LANES * (h + 1)], c, a, bb, qn[...]).astype(BF16)
            outs[1][h] = _mla_q(kv_raw[:, LANES * h:LANES * (h + 1)] + kpe, c, a, bb, kn[...]).astype(BF16)
            outs[2][h] = kv_raw[:, 1024 + 64 * h:1024 + 64 * (h + 1)].astype(BF16)

    consts = [p["mla_q_a_norm"], p["mla_kv_a_norm"], p["mla_q_norm"], p["mla_k_norm"], wq, wkv]
    return _stage(name, fn, s // ts,
                  [(seg0, _rows(ts, 512))] + [(t, _rows(ts, LANES)) for t in tabs] + [(a, _full(a.shape)) for a in consts],
                  [(_sds((HEADS, s, LANES), BF16), _heads(ts, LANES)), (_sds((HEADS, s, LANES), BF16), _heads(ts, LANES)),
                   (_sds((HEADS, s, 64), BF16), _heads(ts, 64))])


def _mla_prep_bwd(seg0, tabs, p, wq, wkv, dq, dk, dv, dff, name):
    s = seg0.shape[0]
    ts = _pick(s, (256, 128))

    def fn(ins, outs, accs, scr):
        blk, cos, sa, sb, qan, kvan, qn, kn, wq_r, wkv_r, dq_r, dk_r, dv_r, dff_r = ins
        dqan, dkvan, dqn, dkn, dwq, dwkv = accs
        dqraw_s, dkvraw_s = scr
        b = blk[...]
        cq, ckv, kt = b[:, :256], b[:, 256:384], b[:, 384:512]
        lane = lax.broadcasted_iota(jnp.int32, kt.shape, 1)
        kpe = jnp.where(lane >= 64, kt, 0.0)
        cqn, vjp_cq = jax.vjp(_rms, cq, qan[...])
        ckvn, vjp_ckv = jax.vjp(_rms, ckv, kvan[...])
        q_raw = _nn(cqn, wq_r[...])
        kv_raw = _nn(ckvn, wkv_r[...])
        c, a, bb = cos[...], sa[...], sb[...]

        def head_bwd(raw, gain, d):
            t = _rope(raw, c, a, bb)
            _, vjp = jax.vjp(functools.partial(_rms, n=MLA_QK), t, gain)
            dt, dgain = vjp(d)
            return _rope_t(dt, c, a, bb), dgain

        dkpe = jnp.zeros(kt.shape, F32)
        for h in range(HEADS):
            dqh, dg = head_bwd(q_raw[:, LANES * h:LANES * (h + 1)], qn[...], dq_r[h])
            dqn[...] += dg
            dqraw_s[:, LANES * h:LANES * (h + 1)] = dqh
            dkh, dg = head_bwd(kv_raw[:, LANES * h:LANES * (h + 1)] + kpe, kn[...], dk_r[h])
            dkn[...] += dg
            dkvraw_s[:, LANES * h:LANES * (h + 1)] = dkh
            dkpe = dkpe + dkh
            dkvraw_s[:, 1024 + 64 * h:1024 + 64 * (h + 1)] = dv_r[h]
        dq_raw = dqraw_s[...]
        dkv_raw = dkvraw_s[...]
        dwq[...] += _tn(cqn, dq_raw)
        dwkv[...] += _tn(ckvn, dkv_raw)
        dcq, dg = vjp_cq(_nt(dq_raw, wq_r[...]))
        dqan[...] += dg
        dckv, dg = vjp_ckv(_nt(dkv_raw, wkv_r[...]))
        dkvan[...] += dg
        outs[0][:, 0:256] = dcq.astype(BF16)
        outs[0][:, 256:384] = dckv.astype(BF16)
        outs[0][:, 384:512] = (jnp.where(lane >= 64, dkpe, 0.0) + dff_r[...]).astype(BF16)

    consts = [p["mla_q_a_norm"], p["mla_kv_a_norm"], p["mla_q_norm"], p["mla_k_norm"], wq, wkv]
    return _stage(name, fn, s // ts,
                  [(seg0, _rows(ts, 512))] + [(t, _rows(ts, LANES)) for t in tabs] + [(a, _full(a.shape)) for a in consts]
                  + [(dq, _heads(ts, LANES)), (dk, _heads(ts, LANES)), (dv, _heads(ts, 64)), (dff, _rows(ts, LANES))],
                  [(_sds((s, 512), BF16), _rows(ts, 512))],
                  accs=[_sds((1, 256)), _sds((1, 128)), _sds((1, LANES)), _sds((1, LANES)), _sds(wq.shape), _sds(wkv.shape)],
                  scratch=[pltpu.VMEM((ts, 1024), F32), pltpu.VMEM((ts, 1536), F32)])


def _fox_prep_fwd(seg0, seg1, bf, qn, kn, name):
    s = seg0.shape[0]
    ts = _pick(s, (256, 128))
    steps = int(math.log2(ts))

    def fn(ins, outs, accs, scr):
        kt_r, x_r, bf_r, qn_r, kn_r = ins
        carry = scr[0]

        @pl.when(pl.program_id(0) == 0)
        def _():
            carry[...] = jnp.zeros(carry.shape, F32)

        x = x_r[...]
        for h in range(HEADS):
            outs[0][h] = _rms(x[:, 64 * h:64 * (h + 1)], qn_r[...]).astype(BF16)
            outs[1][h] = _rms(x[:, 512 + 64 * h:512 + 64 * (h + 1)], kn_r[...]).astype(BF16)
            outs[2][h] = x[:, 1024 + 64 * h:1024 + 64 * (h + 1)].astype(BF16)
        kt = kt_r[...]
        lane = lax.broadcasted_iota(jnp.int32, kt.shape, 1)
        row = lax.broadcasted_iota(jnp.int32, kt.shape, 0)
        cs = jnp.where(lane < HEADS, jax.nn.log_sigmoid(kt + bf_r[...]), 0.0)
        for k in range(steps):
            sh = 1 << k
            cs = cs + jnp.where(row >= sh, pltpu.roll(cs, sh, 0), 0.0)
        cs = cs + carry[0:1, :]
        outs[3][...] = cs
        outs[4][...] = cs.T[0:HEADS, :]
        carry[0:1, :] = cs[ts - 1:ts, :]

    return _stage(name, fn, s // ts,
                  [(seg0, _rows(ts, LANES, 3)), (seg1, _rows(ts, 1536)), (bf, _full(bf.shape)), (qn, _full(qn.shape)),
                   (kn, _full(kn.shape))],
                  [(_sds((HEADS, s, 64), BF16), _heads(ts, 64)), (_sds((HEADS, s, 64), BF16), _heads(ts, 64)),
                   (_sds((HEADS, s, 64), BF16), _heads(ts, 64)), (_sds((s, LANES)), _rows(ts, LANES)),
                   (_sds((HEADS, s)), pl.BlockSpec((HEADS, ts), lambda i: (0, i)))],
                  scratch=[pltpu.VMEM((SUBLANES, LANES), F32)])


def _fox_prep_bwd(seg0, seg1, bf, qn, kn, dq, dk, dv, dck, name):
    s = seg0.shape[0]
    ts = _pick(s, (256, 128))
    n = s // ts
    steps = int(math.log2(ts))

    def fn(ins, outs, accs, scr):
        kt_r, x_r, bf_r, qn_r, kn_r, dq_r, dk_r, dv_r, dck_r = ins
        dqn, dkn, dbf = accs
        carry, dbuf = scr

        @pl.when(pl.program_id(0) == 0)
        def _():
            carry[...] = jnp.zeros(carry.shape, F32)

        x = x_r[...]
        for h in range(HEADS):
            _, vjp = jax.vjp(_rms, x[:, 64 * h:64 * (h + 1)], qn_r[...])
            d, dg = vjp(dq_r[h])
            dbuf[:, 64 * h:64 * (h + 1)] = d
            dqn[...] += dg
            _, vjp = jax.vjp(_rms, x[:, 512 + 64 * h:512 + 64 * (h + 1)], kn_r[...])
            d, dg = vjp(dk_r[h])
            dbuf[:, 512 + 64 * h:512 + 64 * (h + 1)] = d
            dkn[...] += dg
            dbuf[:, 1024 + 64 * h:1024 + 64 * (h + 1)] = dv_r[h]
        outs[0][...] = dbuf[...].astype(BF16)
        dc = dck_r[...].reshape(HEADS, ts)
        dc = jnp.concatenate([dc, jnp.zeros((LANES - HEADS, ts), F32)], axis=0).T
        row = lax.broadcasted_iota(jnp.int32, dc.shape, 0)
        lane = lax.broadcasted_iota(jnp.int32, dc.shape, 1)
        for k in range(steps):
            sh = 1 << k
            dc = dc + jnp.where(row < ts - sh, pltpu.roll(dc, ts - sh, 0), 0.0)
        dc = dc + carry[0:1, :]
        carry[0:1, :] = dc[0:1, :]
        dff = jnp.where(lane < HEADS, dc * jax.nn.sigmoid(-(kt_r[...] + bf_r[...])), 0.0)
        outs[1][...] = dff
        dbf[...] += jnp.sum(dff, axis=0, keepdims=True)

    return _stage(name, fn, n,
                  [(seg0, _rows_rev(ts, LANES, n, 3)), (seg1, _rows_rev(ts, 1536, n)), (bf, _full(bf.shape)),
                   (qn, _full(qn.shape)), (kn, _full(kn.shape)), (dq, _heads_rev(ts, 64, n)), (dk, _heads_rev(ts, 64, n)),
                   (dv, _heads_rev(ts, 64, n)), (dck, pl.BlockSpec((HEADS, 1, ts), lambda i: (0, 0, n - 1 - i)))],
                  [(_sds((s, 1536), BF16), _rows_rev(ts, 1536, n)), (_sds((s, LANES)), _rows_rev(ts, LANES, n))],
                  accs=[_sds((1, 64)), _sds((1, 64)), _sds((1, LANES))],
                  scratch=[pltpu.VMEM((SUBLANES, LANES), F32), pltpu.VMEM((ts, 1536), F32)])


def _allowed(i, j, t, chunk_causal):
    qpos = i * t + lax.broadcasted_iota(jnp.int32, (t, t), 0)
    kpos = j * t + lax.broadcasted_iota(jnp.int32, (t, t), 1)
    if chunk_causal:
        return (kpos // CHUNK) <= (qpos // CHUNK)
    return kpos <= qpos


def _pick_col(c_blk, h):
    lane = lax.broadcasted_iota(jnp.int32, c_blk.shape, 1)
    return jnp.sum(jnp.where(lane == h, c_blk, 0.0), axis=1, keepdims=True)


class _Ride:
    def __init__(self, make, take):
        self.make, self.take = make, take


class _Plan:
    def __init__(self, ins, out_shapes, n_remote, n_local, copies, aliases=None):
        self.ins, self.out_shapes, self.n_remote, self.n_local = list(ins), list(out_shapes), n_remote, n_local
        self.copies, self.aliases = copies, dict(aliases or {})

    def scratch(self):
        return [pltpu.SemaphoreType.DMA((self.n_remote,)), pltpu.SemaphoreType.DMA((self.n_remote,)),
                pltpu.SemaphoreType.DMA((max(self.n_local, 1),))]

    def start(self, in_refs, out_refs, sems):
        remote, local = self.copies(in_refs, out_refs, *sems)
        for cp in local + remote:
            cp.start()

    def wait(self, in_refs, out_refs, sems):
        remote, local = self.copies(in_refs, out_refs, *sems)
        for cp in remote:
            cp.wait()
        for cp in local:
            cp.wait()


class _Off:
    def __init__(self, ref, off):
        self.ref, self.off, self.at = ref, off, self

    def __getitem__(self, k):
        return self.ref.at[k + self.off]


def _join(plans):
    ins = [a for p in plans for a in p.ins]
    outs = [o for p in plans for o in p.out_shapes]
    aliases, i0, o0 = {}, 0, 0
    for p in plans:
        aliases.update({i0 + i: o0 + o for i, o in p.aliases.items()})
        i0, o0 = i0 + len(p.ins), o0 + len(p.out_shapes)

    def copies(in_refs, out_refs, send, recv, local):
        rem, loc, i0, o0, r0, l0 = [], [], 0, 0, 0, 0
        for p in plans:
            r, l = p.copies(in_refs[i0:i0 + len(p.ins)], out_refs[o0:o0 + len(p.out_shapes)], _Off(send, r0), _Off(recv, r0),
                            _Off(local, l0))
            rem, loc = rem + r, loc + l
            i0, o0, r0, l0 = i0 + len(p.ins), o0 + len(p.out_shapes), r0 + p.n_remote, l0 + p.n_local
        return rem, loc

    return _Plan(ins, outs, sum(p.n_remote for p in plans), sum(p.n_local for p in plans), copies, aliases)


def _call_with_ride(core, name, grid, ins, in_specs, out_shape, out_specs, semantics, vmem_mb, ride, scratch=()):
    n_in, n_out, n_scr = len(ins), len(out_shape), len(scratch)
    if ride is None:
        return pl.pallas_call(
            core, name=name, grid=grid, in_specs=in_specs, out_specs=out_specs, out_shape=out_shape,
            scratch_shapes=list(scratch),
            compiler_params=pltpu.CompilerParams(dimension_semantics=semantics, vmem_limit_bytes=vmem_mb * 1024 * 1024),
        )(*ins)
    plan = ride.make()
    ci, co = len(plan.ins), len(plan.out_shapes)

    def body(*refs):
        c_in = refs[n_in:n_in + ci]
        a_out = refs[n_in + ci:n_in + ci + n_out]
        c_out = refs[n_in + ci + n_out:n_in + ci + n_out + co]
        own = refs[n_in + ci + n_out + co:n_in + ci + n_out + co + n_scr]
        sems = refs[n_in + ci + n_out + co + n_scr:]
        ids = [pl.program_id(d) for d in range(len(grid))]
        first = functools.reduce(jnp.logical_and, [i == 0 for i in ids])
        last = functools.reduce(jnp.logical_and, [i == g - 1 for i, g in zip(ids, grid)])

        @pl.when(first)
        def _():
            plan.start(c_in, c_out, sems)

        core(*refs[:n_in], *a_out, *own)

        @pl.when(last)
        def _():
            plan.wait(c_in, c_out, sems)

    res = pl.pallas_call(
        body, name=name, grid=grid, in_specs=list(in_specs) + [ANY] * ci, out_specs=list(out_specs) + [ANY] * co,
        out_shape=list(out_shape) + plan.out_shapes, scratch_shapes=list(scratch) + plan.scratch(),
        input_output_aliases={n_in + i: n_out + o for i, o in plan.aliases.items()},
        compiler_params=pltpu.CompilerParams(dimension_semantics=("arbitrary",) * len(grid),
                                             vmem_limit_bytes=vmem_mb * 1024 * 1024),
    )(*ins, *plan.ins)
    ride.take(res[n_out:])
    return res[:n_out]


def _attn_fwd(q, k, v, scale, chunk_causal, name, c=None, ct=None, hps=8, ride=None):
    _, s, dk = q.shape
    dv = v.shape[2]
    t = _pick(s, (256, 128))
    n = s // t
    bias = c is not None

    def body(*refs):
        if bias:
            q_ref, k_ref, v_ref, c_ref, ct_ref, o_ref, lse_ref, s_s = refs
        else:
            q_ref, k_ref, v_ref, o_ref, lse_ref, s_s = refs
        hp, i = pl.program_id(0), pl.program_id(1)
        qb = [q_ref[e] for e in range(hps)]
        cq = [_pick_col(c_ref[...], hp * hps + e) if bias else None for e in range(hps)]

        def sweep1(j, m, diagonal):
            off = pl.multiple_of(j * t, t)
            out = []
            for e in range(hps):
                sc = _nt(qb[e], k_ref[e, pl.ds(off, t), :]) * scale
                if bias:
                    sc = sc + (cq[e] - ct_ref[pl.ds(hp * hps + e, 1), pl.ds(off, t)])
                if diagonal:
                    sc = jnp.where(_allowed(i, j, t, chunk_causal), sc, NEG)
                s_s[e, j] = sc
                out.append(jnp.maximum(m[e], jnp.max(sc, axis=1, keepdims=True)))
            return tuple(out)

        m0 = tuple(jnp.full((t, 1), NEG, F32) for _ in range(hps))
        m = sweep1(i, lax.fori_loop(0, i, functools.partial(sweep1, diagonal=False), m0), True)

        def sweep2(j, carry):
            off = pl.multiple_of(j * t, t)
            out = []
            for e in range(hps):
                l, acc = carry[e]
                pr = jnp.exp(s_s[e, j] - m[e])
                out.append((l + jnp.sum(pr, axis=1, keepdims=True), acc + _nn(pr, v_ref[e, pl.ds(off, t), :])))
            return tuple(out)

        init = tuple((jnp.zeros((t, 1), F32), jnp.zeros((t, dv), F32)) for _ in range(hps))
        res = lax.fori_loop(0, i + 1, sweep2, init)
        for e in range(hps):
            l, acc = res[e]
            o_ref[e] = acc / l
            lse_ref[e] = m[e] + jnp.log(l)

    ins = [q, k, v] + ([c, ct] if bias else [])
    in_specs = [pl.BlockSpec((hps, t, dk), lambda h, i: (h, i, 0)), pl.BlockSpec((hps, s, dk), lambda h, i: (h, 0, 0)),
                pl.BlockSpec((hps, s, dv), lambda h, i: (h, 0, 0))]
    if bias:
        in_specs += [pl.BlockSpec((t, LANES), lambda h, i: (i, 0)), pl.BlockSpec((HEADS, s), lambda h, i: (0, 0))]
    return _call_with_ride(
        body, name, (HEADS // hps, n), ins, in_specs, [_sds((HEADS, s, dv)), _sds((HEADS, s, 1))],
        [pl.BlockSpec((hps, t, dv), lambda h, i: (h, i, 0)), pl.BlockSpec((hps, t, 1), lambda h, i: (h, i, 0))],
        ("parallel", "parallel"), 56, ride, scratch=[pltpu.VMEM((hps, n, t, t), F32)])


def _attn_bwd_rows(q, k, v, do, lse, scale, chunk_causal, name, c=None, ct=None, hps=4, ride=None):
    _, s, dk = q.shape
    dv = v.shape[2]
    t = _pick(s, (256, 128))
    n = s // t
    bias = c is not None

    def body(*refs):
        if bias:
            q_ref, k_ref, v_ref, do_ref, lse_ref, c_ref, ct_ref, dq_ref, dk_ref, dv_ref, dck_ref, p_s, dp_s = refs
        else:
            q_ref, k_ref, v_ref, do_ref, lse_ref, dq_ref, dk_ref, dv_ref, p_s, dp_s = refs
        hp, i = pl.program_id(0), pl.program_id(1)

        @pl.when(i == 0)
        def _():
            dk_ref[...] = jnp.zeros(dk_ref.shape, F32)
            dv_ref[...] = jnp.zeros(dv_ref.shape, F32)
            if bias:
                dck_ref[...] = jnp.zeros(dck_ref.shape, F32)

        qb = [q_ref[e] for e in range(hps)]
        dob = [do_ref[e] for e in range(hps)]
        lse_b = [lse_ref[e] for e in range(hps)]
        cq = [_pick_col(c_ref[...], hp * hps + e) if bias else None for e in range(hps)]

        def sweep1(j, acc, diagonal):
            off = pl.multiple_of(j * t, t)
            out = []
            for e in range(hps):
                sc = _nt(qb[e], k_ref[e, pl.ds(off, t), :]) * scale
                if bias:
                    sc = sc + (cq[e] - ct_ref[pl.ds(hp * hps + e, 1), pl.ds(off, t)])
                pr = jnp.exp(sc - lse_b[e])
                if diagonal:
                    pr = jnp.where(_allowed(i, j, t, chunk_causal), pr, 0.0)
                dp = _nt(dob[e], v_ref[e, pl.ds(off, t), :])
                p_s[e, j] = pr
                dp_s[e, j] = dp
                out.append(acc[e] + jnp.sum(pr * dp, axis=1, keepdims=True))
            return tuple(out)

        zero = tuple(jnp.zeros((t, 1), F32) for _ in range(hps))
        delta = sweep1(i, lax.fori_loop(0, i, functools.partial(sweep1, diagonal=False), zero), True)

        def sweep2(j, dq_acc):
            off = pl.multiple_of(j * t, t)
            out = []
            for e in range(hps):
                pr = p_s[e, j]
                ds = pr * (dp_s[e, j] - delta[e])
                kb = k_ref[e, pl.ds(off, t), :]
                dv_ref[e, pl.ds(off, t), :] += _tn(pr, dob[e])
                dk_ref[e, pl.ds(off, t), :] += _tn(ds, qb[e]) * scale
                if bias:
                    dck_ref[e, j] -= jnp.sum(ds, axis=0, keepdims=True)
                out.append(dq_acc[e] + _nn(ds, kb))
            return tuple(out)

        dq = lax.fori_loop(0, i + 1, sweep2, tuple(jnp.zeros((t, dk), F32) for _ in range(hps)))
        for e in range(hps):
            dq_ref[e] = dq[e] * scale

    full = lambda d: pl.BlockSpec((hps, s, d), lambda h, i: (h, 0, 0))
    blk = lambda d: pl.BlockSpec((hps, t, d), lambda h, i: (h, i, 0))
    ins = [q, k, v, do, lse] + ([c, ct] if bias else [])
    in_specs = [blk(dk), full(dk), full(dv), blk(dv), blk(1)]
    out_specs = [blk(dk), full(dk), full(dv)]
    out_shape = [_sds((HEADS, s, dk)), _sds((HEADS, s, dk)), _sds((HEADS, s, dv))]
    if bias:
        in_specs += [pl.BlockSpec((t, LANES), lambda h, i: (i, 0)), pl.BlockSpec((HEADS, s), lambda h, i: (0, 0))]
        out_specs.append(pl.BlockSpec((hps, n, 1, t), lambda h, i: (h, 0, 0, 0)))
        out_shape.append(_sds((HEADS, n, 1, t)))
    res = _call_with_ride(body, name, (HEADS // hps, n), ins, in_specs, out_shape, out_specs, ("parallel", "arbitrary"), 56,
                          ride, scratch=[pltpu.VMEM((hps, n, t, t), F32), pltpu.VMEM((hps, n, t, t), F32)])
    return (*res[:3], res[3].reshape(HEADS, 1, s)) if bias else tuple(res)


def _s5_disc(lr, li, ldt, br, bi):
    dt = jnp.exp(ldt)
    mag = jnp.exp(lr * dt)
    a_re = mag * jnp.cos(li * dt)
    a_im = mag * jnp.sin(li * dt)
    den = lr * lr + li * li
    f_re = ((a_re - 1.0) * lr + a_im * li) / den
    f_im = (a_im * lr - (a_re - 1.0) * li) / den
    return a_re, a_im, f_re * br - f_im * bi, f_re * bi + f_im * br


def _s5_params_fwd(lr16, li16, ldt16, br2, bi2, name):
    def body(a, b, c, d, e, o0, o1, o2, o3):
        r = _s5_disc(a[...], b[...], c[...], d[...], e[...])
        o0[...], o1[...], o2[...], o3[...] = r

    return pl.pallas_call(body, name=name, out_shape=[_sds((512, 64))] * 4)(lr16, li16, ldt16, br2, bi2)


def _s5_params_bwd(lr16, li16, ldt16, br2, bi2, da_re16, da_im16, dbb_re, dbb_im, name):
    def body(a, b, c, d, e, g0, g1, g2, g3, o_lr, o_li, o_dt, o_br, o_bi):
        _, vjp = jax.vjp(_s5_disc, a[...], b[...], c[...], d[...], e[...])
        dlr, dli, dldt, dbr, dbi = vjp((g0[...], g1[...], g2[...], g3[...]))
        grp = lambda z: z.reshape(S5_GROUPS, S5_GROUP, S5_STATE).sum(axis=1)
        o_lr[...] = grp(dlr)
        o_li[...] = grp(dli)
        o_dt[...] = jnp.sum(grp(dldt), axis=1, keepdims=True)
        o_br[...] = dbr
        o_bi[...] = dbi

    return pl.pallas_call(
        body, name=name, out_shape=[_sds((32, 64)), _sds((32, 64)), _sds((32, 1)), _sds((512, 64)), _sds((512, 64))],
    )(lr16, li16, ldt16, br2, bi2, da_re16, da_im16, dbb_re, dbb_im)


def _cmul(ar, ai, br, bi):
    return ar * br - ai * bi, ar * bi + ai * br


S5_SUPER = 4


def _scan_loop(bre_r, bim_r, ar, ai, ore_r, oim_r, reverse, xre_r=None, xim_r=None):
    s, lw = bre_r.shape
    nt = s // SUBLANES
    with_da = xre_r is not None
    shp = (SUBLANES, lw)
    row = lax.broadcasted_iota(jnp.int32, shp, 0)
    pows = [(ar, ai)]
    for _ in range(SUBLANES - 1):
        pows.append(_cmul(pows[-1][0], pows[-1][1], ar, ai))
    cm_r, cm_i = jnp.zeros(shp, F32), jnp.zeros(shp, F32)
    for r in range(SUBLANES):
        e = (SUBLANES - 1 - r) if reverse else r
        cm_r = jnp.where(row == r, jnp.broadcast_to(pows[e][0], shp), cm_r)
        cm_i = jnp.where(row == r, jnp.broadcast_to(pows[e][1], shp), cm_i)
    steps = [(1, pows[0]), (2, pows[1]), (4, pows[3])]

    def tile(it, carry):
        if with_da:
            c_r, c_i, acc_r, acc_i = carry
        else:
            c_r, c_i = carry
        r = (nt - 1 - it) if reverse else it
        off = pl.multiple_of(r * SUBLANES, SUBLANES)
        xr, xi = bre_r[pl.ds(off, SUBLANES), :], bim_r[pl.ds(off, SUBLANES), :]
        for sh, (pr, pi) in steps:
            if reverse:
                keep = row < SUBLANES - sh
                sr = jnp.where(keep, pltpu.roll(xr, SUBLANES - sh, 0), 0.0)
                si = jnp.where(keep, pltpu.roll(xi, SUBLANES - sh, 0), 0.0)
            else:
                keep = row >= sh
                sr = jnp.where(keep, pltpu.roll(xr, sh, 0), 0.0)
                si = jnp.where(keep, pltpu.roll(xi, sh, 0), 0.0)
            mr, mi = _cmul(pr, pi, sr, si)
            xr, xi = xr + mr, xi + mi
        mr, mi = _cmul(cm_r, cm_i, c_r, c_i)
        xr, xi = xr + mr, xi + mi
        ore_r[pl.ds(off, SUBLANES), :] = xr
        oim_r[pl.ds(off, SUBLANES), :] = xi
        edge = 0 if reverse else SUBLANES - 1
        c_r, c_i = xr[edge:edge + 1, :], xi[edge:edge + 1, :]
        if not with_da:
            return c_r, c_i
        fr, fi = xre_r[pl.ds(off, SUBLANES), :], xim_r[pl.ds(off, SUBLANES), :]
        poff = pl.multiple_of(jnp.maximum(r - 1, 0) * SUBLANES, SUBLANES)
        live = (r > 0).astype(F32)
        pr_last = xre_r[pl.ds(poff, SUBLANES), :][SUBLANES - 1:SUBLANES, :] * live
        pi_last = xim_r[pl.ds(poff, SUBLANES), :][SUBLANES - 1:SUBLANES, :] * live
        sr = jnp.where(row >= 1, pltpu.roll(fr, 1, 0), jnp.broadcast_to(pr_last, shp))
        si = jnp.where(row >= 1, pltpu.roll(fi, 1, 0), jnp.broadcast_to(pi_last, shp))
        return c_r, c_i, acc_r + xr * sr + xi * si, acc_i + xi * sr - xr * si

    z1 = jnp.zeros((1, lw), F32)
    if not with_da:
        lax.fori_loop(0, nt, tile, (z1, z1))
        return None
    _, _, acc_r, acc_i = lax.fori_loop(0, nt, tile, (z1, z1, jnp.zeros(shp, F32), jnp.zeros(shp, F32)))
    return jnp.sum(acc_r, axis=0, keepdims=True), jnp.sum(acc_i, axis=0, keepdims=True)


S5_ROWS = 512


def _group_compact(p):
    grp = lax.broadcasted_iota(jnp.int32, (LANES, S5_STATE), 0) // S5_GROUP
    out = jnp.zeros((LANES, S5_STATE), F32)
    for j in range(LANES // S5_GROUP):
        out = jnp.where(grp == j, p[:, S5_STATE * j:S5_STATE * (j + 1)], out)
    return out


def _s5_core_fwd(u, wb_re, wb_im, wc_re, wc_im, a_re, a_im, name, ride=None):
    s = u.shape[0]
    lw = S5_LANES // S5_SUPER
    rows = _pick(s, (S5_ROWS, 256, 128))

    def body(u_r, wbr, wbi, wcr, wci, are_r, aim_r, xre_r, xim_r, y_r, bre_s, bim_s):
        for r0 in range(0, s, rows):
            ub = u_r[r0:r0 + rows, :]
            bre_s[r0:r0 + rows, :] = _nn(ub, wbr[0])
            bim_s[r0:r0 + rows, :] = _nn(ub, wbi[0])
        _scan_loop(bre_s, bim_s, are_r[...], aim_r[...], xre_r, xim_r, False)
        for r0 in range(0, s, rows):
            y_r[r0:r0 + rows, :] = _nn(xre_r[r0:r0 + rows, :], wcr[0]) + _nn(xim_r[r0:r0 + rows, :], wci[0])

    nar = pl.BlockSpec((s, LANES), lambda k: (0, k))
    wide = pl.BlockSpec((s, lw), lambda k: (0, k))
    one = pl.BlockSpec((1, lw), lambda k: (0, k))
    wb = pl.BlockSpec((1, LANES, lw), lambda k: (k, 0, 0))
    wc = pl.BlockSpec((1, lw, LANES), lambda k: (k, 0, 0))
    return _call_with_ride(body, name, (S5_SUPER,), [u, wb_re, wb_im, wc_re, wc_im, a_re, a_im], [nar, wb, wb, wc, wc, one, one],
                           [_sds((s, S5_LANES)), _sds((s, S5_LANES)), _sds((s, S5_WIDTH))], [wide, wide, nar], ("parallel",), 56,
                           ride, scratch=[pltpu.VMEM((s, lw), F32), pltpu.VMEM((s, lw), F32)])


def _s5_core_bwd(dy0, u, du_a, x_re, x_im, wb_re, wb_im, wc_re, wc_im, a_re, a_im_neg, name, ride=None):
    s = u.shape[0]
    lw = S5_LANES // S5_SUPER
    rows = _pick(s, (S5_ROWS, 256, 128))

    def body(dy_r, u_r, dua_r, xre_r, xim_r, wbr, wbi, wcr, wci, are_r, aim_r, du_r, dare_r, daim_r, dbr_r, dbi_r, dcr_r, dci_r,
             dre_s, dim_s, gre_s, gim_s):
        for r0 in range(0, s, rows):
            dyb = dy_r[r0:r0 + rows, :]
            dre_s[r0:r0 + rows, :] = _nt(dyb, wcr[0])
            dim_s[r0:r0 + rows, :] = _nt(dyb, wci[0])
        dare_r[...], daim_r[...] = _scan_loop(dre_s, dim_s, are_r[...], aim_r[...], gre_s, gim_s, True, xre_r, xim_r)
        acc = [jnp.zeros((LANES, lw), F32) for _ in range(4)]
        for r0 in range(0, s, rows):
            sl = slice(r0, r0 + rows)
            gr, gi, ub, dyb = gre_s[sl, :], gim_s[sl, :], u_r[sl, :], dy_r[sl, :]
            du_r[sl, :] = dua_r[sl, :] + _nt(gr, wbr[0]) + _nt(gi, wbi[0])
            acc = [acc[0] + _tn(ub, gr), acc[1] + _tn(ub, gi), acc[2] + _tn(dyb, xre_r[sl, :]), acc[3] + _tn(dyb, xim_r[sl, :])]
        dbr_r[...], dbi_r[...], dcr_r[...], dci_r[...] = (_group_compact(a) for a in acc)

    nar = pl.BlockSpec((s, LANES), lambda k: (0, k))
    wide = pl.BlockSpec((s, lw), lambda k: (0, k))
    one = pl.BlockSpec((1, lw), lambda k: (0, k))
    wb = pl.BlockSpec((1, LANES, lw), lambda k: (k, 0, 0))
    wc = pl.BlockSpec((1, lw, LANES), lambda k: (k, 0, 0))
    blk = pl.BlockSpec((LANES, S5_STATE), lambda k: (k, 0))
    return _call_with_ride(
        body, name, (S5_SUPER,), [dy0, u, du_a, x_re, x_im, wb_re, wb_im, wc_re, wc_im, a_re, a_im_neg],
        [nar, nar, nar, wide, wide, wb, wb, wc, wc, one, one],
        [_sds((s, S5_WIDTH)), _sds((1, S5_LANES)), _sds((1, S5_LANES))] + [_sds((S5_WIDTH, S5_STATE))] * 4,
        [nar, one, one, blk, blk, blk, blk], ("parallel",), 60, ride, scratch=[pltpu.VMEM((s, lw), F32)] * 4)


def _s5_seg1(y0, u, d):
    return jax.nn.gelu(y0 + d * u)


def _s5_seg2(z, t, b):
    return z * jax.nn.sigmoid(t + b)


def _s5_post_fwd(y0, seg2, d, wglu, bglu, name):
    s = y0.shape[0]
    ts = _pick(s, (256, 128))

    def fn(ins, outs, accs, scr):
        z = _s5_seg1(ins[0][...], ins[1][...], ins[2][...])
        outs[0][...] = _s5_seg2(z, _nn(z, ins[3][...]), ins[4][...])

    return _stage(name, fn, s // ts,
                  [(y0, _rows(ts, 512)), (seg2, _rows(ts, 512)), (d, _full(d.shape)), (wglu, _full(wglu.shape)),
                   (bglu, _full(bglu.shape))], [(_sds((s, 512)), _rows(ts, 512))])[0]


def _s5_post_bwd(y0, seg2, d, wglu, bglu, dy, name):
    s = y0.shape[0]
    ts = _pick(s, (256, 128))

    def fn(ins, outs, accs, scr):
        y0_r, u_r, d_r, w_r, b_r, dy_r = ins
        z, vjp1 = jax.vjp(_s5_seg1, y0_r[...], u_r[...], d_r[...])
        t = _nn(z, w_r[...])
        _, vjp2 = jax.vjp(_s5_seg2, z, t, b_r[...])
        dz, dt, db = vjp2(dy_r[...])
        accs[0][...] += _tn(z, dt)
        accs[1][...] += db
        dy0, du, dd = vjp1(dz + _nt(dt, w_r[...]))
        accs[2][...] += dd
        outs[0][...] = dy0
        outs[1][...] = du

    return _stage(name, fn, s // ts,
                  [(y0, _rows(ts, 512)), (seg2, _rows(ts, 512)), (d, _full(d.shape)), (wglu, _full(wglu.shape)),
                   (bglu, _full(bglu.shape)), (dy, _rows(ts, 512))],
                  [(_sds((s, 512)), _rows(ts, 512)), (_sds((s, 512)), _rows(ts, 512))],
                  accs=[_sds((512, 512)), _sds((1, 512)), _sds((1, 512))])


def _gate_a(y, g):
    return y * jax.nn.silu(g)


def _gate_m(o0, o1, o2, m0, m1, m2):
    return jax.nn.sigmoid(m0) * o0 + jax.nn.sigmoid(m1) * o1 + jax.nn.sigmoid(m2) * o2


def _assemble(ybuf, o_mla, o_fox, y_s5):
    for h in range(HEADS):
        ybuf[:, 64 * h:64 * (h + 1)] = o_mla[h]
        ybuf[:, 512 + 64 * h:512 + 64 * (h + 1)] = o_fox[h]
    ybuf[:, 1024:1536] = y_s5[...]


def _gate_fwd(o_mla, o_fox, y_s5, seg3, x, wo, wout, name):
    s = x.shape[0]
    ts = _pick(s, (256, 128))

    def fn(ins, outs, accs, scr):
        om, of, ys, gm_r, x_r, wo_r, wout_r = ins
        ybuf = scr[0]
        _assemble(ybuf, om, of, ys)
        a = _gate_a(ybuf[...], gm_r[:, :GATES_W])
        o = [_nn(a[:, 512 * b:512 * (b + 1)], wo_r[512 * b:512 * (b + 1), :]) for b in range(3)]
        merged = _gate_m(o[0], o[1], o[2], *[gm_r[:, GATES_W + 1024 * b:GATES_W + 1024 * (b + 1)] for b in range(3)])
        outs[0][...] = x_r[...] + _nn(merged, wout_r[...])

    return _stage(name, fn, s // ts,
                  [(o_mla, _heads(ts, 64)), (o_fox, _heads(ts, 64)), (y_s5, _rows(ts, 512)), (seg3, _rows(ts, SEG_W[3])),
                   (x, _rows(ts, D_MODEL)), (wo, _full(wo.shape)), (wout, _full(wout.shape))],
                  [(_sds((s, D_MODEL)), _rows(ts, D_MODEL))], scratch=[pltpu.VMEM((ts, 1536), F32)])[0]


def _gate_bwd(o_mla, o_fox, y_s5, seg3, wo, wout, dout, name):
    s = dout.shape[0]
    ts = _pick(s, (256, 128))

    def fn(ins, outs, accs, scr):
        om, of, ys, gm_r, wo_r, wout_r, dout_r = ins
        do_mla, do_fox, dys, dgm_r = outs
        dwo, dwout = accs
        ybuf, dabuf = scr
        _assemble(ybuf, om, of, ys)
        a, vjp_a = jax.vjp(_gate_a, ybuf[...], gm_r[:, :GATES_W])
        o = [_nn(a[:, 512 * b:512 * (b + 1)], wo_r[512 * b:512 * (b + 1), :]) for b in range(3)]
        ms = [gm_r[:, GATES_W + 1024 * b:GATES_W + 1024 * (b + 1)] for b in range(3)]
        merged, vjp_m = jax.vjp(_gate_m, *o, *ms)
        dout_v = dout_r[...]
        dwout[...] += _tn(merged, dout_v)
        cts = vjp_m(_nt(dout_v, wout_r[...]))
        for b in range(3):
            dgm_r[:, GATES_W + 1024 * b:GATES_W + 1024 * (b + 1)] = cts[3 + b].astype(BF16)
            dwo[512 * b:512 * (b + 1), :] += _tn(a[:, 512 * b:512 * (b + 1)], cts[b])
            dabuf[:, 512 * b:512 * (b + 1)] = _nt(cts[b], wo_r[512 * b:512 * (b + 1), :])
        dy, dg = vjp_a(dabuf[...])
        dgm_r[:, :GATES_W] = dg.astype(BF16)
        dys[...] = dy[:, 1024:1536]
        for h in range(HEADS):
            do_mla[h] = dy[:, 64 * h:64 * (h + 1)]
            do_fox[h] = dy[:, 512 + 64 * h:512 + 64 * (h + 1)]

    return _stage(name, fn, s // ts,
                  [(o_mla, _heads(ts, 64)), (o_fox, _heads(ts, 64)), (y_s5, _rows(ts, 512)), (seg3, _rows(ts, SEG_W[3])),
                   (wo, _full(wo.shape)), (wout, _full(wout.shape)), (dout, _rows(ts, D_MODEL))],
                  [(_sds((HEADS, s, 64)), _heads(ts, 64)), (_sds((HEADS, s, 64)), _heads(ts, 64)), (_sds((s, 512)), _rows(ts, 512)),
                   (_sds((s, SEG_W[3]), BF16), _rows(ts, SEG_W[3]))],
                  accs=[_sds(wo.shape), _sds(wout.shape)], scratch=[pltpu.VMEM((ts, 1536), F32), pltpu.VMEM((ts, 1536), F32)],
                  vmem_mb=60, acc_in_scratch=True)


def _loss_head(y, target, name):
    s = y.shape[0]
    ts = _pick(s, (256, 128))

    def fn(ins, outs, accs, scr):
        e = ins[0][...] - ins[1][...]
        outs[0][...] = e / D_MODEL
        accs[0][...] += 0.5 * jnp.sum(jnp.sum(e * e, axis=1, keepdims=True) / D_MODEL, axis=0, keepdims=True)

    r = _stage(name, fn, s // ts, [(y, _rows(ts, D_MODEL)), (target, _rows(ts, D_MODEL))],
               [(_sds((s, D_MODEL)), _rows(ts, D_MODEL))], accs=[_sds((1, 1))])
    return r[1], r[0]


IN_RANGES = ((0, 384, 0), (384, 416, 448), (416, 1952, 512), (1952, 1960, 384), (1960, IN_WIDTH, 2048))
SHARD_W = IN_WIDTH // N_DEV


def _win_pieces(d):
    lo, hi = SHARD_W * d, SHARD_W * (d + 1)
    out = []
    for a, b, p in IN_RANGES:
        s, e = max(a, lo), min(b, hi)
        while s < e:
            pad = p + (s - a)
            k = max(i for i in range(NSEG) if SEG_OFF[i] <= pad)
            w = min(e - s, SEG_OFF[k] + SEG_W[k] - pad)
            out.append((s - lo, w, k, pad - SEG_OFF[k]))
            s += w
    return out


def _win_pad(g, name):
    _, r, _ = g.shape
    tr = 64

    def body(g_ref, o_ref):
        o_ref[...] = jnp.zeros(o_ref.shape, o_ref.dtype)
        for d in range(N_DEV):
            for dst, w, k, src in _win_pieces(d):
                o_ref[:, SEG_OFF[k] + src:SEG_OFF[k] + src + w] = g_ref[d, :, dst:dst + w]

    return pl.pallas_call(
        body, name=name, grid=(r // tr,), in_specs=[pl.BlockSpec((N_DEV, tr, SHARD_W), lambda i: (0, i, 0))],
        out_specs=pl.BlockSpec((tr, PAD_IN), lambda i: (i, 0)), out_shape=jax.ShapeDtypeStruct((r, PAD_IN), g.dtype),
        compiler_params=pltpu.CompilerParams(dimension_semantics=("parallel",)),
    )(g)


def _win_unpad(dsegs, name):
    r = dsegs[0].shape[0]
    tr = 64

    def body(*refs):
        o_ref = refs[NSEG]
        for d in range(N_DEV):
            for dst, w, k, src in _win_pieces(d):
                o_ref[d, :, dst:dst + w] = refs[k][:, src:src + w]

    return pl.pallas_call(
        body, name=name, grid=(r // tr,), in_specs=[pl.BlockSpec((tr, SEG_W[k]), lambda i: (i, 0)) for k in range(NSEG)],
        out_specs=pl.BlockSpec((N_DEV, tr, SHARD_W), lambda i: (0, i, 0)),
        out_shape=jax.ShapeDtypeStruct((N_DEV, r, SHARD_W), dsegs[0].dtype),
        compiler_params=pltpu.CompilerParams(dimension_semantics=("parallel",)),
    )(*dsegs)


def _pad_wq(w):
    w = w.reshape(MLA_Q_RANK, HEADS, MLA_QK)
    return jnp.pad(w, ((0, 0), (0, 0), (0, LANES - MLA_QK))).reshape(MLA_Q_RANK, HEADS * LANES)


def _unpad_wq(d):
    return d.reshape(MLA_Q_RANK, HEADS, LANES)[:, :, :MLA_QK].reshape(MLA_Q_RANK, HEADS * MLA_QK)


def _pad_wkv(w):
    w = w.reshape(MLA_KV_RANK, HEADS, MLA_NOPE + MLA_V)
    k = jnp.pad(w[:, :, :MLA_NOPE], ((0, 0), (0, 0), (0, LANES - MLA_NOPE))).reshape(MLA_KV_RANK, HEADS * LANES)
    return jnp.concatenate([k, w[:, :, MLA_NOPE:].reshape(MLA_KV_RANK, HEADS * MLA_V)], axis=1)


def _unpad_wkv(d):
    k = d[:, :HEADS * LANES].reshape(MLA_KV_RANK, HEADS, LANES)[:, :, :MLA_NOPE]
    v = d[:, HEADS * LANES:].reshape(MLA_KV_RANK, HEADS, MLA_V)
    return jnp.concatenate([k, v], axis=2).reshape(MLA_KV_RANK, HEADS * (MLA_NOPE + MLA_V))


def _pad_lanes(v, n=LANES):
    return jnp.pad(v, (0, n - v.shape[0])).reshape(1, n)


def _super_blocks(b):
    _, r, c = b.shape
    per = S5_GROUPS // S5_SUPER
    b = b.reshape(S5_SUPER, per, r, c)
    eye = jnp.eye(per, dtype=b.dtype)
    return (b[:, :, :, None, :] * eye[None, :, None, :, None]).reshape(S5_SUPER, per * r, per * c)


def _layer_params(l, w, small):
    p = {k: small[k][l] for k in small}
    q = {}
    q["norm_g"] = p["norm_g"].reshape(1, D_MODEL)
    q["mla_q_a_norm"] = p["mla_q_a_norm"].reshape(1, 256)
    q["mla_kv_a_norm"] = p["mla_kv_a_norm"].reshape(1, 128)
    q["mla_q_norm"] = _pad_lanes(p["mla_q_norm"])
    q["mla_k_norm"] = _pad_lanes(p["mla_k_norm"])
    q["fox_b_f"] = _pad_lanes(p["fox_b_f"])
    q["fox_q_norm"] = p["fox_q_norm"].reshape(1, 64)
    q["fox_k_norm"] = p["fox_k_norm"].reshape(1, 64)
    q["s5_d"] = p["s5_d"].reshape(1, 512)
    q["s5_b_glu"] = p["s5_b_glu"].reshape(1, 512)
    rep = lambda z: jnp.repeat(z, S5_GROUP, axis=0)
    q["lr16"], q["li16"] = rep(p["s5_lambda_re"]), rep(p["s5_lambda_im"])
    q["ldt16"] = rep(jnp.broadcast_to(p["s5_log_dt"][:, None], (S5_GROUPS, S5_STATE)))
    q["br2"] = p["s5_b_re"].transpose(0, 2, 1).reshape(512, 64)
    q["bi2"] = p["s5_b_im"].transpose(0, 2, 1).reshape(512, 64)
    q["c_re"], q["c_im"] = p["s5_c_re"], p["s5_c_im"]
    q["w_in"] = _win_pad(w["w_in"], f"l{l}_w_in_pad")
    q["wq"] = _pad_wq(w["mla_w_q_up"])
    q["wkv"] = _pad_wkv(w["mla_w_kv_up"])
    q.update(_late_weights(w))
    return q


def _late_weights(w):
    return {q: w[k] for q, k in (("wglu", "s5_w_glu"), ("wo", "w_branch_out"), ("wout", "w_out")) if k in w}


def _layer_fwd(l, x, tabs, q, rides=None):
    rides = rides or {}
    n = lambda s: f"l{l}_{s}"
    sv = {"x": x}
    h = _norm_fwd(x, q["norm_g"], n("norm_fwd"))
    sv["h"] = h
    seg = [_mm(h, q["w_in"], "nn", n(f"proj{k}"), b_cols=(SEG_OFF[k], SEG_W[k])) for k in range(NSEG)]
    sv["seg"] = seg
    mq, mk, mv = _mla_prep_fwd(seg[0], tabs, q, q["wq"], q["wkv"], n("mla_prep_fwd"))
    o_mla, lse_mla = _attn_fwd(mq, mk, mv, 1.0 / math.sqrt(MLA_QK), True, n("mla_attn_fwd"), ride=rides.get("mla_attn_fwd"))
    sv.update(mq=mq, mk=mk, mv=mv, o_mla=o_mla, lse_mla=lse_mla)
    fq, fk, fv, c, ct = _fox_prep_fwd(seg[0], seg[1], q["fox_b_f"], q["fox_q_norm"], q["fox_k_norm"], n("fox_prep_fwd"))
    o_fox, lse_fox = _attn_fwd(fq, fk, fv, 1.0 / math.sqrt(FOX_DIM), False, n("fox_attn_fwd"), c=c, ct=ct,
                               ride=rides.get("fox_attn_fwd"))
    sv.update(fq=fq, fk=fk, fv=fv, c=c, ct=ct, o_fox=o_fox, lse_fox=lse_fox)
    a_re16, a_im16, bb_re, bb_im = _s5_params_fwd(q["lr16"], q["li16"], q["ldt16"], q["br2"], q["bi2"], n("s5_params_fwd"))
    a_re = a_re16.reshape(S5_GROUPS, S5_GROUP, S5_STATE)[:, 0, :].reshape(1, S5_LANES)
    a_im = a_im16.reshape(S5_GROUPS, S5_GROUP, S5_STATE)[:, 0, :].reshape(1, S5_LANES)
    wb_re = _super_blocks(bb_re.reshape(S5_GROUPS, S5_GROUP, S5_STATE)).astype(BF16)
    wb_im = _super_blocks(bb_im.reshape(S5_GROUPS, S5_GROUP, S5_STATE)).astype(BF16)
    wc_re = _super_blocks(q["c_re"].transpose(0, 2, 1)).astype(BF16)
    wc_im = _super_blocks(-q["c_im"].transpose(0, 2, 1)).astype(BF16)
    x_re, x_im, y0 = _s5_core_fwd(seg[2], wb_re, wb_im, wc_re, wc_im, a_re, a_im, n("s5_scan_fwd"), ride=rides.get("s5_scan_fwd"))
    y_s5 = _s5_post_fwd(y0, seg[2], q["s5_d"], q["wglu"], q["s5_b_glu"], n("s5_post_fwd"))
    sv.update(a_re=a_re, a_im=a_im, wb_re=wb_re, wb_im=wb_im, wc_re=wc_re, wc_im=wc_im, x_re=x_re, x_im=x_im, y0=y0, y_s5=y_s5)
    out = _gate_fwd(o_mla, o_fox, y_s5, seg[3], x, q["wo"], q["wout"], n("gate_fwd"))
    return out, sv


def _layer_bwd(l, dout, tabs, q, sv, rides=None, g=None):
    rides = rides or {}
    n = lambda s: f"l{l}_{s}"
    seg = sv["seg"]
    g = {} if g is None else g
    (do_mla, do_fox, dy_s5, dseg3, g["wo"], g["wout"]) = _gate_bwd(
        sv["o_mla"], sv["o_fox"], sv["y_s5"], seg[3], q["wo"], q["wout"], dout, n("gate_bwd"))
    dy0, du_a, g["wglu"], g["s5_b_glu"], g["s5_d"] = _s5_post_bwd(sv["y0"], seg[2], q["s5_d"], q["wglu"], q["s5_b_glu"], dy_s5,
                                                                 n("s5_post_bwd"))
    dseg2, da_re, da_im, dbb_re, dbb_im, dc_re, dc_im = _s5_core_bwd(
        dy0, seg[2], du_a, sv["x_re"], sv["x_im"], sv["wb_re"], sv["wb_im"], sv["wc_re"], sv["wc_im"], sv["a_re"], -sv["a_im"],
        n("s5_scan_bwd"), ride=rides.get("s5_scan_bwd"))
    g["s5_c_re"] = dc_re.reshape(S5_GROUPS, S5_GROUP, S5_STATE)
    g["s5_c_im"] = -dc_im.reshape(S5_GROUPS, S5_GROUP, S5_STATE)
    first = (jnp.arange(512) % S5_GROUP == 0).astype(F32)[:, None]
    da_re16 = jnp.repeat(da_re.reshape(S5_GROUPS, S5_STATE), S5_GROUP, axis=0) * first
    da_im16 = jnp.repeat(da_im.reshape(S5_GROUPS, S5_STATE), S5_GROUP, axis=0) * first
    dlr, dli, dldt, dbr2, dbi2 = _s5_params_bwd(q["lr16"], q["li16"], q["ldt16"], q["br2"], q["bi2"], da_re16, da_im16, dbb_re,
                                               dbb_im, n("s5_params_bwd"))
    g["s5_lambda_re"], g["s5_lambda_im"], g["s5_log_dt"] = dlr, dli, dldt.reshape(S5_GROUPS)
    g["s5_b_re"] = dbr2.reshape(S5_GROUPS, S5_GROUP, S5_STATE).transpose(0, 2, 1)
    g["s5_b_im"] = dbi2.reshape(S5_GROUPS, S5_GROUP, S5_STATE).transpose(0, 2, 1)
    dfq, dfk, dfv, dck = _attn_bwd_rows(sv["fq"], sv["fk"], sv["fv"], do_fox, sv["lse_fox"], 1.0 / math.sqrt(FOX_DIM), False,
                                        n("fox_attn_bwd"), sv["c"], sv["ct"], ride=rides.get("fox_attn_bwd"))
    dseg1, dff, g["fox_q_norm"], g["fox_k_norm"], dbf = _fox_prep_bwd(seg[0], seg[1], q["fox_b_f"], q["fox_q_norm"], q["fox_k_norm"],
                                                                      dfq, dfk, dfv, dck, n("fox_prep_bwd"))
    g["fox_b_f"] = dbf[0, :HEADS]
    dmq, dmk, dmv = _attn_bwd_rows(sv["mq"], sv["mk"], sv["mv"], do_mla, sv["lse_mla"], 1.0 / math.sqrt(MLA_QK), True,
                                   n("mla_attn_bwd"), ride=rides.get("mla_attn_bwd"))
    dseg0, dqan, dkvan, dqn, dkn, g["wq"], g["wkv"] = _mla_prep_bwd(seg[0], tabs, q, q["wq"], q["wkv"], dmq, dmk, dmv, dff,
                                                                   n("mla_prep_bwd"))
    g["mla_q_a_norm"], g["mla_kv_a_norm"] = dqan, dkvan
    g["mla_q_norm"], g["mla_k_norm"] = dqn[0, :MLA_QK], dkn[0, :MLA_QK]
    dsegs = [dseg0, dseg1, dseg2, dseg3]
    g["w_in"] = [_mm(sv["h"], dsegs[k], "tn", n(f"dwin{k}")) for k in range(NSEG)]
    dh = None
    for k in range(NSEG):
        dh = _mm(dsegs[k], q["w_in"], "nt", n(f"dh{k}"), acc=dh, b_cols=(SEG_OFF[k], SEG_W[k]), ride=rides.get(f"dh{k}"))
    dx, g["norm_g"] = _norm_bwd(sv["x"], q["norm_g"], dh, dout, n("norm_bwd"))
    return dx, g


MESH = pl.DeviceIdType.MESH
ANY = pl.BlockSpec(memory_space=pl.ANY)


def _all_gather(blocks, name):
    n = len(blocks)

    def body(*refs):
        x_refs, out_refs = refs[:n], refs[n:2 * n]
        send_sems, recv_sems, local_sems = refs[2 * n:]
        x, y, c = lax.axis_index("x"), lax.axis_index("y"), lax.axis_index("c")
        me, sibling = (x, y, c), (x, y, 1 - c)
        chips = [(1 - x, y), (x, 1 - y), (1 - x, 1 - y)]

        def slot(a, px, py, pc):
            return out_refs[a].at[4 * px + 2 * py + pc]

        def copy(a, k, blk, to, src=None):
            return pltpu.make_async_remote_copy(src_ref=slot(a, *blk) if src is None else src, dst_ref=slot(a, *blk),
                                                send_sem=send_sems.at[7 * a + k], recv_sem=recv_sems.at[7 * a + k],
                                                device_id=to, device_id_type=MESH)

        mine = [pltpu.make_async_copy(x_refs[a], slot(a, *me), local_sems.at[a]) for a in range(n)]
        for cp in mine:
            cp.start()
        first = []
        for j, chip in enumerate(chips):
            first += [copy(a, 1 + j, me, (*chip, c), src=x_refs[a]) for a in range(n)]
        first += [copy(a, 0, me, sibling, src=x_refs[a]) for a in range(n)]
        for cp in first:
            cp.start()
        passed = []
        for j, chip in enumerate(chips):
            for a in range(n):
                copy(a, 1 + j, (*chip, c), me).wait_recv()
                passed.append(copy(a, 4 + j, (*chip, c), sibling))
                passed[-1].start()
        for a in range(n):
            copy(a, 0, sibling, me).wait_recv()
        for j, chip in enumerate(chips):
            for a in range(n):
                copy(a, 4 + j, (*chip, 1 - c), me).wait_recv()
        for cp in first + passed:
            cp.wait_send()
        for cp in mine:
            cp.wait()

    return pl.pallas_call(
        body, name=name, out_shape=[jax.ShapeDtypeStruct((N_DEV,) + b.shape, b.dtype) for b in blocks],
        in_specs=[ANY] * n, out_specs=[ANY] * n,
        scratch_shapes=[pltpu.SemaphoreType.DMA((7 * n,)), pltpu.SemaphoreType.DMA((7 * n,)), pltpu.SemaphoreType.DMA((n,))],
    )(*blocks)


def _place():
    x, y, c = lax.axis_index("x"), lax.axis_index("y"), lax.axis_index("c")
    return x, y, c, [(1 - x, y), (x, 1 - y), (1 - x, 1 - y)]


def _remote(src, dst, send, recv, k, to):
    return pltpu.make_async_remote_copy(src_ref=src, dst_ref=dst, send_sem=send.at[k], recv_sem=recv.at[k], device_id=to,
                                        device_id_type=MESH)


def _plan_gather_ici(blocks):
    n = len(blocks)

    def copies(in_refs, out_refs, send, recv, local):
        x, y, c, chips = _place()
        mine = 4 * x + 2 * y + c
        loc = [pltpu.make_async_copy(in_refs[a], out_refs[a].at[mine], local.at[a]) for a in range(n)]
        rem = [_remote(in_refs[a], out_refs[a].at[mine], send, recv, 3 * a + j, (px, py, c))
               for j, (px, py) in enumerate(chips) for a in range(n)]
        return rem, loc

    return _Plan(blocks, [jax.ShapeDtypeStruct((N_DEV,) + b.shape, b.dtype) for b in blocks], 3 * n, n, copies)


def _plan_gather_d2d(gathered):
    n = len(gathered)

    def copies(in_refs, out_refs, send, recv, local):
        x, y, c, _ = _place()
        rem = [_remote(in_refs[a].at[2 * j + c], out_refs[a].at[2 * j + c], send, recv, 4 * a + j, (x, y, 1 - c))
               for a in range(n) for j in range(4)]
        return rem, []

    return _Plan(gathered, [jax.ShapeDtypeStruct(g.shape, g.dtype) for g in gathered], 4 * n, 0, copies,
                 aliases={a: a for a in range(n)})


def _plan_reduce_sibling(parts):
    n = len(parts)

    def copies(in_refs, out_refs, send, recv, local):
        x, y, c, _ = _place()
        rem = [_remote(in_refs[a].at[2 * j + (1 - c)], out_refs[a].at[j], send, recv, 4 * a + j, (x, y, 1 - c))
               for a in range(n) for j in range(4)]
        return rem, []

    return _Plan(parts, [jax.ShapeDtypeStruct((4,) + p.shape[1:], p.dtype) for p in parts], 4 * n, 0, copies)


def _plan_reduce_chips(sums):
    n = len(sums)

    def copies(in_refs, out_refs, send, recv, local):
        x, y, c, chips = _place()
        mine = 2 * x + y
        loc = [pltpu.make_async_copy(in_refs[a].at[mine], out_refs[a].at[mine], local.at[a]) for a in range(n)]
        rem = [_remote(in_refs[a].at[2 * px + py], out_refs[a].at[mine], send, recv, 3 * a + k, (px, py, c))
               for k, (px, py) in enumerate(chips) for a in range(n)]
        return rem, loc

    return _Plan(sums, [jax.ShapeDtypeStruct(p.shape, p.dtype) for p in sums], 3 * n, n, copies)


def _add_sibling(parts, got, name):
    _, r, cc = parts.shape
    tr = _pick(r, (512, 256, 128, 64, 32, 16))
    c = lax.axis_index("c")

    def body(c_ref, p_ref, g_ref, o_ref):
        o_ref[...] = (p_ref[...] + g_ref[...]).astype(BF16)

    return pl.pallas_call(
        body, name=name, out_shape=jax.ShapeDtypeStruct((4, r, cc), BF16),
        grid_spec=pltpu.PrefetchScalarGridSpec(
            num_scalar_prefetch=1, grid=(4, r // tr),
            in_specs=[pl.BlockSpec((1, tr, cc), lambda j, i, cr: (2 * j + cr[0], i, 0)),
                      pl.BlockSpec((1, tr, cc), lambda j, i, cr: (j, i, 0))],
            out_specs=pl.BlockSpec((1, tr, cc), lambda j, i, cr: (j, i, 0))),
        compiler_params=_vmem(48),
    )(c.reshape(1).astype(jnp.int32), parts, got)


def _sum_leading(parts, name):
    k, r, cc = parts.shape
    tr = _pick(r, (512, 256, 128, 64, 32, 16, 8))

    def body(p_ref, o_ref):
        acc = p_ref[0]
        for j in range(1, k):
            acc = acc + p_ref[j]
        o_ref[...] = acc

    return pl.pallas_call(
        body, name=name, out_shape=jax.ShapeDtypeStruct((r, cc), F32), grid=(r // tr,),
        in_specs=[pl.BlockSpec((k, tr, cc), lambda i: (0, i, 0))], out_specs=pl.BlockSpec((tr, cc), lambda i: (i, 0)),
    )(parts)


def _adamw_math(w, g, m, v):
    nm = ADAM_B1 * m + (1.0 - ADAM_B1) * g
    nv = ADAM_B2 * v + (1.0 - ADAM_B2) * jnp.square(g)
    m_hat = nm / (1.0 - ADAM_B1 ** ADAM_STEP)
    v_hat = nv / (1.0 - ADAM_B2 ** ADAM_STEP)
    return -ADAM_LR * (m_hat / (jnp.sqrt(v_hat) + ADAM_EPS) + ADAM_WD * w), nm, nv


def _adamw_sum(w, contribs, m, v, name, ride=None):
    nl = len(contribs)
    k, r, cc = contribs[0].shape
    tr = _pick(r, (256, 128, 64, 32, 16))
    nb = r // tr

    def body(w_ref, *rest):
        c_refs = rest[:nl]
        m_ref, v_ref, g_ref, d_ref, nm_ref, nv_ref = rest[nl:]
        for li in range(nl):
            @pl.when(pl.program_id(0) == li)
            def _(c_ref=c_refs[li]):
                g = c_ref[0].astype(F32)
                for j in range(1, k):
                    g = g + c_ref[j].astype(F32)
                g_ref[...] = g
                d_ref[...], nm_ref[...], nv_ref[...] = _adamw_math(w_ref[...], g, m_ref[...], v_ref[...])

    spec = pl.BlockSpec((tr, cc), lambda l, i: (l * nb + i, 0))
    cspec = pl.BlockSpec((k, tr, cc), lambda l, i: (0, i, 0))
    return _call_with_ride(body, name, (nl, nb), [w, *contribs, m, v], [spec] + [cspec] * nl + [spec, spec],
                           [jax.ShapeDtypeStruct(w.shape, F32)] * 4, [spec] * 4, ("parallel", "parallel"), 48, ride)


def _adamw_many(ws, gs, ms, vs, name):
    n = len(ws)

    def body(*refs):
        w_r, g_r, m_r, v_r = refs[:n], refs[n:2 * n], refs[2 * n:3 * n], refs[3 * n:4 * n]
        d_r, nm_r, nv_r = refs[4 * n:5 * n], refs[5 * n:6 * n], refs[6 * n:7 * n]
        for a in range(n):
            d_r[a][...], nm_r[a][...], nv_r[a][...] = _adamw_math(w_r[a][...], g_r[a][...], m_r[a][...], v_r[a][...])

    shapes = [jax.ShapeDtypeStruct(w.shape, F32) for w in ws]
    res = pl.pallas_call(body, name=name, out_shape=shapes * 3,
                         compiler_params=pltpu.CompilerParams(vmem_limit_bytes=56 * 1024 * 1024))(*ws, *gs, *ms, *vs)
    return res[:n], res[n:2 * n], res[2 * n:]


def _pack_rows(flat, lanes, row_mult):
    n = flat.shape[-1]
    rows = -(-n // lanes)
    rows = -(-rows // row_mult) * row_mult
    pad = rows * lanes - n
    if pad:
        flat = jnp.pad(flat, [(0, 0)] * (flat.ndim - 1) + [(0, pad)])
    return flat.reshape(flat.shape[:-1] + (rows, lanes))


def _rope_tables(positions):
    inv = 1.0 / (ROPE_THETA ** (jnp.arange(0, MLA_ROPE, 2, dtype=F32) / MLA_ROPE))
    ang = positions.astype(F32)[:, None] * inv
    cos, sin = jnp.cos(ang), jnp.sin(ang)
    s = positions.shape[0]
    z = lambda n: jnp.zeros((s, n), F32)
    c = jnp.concatenate([jnp.ones((s, 64), F32), cos, cos, z(32)], axis=1)
    sa = jnp.concatenate([z(64), -sin, z(48)], axis=1)
    sb = jnp.concatenate([z(80), sin, z(32)], axis=1)
    return c, sa, sb


def _full_weights(gathered, names=SHARDED):
    full = {}
    for k, g in zip(names, gathered):
        _, r, c = g.shape
        if k == "w_in":
            full[k] = g
        else:
            full[k] = g.transpose(1, 0, 2).reshape(r, N_DEV * c) if k in COL_SHARDED else g.reshape(N_DEV * r, c)
    return full


EARLY = ("s5_w_glu", "w_branch_out", "w_out")
LATE = ("w_in", "mla_w_q_up", "mla_w_kv_up")


def _owner_major(g, names, tag):
    parts = []
    for k in names:
        if k == "w_in":
            parts.append(_win_unpad(g["w_in"], f"{tag}_w_in_unpad"))
            continue
        big = {"mla_w_q_up": lambda: _unpad_wq(g["wq"]), "mla_w_kv_up": lambda: _unpad_wkv(g["wkv"]), "s5_w_glu": lambda: g["wglu"],
               "w_branch_out": lambda: g["wo"], "w_out": lambda: g["wout"]}[k]()
        r, c = big.shape
        if k in COL_SHARDED:
            parts.append(big.reshape(r, N_DEV, c // N_DEV).transpose(1, 0, 2))
        else:
            parts.append(big.reshape(N_DEV, r // N_DEV, c))
    return parts


def _device_step(x, positions, target, shards, small):
    tabs = _rope_tables(positions)
    box = {}
    q0 = _layer_params(0, _full_weights(_all_gather(shards[0][:3], "gather_weights_l0"), LATE), small)

    def arrived(o):
        box.update(w1_b=o[:5])
        q0.update(_late_weights(_full_weights(o[5:], EARLY)))

    rides = {
        "mla_attn_fwd": _Ride(lambda: _plan_gather_ici(shards[1][:1]), lambda o: box.update(ici_a=o)),
        "fox_attn_fwd": _Ride(lambda: _join([_plan_gather_ici(shards[1][1:]), _plan_gather_d2d(box["ici_a"]),
                                             _plan_gather_ici(shards[0][3:])]),
                              lambda o: box.update(ici_b=o[:5], w1_a=o[5:6], ici_0=o[6:])),
        "s5_scan_fwd": _Ride(lambda: _join([_plan_gather_d2d(box["ici_b"]), _plan_gather_d2d(box["ici_0"])]), arrived),
    }
    h, sv0 = _layer_fwd(0, x, tabs, q0, rides)
    q1 = _layer_params(1, _full_weights(list(box["w1_a"]) + list(box["w1_b"])), small)
    h, sv1 = _layer_fwd(1, h, tabs, q1)
    loss, d = _loss_head(h, target, "loss_head")
    d, g1 = _layer_bwd(1, d, tabs, q1, sv1)
    parts1 = _owner_major(g1, SHARDED, "l1")

    def chips_plan(names, parts, got, tag):
        return _plan_reduce_chips([_add_sibling(p, g, f"reduce_add_{tag}_{k}") for k, p, g in zip(names, parts, got)])

    g0 = {}
    rides = {
        "s5_scan_bwd": _Ride(lambda: _join([_plan_reduce_sibling(parts1),
                                            _plan_reduce_sibling(box.setdefault("early0", _owner_major(g0, EARLY, "l0")))]),
                             lambda o: box.update(got1=o[:6], got0e=o[6:])),
        "fox_attn_bwd": _Ride(lambda: chips_plan(SHARDED, parts1, box["got1"], "l1"), lambda o: box.update(contribs1=o)),
        "mla_attn_bwd": _Ride(lambda: chips_plan(EARLY, box["early0"], box["got0e"], "l0"), lambda o: box.update(contribs0e=o)),
        "dh1": _Ride(lambda: _plan_reduce_sibling(box.setdefault("late0", _owner_major(g0, LATE, "l0"))),
                     lambda o: box.update(got0l=o)),
        "dh3": _Ride(lambda: chips_plan(LATE, box["late0"], box["got0l"], "l0"), lambda o: box.update(contribs0l=o)),
    }
    d, _ = _layer_bwd(0, d, tabs, q0, sv0, rides, g0)
    contribs0 = list(box["contribs0l"]) + list(box["contribs0e"])
    return loss[0, 0], d, [g0, g1], [contribs0, box["contribs1"]]


def kernel(x, positions, norm_g, w_in, mla_q_a_norm, mla_w_q_up, mla_kv_a_norm, mla_w_kv_up, mla_q_norm, mla_k_norm, fox_b_f, fox_q_norm, fox_k_norm, s5_lambda_re, s5_lambda_im, s5_log_dt, s5_b_re, s5_b_im, s5_c_re, s5_c_im, s5_d, s5_w_glu, s5_b_glu, w_branch_out, w_out, loss_target, m_norm_g, m_w_in, m_mla_q_a_norm, m_mla_w_q_up, m_mla_kv_a_norm, m_mla_w_kv_up, m_mla_q_norm, m_mla_k_norm, m_fox_b_f, m_fox_q_norm, m_fox_k_norm, m_s5_lambda_re, m_s5_lambda_im, m_s5_log_dt, m_s5_b_re, m_s5_b_im, m_s5_c_re, m_s5_c_im, m_s5_d, m_s5_w_glu, m_s5_b_glu, m_w_branch_out, m_w_out, v_norm_g, v_w_in, v_mla_q_a_norm, v_mla_w_q_up, v_mla_kv_a_norm, v_mla_w_kv_up, v_mla_q_norm, v_mla_k_norm, v_fox_b_f, v_fox_q_norm, v_fox_k_norm, v_s5_lambda_re, v_s5_lambda_im, v_s5_log_dt, v_s5_b_re, v_s5_b_im, v_s5_c_re, v_s5_c_im, v_s5_d, v_s5_w_glu, v_s5_b_glu, v_w_branch_out, v_w_out):
    env = dict(locals())
    wts = {k: env[k] for k in WEIGHTS}
    mom = {k: env["m_" + k] for k in WEIGHTS}
    var = {k: env["v_" + k] for k in WEIGHTS}

    shards = [[wts[k][l].astype(BF16) for k in SHARDED] for l in range(DEPTH)]
    small = {k: wts[k] for k in SMALL}
    loss, dx, grads, contribs = _device_step(x[0], positions[0], loss_target[0], shards, small)
    loss = lax.psum(loss, ("x", "y", "c"))

    two_d = {k: (wts[k].shape[0] * wts[k].shape[1], wts[k].shape[2]) for k in SHARDED}
    sm = {k: jnp.stack([g[k] for g in grads]).reshape(wts[k].shape) for k in SMALL}
    small_block = _pack_rows(jnp.concatenate([sm[k].reshape(-1) for k in SMALL]), LANES, 256)
    box = {}
    rides = {SHARDED[0]: _Ride(lambda: _plan_gather_ici([small_block]), lambda o: box.update(ici=o)),
             SHARDED[1]: _Ride(lambda: _plan_gather_d2d(box["ici"]), lambda o: box.update(all=o))}
    grad_out, delta_out, m_out, v_out = {}, {}, {}, {}
    for i, k in enumerate(SHARDED):
        shp = wts[k].shape
        res = _adamw_sum(wts[k].reshape(two_d[k]), [contribs[l][i] for l in range(DEPTH)], mom[k].reshape(two_d[k]),
                         var[k].reshape(two_d[k]), f"adamw_{k}", ride=rides.get(k))
        grad_out[k], delta_out[k], m_out[k], v_out[k] = (z.reshape(shp) for z in res)
    g_small = _sum_leading(box["all"][0], "sum_small_grads").reshape(-1)
    off = 0
    for k in SMALL:
        cnt = int(np.prod(wts[k].shape))
        grad_out[k] = g_small[off:off + cnt].reshape(wts[k].shape)
        off += cnt
    flat2 = lambda a: a.reshape(-1, a.shape[-1])
    d_s, m_s, v_s = _adamw_many([flat2(wts[k]) for k in SMALL], [flat2(grad_out[k]) for k in SMALL],
                                [flat2(mom[k]) for k in SMALL], [flat2(var[k]) for k in SMALL], "adamw_small")
    for i, k in enumerate(SMALL):
        delta_out[k], m_out[k], v_out[k] = (z[i].reshape(wts[k].shape) for z in (d_s, m_s, v_s))

    return (loss, dx[None], *[grad_out[k] for k in WEIGHTS], *[delta_out[k] for k in WEIGHTS],
            *[m_out[k] for k in WEIGHTS], *[v_out[k] for k in WEIGHTS])
```

```python
import functools
import math

import jax
import jax.numpy as jnp
import numpy as np
from jax import lax
from jax.experimental import pallas as pl
from jax.experimental.pallas import tpu as pltpu

F32 = jnp.float32
BF16 = jnp.bfloat16

D_MODEL = 1024
DEPTH = 2
CHUNK = 64
EPS = 1e-6
HEADS = 8
MLA_NOPE, MLA_ROPE, MLA_V = 64, 32, 64
MLA_Q_RANK, MLA_KV_RANK = 256, 128
MLA_QK = MLA_NOPE + MLA_ROPE
ROPE_THETA = 10000.0
FOX_DIM = 64
S5_WIDTH, S5_GROUP, S5_GROUPS, S5_STATE = 512, 16, 32, 64
S5_LANES = S5_GROUPS * S5_STATE
IN_WIDTH = 7080
N_DEV = 8
LANES = 128
SUBLANES = 8

ADAM_LR, ADAM_B1, ADAM_B2, ADAM_EPS, ADAM_WD, ADAM_STEP = 0.001, 0.9, 0.999, 1e-08, 0.01, 10

SEG_W = (512, 1536, 512, 4608)
SEG_OFF = (0, 512, 2048, 2560)
NSEG = len(SEG_W)
GATES_W = 1536
PAD_IN = 7168
NEG = -1e30

SHARDED = ("w_in", "mla_w_q_up", "mla_w_kv_up", "s5_w_glu", "w_branch_out", "w_out")
COL_SHARDED = ("w_in", "mla_w_q_up", "mla_w_kv_up")
SMALL = ("norm_g", "mla_q_a_norm", "mla_kv_a_norm", "mla_q_norm", "mla_k_norm", "fox_b_f", "fox_q_norm", "fox_k_norm",
         "s5_lambda_re", "s5_lambda_im", "s5_log_dt", "s5_b_re", "s5_b_im", "s5_c_re", "s5_c_im", "s5_d", "s5_b_glu")
WEIGHTS = ("norm_g", "w_in", "mla_q_a_norm", "mla_w_q_up", "mla_kv_a_norm", "mla_w_kv_up", "mla_q_norm", "mla_k_norm",
           "fox_b_f", "fox_q_norm", "fox_k_norm", "s5_lambda_re", "s5_lambda_im", "s5_log_dt", "s5_b_re", "s5_b_im",
           "s5_c_re", "s5_c_im", "s5_d", "s5_w_glu", "s5_b_glu", "w_branch_out", "w_out")


def _pick(n, cands):
    for c in cands:
        if n % c == 0:
            return c
    return n


def _vmem(mb):
    return pltpu.CompilerParams(vmem_limit_bytes=mb * 1024 * 1024)


def _dot(a, b, dims):
    return lax.dot_general(a.astype(BF16), b.astype(BF16), (dims, ((), ())), preferred_element_type=F32)


def _nn(a, b):
    return _dot(a, b, ((1,), (0,)))


def _nt(a, b):
    return _dot(a, b, ((1,), (1,)))


def _tn(a, b):
    return _dot(a, b, ((0,), (0,)))


def _rms(x, g, n=None):
    n = x.shape[-1] if n is None else n
    return x * lax.rsqrt(jnp.sum(x * x, axis=-1, keepdims=True) / n + EPS) * g


def _rope(t, c, sa, sb):
    return t * c + pltpu.roll(t, LANES - 16, 1) * sa + pltpu.roll(t, 16, 1) * sb


def _rope_t(d, c, sa, sb):
    return d * c + pltpu.roll(d * sa, 16, 1) + pltpu.roll(d * sb, LANES - 16, 1)


def _mm(a, b, mode, name, acc=None, b_cols=None, ride=None):
    if mode == "tn":
        kd, m = a.shape
    else:
        m, kd = a.shape
    b_off, b_w = b_cols if b_cols is not None else (0, b.shape[1])
    n = b.shape[0] if mode == "nt" else b_w
    tm, tn, tk = _pick(m, (1024, 512, 256, 128)), _pick(n, (1024, 512, 256, 128)), _pick(kd, (1024, 512, 256, 128))
    nk = kd // tk
    if mode == "tn":
        a_spec = pl.BlockSpec((tk, tm), lambda i, j, k: (k, i))
    else:
        a_spec = pl.BlockSpec((tm, tk), lambda i, j, k: (i, k))
    if mode == "nt":
        assert b_off % tk == 0
        b_spec = pl.BlockSpec((tn, tk), lambda i, j, k: (j, k + b_off // tk))
    else:
        assert b_off % tn == 0
        b_spec = pl.BlockSpec((tk, tn), lambda i, j, k: (k, j + b_off // tn))
    dims = {"nn": ((1,), (0,)), "nt": ((1,), (1,)), "tn": ((0,), (0,))}[mode]
    o_spec = pl.BlockSpec((tm, tn), lambda i, j, k: (i, j))
    has_acc = acc is not None

    def body(*refs):
        if has_acc:
            a_ref, b_ref, c_ref, o_ref = refs
        else:
            a_ref, b_ref, o_ref = refs
        k = pl.program_id(2)
        prod = _dot(a_ref[...], b_ref[...], dims)

        @pl.when(k == 0)
        def _():
            o_ref[...] = prod + c_ref[...] if has_acc else prod

        @pl.when(k > 0)
        def _():
            o_ref[...] += prod

    ins = [a, b] + ([acc] if has_acc else [])
    in_specs = [a_spec, b_spec] + ([o_spec] if has_acc else [])
    return _call_with_ride(body, name, (m // tm, n // tn, nk), ins, in_specs, [jax.ShapeDtypeStruct((m, n), F32)], [o_spec],
                           ("parallel", "parallel", "arbitrary"), 48, ride)[0]


def _stage(name, fn, n_steps, ins, outs, accs=(), scratch=(), vmem_mb=48, acc_in_scratch=False):
    n_in, n_out, n_acc, n_scr = len(ins), len(outs), len(accs), len(scratch)

    def body(*refs):
        in_refs = refs[:n_in]
        out_refs = refs[n_in:n_in + n_out]
        acc_out = refs[n_in + n_out:n_in + n_out + n_acc]
        scr = refs[n_in + n_out + n_acc:n_in + n_out + n_acc + n_scr]
        acc_refs = refs[n_in + n_out + n_acc + n_scr:] if acc_in_scratch else acc_out
        if n_acc:
            @pl.when(pl.program_id(0) == 0)
            def _():
                for r in acc_refs:
                    r[...] = jnp.zeros(r.shape, r.dtype)
        fn(in_refs, out_refs, acc_refs, scr)
        if acc_in_scratch:
            @pl.when(pl.program_id(0) == n_steps - 1)
            def _():
                for src, dst in zip(acc_refs, acc_out):
                    pltpu.sync_copy(src, dst)

    if acc_in_scratch:
        acc_specs = [pl.BlockSpec(memory_space=pl.ANY) for _ in accs]
        scratch = list(scratch) + [pltpu.VMEM(a.shape, a.dtype) for a in accs]
    else:
        acc_specs = [pl.BlockSpec(a.shape, functools.partial(lambda i, nd: (0,) * nd, nd=len(a.shape))) for a in accs]
    res = pl.pallas_call(
        body, name=name, grid=(n_steps,),
        in_specs=[s for _, s in ins], out_specs=[s for _, s in outs] + acc_specs,
        out_shape=[s for s, _ in outs] + list(accs), scratch_shapes=list(scratch),
        compiler_params=pltpu.CompilerParams(dimension_semantics=("arbitrary",),
                                             vmem_limit_bytes=vmem_mb * 1024 * 1024),
    )(*[a for a, _ in ins])
    return res


def _rows(ts, w, j=0):
    return pl.BlockSpec((ts, w), lambda i: (i, j))


def _rows_rev(ts, w, n, j=0):
    return pl.BlockSpec((ts, w), lambda i: (n - 1 - i, j))


def _heads(ts, d):
    return pl.BlockSpec((HEADS, ts, d), lambda i: (0, i, 0))


def _heads_rev(ts, d, n):
    return pl.BlockSpec((HEADS, ts, d), lambda i: (0, n - 1 - i, 0))


def _full(shape):
    nd = len(shape)
    return pl.BlockSpec(tuple(shape), lambda i: (0,) * nd)


def _sds(shape, dtype=F32):
    return jax.ShapeDtypeStruct(tuple(shape), dtype)


def _norm_fwd(x, g, name):
    s = x.shape[0]
    ts = _pick(s, (256, 128))

    def fn(ins, outs, accs, scr):
        outs[0][...] = _rms(ins[0][...], ins[1][...]).astype(BF16)

    return _stage(name, fn, s // ts, [(x, _rows(ts, D_MODEL)), (g, _full(g.shape))],
                  [(_sds((s, D_MODEL), BF16), _rows(ts, D_MODEL))])[0]


def _norm_bwd(x, g, dh, dres, name):
    s = x.shape[0]
    ts = _pick(s, (256, 128))

    def fn(ins, outs, accs, scr):
        _, vjp = jax.vjp(_rms, ins[0][...], ins[1][...])
        dx, dg = vjp(ins[2][...])
        outs[0][...] = dx + ins[3][...]
        accs[0][...] += dg

    r = _stage(name, fn, s // ts,
               [(x, _rows(ts, D_MODEL)), (g, _full(g.shape)), (dh, _rows(ts, D_MODEL)), (dres, _rows(ts, D_MODEL))],
               [(_sds((s, D_MODEL)), _rows(ts, D_MODEL))], accs=[_sds((1, D_MODEL))])
    return r[0], r[1]


def _mla_q(qraw, c, sa, sb, qn):
    return _rms(_rope(qraw, c, sa, sb), qn, MLA_QK)


def _mla_prep_fwd(seg0, tabs, p, wq, wkv, name):
    s = seg0.shape[0]
    ts = _pick(s, (256, 128))

    def fn(ins, outs, accs, scr):
        blk, cos, sa, sb, qan, kvan, qn, kn, wq_r, wkv_r = ins
        b = blk[...]
        cq, ckv, kt = b[:, :256], b[:, 256:384], b[:, 384:512]
        lane = lax.broadcasted_iota(jnp.int32, kt.shape, 1)
        kpe = jnp.where(lane >= 64, kt, 0.0)
        q_raw = _nn(_rms(cq, qan[...]), wq_r[...])
        kv_raw = _nn(_rms(ckv, kvan[...]), wkv_r[...])
        c, a, bb = cos[...], sa[...], sb[...]
        for h in range(HEADS):
            outs[0][h] = _mla_q(q_raw[:, LANES * h:LANES * (h + 1)], c, a, bb, qn[...]).astype(BF16)
            outs[1][h] = _mla_q(kv_raw[:, LANES * h:LANES * (h + 1)] + kpe, c, a, bb, kn[...]).astype(BF16)
            outs[2][h] = kv_raw[:, 1024 + 64 * h:1024 + 64 * (h + 1)].astype(BF16)

    consts = [p["mla_q_a_norm"], p["mla_kv_a_norm"], p["mla_q_norm"], p["mla_k_norm"], wq, wkv]
    return _stage(name, fn, s // ts,
                  [(seg0, _rows(ts, 512))] + [(t, _rows(ts, LANES)) for t in tabs] + [(a, _full(a.shape)) for a in consts],
                  [(_sds((HEADS, s, LANES), BF16), _heads(ts, LANES)), (_sds((HEADS, s, LANES), BF16), _heads(ts, LANES)),
                   (_sds((HEADS, s, 64), BF16), _heads(ts, 64))])


def _mla_prep_bwd(seg0, tabs, p, wq, wkv, dq, dk, dv, dff, name):
    s = seg0.shape[0]
    ts = _pick(s, (256, 128))

    def fn(ins, outs, accs, scr):
        blk, cos, sa, sb, qan, kvan, qn, kn, wq_r, wkv_r, dq_r, dk_r, dv_r, dff_r = ins
        dqan, dkvan, dqn, dkn, dwq, dwkv = accs
        dqraw_s, dkvraw_s = scr
        b = blk[...]
        cq, ckv, kt = b[:, :256], b[:, 256:384], b[:, 384:512]
        lane = lax.broadcasted_iota(jnp.int32, kt.shape, 1)
        kpe = jnp.where(lane >= 64, kt, 0.0)
        cqn, vjp_cq = jax.vjp(_rms, cq, qan[...])
        ckvn, vjp_ckv = jax.vjp(_rms, ckv, kvan[...])
        q_raw = _nn(cqn, wq_r[...])
        kv_raw = _nn(ckvn, wkv_r[...])
        c, a, bb = cos[...], sa[...], sb[...]

        def head_bwd(raw, gain, d):
            t = _rope(raw, c, a, bb)
            _, vjp = jax.vjp(functools.partial(_rms, n=MLA_QK), t, gain)
            dt, dgain = vjp(d)
            return _rope_t(dt, c, a, bb), dgain

        dkpe = jnp.zeros(kt.shape, F32)
        for h in range(HEADS):
            dqh, dg = head_bwd(q_raw[:, LANES * h:LANES * (h + 1)], qn[...], dq_r[h])
            dqn[...] += dg
            dqraw_s[:, LANES * h:LANES * (h + 1)] = dqh
            dkh, dg = head_bwd(kv_raw[:, LANES * h:LANES * (h + 1)] + kpe, kn[...], dk_r[h])
            dkn[...] += dg
            dkvraw_s[:, LANES * h:LANES * (h + 1)] = dkh
            dkpe = dkpe + dkh
            dkvraw_s[:, 1024 + 64 * h:1024 + 64 * (h + 1)] = dv_r[h]
        dq_raw = dqraw_s[...]
        dkv_raw = dkvraw_s[...]
        dwq[...] += _tn(cqn, dq_raw)
        dwkv[...] += _tn(ckvn, dkv_raw)
        dcq, dg = vjp_cq(_nt(dq_raw, wq_r[...]))
        dqan[...] += dg
        dckv, dg = vjp_ckv(_nt(dkv_raw, wkv_r[...]))
        dkvan[...] += dg
        outs[0][:, 0:256] = dcq.astype(BF16)
        outs[0][:, 256:384] = dckv.astype(BF16)
        outs[0][:, 384:512] = (jnp.where(lane >= 64, dkpe, 0.0) + dff_r[...]).astype(BF16)

    consts = [p["mla_q_a_norm"], p["mla_kv_a_norm"], p["mla_q_norm"], p["mla_k_norm"], wq, wkv]
    return _stage(name, fn, s // ts,
                  [(seg0, _rows(ts, 512))] + [(t, _rows(ts, LANES)) for t in tabs] + [(a, _full(a.shape)) for a in consts]
                  + [(dq, _heads(ts, LANES)), (dk, _heads(ts, LANES)), (dv, _heads(ts, 64)), (dff, _rows(ts, LANES))],
                  [(_sds((s, 512), BF16), _rows(ts, 512))],
                  accs=[_sds((1, 256)), _sds((1, 128)), _sds((1, LANES)), _sds((1, LANES)), _sds(wq.shape), _sds(wkv.shape)],
                  scratch=[pltpu.VMEM((ts, 1024), F32), pltpu.VMEM((ts, 1536), F32)])


def _fox_prep_fwd(seg0, seg1, bf, qn, kn, name):
    s = seg0.shape[0]
    ts = _pick(s, (256, 128))
    steps = int(math.log2(ts))

    def fn(ins, outs, accs, scr):
        kt_r, x_r, bf_r, qn_r, kn_r = ins
        carry = scr[0]

        @pl.when(pl.program_id(0) == 0)
        def _():
            carry[...] = jnp.zeros(carry.shape, F32)

        x = x_r[...]
        for h in range(HEADS):
            outs[0][h] = _rms(x[:, 64 * h:64 * (h + 1)], qn_r[...]).astype(BF16)
            outs[1][h] = _rms(x[:, 512 + 64 * h:512 + 64 * (h + 1)], kn_r[...]).astype(BF16)
            outs[2][h] = x[:, 1024 + 64 * h:1024 + 64 * (h + 1)].astype(BF16)
        kt = kt_r[...]
        lane = lax.broadcasted_iota(jnp.int32, kt.shape, 1)
        row = lax.broadcasted_iota(jnp.int32, kt.shape, 0)
        cs = jnp.where(lane < HEADS, jax.nn.log_sigmoid(kt + bf_r[...]), 0.0)
        for k in range(steps):
            sh = 1 << k
            cs = cs + jnp.where(row >= sh, pltpu.roll(cs, sh, 0), 0.0)
        cs = cs + carry[0:1, :]
        outs[3][...] = cs
        outs[4][...] = cs.T[0:HEADS, :]
        carry[0:1, :] = cs[ts - 1:ts, :]

    return _stage(name, fn, s // ts,
                  [(seg0, _rows(ts, LANES, 3)), (seg1, _rows(ts, 1536)), (bf, _full(bf.shape)), (qn, _full(qn.shape)),
                   (kn, _full(kn.shape))],
                  [(_sds((HEADS, s, 64), BF16), _heads(ts, 64)), (_sds((HEADS, s, 64), BF16), _heads(ts, 64)),
                   (_sds((HEADS, s, 64), BF16), _heads(ts, 64)), (_sds((s, LANES)), _rows(ts, LANES)),
                   (_sds((HEADS, s)), pl.BlockSpec((HEADS, ts), lambda i: (0, i)))],
                  scratch=[pltpu.VMEM((SUBLANES, LANES), F32)])


def _fox_prep_bwd(seg0, seg1, bf, qn, kn, dq, dk, dv, dck, name):
    s = seg0.shape[0]
    ts = _pick(s, (256, 128))
    n = s // ts
    steps = int(math.log2(ts))

    def fn(ins, outs, accs, scr):
        kt_r, x_r, bf_r, qn_r, kn_r, dq_r, dk_r, dv_r, dck_r = ins
        dqn, dkn, dbf = accs
        carry, dbuf = scr

        @pl.when(pl.program_id(0) == 0)
        def _():
            carry[...] = jnp.zeros(carry.shape, F32)

        x = x_r[...]
        for h in range(HEADS):
            _, vjp = jax.vjp(_rms, x[:, 64 * h:64 * (h + 1)], qn_r[...])
            d, dg = vjp(dq_r[h])
            dbuf[:, 64 * h:64 * (h + 1)] = d
            dqn[...] += dg
            _, vjp = jax.vjp(_rms, x[:, 512 + 64 * h:512 + 64 * (h + 1)], kn_r[...])
            d, dg = vjp(dk_r[h])
            dbuf[:, 512 + 64 * h:512 + 64 * (h + 1)] = d
            dkn[...] += dg
            dbuf[:, 1024 + 64 * h:1024 + 64 * (h + 1)] = dv_r[h]
        outs[0][...] = dbuf[...].astype(BF16)
        dc = dck_r[...].reshape(HEADS, ts)
        dc = jnp.concatenate([dc, jnp.zeros((LANES - HEADS, ts), F32)], axis=0).T
        row = lax.broadcasted_iota(jnp.int32, dc.shape, 0)
        lane = lax.broadcasted_iota(jnp.int32, dc.shape, 1)
        for k in range(steps):
            sh = 1 << k
            dc = dc + jnp.where(row < ts - sh, pltpu.roll(dc, ts - sh, 0), 0.0)
        dc = dc + carry[0:1, :]
        carry[0:1, :] = dc[0:1, :]
        dff = jnp.where(lane < HEADS, dc * jax.nn.sigmoid(-(kt_r[...] + bf_r[...])), 0.0)
        outs[1][...] = dff
        dbf[...] += jnp.sum(dff, axis=0, keepdims=True)

    return _stage(name, fn, n,
                  [(seg0, _rows_rev(ts, LANES, n, 3)), (seg1, _rows_rev(ts, 1536, n)), (bf, _full(bf.shape)),
                   (qn, _full(qn.shape)), (kn, _full(kn.shape)), (dq, _heads_rev(ts, 64, n)), (dk, _heads_rev(ts, 64, n)),
                   (dv, _heads_rev(ts, 64, n)), (dck, pl.BlockSpec((HEADS, 1, ts), lambda i: (0, 0, n - 1 - i)))],
                  [(_sds((s, 1536), BF16), _rows_rev(ts, 1536, n)), (_sds((s, LANES)), _rows_rev(ts, LANES, n))],
                  accs=[_sds((1, 64)), _sds((1, 64)), _sds((1, LANES))],
                  scratch=[pltpu.VMEM((SUBLANES, LANES), F32), pltpu.VMEM((ts, 1536), F32)])


def _allowed(i, j, t, chunk_causal):
    qpos = i * t + lax.broadcasted_iota(jnp.int32, (t, t), 0)
    kpos = j * t + lax.broadcasted_iota(jnp.int32, (t, t), 1)
    if chunk_causal:
        return (kpos // CHUNK) <= (qpos // CHUNK)
    return kpos <= qpos


def _pick_col(c_blk, h):
    lane = lax.broadcasted_iota(jnp.int32, c_blk.shape, 1)
    return jnp.sum(jnp.where(lane == h, c_blk, 0.0), axis=1, keepdims=True)


class _Ride:
    def __init__(self, make, take):
        self.make, self.take = make, take


class _Plan:
    def __init__(self, ins, out_shapes, n_remote, n_local, copies, aliases=None):
        self.ins, self.out_shapes, self.n_remote, self.n_local = list(ins), list(out_shapes), n_remote, n_local
        self.copies, self.aliases = copies, dict(aliases or {})

    def scratch(self):
        return [pltpu.SemaphoreType.DMA((self.n_remote,)), pltpu.SemaphoreType.DMA((self.n_remote,)),
                pltpu.SemaphoreType.DMA((max(self.n_local, 1),))]

    def start(self, in_refs, out_refs, sems):
        remote, local = self.copies(in_refs, out_refs, *sems)
        for cp in local + remote:
            cp.start()

    def wait(self, in_refs, out_refs, sems):
        remote, local = self.copies(in_refs, out_refs, *sems)
        for cp in remote:
            cp.wait()
        for cp in local:
            cp.wait()


class _Off:
    def __init__(self, ref, off):
        self.ref, self.off, self.at = ref, off, self

    def __getitem__(self, k):
        return self.ref.at[k + self.off]


def _join(plans):
    ins = [a for p in plans for a in p.ins]
    outs = [o for p in plans for o in p.out_shapes]
    aliases, i0, o0 = {}, 0, 0
    for p in plans:
        aliases.update({i0 + i: o0 + o for i, o in p.aliases.items()})
        i0, o0 = i0 + len(p.ins), o0 + len(p.out_shapes)

    def copies(in_refs, out_refs, send, recv, local):
        rem, loc, i0, o0, r0, l0 = [], [], 0, 0, 0, 0
        for p in plans:
            r, l = p.copies(in_refs[i0:i0 + len(p.ins)], out_refs[o0:o0 + len(p.out_shapes)], _Off(send, r0), _Off(recv, r0),
                            _Off(local, l0))
            rem, loc = rem + r, loc + l
            i0, o0, r0, l0 = i0 + len(p.ins), o0 + len(p.out_shapes), r0 + p.n_remote, l0 + p.n_local
        return rem, loc

    return _Plan(ins, outs, sum(p.n_remote for p in plans), sum(p.n_local for p in plans), copies, aliases)


def _call_with_ride(core, name, grid, ins, in_specs, out_shape, out_specs, semantics, vmem_mb, ride, scratch=()):
    n_in, n_out, n_scr = len(ins), len(out_shape), len(scratch)
    if ride is None:
        return pl.pallas_call(
            core, name=name, grid=grid, in_specs=in_specs, out_specs=out_specs, out_shape=out_shape,
            scratch_shapes=list(scratch),
            compiler_params=pltpu.CompilerParams(dimension_semantics=semantics, vmem_limit_bytes=vmem_mb * 1024 * 1024),
        )(*ins)
    plan = ride.make()
    ci, co = len(plan.ins), len(plan.out_shapes)

    def body(*refs):
        c_in = refs[n_in:n_in + ci]
        a_out = refs[n_in + ci:n_in + ci + n_out]
        c_out = refs[n_in + ci + n_out:n_in + ci + n_out + co]
        own = refs[n_in + ci + n_out + co:n_in + ci + n_out + co + n_scr]
        sems = refs[n_in + ci + n_out + co + n_scr:]
        ids = [pl.program_id(d) for d in range(len(grid))]
        first = functools.reduce(jnp.logical_and, [i == 0 for i in ids])
        last = functools.reduce(jnp.logical_and, [i == g - 1 for i, g in zip(ids, grid)])

        @pl.when(first)
        def _():
            plan.start(c_in, c_out, sems)

        core(*refs[:n_in], *a_out, *own)

        @pl.when(last)
        def _():
            plan.wait(c_in, c_out, sems)

    res = pl.pallas_call(
        body, name=name, grid=grid, in_specs=list(in_specs) + [ANY] * ci, out_specs=list(out_specs) + [ANY] * co,
        out_shape=list(out_shape) + plan.out_shapes, scratch_shapes=list(scratch) + plan.scratch(),
        input_output_aliases={n_in + i: n_out + o for i, o in plan.aliases.items()},
        compiler_params=pltpu.CompilerParams(dimension_semantics=("arbitrary",) * len(grid),
                                             vmem_limit_bytes=vmem_mb * 1024 * 1024),
    )(*ins, *plan.ins)
    ride.take(res[n_out:])
    return res[:n_out]


def _attn_fwd(q, k, v, scale, chunk_causal, name, c=None, ct=None, hps=8, ride=None):
    _, s, dk = q.shape
    dv = v.shape[2]
    t = _pick(s, (256, 128))
    n = s // t
    bias = c is not None

    def body(*refs):
        if bias:
            q_ref, k_ref, v_ref, c_ref, ct_ref, o_ref, lse_ref, s_s = refs
        else:
            q_ref, k_ref, v_ref, o_ref, lse_ref, s_s = refs
        hp, i = pl.program_id(0), pl.program_id(1)
        qb = [q_ref[e] for e in range(hps)]
        cq = [_pick_col(c_ref[...], hp * hps + e) if bias else None for e in range(hps)]

        def sweep1(j, m, diagonal):
            off = pl.multiple_of(j * t, t)
            out = []
            for e in range(hps):
                sc = _nt(qb[e], k_ref[e, pl.ds(off, t), :]) * scale
                if bias:
                    sc = sc + (cq[e] - ct_ref[pl.ds(hp * hps + e, 1), pl.ds(off, t)])
                if diagonal:
                    sc = jnp.where(_allowed(i, j, t, chunk_causal), sc, NEG)
                s_s[e, j] = sc
                out.append(jnp.maximum(m[e], jnp.max(sc, axis=1, keepdims=True)))
            return tuple(out)

        m0 = tuple(jnp.full((t, 1), NEG, F32) for _ in range(hps))
        m = sweep1(i, lax.fori_loop(0, i, functools.partial(sweep1, diagonal=False), m0), True)

        def sweep2(j, carry):
            off = pl.multiple_of(j * t, t)
            out = []
            for e in range(hps):
                l, acc = carry[e]
                pr = jnp.exp(s_s[e, j] - m[e])
                out.append((l + jnp.sum(pr, axis=1, keepdims=True), acc + _nn(pr, v_ref[e, pl.ds(off, t), :])))
            return tuple(out)

        init = tuple((jnp.zeros((t, 1), F32), jnp.zeros((t, dv), F32)) for _ in range(hps))
        res = lax.fori_loop(0, i + 1, sweep2, init)
        for e in range(hps):
            l, acc = res[e]
            o_ref[e] = acc / l
            lse_ref[e] = m[e] + jnp.log(l)

    ins = [q, k, v] + ([c, ct] if bias else [])
    in_specs = [pl.BlockSpec((hps, t, dk), lambda h, i: (h, i, 0)), pl.BlockSpec((hps, s, dk), lambda h, i: (h, 0, 0)),
                pl.BlockSpec((hps, s, dv), lambda h, i: (h, 0, 0))]
    if bias:
        in_specs += [pl.BlockSpec((t, LANES), lambda h, i: (i, 0)), pl.BlockSpec((HEADS, s), lambda h, i: (0, 0))]
    return _call_with_ride(
        body, name, (HEADS // hps, n), ins, in_specs, [_sds((HEADS, s, dv)), _sds((HEADS, s, 1))],
        [pl.BlockSpec((hps, t, dv), lambda h, i: (h, i, 0)), pl.BlockSpec((hps, t, 1), lambda h, i: (h, i, 0))],
        ("parallel", "parallel"), 56, ride, scratch=[pltpu.VMEM((hps, n, t, t), F32)])


def _attn_bwd_rows(q, k, v, do, lse, scale, chunk_causal, name, c=None, ct=None, hps=4, ride=None):
    _, s, dk = q.shape
    dv = v.shape[2]
    t = _pick(s, (256, 128))
    n = s // t
    bias = c is not None

    def body(*refs):
        if bias:
            q_ref, k_ref, v_ref, do_ref, lse_ref, c_ref, ct_ref, dq_ref, dk_ref, dv_ref, dck_ref, p_s, dp_s = refs
        else:
            q_ref, k_ref, v_ref, do_ref, lse_ref, dq_ref, dk_ref, dv_ref, p_s, dp_s = refs
        hp, i = pl.program_id(0), pl.program_id(1)

        @pl.when(i == 0)
        def _():
            dk_ref[...] = jnp.zeros(dk_ref.shape, F32)
            dv_ref[...] = jnp.zeros(dv_ref.shape, F32)
            if bias:
                dck_ref[...] = jnp.zeros(dck_ref.shape, F32)

        qb = [q_ref[e] for e in range(hps)]
        dob = [do_ref[e] for e in range(hps)]
        lse_b = [lse_ref[e] for e in range(hps)]
        cq = [_pick_col(c_ref[...], hp * hps + e) if bias else None for e in range(hps)]

        def sweep1(j, acc, diagonal):
            off = pl.multiple_of(j * t, t)
            out = []
            for e in range(hps):
                sc = _nt(qb[e], k_ref[e, pl.ds(off, t), :]) * scale
                if bias:
                    sc = sc + (cq[e] - ct_ref[pl.ds(hp * hps + e, 1), pl.ds(off, t)])
                pr = jnp.exp(sc - lse_b[e])
                if diagonal:
                    pr = jnp.where(_allowed(i, j, t, chunk_causal), pr, 0.0)
                dp = _nt(dob[e], v_ref[e, pl.ds(off, t), :])
                p_s[e, j] = pr
                dp_s[e, j] = dp
                out.append(acc[e] + jnp.sum(pr * dp, axis=1, keepdims=True))
            return tuple(out)

        zero = tuple(jnp.zeros((t, 1), F32) for _ in range(hps))
        delta = sweep1(i, lax.fori_loop(0, i, functools.partial(sweep1, diagonal=False), zero), True)

        def sweep2(j, dq_acc):
            off = pl.multiple_of(j * t, t)
            out = []
            for e in range(hps):
                pr = p_s[e, j]
                ds = pr * (dp_s[e, j] - delta[e])
                kb = k_ref[e, pl.ds(off, t), :]
                dv_ref[e, pl.ds(off, t), :] += _tn(pr, dob[e])
                dk_ref[e, pl.ds(off, t), :] += _tn(ds, qb[e]) * scale
                if bias:
                    dck_ref[e, j] -= jnp.sum(ds, axis=0, keepdims=True)
                out.append(dq_acc[e] + _nn(ds, kb))
            return tuple(out)

        dq = lax.fori_loop(0, i + 1, sweep2, tuple(jnp.zeros((t, dk), F32) for _ in range(hps)))
        for e in range(hps):
            dq_ref[e] = dq[e] * scale

    full = lambda d: pl.BlockSpec((hps, s, d), lambda h, i: (h, 0, 0))
    blk = lambda d: pl.BlockSpec((hps, t, d), lambda h, i: (h, i, 0))
    ins = [q, k, v, do, lse] + ([c, ct] if bias else [])
    in_specs = [blk(dk), full(dk), full(dv), blk(dv), blk(1)]
    out_specs = [blk(dk), full(dk), full(dv)]
    out_shape = [_sds((HEADS, s, dk)), _sds((HEADS, s, dk)), _sds((HEADS, s, dv))]
    if bias:
        in_specs += [pl.BlockSpec((t, LANES), lambda h, i: (i, 0)), pl.BlockSpec((HEADS, s), lambda h, i: (0, 0))]
        out_specs.append(pl.BlockSpec((hps, n, 1, t), lambda h, i: (h, 0, 0, 0)))
        out_shape.append(_sds((HEADS, n, 1, t)))
    res = _call_with_ride(body, name, (HEADS // hps, n), ins, in_specs, out_shape, out_specs, ("parallel", "arbitrary"), 56,
                          ride, scratch=[pltpu.VMEM((hps, n, t, t), F32), pltpu.VMEM((hps, n, t, t), F32)])
    return (*res[:3], res[3].reshape(HEADS, 1, s)) if bias else tuple(res)


def _s5_disc(lr, li, ldt, br, bi):
    dt = jnp.exp(ldt)
    mag = jnp.exp(lr * dt)
    a_re = mag * jnp.cos(li * dt)
    a_im = mag * jnp.sin(li * dt)
    den = lr * lr + li * li
    f_re = ((a_re - 1.0) * lr + a_im * li) / den
    f_im = (a_im * lr - (a_re - 1.0) * li) / den
    return a_re, a_im, f_re * br - f_im * bi, f_re * bi + f_im * br


def _s5_params_fwd(lr16, li16, ldt16, br2, bi2, name):
    def body(a, b, c, d, e, o0, o1, o2, o3):
        r = _s5_disc(a[...], b[...], c[...], d[...], e[...])
        o0[...], o1[...], o2[...], o3[...] = r

    return pl.pallas_call(body, name=name, out_shape=[_sds((512, 64))] * 4)(lr16, li16, ldt16, br2, bi2)


def _s5_params_bwd(lr16, li16, ldt16, br2, bi2, da_re16, da_im16, dbb_re, dbb_im, name):
    def body(a, b, c, d, e, g0, g1, g2, g3, o_lr, o_li, o_dt, o_br, o_bi):
        _, vjp = jax.vjp(_s5_disc, a[...], b[...], c[...], d[...], e[...])
        dlr, dli, dldt, dbr, dbi = vjp((g0[...], g1[...], g2[...], g3[...]))
        grp = lambda z: z.reshape(S5_GROUPS, S5_GROUP, S5_STATE).sum(axis=1)
        o_lr[...] = grp(dlr)
        o_li[...] = grp(dli)
        o_dt[...] = jnp.sum(grp(dldt), axis=1, keepdims=True)
        o_br[...] = dbr
        o_bi[...] = dbi

    return pl.pallas_call(
        body, name=name, out_shape=[_sds((32, 64)), _sds((32, 64)), _sds((32, 1)), _sds((512, 64)), _sds((512, 64))],
    )(lr16, li16, ldt16, br2, bi2, da_re16, da_im16, dbb_re, dbb_im)


def _cmul(ar, ai, br, bi):
    return ar * br - ai * bi, ar * bi + ai * br


S5_SUPER = 4


def _scan_loop(bre_r, bim_r, ar, ai, ore_r, oim_r, reverse, xre_r=None, xim_r=None):
    s, lw = bre_r.shape
    nt = s // SUBLANES
    with_da = xre_r is not None
    shp = (SUBLANES, lw)
    row = lax.broadcasted_iota(jnp.int32, shp, 0)
    pows = [(ar, ai)]
    for _ in range(SUBLANES - 1):
        pows.append(_cmul(pows[-1][0], pows[-1][1], ar, ai))
    cm_r, cm_i = jnp.zeros(shp, F32), jnp.zeros(shp, F32)
    for r in range(SUBLANES):
        e = (SUBLANES - 1 - r) if reverse else r
        cm_r = jnp.where(row == r, jnp.broadcast_to(pows[e][0], shp), cm_r)
        cm_i = jnp.where(row == r, jnp.broadcast_to(pows[e][1], shp), cm_i)
    steps = [(1, pows[0]), (2, pows[1]), (4, pows[3])]

    def tile(it, carry):
        if with_da:
            c_r, c_i, acc_r, acc_i = carry
        else:
            c_r, c_i = carry
        r = (nt - 1 - it) if reverse else it
        off = pl.multiple_of(r * SUBLANES, SUBLANES)
        xr, xi = bre_r[pl.ds(off, SUBLANES), :], bim_r[pl.ds(off, SUBLANES), :]
        for sh, (pr, pi) in steps:
            if reverse:
                keep = row < SUBLANES - sh
                sr = jnp.where(keep, pltpu.roll(xr, SUBLANES - sh, 0), 0.0)
                si = jnp.where(keep, pltpu.roll(xi, SUBLANES - sh, 0), 0.0)
            else:
                keep = row >= sh
                sr = jnp.where(keep, pltpu.roll(xr, sh, 0), 0.0)
                si = jnp.where(keep, pltpu.roll(xi, sh, 0), 0.0)
            mr, mi = _cmul(pr, pi, sr, si)
            xr, xi = xr + mr, xi + mi
        mr, mi = _cmul(cm_r, cm_i, c_r, c_i)
        xr, xi = xr + mr, xi + mi
        ore_r[pl.ds(off, SUBLANES), :] = xr
        oim_r[pl.ds(off, SUBLANES), :] = xi
        edge = 0 if reverse else SUBLANES - 1
        c_r, c_i = xr[edge:edge + 1, :], xi[edge:edge + 1, :]
        if not with_da:
            return c_r, c_i
        fr, fi = xre_r[pl.ds(off, SUBLANES), :], xim_r[pl.ds(off, SUBLANES), :]
        poff = pl.multiple_of(jnp.maximum(r - 1, 0) * SUBLANES, SUBLANES)
        live = (r > 0).astype(F32)
        pr_last = xre_r[pl.ds(poff, SUBLANES), :][SUBLANES - 1:SUBLANES, :] * live
        pi_last = xim_r[pl.ds(poff, SUBLANES), :][SUBLANES - 1:SUBLANES, :] * live
        sr = jnp.where(row >= 1, pltpu.roll(fr, 1, 0), jnp.broadcast_to(pr_last, shp))
        si = jnp.where(row >= 1, pltpu.roll(fi, 1, 0), jnp.broadcast_to(pi_last, shp))
        return c_r, c_i, acc_r + xr * sr + xi * si, acc_i + xi * sr - xr * si

    z1 = jnp.zeros((1, lw), F32)
    if not with_da:
        lax.fori_loop(0, nt, tile, (z1, z1))
        return None
    _, _, acc_r, acc_i = lax.fori_loop(0, nt, tile, (z1, z1, jnp.zeros(shp, F32), jnp.zeros(shp, F32)))
    return jnp.sum(acc_r, axis=0, keepdims=True), jnp.sum(acc_i, axis=0, keepdims=True)


S5_ROWS = 512


def _group_compact(p):
    grp = lax.broadcasted_iota(jnp.int32, (LANES, S5_STATE), 0) // S5_GROUP
    out = jnp.zeros((LANES, S5_STATE), F32)
    for j in range(LANES // S5_GROUP):
        out = jnp.where(grp == j, p[:, S5_STATE * j:S5_STATE * (j + 1)], out)
    return out


def _s5_core_fwd(u, wb_re, wb_im, wc_re, wc_im, a_re, a_im, name, ride=None):
    s = u.shape[0]
    lw = S5_LANES // S5_SUPER
    rows = _pick(s, (S5_ROWS, 256, 128))

    def body(u_r, wbr, wbi, wcr, wci, are_r, aim_r, xre_r, xim_r, y_r, bre_s, bim_s):
        for r0 in range(0, s, rows):
            ub = u_r[r0:r0 + rows, :]
            bre_s[r0:r0 + rows, :] = _nn(ub, wbr[0])
            bim_s[r0:r0 + rows, :] = _nn(ub, wbi[0])
        _scan_loop(bre_s, bim_s, are_r[...], aim_r[...], xre_r, xim_r, False)
        for r0 in range(0, s, rows):
            y_r[r0:r0 + rows, :] = _nn(xre_r[r0:r0 + rows, :], wcr[0]) + _nn(xim_r[r0:r0 + rows, :], wci[0])

    nar = pl.BlockSpec((s, LANES), lambda k: (0, k))
    wide = pl.BlockSpec((s, lw), lambda k: (0, k))
    one = pl.BlockSpec((1, lw), lambda k: (0, k))
    wb = pl.BlockSpec((1, LANES, lw), lambda k: (k, 0, 0))
    wc = pl.BlockSpec((1, lw, LANES), lambda k: (k, 0, 0))
    return _call_with_ride(body, name, (S5_SUPER,), [u, wb_re, wb_im, wc_re, wc_im, a_re, a_im], [nar, wb, wb, wc, wc, one, one],
                           [_sds((s, S5_LANES)), _sds((s, S5_LANES)), _sds((s, S5_WIDTH))], [wide, wide, nar], ("parallel",), 56,
                           ride, scratch=[pltpu.VMEM((s, lw), F32), pltpu.VMEM((s, lw), F32)])


def _s5_core_bwd(dy0, u, du_a, x_re, x_im, wb_re, wb_im, wc_re, wc_im, a_re, a_im_neg, name, ride=None):
    s = u.shape[0]
    lw = S5_LANES // S5_SUPER
    rows = _pick(s, (S5_ROWS, 256, 128))

    def body(dy_r, u_r, dua_r, xre_r, xim_r, wbr, wbi, wcr, wci, are_r, aim_r, du_r, dare_r, daim_r, dbr_r, dbi_r, dcr_r, dci_r,
             dre_s, dim_s, gre_s, gim_s):
        for r0 in range(0, s, rows):
            dyb = dy_r[r0:r0 + rows, :]
            dre_s[r0:r0 + rows, :] = _nt(dyb, wcr[0])
            dim_s[r0:r0 + rows, :] = _nt(dyb, wci[0])
        dare_r[...], daim_r[...] = _scan_loop(dre_s, dim_s, are_r[...], aim_r[...], gre_s, gim_s, True, xre_r, xim_r)
        acc = [jnp.zeros((LANES, lw), F32) for _ in range(4)]
        for r0 in range(0, s, rows):
            sl = slice(r0, r0 + rows)
            gr, gi, ub, dyb = gre_s[sl, :], gim_s[sl, :], u_r[sl, :], dy_r[sl, :]
            du_r[sl, :] = dua_r[sl, :] + _nt(gr, wbr[0]) + _nt(gi, wbi[0])
            acc = [acc[0] + _tn(ub, gr), acc[1] + _tn(ub, gi), acc[2] + _tn(dyb, xre_r[sl, :]), acc[3] + _tn(dyb, xim_r[sl, :])]
        dbr_r[...], dbi_r[...], dcr_r[...], dci_r[...] = (_group_compact(a) for a in acc)

    nar = pl.BlockSpec((s, LANES), lambda k: (0, k))
    wide = pl.BlockSpec((s, lw), lambda k: (0, k))
    one = pl.BlockSpec((1, lw), lambda k: (0, k))
    wb = pl.BlockSpec((1, LANES, lw), lambda k: (k, 0, 0))
    wc = pl.BlockSpec((1, lw, LANES), lambda k: (k, 0, 0))
    blk = pl.BlockSpec((LANES, S5_STATE), lambda k: (k, 0))
    return _call_with_ride(
        body, name, (S5_SUPER,), [dy0, u, du_a, x_re, x_im, wb_re, wb_im, wc_re, wc_im, a_re, a_im_neg],
        [nar, nar, nar, wide, wide, wb, wb, wc, wc, one, one],
        [_sds((s, S5_WIDTH)), _sds((1, S5_LANES)), _sds((1, S5_LANES))] + [_sds((S5_WIDTH, S5_STATE))] * 4,
        [nar, one, one, blk, blk, blk, blk], ("parallel",), 60, ride, scratch=[pltpu.VMEM((s, lw), F32)] * 4)


def _s5_seg1(y0, u, d):
    return jax.nn.gelu(y0 + d * u)


def _s5_seg2(z, t, b):
    return z * jax.nn.sigmoid(t + b)


def _s5_post_fwd(y0, seg2, d, wglu, bglu, name):
    s = y0.shape[0]
    ts = _pick(s, (256, 128))

    def fn(ins, outs, accs, scr):
        z = _s5_seg1(ins[0][...], ins[1][...], ins[2][...])
        outs[0][...] = _s5_seg2(z, _nn(z, ins[3][...]), ins[4][...])

    return _stage(name, fn, s // ts,
                  [(y0, _rows(ts, 512)), (seg2, _rows(ts, 512)), (d, _full(d.shape)), (wglu, _full(wglu.shape)),
                   (bglu, _full(bglu.shape))], [(_sds((s, 512)), _rows(ts, 512))])[0]


def _s5_post_bwd(y0, seg2, d, wglu, bglu, dy, name):
    s = y0.shape[0]
    ts = _pick(s, (256, 128))

    def fn(ins, outs, accs, scr):
        y0_r, u_r, d_r, w_r, b_r, dy_r = ins
        z, vjp1 = jax.vjp(_s5_seg1, y0_r[...], u_r[...], d_r[...])
        t = _nn(z, w_r[...])
        _, vjp2 = jax.vjp(_s5_seg2, z, t, b_r[...])
        dz, dt, db = vjp2(dy_r[...])
        accs[0][...] += _tn(z, dt)
        accs[1][...] += db
        dy0, du, dd = vjp1(dz + _nt(dt, w_r[...]))
        accs[2][...] += dd
        outs[0][...] = dy0
        outs[1][...] = du

    return _stage(name, fn, s // ts,
                  [(y0, _rows(ts, 512)), (seg2, _rows(ts, 512)), (d, _full(d.shape)), (wglu, _full(wglu.shape)),
                   (bglu, _full(bglu.shape)), (dy, _rows(ts, 512))],
                  [(_sds((s, 512)), _rows(ts, 512)), (_sds((s, 512)), _rows(ts, 512))],
                  accs=[_sds((512, 512)), _sds((1, 512)), _sds((1, 512))])


def _gate_a(y, g):
    return y * jax.nn.silu(g)


def _gate_m(o0, o1, o2, m0, m1, m2):
    return jax.nn.sigmoid(m0) * o0 + jax.nn.sigmoid(m1) * o1 + jax.nn.sigmoid(m2) * o2


def _assemble(ybuf, o_mla, o_fox, y_s5):
    for h in range(HEADS):
        ybuf[:, 64 * h:64 * (h + 1)] = o_mla[h]
        ybuf[:, 512 + 64 * h:512 + 64 * (h + 1)] = o_fox[h]
    ybuf[:, 1024:1536] = y_s5[...]


def _gate_fwd(o_mla, o_fox, y_s5, seg3, x, wo, wout, name):
    s = x.shape[0]
    ts = _pick(s, (512, 256, 128))

    def fn(ins, outs, accs, scr):
        om, of, ys, gm_r, x_r, wo_r, wout_r = ins
        ybuf = scr[0]
        _assemble(ybuf, om, of, ys)
        a = _gate_a(ybuf[...], gm_r[:, :GATES_W])
        o = [_nn(a[:, 512 * b:512 * (b + 1)], wo_r[512 * b:512 * (b + 1), :]) for b in range(3)]
        merged = _gate_m(o[0], o[1], o[2], *[gm_r[:, GATES_W + 1024 * b:GATES_W + 1024 * (b + 1)] for b in range(3)])
        outs[0][...] = x_r[...] + _nn(merged, wout_r[...])

    return _stage(name, fn, s // ts,
                  [(o_mla, _heads(ts, 64)), (o_fox, _heads(ts, 64)), (y_s5, _rows(ts, 512)), (seg3, _rows(ts, SEG_W[3])),
                   (x, _rows(ts, D_MODEL)), (wo, _full(wo.shape)), (wout, _full(wout.shape))],
                  [(_sds((s, D_MODEL)), _rows(ts, D_MODEL))], scratch=[pltpu.VMEM((ts, 1536), F32)], vmem_mb=56)[0]


def _gate_bwd(o_mla, o_fox, y_s5, seg3, wo, wout, dout, name):
    s = dout.shape[0]
    ts = _pick(s, (256, 128))

    def fn(ins, outs, accs, scr):
        om, of, ys, gm_r, wo_r, wout_r, dout_r = ins
        do_mla, do_fox, dys, dgm_r = outs
        dwo, dwout = accs
        ybuf, dabuf = scr
        _assemble(ybuf, om, of, ys)
        a, vjp_a = jax.vjp(_gate_a, ybuf[...], gm_r[:, :GATES_W])
        o = [_nn(a[:, 512 * b:512 * (b + 1)], wo_r[512 * b:512 * (b + 1), :]) for b in range(3)]
        ms = [gm_r[:, GATES_W + 1024 * b:GATES_W + 1024 * (b + 1)] for b in range(3)]
        merged, vjp_m = jax.vjp(_gate_m, *o, *ms)
        dout_v = dout_r[...]
        dwout[...] += _tn(merged, dout_v)
        cts = vjp_m(_nt(dout_v, wout_r[...]))
        for b in range(3):
            dgm_r[:, GATES_W + 1024 * b:GATES_W + 1024 * (b + 1)] = cts[3 + b].astype(BF16)
            dwo[512 * b:512 * (b + 1), :] += _tn(a[:, 512 * b:512 * (b + 1)], cts[b])
            dabuf[:, 512 * b:512 * (b + 1)] = _nt(cts[b], wo_r[512 * b:512 * (b + 1), :])
        dy, dg = vjp_a(dabuf[...])
        dgm_r[:, :GATES_W] = dg.astype(BF16)
        dys[...] = dy[:, 1024:1536]
        for h in range(HEADS):
            do_mla[h] = dy[:, 64 * h:64 * (h + 1)]
            do_fox[h] = dy[:, 512 + 64 * h:512 + 64 * (h + 1)]

    return _stage(name, fn, s // ts,
                  [(o_mla, _heads(ts, 64)), (o_fox, _heads(ts, 64)), (y_s5, _rows(ts, 512)), (seg3, _rows(ts, SEG_W[3])),
                   (wo, _full(wo.shape)), (wout, _full(wout.shape)), (dout, _rows(ts, D_MODEL))],
                  [(_sds((HEADS, s, 64)), _heads(ts, 64)), (_sds((HEADS, s, 64)), _heads(ts, 64)), (_sds((s, 512)), _rows(ts, 512)),
                   (_sds((s, SEG_W[3]), BF16), _rows(ts, SEG_W[3]))],
                  accs=[_sds(wo.shape), _sds(wout.shape)], scratch=[pltpu.VMEM((ts, 1536), F32), pltpu.VMEM((ts, 1536), F32)],
                  vmem_mb=60, acc_in_scratch=True)


def _loss_head(y, target, name):
    s = y.shape[0]
    ts = _pick(s, (256, 128))

    def fn(ins, outs, accs, scr):
        e = ins[0][...] - ins[1][...]
        outs[0][...] = e / D_MODEL
        accs[0][...] += 0.5 * jnp.sum(jnp.sum(e * e, axis=1, keepdims=True) / D_MODEL, axis=0, keepdims=True)

    r = _stage(name, fn, s // ts, [(y, _rows(ts, D_MODEL)), (target, _rows(ts, D_MODEL))],
               [(_sds((s, D_MODEL)), _rows(ts, D_MODEL))], accs=[_sds((1, 1))])
    return r[1], r[0]


IN_RANGES = ((0, 384, 0), (384, 416, 448), (416, 1952, 512), (1952, 1960, 384), (1960, IN_WIDTH, 2048))
SHARD_W = IN_WIDTH // N_DEV


def _win_pieces(d):
    lo, hi = SHARD_W * d, SHARD_W * (d + 1)
    out = []
    for a, b, p in IN_RANGES:
        s, e = max(a, lo), min(b, hi)
        while s < e:
            pad = p + (s - a)
            k = max(i for i in range(NSEG) if SEG_OFF[i] <= pad)
            w = min(e - s, SEG_OFF[k] + SEG_W[k] - pad)
            out.append((s - lo, w, k, pad - SEG_OFF[k]))
            s += w
    return out


def _win_pad(g, name):
    _, r, _ = g.shape
    tr = 64

    def body(g_ref, o_ref):
        o_ref[...] = jnp.zeros(o_ref.shape, o_ref.dtype)
        for d in range(N_DEV):
            for dst, w, k, src in _win_pieces(d):
                o_ref[:, SEG_OFF[k] + src:SEG_OFF[k] + src + w] = g_ref[d, :, dst:dst + w]

    return pl.pallas_call(
        body, name=name, grid=(r // tr,), in_specs=[pl.BlockSpec((N_DEV, tr, SHARD_W), lambda i: (0, i, 0))],
        out_specs=pl.BlockSpec((tr, PAD_IN), lambda i: (i, 0)), out_shape=jax.ShapeDtypeStruct((r, PAD_IN), g.dtype),
        compiler_params=pltpu.CompilerParams(dimension_semantics=("parallel",)),
    )(g)


def _win_unpad(dsegs, name):
    r = dsegs[0].shape[0]
    tr = 64

    def body(*refs):
        o_ref = refs[NSEG]
        for d in range(N_DEV):
            for dst, w, k, src in _win_pieces(d):
                o_ref[d, :, dst:dst + w] = refs[k][:, src:src + w]

    return pl.pallas_call(
        body, name=name, grid=(r // tr,), in_specs=[pl.BlockSpec((tr, SEG_W[k]), lambda i: (i, 0)) for k in range(NSEG)],
        out_specs=pl.BlockSpec((N_DEV, tr, SHARD_W), lambda i: (0, i, 0)),
        out_shape=jax.ShapeDtypeStruct((N_DEV, r, SHARD_W), dsegs[0].dtype),
        compiler_params=pltpu.CompilerParams(dimension_semantics=("parallel",)),
    )(*dsegs)


def _pad_wq(w):
    w = w.reshape(MLA_Q_RANK, HEADS, MLA_QK)
    return jnp.pad(w, ((0, 0), (0, 0), (0, LANES - MLA_QK))).reshape(MLA_Q_RANK, HEADS * LANES)


def _unpad_wq(d):
    return d.reshape(MLA_Q_RANK, HEADS, LANES)[:, :, :MLA_QK].reshape(MLA_Q_RANK, HEADS * MLA_QK)


def _pad_wkv(w):
    w = w.reshape(MLA_KV_RANK, HEADS, MLA_NOPE + MLA_V)
    k = jnp.pad(w[:, :, :MLA_NOPE], ((0, 0), (0, 0), (0, LANES - MLA_NOPE))).reshape(MLA_KV_RANK, HEADS * LANES)
    return jnp.concatenate([k, w[:, :, MLA_NOPE:].reshape(MLA_KV_RANK, HEADS * MLA_V)], axis=1)


def _unpad_wkv(d):
    k = d[:, :HEADS * LANES].reshape(MLA_KV_RANK, HEADS, LANES)[:, :, :MLA_NOPE]
    v = d[:, HEADS * LANES:].reshape(MLA_KV_RANK, HEADS, MLA_V)
    return jnp.concatenate([k, v], axis=2).reshape(MLA_KV_RANK, HEADS * (MLA_NOPE + MLA_V))


def _pad_lanes(v, n=LANES):
    return jnp.pad(v, (0, n - v.shape[0])).reshape(1, n)


def _super_blocks(b):
    _, r, c = b.shape
    per = S5_GROUPS // S5_SUPER
    b = b.reshape(S5_SUPER, per, r, c)
    eye = jnp.eye(per, dtype=b.dtype)
    return (b[:, :, :, None, :] * eye[None, :, None, :, None]).reshape(S5_SUPER, per * r, per * c)


def _layer_params(l, w, small):
    p = {k: small[k][l] for k in small}
    q = {}
    q["norm_g"] = p["norm_g"].reshape(1, D_MODEL)
    q["mla_q_a_norm"] = p["mla_q_a_norm"].reshape(1, 256)
    q["mla_kv_a_norm"] = p["mla_kv_a_norm"].reshape(1, 128)
    q["mla_q_norm"] = _pad_lanes(p["mla_q_norm"])
    q["mla_k_norm"] = _pad_lanes(p["mla_k_norm"])
    q["fox_b_f"] = _pad_lanes(p["fox_b_f"])
    q["fox_q_norm"] = p["fox_q_norm"].reshape(1, 64)
    q["fox_k_norm"] = p["fox_k_norm"].reshape(1, 64)
    q["s5_d"] = p["s5_d"].reshape(1, 512)
    q["s5_b_glu"] = p["s5_b_glu"].reshape(1, 512)
    rep = lambda z: jnp.repeat(z, S5_GROUP, axis=0)
    q["lr16"], q["li16"] = rep(p["s5_lambda_re"]), rep(p["s5_lambda_im"])
    q["ldt16"] = rep(jnp.broadcast_to(p["s5_log_dt"][:, None], (S5_GROUPS, S5_STATE)))
    q["br2"] = p["s5_b_re"].transpose(0, 2, 1).reshape(512, 64)
    q["bi2"] = p["s5_b_im"].transpose(0, 2, 1).reshape(512, 64)
    q["c_re"], q["c_im"] = p["s5_c_re"], p["s5_c_im"]
    q["w_in"] = _win_pad(w["w_in"], f"l{l}_w_in_pad")
    q["wq"] = _pad_wq(w["mla_w_q_up"])
    q["wkv"] = _pad_wkv(w["mla_w_kv_up"])
    q.update(_late_weights(w))
    return q


def _late_weights(w):
    return {q: w[k] for q, k in (("wglu", "s5_w_glu"), ("wo", "w_branch_out"), ("wout", "w_out")) if k in w}


def _layer_fwd(l, x, tabs, q, rides=None):
    rides = rides or {}
    n = lambda s: f"l{l}_{s}"
    sv = {"x": x}
    h = _norm_fwd(x, q["norm_g"], n("norm_fwd"))
    sv["h"] = h
    seg = [_mm(h, q["w_in"], "nn", n(f"proj{k}"), b_cols=(SEG_OFF[k], SEG_W[k])) for k in range(NSEG)]
    sv["seg"] = seg
    mq, mk, mv = _mla_prep_fwd(seg[0], tabs, q, q["wq"], q["wkv"], n("mla_prep_fwd"))
    o_mla, lse_mla = _attn_fwd(mq, mk, mv, 1.0 / math.sqrt(MLA_QK), True, n("mla_attn_fwd"), ride=rides.get("mla_attn_fwd"))
    sv.update(mq=mq, mk=mk, mv=mv, o_mla=o_mla, lse_mla=lse_mla)
    fq, fk, fv, c, ct = _fox_prep_fwd(seg[0], seg[1], q["fox_b_f"], q["fox_q_norm"], q["fox_k_norm"], n("fox_prep_fwd"))
    o_fox, lse_fox = _attn_fwd(fq, fk, fv, 1.0 / math.sqrt(FOX_DIM), False, n("fox_attn_fwd"), c=c, ct=ct,
                               ride=rides.get("fox_attn_fwd"))
    sv.update(fq=fq, fk=fk, fv=fv, c=c, ct=ct, o_fox=o_fox, lse_fox=lse_fox)
    a_re16, a_im16, bb_re, bb_im = _s5_params_fwd(q["lr16"], q["li16"], q["ldt16"], q["br2"], q["bi2"], n("s5_params_fwd"))
    a_re = a_re16.reshape(S5_GROUPS, S5_GROUP, S5_STATE)[:, 0, :].reshape(1, S5_LANES)
    a_im = a_im16.reshape(S5_GROUPS, S5_GROUP, S5_STATE)[:, 0, :].reshape(1, S5_LANES)
    wb_re = _super_blocks(bb_re.reshape(S5_GROUPS, S5_GROUP, S5_STATE)).astype(BF16)
    wb_im = _super_blocks(bb_im.reshape(S5_GROUPS, S5_GROUP, S5_STATE)).astype(BF16)
    wc_re = _super_blocks(q["c_re"].transpose(0, 2, 1)).astype(BF16)
    wc_im = _super_blocks(-q["c_im"].transpose(0, 2, 1)).astype(BF16)
    x_re, x_im, y0 = _s5_core_fwd(seg[2], wb_re, wb_im, wc_re, wc_im, a_re, a_im, n("s5_scan_fwd"), ride=rides.get("s5_scan_fwd"))
    y_s5 = _s5_post_fwd(y0, seg[2], q["s5_d"], q["wglu"], q["s5_b_glu"], n("s5_post_fwd"))
    sv.update(a_re=a_re, a_im=a_im, wb_re=wb_re, wb_im=wb_im, wc_re=wc_re, wc_im=wc_im, x_re=x_re, x_im=x_im, y0=y0, y_s5=y_s5)
    out = _gate_fwd(o_mla, o_fox, y_s5, seg[3], x, q["wo"], q["wout"], n("gate_fwd"))
    return out, sv


def _layer_bwd(l, dout, tabs, q, sv, rides=None, g=None):
    rides = rides or {}
    n = lambda s: f"l{l}_{s}"
    seg = sv["seg"]
    g = {} if g is None else g
    (do_mla, do_fox, dy_s5, dseg3, g["wo"], g["wout"]) = _gate_bwd(
        sv["o_mla"], sv["o_fox"], sv["y_s5"], seg[3], q["wo"], q["wout"], dout, n("gate_bwd"))
    dy0, du_a, g["wglu"], g["s5_b_glu"], g["s5_d"] = _s5_post_bwd(sv["y0"], seg[2], q["s5_d"], q["wglu"], q["s5_b_glu"], dy_s5,
                                                                 n("s5_post_bwd"))
    dseg2, da_re, da_im, dbb_re, dbb_im, dc_re, dc_im = _s5_core_bwd(
        dy0, seg[2], du_a, sv["x_re"], sv["x_im"], sv["wb_re"], sv["wb_im"], sv["wc_re"], sv["wc_im"], sv["a_re"], -sv["a_im"],
        n("s5_scan_bwd"), ride=rides.get("s5_scan_bwd"))
    g["s5_c_re"] = dc_re.reshape(S5_GROUPS, S5_GROUP, S5_STATE)
    g["s5_c_im"] = -dc_im.reshape(S5_GROUPS, S5_GROUP, S5_STATE)
    first = (jnp.arange(512) % S5_GROUP == 0).astype(F32)[:, None]
    da_re16 = jnp.repeat(da_re.reshape(S5_GROUPS, S5_STATE), S5_GROUP, axis=0) * first
    da_im16 = jnp.repeat(da_im.reshape(S5_GROUPS, S5_STATE), S5_GROUP, axis=0) * first
    dlr, dli, dldt, dbr2, dbi2 = _s5_params_bwd(q["lr16"], q["li16"], q["ldt16"], q["br2"], q["bi2"], da_re16, da_im16, dbb_re,
                                               dbb_im, n("s5_params_bwd"))
    g["s5_lambda_re"], g["s5_lambda_im"], g["s5_log_dt"] = dlr, dli, dldt.reshape(S5_GROUPS)
    g["s5_b_re"] = dbr2.reshape(S5_GROUPS, S5_GROUP, S5_STATE).transpose(0, 2, 1)
    g["s5_b_im"] = dbi2.reshape(S5_GROUPS, S5_GROUP, S5_STATE).transpose(0, 2, 1)
    dfq, dfk, dfv, dck = _attn_bwd_rows(sv["fq"], sv["fk"], sv["fv"], do_fox, sv["lse_fox"], 1.0 / math.sqrt(FOX_DIM), False,
                                        n("fox_attn_bwd"), sv["c"], sv["ct"], ride=rides.get("fox_attn_bwd"))
    dseg1, dff, g["fox_q_norm"], g["fox_k_norm"], dbf = _fox_prep_bwd(seg[0], seg[1], q["fox_b_f"], q["fox_q_norm"], q["fox_k_norm"],
                                                                      dfq, dfk, dfv, dck, n("fox_prep_bwd"))
    g["fox_b_f"] = dbf[0, :HEADS]
    dmq, dmk, dmv = _attn_bwd_rows(sv["mq"], sv["mk"], sv["mv"], do_mla, sv["lse_mla"], 1.0 / math.sqrt(MLA_QK), True,
                                   n("mla_attn_bwd"), ride=rides.get("mla_attn_bwd"))
    dseg0, dqan, dkvan, dqn, dkn, g["wq"], g["wkv"] = _mla_prep_bwd(seg[0], tabs, q, q["wq"], q["wkv"], dmq, dmk, dmv, dff,
                                                                   n("mla_prep_bwd"))
    g["mla_q_a_norm"], g["mla_kv_a_norm"] = dqan, dkvan
    g["mla_q_norm"], g["mla_k_norm"] = dqn[0, :MLA_QK], dkn[0, :MLA_QK]
    dsegs = [dseg0, dseg1, dseg2, dseg3]
    g["w_in"] = [_mm(sv["h"], dsegs[k], "tn", n(f"dwin{k}")) for k in range(NSEG)]
    dh = None
    for k in range(NSEG):
        dh = _mm(dsegs[k], q["w_in"], "nt", n(f"dh{k}"), acc=dh, b_cols=(SEG_OFF[k], SEG_W[k]), ride=rides.get(f"dh{k}"))
    dx, g["norm_g"] = _norm_bwd(sv["x"], q["norm_g"], dh, dout, n("norm_bwd"))
    return dx, g


MESH = pl.DeviceIdType.MESH
ANY = pl.BlockSpec(memory_space=pl.ANY)


def _all_gather(blocks, name):
    n = len(blocks)

    def body(*refs):
        x_refs, out_refs = refs[:n], refs[n:2 * n]
        send_sems, recv_sems, local_sems = refs[2 * n:]
        x, y, c = lax.axis_index("x"), lax.axis_index("y"), lax.axis_index("c")
        me, sibling = (x, y, c), (x, y, 1 - c)
        chips = [(1 - x, y), (x, 1 - y), (1 - x, 1 - y)]

        def slot(a, px, py, pc):
            return out_refs[a].at[4 * px + 2 * py + pc]

        def copy(a, k, blk, to, src=None):
            return pltpu.make_async_remote_copy(src_ref=slot(a, *blk) if src is None else src, dst_ref=slot(a, *blk),
                                                send_sem=send_sems.at[7 * a + k], recv_sem=recv_sems.at[7 * a + k],
                                                device_id=to, device_id_type=MESH)

        mine = [pltpu.make_async_copy(x_refs[a], slot(a, *me), local_sems.at[a]) for a in range(n)]
        for cp in mine:
            cp.start()
        first = []
        for j, chip in enumerate(chips):
            first += [copy(a, 1 + j, me, (*chip, c), src=x_refs[a]) for a in range(n)]
        first += [copy(a, 0, me, sibling, src=x_refs[a]) for a in range(n)]
        for cp in first:
            cp.start()
        passed = []
        for j, chip in enumerate(chips):
            for a in range(n):
                copy(a, 1 + j, (*chip, c), me).wait_recv()
                passed.append(copy(a, 4 + j, (*chip, c), sibling))
                passed[-1].start()
        for a in range(n):
            copy(a, 0, sibling, me).wait_recv()
        for j, chip in enumerate(chips):
            for a in range(n):
                copy(a, 4 + j, (*chip, 1 - c), me).wait_recv()
        for cp in first + passed:
            cp.wait_send()
        for cp in mine:
            cp.wait()

    return pl.pallas_call(
        body, name=name, out_shape=[jax.ShapeDtypeStruct((N_DEV,) + b.shape, b.dtype) for b in blocks],
        in_specs=[ANY] * n, out_specs=[ANY] * n,
        scratch_shapes=[pltpu.SemaphoreType.DMA((7 * n,)), pltpu.SemaphoreType.DMA((7 * n,)), pltpu.SemaphoreType.DMA((n,))],
    )(*blocks)


def _place():
    x, y, c = lax.axis_index("x"), lax.axis_index("y"), lax.axis_index("c")
    return x, y, c, [(1 - x, y), (x, 1 - y), (1 - x, 1 - y)]


def _remote(src, dst, send, recv, k, to):
    return pltpu.make_async_remote_copy(src_ref=src, dst_ref=dst, send_sem=send.at[k], recv_sem=recv.at[k], device_id=to,
                                        device_id_type=MESH)


def _plan_gather_ici(blocks):
    n = len(blocks)

    def copies(in_refs, out_refs, send, recv, local):
        x, y, c, chips = _place()
        mine = 4 * x + 2 * y + c
        loc = [pltpu.make_async_copy(in_refs[a], out_refs[a].at[mine], local.at[a]) for a in range(n)]
        rem = [_remote(in_refs[a], out_refs[a].at[mine], send, recv, 3 * a + j, (px, py, c))
               for j, (px, py) in enumerate(chips) for a in range(n)]
        return rem, loc

    return _Plan(blocks, [jax.ShapeDtypeStruct((N_DEV,) + b.shape, b.dtype) for b in blocks], 3 * n, n, copies)


def _plan_gather_d2d(gathered):
    n = len(gathered)

    def copies(in_refs, out_refs, send, recv, local):
        x, y, c, _ = _place()
        rem = [_remote(in_refs[a].at[2 * j + c], out_refs[a].at[2 * j + c], send, recv, 4 * a + j, (x, y, 1 - c))
               for a in range(n) for j in range(4)]
        return rem, []

    return _Plan(gathered, [jax.ShapeDtypeStruct(g.shape, g.dtype) for g in gathered], 4 * n, 0, copies,
                 aliases={a: a for a in range(n)})


def _plan_reduce_sibling(parts):
    n = len(parts)

    def copies(in_refs, out_refs, send, recv, local):
        x, y, c, _ = _place()
        rem = [_remote(in_refs[a].at[2 * j + (1 - c)], out_refs[a].at[j], send, recv, 4 * a + j, (x, y, 1 - c))
               for a in range(n) for j in range(4)]
        return rem, []

    return _Plan(parts, [jax.ShapeDtypeStruct((4,) + p.shape[1:], p.dtype) for p in parts], 4 * n, 0, copies)


def _plan_reduce_chips(sums):
    n = len(sums)

    def copies(in_refs, out_refs, send, recv, local):
        x, y, c, chips = _place()
        mine = 2 * x + y
        loc = [pltpu.make_async_copy(in_refs[a].at[mine], out_refs[a].at[mine], local.at[a]) for a in range(n)]
        rem = [_remote(in_refs[a].at[2 * px + py], out_refs[a].at[mine], send, recv, 3 * a + k, (px, py, c))
               for k, (px, py) in enumerate(chips) for a in range(n)]
        return rem, loc

    return _Plan(sums, [jax.ShapeDtypeStruct(p.shape, p.dtype) for p in sums], 3 * n, n, copies)


def _add_sibling(parts, got, name):
    _, r, cc = parts.shape
    tr = _pick(r, (512, 256, 128, 64, 32, 16))
    c = lax.axis_index("c")

    def body(c_ref, p_ref, g_ref, o_ref):
        o_ref[...] = (p_ref[...] + g_ref[...]).astype(BF16)

    return pl.pallas_call(
        body, name=name, out_shape=jax.ShapeDtypeStruct((4, r, cc), BF16),
        grid_spec=pltpu.PrefetchScalarGridSpec(
            num_scalar_prefetch=1, grid=(4, r // tr),
            in_specs=[pl.BlockSpec((1, tr, cc), lambda j, i, cr: (2 * j + cr[0], i, 0)),
                      pl.BlockSpec((1, tr, cc), lambda j, i, cr: (j, i, 0))],
            out_specs=pl.BlockSpec((1, tr, cc), lambda j, i, cr: (j, i, 0))),
        compiler_params=_vmem(48),
    )(c.reshape(1).astype(jnp.int32), parts, got)


def _sum_leading(parts, name):
    k, r, cc = parts.shape
    tr = _pick(r, (512, 256, 128, 64, 32, 16, 8))

    def body(p_ref, o_ref):
        acc = p_ref[0]
        for j in range(1, k):
            acc = acc + p_ref[j]
        o_ref[...] = acc

    return pl.pallas_call(
        body, name=name, out_shape=jax.ShapeDtypeStruct((r, cc), F32), grid=(r // tr,),
        in_specs=[pl.BlockSpec((k, tr, cc), lambda i: (0, i, 0))], out_specs=pl.BlockSpec((tr, cc), lambda i: (i, 0)),
    )(parts)


def _adamw_math(w, g, m, v):
    nm = ADAM_B1 * m + (1.0 - ADAM_B1) * g
    nv = ADAM_B2 * v + (1.0 - ADAM_B2) * jnp.square(g)
    m_hat = nm / (1.0 - ADAM_B1 ** ADAM_STEP)
    v_hat = nv / (1.0 - ADAM_B2 ** ADAM_STEP)
    return -ADAM_LR * (m_hat / (jnp.sqrt(v_hat) + ADAM_EPS) + ADAM_WD * w), nm, nv


def _adamw_sum(w, contribs, m, v, name, ride=None):
    nl = len(contribs)
    k, r, cc = contribs[0].shape
    tr = _pick(r, (256, 128, 64, 32, 16))
    nb = r // tr

    def body(w_ref, *rest):
        c_refs = rest[:nl]
        m_ref, v_ref, g_ref, d_ref, nm_ref, nv_ref = rest[nl:]
        for li in range(nl):
            @pl.when(pl.program_id(0) == li)
            def _(c_ref=c_refs[li]):
                g = c_ref[0].astype(F32)
                for j in range(1, k):
                    g = g + c_ref[j].astype(F32)
                g_ref[...] = g
                d_ref[...], nm_ref[...], nv_ref[...] = _adamw_math(w_ref[...], g, m_ref[...], v_ref[...])

    spec = pl.BlockSpec((tr, cc), lambda l, i: (l * nb + i, 0))
    cspec = pl.BlockSpec((k, tr, cc), lambda l, i: (0, i, 0))
    return _call_with_ride(body, name, (nl, nb), [w, *contribs, m, v], [spec] + [cspec] * nl + [spec, spec],
                           [jax.ShapeDtypeStruct(w.shape, F32)] * 4, [spec] * 4, ("parallel", "parallel"), 48, ride)


def _adamw_many(ws, gs, ms, vs, name):
    n = len(ws)

    def body(*refs):
        w_r, g_r, m_r, v_r = refs[:n], refs[n:2 * n], refs[2 * n:3 * n], refs[3 * n:4 * n]
        d_r, nm_r, nv_r = refs[4 * n:5 * n], refs[5 * n:6 * n], refs[6 * n:7 * n]
        for a in range(n):
            d_r[a][...], nm_r[a][...], nv_r[a][...] = _adamw_math(w_r[a][...], g_r[a][...], m_r[a][...], v_r[a][...])

    shapes = [jax.ShapeDtypeStruct(w.shape, F32) for w in ws]
    res = pl.pallas_call(body, name=name, out_shape=shapes * 3,
                         compiler_params=pltpu.CompilerParams(vmem_limit_bytes=56 * 1024 * 1024))(*ws, *gs, *ms, *vs)
    return res[:n], res[n:2 * n], res[2 * n:]


def _pack_rows(flat, lanes, row_mult):
    n = flat.shape[-1]
    rows = -(-n // lanes)
    rows = -(-rows // row_mult) * row_mult
    pad = rows * lanes - n
    if pad:
        flat = jnp.pad(flat, [(0, 0)] * (flat.ndim - 1) + [(0, pad)])
    return flat.reshape(flat.shape[:-1] + (rows, lanes))


def _rope_tables(positions):
    inv = 1.0 / (ROPE_THETA ** (jnp.arange(0, MLA_ROPE, 2, dtype=F32) / MLA_ROPE))
    ang = positions.astype(F32)[:, None] * inv
    cos, sin = jnp.cos(ang), jnp.sin(ang)
    s = positions.shape[0]
    z = lambda n: jnp.zeros((s, n), F32)
    c = jnp.concatenate([jnp.ones((s, 64), F32), cos, cos, z(32)], axis=1)
    sa = jnp.concatenate([z(64), -sin, z(48)], axis=1)
    sb = jnp.concatenate([z(80), sin, z(32)], axis=1)
    return c, sa, sb


def _full_weights(gathered, names=SHARDED):
    full = {}
    for k, g in zip(names, gathered):
        _, r, c = g.shape
        if k == "w_in":
            full[k] = g
        else:
            full[k] = g.transpose(1, 0, 2).reshape(r, N_DEV * c) if k in COL_SHARDED else g.reshape(N_DEV * r, c)
    return full


EARLY = ("s5_w_glu", "w_branch_out", "w_out")
LATE = ("w_in", "mla_w_q_up", "mla_w_kv_up")


def _owner_major(g, names, tag):
    parts = []
    for k in names:
        if k == "w_in":
            parts.append(_win_unpad(g["w_in"], f"{tag}_w_in_unpad"))
            continue
        big = {"mla_w_q_up": lambda: _unpad_wq(g["wq"]), "mla_w_kv_up": lambda: _unpad_wkv(g["wkv"]), "s5_w_glu": lambda: g["wglu"],
               "w_branch_out": lambda: g["wo"], "w_out": lambda: g["wout"]}[k]()
        r, c = big.shape
        if k in COL_SHARDED:
            parts.append(big.reshape(r, N_DEV, c // N_DEV).transpose(1, 0, 2))
        else:
            parts.append(big.reshape(N_DEV, r // N_DEV, c))
    return parts


def _device_step(x, positions, target, shards, small):
    tabs = _rope_tables(positions)
    box = {}
    q0 = _layer_params(0, _full_weights(_all_gather(shards[0][:3], "gather_weights_l0"), LATE), small)

    def arrived(o):
        box.update(w1_b=o[:5])
        q0.update(_late_weights(_full_weights(o[5:], EARLY)))

    rides = {
        "mla_attn_fwd": _Ride(lambda: _plan_gather_ici(shards[1][:1]), lambda o: box.update(ici_a=o)),
        "fox_attn_fwd": _Ride(lambda: _join([_plan_gather_ici(shards[1][1:]), _plan_gather_d2d(box["ici_a"]),
                                             _plan_gather_ici(shards[0][3:])]),
                              lambda o: box.update(ici_b=o[:5], w1_a=o[5:6], ici_0=o[6:])),
        "s5_scan_fwd": _Ride(lambda: _join([_plan_gather_d2d(box["ici_b"]), _plan_gather_d2d(box["ici_0"])]), arrived),
    }
    h, sv0 = _layer_fwd(0, x, tabs, q0, rides)
    q1 = _layer_params(1, _full_weights(list(box["w1_a"]) + list(box["w1_b"])), small)
    h, sv1 = _layer_fwd(1, h, tabs, q1)
    loss, d = _loss_head(h, target, "loss_head")
    d, g1 = _layer_bwd(1, d, tabs, q1, sv1)
    parts1 = _owner_major(g1, SHARDED, "l1")

    def chips_plan(names, parts, got, tag):
        return _plan_reduce_chips([_add_sibling(p, g, f"reduce_add_{tag}_{k}") for k, p, g in zip(names, parts, got)])

    g0 = {}
    rides = {
        "s5_scan_bwd": _Ride(lambda: _join([_plan_reduce_sibling(parts1),
                                            _plan_reduce_sibling(box.setdefault("early0", _owner_major(g0, EARLY, "l0")))]),
                             lambda o: box.update(got1=o[:6], got0e=o[6:])),
        "fox_attn_bwd": _Ride(lambda: chips_plan(SHARDED, parts1, box["got1"], "l1"), lambda o: box.update(contribs1=o)),
        "mla_attn_bwd": _Ride(lambda: chips_plan(EARLY, box["early0"], box["got0e"], "l0"), lambda o: box.update(contribs0e=o)),
        "dh1": _Ride(lambda: _plan_reduce_sibling(box.setdefault("late0", _owner_major(g0, LATE, "l0"))),
                     lambda o: box.update(got0l=o)),
        "dh3": _Ride(lambda: chips_plan(LATE, box["late0"], box["got0l"], "l0"), lambda o: box.update(contribs0l=o)),
    }
    d, _ = _layer_bwd(0, d, tabs, q0, sv0, rides, g0)
    contribs0 = list(box["contribs0l"]) + list(box["contribs0e"])
    return loss[0, 0], d, [g0, g1], [contribs0, box["contribs1"]]


def kernel(x, positions, norm_g, w_in, mla_q_a_norm, mla_w_q_up, mla_kv_a_norm, mla_w_kv_up, mla_q_norm, mla_k_norm, fox_b_f, fox_q_norm, fox_k_norm, s5_lambda_re, s5_lambda_im, s5_log_dt, s5_b_re, s5_b_im, s5_c_re, s5_c_im, s5_d, s5_w_glu, s5_b_glu, w_branch_out, w_out, loss_target, m_norm_g, m_w_in, m_mla_q_a_norm, m_mla_w_q_up, m_mla_kv_a_norm, m_mla_w_kv_up, m_mla_q_norm, m_mla_k_norm, m_fox_b_f, m_fox_q_norm, m_fox_k_norm, m_s5_lambda_re, m_s5_lambda_im, m_s5_log_dt, m_s5_b_re, m_s5_b_im, m_s5_c_re, m_s5_c_im, m_s5_d, m_s5_w_glu, m_s5_b_glu, m_w_branch_out, m_w_out, v_norm_g, v_w_in, v_mla_q_a_norm, v_mla_w_q_up, v_mla_kv_a_norm, v_mla_w_kv_up, v_mla_q_norm, v_mla_k_norm, v_fox_b_f, v_fox_q_norm, v_fox_k_norm, v_s5_lambda_re, v_s5_lambda_im, v_s5_log_dt, v_s5_b_re, v_s5_b_im, v_s5_c_re, v_s5_c_im, v_s5_d, v_s5_w_glu, v_s5_b_glu, v_w_branch_out, v_w_out):
    env = dict(locals())
    wts = {k: env[k] for k in WEIGHTS}
    mom = {k: env["m_" + k] for k in WEIGHTS}
    var = {k: env["v_" + k] for k in WEIGHTS}

    shards = [[wts[k][l].astype(BF16) for k in SHARDED] for l in range(DEPTH)]
    small = {k: wts[k] for k in SMALL}
    loss, dx, grads, contribs = _device_step(x[0], positions[0], loss_target[0], shards, small)
    loss = lax.psum(loss, ("x", "y", "c"))

    two_d = {k: (wts[k].shape[0] * wts[k].shape[1], wts[k].shape[2]) for k in SHARDED}
    sm = {k: jnp.stack([g[k] for g in grads]).reshape(wts[k].shape) for k in SMALL}
    small_block = _pack_rows(jnp.concatenate([sm[k].reshape(-1) for k in SMALL]), LANES, 256)
    box = {}
    rides = {SHARDED[0]: _Ride(lambda: _plan_gather_ici([small_block]), lambda o: box.update(ici=o)),
             SHARDED[1]: _Ride(lambda: _plan_gather_d2d(box["ici"]), lambda o: box.update(all=o))}
    grad_out, delta_out, m_out, v_out = {}, {}, {}, {}
    for i, k in enumerate(SHARDED):
        shp = wts[k].shape
        res = _adamw_sum(wts[k].reshape(two_d[k]), [contribs[l][i] for l in range(DEPTH)], mom[k].reshape(two_d[k]),
                         var[k].reshape(two_d[k]), f"adamw_{k}", ride=rides.get(k))
        grad_out[k], delta_out[k], m_out[k], v_out[k] = (z.reshape(shp) for z in res)
    g_small = _sum_leading(box["all"][0], "sum_small_grads").reshape(-1)
    off = 0
    for k in SMALL:
        cnt = int(np.prod(wts[k].shape))
        grad_out[k] = g_small[off:off + cnt].reshape(wts[k].shape)
        off += cnt
    flat2 = lambda a: a.reshape(-1, a.shape[-1])
    d_s, m_s, v_s = _adamw_many([flat2(wts[k]) for k in SMALL], [flat2(grad_out[k]) for k in SMALL],
                                [flat2(mom[k]) for k in SMALL], [flat2(var[k]) for k in SMALL], "adamw_small")
    for i, k in enumerate(SMALL):
        delta_out[k], m_out[k], v_out[k] = (z[i].reshape(wts[k].shape) for z in (d_s, m_s, v_s))

    return (loss, dx[None], *[grad_out[k] for k in WEIGHTS], *[delta_out[k] for k in WEIGHTS],
            *[m_out[k] for k in WEIGHTS], *[v_out[k] for k in WEIGHTS])
```
